```python
import math
import jax, jax.numpy as jnp
from jax import lax
import numpy as np

D_MODEL = 2048
BATCH = 16
SEQ = 256
DEPTH = 1
DEC_BATCH = 2
DEC_SEQ = 2048
PAST_LEN = 512

GRID_W = 64
SSM_WIDTH = D_MODEL // 2
SSM_GROUP = 16
SSM_GROUPS = SSM_WIDTH // SSM_GROUP
SSM_STATE = 64
N_HEADS = 16
HEAD_DIM = 64
ATT_WIDTH = N_HEADS * HEAD_DIM
WIN_ROWS = 8
WIN_COLS = 16
KEY_COLS = 2 * WIN_COLS
N_COL_BLOCKS = GRID_W // WIN_COLS
Q_BLOCK = 128
IN_WIDTH = 2 * SSM_WIDTH + 4 * ATT_WIDTH + 2 * D_MODEL
EPS = 1e-6
NEG_INF = -1e30

kernel_name = 'hybrid_s5_natten_prefix_dit_step'


def _rms(x, w):
    xf = x.astype(jnp.float32)
    y = xf * lax.rsqrt(jnp.mean(xf * xf, axis=-1, keepdims=True) + EPS)
    return (y * w.astype(jnp.float32)).astype(x.dtype)


def _modulation(cond, w_ada, b_ada):
    mod = jax.nn.silu(cond) @ w_ada + b_ada
    shift, scale, gate = jnp.split(mod[:, None, :], 3, axis=-1)
    return shift, scale, gate


def _front(x, shift, scale, norm_w, w_in, q_norm_w, k_norm_w):
    h = _rms(x, norm_w) * (1 + scale) + shift
    proj = h @ w_in
    cuts = [SSM_WIDTH, 2 * SSM_WIDTH, 2 * SSM_WIDTH + ATT_WIDTH, 2 * SSM_WIDTH + 2 * ATT_WIDTH,
            2 * SSM_WIDTH + 3 * ATT_WIDTH, 2 * SSM_WIDTH + 4 * ATT_WIDTH,
            2 * SSM_WIDTH + 4 * ATT_WIDTH + D_MODEL]
    u, z_s, q, k, v, z_a, g_s, g_a = jnp.split(proj, cuts, axis=-1)
    n, L, _ = x.shape
    q = _rms(q.reshape(n, L, N_HEADS, HEAD_DIM), q_norm_w)
    k = _rms(k.reshape(n, L, N_HEADS, HEAD_DIM), k_norm_w)
    v = v.reshape(n, L, N_HEADS, HEAD_DIM)
    return u, z_s, q, k, v, z_a, g_s, g_a


def _s5_discretize(a_re, a_im, log_dt, b_re, b_im):
    f32 = jnp.float32
    lam = lax.complex(jnp.minimum(a_re.astype(f32), -1e-4), a_im.astype(f32))
    dt = jnp.exp(log_dt.astype(f32))[..., None]
    lam_bar = jnp.exp(lam * dt)
    b = lax.complex(b_re.astype(f32), b_im.astype(f32))
    b_bar = ((lam_bar - 1) / lam)[..., None] * b
    return lam_bar, b_bar


def _lin_scan(bu, lam_bar, h0):
    bu = bu.at[:, 0].add(lam_bar * h0)
    a = jnp.broadcast_to(lam_bar, bu.shape)

    def combine(e1, e2):
        a1, b1 = e1
        a2, b2 = e2
        return a1 * a2, a2 * b1 + b2

    _, h = lax.associative_scan(combine, (a, bu), axis=1)
    return h


def _s5_bidir(u, lam_bar, b_bar, c_re, c_im, d, h0_f, h0_b):
    n, L, _ = u.shape
    f32 = jnp.float32
    uf = u.astype(f32)
    ug = uf.reshape(n, L, SSM_GROUPS, SSM_GROUP).astype(jnp.complex64)
    c = lax.complex(c_re.astype(f32), c_im.astype(f32))
    bu_f = jnp.einsum('blgi,gpi->blgp', ug, b_bar[0])
    bu_b = jnp.einsum('blgi,gpi->blgp', ug, b_bar[1])
    h_f = _lin_scan(bu_f, lam_bar[0], h0_f)
    h_b = jnp.flip(_lin_scan(jnp.flip(bu_b, axis=1), lam_bar[1], h0_b), axis=1)
    y = (jnp.einsum('blgp,gip->blgi', h_f, c[0]) + jnp.einsum('blgp,gip->blgi', h_b, c[1])).real
    y = y.reshape(n, L, SSM_WIDTH) + d.astype(f32) * uf
    return y, h_f[:, -1], h_b[:, 0]


def _context_attention(q, k, v):
    n, L, H, Dh = q.shape
    nblk = L // Q_BLOCK
    qb = q.reshape(n, nblk, Q_BLOCK, H, Dh).transpose(1, 0, 2, 3, 4)

    def block(qi):
        s = jnp.einsum('bqhd,bkhd->bhqk', qi, k).astype(jnp.float32) * (HEAD_DIM ** -0.5)
        p = jax.nn.softmax(s, axis=-1).astype(v.dtype)
        return jnp.einsum('bhqk,bkhd->bqhd', p, v)

    o = lax.map(block, qb)
    return o.transpose(1, 0, 2, 3, 4).reshape(n, L, H * Dh)


def _neighbourhood_attention(q, k, v, k_ctx, v_ctx, rpb):
    n, L, H, Dh = q.shape
    rows = L // GRID_W
    kh = min(WIN_ROWS, rows)
    qg = q.reshape(n, rows, GRID_W, H, Dh)
    kg = k.reshape(n, rows, GRID_W, H, Dh)
    vg = v.reshape(n, rows, GRID_W, H, Dh)
    jb = np.arange(N_COL_BLOCKS)
    col_start = np.clip(jb * WIN_COLS - WIN_COLS // 2, 0, GRID_W - KEY_COLS)
    col_idx = (col_start[:, None] + np.arange(KEY_COLS)[None, :]).astype(np.int32)
    q_col = jb[:, None] * WIN_COLS + np.arange(WIN_COLS)[None, :]
    cs = np.clip(q_col - WIN_COLS // 2, 0, GRID_W - WIN_COLS)
    kc = col_idx[:, None, :]
    mask = (kc >= cs[..., None]) & (kc < cs[..., None] + WIN_COLS)
    dc_idx = np.clip(kc - q_col[..., None] + WIN_COLS - 1, 0, 2 * WIN_COLS - 2).astype(np.int32)
    rpb32 = rpb.astype(jnp.float32)
    scale = HEAD_DIM ** -0.5

    def row_fn(r):
        rs = jnp.clip(r - kh // 2, 0, rows - kh)
        k_rows = lax.dynamic_slice_in_dim(kg, rs, kh, axis=1)
        v_rows = lax.dynamic_slice_in_dim(vg, rs, kh, axis=1)
        k_blk = k_rows[:, :, col_idx]
        v_blk = v_rows[:, :, col_idx]
        q_r = lax.dynamic_index_in_dim(qg, r, axis=1, keepdims=False)
        q_r = q_r.reshape(n, N_COL_BLOCKS, WIN_COLS, H, Dh)
        s_win = jnp.einsum('bjqhd,bkjmhd->bjqhkm', q_r, k_blk).astype(jnp.float32) * scale
        dr_idx = rs + jnp.arange(kh) - r + WIN_ROWS - 1
        bias = rpb32[:, dr_idx[:, None, None, None], dc_idx[None]]
        bias = bias.transpose(2, 3, 0, 1, 4)
        s_win = jnp.where(mask[:, :, None, None, :], s_win + bias, NEG_INF)
        s_ctx = jnp.einsum('bjqhd,bchd->bjqhc', q_r, k_ctx).astype(jnp.float32) * scale
        nw = kh * KEY_COLS
        s = jnp.concatenate([s_win.reshape(n, N_COL_BLOCKS, WIN_COLS, H, nw), s_ctx], axis=-1)
        p = jax.nn.softmax(s, axis=-1)
        p_win = p[..., :nw].reshape(n, N_COL_BLOCKS, WIN_COLS, H, kh, KEY_COLS).astype(v.dtype)
        p_ctx = p[..., nw:].astype(v_ctx.dtype)
        o = (jnp.einsum('bjqhkm,bkjmhd->bjqhd', p_win, v_blk)
             + jnp.einsum('bjqhc,bchd->bjqhd', p_ctx, v_ctx))
        return o.reshape(n, GRID_W, H * Dh)

    out = lax.map(row_fn, jnp.arange(rows))
    return out.transpose(1, 0, 2, 3).reshape(n, L, H * Dh)


def _back(x, y_ssm, z_s, attn, z_a, g_s, g_a, gate, w_glu, b_glu, w_ssm_out, w_att_out, w_o):
    ys = jax.nn.gelu(y_ssm.astype(x.dtype))
    ys = ys * jax.nn.sigmoid(ys @ w_glu + b_glu)
    ys = ys * jax.nn.silu(z_s)
    p_s = ys @ w_ssm_out
    p_a = (attn * jax.nn.silu(z_a)) @ w_att_out
    merged = jax.nn.sigmoid(g_s) * p_s + jax.nn.sigmoid(g_a) * p_a
    return x + gate * (merged @ w_o)


def setup_inputs(seed: int = 0) -> dict:
    key = jax.random.key(seed)
    ks = jax.random.split(key, 32)
    f32 = jnp.float32

    def nrm(k, shape, s):
        return jax.random.normal(k, shape, f32) * s

    L = DEPTH
    G, P, GC = SSM_GROUPS, SSM_STATE, SSM_GROUP
    n_idx = jnp.arange(P, dtype=f32)
    return {
        'x_prompt': nrm(ks[0], (BATCH, SEQ, D_MODEL), 1.0),
        'x_sample': nrm(ks[1], (DEC_BATCH, DEC_SEQ, D_MODEL), 1.0),
        'cache_k': nrm(ks[2], (DEC_BATCH, L, PAST_LEN, N_HEADS, HEAD_DIM), 1.0),
        'cache_v': nrm(ks[3], (DEC_BATCH, L, PAST_LEN, N_HEADS, HEAD_DIM), 1.0),
        'state_ssm_re': nrm(ks[4], (DEC_BATCH, L, 2, G, P), 0.5),
        'state_ssm_im': nrm(ks[5], (DEC_BATCH, L, 2, G, P), 0.5),
        'c': nrm(ks[6], (DEC_BATCH, D_MODEL), 1.0),
        'c_ctx': nrm(ks[7], (D_MODEL,), 1.0),
        'norm_w': 1.0 + nrm(ks[8], (L, D_MODEL), 0.02),
        'w_ada': nrm(ks[9], (L, D_MODEL, 3 * D_MODEL), 0.5 * D_MODEL ** -0.5),
        'b_ada': nrm(ks[10], (L, 3 * D_MODEL), 0.02),
        'w_in': nrm(ks[11], (L, D_MODEL, IN_WIDTH), D_MODEL ** -0.5),
        'q_norm_w': 1.0 + nrm(ks[12], (L, HEAD_DIM), 0.02),
        'k_norm_w': 1.0 + nrm(ks[13], (L, HEAD_DIM), 0.02),
        'rel_pos_bias': nrm(ks[14], (L, N_HEADS, 2 * WIN_ROWS - 1, 2 * WIN_COLS - 1), 0.02),
        'ssm_a_re': -0.5 + nrm(ks[15], (L, 2, G, P), 0.01),
        'ssm_a_im': jnp.pi * n_idx + nrm(ks[16], (L, 2, G, P), 0.01),
        'ssm_log_dt': jax.random.uniform(ks[17], (L, 2, G), f32, math.log(1e-3), math.log(1e-1)),
        'ssm_b_re': nrm(ks[18], (L, 2, G, P, GC), (2 * GC) ** -0.5),
        'ssm_b_im': nrm(ks[19], (L, 2, G, P, GC), (2 * GC) ** -0.5),
        'ssm_c_re': nrm(ks[20], (L, 2, G, GC, P), (2 * P) ** -0.5),
        'ssm_c_im': nrm(ks[21], (L, 2, G, GC, P), (2 * P) ** -0.5),
        'ssm_d': nrm(ks[22], (L, SSM_WIDTH), 1.0),
        'w_glu': nrm(ks[23], (L, SSM_WIDTH, SSM_WIDTH), SSM_WIDTH ** -0.5),
        'b_glu': nrm(ks[24], (L, SSM_WIDTH), 0.02),
        'w_ssm_out': nrm(ks[25], (L, SSM_WIDTH, D_MODEL), SSM_WIDTH ** -0.5),
        'w_att_out': nrm(ks[26], (L, ATT_WIDTH, D_MODEL), ATT_WIDTH ** -0.5),
        'w_o': nrm(ks[27], (L, D_MODEL, D_MODEL), D_MODEL ** -0.5),
    }


def reference(x_prompt, x_sample, cache_k, cache_v, state_ssm_re, state_ssm_im, c, c_ctx,
              norm_w, w_ada, b_ada, w_in, q_norm_w, k_norm_w, rel_pos_bias,
              ssm_a_re, ssm_a_im, ssm_log_dt, ssm_b_re, ssm_b_im, ssm_c_re, ssm_c_im, ssm_d,
              w_glu, b_glu, w_ssm_out, w_att_out, w_o):
    f32 = jnp.float32
    xp = x_prompt
    xs = x_sample
    new_k, new_v, new_re, new_im = [], [], [], []
    for l in range(DEPTH):
        lam_bar, b_bar = _s5_discretize(ssm_a_re[l], ssm_a_im[l], ssm_log_dt[l], ssm_b_re[l], ssm_b_im[l])

        shift, scale, gate = _modulation(c_ctx[None, :], w_ada[l], b_ada[l])
        u, z_s, q, k, v, z_a, g_s, g_a = _front(xp, shift, scale, norm_w[l], w_in[l], q_norm_w[l], k_norm_w[l])
        h0 = jnp.zeros((xp.shape[0], SSM_GROUPS, SSM_STATE), jnp.complex64)
        y_ssm, hf, hb = _s5_bidir(u, lam_bar, b_bar, ssm_c_re[l], ssm_c_im[l], ssm_d[l], h0, h0)
        attn = _context_attention(q, k, v)
        h_last = jnp.stack([hf, hb], axis=1)
        new_k.append(k)
        new_v.append(v)
        new_re.append(h_last.real)
        new_im.append(h_last.imag)
        xp = _back(xp, y_ssm, z_s, attn, z_a, g_s, g_a, gate,
                   w_glu[l], b_glu[l], w_ssm_out[l], w_att_out[l], w_o[l])

        shift, scale, gate = _modulation(c, w_ada[l], b_ada[l])
        u, z_s, q, k, v, z_a, g_s, g_a = _front(xs, shift, scale, norm_w[l], w_in[l], q_norm_w[l], k_norm_w[l])
        st = lax.complex(state_ssm_re[:, l].astype(f32), state_ssm_im[:, l].astype(f32))
        y_ssm, _, _ = _s5_bidir(u, lam_bar, b_bar, ssm_c_re[l], ssm_c_im[l], ssm_d[l], st[:, 0], st[:, 1])
        attn = _neighbourhood_attention(q, k, v, cache_k[:, l], cache_v[:, l], rel_pos_bias[l])
        xs = _back(xs, y_ssm, z_s, attn, z_a, g_s, g_a, gate,
                   w_glu[l], b_glu[l], w_ssm_out[l], w_att_out[l], w_o[l])

    new_cache_k = jnp.stack(new_k, axis=1)
    new_cache_v = jnp.stack(new_v, axis=1)
    new_state_re = jnp.stack(new_re, axis=1)
    new_state_im = jnp.stack(new_im, axis=1)
    return (xp, xs, new_cache_k, new_cache_v, new_state_re, new_state_im)
```

```python
import functools
import math

import numpy as np
import jax
import jax.numpy as jnp
from jax import lax
from jax.experimental import pallas as pl
from jax.experimental.pallas import tpu as pltpu

D_MODEL = 2048
BATCH = 16
SEQ = 256
DEC_BATCH = 2
DEC_SEQ = 2048
PAST_LEN = 512
GRID_W = 64
GRID_H = DEC_SEQ // GRID_W
SSM_WIDTH = D_MODEL // 2
SSM_GROUP = 16
SSM_GROUPS = SSM_WIDTH // SSM_GROUP
SSM_STATE = 64
N_HEADS = 16
HEAD_DIM = 64
ATT_WIDTH = N_HEADS * HEAD_DIM
WIN_ROWS = 8
WIN_COLS = 16
EPS = 1e-6
NEG_INF = -1e30

N_CTX_TOK = BATCH * SEQ
N_DEC_TOK = DEC_BATCH * DEC_SEQ
N_TOK = N_CTX_TOK + N_DEC_TOK

OFF_U = 0
OFF_ZS = SSM_WIDTH
OFF_Q = 2 * SSM_WIDTH
OFF_K = OFF_Q + ATT_WIDTH
OFF_V = OFF_K + ATT_WIDTH
OFF_ZA = OFF_V + ATT_WIDTH
OFF_GS = OFF_ZA + ATT_WIDTH
OFF_GA = OFF_GS + D_MODEL

CHUNK = 16
FLAT = CHUNK * SSM_GROUP
N_SEG = 16
CH_PER_SEG = 16
DSTATE = 2 * SSM_STATE

Q_ROWS_PER_BLOCK = 4
NBR_Q = Q_ROWS_PER_BLOCK * GRID_W
NBR_KROWS = 12
NBR_K = NBR_KROWS * GRID_W

VMEM_LIMIT = 56 * 1024 * 1024

F32 = jnp.float32
BF16 = jnp.bfloat16
HIGHEST = lax.Precision.HIGHEST


def _sigmoid(x):
    return 1.0 / (1.0 + jnp.exp(-x))


def _silu(x):
    return x * _sigmoid(x)


def _gelu_tanh(x):
    return 0.5 * x * (1.0 + jnp.tanh(math.sqrt(2.0 / math.pi) * (x + 0.044715 * (x * x * x))))


def _cmul(ar, ai, br, bi):
    return ar * br - ai * bi, ar * bi + ai * br


def _dot_nt(a, b):
    return lax.dot_general(a, b, (((1,), (1,)), ((), ())), preferred_element_type=F32)


def _params(*sem):
    return pltpu.CompilerParams(dimension_semantics=sem, vmem_limit_bytes=VMEM_LIMIT)


def _mod_kernel(cond_ref, w_ref, b_ref, o_ref):
    c = cond_ref[...]
    s = _silu(c).astype(BF16)
    o_ref[...] = jnp.dot(s, w_ref[...].astype(BF16), preferred_element_type=F32) + b_ref[...]


def _modulation(cond8, w_ada, b_ada):
    tn = 768
    n = w_ada.shape[1]
    return pl.pallas_call(
        _mod_kernel,
        grid=(n // tn,),
        in_specs=[pl.BlockSpec((8, D_MODEL), lambda j: (0, 0)),
                  pl.BlockSpec((D_MODEL, tn), lambda j: (0, j)),
                  pl.BlockSpec((1, tn), lambda j: (0, j))],
        out_specs=pl.BlockSpec((8, tn), lambda j: (0, j)),
        out_shape=jax.ShapeDtypeStruct((8, n), F32),
        compiler_params=_params("arbitrary"),
        name="modulation",
    )(cond8, w_ada, b_ada.reshape(1, n))


NORM_TM = 512
NORM_CTX_STEPS = N_CTX_TOK // NORM_TM
NORM_STEPS_PER_DEC_SEQ = DEC_SEQ // NORM_TM


def _mod_row(i, ctx_steps, steps_per_seq):
    return jnp.where(i < ctx_steps, 0, 1 + (i - ctx_steps) // steps_per_seq)


def _norm_kernel(xp_ref, xs_ref, shift_ref, scale_ref, nw_ref, o_ref):
    i = pl.program_id(0)

    def body(x):
        ms = jnp.mean(x * x, axis=-1, keepdims=True)
        y = x * lax.rsqrt(ms + EPS) * nw_ref[...]
        o_ref[...] = (y * (1.0 + scale_ref[0]) + shift_ref[0]).astype(BF16)

    @pl.when(i < NORM_CTX_STEPS)
    def _():
        body(xp_ref[...])

    @pl.when(i >= NORM_CTX_STEPS)
    def _():
        body(xs_ref[...])


def _norm_modulate(xp, xs, shift3, scale3, norm_w):
    steps = N_TOK // NORM_TM
    row = functools.partial(_mod_row, ctx_steps=NORM_CTX_STEPS, steps_per_seq=NORM_STEPS_PER_DEC_SEQ)
    return pl.pallas_call(
        _norm_kernel,
        grid=(steps,),
        in_specs=[pl.BlockSpec((NORM_TM, D_MODEL), lambda i: (jnp.minimum(i, NORM_CTX_STEPS - 1), 0)),
                  pl.BlockSpec((NORM_TM, D_MODEL), lambda i: (jnp.maximum(i - NORM_CTX_STEPS, 0), 0)),
                  pl.BlockSpec((1, 1, D_MODEL), lambda i: (row(i), 0, 0)),
                  pl.BlockSpec((1, 1, D_MODEL), lambda i: (row(i), 0, 0)),
                  pl.BlockSpec((1, D_MODEL), lambda i: (0, 0))],
        out_specs=pl.BlockSpec((NORM_TM, D_MODEL), lambda i: (i, 0)),
        out_shape=jax.ShapeDtypeStruct((N_TOK, D_MODEL), BF16),
        compiler_params=_params("arbitrary"),
        name="norm_modulate",
    )(xp, xs, shift3, scale3, norm_w.reshape(1, D_MODEL))


PROJ_TM = 1024
PROJ_TN = 512
PROJ_CTX_STEPS = N_CTX_TOK // PROJ_TM
HEADS_PER_MXU_TILE = 256 // HEAD_DIM


def _head_group_ones():
    r = lax.broadcasted_iota(jnp.int32, (256, 256), 0) // HEAD_DIM
    c = lax.broadcasted_iota(jnp.int32, (256, 256), 1) // HEAD_DIM
    return jnp.where(r == c, 1.0, 0.0).astype(BF16)


def _head_rms(acc, nw):
    ones = _head_group_ones()
    outs = []
    for c in range(acc.shape[1] // 256):
        a = acc[:, c * 256:(c + 1) * 256]
        ssum = jnp.dot((a * a).astype(BF16), ones, preferred_element_type=F32)
        outs.append(a * lax.rsqrt(ssum * (1.0 / HEAD_DIM) + EPS))
    return jnp.concatenate(outs, axis=1) * nw


def _proj_kernel(*refs, mode):
    if mode == "plain":
        h_ref, w_ref, o_ref, wbf = refs
    elif mode == "q":
        h_ref, w_ref, nw_ref, o_ref, wbf = refs
    elif mode == "k":
        h_ref, w_ref, nw_ref, o_ref, of_ref, wbf = refs
    else:
        h_ref, w_ref, o_ref, of_ref, wbf = refs
    i = pl.program_id(1)

    @pl.when(i == 0)
    def _():
        wbf[...] = w_ref[...].astype(BF16)

    acc = jnp.dot(h_ref[...], wbf[...], preferred_element_type=F32)
    if mode == "q":
        acc = _head_rms(acc, nw_ref[...]) * (HEAD_DIM ** -0.5)
    elif mode == "k":
        acc = _head_rms(acc, nw_ref[...])
    o_ref[...] = acc.astype(BF16)
    if mode in ("k", "v"):
        @pl.when(i < PROJ_CTX_STEPS)
        def _():
            of_ref[...] = acc


def _in_proj(h, w_in, col_off, ncols, mode, norm_w=None):
    nj = ncols // PROJ_TN
    ni = N_TOK // PROJ_TM
    joff = col_off // PROJ_TN
    in_specs = [pl.BlockSpec((PROJ_TM, D_MODEL), lambda j, i: (i, 0)),
                pl.BlockSpec((D_MODEL, PROJ_TN), lambda j, i: (0, joff + j))]
    args = [h, w_in]
    if mode in ("q", "k"):
        in_specs.append(pl.BlockSpec((1, PROJ_TN), lambda j, i: (0, 0)))
        args.append(jnp.tile(norm_w.reshape(1, HEAD_DIM), (1, PROJ_TN // HEAD_DIM)))
    out_specs = [pl.BlockSpec((PROJ_TM, PROJ_TN), lambda j, i: (i, j))]
    out_shape = [jax.ShapeDtypeStruct((N_TOK, ncols), BF16)]
    if mode in ("k", "v"):
        out_specs.append(pl.BlockSpec((PROJ_TM, PROJ_TN),
                                      lambda j, i: (jnp.minimum(i, PROJ_CTX_STEPS - 1), j)))
        out_shape.append(jax.ShapeDtypeStruct((N_CTX_TOK, ncols), F32))
    res = pl.pallas_call(
        functools.partial(_proj_kernel, mode=mode),
        grid=(nj, ni),
        in_specs=in_specs,
        out_specs=out_specs,
        out_shape=out_shape,
        scratch_shapes=[pltpu.VMEM((D_MODEL, PROJ_TN), BF16)],
        compiler_params=_params("arbitrary", "arbitrary"),
        name="in_proj_" + mode,
    )(*args)
    return res if mode in ("k", "v") else res[0]


SSM_GB = 4


def _pow_select(e, pows):
    rr = jnp.where((e & 1) != 0, pows[0][0], 1.0)
    ri = jnp.where((e & 1) != 0, pows[0][1], 0.0)
    for b in range(1, len(pows)):
        bit = (e & (1 << b)) != 0
        fr = jnp.where(bit, pows[b][0], 1.0)
        fi = jnp.where(bit, pows[b][1], 0.0)
        rr, ri = _cmul(rr, ri, fr, fi)
    return rr, ri


def _discretize(a_re, a_im, log_dt):
    lr = jnp.minimum(a_re, -1e-4)
    li = a_im
    dt = jnp.exp(log_dt)
    mag = jnp.exp(lr * dt)
    br = mag * jnp.cos(li * dt)
    bi = mag * jnp.sin(li * dt)
    den = lr * lr + li * li
    nr = br - 1.0
    cr = (nr * lr + bi * li) / den
    ci = (bi * lr - nr * li) / den
    return (br, bi), (cr, ci)


def _squarings(pr, pi, n):
    out = [(pr, pi)]
    for _ in range(n):
        pr, pi = _cmul(pr, pi, pr, pi)
        out.append((pr, pi))
    return out


def _ssm_ops_kernel(rowp_ref, colp_ref, btr_ref, bti_ref, ctr_ref, cti_ref,
                    t_ref, s_ref, r_ref, lam_ref, powc_ref):
    lane_s = lax.broadcasted_iota(jnp.int32, (FLAT, DSTATE), 1)
    row_s = lax.broadcasted_iota(jnp.int32, (FLAT, DSTATE), 0)
    exp_s = jnp.where(lane_s < SSM_STATE, (CHUNK - 1) - (row_s >> 4), row_s >> 4)
    row_k = lax.broadcasted_iota(jnp.int32, (DSTATE, FLAT), 0)
    lane_k = lax.broadcasted_iota(jnp.int32, (DSTATE, FLAT), 1)
    exp_k = jnp.where(row_k < SSM_STATE, lane_k >> 4, (CHUNK - 1) - (lane_k >> 4))
    tile_e = jnp.where((lax.broadcasted_iota(jnp.int32, (SSM_GROUP, FLAT), 1) & (SSM_GROUP - 1))
                       == lax.broadcasted_iota(jnp.int32, (SSM_GROUP, FLAT), 0), 1.0, 0.0)
    lane_b = lax.broadcasted_iota(jnp.int32, (SSM_GROUP, DSTATE), 1)
    lane_t = lax.broadcasted_iota(jnp.int32, (SSM_GROUP, FLAT), 1)
    row_c = lax.broadcasted_iota(jnp.int32, (CH_PER_SEG, DSTATE), 0)
    lane_c = lax.broadcasted_iota(jnp.int32, (CH_PER_SEG, DSTATE), 1)
    exp_c = jnp.where(lane_c < SSM_STATE, row_c, (CH_PER_SEG - 1) - row_c)

    for g in range(SSM_GB):
        rp = rowp_ref[g]
        (lbr, lbi), (cfr, cfi) = _discretize(rp[0:1], rp[1:2], rp[2:3])
        sq = _squarings(lbr, lbi, 8)
        bbr, bbi = _cmul(btr_ref[g], bti_ref[g], cfr, cfi)

        pr, pi = _pow_select(exp_s, sq[:4])
        tb_r = jnp.concatenate([bbr] * CHUNK, axis=0)
        tb_i = jnp.concatenate([bbi] * CHUNK, axis=0)
        s_re, s_im = _cmul(tb_r, tb_i, pr, pi)
        s_ref[g] = jnp.concatenate([s_re, s_im], axis=1).astype(BF16)

        cp = colp_ref[g]
        (kbr, kbi), _ = _discretize(cp[:, 0:1], cp[:, 1:2], cp[:, 2:3])
        sqc = _squarings(kbr, kbi, 3)
        ctr = jnp.dot(ctr_ref[g], tile_e, preferred_element_type=F32, precision=HIGHEST)
        cti = jnp.dot(cti_ref[g], tile_e, preferred_element_type=F32, precision=HIGHEST)
        pkr, pki = _pow_select(exp_k, sqc)
        ykr, yki = _cmul(ctr, cti, pkr, pki)
        yrr, yri = _cmul(ykr, yki, kbr, kbi)
        r_ref[g] = jnp.concatenate([yrr, -yri], axis=0).astype(BF16)

        fwd = lane_b < SSM_STATE
        lhs = jnp.concatenate([jnp.where(fwd, bbr, 0.0), jnp.where(fwd, bbi, 0.0),
                               jnp.where(fwd, 0.0, bbr), jnp.where(fwd, 0.0, bbi)], axis=0)
        p1 = jnp.dot(lhs, ykr, preferred_element_type=F32, precision=HIGHEST)
        p2 = jnp.dot(lhs, yki, preferred_element_type=F32, precision=HIGHEST)
        kf = p1[0:16] - p2[16:32]
        kb = p1[32:48] - p2[48:64]
        for sp in range(CHUNK):
            tf = kf if sp == 0 else pltpu.roll(kf, SSM_GROUP * sp, axis=1)
            tf = jnp.where(lane_t >= SSM_GROUP * sp, tf, 0.0)
            shift = (FLAT - SSM_GROUP * (CHUNK - 1 - sp)) % FLAT
            tb = kb if shift == 0 else pltpu.roll(kb, shift, axis=1)
            tb = jnp.where(lane_t < SSM_GROUP * (sp + 1), tb, 0.0)
            t_ref[g, sp * SSM_GROUP:(sp + 1) * SSM_GROUP, :] = (tf + tb).astype(BF16)

        l16 = sq[4]
        l256 = sq[8]
        lam_ref[g] = jnp.concatenate([l16[0], l16[1], l256[0], l256[1],
                                      jnp.zeros((4, DSTATE), F32)], axis=0)
        q16 = _squarings(l16[0], l16[1], 3)
        pcr, pci = _pow_select(exp_c, q16)
        powc_ref[g, 0] = pcr
        powc_ref[g, 1] = pci


def _ssm_operators(rowp, colp, btr, bti, ctr, cti):
    G = SSM_GROUPS
    gb = SSM_GB
    return pl.pallas_call(
        _ssm_ops_kernel,
        grid=(G // gb,),
        in_specs=[pl.BlockSpec((gb, 3, DSTATE), lambda g: (g, 0, 0)),
                  pl.BlockSpec((gb, DSTATE, 3), lambda g: (g, 0, 0)),
                  pl.BlockSpec((gb, SSM_GROUP, DSTATE), lambda g: (g, 0, 0)),
                  pl.BlockSpec((gb, SSM_GROUP, DSTATE), lambda g: (g, 0, 0)),
                  pl.BlockSpec((gb, DSTATE, SSM_GROUP), lambda g: (g, 0, 0)),
                  pl.BlockSpec((gb, DSTATE, SSM_GROUP), lambda g: (g, 0, 0))],
        out_specs=[pl.BlockSpec((gb, FLAT, FLAT), lambda g: (g, 0, 0)),
                   pl.BlockSpec((gb, FLAT, FLAT), lambda g: (g, 0, 0)),
                   pl.BlockSpec((gb, FLAT, FLAT), lambda g: (g, 0, 0)),
                   pl.BlockSpec((gb, 8, DSTATE), lambda g: (g, 0, 0)),
                   pl.BlockSpec((gb, 2, CH_PER_SEG, DSTATE), lambda g: (g, 0, 0, 0))],
        out_shape=[jax.ShapeDtypeStruct((G, FLAT, FLAT), BF16),
                   jax.ShapeDtypeStruct((G, FLAT, FLAT), BF16),
                   jax.ShapeDtypeStruct((G, FLAT, FLAT), BF16),
                   jax.ShapeDtypeStruct((G, 8, DSTATE), F32),
                   jax.ShapeDtypeStruct((G, 2, CH_PER_SEG, DSTATE), F32)],
        compiler_params=_params("arbitrary"),
        name="ssm_operators",
    )(rowp, colp, btr, bti, ctr, cti)


SSM_PATHS = ((BATCH, SEQ // (CHUNK * CH_PER_SEG)), (DEC_BATCH, DEC_SEQ // (CHUNK * CH_PER_SEG)))


def _ssm_path(u, s_op, t_op, r_op, lam, powc_ref, dskip, h0r, h0i, nseg):
    lane = lax.broadcasted_iota(jnp.int32, (N_SEG, DSTATE), 1)
    isf = lane < SSM_STATE
    l16r, l16i, l256r, l256i = lam[0:1], lam[1:2], lam[2:3], lam[3:4]

    z = jnp.dot(u, s_op, preferred_element_type=F32)
    zre, zim = z[:, :DSTATE], z[:, DSTATE:]

    hr = jnp.zeros((N_SEG, DSTATE), F32)
    hi = jnp.zeros((N_SEG, DSTATE), F32)
    hist = []
    for k in range(CH_PER_SEG):
        hist.append((hr, hi))
        kb = CH_PER_SEG - 1 - k
        zr = jnp.where(isf, zre[k * N_SEG:(k + 1) * N_SEG], zre[kb * N_SEG:(kb + 1) * N_SEG])
        zi = jnp.where(isf, zim[k * N_SEG:(k + 1) * N_SEG], zim[kb * N_SEG:(kb + 1) * N_SEG])
        nr, ni = _cmul(l16r, l16i, hr, hi)
        hr, hi = nr + zr, ni + zi

    if nseg > 1:
        h0r = jnp.broadcast_to(h0r[:, None, :], (N_SEG // nseg, nseg, DSTATE)).reshape(N_SEG, DSTATE)
        h0i = jnp.broadcast_to(h0i[:, None, :], (N_SEG // nseg, nseg, DSTATE)).reshape(N_SEG, DSTATE)
    seg = lax.broadcasted_iota(jnp.int32, (N_SEG, DSTATE), 0) & (nseg - 1)
    segpow = _squarings(l256r, l256i, max(int(math.log2(nseg)), 0))

    def shifted(x, d):
        dn = jnp.where(seg >= d, pltpu.roll(x, d, axis=0), 0.0)
        up = jnp.where(seg <= nseg - 1 - d, pltpu.roll(x, N_SEG - d, axis=0), 0.0)
        return jnp.where(isf, dn, up)

    pr, pi = hr, hi
    d = 1
    lvl = 0
    while d < nseg:
        ar, ai = _cmul(segpow[lvl][0], segpow[lvl][1], shifted(pr, d), shifted(pi, d))
        pr, pi = pr + ar, pi + ai
        d *= 2
        lvl += 1
    if nseg > 1:
        e_in = jnp.where(isf, seg, nseg - 1 - seg)
        wr, wi = _pow_select(e_in, segpow[:lvl])
        ar, ai = _cmul(wr, wi, h0r, h0i)
        hsr, hsi = shifted(pr, 1) + ar, shifted(pi, 1) + ai
    else:
        hsr, hsi = h0r, h0i
    er, ei = _cmul(segpow[lvl][0], segpow[lvl][1], h0r, h0i)
    fin = jnp.concatenate([pr + er, pi + ei], axis=1)

    rows = []
    for c in range(CH_PER_SEG):
        cb = CH_PER_SEG - 1 - c
        lr = jnp.where(isf, hist[c][0], hist[cb][0])
        li = jnp.where(isf, hist[c][1], hist[cb][1])
        ar, ai = _cmul(powc_ref[0, c:c + 1, :], powc_ref[1, c:c + 1, :], hsr, hsi)
        rows.append(jnp.concatenate([lr + ar, li + ai], axis=1))
    hent = jnp.concatenate(rows, axis=0).astype(BF16)

    y = (jnp.dot(u, t_op, preferred_element_type=F32)
         + jnp.dot(hent, r_op, preferred_element_type=F32)
         + u.astype(F32) * dskip)
    return y, fin


def _ssm_kernel(u_ref, t_ref, s_ref, r_ref, lam_ref, powc_ref, d_ref,
                h0cr_ref, h0ci_ref, h0dr_ref, h0di_ref, y_ref, fin_ref):
    h0 = ((h0cr_ref, h0ci_ref), (h0dr_ref, h0di_ref))
    for path, (_, nseg) in enumerate(SSM_PATHS):
        y, fin = _ssm_path(u_ref[path, 0], s_ref[0], t_ref[0], r_ref[0], lam_ref[0], powc_ref.at[0],
                           d_ref[0], h0[path][0][0], h0[path][1][0], nseg)
        y_ref[path, 0] = y
        fin_ref[path, 0] = fin


def _ssm_scan(u_flat, t_op, s_op, r_op, lam, powc, dtile, h0c_re, h0c_im, h0d_re, h0d_im):
    G = SSM_GROUPS
    rows = N_SEG * CH_PER_SEG
    return pl.pallas_call(
        _ssm_kernel,
        grid=(G,),
        in_specs=[pl.BlockSpec((2, 1, rows, FLAT), lambda g: (0, g, 0, 0)),
                  pl.BlockSpec((1, FLAT, FLAT), lambda g: (g, 0, 0)),
                  pl.BlockSpec((1, FLAT, FLAT), lambda g: (g, 0, 0)),
                  pl.BlockSpec((1, FLAT, FLAT), lambda g: (g, 0, 0)),
                  pl.BlockSpec((1, 8, DSTATE), lambda g: (g, 0, 0)),
                  pl.BlockSpec((1, 2, CH_PER_SEG, DSTATE), lambda g: (g, 0, 0, 0)),
                  pl.BlockSpec((1, 1, FLAT), lambda g: (g, 0, 0)),
                  pl.BlockSpec((1, BATCH, DSTATE), lambda g: (g, 0, 0)),
                  pl.BlockSpec((1, BATCH, DSTATE), lambda g: (g, 0, 0)),
                  pl.BlockSpec((1, DEC_BATCH, DSTATE), lambda g: (g, 0, 0)),
                  pl.BlockSpec((1, DEC_BATCH, DSTATE), lambda g: (g, 0, 0))],
        out_specs=[pl.BlockSpec((2, 1, rows, FLAT), lambda g: (0, g, 0, 0)),
                   pl.BlockSpec((2, 1, N_SEG, 2 * DSTATE), lambda g: (0, g, 0, 0))],
        out_shape=[jax.ShapeDtypeStruct((2, G, rows, FLAT), F32),
                   jax.ShapeDtypeStruct((2, G, N_SEG, 2 * DSTATE), F32)],
        compiler_params=_params("arbitrary"),
        name="ssm_scan",
    )(u_flat, t_op, s_op, r_op, lam, powc, dtile, h0c_re, h0c_im, h0d_re, h0d_im)


CTX_BB = 4


def _softmax_pv(scores, values):
    m = scores[0].max(axis=-1, keepdims=True)
    for s in scores[1:]:
        m = jnp.maximum(m, s.max(axis=-1, keepdims=True))
    l = None
    o = None
    for s, v in zip(scores, values):
        p = jnp.exp(s - m)
        ls = p.sum(axis=-1, keepdims=True)
        os_ = jnp.dot(p.astype(BF16), v, preferred_element_type=F32)
        l = ls if l is None else l + ls
        o = os_ if o is None else o + os_
    return o / l


def _ctx_attn_kernel(q_ref, k_ref, v_ref, za_ref, o_ref):
    head0 = lax.broadcasted_iota(jnp.int32, (1, 2 * HEAD_DIM), 1) < HEAD_DIM
    for b in range(CTX_BB):
        sl = slice(b * SEQ, (b + 1) * SEQ)
        q, k, v = q_ref[sl, :], k_ref[sl, :], v_ref[sl, :]
        outs = []
        for hh in range(2):
            qh = jnp.where(head0 if hh == 0 else jnp.logical_not(head0), q, jnp.zeros_like(q))
            outs.append(_softmax_pv([_dot_nt(qh, k)], [v]))
        o = jnp.where(head0, outs[0], outs[1])
        o_ref[sl, :] = (o * _silu(za_ref[sl, :].astype(F32))).astype(BF16)


def _ctx_attention(q, k, v, za):
    rows = CTX_BB * SEQ
    spec = pl.BlockSpec((rows, 2 * HEAD_DIM), lambda b, hp: (b, hp))
    return pl.pallas_call(
        _ctx_attn_kernel,
        grid=(BATCH // CTX_BB, N_HEADS // 2),
        in_specs=[spec, spec, spec, spec],
        out_specs=spec,
        out_shape=jax.ShapeDtypeStruct((N_CTX_TOK, ATT_WIDTH), BF16),
        compiler_params=_params("arbitrary", "arbitrary"),
        name="ctx_attention",
    )(q, k, v, za)


def _nbr_attn_kernel(q_ref, k_ref, v_ref, kc_ref, vc_ref, bias_ref, za_ref, o_ref):
    a = pl.program_id(1)
    start = pl.multiple_of(jnp.clip(a * NBR_Q - NBR_Q, 0, DEC_SEQ - NBR_K), NBR_Q)
    head0 = lax.broadcasted_iota(jnp.int32, (1, 2 * HEAD_DIM), 1) < HEAD_DIM
    for b in range(DEC_BATCH):
        q = q_ref[b]
        kw = k_ref[b, pl.ds(start, NBR_K), :]
        vw = v_ref[b, pl.ds(start, NBR_K), :]
        kc = kc_ref[b].astype(BF16)
        vc = vc_ref[b].astype(BF16)
        outs = []
        for hh in range(2):
            qh = jnp.where(head0 if hh == 0 else jnp.logical_not(head0), q, jnp.zeros_like(q))
            sw = _dot_nt(qh, kw) + bias_ref[0, 0, hh]
            sc = _dot_nt(qh, kc)
            outs.append(_softmax_pv([sw, sc], [vw, vc]))
        o = jnp.where(head0, outs[0], outs[1])
        o_ref[b] = (o * _silu(za_ref[b].astype(F32))).astype(BF16)


def _nbr_bias_type(a):
    nblk = GRID_H // Q_ROWS_PER_BLOCK
    return jnp.where(a == 0, 0, jnp.where(a == nblk - 1, 2, 1))


def _nbr_attention(q4, k4, v4, kc, vc, bias, za4):
    first = N_CTX_TOK // DEC_SEQ // DEC_BATCH
    nblk = GRID_H // Q_ROWS_PER_BLOCK
    hw = 2 * HEAD_DIM
    qspec = pl.BlockSpec((DEC_BATCH, NBR_Q, hw), lambda hp, a: (first, a, hp))
    kspec = pl.BlockSpec((DEC_BATCH, DEC_SEQ, hw), lambda hp, a: (first, 0, hp))
    cspec = pl.BlockSpec((DEC_BATCH, PAST_LEN, hw), lambda hp, a: (0, 0, hp))
    bspec = pl.BlockSpec((1, 1, 2, NBR_Q, NBR_K), lambda hp, a: (_nbr_bias_type(a), hp, 0, 0, 0))
    return pl.pallas_call(
        _nbr_attn_kernel,
        grid=(N_HEADS // 2, nblk),
        in_specs=[qspec, kspec, kspec, cspec, cspec, bspec, qspec],
        out_specs=pl.BlockSpec((DEC_BATCH, NBR_Q, hw), lambda hp, a: (0, a, hp)),
        out_shape=jax.ShapeDtypeStruct((DEC_BATCH, DEC_SEQ, ATT_WIDTH), BF16),
        compiler_params=_params("arbitrary", "arbitrary"),
        name="nbr_attention",
    )(q4, k4, v4, kc, vc, bias, za4)


def _nbr_bias_tables(rpb):
    nblk = GRID_H // Q_ROWS_PER_BLOCK
    tables = []
    for a in (0, 1, nblk - 1):
        ks = int(np.clip(Q_ROWS_PER_BLOCK * a - Q_ROWS_PER_BLOCK, 0, GRID_H - NBR_KROWS))
        r = (Q_ROWS_PER_BLOCK * a + np.arange(Q_ROWS_PER_BLOCK))[:, None, None, None]
        qc = np.arange(GRID_W)[None, :, None, None]
        kr = (ks + np.arange(NBR_KROWS))[None, None, :, None]
        kc = np.arange(GRID_W)[None, None, None, :]
        rs = np.clip(r - WIN_ROWS // 2, 0, GRID_H - WIN_ROWS)
        cs = np.clip(qc - WIN_COLS // 2, 0, GRID_W - WIN_COLS)
        valid = (kr >= rs) & (kr < rs + WIN_ROWS) & (kc >= cs) & (kc < cs + WIN_COLS)
        dr = np.clip(kr - r + WIN_ROWS - 1, 0, 2 * WIN_ROWS - 2)
        dc = np.clip(kc - qc + WIN_COLS - 1, 0, 2 * WIN_COLS - 2)
        shape = (Q_ROWS_PER_BLOCK, GRID_W, NBR_KROWS, GRID_W)
        dr = np.broadcast_to(dr, shape).reshape(NBR_Q, NBR_K)
        dc = np.broadcast_to(dc, shape).reshape(NBR_Q, NBR_K)
        valid = np.broadcast_to(valid, shape).reshape(NBR_Q, NBR_K)
        tables.append(jnp.where(valid[None], rpb[:, dr, dc], NEG_INF))
    return jnp.stack(tables).reshape(3, N_HEADS // 2, 2, NBR_Q, NBR_K)


BACK_TM = 256
BACK_CTX_STEPS = N_CTX_TOK // BACK_TM
BACK_STEPS_PER_DEC_SEQ = DEC_SEQ // BACK_TM


def _back_kernel(xp_ref, xs_ref, y_ref, zs_ref, ac_ref, ad_ref, gs_ref, ga_ref, gate_ref, bglu_ref,
                 wglu_ref, wso_ref, wao_ref, wo_ref, op_ref, os_ref):
    i = pl.program_id(0)

    def compute(x, a2):
        ys = _gelu_tanh(y_ref[...])
        t = jnp.dot(ys.astype(BF16), wglu_ref[...], preferred_element_type=F32) + bglu_ref[...]
        ys = ys * _sigmoid(t) * _silu(zs_ref[...].astype(F32))
        p_s = jnp.dot(ys.astype(BF16), wso_ref[...], preferred_element_type=F32)
        p_a = jnp.dot(a2, wao_ref[...], preferred_element_type=F32)
        merged = (_sigmoid(gs_ref[...].astype(F32)) * p_s + _sigmoid(ga_ref[...].astype(F32)) * p_a)
        return x + gate_ref[0] * jnp.dot(merged.astype(BF16), wo_ref[...], preferred_element_type=F32)

    @pl.when(i < BACK_CTX_STEPS)
    def _():
        op_ref[...] = compute(xp_ref[...], ac_ref[...])

    @pl.when(i >= BACK_CTX_STEPS)
    def _():
        os_ref[...] = compute(xs_ref[...], ad_ref[...])


def _back(xp, xs, y, zs, a_ctx, a_dec, gs, ga, gate3, b_glu, w_glu, w_so, w_ao, w_o):
    steps = N_TOK // BACK_TM
    n0 = BACK_CTX_STEPS
    row = functools.partial(_mod_row, ctx_steps=n0, steps_per_seq=BACK_STEPS_PER_DEC_SEQ)
    lo = lambda i: (jnp.minimum(i, n0 - 1), 0)
    hi = lambda i: (jnp.maximum(i - n0, 0), 0)
    cur = lambda i: (i, 0)
    const = lambda i: (0, 0)
    once = pl.Buffered(1)
    return pl.pallas_call(
        _back_kernel,
        grid=(steps,),
        in_specs=[pl.BlockSpec((BACK_TM, D_MODEL), lo),
                  pl.BlockSpec((BACK_TM, D_MODEL), hi),
                  pl.BlockSpec((BACK_TM, SSM_WIDTH), cur),
                  pl.BlockSpec((BACK_TM, SSM_WIDTH), cur),
                  pl.BlockSpec((BACK_TM, ATT_WIDTH), lo),
                  pl.BlockSpec((BACK_TM, ATT_WIDTH), hi),
                  pl.BlockSpec((BACK_TM, D_MODEL), cur),
                  pl.BlockSpec((BACK_TM, D_MODEL), cur),
                  pl.BlockSpec((1, 1, D_MODEL), lambda i: (row(i), 0, 0)),
                  pl.BlockSpec((1, SSM_WIDTH), const),
                  pl.BlockSpec((SSM_WIDTH, SSM_WIDTH), const, pipeline_mode=once),
                  pl.BlockSpec((SSM_WIDTH, D_MODEL), const, pipeline_mode=once),
                  pl.BlockSpec((ATT_WIDTH, D_MODEL), const, pipeline_mode=once),
                  pl.BlockSpec((D_MODEL, D_MODEL), const, pipeline_mode=once)],
        out_specs=[pl.BlockSpec((BACK_TM, D_MODEL), lo),
                   pl.BlockSpec((BACK_TM, D_MODEL), hi)],
        out_shape=[jax.ShapeDtypeStruct((N_CTX_TOK, D_MODEL), F32),
                   jax.ShapeDtypeStruct((N_DEC_TOK, D_MODEL), F32)],
        compiler_params=_params("arbitrary"),
        name="gated_output",
    )(xp, xs, y, zs, a_ctx, a_dec, gs, ga, gate3, b_glu.reshape(1, SSM_WIDTH), w_glu, w_so, w_ao, w_o)


def _layer(xp, xs, cache_k, cache_v, st_re, st_im, c, c_ctx, norm_w, w_ada, b_ada, w_in, q_norm_w, k_norm_w,
           rpb, a_re, a_im, log_dt, b_re, b_im, c_re, c_im, d, w_glu, b_glu, w_so, w_ao, w_o):
    G, P = SSM_GROUPS, SSM_STATE

    cond8 = jnp.zeros((8, D_MODEL), F32).at[0].set(c_ctx).at[1:1 + DEC_BATCH].set(c)
    mod = _modulation(cond8, w_ada, b_ada)
    shift3 = mod[:1 + DEC_BATCH, None, :D_MODEL]
    scale3 = mod[:1 + DEC_BATCH, None, D_MODEL:2 * D_MODEL]
    gate3 = mod[:1 + DEC_BATCH, None, 2 * D_MODEL:]

    h = _norm_modulate(xp, xs, shift3, scale3, norm_w)
    u = _in_proj(h, w_in, OFF_U, SSM_WIDTH, "plain")
    zs = _in_proj(h, w_in, OFF_ZS, SSM_WIDTH, "plain")
    q = _in_proj(h, w_in, OFF_Q, ATT_WIDTH, "q", q_norm_w)
    k, k_ctx32 = _in_proj(h, w_in, OFF_K, ATT_WIDTH, "k", k_norm_w)
    v, v_ctx32 = _in_proj(h, w_in, OFF_V, ATT_WIDTH, "v")
    za = _in_proj(h, w_in, OFF_ZA, ATT_WIDTH, "plain")
    gs = _in_proj(h, w_in, OFF_GS, D_MODEL, "plain")
    ga = _in_proj(h, w_in, OFF_GA, D_MODEL, "plain")

    arow = jnp.transpose(a_re, (1, 0, 2)).reshape(G, 2 * P)
    airow = jnp.transpose(a_im, (1, 0, 2)).reshape(G, 2 * P)
    dtrow = jnp.broadcast_to(jnp.transpose(log_dt, (1, 0))[:, :, None], (G, 2, P)).reshape(G, 2 * P)
    rowp = jnp.stack([arow, airow, dtrow], axis=1)
    colp = jnp.transpose(rowp, (0, 2, 1))
    btr = jnp.transpose(b_re, (1, 3, 0, 2)).reshape(G, SSM_GROUP, 2 * P)
    bti = jnp.transpose(b_im, (1, 3, 0, 2)).reshape(G, SSM_GROUP, 2 * P)
    ctr = jnp.transpose(c_re, (1, 0, 3, 2)).reshape(G, 2 * P, SSM_GROUP)
    cti = jnp.transpose(c_im, (1, 0, 3, 2)).reshape(G, 2 * P, SSM_GROUP)
    t_op, s_op, r_op, lam, powc = _ssm_operators(rowp, colp, btr, bti, ctr, cti)

    u_flat = (u.reshape(2, N_SEG, CH_PER_SEG, CHUNK, G, SSM_GROUP)
              .transpose(0, 4, 2, 1, 3, 5).reshape(2, G, CH_PER_SEG * N_SEG, FLAT))
    dtile = jnp.tile(d.reshape(G, 1, SSM_GROUP), (1, 1, CHUNK))
    h0c = jnp.zeros((G, BATCH, 2 * P), F32)
    h0d_re = jnp.transpose(st_re, (2, 0, 1, 3)).reshape(G, DEC_BATCH, 2 * P)
    h0d_im = jnp.transpose(st_im, (2, 0, 1, 3)).reshape(G, DEC_BATCH, 2 * P)
    y_flat, fin = _ssm_scan(u_flat, t_op, s_op, r_op, lam, powc, dtile, h0c, h0c, h0d_re, h0d_im)
    y = (y_flat.reshape(2, G, CH_PER_SEG, N_SEG, CHUNK, SSM_GROUP)
         .transpose(0, 3, 2, 4, 1, 5).reshape(N_TOK, SSM_WIDTH))

    a_ctx = _ctx_attention(q, k, v, za)
    nseq4 = N_TOK // DEC_SEQ
    a_dec = _nbr_attention(q.reshape(nseq4, DEC_SEQ, ATT_WIDTH), k.reshape(nseq4, DEC_SEQ, ATT_WIDTH),
                           v.reshape(nseq4, DEC_SEQ, ATT_WIDTH),
                           cache_k.reshape(DEC_BATCH, PAST_LEN, ATT_WIDTH),
                           cache_v.reshape(DEC_BATCH, PAST_LEN, ATT_WIDTH),
                           _nbr_bias_tables(rpb.astype(F32)),
                           za.reshape(nseq4, DEC_SEQ, ATT_WIDTH))

    yp, ys_out = _back(xp, xs, y, zs, a_ctx, a_dec.reshape(N_DEC_TOK, ATT_WIDTH), gs, ga, gate3, b_glu,
                       w_glu.astype(BF16), w_so.astype(BF16), w_ao.astype(BF16), w_o.astype(BF16))

    fin_ctx = fin[0]
    new_re = jnp.transpose(fin_ctx[:, :, :2 * P].reshape(G, BATCH, 2, P), (1, 2, 0, 3))
    new_im = jnp.transpose(fin_ctx[:, :, 2 * P:].reshape(G, BATCH, 2, P), (1, 2, 0, 3))
    return yp, ys_out, k_ctx32, v_ctx32, new_re, new_im


def kernel(x_prompt, x_sample, cache_k, cache_v, state_ssm_re, state_ssm_im, c, c_ctx, norm_w, w_ada, b_ada,
           w_in, q_norm_w, k_norm_w, rel_pos_bias, ssm_a_re, ssm_a_im, ssm_log_dt, ssm_b_re, ssm_b_im,
           ssm_c_re, ssm_c_im, ssm_d, w_glu, b_glu, w_ssm_out, w_att_out, w_o):
    depth = norm_w.shape[0]
    xp = x_prompt.reshape(N_CTX_TOK, D_MODEL)
    xs = x_sample.reshape(N_DEC_TOK, D_MODEL)
    new_k, new_v, new_re, new_im = [], [], [], []
    for l in range(depth):
        xp, xs, kl, vl, rl, il = _layer(
            xp, xs, cache_k[:, l], cache_v[:, l], state_ssm_re[:, l], state_ssm_im[:, l], c, c_ctx,
            norm_w[l], w_ada[l], b_ada[l], w_in[l], q_norm_w[l], k_norm_w[l], rel_pos_bias[l],
            ssm_a_re[l], ssm_a_im[l], ssm_log_dt[l], ssm_b_re[l], ssm_b_im[l], ssm_c_re[l], ssm_c_im[l],
            ssm_d[l], w_glu[l], b_glu[l], w_ssm_out[l], w_att_out[l], w_o[l])
        new_k.append(kl.reshape(BATCH, SEQ, N_HEADS, HEAD_DIM))
        new_v.append(vl.reshape(BATCH, SEQ, N_HEADS, HEAD_DIM))
        new_re.append(rl)
        new_im.append(il)
    return (xp.reshape(BATCH, SEQ, D_MODEL), xs.reshape(DEC_BATCH, DEC_SEQ, D_MODEL),
            jnp.stack(new_k, axis=1), jnp.stack(new_v, axis=1),
            jnp.stack(new_re, axis=1), jnp.stack(new_im, axis=1))
```

```python
import functools
import math

import numpy as np
import jax
import jax.numpy as jnp
from jax import lax
from jax.experimental import pallas as pl
from jax.experimental.pallas import tpu as pltpu

D_MODEL = 2048
BATCH = 16
SEQ = 256
DEC_BATCH = 2
DEC_SEQ = 2048
PAST_LEN = 512
GRID_W = 64
GRID_H = DEC_SEQ // GRID_W
SSM_WIDTH = D_MODEL // 2
SSM_GROUP = 16
SSM_GROUPS = SSM_WIDTH // SSM_GROUP
SSM_STATE = 64
N_HEADS = 16
HEAD_DIM = 64
ATT_WIDTH = N_HEADS * HEAD_DIM
WIN_ROWS = 8
WIN_COLS = 16
EPS = 1e-6
NEG_INF = -1e30

N_CTX_TOK = BATCH * SEQ
N_DEC_TOK = DEC_BATCH * DEC_SEQ
N_TOK = N_CTX_TOK + N_DEC_TOK

OFF_U = 0
OFF_ZS = SSM_WIDTH
OFF_Q = 2 * SSM_WIDTH
OFF_K = OFF_Q + ATT_WIDTH
OFF_V = OFF_K + ATT_WIDTH
OFF_ZA = OFF_V + ATT_WIDTH
OFF_GS = OFF_ZA + ATT_WIDTH
OFF_GA = OFF_GS + D_MODEL

CHUNK = 16
FLAT = CHUNK * SSM_GROUP
N_SEG = 16
CH_PER_SEG = 16
DSTATE = 2 * SSM_STATE

Q_ROWS_PER_BLOCK = 4
NBR_Q = Q_ROWS_PER_BLOCK * GRID_W
NBR_KROWS = 12
NBR_K = NBR_KROWS * GRID_W

VMEM_LIMIT = 56 * 1024 * 1024

F32 = jnp.float32
BF16 = jnp.bfloat16
HIGHEST = lax.Precision.HIGHEST


def _sigmoid(x):
    return 1.0 / (1.0 + jnp.exp(-x))


def _silu(x):
    return x * _sigmoid(x)


def _gelu_tanh(x):
    return 0.5 * x * (1.0 + jnp.tanh(math.sqrt(2.0 / math.pi) * (x + 0.044715 * (x * x * x))))


def _cmul(ar, ai, br, bi):
    return ar * br - ai * bi, ar * bi + ai * br


def _dot_nt(a, b):
    return lax.dot_general(a, b, (((1,), (1,)), ((), ())), preferred_element_type=F32)


def _params(*sem):
    return pltpu.CompilerParams(dimension_semantics=sem, vmem_limit_bytes=VMEM_LIMIT)


def _mod_kernel(cond_ref, w_ref, b_ref, o_ref):
    c = cond_ref[...]
    s = _silu(c).astype(BF16)
    o_ref[...] = jnp.dot(s, w_ref[...].astype(BF16), preferred_element_type=F32) + b_ref[...]


def _modulation(cond8, w_ada, b_ada):
    tn = 768
    n = w_ada.shape[1]
    return pl.pallas_call(
        _mod_kernel,
        grid=(n // tn,),
        in_specs=[pl.BlockSpec((8, D_MODEL), lambda j: (0, 0)),
                  pl.BlockSpec((D_MODEL, tn), lambda j: (0, j)),
                  pl.BlockSpec((1, tn), lambda j: (0, j))],
        out_specs=pl.BlockSpec((8, tn), lambda j: (0, j)),
        out_shape=jax.ShapeDtypeStruct((8, n), F32),
        compiler_params=_params("arbitrary"),
        name="modulation",
    )(cond8, w_ada, b_ada.reshape(1, n))


NORM_TM = 512
NORM_CTX_STEPS = N_CTX_TOK // NORM_TM
NORM_STEPS_PER_DEC_SEQ = DEC_SEQ // NORM_TM


def _mod_row(i, ctx_steps, steps_per_seq):
    return jnp.where(i < ctx_steps, 0, 1 + (i - ctx_steps) // steps_per_seq)


def _norm_kernel(xp_ref, xs_ref, shift_ref, scale_ref, nw_ref, o_ref):
    i = pl.program_id(0)

    def body(x):
        ms = jnp.mean(x * x, axis=-1, keepdims=True)
        y = x * lax.rsqrt(ms + EPS) * nw_ref[...]
        o_ref[...] = (y * (1.0 + scale_ref[0]) + shift_ref[0]).astype(BF16)

    @pl.when(i < NORM_CTX_STEPS)
    def _():
        body(xp_ref[...])

    @pl.when(i >= NORM_CTX_STEPS)
    def _():
        body(xs_ref[...])


def _norm_modulate(xp, xs, shift3, scale3, norm_w):
    steps = N_TOK // NORM_TM
    row = functools.partial(_mod_row, ctx_steps=NORM_CTX_STEPS, steps_per_seq=NORM_STEPS_PER_DEC_SEQ)
    return pl.pallas_call(
        _norm_kernel,
        grid=(steps,),
        in_specs=[pl.BlockSpec((NORM_TM, D_MODEL), lambda i: (jnp.minimum(i, NORM_CTX_STEPS - 1), 0)),
                  pl.BlockSpec((NORM_TM, D_MODEL), lambda i: (jnp.maximum(i - NORM_CTX_STEPS, 0), 0)),
                  pl.BlockSpec((1, 1, D_MODEL), lambda i: (row(i), 0, 0)),
                  pl.BlockSpec((1, 1, D_MODEL), lambda i: (row(i), 0, 0)),
                  pl.BlockSpec((1, D_MODEL), lambda i: (0, 0))],
        out_specs=pl.BlockSpec((NORM_TM, D_MODEL), lambda i: (i, 0)),
        out_shape=jax.ShapeDtypeStruct((N_TOK, D_MODEL), BF16),
        compiler_params=_params("arbitrary"),
        name="norm_modulate",
    )(xp, xs, shift3, scale3, norm_w.reshape(1, D_MODEL))


PROJ_TM = 1024
PROJ_TN = 1024
PROJ_CTX_STEPS = N_CTX_TOK // PROJ_TM
HEADS_PER_MXU_TILE = 256 // HEAD_DIM


def _head_group_ones():
    r = lax.broadcasted_iota(jnp.int32, (256, 256), 0) // HEAD_DIM
    c = lax.broadcasted_iota(jnp.int32, (256, 256), 1) // HEAD_DIM
    return jnp.where(r == c, 1.0, 0.0).astype(BF16)


def _head_rms(acc, nw):
    ones = _head_group_ones()
    outs = []
    for c in range(acc.shape[1] // 256):
        a = acc[:, c * 256:(c + 1) * 256]
        ssum = jnp.dot((a * a).astype(BF16), ones, preferred_element_type=F32)
        outs.append(a * lax.rsqrt(ssum * (1.0 / HEAD_DIM) + EPS))
    return jnp.concatenate(outs, axis=1) * nw


def _proj_kernel(*refs, mode):
    if mode == "plain":
        h_ref, w_ref, o_ref, wbf = refs
    elif mode == "q":
        h_ref, w_ref, nw_ref, o_ref, wbf = refs
    elif mode == "k":
        h_ref, w_ref, nw_ref, o_ref, of_ref, wbf = refs
    else:
        h_ref, w_ref, o_ref, of_ref, wbf = refs
    i = pl.program_id(1)

    @pl.when(i == 0)
    def _():
        wbf[...] = w_ref[...].astype(BF16)

    acc = jnp.dot(h_ref[...], wbf[...], preferred_element_type=F32)
    if mode == "q":
        acc = _head_rms(acc, nw_ref[...]) * (HEAD_DIM ** -0.5)
    elif mode == "k":
        acc = _head_rms(acc, nw_ref[...])
    o_ref[...] = acc.astype(BF16)
    if mode in ("k", "v"):
        @pl.when(i < PROJ_CTX_STEPS)
        def _():
            of_ref[...] = acc


def _in_proj(h, w_in, col_off, ncols, mode, norm_w=None):
    nj = ncols // PROJ_TN
    ni = N_TOK // PROJ_TM
    joff = col_off // PROJ_TN
    in_specs = [pl.BlockSpec((PROJ_TM, D_MODEL), lambda j, i: (i, 0)),
                pl.BlockSpec((D_MODEL, PROJ_TN), lambda j, i: (0, joff + j))]
    args = [h, w_in]
    if mode in ("q", "k"):
        in_specs.append(pl.BlockSpec((1, PROJ_TN), lambda j, i: (0, 0)))
        args.append(jnp.tile(norm_w.reshape(1, HEAD_DIM), (1, PROJ_TN // HEAD_DIM)))
    out_specs = [pl.BlockSpec((PROJ_TM, PROJ_TN), lambda j, i: (i, j))]
    out_shape = [jax.ShapeDtypeStruct((N_TOK, ncols), BF16)]
    if mode in ("k", "v"):
        out_specs.append(pl.BlockSpec((PROJ_TM, PROJ_TN),
                                      lambda j, i: (jnp.minimum(i, PROJ_CTX_STEPS - 1), j)))
        out_shape.append(jax.ShapeDtypeStruct((N_CTX_TOK, ncols), F32))
    res = pl.pallas_call(
        functools.partial(_proj_kernel, mode=mode),
        grid=(nj, ni),
        in_specs=in_specs,
        out_specs=out_specs,
        out_shape=out_shape,
        scratch_shapes=[pltpu.VMEM((D_MODEL, PROJ_TN), BF16)],
        compiler_params=_params("arbitrary", "arbitrary"),
        name="in_proj_" + mode,
    )(*args)
    return res if mode in ("k", "v") else res[0]


SSM_GB = 4


def _pow_select(e, pows):
    rr = jnp.where((e & 1) != 0, pows[0][0], 1.0)
    ri = jnp.where((e & 1) != 0, pows[0][1], 0.0)
    for b in range(1, len(pows)):
        bit = (e & (1 << b)) != 0
        fr = jnp.where(bit, pows[b][0], 1.0)
        fi = jnp.where(bit, pows[b][1], 0.0)
        rr, ri = _cmul(rr, ri, fr, fi)
    return rr, ri


def _discretize(a_re, a_im, log_dt):
    lr = jnp.minimum(a_re, -1e-4)
    li = a_im
    dt = jnp.exp(log_dt)
    mag = jnp.exp(lr * dt)
    br = mag * jnp.cos(li * dt)
    bi = mag * jnp.sin(li * dt)
    den = lr * lr + li * li
    nr = br - 1.0
    cr = (nr * lr + bi * li) / den
    ci = (bi * lr - nr * li) / den
    return (br, bi), (cr, ci)


def _squarings(pr, pi, n):
    out = [(pr, pi)]
    for _ in range(n):
        pr, pi = _cmul(pr, pi, pr, pi)
        out.append((pr, pi))
    return out


def _ssm_ops_kernel(rowp_ref, colp_ref, btr_ref, bti_ref, ctr_ref, cti_ref,
                    t_ref, s_ref, r_ref, lam_ref, powc_ref):
    lane_s = lax.broadcasted_iota(jnp.int32, (FLAT, DSTATE), 1)
    row_s = lax.broadcasted_iota(jnp.int32, (FLAT, DSTATE), 0)
    exp_s = jnp.where(lane_s < SSM_STATE, (CHUNK - 1) - (row_s >> 4), row_s >> 4)
    row_k = lax.broadcasted_iota(jnp.int32, (DSTATE, FLAT), 0)
    lane_k = lax.broadcasted_iota(jnp.int32, (DSTATE, FLAT), 1)
    exp_k = jnp.where(row_k < SSM_STATE, lane_k >> 4, (CHUNK - 1) - (lane_k >> 4))
    tile_e = jnp.where((lax.broadcasted_iota(jnp.int32, (SSM_GROUP, FLAT), 1) & (SSM_GROUP - 1))
                       == lax.broadcasted_iota(jnp.int32, (SSM_GROUP, FLAT), 0), 1.0, 0.0)
    lane_b = lax.broadcasted_iota(jnp.int32, (SSM_GROUP, DSTATE), 1)
    lane_t = lax.broadcasted_iota(jnp.int32, (SSM_GROUP, FLAT), 1)
    row_c = lax.broadcasted_iota(jnp.int32, (CH_PER_SEG, DSTATE), 0)
    lane_c = lax.broadcasted_iota(jnp.int32, (CH_PER_SEG, DSTATE), 1)
    exp_c = jnp.where(lane_c < SSM_STATE, row_c, (CH_PER_SEG - 1) - row_c)

    for g in range(SSM_GB):
        rp = rowp_ref[g]
        (lbr, lbi), (cfr, cfi) = _discretize(rp[0:1], rp[1:2], rp[2:3])
        sq = _squarings(lbr, lbi, 8)
        bbr, bbi = _cmul(btr_ref[g], bti_ref[g], cfr, cfi)

        pr, pi = _pow_select(exp_s, sq[:4])
        tb_r = jnp.concatenate([bbr] * CHUNK, axis=0)
        tb_i = jnp.concatenate([bbi] * CHUNK, axis=0)
        s_re, s_im = _cmul(tb_r, tb_i, pr, pi)
        s_ref[g] = jnp.concatenate([s_re, s_im], axis=1).astype(BF16)

        cp = colp_ref[g]
        (kbr, kbi), _ = _discretize(cp[:, 0:1], cp[:, 1:2], cp[:, 2:3])
        sqc = _squarings(kbr, kbi, 3)
        ctr = jnp.dot(ctr_ref[g], tile_e, preferred_element_type=F32, precision=HIGHEST)
        cti = jnp.dot(cti_ref[g], tile_e, preferred_element_type=F32, precision=HIGHEST)
        pkr, pki = _pow_select(exp_k, sqc)
        ykr, yki = _cmul(ctr, cti, pkr, pki)
        yrr, yri = _cmul(ykr, yki, kbr, kbi)
        r_ref[g] = jnp.concatenate([yrr, -yri], axis=0).astype(BF16)

        fwd = lane_b < SSM_STATE
        lhs = jnp.concatenate([jnp.where(fwd, bbr, 0.0), jnp.where(fwd, bbi, 0.0),
                               jnp.where(fwd, 0.0, bbr), jnp.where(fwd, 0.0, bbi)], axis=0)
        p1 = jnp.dot(lhs, ykr, preferred_element_type=F32, precision=HIGHEST)
        p2 = jnp.dot(lhs, yki, preferred_element_type=F32, precision=HIGHEST)
        kf = p1[0:16] - p2[16:32]
        kb = p1[32:48] - p2[48:64]
        for sp in range(CHUNK):
            tf = kf if sp == 0 else pltpu.roll(kf, SSM_GROUP * sp, axis=1)
            tf = jnp.where(lane_t >= SSM_GROUP * sp, tf, 0.0)
            shift = (FLAT - SSM_GROUP * (CHUNK - 1 - sp)) % FLAT
            tb = kb if shift == 0 else pltpu.roll(kb, shift, axis=1)
            tb = jnp.where(lane_t < SSM_GROUP * (sp + 1), tb, 0.0)
            t_ref[g, sp * SSM_GROUP:(sp + 1) * SSM_GROUP, :] = (tf + tb).astype(BF16)

        l16 = sq[4]
        l256 = sq[8]
        lam_ref[g] = jnp.concatenate([l16[0], l16[1], l256[0], l256[1],
                                      jnp.zeros((4, DSTATE), F32)], axis=0)
        q16 = _squarings(l16[0], l16[1], 3)
        pcr, pci = _pow_select(exp_c, q16)
        powc_ref[g, 0] = pcr
        powc_ref[g, 1] = pci


def _ssm_operators(rowp, colp, btr, bti, ctr, cti):
    G = SSM_GROUPS
    gb = SSM_GB
    return pl.pallas_call(
        _ssm_ops_kernel,
        grid=(G // gb,),
        in_specs=[pl.BlockSpec((gb, 3, DSTATE), lambda g: (g, 0, 0)),
                  pl.BlockSpec((gb, DSTATE, 3), lambda g: (g, 0, 0)),
                  pl.BlockSpec((gb, SSM_GROUP, DSTATE), lambda g: (g, 0, 0)),
                  pl.BlockSpec((gb, SSM_GROUP, DSTATE), lambda g: (g, 0, 0)),
                  pl.BlockSpec((gb, DSTATE, SSM_GROUP), lambda g: (g, 0, 0)),
                  pl.BlockSpec((gb, DSTATE, SSM_GROUP), lambda g: (g, 0, 0))],
        out_specs=[pl.BlockSpec((gb, FLAT, FLAT), lambda g: (g, 0, 0)),
                   pl.BlockSpec((gb, FLAT, FLAT), lambda g: (g, 0, 0)),
                   pl.BlockSpec((gb, FLAT, FLAT), lambda g: (g, 0, 0)),
                   pl.BlockSpec((gb, 8, DSTATE), lambda g: (g, 0, 0)),
                   pl.BlockSpec((gb, 2, CH_PER_SEG, DSTATE), lambda g: (g, 0, 0, 0))],
        out_shape=[jax.ShapeDtypeStruct((G, FLAT, FLAT), BF16),
                   jax.ShapeDtypeStruct((G, FLAT, FLAT), BF16),
                   jax.ShapeDtypeStruct((G, FLAT, FLAT), BF16),
                   jax.ShapeDtypeStruct((G, 8, DSTATE), F32),
                   jax.ShapeDtypeStruct((G, 2, CH_PER_SEG, DSTATE), F32)],
        compiler_params=_params("arbitrary"),
        name="ssm_operators",
    )(rowp, colp, btr, bti, ctr, cti)


SSM_PATHS = ((BATCH, SEQ // (CHUNK * CH_PER_SEG)), (DEC_BATCH, DEC_SEQ // (CHUNK * CH_PER_SEG)))


def _ssm_path(u, s_op, t_op, r_op, lam, powc_ref, dskip, h0r, h0i, nseg):
    lane = lax.broadcasted_iota(jnp.int32, (N_SEG, DSTATE), 1)
    isf = lane < SSM_STATE
    l16r, l16i, l256r, l256i = lam[0:1], lam[1:2], lam[2:3], lam[3:4]

    z = jnp.dot(u, s_op, preferred_element_type=F32)
    zre, zim = z[:, :DSTATE], z[:, DSTATE:]

    hr = jnp.zeros((N_SEG, DSTATE), F32)
    hi = jnp.zeros((N_SEG, DSTATE), F32)
    hist = []
    for k in range(CH_PER_SEG):
        hist.append((hr, hi))
        kb = CH_PER_SEG - 1 - k
        zr = jnp.where(isf, zre[k * N_SEG:(k + 1) * N_SEG], zre[kb * N_SEG:(kb + 1) * N_SEG])
        zi = jnp.where(isf, zim[k * N_SEG:(k + 1) * N_SEG], zim[kb * N_SEG:(kb + 1) * N_SEG])
        nr, ni = _cmul(l16r, l16i, hr, hi)
        hr, hi = nr + zr, ni + zi

    if nseg > 1:
        h0r = jnp.broadcast_to(h0r[:, None, :], (N_SEG // nseg, nseg, DSTATE)).reshape(N_SEG, DSTATE)
        h0i = jnp.broadcast_to(h0i[:, None, :], (N_SEG // nseg, nseg, DSTATE)).reshape(N_SEG, DSTATE)
    seg = lax.broadcasted_iota(jnp.int32, (N_SEG, DSTATE), 0) & (nseg - 1)
    segpow = _squarings(l256r, l256i, max(int(math.log2(nseg)), 0))

    def shifted(x, d):
        dn = jnp.where(seg >= d, pltpu.roll(x, d, axis=0), 0.0)
        up = jnp.where(seg <= nseg - 1 - d, pltpu.roll(x, N_SEG - d, axis=0), 0.0)
        return jnp.where(isf, dn, up)

    pr, pi = hr, hi
    d = 1
    lvl = 0
    while d < nseg:
        ar, ai = _cmul(segpow[lvl][0], segpow[lvl][1], shifted(pr, d), shifted(pi, d))
        pr, pi = pr + ar, pi + ai
        d *= 2
        lvl += 1
    if nseg > 1:
        e_in = jnp.where(isf, seg, nseg - 1 - seg)
        wr, wi = _pow_select(e_in, segpow[:lvl])
        ar, ai = _cmul(wr, wi, h0r, h0i)
        hsr, hsi = shifted(pr, 1) + ar, shifted(pi, 1) + ai
    else:
        hsr, hsi = h0r, h0i
    er, ei = _cmul(segpow[lvl][0], segpow[lvl][1], h0r, h0i)
    fin = jnp.concatenate([pr + er, pi + ei], axis=1)

    rows = []
    for c in range(CH_PER_SEG):
        cb = CH_PER_SEG - 1 - c
        lr = jnp.where(isf, hist[c][0], hist[cb][0])
        li = jnp.where(isf, hist[c][1], hist[cb][1])
        ar, ai = _cmul(powc_ref[0, c:c + 1, :], powc_ref[1, c:c + 1, :], hsr, hsi)
        rows.append(jnp.concatenate([lr + ar, li + ai], axis=1))
    hent = jnp.concatenate(rows, axis=0).astype(BF16)

    y = (jnp.dot(u, t_op, preferred_element_type=F32)
         + jnp.dot(hent, r_op, preferred_element_type=F32)
         + u.astype(F32) * dskip)
    return y, fin


def _ssm_kernel(u_ref, t_ref, s_ref, r_ref, lam_ref, powc_ref, d_ref,
                h0cr_ref, h0ci_ref, h0dr_ref, h0di_ref, y_ref, fin_ref):
    h0 = ((h0cr_ref, h0ci_ref), (h0dr_ref, h0di_ref))
    for path, (_, nseg) in enumerate(SSM_PATHS):
        y, fin = _ssm_path(u_ref[path, 0], s_ref[0], t_ref[0], r_ref[0], lam_ref[0], powc_ref.at[0],
                           d_ref[0], h0[path][0][0], h0[path][1][0], nseg)
        y_ref[path, 0] = y
        fin_ref[path, 0] = fin


def _ssm_scan(u_flat, t_op, s_op, r_op, lam, powc, dtile, h0c_re, h0c_im, h0d_re, h0d_im):
    G = SSM_GROUPS
    rows = N_SEG * CH_PER_SEG
    return pl.pallas_call(
        _ssm_kernel,
        grid=(G,),
        in_specs=[pl.BlockSpec((2, 1, rows, FLAT), lambda g: (0, g, 0, 0)),
                  pl.BlockSpec((1, FLAT, FLAT), lambda g: (g, 0, 0)),
                  pl.BlockSpec((1, FLAT, FLAT), lambda g: (g, 0, 0)),
                  pl.BlockSpec((1, FLAT, FLAT), lambda g: (g, 0, 0)),
                  pl.BlockSpec((1, 8, DSTATE), lambda g: (g, 0, 0)),
                  pl.BlockSpec((1, 2, CH_PER_SEG, DSTATE), lambda g: (g, 0, 0, 0)),
                  pl.BlockSpec((1, 1, FLAT), lambda g: (g, 0, 0)),
                  pl.BlockSpec((1, BATCH, DSTATE), lambda g: (g, 0, 0)),
                  pl.BlockSpec((1, BATCH, DSTATE), lambda g: (g, 0, 0)),
                  pl.BlockSpec((1, DEC_BATCH, DSTATE), lambda g: (g, 0, 0)),
                  pl.BlockSpec((1, DEC_BATCH, DSTATE), lambda g: (g, 0, 0))],
        out_specs=[pl.BlockSpec((2, 1, rows, FLAT), lambda g: (0, g, 0, 0)),
                   pl.BlockSpec((2, 1, N_SEG, 2 * DSTATE), lambda g: (0, g, 0, 0))],
        out_shape=[jax.ShapeDtypeStruct((2, G, rows, FLAT), F32),
                   jax.ShapeDtypeStruct((2, G, N_SEG, 2 * DSTATE), F32)],
        compiler_params=_params("arbitrary"),
        name="ssm_scan",
    )(u_flat, t_op, s_op, r_op, lam, powc, dtile, h0c_re, h0c_im, h0d_re, h0d_im)


CTX_BB = 4


def _softmax_pv(scores, values):
    m = scores[0].max(axis=-1, keepdims=True)
    for s in scores[1:]:
        m = jnp.maximum(m, s.max(axis=-1, keepdims=True))
    l = None
    o = None
    for s, v in zip(scores, values):
        p = jnp.exp(s - m)
        ls = p.sum(axis=-1, keepdims=True)
        os_ = jnp.dot(p.astype(BF16), v, preferred_element_type=F32)
        l = ls if l is None else l + ls
        o = os_ if o is None else o + os_
    return o / l


def _ctx_attn_kernel(q_ref, k_ref, v_ref, za_ref, o_ref):
    head0 = lax.broadcasted_iota(jnp.int32, (1, 2 * HEAD_DIM), 1) < HEAD_DIM
    for b in range(CTX_BB):
        sl = slice(b * SEQ, (b + 1) * SEQ)
        q, k, v = q_ref[sl, :], k_ref[sl, :], v_ref[sl, :]
        outs = []
        for hh in range(2):
            qh = jnp.where(head0 if hh == 0 else jnp.logical_not(head0), q, jnp.zeros_like(q))
            outs.append(_softmax_pv([_dot_nt(qh, k)], [v]))
        o = jnp.where(head0, outs[0], outs[1])
        o_ref[sl, :] = (o * _silu(za_ref[sl, :].astype(F32))).astype(BF16)


def _ctx_attention(q, k, v, za):
    rows = CTX_BB * SEQ
    spec = pl.BlockSpec((rows, 2 * HEAD_DIM), lambda b, hp: (b, hp))
    return pl.pallas_call(
        _ctx_attn_kernel,
        grid=(BATCH // CTX_BB, N_HEADS // 2),
        in_specs=[spec, spec, spec, spec],
        out_specs=spec,
        out_shape=jax.ShapeDtypeStruct((N_CTX_TOK, ATT_WIDTH), BF16),
        compiler_params=_params("arbitrary", "arbitrary"),
        name="ctx_attention",
    )(q, k, v, za)


N_DROW = 2 * WIN_ROWS - 1


def _nbr_bias(bias_ref, hh, a):
    half0 = lax.broadcasted_iota(jnp.int32, (1, 2 * GRID_W), 1) < GRID_W
    ks = jnp.clip(Q_ROWS_PER_BLOCK * a - Q_ROWS_PER_BLOCK, 0, GRID_H - NBR_KROWS)
    rows = []
    for ri in range(Q_ROWS_PER_BLOCK):
        r = Q_ROWS_PER_BLOCK * a + ri
        rs = jnp.clip(r - WIN_ROWS // 2, 0, GRID_H - WIN_ROWS)
        tiles = []
        for pair in range(NBR_KROWS // 2):
            idx = []
            for kri in (2 * pair, 2 * pair + 1):
                kr = ks + kri
                valid = jnp.logical_and(kr >= rs, kr < rs + WIN_ROWS)
                idx.append(jnp.where(valid, kr - r + WIN_ROWS - 1, N_DROW))
            tiles.append(jnp.where(half0, bias_ref[hh, idx[0]], bias_ref[hh, idx[1]]))
        rows.append(jnp.concatenate(tiles, axis=1))
    return jnp.concatenate(rows, axis=0)


def _nbr_attn_kernel(q_ref, k_ref, v_ref, kc_ref, vc_ref, bias_ref, za_ref, o_ref):
    a = pl.program_id(1)
    start = pl.multiple_of(jnp.clip(a * NBR_Q - NBR_Q, 0, DEC_SEQ - NBR_K), NBR_Q)
    head0 = lax.broadcasted_iota(jnp.int32, (1, 2 * HEAD_DIM), 1) < HEAD_DIM
    bias = [_nbr_bias(bias_ref, hh, a) for hh in range(2)]
    for b in range(DEC_BATCH):
        q = q_ref[b]
        kw = k_ref[b, pl.ds(start, NBR_K), :]
        vw = v_ref[b, pl.ds(start, NBR_K), :]
        kc = kc_ref[b].astype(BF16)
        vc = vc_ref[b].astype(BF16)
        outs = []
        for hh in range(2):
            qh = jnp.where(head0 if hh == 0 else jnp.logical_not(head0), q, jnp.zeros_like(q))
            sw = _dot_nt(qh, kw) + bias[hh]
            sc = _dot_nt(qh, kc)
            outs.append(_softmax_pv([sw, sc], [vw, vc]))
        o = jnp.where(head0, outs[0], outs[1])
        o_ref[b] = (o * _silu(za_ref[b].astype(F32))).astype(BF16)


def _nbr_attention(q4, k4, v4, kc, vc, bias, za4):
    first = N_CTX_TOK // DEC_SEQ // DEC_BATCH
    nblk = GRID_H // Q_ROWS_PER_BLOCK
    hw = 2 * HEAD_DIM
    qspec = pl.BlockSpec((DEC_BATCH, NBR_Q, hw), lambda hp, a: (first, a, hp))
    kspec = pl.BlockSpec((DEC_BATCH, DEC_SEQ, hw), lambda hp, a: (first, 0, hp))
    cspec = pl.BlockSpec((DEC_BATCH, PAST_LEN, hw), lambda hp, a: (0, 0, hp))
    bspec = pl.BlockSpec((2, N_DROW + 1, GRID_W, 2 * GRID_W), lambda hp, a: (hp, 0, 0, 0))
    return pl.pallas_call(
        _nbr_attn_kernel,
        grid=(N_HEADS // 2, nblk),
        in_specs=[qspec, kspec, kspec, cspec, cspec, bspec, qspec],
        out_specs=pl.BlockSpec((DEC_BATCH, NBR_Q, hw), lambda hp, a: (0, a, hp)),
        out_shape=jax.ShapeDtypeStruct((DEC_BATCH, DEC_SEQ, ATT_WIDTH), BF16),
        compiler_params=_params("arbitrary", "arbitrary"),
        name="nbr_attention",
    )(q4, k4, v4, kc, vc, bias, za4)


def _nbr_bias_tables(rpb):
    w = GRID_W
    ncol = 2 * WIN_COLS - 1
    left = w - WIN_COLS
    band = jnp.pad(rpb, ((0, 0), (0, 0), (left, 2 * w - left - ncol)), constant_values=NEG_INF)
    skew = jnp.tile(band, (1, 1, w))[:, :, :w * (2 * w - 1)].reshape(N_HEADS, N_DROW, w, 2 * w - 1)
    blocks = skew[..., w - 1:]
    qc = np.arange(w)[:, None]
    kc = np.arange(w)[None, :]
    cs = np.clip(qc - WIN_COLS // 2, 0, w - WIN_COLS)
    valid = (kc >= cs) & (kc < cs + WIN_COLS)
    blocks = jnp.where(valid, blocks, NEG_INF)
    blocks = jnp.concatenate([blocks, jnp.full((N_HEADS, 1, w, w), NEG_INF, F32)], axis=1)
    return jnp.concatenate([blocks, blocks], axis=-1)


BACK_TM = 256
BACK_CTX_STEPS = N_CTX_TOK // BACK_TM
BACK_STEPS_PER_DEC_SEQ = DEC_SEQ // BACK_TM


def _back_kernel(xp_ref, xs_ref, y_ref, zs_ref, ac_ref, ad_ref, gs_ref, ga_ref, gate_ref, bglu_ref,
                 wglu_ref, wso_ref, wao_ref, wo_ref, op_ref, os_ref):
    i = pl.program_id(0)

    def compute(x, a2):
        ys = _gelu_tanh(y_ref[...])
        t = jnp.dot(ys.astype(BF16), wglu_ref[...], preferred_element_type=F32) + bglu_ref[...]
        ys = ys * _sigmoid(t) * _silu(zs_ref[...].astype(F32))
        p_s = jnp.dot(ys.astype(BF16), wso_ref[...], preferred_element_type=F32)
        p_a = jnp.dot(a2, wao_ref[...], preferred_element_type=F32)
        merged = (_sigmoid(gs_ref[...].astype(F32)) * p_s + _sigmoid(ga_ref[...].astype(F32)) * p_a)
        return x + gate_ref[0] * jnp.dot(merged.astype(BF16), wo_ref[...], preferred_element_type=F32)

    @pl.when(i < BACK_CTX_STEPS)
    def _():
        op_ref[...] = compute(xp_ref[...], ac_ref[...])

    @pl.when(i >= BACK_CTX_STEPS)
    def _():
        os_ref[...] = compute(xs_ref[...], ad_ref[...])


def _back(xp, xs, y, zs, a_ctx, a_dec, gs, ga, gate3, b_glu, w_glu, w_so, w_ao, w_o):
    steps = N_TOK // BACK_TM
    n0 = BACK_CTX_STEPS
    row = functools.partial(_mod_row, ctx_steps=n0, steps_per_seq=BACK_STEPS_PER_DEC_SEQ)
    lo = lambda i: (jnp.minimum(i, n0 - 1), 0)
    hi = lambda i: (jnp.maximum(i - n0, 0), 0)
    cur = lambda i: (i, 0)
    const = lambda i: (0, 0)
    once = pl.Buffered(1)
    return pl.pallas_call(
        _back_kernel,
        grid=(steps,),
        in_specs=[pl.BlockSpec((BACK_TM, D_MODEL), lo),
                  pl.BlockSpec((BACK_TM, D_MODEL), hi),
                  pl.BlockSpec((BACK_TM, SSM_WIDTH), cur),
                  pl.BlockSpec((BACK_TM, SSM_WIDTH), cur),
                  pl.BlockSpec((BACK_TM, ATT_WIDTH), lo),
                  pl.BlockSpec((BACK_TM, ATT_WIDTH), hi),
                  pl.BlockSpec((BACK_TM, D_MODEL), cur),
                  pl.BlockSpec((BACK_TM, D_MODEL), cur),
                  pl.BlockSpec((1, 1, D_MODEL), lambda i: (row(i), 0, 0)),
                  pl.BlockSpec((1, SSM_WIDTH), const),
                  pl.BlockSpec((SSM_WIDTH, SSM_WIDTH), const, pipeline_mode=once),
                  pl.BlockSpec((SSM_WIDTH, D_MODEL), const, pipeline_mode=once),
                  pl.BlockSpec((ATT_WIDTH, D_MODEL), const, pipeline_mode=once),
                  pl.BlockSpec((D_MODEL, D_MODEL), const, pipeline_mode=once)],
        out_specs=[pl.BlockSpec((BACK_TM, D_MODEL), lo),
                   pl.BlockSpec((BACK_TM, D_MODEL), hi)],
        out_shape=[jax.ShapeDtypeStruct((N_CTX_TOK, D_MODEL), F32),
                   jax.ShapeDtypeStruct((N_DEC_TOK, D_MODEL), F32)],
        compiler_params=_params("arbitrary"),
        name="gated_output",
    )(xp, xs, y, zs, a_ctx, a_dec, gs, ga, gate3, b_glu.reshape(1, SSM_WIDTH), w_glu, w_so, w_ao, w_o)


def _layer(xp, xs, cache_k, cache_v, st_re, st_im, c, c_ctx, norm_w, w_ada, b_ada, w_in, q_norm_w, k_norm_w,
           rpb, a_re, a_im, log_dt, b_re, b_im, c_re, c_im, d, w_glu, b_glu, w_so, w_ao, w_o):
    G, P = SSM_GROUPS, SSM_STATE

    cond8 = jnp.zeros((8, D_MODEL), F32).at[0].set(c_ctx).at[1:1 + DEC_BATCH].set(c)
    mod = _modulation(cond8, w_ada, b_ada)
    shift3 = mod[:1 + DEC_BATCH, None, :D_MODEL]
    scale3 = mod[:1 + DEC_BATCH, None, D_MODEL:2 * D_MODEL]
    gate3 = mod[:1 + DEC_BATCH, None, 2 * D_MODEL:]

    h = _norm_modulate(xp, xs, shift3, scale3, norm_w)
    u = _in_proj(h, w_in, OFF_U, SSM_WIDTH, "plain")
    zs = _in_proj(h, w_in, OFF_ZS, SSM_WIDTH, "plain")
    q = _in_proj(h, w_in, OFF_Q, ATT_WIDTH, "q", q_norm_w)
    k, k_ctx32 = _in_proj(h, w_in, OFF_K, ATT_WIDTH, "k", k_norm_w)
    v, v_ctx32 = _in_proj(h, w_in, OFF_V, ATT_WIDTH, "v")
    za = _in_proj(h, w_in, OFF_ZA, ATT_WIDTH, "plain")
    gs = _in_proj(h, w_in, OFF_GS, D_MODEL, "plain")
    ga = _in_proj(h, w_in, OFF_GA, D_MODEL, "plain")

    arow = jnp.transpose(a_re, (1, 0, 2)).reshape(G, 2 * P)
    airow = jnp.transpose(a_im, (1, 0, 2)).reshape(G, 2 * P)
    dtrow = jnp.broadcast_to(jnp.transpose(log_dt, (1, 0))[:, :, None], (G, 2, P)).reshape(G, 2 * P)
    rowp = jnp.stack([arow, airow, dtrow], axis=1)
    colp = jnp.transpose(rowp, (0, 2, 1))
    btr = jnp.transpose(b_re, (1, 3, 0, 2)).reshape(G, SSM_GROUP, 2 * P)
    bti = jnp.transpose(b_im, (1, 3, 0, 2)).reshape(G, SSM_GROUP, 2 * P)
    ctr = jnp.transpose(c_re, (1, 0, 3, 2)).reshape(G, 2 * P, SSM_GROUP)
    cti = jnp.transpose(c_im, (1, 0, 3, 2)).reshape(G, 2 * P, SSM_GROUP)
    t_op, s_op, r_op, lam, powc = _ssm_operators(rowp, colp, btr, bti, ctr, cti)

    u_flat = (u.reshape(2, N_SEG, CH_PER_SEG, CHUNK, G, SSM_GROUP)
              .transpose(0, 4, 2, 1, 3, 5).reshape(2, G, CH_PER_SEG * N_SEG, FLAT))
    dtile = jnp.tile(d.reshape(G, 1, SSM_GROUP), (1, 1, CHUNK))
    h0c = jnp.zeros((G, BATCH, 2 * P), F32)
    h0d_re = jnp.transpose(st_re, (2, 0, 1, 3)).reshape(G, DEC_BATCH, 2 * P)
    h0d_im = jnp.transpose(st_im, (2, 0, 1, 3)).reshape(G, DEC_BATCH, 2 * P)
    y_flat, fin = _ssm_scan(u_flat, t_op, s_op, r_op, lam, powc, dtile, h0c, h0c, h0d_re, h0d_im)
    y = (y_flat.reshape(2, G, CH_PER_SEG, N_SEG, CHUNK, SSM_GROUP)
         .transpose(0, 3, 2, 4, 1, 5).reshape(N_TOK, SSM_WIDTH))

    a_ctx = _ctx_attention(q, k, v, za)
    nseq4 = N_TOK // DEC_SEQ
    a_dec = _nbr_attention(q.reshape(nseq4, DEC_SEQ, ATT_WIDTH), k.reshape(nseq4, DEC_SEQ, ATT_WIDTH),
                           v.reshape(nseq4, DEC_SEQ, ATT_WIDTH),
                           cache_k.reshape(DEC_BATCH, PAST_LEN, ATT_WIDTH),
                           cache_v.reshape(DEC_BATCH, PAST_LEN, ATT_WIDTH),
                           _nbr_bias_tables(rpb.astype(F32)),
                           za.reshape(nseq4, DEC_SEQ, ATT_WIDTH))

    yp, ys_out = _back(xp, xs, y, zs, a_ctx, a_dec.reshape(N_DEC_TOK, ATT_WIDTH), gs, ga, gate3, b_glu,
                       w_glu.astype(BF16), w_so.astype(BF16), w_ao.astype(BF16), w_o.astype(BF16))

    fin_ctx = fin[0]
    new_re = jnp.transpose(fin_ctx[:, :, :2 * P].reshape(G, BATCH, 2, P), (1, 2, 0, 3))
    new_im = jnp.transpose(fin_ctx[:, :, 2 * P:].reshape(G, BATCH, 2, P), (1, 2, 0, 3))
    return yp, ys_out, k_ctx32, v_ctx32, new_re, new_im


def kernel(x_prompt, x_sample, cache_k, cache_v, state_ssm_re, state_ssm_im, c, c_ctx, norm_w, w_ada, b_ada,
           w_in, q_norm_w, k_norm_w, rel_pos_bias, ssm_a_re, ssm_a_im, ssm_log_dt, ssm_b_re, ssm_b_im,
           ssm_c_re, ssm_c_im, ssm_d, w_glu, b_glu, w_ssm_out, w_att_out, w_o):
    depth = norm_w.shape[0]
    xp = x_prompt.reshape(N_CTX_TOK, D_MODEL)
    xs = x_sample.reshape(N_DEC_TOK, D_MODEL)
    new_k, new_v, new_re, new_im = [], [], [], []
    for l in range(depth):
        xp, xs, kl, vl, rl, il = _layer(
            xp, xs, cache_k[:, l], cache_v[:, l], state_ssm_re[:, l], state_ssm_im[:, l], c, c_ctx,
            norm_w[l], w_ada[l], b_ada[l], w_in[l], q_norm_w[l], k_norm_w[l], rel_pos_bias[l],
            ssm_a_re[l], ssm_a_im[l], ssm_log_dt[l], ssm_b_re[l], ssm_b_im[l], ssm_c_re[l], ssm_c_im[l],
            ssm_d[l], w_glu[l], b_glu[l], w_ssm_out[l], w_att_out[l], w_o[l])
        new_k.append(kl.reshape(BATCH, SEQ, N_HEADS, HEAD_DIM))
        new_v.append(vl.reshape(BATCH, SEQ, N_HEADS, HEAD_DIM))
        new_re.append(rl)
        new_im.append(il)
    return (xp.reshape(BATCH, SEQ, D_MODEL), xs.reshape(DEC_BATCH, DEC_SEQ, D_MODEL),
            jnp.stack(new_k, axis=1), jnp.stack(new_v, axis=1),
            jnp.stack(new_re, axis=1), jnp.stack(new_im, axis=1))
```

```python
import functools
import math

import numpy as np
import jax
import jax.numpy as jnp
from jax import lax
from jax.experimental import pallas as pl
from jax.experimental.pallas import tpu as pltpu

D_MODEL = 2048
BATCH = 16
SEQ = 256
DEC_BATCH = 2
DEC_SEQ = 2048
PAST_LEN = 512
GRID_W = 64
GRID_H = DEC_SEQ // GRID_W
SSM_WIDTH = D_MODEL // 2
SSM_GROUP = 16
SSM_GROUPS = SSM_WIDTH // SSM_GROUP
SSM_STATE = 64
N_HEADS = 16
HEAD_DIM = 64
ATT_WIDTH = N_HEADS * HEAD_DIM
WIN_ROWS = 8
WIN_COLS = 16
EPS = 1e-6
NEG_INF = -1e30

N_CTX_TOK = BATCH * SEQ
N_DEC_TOK = DEC_BATCH * DEC_SEQ
N_TOK = N_CTX_TOK + N_DEC_TOK

OFF_U = 0
OFF_ZS = SSM_WIDTH
OFF_Q = 2 * SSM_WIDTH
OFF_K = OFF_Q + ATT_WIDTH
OFF_V = OFF_K + ATT_WIDTH
OFF_ZA = OFF_V + ATT_WIDTH
OFF_GS = OFF_ZA + ATT_WIDTH
OFF_GA = OFF_GS + D_MODEL

CHUNK = 16
FLAT = CHUNK * SSM_GROUP
N_SEG = 16
CH_PER_SEG = 16
DSTATE = 2 * SSM_STATE

Q_ROWS_PER_BLOCK = 4
NBR_Q = Q_ROWS_PER_BLOCK * GRID_W
NBR_KROWS = 12
NBR_K = NBR_KROWS * GRID_W

VMEM_LIMIT = 56 * 1024 * 1024

F32 = jnp.float32
BF16 = jnp.bfloat16
HIGHEST = lax.Precision.HIGHEST


def _sigmoid(x):
    return 1.0 / (1.0 + jnp.exp(-x))


def _silu(x):
    return x * _sigmoid(x)


def _gelu_tanh(x):
    return 0.5 * x * (1.0 + jnp.tanh(math.sqrt(2.0 / math.pi) * (x + 0.044715 * (x * x * x))))


def _cmul(ar, ai, br, bi):
    return ar * br - ai * bi, ar * bi + ai * br


def _dot_nt(a, b):
    return lax.dot_general(a, b, (((1,), (1,)), ((), ())), preferred_element_type=F32)


def _params(*sem):
    return pltpu.CompilerParams(dimension_semantics=sem, vmem_limit_bytes=VMEM_LIMIT)


def _mod_kernel(cond_ref, w_ref, b_ref, o_ref):
    c = cond_ref[...]
    s = _silu(c).astype(BF16)
    o_ref[...] = jnp.dot(s, w_ref[...].astype(BF16), preferred_element_type=F32) + b_ref[...]


def _modulation(cond8, w_ada, b_ada):
    tn = 768
    n = w_ada.shape[1]
    return pl.pallas_call(
        _mod_kernel,
        grid=(n // tn,),
        in_specs=[pl.BlockSpec((8, D_MODEL), lambda j: (0, 0)),
                  pl.BlockSpec((D_MODEL, tn), lambda j: (0, j)),
                  pl.BlockSpec((1, tn), lambda j: (0, j))],
        out_specs=pl.BlockSpec((8, tn), lambda j: (0, j)),
        out_shape=jax.ShapeDtypeStruct((8, n), F32),
        compiler_params=_params("arbitrary"),
        name="modulation",
    )(cond8, w_ada, b_ada.reshape(1, n))


NORM_TM = 512
NORM_CTX_STEPS = N_CTX_TOK // NORM_TM
NORM_STEPS_PER_DEC_SEQ = DEC_SEQ // NORM_TM


def _mod_row(i, ctx_steps, steps_per_seq):
    return jnp.where(i < ctx_steps, 0, 1 + (i - ctx_steps) // steps_per_seq)


def _norm_kernel(xp_ref, xs_ref, shift_ref, scale_ref, nw_ref, o_ref):
    i = pl.program_id(0)

    def body(x):
        ms = jnp.mean(x * x, axis=-1, keepdims=True)
        y = x * lax.rsqrt(ms + EPS) * nw_ref[...]
        o_ref[...] = (y * (1.0 + scale_ref[0]) + shift_ref[0]).astype(BF16)

    @pl.when(i < NORM_CTX_STEPS)
    def _():
        body(xp_ref[...])

    @pl.when(i >= NORM_CTX_STEPS)
    def _():
        body(xs_ref[...])


def _norm_modulate(xp, xs, shift3, scale3, norm_w):
    steps = N_TOK // NORM_TM
    row = functools.partial(_mod_row, ctx_steps=NORM_CTX_STEPS, steps_per_seq=NORM_STEPS_PER_DEC_SEQ)
    return pl.pallas_call(
        _norm_kernel,
        grid=(steps,),
        in_specs=[pl.BlockSpec((NORM_TM, D_MODEL), lambda i: (jnp.minimum(i, NORM_CTX_STEPS - 1), 0)),
                  pl.BlockSpec((NORM_TM, D_MODEL), lambda i: (jnp.maximum(i - NORM_CTX_STEPS, 0), 0)),
                  pl.BlockSpec((1, 1, D_MODEL), lambda i: (row(i), 0, 0)),
                  pl.BlockSpec((1, 1, D_MODEL), lambda i: (row(i), 0, 0)),
                  pl.BlockSpec((1, D_MODEL), lambda i: (0, 0))],
        out_specs=pl.BlockSpec((NORM_TM, D_MODEL), lambda i: (i, 0)),
        out_shape=jax.ShapeDtypeStruct((N_TOK, D_MODEL), BF16),
        compiler_params=_params("arbitrary"),
        name="norm_modulate",
    )(xp, xs, shift3, scale3, norm_w.reshape(1, D_MODEL))


PROJ_TM = 1024
PROJ_TN = 1024
PROJ_CTX_STEPS = N_CTX_TOK // PROJ_TM
PROJ_SEGS = PROJ_TM // (CHUNK * CH_PER_SEG)
HEADS_PER_MXU_TILE = 256 // HEAD_DIM


def _head_group_ones():
    r = lax.broadcasted_iota(jnp.int32, (256, 256), 0) // HEAD_DIM
    c = lax.broadcasted_iota(jnp.int32, (256, 256), 1) // HEAD_DIM
    return jnp.where(r == c, 1.0, 0.0).astype(BF16)


def _head_rms(acc, nw):
    ones = _head_group_ones()
    outs = []
    for c in range(acc.shape[1] // 256):
        a = acc[:, c * 256:(c + 1) * 256]
        ssum = jnp.dot((a * a).astype(BF16), ones, preferred_element_type=F32)
        outs.append(a * lax.rsqrt(ssum * (1.0 / HEAD_DIM) + EPS))
    return jnp.concatenate(outs, axis=1) * nw


def _proj_kernel(*refs, mode):
    if mode in ("plain", "u"):
        h_ref, w_ref, o_ref, wbf = refs
    elif mode == "q":
        h_ref, w_ref, nw_ref, o_ref, wbf = refs
    elif mode == "k":
        h_ref, w_ref, nw_ref, o_ref, of_ref, wbf = refs
    else:
        h_ref, w_ref, o_ref, of_ref, wbf = refs
    i = pl.program_id(1)

    @pl.when(i == 0)
    def _():
        wbf[...] = w_ref[...].astype(BF16)

    acc = jnp.dot(h_ref[...], wbf[...], preferred_element_type=F32)
    if mode == "q":
        acc = _head_rms(acc, nw_ref[...]) * (HEAD_DIM ** -0.5)
    elif mode == "k":
        acc = _head_rms(acc, nw_ref[...])
    if mode == "u":
        ub = acc.astype(BF16)
        for seg in range(PROJ_SEGS):
            for ch in range(CH_PER_SEG):
                r0 = (seg * CH_PER_SEG + ch) * CHUNK
                o_ref[0, ch, seg] = ub[r0:r0 + CHUNK, :]
    else:
        o_ref[...] = acc.astype(BF16)
    if mode in ("k", "v"):
        @pl.when(i < PROJ_CTX_STEPS)
        def _():
            of_ref[...] = acc


def _in_proj(h, w_in, col_off, ncols, mode, norm_w=None):
    nj = ncols // PROJ_TN
    ni = N_TOK // PROJ_TM
    joff = col_off // PROJ_TN
    in_specs = [pl.BlockSpec((PROJ_TM, D_MODEL), lambda j, i: (i, 0)),
                pl.BlockSpec((D_MODEL, PROJ_TN), lambda j, i: (0, joff + j))]
    args = [h, w_in]
    if mode in ("q", "k"):
        in_specs.append(pl.BlockSpec((1, PROJ_TN), lambda j, i: (0, 0)))
        args.append(jnp.tile(norm_w.reshape(1, HEAD_DIM), (1, PROJ_TN // HEAD_DIM)))
    out_specs = [pl.BlockSpec((PROJ_TM, PROJ_TN), lambda j, i: (i, j))]
    out_shape = [jax.ShapeDtypeStruct((N_TOK, ncols), BF16)]
    if mode == "u":
        tiles_per_path = N_SEG // PROJ_SEGS
        out_specs = [pl.BlockSpec((1, CH_PER_SEG, PROJ_SEGS, CHUNK, PROJ_TN),
                                  lambda j, i: (i // tiles_per_path, 0, i % tiles_per_path, 0, j))]
        out_shape = [jax.ShapeDtypeStruct((2, CH_PER_SEG, N_SEG, CHUNK, ncols), BF16)]
    if mode in ("k", "v"):
        out_specs.append(pl.BlockSpec((PROJ_TM, PROJ_TN),
                                      lambda j, i: (jnp.minimum(i, PROJ_CTX_STEPS - 1), j)))
        out_shape.append(jax.ShapeDtypeStruct((N_CTX_TOK, ncols), F32))
    res = pl.pallas_call(
        functools.partial(_proj_kernel, mode=mode),
        grid=(nj, ni),
        in_specs=in_specs,
        out_specs=out_specs,
        out_shape=out_shape,
        scratch_shapes=[pltpu.VMEM((D_MODEL, PROJ_TN), BF16)],
        compiler_params=_params("arbitrary", "arbitrary"),
        name="in_proj_" + mode,
    )(*args)
    return res if mode in ("k", "v") else res[0]


SSM_GB = 4


def _pow_select(e, pows):
    rr = jnp.where((e & 1) != 0, pows[0][0], 1.0)
    ri = jnp.where((e & 1) != 0, pows[0][1], 0.0)
    for b in range(1, len(pows)):
        bit = (e & (1 << b)) != 0
        fr = jnp.where(bit, pows[b][0], 1.0)
        fi = jnp.where(bit, pows[b][1], 0.0)
        rr, ri = _cmul(rr, ri, fr, fi)
    return rr, ri


def _discretize(a_re, a_im, log_dt):
    lr = jnp.minimum(a_re, -1e-4)
    li = a_im
    dt = jnp.exp(log_dt)
    mag = jnp.exp(lr * dt)
    br = mag * jnp.cos(li * dt)
    bi = mag * jnp.sin(li * dt)
    den = lr * lr + li * li
    nr = br - 1.0
    cr = (nr * lr + bi * li) / den
    ci = (bi * lr - nr * li) / den
    return (br, bi), (cr, ci)


def _squarings(pr, pi, n):
    out = [(pr, pi)]
    for _ in range(n):
        pr, pi = _cmul(pr, pi, pr, pi)
        out.append((pr, pi))
    return out


def _ssm_ops_kernel(rowp_ref, colp_ref, btr_ref, bti_ref, ctr_ref, cti_ref,
                    t_ref, s_ref, r_ref, lam_ref, powc_ref):
    lane_s = lax.broadcasted_iota(jnp.int32, (FLAT, DSTATE), 1)
    row_s = lax.broadcasted_iota(jnp.int32, (FLAT, DSTATE), 0)
    exp_s = jnp.where(lane_s < SSM_STATE, (CHUNK - 1) - (row_s >> 4), row_s >> 4)
    row_k = lax.broadcasted_iota(jnp.int32, (DSTATE, FLAT), 0)
    lane_k = lax.broadcasted_iota(jnp.int32, (DSTATE, FLAT), 1)
    exp_k = jnp.where(row_k < SSM_STATE, lane_k >> 4, (CHUNK - 1) - (lane_k >> 4))
    tile_e = jnp.where((lax.broadcasted_iota(jnp.int32, (SSM_GROUP, FLAT), 1) & (SSM_GROUP - 1))
                       == lax.broadcasted_iota(jnp.int32, (SSM_GROUP, FLAT), 0), 1.0, 0.0)
    lane_b = lax.broadcasted_iota(jnp.int32, (SSM_GROUP, DSTATE), 1)
    lane_t = lax.broadcasted_iota(jnp.int32, (SSM_GROUP, FLAT), 1)
    row_c = lax.broadcasted_iota(jnp.int32, (CH_PER_SEG, DSTATE), 0)
    lane_c = lax.broadcasted_iota(jnp.int32, (CH_PER_SEG, DSTATE), 1)
    exp_c = jnp.where(lane_c < SSM_STATE, row_c, (CH_PER_SEG - 1) - row_c)

    for g in range(SSM_GB):
        rp = rowp_ref[g]
        (lbr, lbi), (cfr, cfi) = _discretize(rp[0:1], rp[1:2], rp[2:3])
        sq = _squarings(lbr, lbi, 8)
        bbr, bbi = _cmul(btr_ref[g], bti_ref[g], cfr, cfi)

        pr, pi = _pow_select(exp_s, sq[:4])
        tb_r = jnp.concatenate([bbr] * CHUNK, axis=0)
        tb_i = jnp.concatenate([bbi] * CHUNK, axis=0)
        s_re, s_im = _cmul(tb_r, tb_i, pr, pi)
        s_ref[g] = jnp.concatenate([s_re, s_im], axis=1).astype(BF16)

        cp = colp_ref[g]
        (kbr, kbi), _ = _discretize(cp[:, 0:1], cp[:, 1:2], cp[:, 2:3])
        sqc = _squarings(kbr, kbi, 3)
        ctr = jnp.dot(ctr_ref[g], tile_e, preferred_element_type=F32, precision=HIGHEST)
        cti = jnp.dot(cti_ref[g], tile_e, preferred_element_type=F32, precision=HIGHEST)
        pkr, pki = _pow_select(exp_k, sqc)
        ykr, yki = _cmul(ctr, cti, pkr, pki)
        yrr, yri = _cmul(ykr, yki, kbr, kbi)
        r_ref[g] = jnp.concatenate([yrr, -yri], axis=0).astype(BF16)

        fwd = lane_b < SSM_STATE
        lhs = jnp.concatenate([jnp.where(fwd, bbr, 0.0), jnp.where(fwd, bbi, 0.0),
                               jnp.where(fwd, 0.0, bbr), jnp.where(fwd, 0.0, bbi)], axis=0)
        p1 = jnp.dot(lhs, ykr, preferred_element_type=F32, precision=HIGHEST)
        p2 = jnp.dot(lhs, yki, preferred_element_type=F32, precision=HIGHEST)
        kf = p1[0:16] - p2[16:32]
        kb = p1[32:48] - p2[48:64]
        for sp in range(CHUNK):
            tf = kf if sp == 0 else pltpu.roll(kf, SSM_GROUP * sp, axis=1)
            tf = jnp.where(lane_t >= SSM_GROUP * sp, tf, 0.0)
            shift = (FLAT - SSM_GROUP * (CHUNK - 1 - sp)) % FLAT
            tb = kb if shift == 0 else pltpu.roll(kb, shift, axis=1)
            tb = jnp.where(lane_t < SSM_GROUP * (sp + 1), tb, 0.0)
            t_ref[g, sp * SSM_GROUP:(sp + 1) * SSM_GROUP, :] = (tf + tb).astype(BF16)

        l16 = sq[4]
        l256 = sq[8]
        lam_ref[g] = jnp.concatenate([l16[0], l16[1], l256[0], l256[1],
                                      jnp.zeros((4, DSTATE), F32)], axis=0)
        q16 = _squarings(l16[0], l16[1], 3)
        pcr, pci = _pow_select(exp_c, q16)
        powc_ref[g, 0] = pcr
        powc_ref[g, 1] = pci


def _ssm_operators(rowp, colp, btr, bti, ctr, cti):
    G = SSM_GROUPS
    gb = SSM_GB
    return pl.pallas_call(
        _ssm_ops_kernel,
        grid=(G // gb,),
        in_specs=[pl.BlockSpec((gb, 3, DSTATE), lambda g: (g, 0, 0)),
                  pl.BlockSpec((gb, DSTATE, 3), lambda g: (g, 0, 0)),
                  pl.BlockSpec((gb, SSM_GROUP, DSTATE), lambda g: (g, 0, 0)),
                  pl.BlockSpec((gb, SSM_GROUP, DSTATE), lambda g: (g, 0, 0)),
                  pl.BlockSpec((gb, DSTATE, SSM_GROUP), lambda g: (g, 0, 0)),
                  pl.BlockSpec((gb, DSTATE, SSM_GROUP), lambda g: (g, 0, 0))],
        out_specs=[pl.BlockSpec((gb, FLAT, FLAT), lambda g: (g, 0, 0)),
                   pl.BlockSpec((gb, FLAT, FLAT), lambda g: (g, 0, 0)),
                   pl.BlockSpec((gb, FLAT, FLAT), lambda g: (g, 0, 0)),
                   pl.BlockSpec((gb, 8, DSTATE), lambda g: (g, 0, 0)),
                   pl.BlockSpec((gb, 2, CH_PER_SEG, DSTATE), lambda g: (g, 0, 0, 0))],
        out_shape=[jax.ShapeDtypeStruct((G, FLAT, FLAT), BF16),
                   jax.ShapeDtypeStruct((G, FLAT, FLAT), BF16),
                   jax.ShapeDtypeStruct((G, FLAT, FLAT), BF16),
                   jax.ShapeDtypeStruct((G, 8, DSTATE), F32),
                   jax.ShapeDtypeStruct((G, 2, CH_PER_SEG, DSTATE), F32)],
        compiler_params=_params("arbitrary"),
        name="ssm_operators",
    )(rowp, colp, btr, bti, ctr, cti)


SSM_PATHS = ((BATCH, SEQ // (CHUNK * CH_PER_SEG)), (DEC_BATCH, DEC_SEQ // (CHUNK * CH_PER_SEG)))


def _ssm_path(u32, s_op, t_op, r_op, lam, powc_ref, dskip, h0r, h0i, nseg):
    u = u32.astype(BF16)
    lane = lax.broadcasted_iota(jnp.int32, (N_SEG, DSTATE), 1)
    isf = lane < SSM_STATE
    l16r, l16i, l256r, l256i = lam[0:1], lam[1:2], lam[2:3], lam[3:4]

    z = jnp.dot(u, s_op, preferred_element_type=F32)
    zre, zim = z[:, :DSTATE], z[:, DSTATE:]

    hr = jnp.zeros((N_SEG, DSTATE), F32)
    hi = jnp.zeros((N_SEG, DSTATE), F32)
    hist = []
    for k in range(CH_PER_SEG):
        hist.append((hr, hi))
        kb = CH_PER_SEG - 1 - k
        zr = jnp.where(isf, zre[k * N_SEG:(k + 1) * N_SEG], zre[kb * N_SEG:(kb + 1) * N_SEG])
        zi = jnp.where(isf, zim[k * N_SEG:(k + 1) * N_SEG], zim[kb * N_SEG:(kb + 1) * N_SEG])
        nr, ni = _cmul(l16r, l16i, hr, hi)
        hr, hi = nr + zr, ni + zi

    if nseg > 1:
        h0r = jnp.broadcast_to(h0r[:, None, :], (N_SEG // nseg, nseg, DSTATE)).reshape(N_SEG, DSTATE)
        h0i = jnp.broadcast_to(h0i[:, None, :], (N_SEG // nseg, nseg, DSTATE)).reshape(N_SEG, DSTATE)
    seg = lax.broadcasted_iota(jnp.int32, (N_SEG, DSTATE), 0) & (nseg - 1)
    segpow = _squarings(l256r, l256i, max(int(math.log2(nseg)), 0))

    def shifted(x, d):
        dn = jnp.where(seg >= d, pltpu.roll(x, d, axis=0), 0.0)
        up = jnp.where(seg <= nseg - 1 - d, pltpu.roll(x, N_SEG - d, axis=0), 0.0)
        return jnp.where(isf, dn, up)

    pr, pi = hr, hi
    d = 1
    lvl = 0
    while d < nseg:
        ar, ai = _cmul(segpow[lvl][0], segpow[lvl][1], shifted(pr, d), shifted(pi, d))
        pr, pi = pr + ar, pi + ai
        d *= 2
        lvl += 1
    if nseg > 1:
        e_in = jnp.where(isf, seg, nseg - 1 - seg)
        wr, wi = _pow_select(e_in, segpow[:lvl])
        ar, ai = _cmul(wr, wi, h0r, h0i)
        hsr, hsi = shifted(pr, 1) + ar, shifted(pi, 1) + ai
    else:
        hsr, hsi = h0r, h0i
    er, ei = _cmul(segpow[lvl][0], segpow[lvl][1], h0r, h0i)
    fin = jnp.concatenate([pr + er, pi + ei], axis=1)

    rows = []
    for c in range(CH_PER_SEG):
        cb = CH_PER_SEG - 1 - c
        lr = jnp.where(isf, hist[c][0], hist[cb][0])
        li = jnp.where(isf, hist[c][1], hist[cb][1])
        ar, ai = _cmul(powc_ref[0, c:c + 1, :], powc_ref[1, c:c + 1, :], hsr, hsi)
        rows.append(jnp.concatenate([lr + ar, li + ai], axis=1))
    hent = jnp.concatenate(rows, axis=0).astype(BF16)

    y = (jnp.dot(u, t_op, preferred_element_type=F32)
         + jnp.dot(hent, r_op, preferred_element_type=F32)
         + u32 * dskip)
    return y, fin


GROUPS_PER_STEP = 128 // SSM_GROUP


def _block_transpose(arrs, blk):
    n = len(arrs)
    width = arrs[0].shape[1]
    j = lax.broadcasted_iota(jnp.int32, arrs[0].shape, 1) // blk
    k = n // 2
    while k >= 1:
        bit = (j & k) != 0
        new = list(arrs)
        for x in range(n):
            if x & k == 0:
                a, b = arrs[x], arrs[x | k]
                new[x] = jnp.where(bit, pltpu.roll(b, k * blk, axis=1), a)
                new[x | k] = jnp.where(bit, b, pltpu.roll(a, width - k * blk, axis=1))
        arrs = new
        k //= 2
    return arrs


def _ssm_kernel(*refs):
    tok_refs = refs[:CHUNK]
    (t_ref, s_ref, r_ref, lam_ref, powc_ref, d_ref, h0cr_ref, h0ci_ref, h0dr_ref, h0di_ref,
     y_ref, fin_ref, ubuf, ybuf) = refs[CHUNK:]
    h0 = ((h0cr_ref, h0ci_ref), (h0dr_ref, h0di_ref))
    nhalf = FLAT // 128
    for path, (_, nseg) in enumerate(SSM_PATHS):
        for half in range(nhalf):
            toks = [tok_refs[half * GROUPS_PER_STEP + sb][path].astype(F32) for sb in range(GROUPS_PER_STEP)]
            grouped = _block_transpose(toks, SSM_GROUP)
            for g in range(GROUPS_PER_STEP):
                ubuf[g, :, half * 128:(half + 1) * 128] = grouped[g]

        def body(g, carry, path=path, nseg=nseg):
            y, fin = _ssm_path(ubuf[g], s_ref[g], t_ref[g], r_ref[g], lam_ref[g], powc_ref.at[g],
                               d_ref[g], h0[path][0][g], h0[path][1][g], nseg)
            ybuf[g] = y
            fin_ref[path, g] = fin
            return carry

        lax.fori_loop(0, GROUPS_PER_STEP, body, 0)

        for half in range(nhalf):
            grouped = [ybuf[g, :, half * 128:(half + 1) * 128] for g in range(GROUPS_PER_STEP)]
            toks = _block_transpose(grouped, SSM_GROUP)
            for sb in range(GROUPS_PER_STEP):
                tok = half * GROUPS_PER_STEP + sb
                for ch in range(CH_PER_SEG):
                    y_ref[path, ch, tok] = toks[sb][ch * N_SEG:(ch + 1) * N_SEG, :]


def _ssm_scan(u5, t_op, s_op, r_op, lam, powc, dtile, h0c_re, h0c_im, h0d_re, h0d_im):
    G = SSM_GROUPS
    gs = GROUPS_PER_STEP
    rows = N_SEG * CH_PER_SEG
    lane_blocks = SSM_WIDTH // 128
    u_rows = u5.reshape(2, rows, CHUNK * SSM_WIDTH)
    tok_specs = [pl.BlockSpec((2, rows, 128), lambda o, s=s: (0, 0, s * lane_blocks + o)) for s in range(CHUNK)]
    return pl.pallas_call(
        _ssm_kernel,
        grid=(G // gs,),
        in_specs=tok_specs + [
            pl.BlockSpec((gs, FLAT, FLAT), lambda o: (o, 0, 0)),
            pl.BlockSpec((gs, FLAT, FLAT), lambda o: (o, 0, 0)),
            pl.BlockSpec((gs, FLAT, FLAT), lambda o: (o, 0, 0)),
            pl.BlockSpec((gs, 8, DSTATE), lambda o: (o, 0, 0)),
            pl.BlockSpec((gs, 2, CH_PER_SEG, DSTATE), lambda o: (o, 0, 0, 0)),
            pl.BlockSpec((gs, 1, FLAT), lambda o: (o, 0, 0)),
            pl.BlockSpec((gs, BATCH, DSTATE), lambda o: (o, 0, 0)),
            pl.BlockSpec((gs, BATCH, DSTATE), lambda o: (o, 0, 0)),
            pl.BlockSpec((gs, DEC_BATCH, DSTATE), lambda o: (o, 0, 0)),
            pl.BlockSpec((gs, DEC_BATCH, DSTATE), lambda o: (o, 0, 0))],
        out_specs=[pl.BlockSpec((2, CH_PER_SEG, CHUNK, N_SEG, 128), lambda o: (0, 0, 0, 0, o)),
                   pl.BlockSpec((2, gs, N_SEG, 2 * DSTATE), lambda o: (0, o, 0, 0))],
        out_shape=[jax.ShapeDtypeStruct((2, CH_PER_SEG, CHUNK, N_SEG, SSM_WIDTH), F32),
                   jax.ShapeDtypeStruct((2, G, N_SEG, 2 * DSTATE), F32)],
        scratch_shapes=[pltpu.VMEM((gs, rows, FLAT), F32), pltpu.VMEM((gs, rows, FLAT), F32)],
        compiler_params=_params("arbitrary"),
        name="ssm_scan",
    )(*([u_rows] * CHUNK), t_op, s_op, r_op, lam, powc, dtile, h0c_re, h0c_im, h0d_re, h0d_im)


CTX_BB = 4


def _softmax_pv(scores, values):
    m = scores[0].max(axis=-1, keepdims=True)
    for s in scores[1:]:
        m = jnp.maximum(m, s.max(axis=-1, keepdims=True))
    l = None
    o = None
    for s, v in zip(scores, values):
        p = jnp.exp(s - m)
        ls = p.sum(axis=-1, keepdims=True)
        os_ = jnp.dot(p.astype(BF16), v, preferred_element_type=F32)
        l = ls if l is None else l + ls
        o = os_ if o is None else o + os_
    return o / l


def _ctx_attn_kernel(q_ref, k_ref, v_ref, za_ref, o_ref):
    head0 = lax.broadcasted_iota(jnp.int32, (1, 2 * HEAD_DIM), 1) < HEAD_DIM
    for b in range(CTX_BB):
        sl = slice(b * SEQ, (b + 1) * SEQ)
        q, k, v = q_ref[sl, :], k_ref[sl, :], v_ref[sl, :]
        outs = []
        for hh in range(2):
            qh = jnp.where(head0 if hh == 0 else jnp.logical_not(head0), q, jnp.zeros_like(q))
            outs.append(_softmax_pv([_dot_nt(qh, k)], [v]))
        o = jnp.where(head0, outs[0], outs[1])
        o_ref[sl, :] = (o * _silu(za_ref[sl, :].astype(F32))).astype(BF16)


def _ctx_attention(q, k, v, za):
    rows = CTX_BB * SEQ
    spec = pl.BlockSpec((rows, 2 * HEAD_DIM), lambda b, hp: (b, hp))
    return pl.pallas_call(
        _ctx_attn_kernel,
        grid=(BATCH // CTX_BB, N_HEADS // 2),
        in_specs=[spec, spec, spec, spec],
        out_specs=spec,
        out_shape=jax.ShapeDtypeStruct((N_CTX_TOK, ATT_WIDTH), BF16),
        compiler_params=_params("arbitrary", "arbitrary"),
        name="ctx_attention",
    )(q, k, v, za)


N_DROW = 2 * WIN_ROWS - 1


def _nbr_bias(bias_ref, hh, a):
    half0 = lax.broadcasted_iota(jnp.int32, (1, 2 * GRID_W), 1) < GRID_W
    ks = jnp.clip(Q_ROWS_PER_BLOCK * a - Q_ROWS_PER_BLOCK, 0, GRID_H - NBR_KROWS)
    rows = []
    for ri in range(Q_ROWS_PER_BLOCK):
        r = Q_ROWS_PER_BLOCK * a + ri
        rs = jnp.clip(r - WIN_ROWS // 2, 0, GRID_H - WIN_ROWS)
        tiles = []
        for pair in range(NBR_KROWS // 2):
            idx = []
            for kri in (2 * pair, 2 * pair + 1):
                kr = ks + kri
                valid = jnp.logical_and(kr >= rs, kr < rs + WIN_ROWS)
                idx.append(jnp.where(valid, kr - r + WIN_ROWS - 1, N_DROW))
            tiles.append(jnp.where(half0, bias_ref[hh, idx[0]], bias_ref[hh, idx[1]]))
        rows.append(jnp.concatenate(tiles, axis=1))
    return jnp.concatenate(rows, axis=0)


def _nbr_attn_kernel(q_ref, k_ref, v_ref, kc_ref, vc_ref, bias_ref, za_ref, o_ref):
    a = pl.program_id(1)
    start = pl.multiple_of(jnp.clip(a * NBR_Q - NBR_Q, 0, DEC_SEQ - NBR_K), NBR_Q)
    head0 = lax.broadcasted_iota(jnp.int32, (1, 2 * HEAD_DIM), 1) < HEAD_DIM
    bias = [_nbr_bias(bias_ref, hh, a) for hh in range(2)]
    for b in range(DEC_BATCH):
        q = q_ref[b]
        kw = k_ref[b, pl.ds(start, NBR_K), :]
        vw = v_ref[b, pl.ds(start, NBR_K), :]
        kc = kc_ref[b].astype(BF16)
        vc = vc_ref[b].astype(BF16)
        outs = []
        for hh in range(2):
            qh = jnp.where(head0 if hh == 0 else jnp.logical_not(head0), q, jnp.zeros_like(q))
            sw = _dot_nt(qh, kw) + bias[hh]
            sc = _dot_nt(qh, kc)
            outs.append(_softmax_pv([sw, sc], [vw, vc]))
        o = jnp.where(head0, outs[0], outs[1])
        o_ref[b] = (o * _silu(za_ref[b].astype(F32))).astype(BF16)


def _nbr_attention(q4, k4, v4, kc, vc, bias, za4):
    first = N_CTX_TOK // DEC_SEQ // DEC_BATCH
    nblk = GRID_H // Q_ROWS_PER_BLOCK
    hw = 2 * HEAD_DIM
    qspec = pl.BlockSpec((DEC_BATCH, NBR_Q, hw), lambda hp, a: (first, a, hp))
    kspec = pl.BlockSpec((DEC_BATCH, DEC_SEQ, hw), lambda hp, a: (first, 0, hp))
    cspec = pl.BlockSpec((DEC_BATCH, PAST_LEN, hw), lambda hp, a: (0, 0, hp))
    bspec = pl.BlockSpec((2, N_DROW + 1, GRID_W, 2 * GRID_W), lambda hp, a: (hp, 0, 0, 0))
    return pl.pallas_call(
        _nbr_attn_kernel,
        grid=(N_HEADS // 2, nblk),
        in_specs=[qspec, kspec, kspec, cspec, cspec, bspec, qspec],
        out_specs=pl.BlockSpec((DEC_BATCH, NBR_Q, hw), lambda hp, a: (0, a, hp)),
        out_shape=jax.ShapeDtypeStruct((DEC_BATCH, DEC_SEQ, ATT_WIDTH), BF16),
        compiler_params=_params("arbitrary", "arbitrary"),
        name="nbr_attention",
    )(q4, k4, v4, kc, vc, bias, za4)


def _nbr_bias_tables(rpb):
    w = GRID_W
    ncol = 2 * WIN_COLS - 1
    left = w - WIN_COLS
    band = jnp.pad(rpb, ((0, 0), (0, 0), (left, 2 * w - left - ncol)), constant_values=NEG_INF)
    skew = jnp.tile(band, (1, 1, w))[:, :, :w * (2 * w - 1)].reshape(N_HEADS, N_DROW, w, 2 * w - 1)
    blocks = skew[..., w - 1:]
    qc = np.arange(w)[:, None]
    kc = np.arange(w)[None, :]
    cs = np.clip(qc - WIN_COLS // 2, 0, w - WIN_COLS)
    valid = (kc >= cs) & (kc < cs + WIN_COLS)
    blocks = jnp.where(valid, blocks, NEG_INF)
    blocks = jnp.concatenate([blocks, jnp.full((N_HEADS, 1, w, w), NEG_INF, F32)], axis=1)
    return jnp.concatenate([blocks, blocks], axis=-1)


BACK_TM = CHUNK * CH_PER_SEG
BACK_CTX_STEPS = N_CTX_TOK // BACK_TM
BACK_STEPS_PER_DEC_SEQ = DEC_SEQ // BACK_TM


def _back_kernel(xp_ref, xs_ref, y_ref, zs_ref, ac_ref, ad_ref, gs_ref, ga_ref, gate_ref, bglu_ref,
                 wglu_ref, wso_ref, wao_ref, wo_ref, op_ref, os_ref):
    i = pl.program_id(0)

    def compute(x, a2):
        ys = _gelu_tanh(y_ref[0].reshape(BACK_TM, SSM_WIDTH))
        t = jnp.dot(ys.astype(BF16), wglu_ref[...], preferred_element_type=F32) + bglu_ref[...]
        ys = ys * _sigmoid(t) * _silu(zs_ref[...].astype(F32))
        p_s = jnp.dot(ys.astype(BF16), wso_ref[...], preferred_element_type=F32)
        p_a = jnp.dot(a2, wao_ref[...], preferred_element_type=F32)
        merged = (_sigmoid(gs_ref[...].astype(F32)) * p_s + _sigmoid(ga_ref[...].astype(F32)) * p_a)
        return x + gate_ref[0] * jnp.dot(merged.astype(BF16), wo_ref[...], preferred_element_type=F32)

    @pl.when(i < BACK_CTX_STEPS)
    def _():
        op_ref[...] = compute(xp_ref[...], ac_ref[...])

    @pl.when(i >= BACK_CTX_STEPS)
    def _():
        os_ref[...] = compute(xs_ref[...], ad_ref[...])


def _back(xp, xs, y, zs, a_ctx, a_dec, gs, ga, gate3, b_glu, w_glu, w_so, w_ao, w_o):
    steps = N_TOK // BACK_TM
    n0 = BACK_CTX_STEPS
    row = functools.partial(_mod_row, ctx_steps=n0, steps_per_seq=BACK_STEPS_PER_DEC_SEQ)
    lo = lambda i: (jnp.minimum(i, n0 - 1), 0)
    hi = lambda i: (jnp.maximum(i - n0, 0), 0)
    cur = lambda i: (i, 0)
    const = lambda i: (0, 0)
    once = pl.Buffered(1)
    return pl.pallas_call(
        _back_kernel,
        grid=(steps,),
        in_specs=[pl.BlockSpec((BACK_TM, D_MODEL), lo),
                  pl.BlockSpec((BACK_TM, D_MODEL), hi),
                  pl.BlockSpec((1, CH_PER_SEG, CHUNK, SSM_WIDTH), lambda i: (i // N_SEG, 0, 0, i % N_SEG)),
                  pl.BlockSpec((BACK_TM, SSM_WIDTH), cur),
                  pl.BlockSpec((BACK_TM, ATT_WIDTH), lo),
                  pl.BlockSpec((BACK_TM, ATT_WIDTH), hi),
                  pl.BlockSpec((BACK_TM, D_MODEL), cur),
                  pl.BlockSpec((BACK_TM, D_MODEL), cur),
                  pl.BlockSpec((1, 1, D_MODEL), lambda i: (row(i), 0, 0)),
                  pl.BlockSpec((1, SSM_WIDTH), const),
                  pl.BlockSpec((SSM_WIDTH, SSM_WIDTH), const, pipeline_mode=once),
                  pl.BlockSpec((SSM_WIDTH, D_MODEL), const, pipeline_mode=once),
                  pl.BlockSpec((ATT_WIDTH, D_MODEL), const, pipeline_mode=once),
                  pl.BlockSpec((D_MODEL, D_MODEL), const, pipeline_mode=once)],
        out_specs=[pl.BlockSpec((BACK_TM, D_MODEL), lo),
                   pl.BlockSpec((BACK_TM, D_MODEL), hi)],
        out_shape=[jax.ShapeDtypeStruct((N_CTX_TOK, D_MODEL), F32),
                   jax.ShapeDtypeStruct((N_DEC_TOK, D_MODEL), F32)],
        compiler_params=_params("arbitrary"),
        name="gated_output",
    )(xp, xs, y, zs, a_ctx, a_dec, gs, ga, gate3, b_glu.reshape(1, SSM_WIDTH), w_glu, w_so, w_ao, w_o)


def _layer(xp, xs, cache_k, cache_v, st_re, st_im, c, c_ctx, norm_w, w_ada, b_ada, w_in, q_norm_w, k_norm_w,
           rpb, a_re, a_im, log_dt, b_re, b_im, c_re, c_im, d, w_glu, b_glu, w_so, w_ao, w_o):
    G, P = SSM_GROUPS, SSM_STATE

    cond8 = jnp.zeros((8, D_MODEL), F32).at[0].set(c_ctx).at[1:1 + DEC_BATCH].set(c)
    mod = _modulation(cond8, w_ada, b_ada)
    shift3 = mod[:1 + DEC_BATCH, None, :D_MODEL]
    scale3 = mod[:1 + DEC_BATCH, None, D_MODEL:2 * D_MODEL]
    gate3 = mod[:1 + DEC_BATCH, None, 2 * D_MODEL:]

    h = _norm_modulate(xp, xs, shift3, scale3, norm_w)
    u = _in_proj(h, w_in, OFF_U, SSM_WIDTH, "u")
    zs = _in_proj(h, w_in, OFF_ZS, SSM_WIDTH, "plain")
    q = _in_proj(h, w_in, OFF_Q, ATT_WIDTH, "q", q_norm_w)
    k, k_ctx32 = _in_proj(h, w_in, OFF_K, ATT_WIDTH, "k", k_norm_w)
    v, v_ctx32 = _in_proj(h, w_in, OFF_V, ATT_WIDTH, "v")
    za = _in_proj(h, w_in, OFF_ZA, ATT_WIDTH, "plain")
    gs = _in_proj(h, w_in, OFF_GS, D_MODEL, "plain")
    ga = _in_proj(h, w_in, OFF_GA, D_MODEL, "plain")

    arow = jnp.transpose(a_re, (1, 0, 2)).reshape(G, 2 * P)
    airow = jnp.transpose(a_im, (1, 0, 2)).reshape(G, 2 * P)
    dtrow = jnp.broadcast_to(jnp.transpose(log_dt, (1, 0))[:, :, None], (G, 2, P)).reshape(G, 2 * P)
    rowp = jnp.stack([arow, airow, dtrow], axis=1)
    colp = jnp.transpose(rowp, (0, 2, 1))
    btr = jnp.transpose(b_re, (1, 3, 0, 2)).reshape(G, SSM_GROUP, 2 * P)
    bti = jnp.transpose(b_im, (1, 3, 0, 2)).reshape(G, SSM_GROUP, 2 * P)
    ctr = jnp.transpose(c_re, (1, 0, 3, 2)).reshape(G, 2 * P, SSM_GROUP)
    cti = jnp.transpose(c_im, (1, 0, 3, 2)).reshape(G, 2 * P, SSM_GROUP)
    t_op, s_op, r_op, lam, powc = _ssm_operators(rowp, colp, btr, bti, ctr, cti)

    dtile = jnp.tile(d.reshape(G, 1, SSM_GROUP), (1, 1, CHUNK))
    h0c = jnp.zeros((G, BATCH, 2 * P), F32)
    h0d_re = jnp.transpose(st_re, (2, 0, 1, 3)).reshape(G, DEC_BATCH, 2 * P)
    h0d_im = jnp.transpose(st_im, (2, 0, 1, 3)).reshape(G, DEC_BATCH, 2 * P)
    y5, fin = _ssm_scan(u, t_op, s_op, r_op, lam, powc, dtile, h0c, h0c, h0d_re, h0d_im)
    y = y5.reshape(2, CH_PER_SEG, CHUNK, N_SEG * SSM_WIDTH)

    a_ctx = _ctx_attention(q, k, v, za)
    nseq4 = N_TOK // DEC_SEQ
    a_dec = _nbr_attention(q.reshape(nseq4, DEC_SEQ, ATT_WIDTH), k.reshape(nseq4, DEC_SEQ, ATT_WIDTH),
                           v.reshape(nseq4, DEC_SEQ, ATT_WIDTH),
                           cache_k.reshape(DEC_BATCH, PAST_LEN, ATT_WIDTH),
                           cache_v.reshape(DEC_BATCH, PAST_LEN, ATT_WIDTH),
                           _nbr_bias_tables(rpb.astype(F32)),
                           za.reshape(nseq4, DEC_SEQ, ATT_WIDTH))

    yp, ys_out = _back(xp, xs, y, zs, a_ctx, a_dec.reshape(N_DEC_TOK, ATT_WIDTH), gs, ga, gate3, b_glu,
                       w_glu.astype(BF16), w_so.astype(BF16), w_ao.astype(BF16), w_o.astype(BF16))

    fin_ctx = fin[0]
    new_re = jnp.transpose(fin_ctx[:, :, :2 * P].reshape(G, BATCH, 2, P), (1, 2, 0, 3))
    new_im = jnp.transpose(fin_ctx[:, :, 2 * P:].reshape(G, BATCH, 2, P), (1, 2, 0, 3))
    return yp, ys_out, k_ctx32, v_ctx32, new_re, new_im


def kernel(x_prompt, x_sample, cache_k, cache_v, state_ssm_re, state_ssm_im, c, c_ctx, norm_w, w_ada, b_ada,
           w_in, q_norm_w, k_norm_w, rel_pos_bias, ssm_a_re, ssm_a_im, ssm_log_dt, ssm_b_re, ssm_b_im,
           ssm_c_re, ssm_c_im, ssm_d, w_glu, b_glu, w_ssm_out, w_att_out, w_o):
    depth = norm_w.shape[0]
    xp = x_prompt.reshape(N_CTX_TOK, D_MODEL)
    xs = x_sample.reshape(N_DEC_TOK, D_MODEL)
    new_k, new_v, new_re, new_im = [], [], [], []
    for l in range(depth):
        xp, xs, kl, vl, rl, il = _layer(
            xp, xs, cache_k[:, l], cache_v[:, l], state_ssm_re[:, l], state_ssm_im[:, l], c, c_ctx,
            norm_w[l], w_ada[l], b_ada[l], w_in[l], q_norm_w[l], k_norm_w[l], rel_pos_bias[l],
            ssm_a_re[l], ssm_a_im[l], ssm_log_dt[l], ssm_b_re[l], ssm_b_im[l], ssm_c_re[l], ssm_c_im[l],
            ssm_d[l], w_glu[l], b_glu[l], w_ssm_out[l], w_att_out[l], w_o[l])
        new_k.append(kl.reshape(BATCH, SEQ, N_HEADS, HEAD_DIM))
        new_v.append(vl.reshape(BATCH, SEQ, N_HEADS, HEAD_DIM))
        new_re.append(rl)
        new_im.append(il)
    return (xp.reshape(BATCH, SEQ, D_MODEL), xs.reshape(DEC_BATCH, DEC_SEQ, D_MODEL),
            jnp.stack(new_k, axis=1), jnp.stack(new_v, axis=1),
            jnp.stack(new_re, axis=1), jnp.stack(new_im, axis=1))
```

```python
import functools
import math

import numpy as np
import jax
import jax.numpy as jnp
from jax import lax
from jax.experimental import pallas as pl
from jax.experimental.pallas import tpu as pltpu

D_MODEL = 2048
BATCH = 16
SEQ = 256
DEC_BATCH = 2
DEC_SEQ = 2048
PAST_LEN = 512
GRID_W = 64
GRID_H = DEC_SEQ // GRID_W
SSM_WIDTH = D_MODEL // 2
SSM_GROUP = 16
SSM_GROUPS = SSM_WIDTH // SSM_GROUP
SSM_STATE = 64
N_HEADS = 16
HEAD_DIM = 64
ATT_WIDTH = N_HEADS * HEAD_DIM
WIN_ROWS = 8
WIN_COLS = 16
EPS = 1e-6
NEG_INF = -1e30

N_CTX_TOK = BATCH * SEQ
N_DEC_TOK = DEC_BATCH * DEC_SEQ
N_TOK = N_CTX_TOK + N_DEC_TOK

OFF_U = 0
OFF_ZS = SSM_WIDTH
OFF_Q = 2 * SSM_WIDTH
OFF_K = OFF_Q + ATT_WIDTH
OFF_V = OFF_K + ATT_WIDTH
OFF_ZA = OFF_V + ATT_WIDTH
OFF_GS = OFF_ZA + ATT_WIDTH
OFF_GA = OFF_GS + D_MODEL

CHUNK = 16
FLAT = CHUNK * SSM_GROUP
N_SEG = 16
CH_PER_SEG = 16
DSTATE = 2 * SSM_STATE

Q_ROWS_PER_BLOCK = 4
NBR_Q = Q_ROWS_PER_BLOCK * GRID_W
NBR_KROWS = 12
NBR_K = NBR_KROWS * GRID_W

VMEM_LIMIT = 56 * 1024 * 1024

F32 = jnp.float32
BF16 = jnp.bfloat16
HIGHEST = lax.Precision.HIGHEST


def _sigmoid(x):
    return 1.0 / (1.0 + jnp.exp(-x))


def _silu(x):
    return x * _sigmoid(x)


def _gelu_tanh(x):
    return 0.5 * x * (1.0 + jnp.tanh(math.sqrt(2.0 / math.pi) * (x + 0.044715 * (x * x * x))))


def _cmul(ar, ai, br, bi):
    return ar * br - ai * bi, ar * bi + ai * br


def _dot_nt(a, b):
    return lax.dot_general(a, b, (((1,), (1,)), ((), ())), preferred_element_type=F32)


def _params(*sem):
    return pltpu.CompilerParams(dimension_semantics=sem, vmem_limit_bytes=VMEM_LIMIT)


def _mod_kernel(cond_ref, w_ref, b_ref, o_ref):
    c = cond_ref[...]
    s = _silu(c).astype(BF16)
    o_ref[...] = jnp.dot(s, w_ref[...].astype(BF16), preferred_element_type=F32) + b_ref[...]


def _modulation(cond8, w_ada, b_ada):
    tn = 768
    n = w_ada.shape[1]
    return pl.pallas_call(
        _mod_kernel,
        grid=(n // tn,),
        in_specs=[pl.BlockSpec((8, D_MODEL), lambda j: (0, 0)),
                  pl.BlockSpec((D_MODEL, tn), lambda j: (0, j)),
                  pl.BlockSpec((1, tn), lambda j: (0, j))],
        out_specs=pl.BlockSpec((8, tn), lambda j: (0, j)),
        out_shape=jax.ShapeDtypeStruct((8, n), F32),
        compiler_params=_params("arbitrary"),
        name="modulation",
    )(cond8, w_ada, b_ada.reshape(1, n))


NORM_TM = 512
NORM_CTX_STEPS = N_CTX_TOK // NORM_TM
NORM_STEPS_PER_DEC_SEQ = DEC_SEQ // NORM_TM


def _mod_row(i, ctx_steps, steps_per_seq):
    return jnp.where(i < ctx_steps, 0, 1 + (i - ctx_steps) // steps_per_seq)


def _norm_kernel(xp_ref, xs_ref, shift_ref, scale_ref, nw_ref, o_ref):
    i = pl.program_id(0)

    def body(x):
        ms = jnp.mean(x * x, axis=-1, keepdims=True)
        y = x * lax.rsqrt(ms + EPS) * nw_ref[...]
        o_ref[...] = (y * (1.0 + scale_ref[0]) + shift_ref[0]).astype(BF16)

    @pl.when(i < NORM_CTX_STEPS)
    def _():
        body(xp_ref[...])

    @pl.when(i >= NORM_CTX_STEPS)
    def _():
        body(xs_ref[...])


def _norm_modulate(xp, xs, shift3, scale3, norm_w):
    steps = N_TOK // NORM_TM
    row = functools.partial(_mod_row, ctx_steps=NORM_CTX_STEPS, steps_per_seq=NORM_STEPS_PER_DEC_SEQ)
    return pl.pallas_call(
        _norm_kernel,
        grid=(steps,),
        in_specs=[pl.BlockSpec((NORM_TM, D_MODEL), lambda i: (jnp.minimum(i, NORM_CTX_STEPS - 1), 0)),
                  pl.BlockSpec((NORM_TM, D_MODEL), lambda i: (jnp.maximum(i - NORM_CTX_STEPS, 0), 0)),
                  pl.BlockSpec((1, 1, D_MODEL), lambda i: (row(i), 0, 0)),
                  pl.BlockSpec((1, 1, D_MODEL), lambda i: (row(i), 0, 0)),
                  pl.BlockSpec((1, D_MODEL), lambda i: (0, 0))],
        out_specs=pl.BlockSpec((NORM_TM, D_MODEL), lambda i: (i, 0)),
        out_shape=jax.ShapeDtypeStruct((N_TOK, D_MODEL), BF16),
        compiler_params=_params("arbitrary"),
        name="norm_modulate",
    )(xp, xs, shift3, scale3, norm_w.reshape(1, D_MODEL))


PROJ_TM = 1024
PROJ_TN = 1024
PROJ_CTX_STEPS = N_CTX_TOK // PROJ_TM
PROJ_SEGS = PROJ_TM // (CHUNK * CH_PER_SEG)
HEADS_PER_MXU_TILE = 256 // HEAD_DIM


def _head_group_ones():
    r = lax.broadcasted_iota(jnp.int32, (256, 256), 0) // HEAD_DIM
    c = lax.broadcasted_iota(jnp.int32, (256, 256), 1) // HEAD_DIM
    return jnp.where(r == c, 1.0, 0.0).astype(BF16)


def _head_rms(acc, nw):
    ones = _head_group_ones()
    outs = []
    for c in range(acc.shape[1] // 256):
        a = acc[:, c * 256:(c + 1) * 256]
        ssum = jnp.dot((a * a).astype(BF16), ones, preferred_element_type=F32)
        outs.append(a * lax.rsqrt(ssum * (1.0 / HEAD_DIM) + EPS))
    return jnp.concatenate(outs, axis=1) * nw


def _proj_kernel(*refs, mode):
    if mode in ("plain", "u"):
        h_ref, w_ref, o_ref, wbf = refs
    elif mode == "q":
        h_ref, w_ref, nw_ref, o_ref, wbf = refs
    elif mode == "k":
        h_ref, w_ref, nw_ref, o_ref, of_ref, wbf = refs
    else:
        h_ref, w_ref, o_ref, of_ref, wbf = refs
    i = pl.program_id(1)

    @pl.when(i == 0)
    def _():
        wbf[...] = w_ref[...].astype(BF16)

    acc = jnp.dot(h_ref[...], wbf[...], preferred_element_type=F32)
    if mode == "q":
        acc = _head_rms(acc, nw_ref[...]) * (HEAD_DIM ** -0.5)
    elif mode == "k":
        acc = _head_rms(acc, nw_ref[...])
    if mode == "u":
        for seg in range(PROJ_SEGS):
            for ch in range(CH_PER_SEG):
                r0 = (seg * CH_PER_SEG + ch) * CHUNK
                o_ref[0, ch, seg] = acc[r0:r0 + CHUNK, :]
    else:
        o_ref[...] = acc.astype(BF16)
    if mode in ("k", "v"):
        @pl.when(i < PROJ_CTX_STEPS)
        def _():
            of_ref[...] = acc


def _in_proj(h, w_in, col_off, ncols, mode, norm_w=None):
    nj = ncols // PROJ_TN
    ni = N_TOK // PROJ_TM
    joff = col_off // PROJ_TN
    in_specs = [pl.BlockSpec((PROJ_TM, D_MODEL), lambda j, i: (i, 0)),
                pl.BlockSpec((D_MODEL, PROJ_TN), lambda j, i: (0, joff + j))]
    args = [h, w_in]
    if mode in ("q", "k"):
        in_specs.append(pl.BlockSpec((1, PROJ_TN), lambda j, i: (0, 0)))
        args.append(jnp.tile(norm_w.reshape(1, HEAD_DIM), (1, PROJ_TN // HEAD_DIM)))
    out_specs = [pl.BlockSpec((PROJ_TM, PROJ_TN), lambda j, i: (i, j))]
    out_shape = [jax.ShapeDtypeStruct((N_TOK, ncols), BF16)]
    if mode == "u":
        tiles_per_path = N_SEG // PROJ_SEGS
        out_specs = [pl.BlockSpec((1, CH_PER_SEG, PROJ_SEGS, CHUNK, PROJ_TN),
                                  lambda j, i: (i // tiles_per_path, 0, i % tiles_per_path, 0, j))]
        out_shape = [jax.ShapeDtypeStruct((2, CH_PER_SEG, N_SEG, CHUNK, ncols), F32)]
    if mode in ("k", "v"):
        out_specs.append(pl.BlockSpec((PROJ_TM, PROJ_TN),
                                      lambda j, i: (jnp.minimum(i, PROJ_CTX_STEPS - 1), j)))
        out_shape.append(jax.ShapeDtypeStruct((N_CTX_TOK, ncols), F32))
    res = pl.pallas_call(
        functools.partial(_proj_kernel, mode=mode),
        grid=(nj, ni),
        in_specs=in_specs,
        out_specs=out_specs,
        out_shape=out_shape,
        scratch_shapes=[pltpu.VMEM((D_MODEL, PROJ_TN), BF16)],
        compiler_params=_params("arbitrary", "arbitrary"),
        name="in_proj_" + mode,
    )(*args)
    return res if mode in ("k", "v") else res[0]


SSM_GB = 4


def _pow_select(e, pows):
    rr = jnp.where((e & 1) != 0, pows[0][0], 1.0)
    ri = jnp.where((e & 1) != 0, pows[0][1], 0.0)
    for b in range(1, len(pows)):
        bit = (e & (1 << b)) != 0
        fr = jnp.where(bit, pows[b][0], 1.0)
        fi = jnp.where(bit, pows[b][1], 0.0)
        rr, ri = _cmul(rr, ri, fr, fi)
    return rr, ri


def _discretize(a_re, a_im, log_dt):
    lr = jnp.minimum(a_re, -1e-4)
    li = a_im
    dt = jnp.exp(log_dt)
    mag = jnp.exp(lr * dt)
    br = mag * jnp.cos(li * dt)
    bi = mag * jnp.sin(li * dt)
    den = lr * lr + li * li
    nr = br - 1.0
    cr = (nr * lr + bi * li) / den
    ci = (bi * lr - nr * li) / den
    return (br, bi), (cr, ci)


def _squarings(pr, pi, n):
    out = [(pr, pi)]
    for _ in range(n):
        pr, pi = _cmul(pr, pi, pr, pi)
        out.append((pr, pi))
    return out


def _ssm_ops_kernel(rowp_ref, colp_ref, btr_ref, bti_ref, ctr_ref, cti_ref,
                    t_ref, s_ref, r_ref, lam_ref, powc_ref):
    lane_s = lax.broadcasted_iota(jnp.int32, (FLAT, DSTATE), 1)
    row_s = lax.broadcasted_iota(jnp.int32, (FLAT, DSTATE), 0)
    exp_s = jnp.where(lane_s < SSM_STATE, (CHUNK - 1) - (row_s >> 4), row_s >> 4)
    row_k = lax.broadcasted_iota(jnp.int32, (DSTATE, FLAT), 0)
    lane_k = lax.broadcasted_iota(jnp.int32, (DSTATE, FLAT), 1)
    exp_k = jnp.where(row_k < SSM_STATE, lane_k >> 4, (CHUNK - 1) - (lane_k >> 4))
    tile_e = jnp.where((lax.broadcasted_iota(jnp.int32, (SSM_GROUP, FLAT), 1) & (SSM_GROUP - 1))
                       == lax.broadcasted_iota(jnp.int32, (SSM_GROUP, FLAT), 0), 1.0, 0.0)
    lane_b = lax.broadcasted_iota(jnp.int32, (SSM_GROUP, DSTATE), 1)
    lane_t = lax.broadcasted_iota(jnp.int32, (SSM_GROUP, FLAT), 1)
    row_c = lax.broadcasted_iota(jnp.int32, (CH_PER_SEG, DSTATE), 0)
    lane_c = lax.broadcasted_iota(jnp.int32, (CH_PER_SEG, DSTATE), 1)
    exp_c = jnp.where(lane_c < SSM_STATE, row_c, (CH_PER_SEG - 1) - row_c)

    for g in range(SSM_GB):
        rp = rowp_ref[g]
        (lbr, lbi), (cfr, cfi) = _discretize(rp[0:1], rp[1:2], rp[2:3])
        sq = _squarings(lbr, lbi, 8)
        bbr, bbi = _cmul(btr_ref[g], bti_ref[g], cfr, cfi)

        pr, pi = _pow_select(exp_s, sq[:4])
        tb_r = jnp.concatenate([bbr] * CHUNK, axis=0)
        tb_i = jnp.concatenate([bbi] * CHUNK, axis=0)
        s_re, s_im = _cmul(tb_r, tb_i, pr, pi)
        s_ref[g] = jnp.concatenate([s_re, s_im], axis=1).astype(BF16)

        cp = colp_ref[g]
        (kbr, kbi), _ = _discretize(cp[:, 0:1], cp[:, 1:2], cp[:, 2:3])
        sqc = _squarings(kbr, kbi, 3)
        ctr = jnp.dot(ctr_ref[g], tile_e, preferred_element_type=F32, precision=HIGHEST)
        cti = jnp.dot(cti_ref[g], tile_e, preferred_element_type=F32, precision=HIGHEST)
        pkr, pki = _pow_select(exp_k, sqc)
        ykr, yki = _cmul(ctr, cti, pkr, pki)
        yrr, yri = _cmul(ykr, yki, kbr, kbi)
        r_ref[g] = jnp.concatenate([yrr, -yri], axis=0).astype(BF16)

        fwd = lane_b < SSM_STATE
        lhs = jnp.concatenate([jnp.where(fwd, bbr, 0.0), jnp.where(fwd, bbi, 0.0),
                               jnp.where(fwd, 0.0, bbr), jnp.where(fwd, 0.0, bbi)], axis=0)
        p1 = jnp.dot(lhs, ykr, preferred_element_type=F32, precision=HIGHEST)
        p2 = jnp.dot(lhs, yki, preferred_element_type=F32, precision=HIGHEST)
        kf = p1[0:16] - p2[16:32]
        kb = p1[32:48] - p2[48:64]
        for sp in range(CHUNK):
            tf = kf if sp == 0 else pltpu.roll(kf, SSM_GROUP * sp, axis=1)
            tf = jnp.where(lane_t >= SSM_GROUP * sp, tf, 0.0)
            shift = (FLAT - SSM_GROUP * (CHUNK - 1 - sp)) % FLAT
            tb = kb if shift == 0 else pltpu.roll(kb, shift, axis=1)
            tb = jnp.where(lane_t < SSM_GROUP * (sp + 1), tb, 0.0)
            t_ref[g, sp * SSM_GROUP:(sp + 1) * SSM_GROUP, :] = (tf + tb).astype(BF16)

        l16 = sq[4]
        l256 = sq[8]
        lam_ref[g] = jnp.concatenate([l16[0], l16[1], l256[0], l256[1],
                                      jnp.zeros((4, DSTATE), F32)], axis=0)
        q16 = _squarings(l16[0], l16[1], 3)
        pcr, pci = _pow_select(exp_c, q16)
        powc_ref[g, 0] = pcr
        powc_ref[g, 1] = pci


def _ssm_operators(rowp, colp, btr, bti, ctr, cti):
    G = SSM_GROUPS
    gb = SSM_GB
    return pl.pallas_call(
        _ssm_ops_kernel,
        grid=(G // gb,),
        in_specs=[pl.BlockSpec((gb, 3, DSTATE), lambda g: (g, 0, 0)),
                  pl.BlockSpec((gb, DSTATE, 3), lambda g: (g, 0, 0)),
                  pl.BlockSpec((gb, SSM_GROUP, DSTATE), lambda g: (g, 0, 0)),
                  pl.BlockSpec((gb, SSM_GROUP, DSTATE), lambda g: (g, 0, 0)),
                  pl.BlockSpec((gb, DSTATE, SSM_GROUP), lambda g: (g, 0, 0)),
                  pl.BlockSpec((gb, DSTATE, SSM_GROUP), lambda g: (g, 0, 0))],
        out_specs=[pl.BlockSpec((gb, FLAT, FLAT), lambda g: (g, 0, 0)),
                   pl.BlockSpec((gb, FLAT, FLAT), lambda g: (g, 0, 0)),
                   pl.BlockSpec((gb, FLAT, FLAT), lambda g: (g, 0, 0)),
                   pl.BlockSpec((gb, 8, DSTATE), lambda g: (g, 0, 0)),
                   pl.BlockSpec((gb, 2, CH_PER_SEG, DSTATE), lambda g: (g, 0, 0, 0))],
        out_shape=[jax.ShapeDtypeStruct((G, FLAT, FLAT), BF16),
                   jax.ShapeDtypeStruct((G, FLAT, FLAT), BF16),
                   jax.ShapeDtypeStruct((G, FLAT, FLAT), BF16),
                   jax.ShapeDtypeStruct((G, 8, DSTATE), F32),
                   jax.ShapeDtypeStruct((G, 2, CH_PER_SEG, DSTATE), F32)],
        compiler_params=_params("arbitrary"),
        name="ssm_operators",
    )(rowp, colp, btr, bti, ctr, cti)


SSM_PATHS = ((BATCH, SEQ // (CHUNK * CH_PER_SEG)), (DEC_BATCH, DEC_SEQ // (CHUNK * CH_PER_SEG)))


def _ssm_path(u32, s_op, t_op, r_op, lam, powc_ref, dskip, h0r, h0i, nseg):
    u = u32.astype(BF16)
    lane = lax.broadcasted_iota(jnp.int32, (N_SEG, DSTATE), 1)
    isf = lane < SSM_STATE
    l16r, l16i, l256r, l256i = lam[0:1], lam[1:2], lam[2:3], lam[3:4]

    z = jnp.dot(u, s_op, preferred_element_type=F32)
    zre, zim = z[:, :DSTATE], z[:, DSTATE:]

    hr = jnp.zeros((N_SEG, DSTATE), F32)
    hi = jnp.zeros((N_SEG, DSTATE), F32)
    hist = []
    for k in range(CH_PER_SEG):
        hist.append((hr, hi))
        kb = CH_PER_SEG - 1 - k
        zr = jnp.where(isf, zre[k * N_SEG:(k + 1) * N_SEG], zre[kb * N_SEG:(kb + 1) * N_SEG])
        zi = jnp.where(isf, zim[k * N_SEG:(k + 1) * N_SEG], zim[kb * N_SEG:(kb + 1) * N_SEG])
        nr, ni = _cmul(l16r, l16i, hr, hi)
        hr, hi = nr + zr, ni + zi

    if nseg > 1:
        h0r = jnp.broadcast_to(h0r[:, None, :], (N_SEG // nseg, nseg, DSTATE)).reshape(N_SEG, DSTATE)
        h0i = jnp.broadcast_to(h0i[:, None, :], (N_SEG // nseg, nseg, DSTATE)).reshape(N_SEG, DSTATE)
    seg = lax.broadcasted_iota(jnp.int32, (N_SEG, DSTATE), 0) & (nseg - 1)
    segpow = _squarings(l256r, l256i, max(int(math.log2(nseg)), 0))

    def shifted(x, d):
        dn = jnp.where(seg >= d, pltpu.roll(x, d, axis=0), 0.0)
        up = jnp.where(seg <= nseg - 1 - d, pltpu.roll(x, N_SEG - d, axis=0), 0.0)
        return jnp.where(isf, dn, up)

    pr, pi = hr, hi
    d = 1
    lvl = 0
    while d < nseg:
        ar, ai = _cmul(segpow[lvl][0], segpow[lvl][1], shifted(pr, d), shifted(pi, d))
        pr, pi = pr + ar, pi + ai
        d *= 2
        lvl += 1
    if nseg > 1:
        e_in = jnp.where(isf, seg, nseg - 1 - seg)
        wr, wi = _pow_select(e_in, segpow[:lvl])
        ar, ai = _cmul(wr, wi, h0r, h0i)
        hsr, hsi = shifted(pr, 1) + ar, shifted(pi, 1) + ai
    else:
        hsr, hsi = h0r, h0i
    er, ei = _cmul(segpow[lvl][0], segpow[lvl][1], h0r, h0i)
    fin = jnp.concatenate([pr + er, pi + ei], axis=1)

    rows = []
    for c in range(CH_PER_SEG):
        cb = CH_PER_SEG - 1 - c
        lr = jnp.where(isf, hist[c][0], hist[cb][0])
        li = jnp.where(isf, hist[c][1], hist[cb][1])
        ar, ai = _cmul(powc_ref[0, c:c + 1, :], powc_ref[1, c:c + 1, :], hsr, hsi)
        rows.append(jnp.concatenate([lr + ar, li + ai], axis=1))
    hent = jnp.concatenate(rows, axis=0).astype(BF16)

    y = (jnp.dot(u, t_op, preferred_element_type=F32)
         + jnp.dot(hent, r_op, preferred_element_type=F32)
         + u32 * dskip)
    return y, fin


GROUPS_PER_STEP = 128 // SSM_GROUP


def _block_transpose(arrs, blk):
    n = len(arrs)
    width = arrs[0].shape[1]
    j = lax.broadcasted_iota(jnp.int32, arrs[0].shape, 1) // blk
    k = n // 2
    while k >= 1:
        bit = (j & k) != 0
        new = list(arrs)
        for x in range(n):
            if x & k == 0:
                a, b = arrs[x], arrs[x | k]
                new[x] = jnp.where(bit, pltpu.roll(b, k * blk, axis=1), a)
                new[x | k] = jnp.where(bit, b, pltpu.roll(a, width - k * blk, axis=1))
        arrs = new
        k //= 2
    return arrs


def _ssm_kernel(u_ref, t_ref, s_ref, r_ref, lam_ref, powc_ref, d_ref, h0cr_ref, h0ci_ref, h0dr_ref, h0di_ref,
                y_ref, fin_ref, ubuf, ybuf):
    h0 = ((h0cr_ref, h0ci_ref), (h0dr_ref, h0di_ref))
    nhalf = FLAT // 128
    rows = N_SEG * CH_PER_SEG
    for path, (_, nseg) in enumerate(SSM_PATHS):
        row0 = path * rows * CHUNK
        for half in range(nhalf):
            toks = [u_ref[pl.ds(row0 + half * GROUPS_PER_STEP + sb, rows, stride=CHUNK), :]
                    for sb in range(GROUPS_PER_STEP)]
            grouped = _block_transpose(toks, SSM_GROUP)
            for g in range(GROUPS_PER_STEP):
                ubuf[g, :, half * 128:(half + 1) * 128] = grouped[g]

        def body(g, carry, path=path, nseg=nseg):
            y, fin = _ssm_path(ubuf[g], s_ref[g], t_ref[g], r_ref[g], lam_ref[g], powc_ref.at[g],
                               d_ref[g], h0[path][0][g], h0[path][1][g], nseg)
            ybuf[g] = y
            fin_ref[path, g] = fin
            return carry

        lax.fori_loop(0, GROUPS_PER_STEP, body, 0)

        for half in range(nhalf):
            grouped = [ybuf[g, :, half * 128:(half + 1) * 128] for g in range(GROUPS_PER_STEP)]
            toks = _block_transpose(grouped, SSM_GROUP)
            for sb in range(GROUPS_PER_STEP):
                y_ref[pl.ds(row0 + half * GROUPS_PER_STEP + sb, rows, stride=CHUNK), :] = toks[sb]


def _ssm_scan(u2d, t_op, s_op, r_op, lam, powc, dtile, h0c_re, h0c_im, h0d_re, h0d_im):
    G = SSM_GROUPS
    gs = GROUPS_PER_STEP
    rows = N_SEG * CH_PER_SEG
    return pl.pallas_call(
        _ssm_kernel,
        grid=(G // gs,),
        in_specs=[
            pl.BlockSpec((N_TOK, 128), lambda o: (0, o)),
            pl.BlockSpec((gs, FLAT, FLAT), lambda o: (o, 0, 0)),
            pl.BlockSpec((gs, FLAT, FLAT), lambda o: (o, 0, 0)),
            pl.BlockSpec((gs, FLAT, FLAT), lambda o: (o, 0, 0)),
            pl.BlockSpec((gs, 8, DSTATE), lambda o: (o, 0, 0)),
            pl.BlockSpec((gs, 2, CH_PER_SEG, DSTATE), lambda o: (o, 0, 0, 0)),
            pl.BlockSpec((gs, 1, FLAT), lambda o: (o, 0, 0)),
            pl.BlockSpec((gs, BATCH, DSTATE), lambda o: (o, 0, 0)),
            pl.BlockSpec((gs, BATCH, DSTATE), lambda o: (o, 0, 0)),
            pl.BlockSpec((gs, DEC_BATCH, DSTATE), lambda o: (o, 0, 0)),
            pl.BlockSpec((gs, DEC_BATCH, DSTATE), lambda o: (o, 0, 0))],
        out_specs=[pl.BlockSpec((N_TOK, 128), lambda o: (0, o)),
                   pl.BlockSpec((2, gs, N_SEG, 2 * DSTATE), lambda o: (0, o, 0, 0))],
        out_shape=[jax.ShapeDtypeStruct((N_TOK, SSM_WIDTH), F32),
                   jax.ShapeDtypeStruct((2, G, N_SEG, 2 * DSTATE), F32)],
        scratch_shapes=[pltpu.VMEM((gs, rows, FLAT), F32), pltpu.VMEM((gs, rows, FLAT), F32)],
        compiler_params=_params("arbitrary"),
        name="ssm_scan",
    )(u2d, t_op, s_op, r_op, lam, powc, dtile, h0c_re, h0c_im, h0d_re, h0d_im)


CTX_BB = 4


def _softmax_pv(scores, values):
    m = scores[0].max(axis=-1, keepdims=True)
    for s in scores[1:]:
        m = jnp.maximum(m, s.max(axis=-1, keepdims=True))
    l = None
    o = None
    for s, v in zip(scores, values):
        p = jnp.exp(s - m)
        ls = p.sum(axis=-1, keepdims=True)
        os_ = jnp.dot(p.astype(BF16), v, preferred_element_type=F32)
        l = ls if l is None else l + ls
        o = os_ if o is None else o + os_
    return o / l


def _ctx_attn_kernel(q_ref, k_ref, v_ref, za_ref, o_ref):
    head0 = lax.broadcasted_iota(jnp.int32, (1, 2 * HEAD_DIM), 1) < HEAD_DIM
    for b in range(CTX_BB):
        sl = slice(b * SEQ, (b + 1) * SEQ)
        q, k, v = q_ref[sl, :], k_ref[sl, :], v_ref[sl, :]
        outs = []
        for hh in range(2):
            qh = jnp.where(head0 if hh == 0 else jnp.logical_not(head0), q, jnp.zeros_like(q))
            outs.append(_softmax_pv([_dot_nt(qh, k)], [v]))
        o = jnp.where(head0, outs[0], outs[1])
        o_ref[sl, :] = (o * _silu(za_ref[sl, :].astype(F32))).astype(BF16)


def _ctx_attention(q, k, v, za):
    rows = CTX_BB * SEQ
    spec = pl.BlockSpec((rows, 2 * HEAD_DIM), lambda b, hp: (b, hp))
    return pl.pallas_call(
        _ctx_attn_kernel,
        grid=(BATCH // CTX_BB, N_HEADS // 2),
        in_specs=[spec, spec, spec, spec],
        out_specs=spec,
        out_shape=jax.ShapeDtypeStruct((N_CTX_TOK, ATT_WIDTH), BF16),
        compiler_params=_params("arbitrary", "arbitrary"),
        name="ctx_attention",
    )(q, k, v, za)


N_DROW = 2 * WIN_ROWS - 1


def _nbr_bias(bias_ref, hh, a):
    half0 = lax.broadcasted_iota(jnp.int32, (1, 2 * GRID_W), 1) < GRID_W
    ks = jnp.clip(Q_ROWS_PER_BLOCK * a - Q_ROWS_PER_BLOCK, 0, GRID_H - NBR_KROWS)
    rows = []
    for ri in range(Q_ROWS_PER_BLOCK):
        r = Q_ROWS_PER_BLOCK * a + ri
        rs = jnp.clip(r - WIN_ROWS // 2, 0, GRID_H - WIN_ROWS)
        tiles = []
        for pair in range(NBR_KROWS // 2):
            idx = []
            for kri in (2 * pair, 2 * pair + 1):
                kr = ks + kri
                valid = jnp.logical_and(kr >= rs, kr < rs + WIN_ROWS)
                idx.append(jnp.where(valid, kr - r + WIN_ROWS - 1, N_DROW))
            tiles.append(jnp.where(half0, bias_ref[hh, idx[0]], bias_ref[hh, idx[1]]))
        rows.append(jnp.concatenate(tiles, axis=1))
    return jnp.concatenate(rows, axis=0)


def _nbr_attn_kernel(q_ref, k_ref, v_ref, kc_ref, vc_ref, bias_ref, za_ref, o_ref):
    a = pl.program_id(1)
    start = pl.multiple_of(jnp.clip(a * NBR_Q - NBR_Q, 0, DEC_SEQ - NBR_K), NBR_Q)
    head0 = lax.broadcasted_iota(jnp.int32, (1, 2 * HEAD_DIM), 1) < HEAD_DIM
    bias = [_nbr_bias(bias_ref, hh, a) for hh in range(2)]
    for b in range(DEC_BATCH):
        q = q_ref[b]
        kw = k_ref[b, pl.ds(start, NBR_K), :]
        vw = v_ref[b, pl.ds(start, NBR_K), :]
        kc = kc_ref[b].astype(BF16)
        vc = vc_ref[b].astype(BF16)
        outs = []
        for hh in range(2):
            qh = jnp.where(head0 if hh == 0 else jnp.logical_not(head0), q, jnp.zeros_like(q))
            sw = _dot_nt(qh, kw) + bias[hh]
            sc = _dot_nt(qh, kc)
            outs.append(_softmax_pv([sw, sc], [vw, vc]))
        o = jnp.where(head0, outs[0], outs[1])
        o_ref[b] = (o * _silu(za_ref[b].astype(F32))).astype(BF16)


def _nbr_attention(q4, k4, v4, kc, vc, bias, za4):
    first = N_CTX_TOK // DEC_SEQ // DEC_BATCH
    nblk = GRID_H // Q_ROWS_PER_BLOCK
    hw = 2 * HEAD_DIM
    qspec = pl.BlockSpec((DEC_BATCH, NBR_Q, hw), lambda hp, a: (first, a, hp))
    kspec = pl.BlockSpec((DEC_BATCH, DEC_SEQ, hw), lambda hp, a: (first, 0, hp))
    cspec = pl.BlockSpec((DEC_BATCH, PAST_LEN, hw), lambda hp, a: (0, 0, hp))
    bspec = pl.BlockSpec((2, N_DROW + 1, GRID_W, 2 * GRID_W), lambda hp, a: (hp, 0, 0, 0))
    return pl.pallas_call(
        _nbr_attn_kernel,
        grid=(N_HEADS // 2, nblk),
        in_specs=[qspec, kspec, kspec, cspec, cspec, bspec, qspec],
        out_specs=pl.BlockSpec((DEC_BATCH, NBR_Q, hw), lambda hp, a: (0, a, hp)),
        out_shape=jax.ShapeDtypeStruct((DEC_BATCH, DEC_SEQ, ATT_WIDTH), BF16),
        compiler_params=_params("arbitrary", "arbitrary"),
        name="nbr_attention",
    )(q4, k4, v4, kc, vc, bias, za4)


def _nbr_bias_tables(rpb):
    w = GRID_W
    ncol = 2 * WIN_COLS - 1
    left = w - WIN_COLS
    band = jnp.pad(rpb, ((0, 0), (0, 0), (left, 2 * w - left - ncol)), constant_values=NEG_INF)
    skew = jnp.tile(band, (1, 1, w))[:, :, :w * (2 * w - 1)].reshape(N_HEADS, N_DROW, w, 2 * w - 1)
    blocks = skew[..., w - 1:]
    qc = np.arange(w)[:, None]
    kc = np.arange(w)[None, :]
    cs = np.clip(qc - WIN_COLS // 2, 0, w - WIN_COLS)
    valid = (kc >= cs) & (kc < cs + WIN_COLS)
    blocks = jnp.where(valid, blocks, NEG_INF)
    blocks = jnp.concatenate([blocks, jnp.full((N_HEADS, 1, w, w), NEG_INF, F32)], axis=1)
    return jnp.concatenate([blocks, blocks], axis=-1)


BACK_TM = CHUNK * CH_PER_SEG
BACK_CTX_STEPS = N_CTX_TOK // BACK_TM
BACK_STEPS_PER_DEC_SEQ = DEC_SEQ // BACK_TM


def _back_kernel(xp_ref, xs_ref, y_ref, zs_ref, ac_ref, ad_ref, gs_ref, ga_ref, gate_ref, bglu_ref,
                 wglu_ref, wso_ref, wao_ref, wo_ref, op_ref, os_ref):
    i = pl.program_id(0)

    def compute(x, a2):
        ys = _gelu_tanh(y_ref[0].reshape(BACK_TM, SSM_WIDTH))
        t = jnp.dot(ys.astype(BF16), wglu_ref[...], preferred_element_type=F32) + bglu_ref[...]
        ys = ys * _sigmoid(t) * _silu(zs_ref[...].astype(F32))
        p_s = jnp.dot(ys.astype(BF16), wso_ref[...], preferred_element_type=F32)
        p_a = jnp.dot(a2, wao_ref[...], preferred_element_type=F32)
        merged = (_sigmoid(gs_ref[...].astype(F32)) * p_s + _sigmoid(ga_ref[...].astype(F32)) * p_a)
        return x + gate_ref[0] * jnp.dot(merged.astype(BF16), wo_ref[...], preferred_element_type=F32)

    @pl.when(i < BACK_CTX_STEPS)
    def _():
        op_ref[...] = compute(xp_ref[...], ac_ref[...])

    @pl.when(i >= BACK_CTX_STEPS)
    def _():
        os_ref[...] = compute(xs_ref[...], ad_ref[...])


def _back(xp, xs, y, zs, a_ctx, a_dec, gs, ga, gate3, b_glu, w_glu, w_so, w_ao, w_o):
    steps = N_TOK // BACK_TM
    n0 = BACK_CTX_STEPS
    row = functools.partial(_mod_row, ctx_steps=n0, steps_per_seq=BACK_STEPS_PER_DEC_SEQ)
    lo = lambda i: (jnp.minimum(i, n0 - 1), 0)
    hi = lambda i: (jnp.maximum(i - n0, 0), 0)
    cur = lambda i: (i, 0)
    const = lambda i: (0, 0)
    once = pl.Buffered(1)
    return pl.pallas_call(
        _back_kernel,
        grid=(steps,),
        in_specs=[pl.BlockSpec((BACK_TM, D_MODEL), lo),
                  pl.BlockSpec((BACK_TM, D_MODEL), hi),
                  pl.BlockSpec((1, CH_PER_SEG, None, CHUNK, SSM_WIDTH),
                               lambda i: (i // N_SEG, 0, i % N_SEG, 0, 0)),
                  pl.BlockSpec((BACK_TM, SSM_WIDTH), cur),
                  pl.BlockSpec((BACK_TM, ATT_WIDTH), lo),
                  pl.BlockSpec((BACK_TM, ATT_WIDTH), hi),
                  pl.BlockSpec((BACK_TM, D_MODEL), cur),
                  pl.BlockSpec((BACK_TM, D_MODEL), cur),
                  pl.BlockSpec((1, 1, D_MODEL), lambda i: (row(i), 0, 0)),
                  pl.BlockSpec((1, SSM_WIDTH), const),
                  pl.BlockSpec((SSM_WIDTH, SSM_WIDTH), const, pipeline_mode=once),
                  pl.BlockSpec((SSM_WIDTH, D_MODEL), const, pipeline_mode=once),
                  pl.BlockSpec((ATT_WIDTH, D_MODEL), const, pipeline_mode=once),
                  pl.BlockSpec((D_MODEL, D_MODEL), const, pipeline_mode=once)],
        out_specs=[pl.BlockSpec((BACK_TM, D_MODEL), lo),
                   pl.BlockSpec((BACK_TM, D_MODEL), hi)],
        out_shape=[jax.ShapeDtypeStruct((N_CTX_TOK, D_MODEL), F32),
                   jax.ShapeDtypeStruct((N_DEC_TOK, D_MODEL), F32)],
        compiler_params=_params("arbitrary"),
        name="gated_output",
    )(xp, xs, y, zs, a_ctx, a_dec, gs, ga, gate3, b_glu.reshape(1, SSM_WIDTH), w_glu, w_so, w_ao, w_o)


def _layer(xp, xs, cache_k, cache_v, st_re, st_im, c, c_ctx, norm_w, w_ada, b_ada, w_in, q_norm_w, k_norm_w,
           rpb, a_re, a_im, log_dt, b_re, b_im, c_re, c_im, d, w_glu, b_glu, w_so, w_ao, w_o):
    G, P = SSM_GROUPS, SSM_STATE

    cond8 = jnp.zeros((8, D_MODEL), F32).at[0].set(c_ctx).at[1:1 + DEC_BATCH].set(c)
    mod = _modulation(cond8, w_ada, b_ada)
    shift3 = mod[:1 + DEC_BATCH, None, :D_MODEL]
    scale3 = mod[:1 + DEC_BATCH, None, D_MODEL:2 * D_MODEL]
    gate3 = mod[:1 + DEC_BATCH, None, 2 * D_MODEL:]

    h = _norm_modulate(xp, xs, shift3, scale3, norm_w)
    u = _in_proj(h, w_in, OFF_U, SSM_WIDTH, "u")
    zs = _in_proj(h, w_in, OFF_ZS, SSM_WIDTH, "plain")
    q = _in_proj(h, w_in, OFF_Q, ATT_WIDTH, "q", q_norm_w)
    k, k_ctx32 = _in_proj(h, w_in, OFF_K, ATT_WIDTH, "k", k_norm_w)
    v, v_ctx32 = _in_proj(h, w_in, OFF_V, ATT_WIDTH, "v")
    za = _in_proj(h, w_in, OFF_ZA, ATT_WIDTH, "plain")
    gs = _in_proj(h, w_in, OFF_GS, D_MODEL, "plain")
    ga = _in_proj(h, w_in, OFF_GA, D_MODEL, "plain")

    arow = jnp.transpose(a_re, (1, 0, 2)).reshape(G, 2 * P)
    airow = jnp.transpose(a_im, (1, 0, 2)).reshape(G, 2 * P)
    dtrow = jnp.broadcast_to(jnp.transpose(log_dt, (1, 0))[:, :, None], (G, 2, P)).reshape(G, 2 * P)
    rowp = jnp.stack([arow, airow, dtrow], axis=1)
    colp = jnp.transpose(rowp, (0, 2, 1))
    btr = jnp.transpose(b_re, (1, 3, 0, 2)).reshape(G, SSM_GROUP, 2 * P)
    bti = jnp.transpose(b_im, (1, 3, 0, 2)).reshape(G, SSM_GROUP, 2 * P)
    ctr = jnp.transpose(c_re, (1, 0, 3, 2)).reshape(G, 2 * P, SSM_GROUP)
    cti = jnp.transpose(c_im, (1, 0, 3, 2)).reshape(G, 2 * P, SSM_GROUP)
    t_op, s_op, r_op, lam, powc = _ssm_operators(rowp, colp, btr, bti, ctr, cti)

    dtile = jnp.tile(d.reshape(G, 1, SSM_GROUP), (1, 1, CHUNK))
    h0c = jnp.zeros((G, BATCH, 2 * P), F32)
    h0d_re = jnp.transpose(st_re, (2, 0, 1, 3)).reshape(G, DEC_BATCH, 2 * P)
    h0d_im = jnp.transpose(st_im, (2, 0, 1, 3)).reshape(G, DEC_BATCH, 2 * P)
    y2d, fin = _ssm_scan(u.reshape(N_TOK, SSM_WIDTH), t_op, s_op, r_op, lam, powc, dtile,
                         h0c, h0c, h0d_re, h0d_im)
    y = y2d.reshape(2, CH_PER_SEG, N_SEG, CHUNK, SSM_WIDTH)

    a_ctx = _ctx_attention(q, k, v, za)
    nseq4 = N_TOK // DEC_SEQ
    a_dec = _nbr_attention(q.reshape(nseq4, DEC_SEQ, ATT_WIDTH), k.reshape(nseq4, DEC_SEQ, ATT_WIDTH),
                           v.reshape(nseq4, DEC_SEQ, ATT_WIDTH),
                           cache_k.reshape(DEC_BATCH, PAST_LEN, ATT_WIDTH),
                           cache_v.reshape(DEC_BATCH, PAST_LEN, ATT_WIDTH),
                           _nbr_bias_tables(rpb.astype(F32)),
                           za.reshape(nseq4, DEC_SEQ, ATT_WIDTH))

    yp, ys_out = _back(xp, xs, y, zs, a_ctx, a_dec.reshape(N_DEC_TOK, ATT_WIDTH), gs, ga, gate3, b_glu,
                       w_glu.astype(BF16), w_so.astype(BF16), w_ao.astype(BF16), w_o.astype(BF16))

    fin_ctx = fin[0]
    new_re = jnp.transpose(fin_ctx[:, :, :2 * P].reshape(G, BATCH, 2, P), (1, 2, 0, 3))
    new_im = jnp.transpose(fin_ctx[:, :, 2 * P:].reshape(G, BATCH, 2, P), (1, 2, 0, 3))
    return yp, ys_out, k_ctx32, v_ctx32, new_re, new_im


def kernel(x_prompt, x_sample, cache_k, cache_v, state_ssm_re, state_ssm_im, c, c_ctx, norm_w, w_ada, b_ada,
           w_in, q_norm_w, k_norm_w, rel_pos_bias, ssm_a_re, ssm_a_im, ssm_log_dt, ssm_b_re, ssm_b_im,
           ssm_c_re, ssm_c_im, ssm_d, w_glu, b_glu, w_ssm_out, w_att_out, w_o):
    depth = norm_w.shape[0]
    xp = x_prompt.reshape(N_CTX_TOK, D_MODEL)
    xs = x_sample.reshape(N_DEC_TOK, D_MODEL)
    new_k, new_v, new_re, new_im = [], [], [], []
    for l in range(depth):
        xp, xs, kl, vl, rl, il = _layer(
            xp, xs, cache_k[:, l], cache_v[:, l], state_ssm_re[:, l], state_ssm_im[:, l], c, c_ctx,
            norm_w[l], w_ada[l], b_ada[l], w_in[l], q_norm_w[l], k_norm_w[l], rel_pos_bias[l],
            ssm_a_re[l], ssm_a_im[l], ssm_log_dt[l], ssm_b_re[l], ssm_b_im[l], ssm_c_re[l], ssm_c_im[l],
            ssm_d[l], w_glu[l], b_glu[l], w_ssm_out[l], w_att_out[l], w_o[l])
        new_k.append(kl.reshape(BATCH, SEQ, N_HEADS, HEAD_DIM))
        new_v.append(vl.reshape(BATCH, SEQ, N_HEADS, HEAD_DIM))
        new_re.append(rl)
        new_im.append(il)
    return (xp.reshape(BATCH, SEQ, D_MODEL), xs.reshape(DEC_BATCH, DEC_SEQ, D_MODEL),
            jnp.stack(new_k, axis=1), jnp.stack(new_v, axis=1),
            jnp.stack(new_re, axis=1), jnp.stack(new_im, axis=1))
```

```python
import functools
import math

import numpy as np
import jax
import jax.numpy as jnp
from jax import lax
from jax.experimental import pallas as pl
from jax.experimental.pallas import tpu as pltpu

D_MODEL = 2048
BATCH = 16
SEQ = 256
DEC_BATCH = 2
DEC_SEQ = 2048
PAST_LEN = 512
GRID_W = 64
GRID_H = DEC_SEQ // GRID_W
SSM_WIDTH = D_MODEL // 2
SSM_GROUP = 16
SSM_GROUPS = SSM_WIDTH // SSM_GROUP
SSM_STATE = 64
N_HEADS = 16
HEAD_DIM = 64
ATT_WIDTH = N_HEADS * HEAD_DIM
WIN_ROWS = 8
WIN_COLS = 16
EPS = 1e-6
NEG_INF = -1e30

N_CTX_TOK = BATCH * SEQ
N_DEC_TOK = DEC_BATCH * DEC_SEQ
N_TOK = N_CTX_TOK + N_DEC_TOK

OFF_U = 0
OFF_ZS = SSM_WIDTH
OFF_Q = 2 * SSM_WIDTH
OFF_K = OFF_Q + ATT_WIDTH
OFF_V = OFF_K + ATT_WIDTH
OFF_ZA = OFF_V + ATT_WIDTH
OFF_GS = OFF_ZA + ATT_WIDTH
OFF_GA = OFF_GS + D_MODEL

CHUNK = 16
FLAT = CHUNK * SSM_GROUP
N_SEG = 16
CH_PER_SEG = 16
DSTATE = 2 * SSM_STATE

Q_ROWS_PER_BLOCK = 4
NBR_Q = Q_ROWS_PER_BLOCK * GRID_W
NBR_KROWS = 12
NBR_K = NBR_KROWS * GRID_W

VMEM_LIMIT = 56 * 1024 * 1024

F32 = jnp.float32
BF16 = jnp.bfloat16
HIGHEST = lax.Precision.HIGHEST


def _sigmoid(x):
    return 1.0 / (1.0 + jnp.exp(-x))


def _silu(x):
    return x * _sigmoid(x)


def _gelu_tanh(x):
    return 0.5 * x * (1.0 + jnp.tanh(math.sqrt(2.0 / math.pi) * (x + 0.044715 * (x * x * x))))


def _cmul(ar, ai, br, bi):
    return ar * br - ai * bi, ar * bi + ai * br


def _dot_nt(a, b):
    return lax.dot_general(a, b, (((1,), (1,)), ((), ())), preferred_element_type=F32)


def _params(*sem):
    return pltpu.CompilerParams(dimension_semantics=sem, vmem_limit_bytes=VMEM_LIMIT)


def _mod_kernel(cond_ref, w_ref, b_ref, o_ref):
    c = cond_ref[...]
    s = _silu(c).astype(BF16)
    o_ref[...] = jnp.dot(s, w_ref[...].astype(BF16), preferred_element_type=F32) + b_ref[...]


def _modulation(cond8, w_ada, b_ada):
    tn = 768
    n = w_ada.shape[1]
    return pl.pallas_call(
        _mod_kernel,
        grid=(n // tn,),
        in_specs=[pl.BlockSpec((8, D_MODEL), lambda j: (0, 0)),
                  pl.BlockSpec((D_MODEL, tn), lambda j: (0, j)),
                  pl.BlockSpec((1, tn), lambda j: (0, j))],
        out_specs=pl.BlockSpec((8, tn), lambda j: (0, j)),
        out_shape=jax.ShapeDtypeStruct((8, n), F32),
        compiler_params=_params("arbitrary"),
        name="modulation",
    )(cond8, w_ada, b_ada.reshape(1, n))


NORM_TM = 512
NORM_CTX_STEPS = N_CTX_TOK // NORM_TM
NORM_STEPS_PER_DEC_SEQ = DEC_SEQ // NORM_TM


def _mod_row(i, ctx_steps, steps_per_seq):
    return jnp.where(i < ctx_steps, 0, 1 + (i - ctx_steps) // steps_per_seq)


def _norm_kernel(xp_ref, xs_ref, shift_ref, scale_ref, nw_ref, o_ref):
    i = pl.program_id(0)

    def body(x):
        ms = jnp.mean(x * x, axis=-1, keepdims=True)
        y = x * lax.rsqrt(ms + EPS) * nw_ref[...]
        o_ref[...] = (y * (1.0 + scale_ref[0]) + shift_ref[0]).astype(BF16)

    @pl.when(i < NORM_CTX_STEPS)
    def _():
        body(xp_ref[...])

    @pl.when(i >= NORM_CTX_STEPS)
    def _():
        body(xs_ref[...])


def _norm_modulate(xp, xs, shift3, scale3, norm_w):
    steps = N_TOK // NORM_TM
    row = functools.partial(_mod_row, ctx_steps=NORM_CTX_STEPS, steps_per_seq=NORM_STEPS_PER_DEC_SEQ)
    return pl.pallas_call(
        _norm_kernel,
        grid=(steps,),
        in_specs=[pl.BlockSpec((NORM_TM, D_MODEL), lambda i: (jnp.minimum(i, NORM_CTX_STEPS - 1), 0)),
                  pl.BlockSpec((NORM_TM, D_MODEL), lambda i: (jnp.maximum(i - NORM_CTX_STEPS, 0), 0)),
                  pl.BlockSpec((1, 1, D_MODEL), lambda i: (row(i), 0, 0)),
                  pl.BlockSpec((1, 1, D_MODEL), lambda i: (row(i), 0, 0)),
                  pl.BlockSpec((1, D_MODEL), lambda i: (0, 0))],
        out_specs=pl.BlockSpec((NORM_TM, D_MODEL), lambda i: (i, 0)),
        out_shape=jax.ShapeDtypeStruct((N_TOK, D_MODEL), BF16),
        compiler_params=_params("arbitrary"),
        name="norm_modulate",
    )(xp, xs, shift3, scale3, norm_w.reshape(1, D_MODEL))


PROJ_TM = 512
PROJ_TN = 1024
PROJ_CTX_STEPS = N_CTX_TOK // PROJ_TM
PROJ_SEGS = PROJ_TM // (CHUNK * CH_PER_SEG)
HEADS_PER_MXU_TILE = 256 // HEAD_DIM


def _head_group_ones():
    r = lax.broadcasted_iota(jnp.int32, (256, 256), 0) // HEAD_DIM
    c = lax.broadcasted_iota(jnp.int32, (256, 256), 1) // HEAD_DIM
    return jnp.where(r == c, 1.0, 0.0).astype(BF16)


def _head_rms(acc, nw):
    ones = _head_group_ones()
    outs = []
    for c in range(acc.shape[1] // 256):
        a = acc[:, c * 256:(c + 1) * 256]
        ssum = jnp.dot((a * a).astype(BF16), ones, preferred_element_type=F32)
        outs.append(a * lax.rsqrt(ssum * (1.0 / HEAD_DIM) + EPS))
    return jnp.concatenate(outs, axis=1) * nw


PROJ_TILE_U, PROJ_TILE_ZS, PROJ_TILE_Q, PROJ_TILE_K, PROJ_TILE_V, PROJ_TILE_ZA = 0, 1, 2, 3, 4, 5
PROJ_TILE_GS = OFF_GS // PROJ_TN
PROJ_TILE_GA = OFF_GA // PROJ_TN
PROJ_N_TILES = (OFF_GA + D_MODEL) // PROJ_TN
BF_GS = 0
BF_GA = BF_GS + D_MODEL // PROJ_TN
BF_FIRST_NARROW = BF_GA + D_MODEL // PROJ_TN
BF_ZS, BF_Q, BF_K, BF_V, BF_ZA = (BF_FIRST_NARROW + t for t in range(5))


def _proj_kernel(h_ref, w_ref, nw_ref, u_ref, o_ref, kv_ref, wbf):
    j = pl.program_id(0)
    i = pl.program_id(1)

    @pl.when(i == 0)
    def _():
        wbf[...] = w_ref[...].astype(BF16)

    acc = jnp.dot(h_ref[...], wbf[...], preferred_element_type=F32)

    @pl.when(j == PROJ_TILE_U)
    def _():
        for seg in range(PROJ_SEGS):
            for ch in range(CH_PER_SEG):
                r0 = (seg * CH_PER_SEG + ch) * CHUNK
                u_ref[0, ch, seg] = acc[r0:r0 + CHUNK, :]

    @pl.when(j == PROJ_TILE_Q)
    def _():
        o_ref[...] = (_head_rms(acc, nw_ref[0]) * (HEAD_DIM ** -0.5)).astype(BF16)

    @pl.when(j == PROJ_TILE_K)
    def _():
        kn = _head_rms(acc, nw_ref[0])
        o_ref[...] = kn.astype(BF16)

        @pl.when(i < PROJ_CTX_STEPS)
        def _():
            kv_ref[...] = kn

    @pl.when(j == PROJ_TILE_V)
    def _():
        o_ref[...] = acc.astype(BF16)

        @pl.when(i < PROJ_CTX_STEPS)
        def _():
            kv_ref[...] = acc

    @pl.when(jnp.logical_or(j == PROJ_TILE_ZS, j >= PROJ_TILE_ZA))
    def _():
        o_ref[...] = acc.astype(BF16)


def _in_proj(h, w_in, q_norm_w, k_norm_w):
    ni = N_TOK // PROJ_TM
    last = ni - 1
    tiles_per_path = N_SEG // PROJ_SEGS
    nw = jnp.stack([jnp.tile(q_norm_w.reshape(1, HEAD_DIM), (1, PROJ_TN // HEAD_DIM)),
                    jnp.tile(k_norm_w.reshape(1, HEAD_DIM), (1, PROJ_TN // HEAD_DIM))])

    def u_map(j, i):
        ii = jnp.where(j == PROJ_TILE_U, i, last)
        return (ii // tiles_per_path, 0, ii % tiles_per_path, 0, 0)

    def bf_map(j, i):
        narrow = jnp.maximum(j, PROJ_TILE_ZS) - PROJ_TILE_ZS + BF_FIRST_NARROW
        return (jnp.where(j == PROJ_TILE_U, 0, i), jnp.where(j >= PROJ_TILE_GS, j - PROJ_TILE_GS, narrow))

    def kv_map(j, i):
        ctx_last = PROJ_CTX_STEPS - 1
        ii = jnp.where(j < PROJ_TILE_K, 0, jnp.where(j > PROJ_TILE_V, ctx_last, jnp.minimum(i, ctx_last)))
        return (ii, jnp.clip(j - PROJ_TILE_K, 0, 1))

    return pl.pallas_call(
        _proj_kernel,
        grid=(PROJ_N_TILES, ni),
        in_specs=[pl.BlockSpec((PROJ_TM, D_MODEL), lambda j, i: (i, 0)),
                  pl.BlockSpec((D_MODEL, PROJ_TN), lambda j, i: (0, j)),
                  pl.BlockSpec((1, 1, PROJ_TN), lambda j, i: (jnp.where(j == PROJ_TILE_K, 1, 0), 0, 0))],
        out_specs=[pl.BlockSpec((1, CH_PER_SEG, PROJ_SEGS, CHUNK, PROJ_TN), u_map),
                   pl.BlockSpec((PROJ_TM, PROJ_TN), bf_map),
                   pl.BlockSpec((PROJ_TM, PROJ_TN), kv_map)],
        out_shape=[jax.ShapeDtypeStruct((2, CH_PER_SEG, N_SEG, CHUNK, SSM_WIDTH), F32),
                   jax.ShapeDtypeStruct((N_TOK, (PROJ_N_TILES - 1) * PROJ_TN), BF16),
                   jax.ShapeDtypeStruct((N_CTX_TOK, 2 * PROJ_TN), F32)],
        scratch_shapes=[pltpu.VMEM((D_MODEL, PROJ_TN), BF16)],
        compiler_params=_params("arbitrary", "arbitrary"),
        name="in_proj",
    )(h, w_in, nw)


SSM_GB = 4


def _pow_select(e, pows):
    rr = jnp.where((e & 1) != 0, pows[0][0], 1.0)
    ri = jnp.where((e & 1) != 0, pows[0][1], 0.0)
    for b in range(1, len(pows)):
        bit = (e & (1 << b)) != 0
        fr = jnp.where(bit, pows[b][0], 1.0)
        fi = jnp.where(bit, pows[b][1], 0.0)
        rr, ri = _cmul(rr, ri, fr, fi)
    return rr, ri


def _discretize(a_re, a_im, log_dt):
    lr = jnp.minimum(a_re, -1e-4)
    li = a_im
    dt = jnp.exp(log_dt)
    mag = jnp.exp(lr * dt)
    br = mag * jnp.cos(li * dt)
    bi = mag * jnp.sin(li * dt)
    den = lr * lr + li * li
    nr = br - 1.0
    cr = (nr * lr + bi * li) / den
    ci = (bi * lr - nr * li) / den
    return (br, bi), (cr, ci)


def _squarings(pr, pi, n):
    out = [(pr, pi)]
    for _ in range(n):
        pr, pi = _cmul(pr, pi, pr, pi)
        out.append((pr, pi))
    return out


def _ssm_ops_kernel(rowp_ref, colp_ref, btr_ref, bti_ref, ctr_ref, cti_ref,
                    t_ref, s_ref, r_ref, lam_ref, powc_ref):
    lane_s = lax.broadcasted_iota(jnp.int32, (FLAT, DSTATE), 1)
    row_s = lax.broadcasted_iota(jnp.int32, (FLAT, DSTATE), 0)
    exp_s = jnp.where(lane_s < SSM_STATE, (CHUNK - 1) - (row_s >> 4), row_s >> 4)
    row_k = lax.broadcasted_iota(jnp.int32, (DSTATE, FLAT), 0)
    lane_k = lax.broadcasted_iota(jnp.int32, (DSTATE, FLAT), 1)
    exp_k = jnp.where(row_k < SSM_STATE, lane_k >> 4, (CHUNK - 1) - (lane_k >> 4))
    tile_e = jnp.where((lax.broadcasted_iota(jnp.int32, (SSM_GROUP, FLAT), 1) & (SSM_GROUP - 1))
                       == lax.broadcasted_iota(jnp.int32, (SSM_GROUP, FLAT), 0), 1.0, 0.0)
    lane_b = lax.broadcasted_iota(jnp.int32, (SSM_GROUP, DSTATE), 1)
    lane_t = lax.broadcasted_iota(jnp.int32, (SSM_GROUP, FLAT), 1)
    row_c = lax.broadcasted_iota(jnp.int32, (CH_PER_SEG, DSTATE), 0)
    lane_c = lax.broadcasted_iota(jnp.int32, (CH_PER_SEG, DSTATE), 1)
    exp_c = jnp.where(lane_c < SSM_STATE, row_c, (CH_PER_SEG - 1) - row_c)

    for g in range(SSM_GB):
        rp = rowp_ref[g]
        (lbr, lbi), (cfr, cfi) = _discretize(rp[0:1], rp[1:2], rp[2:3])
        sq = _squarings(lbr, lbi, 8)
        bbr, bbi = _cmul(btr_ref[g], bti_ref[g], cfr, cfi)

        pr, pi = _pow_select(exp_s, sq[:4])
        tb_r = jnp.concatenate([bbr] * CHUNK, axis=0)
        tb_i = jnp.concatenate([bbi] * CHUNK, axis=0)
        s_re, s_im = _cmul(tb_r, tb_i, pr, pi)
        s_ref[g] = jnp.concatenate([s_re, s_im], axis=1).astype(BF16)

        cp = colp_ref[g]
        (kbr, kbi), _ = _discretize(cp[:, 0:1], cp[:, 1:2], cp[:, 2:3])
        sqc = _squarings(kbr, kbi, 3)
        ctr = jnp.dot(ctr_ref[g], tile_e, preferred_element_type=F32, precision=HIGHEST)
        cti = jnp.dot(cti_ref[g], tile_e, preferred_element_type=F32, precision=HIGHEST)
        pkr, pki = _pow_select(exp_k, sqc)
        ykr, yki = _cmul(ctr, cti, pkr, pki)
        yrr, yri = _cmul(ykr, yki, kbr, kbi)
        r_ref[g] = jnp.concatenate([yrr, -yri], axis=0).astype(BF16)

        fwd = lane_b < SSM_STATE
        lhs = jnp.concatenate([jnp.where(fwd, bbr, 0.0), jnp.where(fwd, bbi, 0.0),
                               jnp.where(fwd, 0.0, bbr), jnp.where(fwd, 0.0, bbi)], axis=0)
        p1 = jnp.dot(lhs, ykr, preferred_element_type=F32, precision=HIGHEST)
        p2 = jnp.dot(lhs, yki, preferred_element_type=F32, precision=HIGHEST)
        kf = p1[0:16] - p2[16:32]
        kb = p1[32:48] - p2[48:64]
        for sp in range(CHUNK):
            tf = kf if sp == 0 else pltpu.roll(kf, SSM_GROUP * sp, axis=1)
            tf = jnp.where(lane_t >= SSM_GROUP * sp, tf, 0.0)
            shift = (FLAT - SSM_GROUP * (CHUNK - 1 - sp)) % FLAT
            tb = kb if shift == 0 else pltpu.roll(kb, shift, axis=1)
            tb = jnp.where(lane_t < SSM_GROUP * (sp + 1), tb, 0.0)
            t_ref[g, sp * SSM_GROUP:(sp + 1) * SSM_GROUP, :] = (tf + tb).astype(BF16)

        l16 = sq[4]
        l256 = sq[8]
        lam_ref[g] = jnp.concatenate([l16[0], l16[1], l256[0], l256[1],
                                      jnp.zeros((4, DSTATE), F32)], axis=0)
        q16 = _squarings(l16[0], l16[1], 3)
        pcr, pci = _pow_select(exp_c, q16)
        powc_ref[g, 0] = pcr
        powc_ref[g, 1] = pci


def _ssm_operators(rowp, colp, btr, bti, ctr, cti):
    G = SSM_GROUPS
    gb = SSM_GB
    return pl.pallas_call(
        _ssm_ops_kernel,
        grid=(G // gb,),
        in_specs=[pl.BlockSpec((gb, 3, DSTATE), lambda g: (g, 0, 0)),
                  pl.BlockSpec((gb, DSTATE, 3), lambda g: (g, 0, 0)),
                  pl.BlockSpec((gb, SSM_GROUP, DSTATE), lambda g: (g, 0, 0)),
                  pl.BlockSpec((gb, SSM_GROUP, DSTATE), lambda g: (g, 0, 0)),
                  pl.BlockSpec((gb, DSTATE, SSM_GROUP), lambda g: (g, 0, 0)),
                  pl.BlockSpec((gb, DSTATE, SSM_GROUP), lambda g: (g, 0, 0))],
        out_specs=[pl.BlockSpec((gb, FLAT, FLAT), lambda g: (g, 0, 0)),
                   pl.BlockSpec((gb, FLAT, FLAT), lambda g: (g, 0, 0)),
                   pl.BlockSpec((gb, FLAT, FLAT), lambda g: (g, 0, 0)),
                   pl.BlockSpec((gb, 8, DSTATE), lambda g: (g, 0, 0)),
                   pl.BlockSpec((gb, 2, CH_PER_SEG, DSTATE), lambda g: (g, 0, 0, 0))],
        out_shape=[jax.ShapeDtypeStruct((G, FLAT, FLAT), BF16),
                   jax.ShapeDtypeStruct((G, FLAT, FLAT), BF16),
                   jax.ShapeDtypeStruct((G, FLAT, FLAT), BF16),
                   jax.ShapeDtypeStruct((G, 8, DSTATE), F32),
                   jax.ShapeDtypeStruct((G, 2, CH_PER_SEG, DSTATE), F32)],
        compiler_params=_params("arbitrary"),
        name="ssm_operators",
    )(rowp, colp, btr, bti, ctr, cti)


SSM_PATHS = ((BATCH, SEQ // (CHUNK * CH_PER_SEG)), (DEC_BATCH, DEC_SEQ // (CHUNK * CH_PER_SEG)))


def _ssm_path(u32, s_op, t_op, r_op, lam, powc_ref, dskip, h0r, h0i, nseg):
    u = u32.astype(BF16)
    lane = lax.broadcasted_iota(jnp.int32, (N_SEG, DSTATE), 1)
    isf = lane < SSM_STATE
    l16r, l16i, l256r, l256i = lam[0:1], lam[1:2], lam[2:3], lam[3:4]

    z = jnp.dot(u, s_op, preferred_element_type=F32)
    zre, zim = z[:, :DSTATE], z[:, DSTATE:]

    hr = jnp.zeros((N_SEG, DSTATE), F32)
    hi = jnp.zeros((N_SEG, DSTATE), F32)
    hist = []
    for k in range(CH_PER_SEG):
        hist.append((hr, hi))
        kb = CH_PER_SEG - 1 - k
        zr = jnp.where(isf, zre[k * N_SEG:(k + 1) * N_SEG], zre[kb * N_SEG:(kb + 1) * N_SEG])
        zi = jnp.where(isf, zim[k * N_SEG:(k + 1) * N_SEG], zim[kb * N_SEG:(kb + 1) * N_SEG])
        nr, ni = _cmul(l16r, l16i, hr, hi)
        hr, hi = nr + zr, ni + zi

    if nseg > 1:
        h0r = jnp.broadcast_to(h0r[:, None, :], (N_SEG // nseg, nseg, DSTATE)).reshape(N_SEG, DSTATE)
        h0i = jnp.broadcast_to(h0i[:, None, :], (N_SEG // nseg, nseg, DSTATE)).reshape(N_SEG, DSTATE)
    seg = lax.broadcasted_iota(jnp.int32, (N_SEG, DSTATE), 0) & (nseg - 1)
    segpow = _squarings(l256r, l256i, max(int(math.log2(nseg)), 0))

    def shifted(x, d):
        dn = jnp.where(seg >= d, pltpu.roll(x, d, axis=0), 0.0)
        up = jnp.where(seg <= nseg - 1 - d, pltpu.roll(x, N_SEG - d, axis=0), 0.0)
        return jnp.where(isf, dn, up)

    pr, pi = hr, hi
    d = 1
    lvl = 0
    while d < nseg:
        ar, ai = _cmul(segpow[lvl][0], segpow[lvl][1], shifted(pr, d), shifted(pi, d))
        pr, pi = pr + ar, pi + ai
        d *= 2
        lvl += 1
    if nseg > 1:
        e_in = jnp.where(isf, seg, nseg - 1 - seg)
        wr, wi = _pow_select(e_in, segpow[:lvl])
        ar, ai = _cmul(wr, wi, h0r, h0i)
        hsr, hsi = shifted(pr, 1) + ar, shifted(pi, 1) + ai
    else:
        hsr, hsi = h0r, h0i
    er, ei = _cmul(segpow[lvl][0], segpow[lvl][1], h0r, h0i)
    fin = jnp.concatenate([pr + er, pi + ei], axis=1)

    rows = []
    for c in range(CH_PER_SEG):
        cb = CH_PER_SEG - 1 - c
        lr = jnp.where(isf, hist[c][0], hist[cb][0])
        li = jnp.where(isf, hist[c][1], hist[cb][1])
        ar, ai = _cmul(powc_ref[0, c:c + 1, :], powc_ref[1, c:c + 1, :], hsr, hsi)
        rows.append(jnp.concatenate([lr + ar, li + ai], axis=1))
    hent = jnp.concatenate(rows, axis=0).astype(BF16)

    y = (jnp.dot(u, t_op, preferred_element_type=F32)
         + jnp.dot(hent, r_op, preferred_element_type=F32)
         + u32 * dskip)
    return y, fin


GROUPS_PER_STEP = 128 // SSM_GROUP


def _block_transpose(arrs, blk):
    n = len(arrs)
    width = arrs[0].shape[1]
    j = lax.broadcasted_iota(jnp.int32, arrs[0].shape, 1) // blk
    k = n // 2
    while k >= 1:
        bit = (j & k) != 0
        new = list(arrs)
        for x in range(n):
            if x & k == 0:
                a, b = arrs[x], arrs[x | k]
                new[x] = jnp.where(bit, pltpu.roll(b, k * blk, axis=1), a)
                new[x | k] = jnp.where(bit, b, pltpu.roll(a, width - k * blk, axis=1))
        arrs = new
        k //= 2
    return arrs


def _ssm_kernel(u_ref, t_ref, s_ref, r_ref, lam_ref, powc_ref, d_ref, h0cr_ref, h0ci_ref, h0dr_ref, h0di_ref,
                y_ref, fin_ref, ubuf, ybuf):
    h0 = ((h0cr_ref, h0ci_ref), (h0dr_ref, h0di_ref))
    nhalf = FLAT // 128
    rows = N_SEG * CH_PER_SEG
    for path, (_, nseg) in enumerate(SSM_PATHS):
        row0 = path * rows * CHUNK
        for half in range(nhalf):
            toks = [u_ref[pl.ds(row0 + half * GROUPS_PER_STEP + sb, rows, stride=CHUNK), :]
                    for sb in range(GROUPS_PER_STEP)]
            grouped = _block_transpose(toks, SSM_GROUP)
            for g in range(GROUPS_PER_STEP):
                ubuf[g, :, half * 128:(half + 1) * 128] = grouped[g]

        def body(g, carry, path=path, nseg=nseg):
            y, fin = _ssm_path(ubuf[g], s_ref[g], t_ref[g], r_ref[g], lam_ref[g], powc_ref.at[g],
                               d_ref[g], h0[path][0][g], h0[path][1][g], nseg)
            ybuf[g] = y
            fin_ref[path, g] = fin
            return carry

        lax.fori_loop(0, GROUPS_PER_STEP, body, 0)

        for half in range(nhalf):
            grouped = [ybuf[g, :, half * 128:(half + 1) * 128] for g in range(GROUPS_PER_STEP)]
            toks = _block_transpose(grouped, SSM_GROUP)
            for sb in range(GROUPS_PER_STEP):
                y_ref[pl.ds(row0 + half * GROUPS_PER_STEP + sb, rows, stride=CHUNK), :] = toks[sb]


def _ssm_scan(u2d, t_op, s_op, r_op, lam, powc, dtile, h0c_re, h0c_im, h0d_re, h0d_im):
    G = SSM_GROUPS
    gs = GROUPS_PER_STEP
    rows = N_SEG * CH_PER_SEG
    return pl.pallas_call(
        _ssm_kernel,
        grid=(G // gs,),
        in_specs=[
            pl.BlockSpec((N_TOK, 128), lambda o: (0, o)),
            pl.BlockSpec((gs, FLAT, FLAT), lambda o: (o, 0, 0)),
            pl.BlockSpec((gs, FLAT, FLAT), lambda o: (o, 0, 0)),
            pl.BlockSpec((gs, FLAT, FLAT), lambda o: (o, 0, 0)),
            pl.BlockSpec((gs, 8, DSTATE), lambda o: (o, 0, 0)),
            pl.BlockSpec((gs, 2, CH_PER_SEG, DSTATE), lambda o: (o, 0, 0, 0)),
            pl.BlockSpec((gs, 1, FLAT), lambda o: (o, 0, 0)),
            pl.BlockSpec((gs, BATCH, DSTATE), lambda o: (o, 0, 0)),
            pl.BlockSpec((gs, BATCH, DSTATE), lambda o: (o, 0, 0)),
            pl.BlockSpec((gs, DEC_BATCH, DSTATE), lambda o: (o, 0, 0)),
            pl.BlockSpec((gs, DEC_BATCH, DSTATE), lambda o: (o, 0, 0))],
        out_specs=[pl.BlockSpec((N_TOK, 128), lambda o: (0, o)),
                   pl.BlockSpec((2, gs, N_SEG, 2 * DSTATE), lambda o: (0, o, 0, 0))],
        out_shape=[jax.ShapeDtypeStruct((N_TOK, SSM_WIDTH), F32),
                   jax.ShapeDtypeStruct((2, G, N_SEG, 2 * DSTATE), F32)],
        scratch_shapes=[pltpu.VMEM((gs, rows, FLAT), F32), pltpu.VMEM((gs, rows, FLAT), F32)],
        compiler_params=_params("arbitrary"),
        name="ssm_scan",
    )(u2d, t_op, s_op, r_op, lam, powc, dtile, h0c_re, h0c_im, h0d_re, h0d_im)


CTX_BB = 4


def _softmax_pv(scores, values):
    m = scores[0].max(axis=-1, keepdims=True)
    for s in scores[1:]:
        m = jnp.maximum(m, s.max(axis=-1, keepdims=True))
    l = None
    o = None
    for s, v in zip(scores, values):
        p = jnp.exp(s - m)
        ls = p.sum(axis=-1, keepdims=True)
        os_ = jnp.dot(p.astype(BF16), v, preferred_element_type=F32)
        l = ls if l is None else l + ls
        o = os_ if o is None else o + os_
    return o / l


def _ctx_attn_kernel(q_ref, k_ref, v_ref, za_ref, o_ref):
    head0 = lax.broadcasted_iota(jnp.int32, (1, 2 * HEAD_DIM), 1) < HEAD_DIM
    for b in range(CTX_BB):
        sl = slice(b * SEQ, (b + 1) * SEQ)
        q, k, v = q_ref[sl, :], k_ref[sl, :], v_ref[sl, :]
        outs = []
        for hh in range(2):
            qh = jnp.where(head0 if hh == 0 else jnp.logical_not(head0), q, jnp.zeros_like(q))
            outs.append(_softmax_pv([_dot_nt(qh, k)], [v]))
        o = jnp.where(head0, outs[0], outs[1])
        o_ref[sl, :] = (o * _silu(za_ref[sl, :].astype(F32))).astype(BF16)


HEAD_PAIRS_PER_TILE = PROJ_TN // (2 * HEAD_DIM)


def _ctx_attention(pb):
    rows = CTX_BB * SEQ
    hw = 2 * HEAD_DIM

    def tile(t):
        return pl.BlockSpec((rows, hw), lambda b, hp: (b, t * HEAD_PAIRS_PER_TILE + hp))

    return pl.pallas_call(
        _ctx_attn_kernel,
        grid=(BATCH // CTX_BB, N_HEADS // 2),
        in_specs=[tile(BF_Q), tile(BF_K), tile(BF_V), tile(BF_ZA)],
        out_specs=pl.BlockSpec((rows, hw), lambda b, hp: (b, hp)),
        out_shape=jax.ShapeDtypeStruct((N_CTX_TOK, ATT_WIDTH), BF16),
        compiler_params=_params("arbitrary", "arbitrary"),
        name="ctx_attention",
    )(pb, pb, pb, pb)


N_DROW = 2 * WIN_ROWS - 1


def _nbr_bias(bias_ref, hh, a):
    half0 = lax.broadcasted_iota(jnp.int32, (1, 2 * GRID_W), 1) < GRID_W
    ks = jnp.clip(Q_ROWS_PER_BLOCK * a - Q_ROWS_PER_BLOCK, 0, GRID_H - NBR_KROWS)
    rows = []
    for ri in range(Q_ROWS_PER_BLOCK):
        r = Q_ROWS_PER_BLOCK * a + ri
        rs = jnp.clip(r - WIN_ROWS // 2, 0, GRID_H - WIN_ROWS)
        tiles = []
        for pair in range(NBR_KROWS // 2):
            idx = []
            for kri in (2 * pair, 2 * pair + 1):
                kr = ks + kri
                valid = jnp.logical_and(kr >= rs, kr < rs + WIN_ROWS)
                idx.append(jnp.where(valid, kr - r + WIN_ROWS - 1, N_DROW))
            tiles.append(jnp.where(half0, bias_ref[hh, idx[0]], bias_ref[hh, idx[1]]))
        rows.append(jnp.concatenate(tiles, axis=1))
    return jnp.concatenate(rows, axis=0)


def _nbr_attn_kernel(q_ref, k_ref, v_ref, kc_ref, vc_ref, bias_ref, za_ref, o_ref):
    a = pl.program_id(1)
    start = pl.multiple_of(jnp.clip(a * NBR_Q - NBR_Q, 0, DEC_SEQ - NBR_K), NBR_Q)
    head0 = lax.broadcasted_iota(jnp.int32, (1, 2 * HEAD_DIM), 1) < HEAD_DIM
    bias = [_nbr_bias(bias_ref, hh, a) for hh in range(2)]
    for b in range(DEC_BATCH):
        q = q_ref[b]
        kw = k_ref[b, pl.ds(start, NBR_K), :]
        vw = v_ref[b, pl.ds(start, NBR_K), :]
        kc = kc_ref[b].astype(BF16)
        vc = vc_ref[b].astype(BF16)
        outs = []
        for hh in range(2):
            qh = jnp.where(head0 if hh == 0 else jnp.logical_not(head0), q, jnp.zeros_like(q))
            sw = _dot_nt(qh, kw) + bias[hh]
            sc = _dot_nt(qh, kc)
            outs.append(_softmax_pv([sw, sc], [vw, vc]))
        o = jnp.where(head0, outs[0], outs[1])
        o_ref[b] = (o * _silu(za_ref[b].astype(F32))).astype(BF16)


def _nbr_attention(pb4, kc, vc, bias):
    first = N_CTX_TOK // DEC_SEQ // DEC_BATCH
    nblk = GRID_H // Q_ROWS_PER_BLOCK
    hw = 2 * HEAD_DIM

    def qspec(t):
        return pl.BlockSpec((DEC_BATCH, NBR_Q, hw), lambda hp, a: (first, a, t * HEAD_PAIRS_PER_TILE + hp))

    def kspec(t):
        return pl.BlockSpec((DEC_BATCH, DEC_SEQ, hw), lambda hp, a: (first, 0, t * HEAD_PAIRS_PER_TILE + hp))

    cspec = pl.BlockSpec((DEC_BATCH, PAST_LEN, hw), lambda hp, a: (0, 0, hp))
    bspec = pl.BlockSpec((2, N_DROW + 1, GRID_W, 2 * GRID_W), lambda hp, a: (hp, 0, 0, 0))
    return pl.pallas_call(
        _nbr_attn_kernel,
        grid=(N_HEADS // 2, nblk),
        in_specs=[qspec(BF_Q), kspec(BF_K), kspec(BF_V), cspec, cspec, bspec, qspec(BF_ZA)],
        out_specs=pl.BlockSpec((DEC_BATCH, NBR_Q, hw), lambda hp, a: (0, a, hp)),
        out_shape=jax.ShapeDtypeStruct((DEC_BATCH, DEC_SEQ, ATT_WIDTH), BF16),
        compiler_params=_params("arbitrary", "arbitrary"),
        name="nbr_attention",
    )(pb4, pb4, pb4, kc, vc, bias, pb4)


def _nbr_bias_tables(rpb):
    w = GRID_W
    ncol = 2 * WIN_COLS - 1
    left = w - WIN_COLS
    band = jnp.pad(rpb, ((0, 0), (0, 0), (left, 2 * w - left - ncol)), constant_values=NEG_INF)
    skew = jnp.tile(band, (1, 1, w))[:, :, :w * (2 * w - 1)].reshape(N_HEADS, N_DROW, w, 2 * w - 1)
    blocks = skew[..., w - 1:]
    qc = np.arange(w)[:, None]
    kc = np.arange(w)[None, :]
    cs = np.clip(qc - WIN_COLS // 2, 0, w - WIN_COLS)
    valid = (kc >= cs) & (kc < cs + WIN_COLS)
    blocks = jnp.where(valid, blocks, NEG_INF)
    blocks = jnp.concatenate([blocks, jnp.full((N_HEADS, 1, w, w), NEG_INF, F32)], axis=1)
    return jnp.concatenate([blocks, blocks], axis=-1)


BACK_TM = CHUNK * CH_PER_SEG
BACK_CTX_STEPS = N_CTX_TOK // BACK_TM
BACK_STEPS_PER_DEC_SEQ = DEC_SEQ // BACK_TM


def _back_kernel(xp_ref, xs_ref, y_ref, zs_ref, ac_ref, ad_ref, gs_ref, ga_ref, gate_ref, bglu_ref,
                 wglu_ref, wso_ref, wao_ref, wo_ref, op_ref, os_ref):
    i = pl.program_id(0)

    def compute(x, a2):
        ys = _gelu_tanh(y_ref[0].reshape(BACK_TM, SSM_WIDTH))
        t = jnp.dot(ys.astype(BF16), wglu_ref[...], preferred_element_type=F32) + bglu_ref[...]
        ys = ys * _sigmoid(t) * _silu(zs_ref[...].astype(F32))
        p_s = jnp.dot(ys.astype(BF16), wso_ref[...], preferred_element_type=F32)
        p_a = jnp.dot(a2, wao_ref[...], preferred_element_type=F32)
        merged = (_sigmoid(gs_ref[...].astype(F32)) * p_s + _sigmoid(ga_ref[...].astype(F32)) * p_a)
        return x + gate_ref[0] * jnp.dot(merged.astype(BF16), wo_ref[...], preferred_element_type=F32)

    @pl.when(i < BACK_CTX_STEPS)
    def _():
        op_ref[...] = compute(xp_ref[...], ac_ref[...])

    @pl.when(i >= BACK_CTX_STEPS)
    def _():
        os_ref[...] = compute(xs_ref[...], ad_ref[...])


def _back(xp, xs, y, pb, a_ctx, a_dec, gate3, b_glu, w_glu, w_so, w_ao, w_o):
    steps = N_TOK // BACK_TM
    gtiles = D_MODEL // PROJ_TN
    n0 = BACK_CTX_STEPS
    row = functools.partial(_mod_row, ctx_steps=n0, steps_per_seq=BACK_STEPS_PER_DEC_SEQ)
    lo = lambda i: (jnp.minimum(i, n0 - 1), 0)
    hi = lambda i: (jnp.maximum(i - n0, 0), 0)
    cur = lambda i: (i, 0)
    const = lambda i: (0, 0)
    once = pl.Buffered(1)
    return pl.pallas_call(
        _back_kernel,
        grid=(steps,),
        in_specs=[pl.BlockSpec((BACK_TM, D_MODEL), lo),
                  pl.BlockSpec((BACK_TM, D_MODEL), hi),
                  pl.BlockSpec((1, CH_PER_SEG, None, CHUNK, SSM_WIDTH),
                               lambda i: (i // N_SEG, 0, i % N_SEG, 0, 0)),
                  pl.BlockSpec((BACK_TM, SSM_WIDTH), lambda i: (i, BF_ZS)),
                  pl.BlockSpec((BACK_TM, ATT_WIDTH), lo),
                  pl.BlockSpec((BACK_TM, ATT_WIDTH), hi),
                  pl.BlockSpec((BACK_TM, D_MODEL), lambda i: (i, BF_GS // gtiles)),
                  pl.BlockSpec((BACK_TM, D_MODEL), lambda i: (i, BF_GA // gtiles)),
                  pl.BlockSpec((1, 1, D_MODEL), lambda i: (row(i), 0, 0)),
                  pl.BlockSpec((1, SSM_WIDTH), const),
                  pl.BlockSpec((SSM_WIDTH, SSM_WIDTH), const, pipeline_mode=once),
                  pl.BlockSpec((SSM_WIDTH, D_MODEL), const, pipeline_mode=once),
                  pl.BlockSpec((ATT_WIDTH, D_MODEL), const, pipeline_mode=once),
                  pl.BlockSpec((D_MODEL, D_MODEL), const, pipeline_mode=once)],
        out_specs=[pl.BlockSpec((BACK_TM, D_MODEL), lo),
                   pl.BlockSpec((BACK_TM, D_MODEL), hi)],
        out_shape=[jax.ShapeDtypeStruct((N_CTX_TOK, D_MODEL), F32),
                   jax.ShapeDtypeStruct((N_DEC_TOK, D_MODEL), F32)],
        compiler_params=_params("arbitrary"),
        name="gated_output",
    )(xp, xs, y, pb, a_ctx, a_dec, pb, pb, gate3, b_glu.reshape(1, SSM_WIDTH), w_glu, w_so, w_ao, w_o)


def _layer(xp, xs, cache_k, cache_v, st_re, st_im, c, c_ctx, norm_w, w_ada, b_ada, w_in, q_norm_w, k_norm_w,
           rpb, a_re, a_im, log_dt, b_re, b_im, c_re, c_im, d, w_glu, b_glu, w_so, w_ao, w_o):
    G, P = SSM_GROUPS, SSM_STATE

    cond8 = jnp.zeros((8, D_MODEL), F32).at[0].set(c_ctx).at[1:1 + DEC_BATCH].set(c)
    mod = _modulation(cond8, w_ada, b_ada)
    shift3 = mod[:1 + DEC_BATCH, None, :D_MODEL]
    scale3 = mod[:1 + DEC_BATCH, None, D_MODEL:2 * D_MODEL]
    gate3 = mod[:1 + DEC_BATCH, None, 2 * D_MODEL:]

    h = _norm_modulate(xp, xs, shift3, scale3, norm_w)
    u, pb, kv32 = _in_proj(h, w_in, q_norm_w, k_norm_w)
    k_ctx32 = kv32[:, :ATT_WIDTH]
    v_ctx32 = kv32[:, ATT_WIDTH:]

    arow = jnp.transpose(a_re, (1, 0, 2)).reshape(G, 2 * P)
    airow = jnp.transpose(a_im, (1, 0, 2)).reshape(G, 2 * P)
    dtrow = jnp.broadcast_to(jnp.transpose(log_dt, (1, 0))[:, :, None], (G, 2, P)).reshape(G, 2 * P)
    rowp = jnp.stack([arow, airow, dtrow], axis=1)
    colp = jnp.transpose(rowp, (0, 2, 1))
    btr = jnp.transpose(b_re, (1, 3, 0, 2)).reshape(G, SSM_GROUP, 2 * P)
    bti = jnp.transpose(b_im, (1, 3, 0, 2)).reshape(G, SSM_GROUP, 2 * P)
    ctr = jnp.transpose(c_re, (1, 0, 3, 2)).reshape(G, 2 * P, SSM_GROUP)
    cti = jnp.transpose(c_im, (1, 0, 3, 2)).reshape(G, 2 * P, SSM_GROUP)
    t_op, s_op, r_op, lam, powc = _ssm_operators(rowp, colp, btr, bti, ctr, cti)

    dtile = jnp.tile(d.reshape(G, 1, SSM_GROUP), (1, 1, CHUNK))
    h0c = jnp.zeros((G, BATCH, 2 * P), F32)
    h0d_re = jnp.transpose(st_re, (2, 0, 1, 3)).reshape(G, DEC_BATCH, 2 * P)
    h0d_im = jnp.transpose(st_im, (2, 0, 1, 3)).reshape(G, DEC_BATCH, 2 * P)
    y2d, fin = _ssm_scan(u.reshape(N_TOK, SSM_WIDTH), t_op, s_op, r_op, lam, powc, dtile,
                         h0c, h0c, h0d_re, h0d_im)
    y = y2d.reshape(2, CH_PER_SEG, N_SEG, CHUNK, SSM_WIDTH)

    a_ctx = _ctx_attention(pb)
    a_dec = _nbr_attention(pb.reshape(N_TOK // DEC_SEQ, DEC_SEQ, pb.shape[1]),
                           cache_k.reshape(DEC_BATCH, PAST_LEN, ATT_WIDTH),
                           cache_v.reshape(DEC_BATCH, PAST_LEN, ATT_WIDTH),
                           _nbr_bias_tables(rpb.astype(F32)))

    yp, ys_out = _back(xp, xs, y, pb, a_ctx, a_dec.reshape(N_DEC_TOK, ATT_WIDTH), gate3, b_glu,
                       w_glu.astype(BF16), w_so.astype(BF16), w_ao.astype(BF16), w_o.astype(BF16))

    fin_ctx = fin[0]
    new_re = jnp.transpose(fin_ctx[:, :, :2 * P].reshape(G, BATCH, 2, P), (1, 2, 0, 3))
    new_im = jnp.transpose(fin_ctx[:, :, 2 * P:].reshape(G, BATCH, 2, P), (1, 2, 0, 3))
    return yp, ys_out, k_ctx32, v_ctx32, new_re, new_im


def kernel(x_prompt, x_sample, cache_k, cache_v, state_ssm_re, state_ssm_im, c, c_ctx, norm_w, w_ada, b_ada,
           w_in, q_norm_w, k_norm_w, rel_pos_bias, ssm_a_re, ssm_a_im, ssm_log_dt, ssm_b_re, ssm_b_im,
           ssm_c_re, ssm_c_im, ssm_d, w_glu, b_glu, w_ssm_out, w_att_out, w_o):
    depth = norm_w.shape[0]
    xp = x_prompt.reshape(N_CTX_TOK, D_MODEL)
    xs = x_sample.reshape(N_DEC_TOK, D_MODEL)
    new_k, new_v, new_re, new_im = [], [], [], []
    for l in range(depth):
        xp, xs, kl, vl, rl, il = _layer(
            xp, xs, cache_k[:, l], cache_v[:, l], state_ssm_re[:, l], state_ssm_im[:, l], c, c_ctx,
            norm_w[l], w_ada[l], b_ada[l], w_in[l], q_norm_w[l], k_norm_w[l], rel_pos_bias[l],
            ssm_a_re[l], ssm_a_im[l], ssm_log_dt[l], ssm_b_re[l], ssm_b_im[l], ssm_c_re[l], ssm_c_im[l],
            ssm_d[l], w_glu[l], b_glu[l], w_ssm_out[l], w_att_out[l], w_o[l])
        new_k.append(kl.reshape(BATCH, SEQ, N_HEADS, HEAD_DIM))
        new_v.append(vl.reshape(BATCH, SEQ, N_HEADS, HEAD_DIM))
        new_re.append(rl)
        new_im.append(il)
    return (xp.reshape(BATCH, SEQ, D_MODEL), xs.reshape(DEC_BATCH, DEC_SEQ, D_MODEL),
            jnp.stack(new_k, axis=1), jnp.stack(new_v, axis=1),
            jnp.stack(new_re, axis=1), jnp.stack(new_im, axis=1))
```

```python
import functools
import math

import jax
import jax.numpy as jnp
from jax import lax
from jax.experimental import pallas as pl
from jax.experimental.pallas import tpu as pltpu

D_MODEL = 2048
BATCH = 16
SEQ = 256
DEC_BATCH = 2
DEC_SEQ = 2048
PAST_LEN = 512
GRID_W = 64
GRID_H = DEC_SEQ // GRID_W
SSM_WIDTH = D_MODEL // 2
SSM_GROUP = 16
SSM_GROUPS = SSM_WIDTH // SSM_GROUP
SSM_STATE = 64
N_HEADS = 16
HEAD_DIM = 64
ATT_WIDTH = N_HEADS * HEAD_DIM
WIN_ROWS = 8
WIN_COLS = 16
EPS = 1e-6
NEG_INF = -1e30

N_CTX_TOK = BATCH * SEQ
N_DEC_TOK = DEC_BATCH * DEC_SEQ
N_TOK = N_CTX_TOK + N_DEC_TOK

OFF_U = 0
OFF_ZS = SSM_WIDTH
OFF_Q = 2 * SSM_WIDTH
OFF_K = OFF_Q + ATT_WIDTH
OFF_V = OFF_K + ATT_WIDTH
OFF_ZA = OFF_V + ATT_WIDTH
OFF_GS = OFF_ZA + ATT_WIDTH
OFF_GA = OFF_GS + D_MODEL

CHUNK = 16
FLAT = CHUNK * SSM_GROUP
N_SEG = 16
CH_PER_SEG = 16
DSTATE = 2 * SSM_STATE

Q_ROWS_PER_BLOCK = 4
NBR_Q = Q_ROWS_PER_BLOCK * GRID_W
NBR_KROWS = 12
NBR_K = NBR_KROWS * GRID_W

VMEM_LIMIT = 56 * 1024 * 1024

F32 = jnp.float32
BF16 = jnp.bfloat16
HIGHEST = lax.Precision.HIGHEST


def _sigmoid(x):
    return 1.0 / (1.0 + jnp.exp(-x))


def _silu(x):
    return x * _sigmoid(x)


def _gelu_tanh(x):
    return 0.5 * x * (1.0 + jnp.tanh(math.sqrt(2.0 / math.pi) * (x + 0.044715 * (x * x * x))))


def _cmul(ar, ai, br, bi):
    return ar * br - ai * bi, ar * bi + ai * br


def _dot_nt(a, b):
    return lax.dot_general(a, b, (((1,), (1,)), ((), ())), preferred_element_type=F32)


def _params(*sem):
    return pltpu.CompilerParams(dimension_semantics=sem, vmem_limit_bytes=VMEM_LIMIT)


def _mod_kernel(cond_ref, w_ref, b_ref, o_ref):
    c = cond_ref[...]
    s = _silu(c).astype(BF16)
    o_ref[...] = jnp.dot(s, w_ref[...].astype(BF16), preferred_element_type=F32) + b_ref[...]


def _modulation(cond8, w_ada, b_ada):
    tn = 768
    n = w_ada.shape[1]
    return pl.pallas_call(
        _mod_kernel,
        grid=(n // tn,),
        in_specs=[pl.BlockSpec((8, D_MODEL), lambda j: (0, 0)),
                  pl.BlockSpec((D_MODEL, tn), lambda j: (0, j)),
                  pl.BlockSpec((1, tn), lambda j: (0, j))],
        out_specs=pl.BlockSpec((8, tn), lambda j: (0, j)),
        out_shape=jax.ShapeDtypeStruct((8, n), F32),
        compiler_params=_params("arbitrary"),
        name="modulation",
    )(cond8, w_ada, b_ada.reshape(1, n))


NORM_TM = 512
NORM_CTX_STEPS = N_CTX_TOK // NORM_TM
NORM_STEPS_PER_DEC_SEQ = DEC_SEQ // NORM_TM


def _mod_row(i, ctx_steps, steps_per_seq):
    return jnp.where(i < ctx_steps, 0, 1 + (i - ctx_steps) // steps_per_seq)


def _norm_kernel(xp_ref, xs_ref, shift_ref, scale_ref, nw_ref, o_ref):
    i = pl.program_id(0)

    def body(x):
        ms = jnp.mean(x * x, axis=-1, keepdims=True)
        y = x * lax.rsqrt(ms + EPS) * nw_ref[...]
        o_ref[...] = (y * (1.0 + scale_ref[0]) + shift_ref[0]).astype(BF16)

    @pl.when(i < NORM_CTX_STEPS)
    def _():
        body(xp_ref[...])

    @pl.when(i >= NORM_CTX_STEPS)
    def _():
        body(xs_ref[...])


def _norm_modulate(xp, xs, shift3, scale3, norm_w):
    steps = N_TOK // NORM_TM
    row = functools.partial(_mod_row, ctx_steps=NORM_CTX_STEPS, steps_per_seq=NORM_STEPS_PER_DEC_SEQ)
    return pl.pallas_call(
        _norm_kernel,
        grid=(steps,),
        in_specs=[pl.BlockSpec((NORM_TM, D_MODEL), lambda i: (jnp.minimum(i, NORM_CTX_STEPS - 1), 0)),
                  pl.BlockSpec((NORM_TM, D_MODEL), lambda i: (jnp.maximum(i - NORM_CTX_STEPS, 0), 0)),
                  pl.BlockSpec((1, 1, D_MODEL), lambda i: (row(i), 0, 0)),
                  pl.BlockSpec((1, 1, D_MODEL), lambda i: (row(i), 0, 0)),
                  pl.BlockSpec((1, D_MODEL), lambda i: (0, 0))],
        out_specs=pl.BlockSpec((NORM_TM, D_MODEL), lambda i: (i, 0)),
        out_shape=jax.ShapeDtypeStruct((N_TOK, D_MODEL), BF16),
        compiler_params=_params("arbitrary"),
        name="norm_modulate",
    )(xp, xs, shift3, scale3, norm_w.reshape(1, D_MODEL))


PROJ_TM = 1024
PROJ_TN = 1024
PROJ_CTX_STEPS = N_CTX_TOK // PROJ_TM
PROJ_SEGS = PROJ_TM // (CHUNK * CH_PER_SEG)


def _head_group_ones():
    r = lax.broadcasted_iota(jnp.int32, (256, 256), 0) // HEAD_DIM
    c = lax.broadcasted_iota(jnp.int32, (256, 256), 1) // HEAD_DIM
    return jnp.where(r == c, 1.0, 0.0).astype(BF16)


def _head_rms(acc, nw):
    ones = _head_group_ones()
    outs = []
    for c in range(acc.shape[1] // 256):
        a = acc[:, c * 256:(c + 1) * 256]
        ssum = jnp.dot((a * a).astype(BF16), ones, preferred_element_type=F32)
        outs.append(a * lax.rsqrt(ssum * (1.0 / HEAD_DIM) + EPS))
    return jnp.concatenate(outs, axis=1) * nw


def _proj_kernel(*refs, mode):
    if mode in ("plain", "u"):
        h_ref, w_ref, o_ref, wbf = refs
    elif mode == "q":
        h_ref, w_ref, nw_ref, o_ref, wbf = refs
    elif mode == "k":
        h_ref, w_ref, nw_ref, o_ref, of_ref, wbf = refs
    else:
        h_ref, w_ref, o_ref, of_ref, wbf = refs
    i = pl.program_id(1)

    @pl.when(i == 0)
    def _():
        wbf[...] = w_ref[...].astype(BF16)

    acc = jnp.dot(h_ref[...], wbf[...], preferred_element_type=F32)
    if mode == "q":
        acc = _head_rms(acc, nw_ref[...]) * (HEAD_DIM ** -0.5)
    elif mode == "k":
        acc = _head_rms(acc, nw_ref[...])
    if mode == "u":
        for seg in range(PROJ_SEGS):
            for ch in range(CH_PER_SEG):
                r0 = (seg * CH_PER_SEG + ch) * CHUNK
                o_ref[0, ch, seg] = acc[r0:r0 + CHUNK, :]
    else:
        o_ref[...] = acc.astype(BF16)
    if mode in ("k", "v"):
        @pl.when(i < PROJ_CTX_STEPS)
        def _():
            of_ref[...] = acc


def _in_proj(h, w_in, col_off, ncols, mode, norm_w=None):
    nj = ncols // PROJ_TN
    ni = N_TOK // PROJ_TM
    joff = col_off // PROJ_TN
    in_specs = [pl.BlockSpec((PROJ_TM, D_MODEL), lambda j, i: (i, 0)),
                pl.BlockSpec((D_MODEL, PROJ_TN), lambda j, i: (0, joff + j))]
    args = [h, w_in]
    if mode in ("q", "k"):
        in_specs.append(pl.BlockSpec((1, PROJ_TN), lambda j, i: (0, 0)))
        args.append(jnp.tile(norm_w.reshape(1, HEAD_DIM), (1, PROJ_TN // HEAD_DIM)))
    out_specs = [pl.BlockSpec((PROJ_TM, PROJ_TN), lambda j, i: (i, j))]
    out_shape = [jax.ShapeDtypeStruct((N_TOK, ncols), BF16)]
    if mode == "u":
        tiles_per_path = N_SEG // PROJ_SEGS
        out_specs = [pl.BlockSpec((1, CH_PER_SEG, PROJ_SEGS, CHUNK, PROJ_TN),
                                  lambda j, i: (i // tiles_per_path, 0, i % tiles_per_path, 0, j))]
        out_shape = [jax.ShapeDtypeStruct((2, CH_PER_SEG, N_SEG, CHUNK, ncols), F32)]
    if mode in ("k", "v"):
        out_specs.append(pl.BlockSpec((PROJ_TM, PROJ_TN),
                                      lambda j, i: (jnp.minimum(i, PROJ_CTX_STEPS - 1), j)))
        out_shape.append(jax.ShapeDtypeStruct((N_CTX_TOK, ncols), F32))
    res = pl.pallas_call(
        functools.partial(_proj_kernel, mode=mode),
        grid=(nj, ni),
        in_specs=in_specs,
        out_specs=out_specs,
        out_shape=out_shape,
        scratch_shapes=[pltpu.VMEM((D_MODEL, PROJ_TN), BF16)],
        compiler_params=_params("arbitrary", "arbitrary"),
        name="in_proj_" + mode,
    )(*args)
    return res if mode in ("k", "v") else res[0]


SSM_GB = 4


def _pow_select(e, pows):
    rr = jnp.where((e & 1) != 0, pows[0][0], 1.0)
    ri = jnp.where((e & 1) != 0, pows[0][1], 0.0)
    for b in range(1, len(pows)):
        bit = (e & (1 << b)) != 0
        fr = jnp.where(bit, pows[b][0], 1.0)
        fi = jnp.where(bit, pows[b][1], 0.0)
        rr, ri = _cmul(rr, ri, fr, fi)
    return rr, ri


def _discretize(a_re, a_im, log_dt):
    lr = jnp.minimum(a_re, -1e-4)
    li = a_im
    dt = jnp.exp(log_dt)
    mag = jnp.exp(lr * dt)
    br = mag * jnp.cos(li * dt)
    bi = mag * jnp.sin(li * dt)
    den = lr * lr + li * li
    nr = br - 1.0
    cr = (nr * lr + bi * li) / den
    ci = (bi * lr - nr * li) / den
    return (br, bi), (cr, ci)


def _squarings(pr, pi, n):
    out = [(pr, pi)]
    for _ in range(n):
        pr, pi = _cmul(pr, pi, pr, pi)
        out.append((pr, pi))
    return out


def _ssm_ops_kernel(rowp_ref, colp_ref, btr_ref, bti_ref, ctr_ref, cti_ref,
                    t_ref, s_ref, r_ref, lam_ref, powc_ref):
    lane_s = lax.broadcasted_iota(jnp.int32, (FLAT, DSTATE), 1)
    row_s = lax.broadcasted_iota(jnp.int32, (FLAT, DSTATE), 0)
    exp_s = jnp.where(lane_s < SSM_STATE, (CHUNK - 1) - (row_s >> 4), row_s >> 4)
    row_k = lax.broadcasted_iota(jnp.int32, (DSTATE, FLAT), 0)
    lane_k = lax.broadcasted_iota(jnp.int32, (DSTATE, FLAT), 1)
    exp_k = jnp.where(row_k < SSM_STATE, lane_k >> 4, (CHUNK - 1) - (lane_k >> 4))
    tile_e = jnp.where((lax.broadcasted_iota(jnp.int32, (SSM_GROUP, FLAT), 1) & (SSM_GROUP - 1))
                       == lax.broadcasted_iota(jnp.int32, (SSM_GROUP, FLAT), 0), 1.0, 0.0)
    lane_b = lax.broadcasted_iota(jnp.int32, (SSM_GROUP, DSTATE), 1)
    lane_t = lax.broadcasted_iota(jnp.int32, (SSM_GROUP, FLAT), 1)
    row_c = lax.broadcasted_iota(jnp.int32, (CH_PER_SEG, DSTATE), 0)
    lane_c = lax.broadcasted_iota(jnp.int32, (CH_PER_SEG, DSTATE), 1)
    exp_c = jnp.where(lane_c < SSM_STATE, row_c, (CH_PER_SEG - 1) - row_c)

    for g in range(SSM_GB):
        rp = rowp_ref[g]
        (lbr, lbi), (cfr, cfi) = _discretize(rp[0:1], rp[1:2], rp[2:3])
        sq = _squarings(lbr, lbi, 8)
        bbr, bbi = _cmul(btr_ref[g], bti_ref[g], cfr, cfi)

        pr, pi = _pow_select(exp_s, sq[:4])
        tb_r = jnp.concatenate([bbr] * CHUNK, axis=0)
        tb_i = jnp.concatenate([bbi] * CHUNK, axis=0)
        s_re, s_im = _cmul(tb_r, tb_i, pr, pi)
        s_ref[g] = jnp.concatenate([s_re, s_im], axis=1).astype(BF16)

        cp = colp_ref[g]
        (kbr, kbi), _ = _discretize(cp[:, 0:1], cp[:, 1:2], cp[:, 2:3])
        sqc = _squarings(kbr, kbi, 3)
        ctr = jnp.dot(ctr_ref[g], tile_e, preferred_element_type=F32, precision=HIGHEST)
        cti = jnp.dot(cti_ref[g], tile_e, preferred_element_type=F32, precision=HIGHEST)
        pkr, pki = _pow_select(exp_k, sqc)
        ykr, yki = _cmul(ctr, cti, pkr, pki)
        yrr, yri = _cmul(ykr, yki, kbr, kbi)
        r_ref[g] = jnp.concatenate([yrr, -yri], axis=0).astype(BF16)

        fwd = lane_b < SSM_STATE
        lhs = jnp.concatenate([jnp.where(fwd, bbr, 0.0), jnp.where(fwd, bbi, 0.0),
                               jnp.where(fwd, 0.0, bbr), jnp.where(fwd, 0.0, bbi)], axis=0)
        p1 = jnp.dot(lhs, ykr, preferred_element_type=F32, precision=HIGHEST)
        p2 = jnp.dot(lhs, yki, preferred_element_type=F32, precision=HIGHEST)
        kf = p1[0:16] - p2[16:32]
        kb = p1[32:48] - p2[48:64]
        for sp in range(CHUNK):
            tf = kf if sp == 0 else pltpu.roll(kf, SSM_GROUP * sp, axis=1)
            tf = jnp.where(lane_t >= SSM_GROUP * sp, tf, 0.0)
            shift = (FLAT - SSM_GROUP * (CHUNK - 1 - sp)) % FLAT
            tb = kb if shift == 0 else pltpu.roll(kb, shift, axis=1)
            tb = jnp.where(lane_t < SSM_GROUP * (sp + 1), tb, 0.0)
            t_ref[g, sp * SSM_GROUP:(sp + 1) * SSM_GROUP, :] = (tf + tb).astype(BF16)

        l16 = sq[4]
        l256 = sq[8]
        lam_ref[g] = jnp.concatenate([l16[0], l16[1], l256[0], l256[1],
                                      jnp.zeros((4, DSTATE), F32)], axis=0)
        q16 = _squarings(l16[0], l16[1], 3)
        pcr, pci = _pow_select(exp_c, q16)
        powc_ref[g, 0] = pcr
        powc_ref[g, 1] = pci


def _ssm_operators(rowp, colp, btr, bti, ctr, cti):
    G = SSM_GROUPS
    gb = SSM_GB
    return pl.pallas_call(
        _ssm_ops_kernel,
        grid=(G // gb,),
        in_specs=[pl.BlockSpec((gb, 3, DSTATE), lambda g: (g, 0, 0)),
                  pl.BlockSpec((gb, DSTATE, 3), lambda g: (g, 0, 0)),
                  pl.BlockSpec((gb, SSM_GROUP, DSTATE), lambda g: (g, 0, 0)),
                  pl.BlockSpec((gb, SSM_GROUP, DSTATE), lambda g: (g, 0, 0)),
                  pl.BlockSpec((gb, DSTATE, SSM_GROUP), lambda g: (g, 0, 0)),
                  pl.BlockSpec((gb, DSTATE, SSM_GROUP), lambda g: (g, 0, 0))],
        out_specs=[pl.BlockSpec((gb, FLAT, FLAT), lambda g: (g, 0, 0)),
                   pl.BlockSpec((gb, FLAT, FLAT), lambda g: (g, 0, 0)),
                   pl.BlockSpec((gb, FLAT, FLAT), lambda g: (g, 0, 0)),
                   pl.BlockSpec((gb, 8, DSTATE), lambda g: (g, 0, 0)),
                   pl.BlockSpec((gb, 2, CH_PER_SEG, DSTATE), lambda g: (g, 0, 0, 0))],
        out_shape=[jax.ShapeDtypeStruct((G, FLAT, FLAT), BF16),
                   jax.ShapeDtypeStruct((G, FLAT, FLAT), BF16),
                   jax.ShapeDtypeStruct((G, FLAT, FLAT), BF16),
                   jax.ShapeDtypeStruct((G, 8, DSTATE), F32),
                   jax.ShapeDtypeStruct((G, 2, CH_PER_SEG, DSTATE), F32)],
        compiler_params=_params("arbitrary"),
        name="ssm_operators",
    )(rowp, colp, btr, bti, ctr, cti)


SSM_PATHS = ((BATCH, SEQ // (CHUNK * CH_PER_SEG)), (DEC_BATCH, DEC_SEQ // (CHUNK * CH_PER_SEG)))


def _ssm_path(u32, s_op, t_op, r_op, lam, powc_ref, dskip, h0r, h0i, nseg):
    u = u32.astype(BF16)
    lane = lax.broadcasted_iota(jnp.int32, (N_SEG, DSTATE), 1)
    isf = lane < SSM_STATE
    l16r, l16i, l256r, l256i = lam[0:1], lam[1:2], lam[2:3], lam[3:4]

    z = jnp.dot(u, s_op, preferred_element_type=F32)
    zre, zim = z[:, :DSTATE], z[:, DSTATE:]

    hr = jnp.zeros((N_SEG, DSTATE), F32)
    hi = jnp.zeros((N_SEG, DSTATE), F32)
    hist = []
    for k in range(CH_PER_SEG):
        hist.append((hr, hi))
        kb = CH_PER_SEG - 1 - k
        zr = jnp.where(isf, zre[k * N_SEG:(k + 1) * N_SEG], zre[kb * N_SEG:(kb + 1) * N_SEG])
        zi = jnp.where(isf, zim[k * N_SEG:(k + 1) * N_SEG], zim[kb * N_SEG:(kb + 1) * N_SEG])
        nr, ni = _cmul(l16r, l16i, hr, hi)
        hr, hi = nr + zr, ni + zi

    if nseg > 1:
        h0r = jnp.broadcast_to(h0r[:, None, :], (N_SEG // nseg, nseg, DSTATE)).reshape(N_SEG, DSTATE)
        h0i = jnp.broadcast_to(h0i[:, None, :], (N_SEG // nseg, nseg, DSTATE)).reshape(N_SEG, DSTATE)
    seg = lax.broadcasted_iota(jnp.int32, (N_SEG, DSTATE), 0) & (nseg - 1)
    segpow = _squarings(l256r, l256i, max(int(math.log2(nseg)), 0))

    def shifted(x, d):
        dn = jnp.where(seg >= d, pltpu.roll(x, d, axis=0), 0.0)
        up = jnp.where(seg <= nseg - 1 - d, pltpu.roll(x, N_SEG - d, axis=0), 0.0)
        return jnp.where(isf, dn, up)

    pr, pi = hr, hi
    d = 1
    lvl = 0
    while d < nseg:
        ar, ai = _cmul(segpow[lvl][0], segpow[lvl][1], shifted(pr, d), shifted(pi, d))
        pr, pi = pr + ar, pi + ai
        d *= 2
        lvl += 1
    if nseg > 1:
        e_in = jnp.where(isf, seg, nseg - 1 - seg)
        wr, wi = _pow_select(e_in, segpow[:lvl])
        ar, ai = _cmul(wr, wi, h0r, h0i)
        hsr, hsi = shifted(pr, 1) + ar, shifted(pi, 1) + ai
    else:
        hsr, hsi = h0r, h0i
    er, ei = _cmul(segpow[lvl][0], segpow[lvl][1], h0r, h0i)
    fin = jnp.concatenate([pr + er, pi + ei], axis=1)

    rows = []
    for c in range(CH_PER_SEG):
        cb = CH_PER_SEG - 1 - c
        lr = jnp.where(isf, hist[c][0], hist[cb][0])
        li = jnp.where(isf, hist[c][1], hist[cb][1])
        ar, ai = _cmul(powc_ref[0, c:c + 1, :], powc_ref[1, c:c + 1, :], hsr, hsi)
        rows.append(jnp.concatenate([lr + ar, li + ai], axis=1))
    hent = jnp.concatenate(rows, axis=0).astype(BF16)

    y = (jnp.dot(u, t_op, preferred_element_type=F32)
         + jnp.dot(hent, r_op, preferred_element_type=F32)
         + u32 * dskip)
    return y, fin


GROUPS_PER_STEP = 128 // SSM_GROUP


def _block_transpose(arrs, blk):
    n = len(arrs)
    width = arrs[0].shape[1]
    j = lax.broadcasted_iota(jnp.int32, arrs[0].shape, 1) // blk
    k = n // 2
    while k >= 1:
        bit = (j & k) != 0
        new = list(arrs)
        for x in range(n):
            if x & k == 0:
                a, b = arrs[x], arrs[x | k]
                new[x] = jnp.where(bit, pltpu.roll(b, k * blk, axis=1), a)
                new[x | k] = jnp.where(bit, b, pltpu.roll(a, width - k * blk, axis=1))
        arrs = new
        k //= 2
    return arrs


def _ssm_kernel(u_ref, t_ref, s_ref, r_ref, lam_ref, powc_ref, d_ref, h0cr_ref, h0ci_ref, h0dr_ref, h0di_ref,
                y_ref, fin_ref, ubuf, ybuf):
    h0 = ((h0cr_ref, h0ci_ref), (h0dr_ref, h0di_ref))
    nhalf = FLAT // 128
    rows = N_SEG * CH_PER_SEG
    for path, (_, nseg) in enumerate(SSM_PATHS):
        row0 = path * rows * CHUNK
        for half in range(nhalf):
            toks = [u_ref[pl.ds(row0 + half * GROUPS_PER_STEP + sb, rows, stride=CHUNK), :]
                    for sb in range(GROUPS_PER_STEP)]
            grouped = _block_transpose(toks, SSM_GROUP)
            for g in range(GROUPS_PER_STEP):
                ubuf[g, :, half * 128:(half + 1) * 128] = grouped[g]

        def body(g, carry, path=path, nseg=nseg):
            y, fin = _ssm_path(ubuf[g], s_ref[g], t_ref[g], r_ref[g], lam_ref[g], powc_ref.at[g],
                               d_ref[g], h0[path][0][g], h0[path][1][g], nseg)
            ybuf[g] = y
            fin_ref[path, g] = fin
            return carry

        lax.fori_loop(0, GROUPS_PER_STEP, body, 0)

        for half in range(nhalf):
            grouped = [ybuf[g, :, half * 128:(half + 1) * 128] for g in range(GROUPS_PER_STEP)]
            toks = _block_transpose(grouped, SSM_GROUP)
            for sb in range(GROUPS_PER_STEP):
                y_ref[pl.ds(row0 + half * GROUPS_PER_STEP + sb, rows, stride=CHUNK), :] = toks[sb]


def _ssm_scan(u2d, t_op, s_op, r_op, lam, powc, dtile, h0c_re, h0c_im, h0d_re, h0d_im):
    G = SSM_GROUPS
    gs = GROUPS_PER_STEP
    rows = N_SEG * CH_PER_SEG
    return pl.pallas_call(
        _ssm_kernel,
        grid=(G // gs,),
        in_specs=[
            pl.BlockSpec((N_TOK, 128), lambda o: (0, o)),
            pl.BlockSpec((gs, FLAT, FLAT), lambda o: (o, 0, 0)),
            pl.BlockSpec((gs, FLAT, FLAT), lambda o: (o, 0, 0)),
            pl.BlockSpec((gs, FLAT, FLAT), lambda o: (o, 0, 0)),
            pl.BlockSpec((gs, 8, DSTATE), lambda o: (o, 0, 0)),
            pl.BlockSpec((gs, 2, CH_PER_SEG, DSTATE), lambda o: (o, 0, 0, 0)),
            pl.BlockSpec((gs, 1, FLAT), lambda o: (o, 0, 0)),
            pl.BlockSpec((gs, BATCH, DSTATE), lambda o: (o, 0, 0)),
            pl.BlockSpec((gs, BATCH, DSTATE), lambda o: (o, 0, 0)),
            pl.BlockSpec((gs, DEC_BATCH, DSTATE), lambda o: (o, 0, 0)),
            pl.BlockSpec((gs, DEC_BATCH, DSTATE), lambda o: (o, 0, 0))],
        out_specs=[pl.BlockSpec((N_TOK, 128), lambda o: (0, o)),
                   pl.BlockSpec((2, gs, N_SEG, 2 * DSTATE), lambda o: (0, o, 0, 0))],
        out_shape=[jax.ShapeDtypeStruct((N_TOK, SSM_WIDTH), F32),
                   jax.ShapeDtypeStruct((2, G, N_SEG, 2 * DSTATE), F32)],
        scratch_shapes=[pltpu.VMEM((gs, rows, FLAT), F32), pltpu.VMEM((gs, rows, FLAT), F32)],
        compiler_params=_params("arbitrary"),
        name="ssm_scan",
    )(u2d, t_op, s_op, r_op, lam, powc, dtile, h0c_re, h0c_im, h0d_re, h0d_im)


CTX_BB = 4


def _softmax_pv(scores, values):
    m = scores[0].max(axis=-1, keepdims=True)
    for s in scores[1:]:
        m = jnp.maximum(m, s.max(axis=-1, keepdims=True))
    l = None
    o = None
    for s, v in zip(scores, values):
        p = jnp.exp(s - m)
        ls = p.sum(axis=-1, keepdims=True)
        os_ = jnp.dot(p.astype(BF16), v, preferred_element_type=F32)
        l = ls if l is None else l + ls
        o = os_ if o is None else o + os_
    return o / l


def _ctx_attn_kernel(q_ref, k_ref, v_ref, za_ref, o_ref):
    head0 = lax.broadcasted_iota(jnp.int32, (1, 2 * HEAD_DIM), 1) < HEAD_DIM
    for b in range(CTX_BB):
        sl = slice(b * SEQ, (b + 1) * SEQ)
        q, k, v = q_ref[sl, :], k_ref[sl, :], v_ref[sl, :]
        outs = []
        for hh in range(2):
            qh = jnp.where(head0 if hh == 0 else jnp.logical_not(head0), q, jnp.zeros_like(q))
            outs.append(_softmax_pv([_dot_nt(qh, k)], [v]))
        o = jnp.where(head0, outs[0], outs[1])
        o_ref[sl, :] = (o * _silu(za_ref[sl, :].astype(F32))).astype(BF16)


def _ctx_attention(q, k, v, za):
    rows = CTX_BB * SEQ
    spec = pl.BlockSpec((rows, 2 * HEAD_DIM), lambda b, hp: (b, hp))
    return pl.pallas_call(
        _ctx_attn_kernel,
        grid=(BATCH // CTX_BB, N_HEADS // 2),
        in_specs=[spec, spec, spec, spec],
        out_specs=spec,
        out_shape=jax.ShapeDtypeStruct((N_CTX_TOK, ATT_WIDTH), BF16),
        compiler_params=_params("arbitrary", "arbitrary"),
        name="ctx_attention",
    )(q, k, v, za)


N_DROW = 2 * WIN_ROWS - 1


def _nbr_build_table(band_ref, tbl):
    lane = lax.broadcasted_iota(jnp.int32, (GRID_W, 2 * GRID_W), 1)
    qc = lax.broadcasted_iota(jnp.int32, (GRID_W, 2 * GRID_W), 0)
    kc = lane & (GRID_W - 1)
    cs = jnp.clip(qc - WIN_COLS // 2, 0, GRID_W - WIN_COLS)
    valid = jnp.logical_and(kc >= cs, kc < cs + WIN_COLS)
    for hh in range(2):
        for dr in range(N_DROW):
            x = jnp.broadcast_to(band_ref[hh, dr:dr + 1, :], (GRID_W, 2 * GRID_W))
            lo = pltpu.roll(x, 0, axis=1, stride=1, stride_axis=0)
            hi = pltpu.roll(x, GRID_W, axis=1, stride=1, stride_axis=0)
            tbl[hh, dr] = jnp.where(valid, jnp.where(lane < GRID_W, lo, hi), NEG_INF)
        tbl[hh, N_DROW] = jnp.full((GRID_W, 2 * GRID_W), NEG_INF, F32)


def _nbr_bias(tbl, hh, a):
    half0 = lax.broadcasted_iota(jnp.int32, (1, 2 * GRID_W), 1) < GRID_W
    ks = jnp.clip(Q_ROWS_PER_BLOCK * a - Q_ROWS_PER_BLOCK, 0, GRID_H - NBR_KROWS)
    rows = []
    for ri in range(Q_ROWS_PER_BLOCK):
        r = Q_ROWS_PER_BLOCK * a + ri
        rs = jnp.clip(r - WIN_ROWS // 2, 0, GRID_H - WIN_ROWS)
        tiles = []
        for pair in range(NBR_KROWS // 2):
            idx = []
            for kri in (2 * pair, 2 * pair + 1):
                kr = ks + kri
                valid = jnp.logical_and(kr >= rs, kr < rs + WIN_ROWS)
                idx.append(jnp.where(valid, kr - r + WIN_ROWS - 1, N_DROW))
            tiles.append(jnp.where(half0, tbl[hh, idx[0]], tbl[hh, idx[1]]))
        rows.append(jnp.concatenate(tiles, axis=1))
    return jnp.concatenate(rows, axis=0)


def _nbr_attn_kernel(q_ref, k_ref, v_ref, kc_ref, vc_ref, band_ref, za_ref, o_ref, tbl):
    a = pl.program_id(1)

    @pl.when(a == 0)
    def _():
        _nbr_build_table(band_ref, tbl)

    start = pl.multiple_of(jnp.clip(a * NBR_Q - NBR_Q, 0, DEC_SEQ - NBR_K), NBR_Q)
    head0 = lax.broadcasted_iota(jnp.int32, (1, 2 * HEAD_DIM), 1) < HEAD_DIM
    bias = [_nbr_bias(tbl, hh, a) for hh in range(2)]
    for b in range(DEC_BATCH):
        q = q_ref[b]
        kw = k_ref[b, pl.ds(start, NBR_K), :]
        vw = v_ref[b, pl.ds(start, NBR_K), :]
        kc = kc_ref[b].astype(BF16)
        vc = vc_ref[b].astype(BF16)
        outs = []
        for hh in range(2):
            qh = jnp.where(head0 if hh == 0 else jnp.logical_not(head0), q, jnp.zeros_like(q))
            sw = _dot_nt(qh, kw) + bias[hh]
            sc = _dot_nt(qh, kc)
            outs.append(_softmax_pv([sw, sc], [vw, vc]))
        o = jnp.where(head0, outs[0], outs[1])
        o_ref[b] = (o * _silu(za_ref[b].astype(F32))).astype(BF16)


def _nbr_attention(q4, k4, v4, kc, vc, band, za4):
    first = N_CTX_TOK // DEC_SEQ // DEC_BATCH
    nblk = GRID_H // Q_ROWS_PER_BLOCK
    hw = 2 * HEAD_DIM
    qspec = pl.BlockSpec((DEC_BATCH, NBR_Q, hw), lambda hp, a: (first, a, hp))
    kspec = pl.BlockSpec((DEC_BATCH, DEC_SEQ, hw), lambda hp, a: (first, 0, hp))
    cspec = pl.BlockSpec((DEC_BATCH, PAST_LEN, hw), lambda hp, a: (0, 0, hp))
    bspec = pl.BlockSpec((2, N_DROW + 1, 2 * GRID_W), lambda hp, a: (hp, 0, 0))
    return pl.pallas_call(
        _nbr_attn_kernel,
        grid=(N_HEADS // 2, nblk),
        in_specs=[qspec, kspec, kspec, cspec, cspec, bspec, qspec],
        out_specs=pl.BlockSpec((DEC_BATCH, NBR_Q, hw), lambda hp, a: (0, a, hp)),
        out_shape=jax.ShapeDtypeStruct((DEC_BATCH, DEC_SEQ, ATT_WIDTH), BF16),
        scratch_shapes=[pltpu.VMEM((2, N_DROW + 1, GRID_W, 2 * GRID_W), F32)],
        compiler_params=_params("arbitrary", "arbitrary"),
        name="nbr_attention",
    )(q4, k4, v4, kc, vc, band, za4)


def _nbr_bias_band(rpb):
    ncol = 2 * WIN_COLS - 1
    fill = jnp.full(rpb.shape[:2] + (2 * GRID_W - ncol,), NEG_INF, F32)
    band = jnp.concatenate([rpb[..., WIN_COLS - 1:], fill, rpb[..., :WIN_COLS - 1]], axis=-1)
    return jnp.pad(band, ((0, 0), (0, 1), (0, 0)))


BACK_TM = CHUNK * CH_PER_SEG
BACK_CTX_STEPS = N_CTX_TOK // BACK_TM
BACK_STEPS_PER_DEC_SEQ = DEC_SEQ // BACK_TM


def _back_kernel(xp_ref, xs_ref, y_ref, zs_ref, ac_ref, ad_ref, gs_ref, ga_ref, gate_ref, bglu_ref,
                 wglu_ref, wso_ref, wao_ref, wo_ref, op_ref, os_ref):
    i = pl.program_id(0)

    def compute(x, a2):
        ys = _gelu_tanh(y_ref[0].reshape(BACK_TM, SSM_WIDTH))
        t = jnp.dot(ys.astype(BF16), wglu_ref[...], preferred_element_type=F32) + bglu_ref[...]
        ys = ys * _sigmoid(t) * _silu(zs_ref[...].astype(F32))
        p_s = jnp.dot(ys.astype(BF16), wso_ref[...], preferred_element_type=F32)
        p_a = jnp.dot(a2, wao_ref[...], preferred_element_type=F32)
        merged = (_sigmoid(gs_ref[...].astype(F32)) * p_s + _sigmoid(ga_ref[...].astype(F32)) * p_a)
        return x + gate_ref[0] * jnp.dot(merged.astype(BF16), wo_ref[...], preferred_element_type=F32)

    @pl.when(i < BACK_CTX_STEPS)
    def _():
        op_ref[...] = compute(xp_ref[...], ac_ref[...])

    @pl.when(i >= BACK_CTX_STEPS)
    def _():
        os_ref[...] = compute(xs_ref[...], ad_ref[...])


def _back(xp, xs, y, zs, a_ctx, a_dec, gs, ga, gate3, b_glu, w_glu, w_so, w_ao, w_o):
    steps = N_TOK // BACK_TM
    n0 = BACK_CTX_STEPS
    row = functools.partial(_mod_row, ctx_steps=n0, steps_per_seq=BACK_STEPS_PER_DEC_SEQ)
    lo = lambda i: (jnp.minimum(i, n0 - 1), 0)
    hi = lambda i: (jnp.maximum(i - n0, 0), 0)
    cur = lambda i: (i, 0)
    const = lambda i: (0, 0)
    once = pl.Buffered(1)
    return pl.pallas_call(
        _back_kernel,
        grid=(steps,),
        in_specs=[pl.BlockSpec((BACK_TM, D_MODEL), lo),
                  pl.BlockSpec((BACK_TM, D_MODEL), hi),
                  pl.BlockSpec((1, CH_PER_SEG, None, CHUNK, SSM_WIDTH),
                               lambda i: (i // N_SEG, 0, i % N_SEG, 0, 0)),
                  pl.BlockSpec((BACK_TM, SSM_WIDTH), cur),
                  pl.BlockSpec((BACK_TM, ATT_WIDTH), lo),
                  pl.BlockSpec((BACK_TM, ATT_WIDTH), hi),
                  pl.BlockSpec((BACK_TM, D_MODEL), cur),
                  pl.BlockSpec((BACK_TM, D_MODEL), cur),
                  pl.BlockSpec((1, 1, D_MODEL), lambda i: (row(i), 0, 0)),
                  pl.BlockSpec((1, SSM_WIDTH), const),
                  pl.BlockSpec((SSM_WIDTH, SSM_WIDTH), const, pipeline_mode=once),
                  pl.BlockSpec((SSM_WIDTH, D_MODEL), const, pipeline_mode=once),
                  pl.BlockSpec((ATT_WIDTH, D_MODEL), const, pipeline_mode=once),
                  pl.BlockSpec((D_MODEL, D_MODEL), const, pipeline_mode=once)],
        out_specs=[pl.BlockSpec((BACK_TM, D_MODEL), lo),
                   pl.BlockSpec((BACK_TM, D_MODEL), hi)],
        out_shape=[jax.ShapeDtypeStruct((N_CTX_TOK, D_MODEL), F32),
                   jax.ShapeDtypeStruct((N_DEC_TOK, D_MODEL), F32)],
        compiler_params=_params("arbitrary"),
        name="gated_output",
    )(xp, xs, y, zs, a_ctx, a_dec, gs, ga, gate3, b_glu.reshape(1, SSM_WIDTH), w_glu, w_so, w_ao, w_o)


def _layer(xp, xs, cache_k, cache_v, st_re, st_im, c, c_ctx, norm_w, w_ada, b_ada, w_in, q_norm_w, k_norm_w,
           rpb, a_re, a_im, log_dt, b_re, b_im, c_re, c_im, d, w_glu, b_glu, w_so, w_ao, w_o):
    G, P = SSM_GROUPS, SSM_STATE

    cond8 = jnp.zeros((8, D_MODEL), F32).at[0].set(c_ctx).at[1:1 + DEC_BATCH].set(c)
    mod = _modulation(cond8, w_ada, b_ada)
    shift3 = mod[:1 + DEC_BATCH, None, :D_MODEL]
    scale3 = mod[:1 + DEC_BATCH, None, D_MODEL:2 * D_MODEL]
    gate3 = mod[:1 + DEC_BATCH, None, 2 * D_MODEL:]

    h = _norm_modulate(xp, xs, shift3, scale3, norm_w)
    u = _in_proj(h, w_in, OFF_U, SSM_WIDTH, "u")
    zs = _in_proj(h, w_in, OFF_ZS, SSM_WIDTH, "plain")
    q = _in_proj(h, w_in, OFF_Q, ATT_WIDTH, "q", q_norm_w)
    k, k_ctx32 = _in_proj(h, w_in, OFF_K, ATT_WIDTH, "k", k_norm_w)
    v, v_ctx32 = _in_proj(h, w_in, OFF_V, ATT_WIDTH, "v")
    za = _in_proj(h, w_in, OFF_ZA, ATT_WIDTH, "plain")
    gs = _in_proj(h, w_in, OFF_GS, D_MODEL, "plain")
    ga = _in_proj(h, w_in, OFF_GA, D_MODEL, "plain")

    arow = jnp.transpose(a_re, (1, 0, 2)).reshape(G, 2 * P)
    airow = jnp.transpose(a_im, (1, 0, 2)).reshape(G, 2 * P)
    dtrow = jnp.broadcast_to(jnp.transpose(log_dt, (1, 0))[:, :, None], (G, 2, P)).reshape(G, 2 * P)
    rowp = jnp.stack([arow, airow, dtrow], axis=1)
    colp = jnp.transpose(rowp, (0, 2, 1))
    btr = jnp.transpose(b_re, (1, 3, 0, 2)).reshape(G, SSM_GROUP, 2 * P)
    bti = jnp.transpose(b_im, (1, 3, 0, 2)).reshape(G, SSM_GROUP, 2 * P)
    ctr = jnp.transpose(c_re, (1, 0, 3, 2)).reshape(G, 2 * P, SSM_GROUP)
    cti = jnp.transpose(c_im, (1, 0, 3, 2)).reshape(G, 2 * P, SSM_GROUP)
    t_op, s_op, r_op, lam, powc = _ssm_operators(rowp, colp, btr, bti, ctr, cti)

    dtile = jnp.tile(d.reshape(G, 1, SSM_GROUP), (1, 1, CHUNK))
    h0c = jnp.zeros((G, BATCH, 2 * P), F32)
    h0d_re = jnp.transpose(st_re, (2, 0, 1, 3)).reshape(G, DEC_BATCH, 2 * P)
    h0d_im = jnp.transpose(st_im, (2, 0, 1, 3)).reshape(G, DEC_BATCH, 2 * P)
    y2d, fin = _ssm_scan(u.reshape(N_TOK, SSM_WIDTH), t_op, s_op, r_op, lam, powc, dtile,
                         h0c, h0c, h0d_re, h0d_im)
    y = y2d.reshape(2, CH_PER_SEG, N_SEG, CHUNK, SSM_WIDTH)

    a_ctx = _ctx_attention(q, k, v, za)
    nseq4 = N_TOK // DEC_SEQ
    a_dec = _nbr_attention(q.reshape(nseq4, DEC_SEQ, ATT_WIDTH), k.reshape(nseq4, DEC_SEQ, ATT_WIDTH),
                           v.reshape(nseq4, DEC_SEQ, ATT_WIDTH),
                           cache_k.reshape(DEC_BATCH, PAST_LEN, ATT_WIDTH),
                           cache_v.reshape(DEC_BATCH, PAST_LEN, ATT_WIDTH),
                           _nbr_bias_band(rpb.astype(F32)),
                           za.reshape(nseq4, DEC_SEQ, ATT_WIDTH))

    yp, ys_out = _back(xp, xs, y, zs, a_ctx, a_dec.reshape(N_DEC_TOK, ATT_WIDTH), gs, ga, gate3, b_glu,
                       w_glu.astype(BF16), w_so.astype(BF16), w_ao.astype(BF16), w_o.astype(BF16))

    fin_ctx = fin[0]
    new_re = jnp.transpose(fin_ctx[:, :, :2 * P].reshape(G, BATCH, 2, P), (1, 2, 0, 3))
    new_im = jnp.transpose(fin_ctx[:, :, 2 * P:].reshape(G, BATCH, 2, P), (1, 2, 0, 3))
    return yp, ys_out, k_ctx32, v_ctx32, new_re, new_im


def kernel(x_prompt, x_sample, cache_k, cache_v, state_ssm_re, state_ssm_im, c, c_ctx, norm_w, w_ada, b_ada,
           w_in, q_norm_w, k_norm_w, rel_pos_bias, ssm_a_re, ssm_a_im, ssm_log_dt, ssm_b_re, ssm_b_im,
           ssm_c_re, ssm_c_im, ssm_d, w_glu, b_glu, w_ssm_out, w_att_out, w_o):
    depth = norm_w.shape[0]
    xp = x_prompt.reshape(N_CTX_TOK, D_MODEL)
    xs = x_sample.reshape(N_DEC_TOK, D_MODEL)
    new_k, new_v, new_re, new_im = [], [], [], []
    for l in range(depth):
        xp, xs, kl, vl, rl, il = _layer(
            xp, xs, cache_k[:, l], cache_v[:, l], state_ssm_re[:, l], state_ssm_im[:, l], c, c_ctx,
            norm_w[l], w_ada[l], b_ada[l], w_in[l], q_norm_w[l], k_norm_w[l], rel_pos_bias[l],
            ssm_a_re[l], ssm_a_im[l], ssm_log_dt[l], ssm_b_re[l], ssm_b_im[l], ssm_c_re[l], ssm_c_im[l],
            ssm_d[l], w_glu[l], b_glu[l], w_ssm_out[l], w_att_out[l], w_o[l])
        new_k.append(kl.reshape(BATCH, SEQ, N_HEADS, HEAD_DIM))
        new_v.append(vl.reshape(BATCH, SEQ, N_HEADS, HEAD_DIM))
        new_re.append(rl)
        new_im.append(il)
    return (xp.reshape(BATCH, SEQ, D_MODEL), xs.reshape(DEC_BATCH, DEC_SEQ, D_MODEL),
            jnp.stack(new_k, axis=1), jnp.stack(new_v, axis=1),
            jnp.stack(new_re, axis=1), jnp.stack(new_im, axis=1))
```

```python
import functools
import math

import jax
import jax.numpy as jnp
from jax import lax
from jax.experimental import pallas as pl
from jax.experimental.pallas import tpu as pltpu

D_MODEL = 2048
BATCH = 16
SEQ = 256
DEC_BATCH = 2
DEC_SEQ = 2048
PAST_LEN = 512
GRID_W = 64
GRID_H = DEC_SEQ // GRID_W
SSM_WIDTH = D_MODEL // 2
SSM_GROUP = 16
SSM_GROUPS = SSM_WIDTH // SSM_GROUP
SSM_STATE = 64
N_HEADS = 16
HEAD_DIM = 64
ATT_WIDTH = N_HEADS * HEAD_DIM
WIN_ROWS = 8
WIN_COLS = 16
EPS = 1e-6
NEG_INF = -1e30

N_CTX_TOK = BATCH * SEQ
N_DEC_TOK = DEC_BATCH * DEC_SEQ
N_TOK = N_CTX_TOK + N_DEC_TOK

OFF_U = 0
OFF_ZS = SSM_WIDTH
OFF_Q = 2 * SSM_WIDTH
OFF_K = OFF_Q + ATT_WIDTH
OFF_V = OFF_K + ATT_WIDTH
OFF_ZA = OFF_V + ATT_WIDTH
OFF_GS = OFF_ZA + ATT_WIDTH
OFF_GA = OFF_GS + D_MODEL

CHUNK = 16
FLAT = CHUNK * SSM_GROUP
N_SEG = 16
CH_PER_SEG = 16
DSTATE = 2 * SSM_STATE

Q_ROWS_PER_BLOCK = 4
NBR_Q = Q_ROWS_PER_BLOCK * GRID_W
NBR_KROWS = 12
NBR_K = NBR_KROWS * GRID_W

VMEM_LIMIT = 56 * 1024 * 1024

F32 = jnp.float32
BF16 = jnp.bfloat16
HIGHEST = lax.Precision.HIGHEST


def _sigmoid(x):
    return 1.0 / (1.0 + jnp.exp(-x))


def _silu(x):
    return x * _sigmoid(x)


def _gelu_tanh(x):
    return 0.5 * x * (1.0 + jnp.tanh(math.sqrt(2.0 / math.pi) * (x + 0.044715 * (x * x * x))))


def _cmul(ar, ai, br, bi):
    return ar * br - ai * bi, ar * bi + ai * br


def _dot_nt(a, b):
    return lax.dot_general(a, b, (((1,), (1,)), ((), ())), preferred_element_type=F32)


def _params(*sem):
    return pltpu.CompilerParams(dimension_semantics=sem, vmem_limit_bytes=VMEM_LIMIT)


def _mod_kernel(cond_ref, w_ref, b_ref, o_ref):
    c = cond_ref[...]
    s = _silu(c).astype(BF16)
    o_ref[...] = jnp.dot(s, w_ref[...].astype(BF16), preferred_element_type=F32) + b_ref[...]


def _modulation(cond8, w_ada, b_ada):
    tn = 768
    n = w_ada.shape[1]
    return pl.pallas_call(
        _mod_kernel,
        grid=(n // tn,),
        in_specs=[pl.BlockSpec((8, D_MODEL), lambda j: (0, 0)),
                  pl.BlockSpec((D_MODEL, tn), lambda j: (0, j)),
                  pl.BlockSpec((1, tn), lambda j: (0, j))],
        out_specs=pl.BlockSpec((8, tn), lambda j: (0, j)),
        out_shape=jax.ShapeDtypeStruct((8, n), F32),
        compiler_params=_params("arbitrary"),
        name="modulation",
    )(cond8, w_ada, b_ada.reshape(1, n))


NORM_TM = 512
NORM_CTX_STEPS = N_CTX_TOK // NORM_TM
NORM_STEPS_PER_DEC_SEQ = DEC_SEQ // NORM_TM


def _mod_row(i, ctx_steps, steps_per_seq):
    return jnp.where(i < ctx_steps, 0, 1 + (i - ctx_steps) // steps_per_seq)


def _norm_kernel(xp_ref, xs_ref, shift_ref, scale_ref, nw_ref, o_ref):
    i = pl.program_id(0)

    def body(x):
        ms = jnp.mean(x * x, axis=-1, keepdims=True)
        y = x * lax.rsqrt(ms + EPS) * nw_ref[...]
        o_ref[...] = (y * (1.0 + scale_ref[0]) + shift_ref[0]).astype(BF16)

    @pl.when(i < NORM_CTX_STEPS)
    def _():
        body(xp_ref[...])

    @pl.when(i >= NORM_CTX_STEPS)
    def _():
        body(xs_ref[...])


def _norm_modulate(xp, xs, shift3, scale3, norm_w):
    steps = N_TOK // NORM_TM
    row = functools.partial(_mod_row, ctx_steps=NORM_CTX_STEPS, steps_per_seq=NORM_STEPS_PER_DEC_SEQ)
    return pl.pallas_call(
        _norm_kernel,
        grid=(steps,),
        in_specs=[pl.BlockSpec((NORM_TM, D_MODEL), lambda i: (jnp.minimum(i, NORM_CTX_STEPS - 1), 0)),
                  pl.BlockSpec((NORM_TM, D_MODEL), lambda i: (jnp.maximum(i - NORM_CTX_STEPS, 0), 0)),
                  pl.BlockSpec((1, 1, D_MODEL), lambda i: (row(i), 0, 0)),
                  pl.BlockSpec((1, 1, D_MODEL), lambda i: (row(i), 0, 0)),
                  pl.BlockSpec((1, D_MODEL), lambda i: (0, 0))],
        out_specs=pl.BlockSpec((NORM_TM, D_MODEL), lambda i: (i, 0)),
        out_shape=jax.ShapeDtypeStruct((N_TOK, D_MODEL), BF16),
        compiler_params=_params("arbitrary"),
        name="norm_modulate",
    )(xp, xs, shift3, scale3, norm_w.reshape(1, D_MODEL))


PROJ_TM = 1024
PROJ_TN = 1024
PROJ_CTX_STEPS = N_CTX_TOK // PROJ_TM
PROJ_SEGS = PROJ_TM // (CHUNK * CH_PER_SEG)


def _head_group_ones():
    r = lax.broadcasted_iota(jnp.int32, (256, 256), 0) // HEAD_DIM
    c = lax.broadcasted_iota(jnp.int32, (256, 256), 1) // HEAD_DIM
    return jnp.where(r == c, 1.0, 0.0).astype(BF16)


def _head_rms(acc, nw):
    ones = _head_group_ones()
    outs = []
    for c in range(acc.shape[1] // 256):
        a = acc[:, c * 256:(c + 1) * 256]
        ssum = jnp.dot((a * a).astype(BF16), ones, preferred_element_type=F32)
        outs.append(a * lax.rsqrt(ssum * (1.0 / HEAD_DIM) + EPS))
    return jnp.concatenate(outs, axis=1) * nw


def _proj_kernel(*refs, mode):
    if mode in ("plain", "u"):
        h_ref, w_ref, o_ref, wbf = refs
    elif mode == "q":
        h_ref, w_ref, nw_ref, o_ref, wbf = refs
    elif mode == "k":
        h_ref, w_ref, nw_ref, o_ref, of_ref, wbf = refs
    else:
        h_ref, w_ref, o_ref, of_ref, wbf = refs
    i = pl.program_id(1)

    @pl.when(i == 0)
    def _():
        wbf[...] = w_ref[...].astype(BF16)

    acc = jnp.dot(h_ref[...], wbf[...], preferred_element_type=F32)
    if mode == "q":
        acc = _head_rms(acc, nw_ref[...]) * (HEAD_DIM ** -0.5)
    elif mode == "k":
        acc = _head_rms(acc, nw_ref[...])
    if mode == "u":
        for seg in range(PROJ_SEGS):
            for ch in range(CH_PER_SEG):
                r0 = (seg * CH_PER_SEG + ch) * CHUNK
                o_ref[0, ch, seg] = acc[r0:r0 + CHUNK, :]
    else:
        o_ref[...] = acc.astype(BF16)
    if mode in ("k", "v"):
        @pl.when(i < PROJ_CTX_STEPS)
        def _():
            of_ref[...] = acc


def _in_proj(h, w_in, col_off, ncols, mode, norm_w=None):
    nj = ncols // PROJ_TN
    ni = N_TOK // PROJ_TM
    joff = col_off // PROJ_TN
    in_specs = [pl.BlockSpec((PROJ_TM, D_MODEL), lambda j, i: (i, 0)),
                pl.BlockSpec((D_MODEL, PROJ_TN), lambda j, i: (0, joff + j))]
    args = [h, w_in]
    if mode in ("q", "k"):
        in_specs.append(pl.BlockSpec((1, PROJ_TN), lambda j, i: (0, 0)))
        args.append(jnp.tile(norm_w.reshape(1, HEAD_DIM), (1, PROJ_TN // HEAD_DIM)))
    out_specs = [pl.BlockSpec((PROJ_TM, PROJ_TN), lambda j, i: (i, j))]
    out_shape = [jax.ShapeDtypeStruct((N_TOK, ncols), BF16)]
    if mode == "u":
        tiles_per_path = N_SEG // PROJ_SEGS
        out_specs = [pl.BlockSpec((1, CH_PER_SEG, PROJ_SEGS, CHUNK, PROJ_TN),
                                  lambda j, i: (i // tiles_per_path, 0, i % tiles_per_path, 0, j))]
        out_shape = [jax.ShapeDtypeStruct((2, CH_PER_SEG, N_SEG, CHUNK, ncols), F32)]
    if mode in ("k", "v"):
        out_specs.append(pl.BlockSpec((PROJ_TM, PROJ_TN),
                                      lambda j, i: (jnp.minimum(i, PROJ_CTX_STEPS - 1), j)))
        out_shape.append(jax.ShapeDtypeStruct((N_CTX_TOK, ncols), F32))
    res = pl.pallas_call(
        functools.partial(_proj_kernel, mode=mode),
        grid=(nj, ni),
        in_specs=in_specs,
        out_specs=out_specs,
        out_shape=out_shape,
        scratch_shapes=[pltpu.VMEM((D_MODEL, PROJ_TN), BF16)],
        compiler_params=_params("arbitrary", "arbitrary"),
        name="in_proj_" + mode,
    )(*args)
    return res if mode in ("k", "v") else res[0]


SSM_GB = 4


def _pow_select(e, pows):
    rr = jnp.where((e & 1) != 0, pows[0][0], 1.0)
    ri = jnp.where((e & 1) != 0, pows[0][1], 0.0)
    for b in range(1, len(pows)):
        bit = (e & (1 << b)) != 0
        fr = jnp.where(bit, pows[b][0], 1.0)
        fi = jnp.where(bit, pows[b][1], 0.0)
        rr, ri = _cmul(rr, ri, fr, fi)
    return rr, ri


def _discretize(a_re, a_im, log_dt):
    lr = jnp.minimum(a_re, -1e-4)
    li = a_im
    dt = jnp.exp(log_dt)
    mag = jnp.exp(lr * dt)
    br = mag * jnp.cos(li * dt)
    bi = mag * jnp.sin(li * dt)
    den = lr * lr + li * li
    nr = br - 1.0
    cr = (nr * lr + bi * li) / den
    ci = (bi * lr - nr * li) / den
    return (br, bi), (cr, ci)


def _squarings(pr, pi, n):
    out = [(pr, pi)]
    for _ in range(n):
        pr, pi = _cmul(pr, pi, pr, pi)
        out.append((pr, pi))
    return out


def _ssm_ops_kernel(rowp_ref, colp_ref, btr_ref, bti_ref, ctr_ref, cti_ref,
                    t_ref, s_ref, r_ref, lam_ref, powc_ref):
    lane_s = lax.broadcasted_iota(jnp.int32, (FLAT, DSTATE), 1)
    row_s = lax.broadcasted_iota(jnp.int32, (FLAT, DSTATE), 0)
    exp_s = jnp.where(lane_s < SSM_STATE, (CHUNK - 1) - (row_s >> 4), row_s >> 4)
    row_k = lax.broadcasted_iota(jnp.int32, (DSTATE, FLAT), 0)
    lane_k = lax.broadcasted_iota(jnp.int32, (DSTATE, FLAT), 1)
    exp_k = jnp.where(row_k < SSM_STATE, lane_k >> 4, (CHUNK - 1) - (lane_k >> 4))
    tile_e = jnp.where((lax.broadcasted_iota(jnp.int32, (SSM_GROUP, FLAT), 1) & (SSM_GROUP - 1))
                       == lax.broadcasted_iota(jnp.int32, (SSM_GROUP, FLAT), 0), 1.0, 0.0)
    lane_b = lax.broadcasted_iota(jnp.int32, (SSM_GROUP, DSTATE), 1)
    lane_t = lax.broadcasted_iota(jnp.int32, (SSM_GROUP, FLAT), 1)
    row_c = lax.broadcasted_iota(jnp.int32, (CH_PER_SEG, DSTATE), 0)
    lane_c = lax.broadcasted_iota(jnp.int32, (CH_PER_SEG, DSTATE), 1)
    exp_c = jnp.where(lane_c < SSM_STATE, row_c, (CH_PER_SEG - 1) - row_c)

    for g in range(SSM_GB):
        rp = rowp_ref[g]
        (lbr, lbi), (cfr, cfi) = _discretize(rp[0:1], rp[1:2], rp[2:3])
        sq = _squarings(lbr, lbi, 8)
        bbr, bbi = _cmul(btr_ref[g], bti_ref[g], cfr, cfi)

        pr, pi = _pow_select(exp_s, sq[:4])
        tb_r = jnp.concatenate([bbr] * CHUNK, axis=0)
        tb_i = jnp.concatenate([bbi] * CHUNK, axis=0)
        s_re, s_im = _cmul(tb_r, tb_i, pr, pi)
        s_ref[g] = jnp.concatenate([s_re, s_im], axis=1).astype(BF16)

        cp = colp_ref[g]
        (kbr, kbi), _ = _discretize(cp[:, 0:1], cp[:, 1:2], cp[:, 2:3])
        sqc = _squarings(kbr, kbi, 3)
        ctr = jnp.dot(ctr_ref[g], tile_e, preferred_element_type=F32, precision=HIGHEST)
        cti = jnp.dot(cti_ref[g], tile_e, preferred_element_type=F32, precision=HIGHEST)
        pkr, pki = _pow_select(exp_k, sqc)
        ykr, yki = _cmul(ctr, cti, pkr, pki)
        yrr, yri = _cmul(ykr, yki, kbr, kbi)
        r_ref[g] = jnp.concatenate([yrr, -yri], axis=0).astype(BF16)

        fwd = lane_b < SSM_STATE
        lhs = jnp.concatenate([jnp.where(fwd, bbr, 0.0), jnp.where(fwd, bbi, 0.0),
                               jnp.where(fwd, 0.0, bbr), jnp.where(fwd, 0.0, bbi)], axis=0)
        p1 = jnp.dot(lhs, ykr, preferred_element_type=F32, precision=HIGHEST)
        p2 = jnp.dot(lhs, yki, preferred_element_type=F32, precision=HIGHEST)
        kf = p1[0:16] - p2[16:32]
        kb = p1[32:48] - p2[48:64]
        for sp in range(CHUNK):
            tf = kf if sp == 0 else pltpu.roll(kf, SSM_GROUP * sp, axis=1)
            tf = jnp.where(lane_t >= SSM_GROUP * sp, tf, 0.0)
            shift = (FLAT - SSM_GROUP * (CHUNK - 1 - sp)) % FLAT
            tb = kb if shift == 0 else pltpu.roll(kb, shift, axis=1)
            tb = jnp.where(lane_t < SSM_GROUP * (sp + 1), tb, 0.0)
            t_ref[g, sp * SSM_GROUP:(sp + 1) * SSM_GROUP, :] = (tf + tb).astype(BF16)

        l16 = sq[4]
        l256 = sq[8]
        lam_ref[g] = jnp.concatenate([l16[0], l16[1], l256[0], l256[1],
                                      jnp.zeros((4, DSTATE), F32)], axis=0)
        q16 = _squarings(l16[0], l16[1], 3)
        pcr, pci = _pow_select(exp_c, q16)
        powc_ref[g, 0] = pcr
        powc_ref[g, 1] = pci


def _ssm_operators(rowp, colp, btr, bti, ctr, cti):
    G = SSM_GROUPS
    gb = SSM_GB
    return pl.pallas_call(
        _ssm_ops_kernel,
        grid=(G // gb,),
        in_specs=[pl.BlockSpec((gb, 3, DSTATE), lambda g: (g, 0, 0)),
                  pl.BlockSpec((gb, DSTATE, 3), lambda g: (g, 0, 0)),
                  pl.BlockSpec((gb, SSM_GROUP, DSTATE), lambda g: (g, 0, 0)),
                  pl.BlockSpec((gb, SSM_GROUP, DSTATE), lambda g: (g, 0, 0)),
                  pl.BlockSpec((gb, DSTATE, SSM_GROUP), lambda g: (g, 0, 0)),
                  pl.BlockSpec((gb, DSTATE, SSM_GROUP), lambda g: (g, 0, 0))],
        out_specs=[pl.BlockSpec((gb, FLAT, FLAT), lambda g: (g, 0, 0)),
                   pl.BlockSpec((gb, FLAT, FLAT), lambda g: (g, 0, 0)),
                   pl.BlockSpec((gb, FLAT, FLAT), lambda g: (g, 0, 0)),
                   pl.BlockSpec((gb, 8, DSTATE), lambda g: (g, 0, 0)),
                   pl.BlockSpec((gb, 2, CH_PER_SEG, DSTATE), lambda g: (g, 0, 0, 0))],
        out_shape=[jax.ShapeDtypeStruct((G, FLAT, FLAT), BF16),
                   jax.ShapeDtypeStruct((G, FLAT, FLAT), BF16),
                   jax.ShapeDtypeStruct((G, FLAT, FLAT), BF16),
                   jax.ShapeDtypeStruct((G, 8, DSTATE), F32),
                   jax.ShapeDtypeStruct((G, 2, CH_PER_SEG, DSTATE), F32)],
        compiler_params=_params("arbitrary"),
        name="ssm_operators",
    )(rowp, colp, btr, bti, ctr, cti)


SSM_PATHS = ((BATCH, SEQ // (CHUNK * CH_PER_SEG)), (DEC_BATCH, DEC_SEQ // (CHUNK * CH_PER_SEG)))


def _ssm_path(u, s_op, t_op, r_op, lam, powc_ref, h0r, h0i, nseg):
    lane = lax.broadcasted_iota(jnp.int32, (N_SEG, DSTATE), 1)
    isf = lane < SSM_STATE
    l16r, l16i, l256r, l256i = lam[0:1], lam[1:2], lam[2:3], lam[3:4]

    z = jnp.dot(u, s_op, preferred_element_type=F32)
    zre, zim = z[:, :DSTATE], z[:, DSTATE:]

    hr = jnp.zeros((N_SEG, DSTATE), F32)
    hi = jnp.zeros((N_SEG, DSTATE), F32)
    hist = []
    for k in range(CH_PER_SEG):
        hist.append((hr, hi))
        kb = CH_PER_SEG - 1 - k
        zr = jnp.where(isf, zre[k * N_SEG:(k + 1) * N_SEG], zre[kb * N_SEG:(kb + 1) * N_SEG])
        zi = jnp.where(isf, zim[k * N_SEG:(k + 1) * N_SEG], zim[kb * N_SEG:(kb + 1) * N_SEG])
        nr, ni = _cmul(l16r, l16i, hr, hi)
        hr, hi = nr + zr, ni + zi

    if nseg > 1:
        h0r = jnp.broadcast_to(h0r[:, None, :], (N_SEG // nseg, nseg, DSTATE)).reshape(N_SEG, DSTATE)
        h0i = jnp.broadcast_to(h0i[:, None, :], (N_SEG // nseg, nseg, DSTATE)).reshape(N_SEG, DSTATE)
    seg = lax.broadcasted_iota(jnp.int32, (N_SEG, DSTATE), 0) & (nseg - 1)
    segpow = _squarings(l256r, l256i, max(int(math.log2(nseg)), 0))

    def shifted(x, d):
        dn = jnp.where(seg >= d, pltpu.roll(x, d, axis=0), 0.0)
        up = jnp.where(seg <= nseg - 1 - d, pltpu.roll(x, N_SEG - d, axis=0), 0.0)
        return jnp.where(isf, dn, up)

    pr, pi = hr, hi
    d = 1
    lvl = 0
    while d < nseg:
        ar, ai = _cmul(segpow[lvl][0], segpow[lvl][1], shifted(pr, d), shifted(pi, d))
        pr, pi = pr + ar, pi + ai
        d *= 2
        lvl += 1
    if nseg > 1:
        e_in = jnp.where(isf, seg, nseg - 1 - seg)
        wr, wi = _pow_select(e_in, segpow[:lvl])
        ar, ai = _cmul(wr, wi, h0r, h0i)
        hsr, hsi = shifted(pr, 1) + ar, shifted(pi, 1) + ai
    else:
        hsr, hsi = h0r, h0i
    er, ei = _cmul(segpow[lvl][0], segpow[lvl][1], h0r, h0i)
    fin = jnp.concatenate([pr + er, pi + ei], axis=1)

    rows = []
    for c in range(CH_PER_SEG):
        cb = CH_PER_SEG - 1 - c
        lr = jnp.where(isf, hist[c][0], hist[cb][0])
        li = jnp.where(isf, hist[c][1], hist[cb][1])
        ar, ai = _cmul(powc_ref[0, c:c + 1, :], powc_ref[1, c:c + 1, :], hsr, hsi)
        rows.append(jnp.concatenate([lr + ar, li + ai], axis=1))
    hent = jnp.concatenate(rows, axis=0).astype(BF16)

    y = jnp.dot(u, t_op, preferred_element_type=F32) + jnp.dot(hent, r_op, preferred_element_type=F32)
    return y, fin


GROUPS_PER_STEP = 128 // SSM_GROUP
GROUPS_PER_ITER = 8


def _block_transpose(arrs, blk):
    n = len(arrs)
    width = arrs[0].shape[1]
    j = lax.broadcasted_iota(jnp.int32, arrs[0].shape, 1) // blk
    k = n // 2
    while k >= 1:
        bit = (j & k) != 0
        new = list(arrs)
        for x in range(n):
            if x & k == 0:
                a, b = arrs[x], arrs[x | k]
                new[x] = jnp.where(bit, pltpu.roll(b, k * blk, axis=1), a)
                new[x | k] = jnp.where(bit, b, pltpu.roll(a, width - k * blk, axis=1))
        arrs = new
        k //= 2
    return arrs


def _ssm_kernel(u_ref, t_ref, s_ref, r_ref, lam_ref, powc_ref, d_ref, h0cr_ref, h0ci_ref, h0dr_ref, h0di_ref,
                y_ref, fin_ref, ubuf, ybuf):
    h0 = ((h0cr_ref, h0ci_ref), (h0dr_ref, h0di_ref))
    nhalf = FLAT // 128
    rows = N_SEG * CH_PER_SEG
    def token_rows(path, tok, r0, nrows):
        return pl.ds(path * rows * CHUNK + r0 * CHUNK + tok, nrows, stride=CHUNK)

    in_rows = rows
    out_rows = rows
    for path, (_, nseg) in enumerate(SSM_PATHS):
        for half in range(nhalf):
            for r0 in range(0, rows, in_rows):
                toks = [pltpu.bitcast(
                    u_ref[token_rows(path, half * GROUPS_PER_STEP + sb, r0, in_rows), :].astype(BF16), jnp.uint32)
                    for sb in range(GROUPS_PER_STEP)]
                grouped = _block_transpose(toks, SSM_GROUP)
                for g in range(GROUPS_PER_STEP):
                    ubuf[g, r0:r0 + in_rows, half * 128:(half + 1) * 128] = pltpu.bitcast(grouped[g], BF16)

        def body(gi, carry, path=path, nseg=nseg):
            for dg in range(GROUPS_PER_ITER):
                g = gi * GROUPS_PER_ITER + dg
                y, fin = _ssm_path(ubuf[g], s_ref[g], t_ref[g], r_ref[g], lam_ref[g], powc_ref.at[g],
                                   h0[path][0][g], h0[path][1][g], nseg)
                ybuf[g] = y
                fin_ref[path, g] = fin
            return carry

        lax.fori_loop(0, GROUPS_PER_STEP // GROUPS_PER_ITER, body, 0)

        for half in range(nhalf):
            for r0 in range(0, rows, out_rows):
                grouped = [ybuf[g, r0:r0 + out_rows, half * 128:(half + 1) * 128] for g in range(GROUPS_PER_STEP)]
                toks = _block_transpose(grouped, SSM_GROUP)
                for sb in range(GROUPS_PER_STEP):
                    sel = token_rows(path, half * GROUPS_PER_STEP + sb, r0, out_rows)
                    y_ref[sel, :] = toks[sb] + u_ref[sel, :] * d_ref[...]


def _ssm_scan(u2d, t_op, s_op, r_op, lam, powc, dtile, h0c_re, h0c_im, h0d_re, h0d_im):
    G = SSM_GROUPS
    gs = GROUPS_PER_STEP
    rows = N_SEG * CH_PER_SEG
    return pl.pallas_call(
        _ssm_kernel,
        grid=(G // gs,),
        in_specs=[
            pl.BlockSpec((N_TOK, 128), lambda o: (0, o)),
            pl.BlockSpec((gs, FLAT, FLAT), lambda o: (o, 0, 0)),
            pl.BlockSpec((gs, FLAT, FLAT), lambda o: (o, 0, 0)),
            pl.BlockSpec((gs, FLAT, FLAT), lambda o: (o, 0, 0)),
            pl.BlockSpec((gs, 8, DSTATE), lambda o: (o, 0, 0)),
            pl.BlockSpec((gs, 2, CH_PER_SEG, DSTATE), lambda o: (o, 0, 0, 0)),
            pl.BlockSpec((1, 128), lambda o: (0, o)),
            pl.BlockSpec((gs, BATCH, DSTATE), lambda o: (o, 0, 0)),
            pl.BlockSpec((gs, BATCH, DSTATE), lambda o: (o, 0, 0)),
            pl.BlockSpec((gs, DEC_BATCH, DSTATE), lambda o: (o, 0, 0)),
            pl.BlockSpec((gs, DEC_BATCH, DSTATE), lambda o: (o, 0, 0))],
        out_specs=[pl.BlockSpec((N_TOK, 128), lambda o: (0, o)),
                   pl.BlockSpec((2, gs, N_SEG, 2 * DSTATE), lambda o: (0, o, 0, 0))],
        out_shape=[jax.ShapeDtypeStruct((N_TOK, SSM_WIDTH), F32),
                   jax.ShapeDtypeStruct((2, G, N_SEG, 2 * DSTATE), F32)],
        scratch_shapes=[pltpu.VMEM((gs, rows, FLAT), BF16), pltpu.VMEM((gs, rows, FLAT), F32)],
        compiler_params=_params("arbitrary"),
        name="ssm_scan",
    )(u2d, t_op, s_op, r_op, lam, powc, dtile, h0c_re, h0c_im, h0d_re, h0d_im)


CTX_BB = 4


def _softmax_pv(scores, values):
    m = scores[0].max(axis=-1, keepdims=True)
    for s in scores[1:]:
        m = jnp.maximum(m, s.max(axis=-1, keepdims=True))
    l = None
    o = None
    for s, v in zip(scores, values):
        p = jnp.exp(s - m)
        ls = p.sum(axis=-1, keepdims=True)
        os_ = jnp.dot(p.astype(BF16), v, preferred_element_type=F32)
        l = ls if l is None else l + ls
        o = os_ if o is None else o + os_
    return o / l


def _ctx_attn_kernel(q_ref, k_ref, v_ref, za_ref, o_ref):
    head0 = lax.broadcasted_iota(jnp.int32, (1, 2 * HEAD_DIM), 1) < HEAD_DIM
    for b in range(CTX_BB):
        sl = slice(b * SEQ, (b + 1) * SEQ)
        q, k, v = q_ref[sl, :], k_ref[sl, :], v_ref[sl, :]
        outs = []
        for hh in range(2):
            qh = jnp.where(head0 if hh == 0 else jnp.logical_not(head0), q, jnp.zeros_like(q))
            outs.append(_softmax_pv([_dot_nt(qh, k)], [v]))
        o = jnp.where(head0, outs[0], outs[1])
        o_ref[sl, :] = (o * _silu(za_ref[sl, :].astype(F32))).astype(BF16)


def _ctx_attention(q, k, v, za):
    rows = CTX_BB * SEQ
    spec = pl.BlockSpec((rows, 2 * HEAD_DIM), lambda b, hp: (b, hp))
    return pl.pallas_call(
        _ctx_attn_kernel,
        grid=(BATCH // CTX_BB, N_HEADS // 2),
        in_specs=[spec, spec, spec, spec],
        out_specs=spec,
        out_shape=jax.ShapeDtypeStruct((N_CTX_TOK, ATT_WIDTH), BF16),
        compiler_params=_params("arbitrary", "arbitrary"),
        name="ctx_attention",
    )(q, k, v, za)


N_DROW = 2 * WIN_ROWS - 1


def _nbr_build_table(band_ref, tbl):
    lane = lax.broadcasted_iota(jnp.int32, (GRID_W, 2 * GRID_W), 1)
    qc = lax.broadcasted_iota(jnp.int32, (GRID_W, 2 * GRID_W), 0)
    kc = lane & (GRID_W - 1)
    cs = jnp.clip(qc - WIN_COLS // 2, 0, GRID_W - WIN_COLS)
    valid = jnp.logical_and(kc >= cs, kc < cs + WIN_COLS)
    for hh in range(2):
        for dr in range(N_DROW):
            x = jnp.broadcast_to(band_ref[hh, dr:dr + 1, :], (GRID_W, 2 * GRID_W))
            lo = pltpu.roll(x, 0, axis=1, stride=1, stride_axis=0)
            hi = pltpu.roll(x, GRID_W, axis=1, stride=1, stride_axis=0)
            tbl[hh, dr] = jnp.where(valid, jnp.where(lane < GRID_W, lo, hi), NEG_INF)
        tbl[hh, N_DROW] = jnp.full((GRID_W, 2 * GRID_W), NEG_INF, F32)


def _nbr_bias(tbl, hh, a):
    half0 = lax.broadcasted_iota(jnp.int32, (1, 2 * GRID_W), 1) < GRID_W
    ks = jnp.clip(Q_ROWS_PER_BLOCK * a - Q_ROWS_PER_BLOCK, 0, GRID_H - NBR_KROWS)
    rows = []
    for ri in range(Q_ROWS_PER_BLOCK):
        r = Q_ROWS_PER_BLOCK * a + ri
        rs = jnp.clip(r - WIN_ROWS // 2, 0, GRID_H - WIN_ROWS)
        tiles = []
        for pair in range(NBR_KROWS // 2):
            idx = []
            for kri in (2 * pair, 2 * pair + 1):
                kr = ks + kri
                valid = jnp.logical_and(kr >= rs, kr < rs + WIN_ROWS)
                idx.append(jnp.where(valid, kr - r + WIN_ROWS - 1, N_DROW))
            tiles.append(jnp.where(half0, tbl[hh, idx[0]], tbl[hh, idx[1]]))
        rows.append(jnp.concatenate(tiles, axis=1))
    return jnp.concatenate(rows, axis=0)


def _nbr_attn_kernel(q_ref, k_ref, v_ref, kc_ref, vc_ref, band_ref, za_ref, o_ref, tbl):
    a = pl.program_id(1)

    @pl.when(a == 0)
    def _():
        _nbr_build_table(band_ref, tbl)

    start = pl.multiple_of(jnp.clip(a * NBR_Q - NBR_Q, 0, DEC_SEQ - NBR_K), NBR_Q)
    head0 = lax.broadcasted_iota(jnp.int32, (1, 2 * HEAD_DIM), 1) < HEAD_DIM
    bias = [_nbr_bias(tbl, hh, a) for hh in range(2)]
    for b in range(DEC_BATCH):
        q = q_ref[b]
        kw = k_ref[b, pl.ds(start, NBR_K), :]
        vw = v_ref[b, pl.ds(start, NBR_K), :]
        kc = kc_ref[b].astype(BF16)
        vc = vc_ref[b].astype(BF16)
        outs = []
        for hh in range(2):
            qh = jnp.where(head0 if hh == 0 else jnp.logical_not(head0), q, jnp.zeros_like(q))
            sw = _dot_nt(qh, kw) + bias[hh]
            sc = _dot_nt(qh, kc)
            outs.append(_softmax_pv([sw, sc], [vw, vc]))
        o = jnp.where(head0, outs[0], outs[1])
        o_ref[b] = (o * _silu(za_ref[b].astype(F32))).astype(BF16)


def _nbr_attention(q4, k4, v4, kc, vc, band, za4):
    first = N_CTX_TOK // DEC_SEQ // DEC_BATCH
    nblk = GRID_H // Q_ROWS_PER_BLOCK
    hw = 2 * HEAD_DIM
    qspec = pl.BlockSpec((DEC_BATCH, NBR_Q, hw), lambda hp, a: (first, a, hp))
    kspec = pl.BlockSpec((DEC_BATCH, DEC_SEQ, hw), lambda hp, a: (first, 0, hp))
    cspec = pl.BlockSpec((DEC_BATCH, PAST_LEN, hw), lambda hp, a: (0, 0, hp))
    bspec = pl.BlockSpec((2, N_DROW + 1, 2 * GRID_W), lambda hp, a: (hp, 0, 0))
    return pl.pallas_call(
        _nbr_attn_kernel,
        grid=(N_HEADS // 2, nblk),
        in_specs=[qspec, kspec, kspec, cspec, cspec, bspec, qspec],
        out_specs=pl.BlockSpec((DEC_BATCH, NBR_Q, hw), lambda hp, a: (0, a, hp)),
        out_shape=jax.ShapeDtypeStruct((DEC_BATCH, DEC_SEQ, ATT_WIDTH), BF16),
        scratch_shapes=[pltpu.VMEM((2, N_DROW + 1, GRID_W, 2 * GRID_W), F32)],
        compiler_params=_params("arbitrary", "arbitrary"),
        name="nbr_attention",
    )(q4, k4, v4, kc, vc, band, za4)


def _nbr_bias_band(rpb):
    ncol = 2 * WIN_COLS - 1
    fill = jnp.full(rpb.shape[:2] + (2 * GRID_W - ncol,), NEG_INF, F32)
    band = jnp.concatenate([rpb[..., WIN_COLS - 1:], fill, rpb[..., :WIN_COLS - 1]], axis=-1)
    return jnp.pad(band, ((0, 0), (0, 1), (0, 0)))


BACK_TM = CHUNK * CH_PER_SEG
BACK_CTX_STEPS = N_CTX_TOK // BACK_TM
BACK_STEPS_PER_DEC_SEQ = DEC_SEQ // BACK_TM


def _back_kernel(xp_ref, xs_ref, y_ref, zs_ref, ac_ref, ad_ref, gs_ref, ga_ref, gate_ref, bglu_ref,
                 wglu_ref, wso_ref, wao_ref, wo_ref, op_ref, os_ref):
    i = pl.program_id(0)

    def compute(x, a2):
        ys = _gelu_tanh(y_ref[0].reshape(BACK_TM, SSM_WIDTH))
        t = jnp.dot(ys.astype(BF16), wglu_ref[...], preferred_element_type=F32) + bglu_ref[...]
        ys = ys * _sigmoid(t) * _silu(zs_ref[...].astype(F32))
        p_s = jnp.dot(ys.astype(BF16), wso_ref[...], preferred_element_type=F32)
        p_a = jnp.dot(a2, wao_ref[...], preferred_element_type=F32)
        merged = (_sigmoid(gs_ref[...].astype(F32)) * p_s + _sigmoid(ga_ref[...].astype(F32)) * p_a)
        return x + gate_ref[0] * jnp.dot(merged.astype(BF16), wo_ref[...], preferred_element_type=F32)

    @pl.when(i < BACK_CTX_STEPS)
    def _():
        op_ref[...] = compute(xp_ref[...], ac_ref[...])

    @pl.when(i >= BACK_CTX_STEPS)
    def _():
        os_ref[...] = compute(xs_ref[...], ad_ref[...])


def _back(xp, xs, y, zs, a_ctx, a_dec, gs, ga, gate3, b_glu, w_glu, w_so, w_ao, w_o):
    steps = N_TOK // BACK_TM
    n0 = BACK_CTX_STEPS
    row = functools.partial(_mod_row, ctx_steps=n0, steps_per_seq=BACK_STEPS_PER_DEC_SEQ)
    lo = lambda i: (jnp.minimum(i, n0 - 1), 0)
    hi = lambda i: (jnp.maximum(i - n0, 0), 0)
    cur = lambda i: (i, 0)
    const = lambda i: (0, 0)
    once = pl.Buffered(1)
    return pl.pallas_call(
        _back_kernel,
        grid=(steps,),
        in_specs=[pl.BlockSpec((BACK_TM, D_MODEL), lo),
                  pl.BlockSpec((BACK_TM, D_MODEL), hi),
                  pl.BlockSpec((1, CH_PER_SEG, None, CHUNK, SSM_WIDTH),
                               lambda i: (i // N_SEG, 0, i % N_SEG, 0, 0)),
                  pl.BlockSpec((BACK_TM, SSM_WIDTH), cur),
                  pl.BlockSpec((BACK_TM, ATT_WIDTH), lo),
                  pl.BlockSpec((BACK_TM, ATT_WIDTH), hi),
                  pl.BlockSpec((BACK_TM, D_MODEL), cur),
                  pl.BlockSpec((BACK_TM, D_MODEL), cur),
                  pl.BlockSpec((1, 1, D_MODEL), lambda i: (row(i), 0, 0)),
                  pl.BlockSpec((1, SSM_WIDTH), const),
                  pl.BlockSpec((SSM_WIDTH, SSM_WIDTH), const, pipeline_mode=once),
                  pl.BlockSpec((SSM_WIDTH, D_MODEL), const, pipeline_mode=once),
                  pl.BlockSpec((ATT_WIDTH, D_MODEL), const, pipeline_mode=once),
                  pl.BlockSpec((D_MODEL, D_MODEL), const, pipeline_mode=once)],
        out_specs=[pl.BlockSpec((BACK_TM, D_MODEL), lo),
                   pl.BlockSpec((BACK_TM, D_MODEL), hi)],
        out_shape=[jax.ShapeDtypeStruct((N_CTX_TOK, D_MODEL), F32),
                   jax.ShapeDtypeStruct((N_DEC_TOK, D_MODEL), F32)],
        compiler_params=_params("arbitrary"),
        name="gated_output",
    )(xp, xs, y, zs, a_ctx, a_dec, gs, ga, gate3, b_glu.reshape(1, SSM_WIDTH), w_glu, w_so, w_ao, w_o)


def _layer(xp, xs, cache_k, cache_v, st_re, st_im, c, c_ctx, norm_w, w_ada, b_ada, w_in, q_norm_w, k_norm_w,
           rpb, a_re, a_im, log_dt, b_re, b_im, c_re, c_im, d, w_glu, b_glu, w_so, w_ao, w_o):
    G, P = SSM_GROUPS, SSM_STATE

    cond8 = jnp.zeros((8, D_MODEL), F32).at[0].set(c_ctx).at[1:1 + DEC_BATCH].set(c)
    mod = _modulation(cond8, w_ada, b_ada)
    shift3 = mod[:1 + DEC_BATCH, None, :D_MODEL]
    scale3 = mod[:1 + DEC_BATCH, None, D_MODEL:2 * D_MODEL]
    gate3 = mod[:1 + DEC_BATCH, None, 2 * D_MODEL:]

    h = _norm_modulate(xp, xs, shift3, scale3, norm_w)
    u = _in_proj(h, w_in, OFF_U, SSM_WIDTH, "u")
    zs = _in_proj(h, w_in, OFF_ZS, SSM_WIDTH, "plain")
    q = _in_proj(h, w_in, OFF_Q, ATT_WIDTH, "q", q_norm_w)
    k, k_ctx32 = _in_proj(h, w_in, OFF_K, ATT_WIDTH, "k", k_norm_w)
    v, v_ctx32 = _in_proj(h, w_in, OFF_V, ATT_WIDTH, "v")
    za = _in_proj(h, w_in, OFF_ZA, ATT_WIDTH, "plain")
    gs = _in_proj(h, w_in, OFF_GS, D_MODEL, "plain")
    ga = _in_proj(h, w_in, OFF_GA, D_MODEL, "plain")

    arow = jnp.transpose(a_re, (1, 0, 2)).reshape(G, 2 * P)
    airow = jnp.transpose(a_im, (1, 0, 2)).reshape(G, 2 * P)
    dtrow = jnp.broadcast_to(jnp.transpose(log_dt, (1, 0))[:, :, None], (G, 2, P)).reshape(G, 2 * P)
    rowp = jnp.stack([arow, airow, dtrow], axis=1)
    colp = jnp.transpose(rowp, (0, 2, 1))
    btr = jnp.transpose(b_re, (1, 3, 0, 2)).reshape(G, SSM_GROUP, 2 * P)
    bti = jnp.transpose(b_im, (1, 3, 0, 2)).reshape(G, SSM_GROUP, 2 * P)
    ctr = jnp.transpose(c_re, (1, 0, 3, 2)).reshape(G, 2 * P, SSM_GROUP)
    cti = jnp.transpose(c_im, (1, 0, 3, 2)).reshape(G, 2 * P, SSM_GROUP)
    t_op, s_op, r_op, lam, powc = _ssm_operators(rowp, colp, btr, bti, ctr, cti)

    dtile = d.reshape(1, SSM_WIDTH)
    h0c = jnp.zeros((G, BATCH, 2 * P), F32)
    h0d_re = jnp.transpose(st_re, (2, 0, 1, 3)).reshape(G, DEC_BATCH, 2 * P)
    h0d_im = jnp.transpose(st_im, (2, 0, 1, 3)).reshape(G, DEC_BATCH, 2 * P)
    y2d, fin = _ssm_scan(u.reshape(N_TOK, SSM_WIDTH), t_op, s_op, r_op, lam, powc, dtile,
                         h0c, h0c, h0d_re, h0d_im)
    y = y2d.reshape(2, CH_PER_SEG, N_SEG, CHUNK, SSM_WIDTH)

    a_ctx = _ctx_attention(q, k, v, za)
    nseq4 = N_TOK // DEC_SEQ
    a_dec = _nbr_attention(q.reshape(nseq4, DEC_SEQ, ATT_WIDTH), k.reshape(nseq4, DEC_SEQ, ATT_WIDTH),
                           v.reshape(nseq4, DEC_SEQ, ATT_WIDTH),
                           cache_k.reshape(DEC_BATCH, PAST_LEN, ATT_WIDTH),
                           cache_v.reshape(DEC_BATCH, PAST_LEN, ATT_WIDTH),
                           _nbr_bias_band(rpb.astype(F32)),
                           za.reshape(nseq4, DEC_SEQ, ATT_WIDTH))

    yp, ys_out = _back(xp, xs, y, zs, a_ctx, a_dec.reshape(N_DEC_TOK, ATT_WIDTH), gs, ga, gate3, b_glu,
                       w_glu.astype(BF16), w_so.astype(BF16), w_ao.astype(BF16), w_o.astype(BF16))

    fin_ctx = fin[0]
    new_re = jnp.transpose(fin_ctx[:, :, :2 * P].reshape(G, BATCH, 2, P), (1, 2, 0, 3))
    new_im = jnp.transpose(fin_ctx[:, :, 2 * P:].reshape(G, BATCH, 2, P), (1, 2, 0, 3))
    return yp, ys_out, k_ctx32, v_ctx32, new_re, new_im


def kernel(x_prompt, x_sample, cache_k, cache_v, state_ssm_re, state_ssm_im, c, c_ctx, norm_w, w_ada, b_ada,
           w_in, q_norm_w, k_norm_w, rel_pos_bias, ssm_a_re, ssm_a_im, ssm_log_dt, ssm_b_re, ssm_b_im,
           ssm_c_re, ssm_c_im, ssm_d, w_glu, b_glu, w_ssm_out, w_att_out, w_o):
    depth = norm_w.shape[0]
    xp = x_prompt.reshape(N_CTX_TOK, D_MODEL)
    xs = x_sample.reshape(N_DEC_TOK, D_MODEL)
    new_k, new_v, new_re, new_im = [], [], [], []
    for l in range(depth):
        xp, xs, kl, vl, rl, il = _layer(
            xp, xs, cache_k[:, l], cache_v[:, l], state_ssm_re[:, l], state_ssm_im[:, l], c, c_ctx,
            norm_w[l], w_ada[l], b_ada[l], w_in[l], q_norm_w[l], k_norm_w[l], rel_pos_bias[l],
            ssm_a_re[l], ssm_a_im[l], ssm_log_dt[l], ssm_b_re[l], ssm_b_im[l], ssm_c_re[l], ssm_c_im[l],
            ssm_d[l], w_glu[l], b_glu[l], w_ssm_out[l], w_att_out[l], w_o[l])
        new_k.append(kl.reshape(BATCH, SEQ, N_HEADS, HEAD_DIM))
        new_v.append(vl.reshape(BATCH, SEQ, N_HEADS, HEAD_DIM))
        new_re.append(rl)
        new_im.append(il)
    return (xp.reshape(BATCH, SEQ, D_MODEL), xs.reshape(DEC_BATCH, DEC_SEQ, D_MODEL),
            jnp.stack(new_k, axis=1), jnp.stack(new_v, axis=1),
            jnp.stack(new_re, axis=1), jnp.stack(new_im, axis=1))
```

```python
import functools
import math

import jax
import jax.numpy as jnp
from jax import lax
from jax.experimental import pallas as pl
from jax.experimental.pallas import tpu as pltpu

D_MODEL = 2048
BATCH = 16
SEQ = 256
DEC_BATCH = 2
DEC_SEQ = 2048
PAST_LEN = 512
GRID_W = 64
GRID_H = DEC_SEQ // GRID_W
SSM_WIDTH = D_MODEL // 2
SSM_GROUP = 16
SSM_GROUPS = SSM_WIDTH // SSM_GROUP
SSM_STATE = 64
N_HEADS = 16
HEAD_DIM = 64
ATT_WIDTH = N_HEADS * HEAD_DIM
WIN_ROWS = 8
WIN_COLS = 16
EPS = 1e-6
NEG_INF = -1e30

N_CTX_TOK = BATCH * SEQ
N_DEC_TOK = DEC_BATCH * DEC_SEQ
N_TOK = N_CTX_TOK + N_DEC_TOK

OFF_U = 0
OFF_ZS = SSM_WIDTH
OFF_Q = 2 * SSM_WIDTH
OFF_K = OFF_Q + ATT_WIDTH
OFF_V = OFF_K + ATT_WIDTH
OFF_ZA = OFF_V + ATT_WIDTH
OFF_GS = OFF_ZA + ATT_WIDTH
OFF_GA = OFF_GS + D_MODEL

CHUNK = 16
FLAT = CHUNK * SSM_GROUP
N_SEG = 16
CH_PER_SEG = 16
DSTATE = 2 * SSM_STATE

Q_ROWS_PER_BLOCK = 4
NBR_Q = Q_ROWS_PER_BLOCK * GRID_W
NBR_KROWS = 12
NBR_K = NBR_KROWS * GRID_W

VMEM_LIMIT = 56 * 1024 * 1024

F32 = jnp.float32
BF16 = jnp.bfloat16
HIGHEST = lax.Precision.HIGHEST


def _sigmoid(x):
    return 1.0 / (1.0 + jnp.exp(-x))


def _silu(x):
    return x * _sigmoid(x)


def _gelu_tanh(x):
    return 0.5 * x * (1.0 + jnp.tanh(math.sqrt(2.0 / math.pi) * (x + 0.044715 * (x * x * x))))


def _cmul(ar, ai, br, bi):
    return ar * br - ai * bi, ar * bi + ai * br


def _dot_nt(a, b):
    return lax.dot_general(a, b, (((1,), (1,)), ((), ())), preferred_element_type=F32)


def _params(*sem):
    return pltpu.CompilerParams(dimension_semantics=sem, vmem_limit_bytes=VMEM_LIMIT)


def _mod_kernel(cond_ref, w_ref, b_ref, o_ref):
    c = cond_ref[...]
    s = _silu(c).astype(BF16)
    o_ref[...] = jnp.dot(s, w_ref[...].astype(BF16), preferred_element_type=F32) + b_ref[...]


def _modulation(cond8, w_ada, b_ada):
    tn = 768
    n = w_ada.shape[1]
    return pl.pallas_call(
        _mod_kernel,
        grid=(n // tn,),
        in_specs=[pl.BlockSpec((8, D_MODEL), lambda j: (0, 0)),
                  pl.BlockSpec((D_MODEL, tn), lambda j: (0, j)),
                  pl.BlockSpec((1, tn), lambda j: (0, j))],
        out_specs=pl.BlockSpec((8, tn), lambda j: (0, j)),
        out_shape=jax.ShapeDtypeStruct((8, n), F32),
        compiler_params=_params("arbitrary"),
        name="modulation",
    )(cond8, w_ada, b_ada.reshape(1, n))


NORM_TM = 512
NORM_CTX_STEPS = N_CTX_TOK // NORM_TM
NORM_STEPS_PER_DEC_SEQ = DEC_SEQ // NORM_TM


def _mod_row(i, ctx_steps, steps_per_seq):
    return jnp.where(i < ctx_steps, 0, 1 + (i - ctx_steps) // steps_per_seq)


def _norm_kernel(xp_ref, xs_ref, shift_ref, scale_ref, nw_ref, o_ref):
    i = pl.program_id(0)

    def body(x):
        ms = jnp.mean(x * x, axis=-1, keepdims=True)
        y = x * lax.rsqrt(ms + EPS) * nw_ref[...]
        o_ref[...] = (y * (1.0 + scale_ref[0]) + shift_ref[0]).astype(BF16)

    @pl.when(i < NORM_CTX_STEPS)
    def _():
        body(xp_ref[...])

    @pl.when(i >= NORM_CTX_STEPS)
    def _():
        body(xs_ref[...])


def _norm_modulate(xp, xs, shift3, scale3, norm_w):
    steps = N_TOK // NORM_TM
    row = functools.partial(_mod_row, ctx_steps=NORM_CTX_STEPS, steps_per_seq=NORM_STEPS_PER_DEC_SEQ)
    return pl.pallas_call(
        _norm_kernel,
        grid=(steps,),
        in_specs=[pl.BlockSpec((NORM_TM, D_MODEL), lambda i: (jnp.minimum(i, NORM_CTX_STEPS - 1), 0)),
                  pl.BlockSpec((NORM_TM, D_MODEL), lambda i: (jnp.maximum(i - NORM_CTX_STEPS, 0), 0)),
                  pl.BlockSpec((1, 1, D_MODEL), lambda i: (row(i), 0, 0)),
                  pl.BlockSpec((1, 1, D_MODEL), lambda i: (row(i), 0, 0)),
                  pl.BlockSpec((1, D_MODEL), lambda i: (0, 0))],
        out_specs=pl.BlockSpec((NORM_TM, D_MODEL), lambda i: (i, 0)),
        out_shape=jax.ShapeDtypeStruct((N_TOK, D_MODEL), BF16),
        compiler_params=_params("arbitrary"),
        name="norm_modulate",
    )(xp, xs, shift3, scale3, norm_w.reshape(1, D_MODEL))


PROJ_TM = 1024
PROJ_TN = 1024
PROJ_CTX_STEPS = N_CTX_TOK // PROJ_TM
PROJ_SEGS = PROJ_TM // (CHUNK * CH_PER_SEG)


def _head_group_ones():
    r = lax.broadcasted_iota(jnp.int32, (256, 256), 0) // HEAD_DIM
    c = lax.broadcasted_iota(jnp.int32, (256, 256), 1) // HEAD_DIM
    return jnp.where(r == c, 1.0, 0.0).astype(BF16)


def _head_rms(acc, nw):
    ones = _head_group_ones()
    outs = []
    for c in range(acc.shape[1] // 256):
        a = acc[:, c * 256:(c + 1) * 256]
        ssum = jnp.dot((a * a).astype(BF16), ones, preferred_element_type=F32)
        outs.append(a * lax.rsqrt(ssum * (1.0 / HEAD_DIM) + EPS))
    return jnp.concatenate(outs, axis=1) * nw


def _proj_kernel(*refs, mode):
    if mode in ("plain", "u"):
        h_ref, w_ref, o_ref, wbf = refs
    elif mode == "q":
        h_ref, w_ref, nw_ref, o_ref, wbf = refs
    elif mode == "k":
        h_ref, w_ref, nw_ref, o_ref, of_ref, wbf = refs
    else:
        h_ref, w_ref, o_ref, of_ref, wbf = refs
    i = pl.program_id(1)

    @pl.when(i == 0)
    def _():
        wbf[...] = w_ref[...].astype(BF16)

    acc = jnp.dot(h_ref[...], wbf[...], preferred_element_type=F32)
    if mode == "q":
        acc = _head_rms(acc, nw_ref[...]) * (HEAD_DIM ** -0.5)
    elif mode == "k":
        acc = _head_rms(acc, nw_ref[...])
    if mode == "u":
        for seg in range(PROJ_SEGS):
            for ch in range(CH_PER_SEG):
                r0 = (seg * CH_PER_SEG + ch) * CHUNK
                o_ref[0, ch, seg] = acc[r0:r0 + CHUNK, :]
    else:
        o_ref[...] = acc.astype(BF16)
    if mode in ("k", "v"):
        @pl.when(i < PROJ_CTX_STEPS)
        def _():
            of_ref[...] = acc


def _in_proj(h, w_in, col_off, ncols, mode, norm_w=None):
    nj = ncols // PROJ_TN
    ni = N_TOK // PROJ_TM
    joff = col_off // PROJ_TN
    in_specs = [pl.BlockSpec((PROJ_TM, D_MODEL), lambda j, i: (i, 0)),
                pl.BlockSpec((D_MODEL, PROJ_TN), lambda j, i: (0, joff + j))]
    args = [h, w_in]
    if mode in ("q", "k"):
        in_specs.append(pl.BlockSpec((1, PROJ_TN), lambda j, i: (0, 0)))
        args.append(jnp.tile(norm_w.reshape(1, HEAD_DIM), (1, PROJ_TN // HEAD_DIM)))
    out_specs = [pl.BlockSpec((PROJ_TM, PROJ_TN), lambda j, i: (i, j))]
    out_shape = [jax.ShapeDtypeStruct((N_TOK, ncols), BF16)]
    if mode == "u":
        tiles_per_path = N_SEG // PROJ_SEGS
        out_specs = [pl.BlockSpec((1, CH_PER_SEG, PROJ_SEGS, CHUNK, PROJ_TN),
                                  lambda j, i: (i // tiles_per_path, 0, i % tiles_per_path, 0, j))]
        out_shape = [jax.ShapeDtypeStruct((2, CH_PER_SEG, N_SEG, CHUNK, ncols), F32)]
    if mode in ("k", "v"):
        out_specs.append(pl.BlockSpec((PROJ_TM, PROJ_TN),
                                      lambda j, i: (jnp.minimum(i, PROJ_CTX_STEPS - 1), j)))
        out_shape.append(jax.ShapeDtypeStruct((N_CTX_TOK, ncols), F32))
    res = pl.pallas_call(
        functools.partial(_proj_kernel, mode=mode),
        grid=(nj, ni),
        in_specs=in_specs,
        out_specs=out_specs,
        out_shape=out_shape,
        scratch_shapes=[pltpu.VMEM((D_MODEL, PROJ_TN), BF16)],
        compiler_params=_params("arbitrary", "arbitrary"),
        name="in_proj_" + mode,
    )(*args)
    return res if mode in ("k", "v") else res[0]


SSM_GB = 4


def _pow_select(e, pows):
    rr = jnp.where((e & 1) != 0, pows[0][0], 1.0)
    ri = jnp.where((e & 1) != 0, pows[0][1], 0.0)
    for b in range(1, len(pows)):
        bit = (e & (1 << b)) != 0
        fr = jnp.where(bit, pows[b][0], 1.0)
        fi = jnp.where(bit, pows[b][1], 0.0)
        rr, ri = _cmul(rr, ri, fr, fi)
    return rr, ri


def _discretize(a_re, a_im, log_dt):
    lr = jnp.minimum(a_re, -1e-4)
    li = a_im
    dt = jnp.exp(log_dt)
    mag = jnp.exp(lr * dt)
    br = mag * jnp.cos(li * dt)
    bi = mag * jnp.sin(li * dt)
    den = lr * lr + li * li
    nr = br - 1.0
    cr = (nr * lr + bi * li) / den
    ci = (bi * lr - nr * li) / den
    return (br, bi), (cr, ci)


def _squarings(pr, pi, n):
    out = [(pr, pi)]
    for _ in range(n):
        pr, pi = _cmul(pr, pi, pr, pi)
        out.append((pr, pi))
    return out


def _dot_nt_exact(a, b):
    return lax.dot_general(a, b, (((1,), (1,)), ((), ())), preferred_element_type=F32, precision=HIGHEST)


def _ssm_ops_kernel(rowp_ref, btr_ref, bti_ref, ctr_ref, cti_ref, t_ref, s_ref, rt_ref, lam_ref, powc_ref):
    lane_b = lax.broadcasted_iota(jnp.int32, (SSM_GROUP, DSTATE), 1)
    lane_t = lax.broadcasted_iota(jnp.int32, (SSM_GROUP, FLAT), 1)
    row_c = lax.broadcasted_iota(jnp.int32, (CH_PER_SEG, DSTATE), 0)
    lane_c = lax.broadcasted_iota(jnp.int32, (CH_PER_SEG, DSTATE), 1)
    exp_c = jnp.where(lane_c < SSM_STATE, row_c, (CH_PER_SEG - 1) - row_c)

    for g in range(SSM_GB):
        rp = rowp_ref[g]
        (lbr, lbi), (cfr, cfi) = _discretize(rp[0:1], rp[1:2], rp[2:3])
        sq = _squarings(lbr, lbi, 8)
        bbr, bbi = _cmul(btr_ref[g], bti_ref[g], cfr, cfi)

        ptr, pti = _pow_select(exp_c, sq[:4])

        s_re, s_im = [], []
        for sp in range(CHUNK):
            e = CHUNK - 1 - sp
            br, bi = _cmul(bbr, bbi, ptr[e:e + 1], pti[e:e + 1])
            s_re.append(br)
            s_im.append(bi)
        s_ref[g] = jnp.concatenate([jnp.concatenate(s_re, axis=0), jnp.concatenate(s_im, axis=0)],
                                   axis=1).astype(BF16)

        xcr = jnp.concatenate([ctr_ref[g]] * CHUNK, axis=0)
        xci = jnp.concatenate([cti_ref[g]] * CHUNK, axis=0)
        xpr = jnp.broadcast_to(ptr[:, None, :], (CHUNK, SSM_GROUP, DSTATE)).reshape(FLAT, DSTATE)
        xpi = jnp.broadcast_to(pti[:, None, :], (CHUNK, SSM_GROUP, DSTATE)).reshape(FLAT, DSTATE)
        ykr, yki = _cmul(xcr, xci, xpr, xpi)
        yrr, yri = _cmul(ykr, yki, lbr, lbi)
        rt_ref[g] = jnp.concatenate([yrr, -yri], axis=1).astype(BF16)

        fwd = lane_b < SSM_STATE
        lhs = jnp.concatenate([jnp.where(fwd, bbr, 0.0), jnp.where(fwd, bbi, 0.0),
                               jnp.where(fwd, 0.0, bbr), jnp.where(fwd, 0.0, bbi)], axis=0)
        p1 = _dot_nt_exact(lhs, ykr)
        p2 = _dot_nt_exact(lhs, yki)
        kf = p1[0:16] - p2[16:32]
        kb = p1[32:48] - p2[48:64]
        for sp in range(CHUNK):
            tf = kf if sp == 0 else pltpu.roll(kf, SSM_GROUP * sp, axis=1)
            tf = jnp.where(lane_t >= SSM_GROUP * sp, tf, 0.0)
            shift = (FLAT - SSM_GROUP * (CHUNK - 1 - sp)) % FLAT
            tb = kb if shift == 0 else pltpu.roll(kb, shift, axis=1)
            tb = jnp.where(lane_t < SSM_GROUP * (sp + 1), tb, 0.0)
            t_ref[g, sp * SSM_GROUP:(sp + 1) * SSM_GROUP, :] = (tf + tb).astype(BF16)

        l16 = sq[4]
        l256 = sq[8]
        lam_ref[g] = jnp.concatenate([l16[0], l16[1], l256[0], l256[1],
                                      jnp.zeros((4, DSTATE), F32)], axis=0)
        q16 = _squarings(l16[0], l16[1], 3)
        pcr, pci = _pow_select(exp_c, q16)
        powc_ref[g, 0] = pcr
        powc_ref[g, 1] = pci


def _ssm_operators(rowp, btr, bti, ctr, cti):
    G = SSM_GROUPS
    gb = SSM_GB
    return pl.pallas_call(
        _ssm_ops_kernel,
        grid=(G // gb,),
        in_specs=[pl.BlockSpec((gb, 3, DSTATE), lambda g: (g, 0, 0)),
                  pl.BlockSpec((gb, SSM_GROUP, DSTATE), lambda g: (g, 0, 0)),
                  pl.BlockSpec((gb, SSM_GROUP, DSTATE), lambda g: (g, 0, 0)),
                  pl.BlockSpec((gb, SSM_GROUP, DSTATE), lambda g: (g, 0, 0)),
                  pl.BlockSpec((gb, SSM_GROUP, DSTATE), lambda g: (g, 0, 0))],
        out_specs=[pl.BlockSpec((gb, FLAT, FLAT), lambda g: (g, 0, 0)),
                   pl.BlockSpec((gb, FLAT, FLAT), lambda g: (g, 0, 0)),
                   pl.BlockSpec((gb, FLAT, FLAT), lambda g: (g, 0, 0)),
                   pl.BlockSpec((gb, 8, DSTATE), lambda g: (g, 0, 0)),
                   pl.BlockSpec((gb, 2, CH_PER_SEG, DSTATE), lambda g: (g, 0, 0, 0))],
        out_shape=[jax.ShapeDtypeStruct((G, FLAT, FLAT), BF16),
                   jax.ShapeDtypeStruct((G, FLAT, FLAT), BF16),
                   jax.ShapeDtypeStruct((G, FLAT, FLAT), BF16),
                   jax.ShapeDtypeStruct((G, 8, DSTATE), F32),
                   jax.ShapeDtypeStruct((G, 2, CH_PER_SEG, DSTATE), F32)],
        compiler_params=_params("arbitrary"),
        name="ssm_operators",
    )(rowp, btr, bti, ctr, cti)


SSM_PATHS = ((BATCH, SEQ // (CHUNK * CH_PER_SEG)), (DEC_BATCH, DEC_SEQ // (CHUNK * CH_PER_SEG)))


def _ssm_path(u, s_op, t_op, rt_op, lam, powc_ref, h0r, h0i, nseg):
    lane = lax.broadcasted_iota(jnp.int32, (N_SEG, DSTATE), 1)
    isf = lane < SSM_STATE
    l16r, l16i, l256r, l256i = lam[0:1], lam[1:2], lam[2:3], lam[3:4]

    z = jnp.dot(u, s_op, preferred_element_type=F32)
    zre, zim = z[:, :DSTATE], z[:, DSTATE:]

    hr = jnp.zeros((N_SEG, DSTATE), F32)
    hi = jnp.zeros((N_SEG, DSTATE), F32)
    hist = []
    for k in range(CH_PER_SEG):
        hist.append((hr, hi))
        kb = CH_PER_SEG - 1 - k
        zr = jnp.where(isf, zre[k * N_SEG:(k + 1) * N_SEG], zre[kb * N_SEG:(kb + 1) * N_SEG])
        zi = jnp.where(isf, zim[k * N_SEG:(k + 1) * N_SEG], zim[kb * N_SEG:(kb + 1) * N_SEG])
        nr, ni = _cmul(l16r, l16i, hr, hi)
        hr, hi = nr + zr, ni + zi

    if nseg > 1:
        h0r = jnp.broadcast_to(h0r[:, None, :], (N_SEG // nseg, nseg, DSTATE)).reshape(N_SEG, DSTATE)
        h0i = jnp.broadcast_to(h0i[:, None, :], (N_SEG // nseg, nseg, DSTATE)).reshape(N_SEG, DSTATE)
    seg = lax.broadcasted_iota(jnp.int32, (N_SEG, DSTATE), 0) & (nseg - 1)
    segpow = _squarings(l256r, l256i, max(int(math.log2(nseg)), 0))

    def shifted(x, d):
        dn = jnp.where(seg >= d, pltpu.roll(x, d, axis=0), 0.0)
        up = jnp.where(seg <= nseg - 1 - d, pltpu.roll(x, N_SEG - d, axis=0), 0.0)
        return jnp.where(isf, dn, up)

    pr, pi = hr, hi
    d = 1
    lvl = 0
    while d < nseg:
        ar, ai = _cmul(segpow[lvl][0], segpow[lvl][1], shifted(pr, d), shifted(pi, d))
        pr, pi = pr + ar, pi + ai
        d *= 2
        lvl += 1
    if nseg > 1:
        e_in = jnp.where(isf, seg, nseg - 1 - seg)
        wr, wi = _pow_select(e_in, segpow[:lvl])
        ar, ai = _cmul(wr, wi, h0r, h0i)
        hsr, hsi = shifted(pr, 1) + ar, shifted(pi, 1) + ai
    else:
        hsr, hsi = h0r, h0i
    er, ei = _cmul(segpow[lvl][0], segpow[lvl][1], h0r, h0i)
    fin = jnp.concatenate([pr + er, pi + ei], axis=1)

    rows = []
    for c in range(CH_PER_SEG):
        cb = CH_PER_SEG - 1 - c
        lr = jnp.where(isf, hist[c][0], hist[cb][0])
        li = jnp.where(isf, hist[c][1], hist[cb][1])
        ar, ai = _cmul(powc_ref[0, c:c + 1, :], powc_ref[1, c:c + 1, :], hsr, hsi)
        rows.append(jnp.concatenate([lr + ar, li + ai], axis=1))
    hent = jnp.concatenate(rows, axis=0).astype(BF16)

    y = jnp.dot(u, t_op, preferred_element_type=F32) + _dot_nt(hent, rt_op)
    return y, fin


GROUPS_PER_STEP = 128 // SSM_GROUP
GROUPS_PER_ITER = 8


def _block_transpose(arrs, blk):
    n = len(arrs)
    width = arrs[0].shape[1]
    j = lax.broadcasted_iota(jnp.int32, arrs[0].shape, 1) // blk
    k = n // 2
    while k >= 1:
        bit = (j & k) != 0
        new = list(arrs)
        for x in range(n):
            if x & k == 0:
                a, b = arrs[x], arrs[x | k]
                new[x] = jnp.where(bit, pltpu.roll(b, k * blk, axis=1), a)
                new[x | k] = jnp.where(bit, b, pltpu.roll(a, width - k * blk, axis=1))
        arrs = new
        k //= 2
    return arrs


def _ssm_kernel(u_ref, t_ref, s_ref, r_ref, lam_ref, powc_ref, d_ref, h0cr_ref, h0ci_ref, h0dr_ref, h0di_ref,
                y_ref, fin_ref, ubuf, ybuf):
    h0 = ((h0cr_ref, h0ci_ref), (h0dr_ref, h0di_ref))
    nhalf = FLAT // 128
    rows = N_SEG * CH_PER_SEG
    def token_rows(path, tok, r0, nrows):
        return pl.ds(path * rows * CHUNK + r0 * CHUNK + tok, nrows, stride=CHUNK)

    in_rows = rows
    out_rows = rows
    for path, (_, nseg) in enumerate(SSM_PATHS):
        for half in range(nhalf):
            for r0 in range(0, rows, in_rows):
                toks = [pltpu.bitcast(
                    u_ref[token_rows(path, half * GROUPS_PER_STEP + sb, r0, in_rows), :].astype(BF16), jnp.uint32)
                    for sb in range(GROUPS_PER_STEP)]
                grouped = _block_transpose(toks, SSM_GROUP)
                for g in range(GROUPS_PER_STEP):
                    ubuf[g, r0:r0 + in_rows, half * 128:(half + 1) * 128] = pltpu.bitcast(grouped[g], BF16)

        def body(gi, carry, path=path, nseg=nseg):
            for dg in range(GROUPS_PER_ITER):
                g = gi * GROUPS_PER_ITER + dg
                y, fin = _ssm_path(ubuf[g], s_ref[g], t_ref[g], r_ref[g], lam_ref[g], powc_ref.at[g],
                                   h0[path][0][g], h0[path][1][g], nseg)
                ybuf[g] = y
                fin_ref[path, g] = fin
            return carry

        lax.fori_loop(0, GROUPS_PER_STEP // GROUPS_PER_ITER, body, 0)

        for half in range(nhalf):
            for r0 in range(0, rows, out_rows):
                grouped = [ybuf[g, r0:r0 + out_rows, half * 128:(half + 1) * 128] for g in range(GROUPS_PER_STEP)]
                toks = _block_transpose(grouped, SSM_GROUP)
                for sb in range(GROUPS_PER_STEP):
                    sel = token_rows(path, half * GROUPS_PER_STEP + sb, r0, out_rows)
                    y_ref[sel, :] = toks[sb] + u_ref[sel, :] * d_ref[...]


def _ssm_scan(u2d, t_op, s_op, r_op, lam, powc, dtile, h0c_re, h0c_im, h0d_re, h0d_im):
    G = SSM_GROUPS
    gs = GROUPS_PER_STEP
    rows = N_SEG * CH_PER_SEG
    return pl.pallas_call(
        _ssm_kernel,
        grid=(G // gs,),
        in_specs=[
            pl.BlockSpec((N_TOK, 128), lambda o: (0, o)),
            pl.BlockSpec((gs, FLAT, FLAT), lambda o: (o, 0, 0)),
            pl.BlockSpec((gs, FLAT, FLAT), lambda o: (o, 0, 0)),
            pl.BlockSpec((gs, FLAT, FLAT), lambda o: (o, 0, 0)),
            pl.BlockSpec((gs, 8, DSTATE), lambda o: (o, 0, 0)),
            pl.BlockSpec((gs, 2, CH_PER_SEG, DSTATE), lambda o: (o, 0, 0, 0)),
            pl.BlockSpec((1, 128), lambda o: (0, o)),
            pl.BlockSpec((gs, BATCH, DSTATE), lambda o: (o, 0, 0)),
            pl.BlockSpec((gs, BATCH, DSTATE), lambda o: (o, 0, 0)),
            pl.BlockSpec((gs, DEC_BATCH, DSTATE), lambda o: (o, 0, 0)),
            pl.BlockSpec((gs, DEC_BATCH, DSTATE), lambda o: (o, 0, 0))],
        out_specs=[pl.BlockSpec((N_TOK, 128), lambda o: (0, o)),
                   pl.BlockSpec((2, gs, N_SEG, 2 * DSTATE), lambda o: (0, o, 0, 0))],
        out_shape=[jax.ShapeDtypeStruct((N_TOK, SSM_WIDTH), F32),
                   jax.ShapeDtypeStruct((2, G, N_SEG, 2 * DSTATE), F32)],
        scratch_shapes=[pltpu.VMEM((gs, rows, FLAT), BF16), pltpu.VMEM((gs, rows, FLAT), F32)],
        compiler_params=_params("arbitrary"),
        name="ssm_scan",
    )(u2d, t_op, s_op, r_op, lam, powc, dtile, h0c_re, h0c_im, h0d_re, h0d_im)


CTX_BB = 4


def _softmax_pv(scores, values):
    m = scores[0].max(axis=-1, keepdims=True)
    for s in scores[1:]:
        m = jnp.maximum(m, s.max(axis=-1, keepdims=True))
    l = None
    o = None
    for s, v in zip(scores, values):
        p = jnp.exp(s - m)
        ls = p.sum(axis=-1, keepdims=True)
        os_ = jnp.dot(p.astype(BF16), v, preferred_element_type=F32)
        l = ls if l is None else l + ls
        o = os_ if o is None else o + os_
    return o / l


def _ctx_attn_kernel(q_ref, k_ref, v_ref, za_ref, o_ref):
    head0 = lax.broadcasted_iota(jnp.int32, (1, 2 * HEAD_DIM), 1) < HEAD_DIM
    for b in range(CTX_BB):
        sl = slice(b * SEQ, (b + 1) * SEQ)
        q, k, v = q_ref[sl, :], k_ref[sl, :], v_ref[sl, :]
        outs = []
        for hh in range(2):
            qh = jnp.where(head0 if hh == 0 else jnp.logical_not(head0), q, jnp.zeros_like(q))
            outs.append(_softmax_pv([_dot_nt(qh, k)], [v]))
        o = jnp.where(head0, outs[0], outs[1])
        o_ref[sl, :] = (o * _silu(za_ref[sl, :].astype(F32))).astype(BF16)


def _ctx_attention(q, k, v, za):
    rows = CTX_BB * SEQ
    spec = pl.BlockSpec((rows, 2 * HEAD_DIM), lambda b, hp: (b, hp))
    return pl.pallas_call(
        _ctx_attn_kernel,
        grid=(BATCH // CTX_BB, N_HEADS // 2),
        in_specs=[spec, spec, spec, spec],
        out_specs=spec,
        out_shape=jax.ShapeDtypeStruct((N_CTX_TOK, ATT_WIDTH), BF16),
        compiler_params=_params("arbitrary", "arbitrary"),
        name="ctx_attention",
    )(q, k, v, za)


N_DROW = 2 * WIN_ROWS - 1


def _nbr_build_table(band_ref, tbl):
    lane = lax.broadcasted_iota(jnp.int32, (GRID_W, 2 * GRID_W), 1)
    qc = lax.broadcasted_iota(jnp.int32, (GRID_W, 2 * GRID_W), 0)
    kc = lane & (GRID_W - 1)
    cs = jnp.clip(qc - WIN_COLS // 2, 0, GRID_W - WIN_COLS)
    valid = jnp.logical_and(kc >= cs, kc < cs + WIN_COLS)
    for hh in range(2):
        for dr in range(N_DROW):
            x = jnp.broadcast_to(band_ref[hh, dr:dr + 1, :], (GRID_W, 2 * GRID_W))
            lo = pltpu.roll(x, 0, axis=1, stride=1, stride_axis=0)
            hi = pltpu.roll(x, GRID_W, axis=1, stride=1, stride_axis=0)
            tbl[hh, dr] = jnp.where(valid, jnp.where(lane < GRID_W, lo, hi), NEG_INF)
        tbl[hh, N_DROW] = jnp.full((GRID_W, 2 * GRID_W), NEG_INF, F32)


def _nbr_bias(tbl, hh, a):
    half0 = lax.broadcasted_iota(jnp.int32, (1, 2 * GRID_W), 1) < GRID_W
    ks = jnp.clip(Q_ROWS_PER_BLOCK * a - Q_ROWS_PER_BLOCK, 0, GRID_H - NBR_KROWS)
    rows = []
    for ri in range(Q_ROWS_PER_BLOCK):
        r = Q_ROWS_PER_BLOCK * a + ri
        rs = jnp.clip(r - WIN_ROWS // 2, 0, GRID_H - WIN_ROWS)
        tiles = []
        for pair in range(NBR_KROWS // 2):
            idx = []
            for kri in (2 * pair, 2 * pair + 1):
                kr = ks + kri
                valid = jnp.logical_and(kr >= rs, kr < rs + WIN_ROWS)
                idx.append(jnp.where(valid, kr - r + WIN_ROWS - 1, N_DROW))
            tiles.append(jnp.where(half0, tbl[hh, idx[0]], tbl[hh, idx[1]]))
        rows.append(jnp.concatenate(tiles, axis=1))
    return jnp.concatenate(rows, axis=0)


def _nbr_attn_kernel(q_ref, k_ref, v_ref, kc_ref, vc_ref, band_ref, za_ref, o_ref, tbl):
    a = pl.program_id(1)

    @pl.when(a == 0)
    def _():
        _nbr_build_table(band_ref, tbl)

    start = pl.multiple_of(jnp.clip(a * NBR_Q - NBR_Q, 0, DEC_SEQ - NBR_K), NBR_Q)
    head0 = lax.broadcasted_iota(jnp.int32, (1, 2 * HEAD_DIM), 1) < HEAD_DIM
    bias = [_nbr_bias(tbl, hh, a) for hh in range(2)]
    for b in range(DEC_BATCH):
        q = q_ref[b]
        kw = k_ref[b, pl.ds(start, NBR_K), :]
        vw = v_ref[b, pl.ds(start, NBR_K), :]
        kc = kc_ref[b].astype(BF16)
        vc = vc_ref[b].astype(BF16)
        outs = []
        for hh in range(2):
            qh = jnp.where(head0 if hh == 0 else jnp.logical_not(head0), q, jnp.zeros_like(q))
            sw = _dot_nt(qh, kw) + bias[hh]
            sc = _dot_nt(qh, kc)
            outs.append(_softmax_pv([sw, sc], [vw, vc]))
        o = jnp.where(head0, outs[0], outs[1])
        o_ref[b] = (o * _silu(za_ref[b].astype(F32))).astype(BF16)


def _nbr_attention(q4, k4, v4, kc, vc, band, za4):
    first = N_CTX_TOK // DEC_SEQ // DEC_BATCH
    nblk = GRID_H // Q_ROWS_PER_BLOCK
    hw = 2 * HEAD_DIM
    qspec = pl.BlockSpec((DEC_BATCH, NBR_Q, hw), lambda hp, a: (first, a, hp))
    kspec = pl.BlockSpec((DEC_BATCH, DEC_SEQ, hw), lambda hp, a: (first, 0, hp))
    cspec = pl.BlockSpec((DEC_BATCH, PAST_LEN, hw), lambda hp, a: (0, 0, hp))
    bspec = pl.BlockSpec((2, N_DROW + 1, 2 * GRID_W), lambda hp, a: (hp, 0, 0))
    return pl.pallas_call(
        _nbr_attn_kernel,
        grid=(N_HEADS // 2, nblk),
        in_specs=[qspec, kspec, kspec, cspec, cspec, bspec, qspec],
        out_specs=pl.BlockSpec((DEC_BATCH, NBR_Q, hw), lambda hp, a: (0, a, hp)),
        out_shape=jax.ShapeDtypeStruct((DEC_BATCH, DEC_SEQ, ATT_WIDTH), BF16),
        scratch_shapes=[pltpu.VMEM((2, N_DROW + 1, GRID_W, 2 * GRID_W), F32)],
        compiler_params=_params("arbitrary", "arbitrary"),
        name="nbr_attention",
    )(q4, k4, v4, kc, vc, band, za4)


def _nbr_bias_band(rpb):
    ncol = 2 * WIN_COLS - 1
    fill = jnp.full(rpb.shape[:2] + (2 * GRID_W - ncol,), NEG_INF, F32)
    band = jnp.concatenate([rpb[..., WIN_COLS - 1:], fill, rpb[..., :WIN_COLS - 1]], axis=-1)
    return jnp.pad(band, ((0, 0), (0, 1), (0, 0)))


BACK_TM = CHUNK * CH_PER_SEG
BACK_CTX_STEPS = N_CTX_TOK // BACK_TM
BACK_STEPS_PER_DEC_SEQ = DEC_SEQ // BACK_TM


def _back_kernel(xp_ref, xs_ref, y_ref, zs_ref, ac_ref, ad_ref, gs_ref, ga_ref, gate_ref, bglu_ref,
                 wglu_ref, wso_ref, wao_ref, wo_ref, op_ref, os_ref):
    i = pl.program_id(0)

    def compute(x, a2):
        ys = _gelu_tanh(y_ref[0].reshape(BACK_TM, SSM_WIDTH))
        t = jnp.dot(ys.astype(BF16), wglu_ref[...], preferred_element_type=F32) + bglu_ref[...]
        ys = ys * _sigmoid(t) * _silu(zs_ref[...].astype(F32))
        p_s = jnp.dot(ys.astype(BF16), wso_ref[...], preferred_element_type=F32)
        p_a = jnp.dot(a2, wao_ref[...], preferred_element_type=F32)
        merged = (_sigmoid(gs_ref[...].astype(F32)) * p_s + _sigmoid(ga_ref[...].astype(F32)) * p_a)
        return x + gate_ref[0] * jnp.dot(merged.astype(BF16), wo_ref[...], preferred_element_type=F32)

    @pl.when(i < BACK_CTX_STEPS)
    def _():
        op_ref[...] = compute(xp_ref[...], ac_ref[...])

    @pl.when(i >= BACK_CTX_STEPS)
    def _():
        os_ref[...] = compute(xs_ref[...], ad_ref[...])


def _back(xp, xs, y, zs, a_ctx, a_dec, gs, ga, gate3, b_glu, w_glu, w_so, w_ao, w_o):
    steps = N_TOK // BACK_TM
    n0 = BACK_CTX_STEPS
    row = functools.partial(_mod_row, ctx_steps=n0, steps_per_seq=BACK_STEPS_PER_DEC_SEQ)
    lo = lambda i: (jnp.minimum(i, n0 - 1), 0)
    hi = lambda i: (jnp.maximum(i - n0, 0), 0)
    cur = lambda i: (i, 0)
    const = lambda i: (0, 0)
    once = pl.Buffered(1)
    return pl.pallas_call(
        _back_kernel,
        grid=(steps,),
        in_specs=[pl.BlockSpec((BACK_TM, D_MODEL), lo),
                  pl.BlockSpec((BACK_TM, D_MODEL), hi),
                  pl.BlockSpec((1, CH_PER_SEG, None, CHUNK, SSM_WIDTH),
                               lambda i: (i // N_SEG, 0, i % N_SEG, 0, 0)),
                  pl.BlockSpec((BACK_TM, SSM_WIDTH), cur),
                  pl.BlockSpec((BACK_TM, ATT_WIDTH), lo),
                  pl.BlockSpec((BACK_TM, ATT_WIDTH), hi),
                  pl.BlockSpec((BACK_TM, D_MODEL), cur),
                  pl.BlockSpec((BACK_TM, D_MODEL), cur),
                  pl.BlockSpec((1, 1, D_MODEL), lambda i: (row(i), 0, 0)),
                  pl.BlockSpec((1, SSM_WIDTH), const),
                  pl.BlockSpec((SSM_WIDTH, SSM_WIDTH), const, pipeline_mode=once),
                  pl.BlockSpec((SSM_WIDTH, D_MODEL), const, pipeline_mode=once),
                  pl.BlockSpec((ATT_WIDTH, D_MODEL), const, pipeline_mode=once),
                  pl.BlockSpec((D_MODEL, D_MODEL), const, pipeline_mode=once)],
        out_specs=[pl.BlockSpec((BACK_TM, D_MODEL), lo),
                   pl.BlockSpec((BACK_TM, D_MODEL), hi)],
        out_shape=[jax.ShapeDtypeStruct((N_CTX_TOK, D_MODEL), F32),
                   jax.ShapeDtypeStruct((N_DEC_TOK, D_MODEL), F32)],
        compiler_params=_params("arbitrary"),
        name="gated_output",
    )(xp, xs, y, zs, a_ctx, a_dec, gs, ga, gate3, b_glu.reshape(1, SSM_WIDTH), w_glu, w_so, w_ao, w_o)


def _layer(xp, xs, cache_k, cache_v, st_re, st_im, c, c_ctx, norm_w, w_ada, b_ada, w_in, q_norm_w, k_norm_w,
           rpb, a_re, a_im, log_dt, b_re, b_im, c_re, c_im, d, w_glu, b_glu, w_so, w_ao, w_o):
    G, P = SSM_GROUPS, SSM_STATE

    cond8 = jnp.zeros((8, D_MODEL), F32).at[0].set(c_ctx).at[1:1 + DEC_BATCH].set(c)
    mod = _modulation(cond8, w_ada, b_ada)
    shift3 = mod[:1 + DEC_BATCH, None, :D_MODEL]
    scale3 = mod[:1 + DEC_BATCH, None, D_MODEL:2 * D_MODEL]
    gate3 = mod[:1 + DEC_BATCH, None, 2 * D_MODEL:]

    h = _norm_modulate(xp, xs, shift3, scale3, norm_w)
    u = _in_proj(h, w_in, OFF_U, SSM_WIDTH, "u")
    zs = _in_proj(h, w_in, OFF_ZS, SSM_WIDTH, "plain")
    q = _in_proj(h, w_in, OFF_Q, ATT_WIDTH, "q", q_norm_w)
    k, k_ctx32 = _in_proj(h, w_in, OFF_K, ATT_WIDTH, "k", k_norm_w)
    v, v_ctx32 = _in_proj(h, w_in, OFF_V, ATT_WIDTH, "v")
    za = _in_proj(h, w_in, OFF_ZA, ATT_WIDTH, "plain")
    gs = _in_proj(h, w_in, OFF_GS, D_MODEL, "plain")
    ga = _in_proj(h, w_in, OFF_GA, D_MODEL, "plain")

    arow = jnp.transpose(a_re, (1, 0, 2)).reshape(G, 2 * P)
    airow = jnp.transpose(a_im, (1, 0, 2)).reshape(G, 2 * P)
    dtrow = jnp.broadcast_to(jnp.transpose(log_dt, (1, 0))[:, :, None], (G, 2, P)).reshape(G, 2 * P)
    rowp = jnp.stack([arow, airow, dtrow], axis=1)
    btr = jnp.transpose(b_re, (1, 3, 0, 2)).reshape(G, SSM_GROUP, 2 * P)
    bti = jnp.transpose(b_im, (1, 3, 0, 2)).reshape(G, SSM_GROUP, 2 * P)
    ctr = jnp.transpose(c_re, (1, 2, 0, 3)).reshape(G, SSM_GROUP, 2 * P)
    cti = jnp.transpose(c_im, (1, 2, 0, 3)).reshape(G, SSM_GROUP, 2 * P)
    t_op, s_op, r_op, lam, powc = _ssm_operators(rowp, btr, bti, ctr, cti)

    dtile = d.reshape(1, SSM_WIDTH)
    h0c = jnp.zeros((G, BATCH, 2 * P), F32)
    h0d_re = jnp.transpose(st_re, (2, 0, 1, 3)).reshape(G, DEC_BATCH, 2 * P)
    h0d_im = jnp.transpose(st_im, (2, 0, 1, 3)).reshape(G, DEC_BATCH, 2 * P)
    y2d, fin = _ssm_scan(u.reshape(N_TOK, SSM_WIDTH), t_op, s_op, r_op, lam, powc, dtile,
                         h0c, h0c, h0d_re, h0d_im)
    y = y2d.reshape(2, CH_PER_SEG, N_SEG, CHUNK, SSM_WIDTH)

    a_ctx = _ctx_attention(q, k, v, za)
    nseq4 = N_TOK // DEC_SEQ
    a_dec = _nbr_attention(q.reshape(nseq4, DEC_SEQ, ATT_WIDTH), k.reshape(nseq4, DEC_SEQ, ATT_WIDTH),
                           v.reshape(nseq4, DEC_SEQ, ATT_WIDTH),
                           cache_k.reshape(DEC_BATCH, PAST_LEN, ATT_WIDTH),
                           cache_v.reshape(DEC_BATCH, PAST_LEN, ATT_WIDTH),
                           _nbr_bias_band(rpb.astype(F32)),
                           za.reshape(nseq4, DEC_SEQ, ATT_WIDTH))

    yp, ys_out = _back(xp, xs, y, zs, a_ctx, a_dec.reshape(N_DEC_TOK, ATT_WIDTH), gs, ga, gate3, b_glu,
                       w_glu.astype(BF16), w_so.astype(BF16), w_ao.astype(BF16), w_o.astype(BF16))

    fin_ctx = fin[0]
    new_re = jnp.transpose(fin_ctx[:, :, :2 * P].reshape(G, BATCH, 2, P), (1, 2, 0, 3))
    new_im = jnp.transpose(fin_ctx[:, :, 2 * P:].reshape(G, BATCH, 2, P), (1, 2, 0, 3))
    return yp, ys_out, k_ctx32, v_ctx32, new_re, new_im


def kernel(x_prompt, x_sample, cache_k, cache_v, state_ssm_re, state_ssm_im, c, c_ctx, norm_w, w_ada, b_ada,
           w_in, q_norm_w, k_norm_w, rel_pos_bias, ssm_a_re, ssm_a_im, ssm_log_dt, ssm_b_re, ssm_b_im,
           ssm_c_re, ssm_c_im, ssm_d, w_glu, b_glu, w_ssm_out, w_att_out, w_o):
    depth = norm_w.shape[0]
    xp = x_prompt.reshape(N_CTX_TOK, D_MODEL)
    xs = x_sample.reshape(N_DEC_TOK, D_MODEL)
    new_k, new_v, new_re, new_im = [], [], [], []
    for l in range(depth):
        xp, xs, kl, vl, rl, il = _layer(
            xp, xs, cache_k[:, l], cache_v[:, l], state_ssm_re[:, l], state_ssm_im[:, l], c, c_ctx,
            norm_w[l], w_ada[l], b_ada[l], w_in[l], q_norm_w[l], k_norm_w[l], rel_pos_bias[l],
            ssm_a_re[l], ssm_a_im[l], ssm_log_dt[l], ssm_b_re[l], ssm_b_im[l], ssm_c_re[l], ssm_c_im[l],
            ssm_d[l], w_glu[l], b_glu[l], w_ssm_out[l], w_att_out[l], w_o[l])
        new_k.append(kl.reshape(BATCH, SEQ, N_HEADS, HEAD_DIM))
        new_v.append(vl.reshape(BATCH, SEQ, N_HEADS, HEAD_DIM))
        new_re.append(rl)
        new_im.append(il)
    return (xp.reshape(BATCH, SEQ, D_MODEL), xs.reshape(DEC_BATCH, DEC_SEQ, D_MODEL),
            jnp.stack(new_k, axis=1), jnp.stack(new_v, axis=1),
            jnp.stack(new_re, axis=1), jnp.stack(new_im, axis=1))
```

```python
import functools
import math

import jax
import jax.numpy as jnp
from jax import lax
from jax.experimental import pallas as pl
from jax.experimental.pallas import tpu as pltpu

D_MODEL = 2048
BATCH = 16
SEQ = 256
DEC_BATCH = 2
DEC_SEQ = 2048
PAST_LEN = 512
GRID_W = 64
GRID_H = DEC_SEQ // GRID_W
SSM_WIDTH = D_MODEL // 2
SSM_GROUP = 16
SSM_GROUPS = SSM_WIDTH // SSM_GROUP
SSM_STATE = 64
N_HEADS = 16
HEAD_DIM = 64
ATT_WIDTH = N_HEADS * HEAD_DIM
WIN_ROWS = 8
WIN_COLS = 16
EPS = 1e-6
NEG_INF = -1e30

N_CTX_TOK = BATCH * SEQ
N_DEC_TOK = DEC_BATCH * DEC_SEQ
N_TOK = N_CTX_TOK + N_DEC_TOK

OFF_U = 0
OFF_ZS = SSM_WIDTH
OFF_Q = 2 * SSM_WIDTH
OFF_K = OFF_Q + ATT_WIDTH
OFF_V = OFF_K + ATT_WIDTH
OFF_ZA = OFF_V + ATT_WIDTH
OFF_GS = OFF_ZA + ATT_WIDTH
OFF_GA = OFF_GS + D_MODEL

CHUNK = 16
FLAT = CHUNK * SSM_GROUP
N_SEG = 16
CH_PER_SEG = 16
DSTATE = 2 * SSM_STATE

Q_ROWS_PER_BLOCK = 4
NBR_Q = Q_ROWS_PER_BLOCK * GRID_W
NBR_KROWS = 12
NBR_K = NBR_KROWS * GRID_W

VMEM_LIMIT = 56 * 1024 * 1024

F32 = jnp.float32
BF16 = jnp.bfloat16
HIGHEST = lax.Precision.HIGHEST


def _sigmoid(x):
    return 1.0 / (1.0 + jnp.exp(-x))


def _silu(x):
    return x * _sigmoid(x)


def _gelu_tanh(x):
    return 0.5 * x * (1.0 + jnp.tanh(math.sqrt(2.0 / math.pi) * (x + 0.044715 * (x * x * x))))


def _cmul(ar, ai, br, bi):
    return ar * br - ai * bi, ar * bi + ai * br


def _dot_nt(a, b):
    return lax.dot_general(a, b, (((1,), (1,)), ((), ())), preferred_element_type=F32)


def _params(*sem):
    return pltpu.CompilerParams(dimension_semantics=sem, vmem_limit_bytes=VMEM_LIMIT)


def _mod_kernel(cond_ref, w_ref, b_ref, o_ref):
    c = cond_ref[...]
    s = _silu(c).astype(BF16)
    o_ref[...] = jnp.dot(s, w_ref[...].astype(BF16), preferred_element_type=F32) + b_ref[...]


def _modulation(cond8, w_ada, b_ada):
    tn = 768
    n = w_ada.shape[1]
    return pl.pallas_call(
        _mod_kernel,
        grid=(n // tn,),
        in_specs=[pl.BlockSpec((8, D_MODEL), lambda j: (0, 0)),
                  pl.BlockSpec((D_MODEL, tn), lambda j: (0, j)),
                  pl.BlockSpec((1, tn), lambda j: (0, j))],
        out_specs=pl.BlockSpec((8, tn), lambda j: (0, j)),
        out_shape=jax.ShapeDtypeStruct((8, n), F32),
        compiler_params=_params("arbitrary"),
        name="modulation",
    )(cond8, w_ada, b_ada.reshape(1, n))


NORM_TM = 512
NORM_CTX_STEPS = N_CTX_TOK // NORM_TM
NORM_STEPS_PER_DEC_SEQ = DEC_SEQ // NORM_TM


def _mod_row(i, ctx_steps, steps_per_seq):
    return jnp.where(i < ctx_steps, 0, 1 + (i - ctx_steps) // steps_per_seq)


def _norm_kernel(xp_ref, xs_ref, shift_ref, scale_ref, nw_ref, o_ref):
    i = pl.program_id(0)

    def body(x):
        ms = jnp.mean(x * x, axis=-1, keepdims=True)
        y = x * lax.rsqrt(ms + EPS) * nw_ref[...]
        o_ref[...] = (y * (1.0 + scale_ref[0]) + shift_ref[0]).astype(BF16)

    @pl.when(i < NORM_CTX_STEPS)
    def _():
        body(xp_ref[...])

    @pl.when(i >= NORM_CTX_STEPS)
    def _():
        body(xs_ref[...])


def _norm_modulate(xp, xs, shift3, scale3, norm_w):
    steps = N_TOK // NORM_TM
    row = functools.partial(_mod_row, ctx_steps=NORM_CTX_STEPS, steps_per_seq=NORM_STEPS_PER_DEC_SEQ)
    return pl.pallas_call(
        _norm_kernel,
        grid=(steps,),
        in_specs=[pl.BlockSpec((NORM_TM, D_MODEL), lambda i: (jnp.minimum(i, NORM_CTX_STEPS - 1), 0)),
                  pl.BlockSpec((NORM_TM, D_MODEL), lambda i: (jnp.maximum(i - NORM_CTX_STEPS, 0), 0)),
                  pl.BlockSpec((1, 1, D_MODEL), lambda i: (row(i), 0, 0)),
                  pl.BlockSpec((1, 1, D_MODEL), lambda i: (row(i), 0, 0)),
                  pl.BlockSpec((1, D_MODEL), lambda i: (0, 0))],
        out_specs=pl.BlockSpec((NORM_TM, D_MODEL), lambda i: (i, 0)),
        out_shape=jax.ShapeDtypeStruct((N_TOK, D_MODEL), BF16),
        compiler_params=_params("arbitrary"),
        name="norm_modulate",
    )(xp, xs, shift3, scale3, norm_w.reshape(1, D_MODEL))


PROJ_TM = 1024
PROJ_TN = 1024
PROJ_CTX_STEPS = N_CTX_TOK // PROJ_TM
PROJ_SEGS = PROJ_TM // (CHUNK * CH_PER_SEG)


def _head_group_ones():
    r = lax.broadcasted_iota(jnp.int32, (256, 256), 0) // HEAD_DIM
    c = lax.broadcasted_iota(jnp.int32, (256, 256), 1) // HEAD_DIM
    return jnp.where(r == c, 1.0, 0.0).astype(BF16)


def _head_rms(acc, nw):
    ones = _head_group_ones()
    outs = []
    for c in range(acc.shape[1] // 256):
        a = acc[:, c * 256:(c + 1) * 256]
        ssum = jnp.dot((a * a).astype(BF16), ones, preferred_element_type=F32)
        outs.append(a * lax.rsqrt(ssum * (1.0 / HEAD_DIM) + EPS))
    return jnp.concatenate(outs, axis=1) * nw


def _proj_kernel(*refs, mode):
    if mode in ("plain", "u"):
        h_ref, w_ref, o_ref, wbf = refs
    elif mode == "q":
        h_ref, w_ref, nw_ref, o_ref, wbf = refs
    elif mode == "k":
        h_ref, w_ref, nw_ref, o_ref, of_ref, wbf = refs
    else:
        h_ref, w_ref, o_ref, of_ref, wbf = refs
    i = pl.program_id(1)

    @pl.when(i == 0)
    def _():
        wbf[...] = w_ref[...].astype(BF16)

    acc = jnp.dot(h_ref[...], wbf[...], preferred_element_type=F32)
    if mode == "q":
        acc = _head_rms(acc, nw_ref[...]) * (HEAD_DIM ** -0.5)
    elif mode == "k":
        acc = _head_rms(acc, nw_ref[...])
    if mode == "u":
        for seg in range(PROJ_SEGS):
            for ch in range(CH_PER_SEG):
                r0 = (seg * CH_PER_SEG + ch) * CHUNK
                o_ref[0, ch, seg] = acc[r0:r0 + CHUNK, :]
    else:
        o_ref[...] = acc.astype(BF16)
    if mode in ("k", "v"):
        @pl.when(i < PROJ_CTX_STEPS)
        def _():
            of_ref[...] = acc


def _in_proj(h, w_in, col_off, ncols, mode, norm_w=None):
    nj = ncols // PROJ_TN
    ni = N_TOK // PROJ_TM
    joff = col_off // PROJ_TN
    in_specs = [pl.BlockSpec((PROJ_TM, D_MODEL), lambda j, i: (i, 0)),
                pl.BlockSpec((D_MODEL, PROJ_TN), lambda j, i: (0, joff + j))]
    args = [h, w_in]
    if mode in ("q", "k"):
        in_specs.append(pl.BlockSpec((1, PROJ_TN), lambda j, i: (0, 0)))
        args.append(jnp.tile(norm_w.reshape(1, HEAD_DIM), (1, PROJ_TN // HEAD_DIM)))
    out_specs = [pl.BlockSpec((PROJ_TM, PROJ_TN), lambda j, i: (i, j))]
    out_shape = [jax.ShapeDtypeStruct((N_TOK, ncols), BF16)]
    if mode == "u":
        tiles_per_path = N_SEG // PROJ_SEGS
        out_specs = [pl.BlockSpec((1, CH_PER_SEG, PROJ_SEGS, CHUNK, PROJ_TN),
                                  lambda j, i: (i // tiles_per_path, 0, i % tiles_per_path, 0, j))]
        out_shape = [jax.ShapeDtypeStruct((2, CH_PER_SEG, N_SEG, CHUNK, ncols), F32)]
    if mode in ("k", "v"):
        out_specs.append(pl.BlockSpec((PROJ_TM, PROJ_TN),
                                      lambda j, i: (jnp.minimum(i, PROJ_CTX_STEPS - 1), j)))
        out_shape.append(jax.ShapeDtypeStruct((N_CTX_TOK, ncols), F32))
    res = pl.pallas_call(
        functools.partial(_proj_kernel, mode=mode),
        grid=(nj, ni),
        in_specs=in_specs,
        out_specs=out_specs,
        out_shape=out_shape,
        scratch_shapes=[pltpu.VMEM((D_MODEL, PROJ_TN), BF16)],
        compiler_params=_params("arbitrary", "arbitrary"),
        name="in_proj_" + mode,
    )(*args)
    return res if mode in ("k", "v") else res[0]


SSM_GB = 4


def _pow_select(e, pows):
    rr = jnp.where((e & 1) != 0, pows[0][0], 1.0)
    ri = jnp.where((e & 1) != 0, pows[0][1], 0.0)
    for b in range(1, len(pows)):
        bit = (e & (1 << b)) != 0
        fr = jnp.where(bit, pows[b][0], 1.0)
        fi = jnp.where(bit, pows[b][1], 0.0)
        rr, ri = _cmul(rr, ri, fr, fi)
    return rr, ri


def _discretize(a_re, a_im, log_dt):
    lr = jnp.minimum(a_re, -1e-4)
    li = a_im
    dt = jnp.exp(log_dt)
    mag = jnp.exp(lr * dt)
    br = mag * jnp.cos(li * dt)
    bi = mag * jnp.sin(li * dt)
    den = lr * lr + li * li
    nr = br - 1.0
    cr = (nr * lr + bi * li) / den
    ci = (bi * lr - nr * li) / den
    return (br, bi), (cr, ci)


def _squarings(pr, pi, n):
    out = [(pr, pi)]
    for _ in range(n):
        pr, pi = _cmul(pr, pi, pr, pi)
        out.append((pr, pi))
    return out


def _dot_nt_exact(a, b):
    return lax.dot_general(a, b, (((1,), (1,)), ((), ())), preferred_element_type=F32, precision=HIGHEST)


def _ssm_ops_kernel(rowp_ref, btr_ref, bti_ref, ctr_ref, cti_ref, t_ref, s_ref, rt_ref, lam_ref, powc_ref):
    lane_b = lax.broadcasted_iota(jnp.int32, (SSM_GROUP, DSTATE), 1)
    lane_t = lax.broadcasted_iota(jnp.int32, (SSM_GROUP, FLAT), 1)
    row_c = lax.broadcasted_iota(jnp.int32, (CH_PER_SEG, DSTATE), 0)
    lane_c = lax.broadcasted_iota(jnp.int32, (CH_PER_SEG, DSTATE), 1)
    exp_c = jnp.where(lane_c < SSM_STATE, row_c, (CH_PER_SEG - 1) - row_c)

    for g in range(SSM_GB):
        rp = rowp_ref[g]
        (lbr, lbi), (cfr, cfi) = _discretize(rp[0:1], rp[1:2], rp[2:3])
        sq = _squarings(lbr, lbi, 8)
        bbr, bbi = _cmul(btr_ref[g], bti_ref[g], cfr, cfi)

        ptr, pti = _pow_select(exp_c, sq[:4])

        s_re, s_im = [], []
        for sp in range(CHUNK):
            e = CHUNK - 1 - sp
            br, bi = _cmul(bbr, bbi, ptr[e:e + 1], pti[e:e + 1])
            s_re.append(br)
            s_im.append(bi)
        s_ref[g] = jnp.concatenate([jnp.concatenate(s_re, axis=0), jnp.concatenate(s_im, axis=0)],
                                   axis=1).astype(BF16)

        xcr = jnp.concatenate([ctr_ref[g]] * CHUNK, axis=0)
        xci = jnp.concatenate([cti_ref[g]] * CHUNK, axis=0)
        xpr = jnp.broadcast_to(ptr[:, None, :], (CHUNK, SSM_GROUP, DSTATE)).reshape(FLAT, DSTATE)
        xpi = jnp.broadcast_to(pti[:, None, :], (CHUNK, SSM_GROUP, DSTATE)).reshape(FLAT, DSTATE)
        ykr, yki = _cmul(xcr, xci, xpr, xpi)
        yrr, yri = _cmul(ykr, yki, lbr, lbi)
        rt_ref[g] = jnp.concatenate([yrr, -yri], axis=1).astype(BF16)

        fwd = lane_b < SSM_STATE
        lhs = jnp.concatenate([jnp.where(fwd, bbr, 0.0), jnp.where(fwd, bbi, 0.0),
                               jnp.where(fwd, 0.0, bbr), jnp.where(fwd, 0.0, bbi)], axis=0)
        p1 = _dot_nt_exact(lhs, ykr)
        p2 = _dot_nt_exact(lhs, yki)
        kf = p1[0:16] - p2[16:32]
        kb = p1[32:48] - p2[48:64]
        for sp in range(CHUNK):
            tf = kf if sp == 0 else pltpu.roll(kf, SSM_GROUP * sp, axis=1)
            tf = jnp.where(lane_t >= SSM_GROUP * sp, tf, 0.0)
            shift = (FLAT - SSM_GROUP * (CHUNK - 1 - sp)) % FLAT
            tb = kb if shift == 0 else pltpu.roll(kb, shift, axis=1)
            tb = jnp.where(lane_t < SSM_GROUP * (sp + 1), tb, 0.0)
            t_ref[g, sp * SSM_GROUP:(sp + 1) * SSM_GROUP, :] = (tf + tb).astype(BF16)

        l16 = sq[4]
        l256 = sq[8]
        lam_ref[g] = jnp.concatenate([l16[0], l16[1], l256[0], l256[1],
                                      jnp.zeros((4, DSTATE), F32)], axis=0)
        q16 = _squarings(l16[0], l16[1], 3)
        pcr, pci = _pow_select(exp_c, q16)
        powc_ref[g, 0] = pcr
        powc_ref[g, 1] = pci


def _ssm_operators(rowp, btr, bti, ctr, cti):
    G = SSM_GROUPS
    gb = SSM_GB
    return pl.pallas_call(
        _ssm_ops_kernel,
        grid=(G // gb,),
        in_specs=[pl.BlockSpec((gb, 3, DSTATE), lambda g: (g, 0, 0)),
                  pl.BlockSpec((gb, SSM_GROUP, DSTATE), lambda g: (g, 0, 0)),
                  pl.BlockSpec((gb, SSM_GROUP, DSTATE), lambda g: (g, 0, 0)),
                  pl.BlockSpec((gb, SSM_GROUP, DSTATE), lambda g: (g, 0, 0)),
                  pl.BlockSpec((gb, SSM_GROUP, DSTATE), lambda g: (g, 0, 0))],
        out_specs=[pl.BlockSpec((gb, FLAT, FLAT), lambda g: (g, 0, 0)),
                   pl.BlockSpec((gb, FLAT, FLAT), lambda g: (g, 0, 0)),
                   pl.BlockSpec((gb, FLAT, FLAT), lambda g: (g, 0, 0)),
                   pl.BlockSpec((gb, 8, DSTATE), lambda g: (g, 0, 0)),
                   pl.BlockSpec((gb, 2, CH_PER_SEG, DSTATE), lambda g: (g, 0, 0, 0))],
        out_shape=[jax.ShapeDtypeStruct((G, FLAT, FLAT), BF16),
                   jax.ShapeDtypeStruct((G, FLAT, FLAT), BF16),
                   jax.ShapeDtypeStruct((G, FLAT, FLAT), BF16),
                   jax.ShapeDtypeStruct((G, 8, DSTATE), F32),
                   jax.ShapeDtypeStruct((G, 2, CH_PER_SEG, DSTATE), F32)],
        compiler_params=_params("arbitrary"),
        name="ssm_operators",
    )(rowp, btr, bti, ctr, cti)


SSM_PATHS = ((BATCH, SEQ // (CHUNK * CH_PER_SEG)), (DEC_BATCH, DEC_SEQ // (CHUNK * CH_PER_SEG)))


def _ssm_path(u, s_op, t_op, rt_op, lam, powc_ref, h0r, h0i, nseg):
    lane = lax.broadcasted_iota(jnp.int32, (N_SEG, DSTATE), 1)
    isf = lane < SSM_STATE
    l16r, l16i, l256r, l256i = lam[0:1], lam[1:2], lam[2:3], lam[3:4]

    z = jnp.dot(u, s_op, preferred_element_type=F32)
    zre, zim = z[:, :DSTATE], z[:, DSTATE:]

    hr = jnp.zeros((N_SEG, DSTATE), F32)
    hi = jnp.zeros((N_SEG, DSTATE), F32)
    hist = []
    for k in range(CH_PER_SEG):
        hist.append((hr, hi))
        kb = CH_PER_SEG - 1 - k
        zr = jnp.where(isf, zre[k * N_SEG:(k + 1) * N_SEG], zre[kb * N_SEG:(kb + 1) * N_SEG])
        zi = jnp.where(isf, zim[k * N_SEG:(k + 1) * N_SEG], zim[kb * N_SEG:(kb + 1) * N_SEG])
        nr, ni = _cmul(l16r, l16i, hr, hi)
        hr, hi = nr + zr, ni + zi

    if nseg > 1:
        h0r = jnp.broadcast_to(h0r[:, None, :], (N_SEG // nseg, nseg, DSTATE)).reshape(N_SEG, DSTATE)
        h0i = jnp.broadcast_to(h0i[:, None, :], (N_SEG // nseg, nseg, DSTATE)).reshape(N_SEG, DSTATE)
    seg = lax.broadcasted_iota(jnp.int32, (N_SEG, DSTATE), 0) & (nseg - 1)
    segpow = _squarings(l256r, l256i, max(int(math.log2(nseg)), 0))

    def shifted(x, d):
        dn = jnp.where(seg >= d, pltpu.roll(x, d, axis=0), 0.0)
        up = jnp.where(seg <= nseg - 1 - d, pltpu.roll(x, N_SEG - d, axis=0), 0.0)
        return jnp.where(isf, dn, up)

    pr, pi = hr, hi
    d = 1
    lvl = 0
    while d < nseg:
        ar, ai = _cmul(segpow[lvl][0], segpow[lvl][1], shifted(pr, d), shifted(pi, d))
        pr, pi = pr + ar, pi + ai
        d *= 2
        lvl += 1
    if nseg > 1:
        e_in = jnp.where(isf, seg, nseg - 1 - seg)
        wr, wi = _pow_select(e_in, segpow[:lvl])
        ar, ai = _cmul(wr, wi, h0r, h0i)
        hsr, hsi = shifted(pr, 1) + ar, shifted(pi, 1) + ai
    else:
        hsr, hsi = h0r, h0i
    er, ei = _cmul(segpow[lvl][0], segpow[lvl][1], h0r, h0i)
    fin = jnp.concatenate([pr + er, pi + ei], axis=1)

    rows = []
    for c in range(CH_PER_SEG):
        cb = CH_PER_SEG - 1 - c
        lr = jnp.where(isf, hist[c][0], hist[cb][0])
        li = jnp.where(isf, hist[c][1], hist[cb][1])
        ar, ai = _cmul(powc_ref[0, c:c + 1, :], powc_ref[1, c:c + 1, :], hsr, hsi)
        rows.append(jnp.concatenate([lr + ar, li + ai], axis=1))
    hent = jnp.concatenate(rows, axis=0).astype(BF16)

    y = jnp.dot(u, t_op, preferred_element_type=F32) + _dot_nt(hent, rt_op)
    return y, fin


GROUPS_PER_STEP = 128 // SSM_GROUP
GROUPS_PER_ITER = 8


def _block_transpose(arrs, blk):
    n = len(arrs)
    width = arrs[0].shape[1]
    j = lax.broadcasted_iota(jnp.int32, arrs[0].shape, 1) // blk
    k = n // 2
    while k >= 1:
        bit = (j & k) != 0
        new = list(arrs)
        for x in range(n):
            if x & k == 0:
                a, b = arrs[x], arrs[x | k]
                new[x] = jnp.where(bit, pltpu.roll(b, k * blk, axis=1), a)
                new[x | k] = jnp.where(bit, b, pltpu.roll(a, width - k * blk, axis=1))
        arrs = new
        k //= 2
    return arrs


def _ssm_kernel(u_ref, t_ref, s_ref, r_ref, lam_ref, powc_ref, d_ref, h0cr_ref, h0ci_ref, h0dr_ref, h0di_ref,
                y_ref, fin_ref, ubuf, ybuf):
    h0 = ((h0cr_ref, h0ci_ref), (h0dr_ref, h0di_ref))
    nhalf = FLAT // 128
    rows = N_SEG * CH_PER_SEG
    def token_rows(path, tok, r0, nrows):
        return pl.ds(path * rows * CHUNK + r0 * CHUNK + tok, nrows, stride=CHUNK)

    in_rows = rows
    out_rows = rows
    for path, (_, nseg) in enumerate(SSM_PATHS):
        for half in range(nhalf):
            for r0 in range(0, rows, in_rows):
                toks = [pltpu.bitcast(
                    u_ref[token_rows(path, half * GROUPS_PER_STEP + sb, r0, in_rows), :].astype(BF16), jnp.uint32)
                    for sb in range(GROUPS_PER_STEP)]
                grouped = _block_transpose(toks, SSM_GROUP)
                for g in range(GROUPS_PER_STEP):
                    ubuf[g, r0:r0 + in_rows, half * 128:(half + 1) * 128] = pltpu.bitcast(grouped[g], BF16)

        def body(gi, carry, path=path, nseg=nseg):
            for dg in range(GROUPS_PER_ITER):
                g = gi * GROUPS_PER_ITER + dg
                y, fin = _ssm_path(ubuf[g], s_ref[g], t_ref[g], r_ref[g], lam_ref[g], powc_ref.at[g],
                                   h0[path][0][g], h0[path][1][g], nseg)
                ybuf[g] = y
                fin_ref[path, g] = fin
            return carry

        lax.fori_loop(0, GROUPS_PER_STEP // GROUPS_PER_ITER, body, 0)

        for half in range(nhalf):
            for r0 in range(0, rows, out_rows):
                grouped = [ybuf[g, r0:r0 + out_rows, half * 128:(half + 1) * 128] for g in range(GROUPS_PER_STEP)]
                toks = _block_transpose(grouped, SSM_GROUP)
                for sb in range(GROUPS_PER_STEP):
                    sel = token_rows(path, half * GROUPS_PER_STEP + sb, r0, out_rows)
                    y_ref[sel, :] = toks[sb] + u_ref[sel, :] * d_ref[...]


def _ssm_scan(u2d, t_op, s_op, r_op, lam, powc, dtile, h0c_re, h0c_im, h0d_re, h0d_im):
    G = SSM_GROUPS
    gs = GROUPS_PER_STEP
    rows = N_SEG * CH_PER_SEG
    return pl.pallas_call(
        _ssm_kernel,
        grid=(G // gs,),
        in_specs=[
            pl.BlockSpec((N_TOK, 128), lambda o: (0, o)),
            pl.BlockSpec((gs, FLAT, FLAT), lambda o: (o, 0, 0)),
            pl.BlockSpec((gs, FLAT, FLAT), lambda o: (o, 0, 0)),
            pl.BlockSpec((gs, FLAT, FLAT), lambda o: (o, 0, 0)),
            pl.BlockSpec((gs, 8, DSTATE), lambda o: (o, 0, 0)),
            pl.BlockSpec((gs, 2, CH_PER_SEG, DSTATE), lambda o: (o, 0, 0, 0)),
            pl.BlockSpec((1, 128), lambda o: (0, o)),
            pl.BlockSpec((gs, BATCH, DSTATE), lambda o: (o, 0, 0)),
            pl.BlockSpec((gs, BATCH, DSTATE), lambda o: (o, 0, 0)),
            pl.BlockSpec((gs, DEC_BATCH, DSTATE), lambda o: (o, 0, 0)),
            pl.BlockSpec((gs, DEC_BATCH, DSTATE), lambda o: (o, 0, 0))],
        out_specs=[pl.BlockSpec((N_TOK, 128), lambda o: (0, o)),
                   pl.BlockSpec((2, gs, N_SEG, 2 * DSTATE), lambda o: (0, o, 0, 0))],
        out_shape=[jax.ShapeDtypeStruct((N_TOK, SSM_WIDTH), F32),
                   jax.ShapeDtypeStruct((2, G, N_SEG, 2 * DSTATE), F32)],
        scratch_shapes=[pltpu.VMEM((gs, rows, FLAT), BF16), pltpu.VMEM((gs, rows, FLAT), F32)],
        compiler_params=_params("arbitrary"),
        name="ssm_scan",
    )(u2d, t_op, s_op, r_op, lam, powc, dtile, h0c_re, h0c_im, h0d_re, h0d_im)


CTX_BB = 4
HEADS_PER_STEP = 256 // HEAD_DIM
HEAD_LANES = HEADS_PER_STEP * HEAD_DIM


def _softmax_pv(scores, values):
    m = scores[0].max(axis=-1, keepdims=True)
    for s in scores[1:]:
        m = jnp.maximum(m, s.max(axis=-1, keepdims=True))
    l = None
    o = None
    for s, v in zip(scores, values):
        p = jnp.exp(s - m)
        ls = p.sum(axis=-1, keepdims=True)
        os_ = jnp.dot(p.astype(BF16), v, preferred_element_type=F32)
        l = ls if l is None else l + ls
        o = os_ if o is None else o + os_
    return o / l


def _head_of_lane():
    return lax.broadcasted_iota(jnp.int32, (1, HEAD_LANES), 1) // HEAD_DIM


def _merge_heads(outs, za):
    head = _head_of_lane()
    o = outs[-1]
    for hh in range(HEADS_PER_STEP - 2, -1, -1):
        o = jnp.where(head == hh, outs[hh], o)
    return (o * _silu(za.astype(F32))).astype(BF16)


def _ctx_attn_kernel(q_ref, k_ref, v_ref, za_ref, o_ref):
    head = _head_of_lane()
    for b in range(CTX_BB):
        sl = slice(b * SEQ, (b + 1) * SEQ)
        q, k, v = q_ref[sl, :], k_ref[sl, :], v_ref[sl, :]
        outs = []
        for hh in range(HEADS_PER_STEP):
            qh = jnp.where(head == hh, q, jnp.zeros_like(q))
            outs.append(_softmax_pv([_dot_nt(qh, k)], [v]))
        o_ref[sl, :] = _merge_heads(outs, za_ref[sl, :])


def _ctx_attention(q, k, v, za):
    rows = CTX_BB * SEQ
    spec = pl.BlockSpec((rows, HEAD_LANES), lambda b, hq: (b, hq))
    return pl.pallas_call(
        _ctx_attn_kernel,
        grid=(BATCH // CTX_BB, N_HEADS // HEADS_PER_STEP),
        in_specs=[spec, spec, spec, spec],
        out_specs=spec,
        out_shape=jax.ShapeDtypeStruct((N_CTX_TOK, ATT_WIDTH), BF16),
        compiler_params=_params("arbitrary", "arbitrary"),
        name="ctx_attention",
    )(q, k, v, za)


N_DROW = 2 * WIN_ROWS - 1


def _nbr_build_table(band_ref, tbl):
    lane = lax.broadcasted_iota(jnp.int32, (GRID_W, 2 * GRID_W), 1)
    qc = lax.broadcasted_iota(jnp.int32, (GRID_W, 2 * GRID_W), 0)
    kc = lane & (GRID_W - 1)
    cs = jnp.clip(qc - WIN_COLS // 2, 0, GRID_W - WIN_COLS)
    valid = jnp.logical_and(kc >= cs, kc < cs + WIN_COLS)
    for hh in range(HEADS_PER_STEP):
        for dr in range(N_DROW):
            x = jnp.broadcast_to(band_ref[hh, dr:dr + 1, :], (GRID_W, 2 * GRID_W))
            lo = pltpu.roll(x, 0, axis=1, stride=1, stride_axis=0)
            hi = pltpu.roll(x, GRID_W, axis=1, stride=1, stride_axis=0)
            tbl[hh, dr] = jnp.where(valid, jnp.where(lane < GRID_W, lo, hi), NEG_INF)
        tbl[hh, N_DROW] = jnp.full((GRID_W, 2 * GRID_W), NEG_INF, F32)


NBR_KEYS = NBR_K + PAST_LEN
KEY_TILE = 256
SOFTMAX_ROWS = 64


def _nbr_block_index(a):
    ks = jnp.clip(Q_ROWS_PER_BLOCK * a - Q_ROWS_PER_BLOCK, 0, GRID_H - NBR_KROWS)
    idx = []
    for ri in range(Q_ROWS_PER_BLOCK):
        r = Q_ROWS_PER_BLOCK * a + ri
        rs = jnp.clip(r - WIN_ROWS // 2, 0, GRID_H - WIN_ROWS)
        row = []
        for kri in range(NBR_KROWS):
            kr = ks + kri
            valid = jnp.logical_and(kr >= rs, kr < rs + WIN_ROWS)
            row.append(jnp.where(valid, kr - r + WIN_ROWS - 1, N_DROW))
        idx.append(row)
    return idx


def _nbr_bias_tile(tbl, hh, idx, t):
    half0 = lax.broadcasted_iota(jnp.int32, (1, 2 * GRID_W), 1) < GRID_W
    krows = KEY_TILE // GRID_W
    rows = []
    for ri in range(Q_ROWS_PER_BLOCK):
        tiles = [jnp.where(half0, tbl[hh, idx[ri][kri]], tbl[hh, idx[ri][kri + 1]])
                 for kri in range(krows * t, krows * (t + 1), 2)]
        rows.append(jnp.concatenate(tiles, axis=1))
    return jnp.concatenate(rows, axis=0)


def _nbr_attn_kernel(q_ref, k_ref, v_ref, kc_ref, vc_ref, band_ref, za_ref, o_ref,
                     tbl, kcat, vcat, s_scr, p_scr, o_scr):
    a = pl.program_id(1)

    @pl.when(a == 0)
    def _():
        _nbr_build_table(band_ref, tbl)

    start = pl.multiple_of(jnp.clip(a * NBR_Q - NBR_Q, 0, DEC_SEQ - NBR_K), NBR_Q)
    head = _head_of_lane()
    idx = _nbr_block_index(a)
    for b in range(DEC_BATCH):
        kcat[b, 0:NBR_K, :] = k_ref[b, pl.ds(start, NBR_K), :]
        kcat[b, NBR_K:NBR_KEYS, :] = kc_ref[b].astype(BF16)
        vcat[b, 0:NBR_K, :] = v_ref[b, pl.ds(start, NBR_K), :]
        vcat[b, NBR_K:NBR_KEYS, :] = vc_ref[b].astype(BF16)

    def scores(b, hh):
        q = q_ref[b]
        qh = jnp.where(head == hh, q, jnp.zeros_like(q))
        mrun = None
        for t in range(NBR_KEYS // KEY_TILE):
            st = _dot_nt(qh, kcat[b, t * KEY_TILE:(t + 1) * KEY_TILE, :])
            if t < NBR_K // KEY_TILE:
                st = st + _nbr_bias_tile(tbl, hh, idx, t)
            s_scr[hh, :, t * KEY_TILE:(t + 1) * KEY_TILE] = st
            mt = jnp.maximum(st[:, :128], st[:, 128:])
            mrun = mt if mrun is None else jnp.maximum(mrun, mt)
        return mrun.max(axis=-1, keepdims=True)

    def attend(b, hh, m):
        ls = []
        for r0 in range(0, NBR_Q, SOFTMAX_ROWS):
            mr = m[r0:r0 + SOFTMAX_ROWS]
            lrun = jnp.zeros((SOFTMAX_ROWS, 128), F32)
            for c0 in range(0, NBR_KEYS, 128):
                p = jnp.exp(s_scr[hh, r0:r0 + SOFTMAX_ROWS, c0:c0 + 128] - mr)
                lrun = lrun + p
                p_scr[hh, r0:r0 + SOFTMAX_ROWS, c0:c0 + 128] = p.astype(BF16)
            ls.append(lrun.sum(axis=-1, keepdims=True))
        l = jnp.concatenate(ls, axis=0)
        o = jnp.dot(p_scr[hh], vcat[b], preferred_element_type=F32)
        lanes = slice(hh * HEAD_DIM, (hh + 1) * HEAD_DIM)
        o_scr[b, :, lanes] = o[:, lanes] / l

    order = [(b, hh) for b in range(DEC_BATCH) for hh in range(HEADS_PER_STEP)]
    m_cur = scores(*order[0])
    for n, (b, hh) in enumerate(order):
        m_next = scores(*order[n + 1]) if n + 1 < len(order) else None
        attend(b, hh, m_cur)
        m_cur = m_next
    for b in range(DEC_BATCH):
        o_ref[b] = (o_scr[b] * _silu(za_ref[b].astype(F32))).astype(BF16)


def _nbr_attention(q4, k4, v4, kc, vc, band, za4):
    first = N_CTX_TOK // DEC_SEQ // DEC_BATCH
    nblk = GRID_H // Q_ROWS_PER_BLOCK
    hw = HEAD_LANES
    qspec = pl.BlockSpec((DEC_BATCH, NBR_Q, hw), lambda hp, a: (first, a, hp))
    kspec = pl.BlockSpec((DEC_BATCH, DEC_SEQ, hw), lambda hp, a: (first, 0, hp))
    cspec = pl.BlockSpec((DEC_BATCH, PAST_LEN, hw), lambda hp, a: (0, 0, hp))
    bspec = pl.BlockSpec((HEADS_PER_STEP, N_DROW + 1, 2 * GRID_W), lambda hp, a: (hp, 0, 0))
    return pl.pallas_call(
        _nbr_attn_kernel,
        grid=(N_HEADS // HEADS_PER_STEP, nblk),
        in_specs=[qspec, kspec, kspec, cspec, cspec, bspec, qspec],
        out_specs=pl.BlockSpec((DEC_BATCH, NBR_Q, hw), lambda hp, a: (0, a, hp)),
        out_shape=jax.ShapeDtypeStruct((DEC_BATCH, DEC_SEQ, ATT_WIDTH), BF16),
        scratch_shapes=[pltpu.VMEM((HEADS_PER_STEP, N_DROW + 1, GRID_W, 2 * GRID_W), F32),
                        pltpu.VMEM((DEC_BATCH, NBR_KEYS, HEAD_LANES), BF16),
                        pltpu.VMEM((DEC_BATCH, NBR_KEYS, HEAD_LANES), BF16),
                        pltpu.VMEM((HEADS_PER_STEP, NBR_Q, NBR_KEYS), F32),
                        pltpu.VMEM((HEADS_PER_STEP, NBR_Q, NBR_KEYS), BF16),
                        pltpu.VMEM((DEC_BATCH, NBR_Q, HEAD_LANES), F32)],
        compiler_params=_params("arbitrary", "arbitrary"),
        name="nbr_attention",
    )(q4, k4, v4, kc, vc, band, za4)


def _nbr_bias_band(rpb):
    ncol = 2 * WIN_COLS - 1
    fill = jnp.full(rpb.shape[:2] + (2 * GRID_W - ncol,), NEG_INF, F32)
    band = jnp.concatenate([rpb[..., WIN_COLS - 1:], fill, rpb[..., :WIN_COLS - 1]], axis=-1)
    return jnp.pad(band, ((0, 0), (0, 1), (0, 0)))


BACK_TM = CHUNK * CH_PER_SEG
BACK_CTX_STEPS = N_CTX_TOK // BACK_TM
BACK_STEPS_PER_DEC_SEQ = DEC_SEQ // BACK_TM


def _back_kernel(xp_ref, xs_ref, y_ref, zs_ref, ac_ref, ad_ref, gs_ref, ga_ref, gate_ref, bglu_ref,
                 wglu_ref, wso_ref, wao_ref, wo_ref, op_ref, os_ref):
    i = pl.program_id(0)

    def compute(x, a2):
        ys = _gelu_tanh(y_ref[0].reshape(BACK_TM, SSM_WIDTH))
        t = jnp.dot(ys.astype(BF16), wglu_ref[...], preferred_element_type=F32) + bglu_ref[...]
        ys = ys * _sigmoid(t) * _silu(zs_ref[...].astype(F32))
        p_s = jnp.dot(ys.astype(BF16), wso_ref[...], preferred_element_type=F32)
        p_a = jnp.dot(a2, wao_ref[...], preferred_element_type=F32)
        merged = (_sigmoid(gs_ref[...].astype(F32)) * p_s + _sigmoid(ga_ref[...].astype(F32)) * p_a)
        return x + gate_ref[0] * jnp.dot(merged.astype(BF16), wo_ref[...], preferred_element_type=F32)

    @pl.when(i < BACK_CTX_STEPS)
    def _():
        op_ref[...] = compute(xp_ref[...], ac_ref[...])

    @pl.when(i >= BACK_CTX_STEPS)
    def _():
        os_ref[...] = compute(xs_ref[...], ad_ref[...])


def _back(xp, xs, y, zs, a_ctx, a_dec, gs, ga, gate3, b_glu, w_glu, w_so, w_ao, w_o):
    steps = N_TOK // BACK_TM
    n0 = BACK_CTX_STEPS
    row = functools.partial(_mod_row, ctx_steps=n0, steps_per_seq=BACK_STEPS_PER_DEC_SEQ)
    lo = lambda i: (jnp.minimum(i, n0 - 1), 0)
    hi = lambda i: (jnp.maximum(i - n0, 0), 0)
    cur = lambda i: (i, 0)
    const = lambda i: (0, 0)
    once = pl.Buffered(1)
    return pl.pallas_call(
        _back_kernel,
        grid=(steps,),
        in_specs=[pl.BlockSpec((BACK_TM, D_MODEL), lo),
                  pl.BlockSpec((BACK_TM, D_MODEL), hi),
                  pl.BlockSpec((1, CH_PER_SEG, None, CHUNK, SSM_WIDTH),
                               lambda i: (i // N_SEG, 0, i % N_SEG, 0, 0)),
                  pl.BlockSpec((BACK_TM, SSM_WIDTH), cur),
                  pl.BlockSpec((BACK_TM, ATT_WIDTH), lo),
                  pl.BlockSpec((BACK_TM, ATT_WIDTH), hi),
                  pl.BlockSpec((BACK_TM, D_MODEL), cur),
                  pl.BlockSpec((BACK_TM, D_MODEL), cur),
                  pl.BlockSpec((1, 1, D_MODEL), lambda i: (row(i), 0, 0)),
                  pl.BlockSpec((1, SSM_WIDTH), const),
                  pl.BlockSpec((SSM_WIDTH, SSM_WIDTH), const, pipeline_mode=once),
                  pl.BlockSpec((SSM_WIDTH, D_MODEL), const, pipeline_mode=once),
                  pl.BlockSpec((ATT_WIDTH, D_MODEL), const, pipeline_mode=once),
                  pl.BlockSpec((D_MODEL, D_MODEL), const, pipeline_mode=once)],
        out_specs=[pl.BlockSpec((BACK_TM, D_MODEL), lo),
                   pl.BlockSpec((BACK_TM, D_MODEL), hi)],
        out_shape=[jax.ShapeDtypeStruct((N_CTX_TOK, D_MODEL), F32),
                   jax.ShapeDtypeStruct((N_DEC_TOK, D_MODEL), F32)],
        compiler_params=_params("arbitrary"),
        name="gated_output",
    )(xp, xs, y, zs, a_ctx, a_dec, gs, ga, gate3, b_glu.reshape(1, SSM_WIDTH), w_glu, w_so, w_ao, w_o)


def _layer(xp, xs, cache_k, cache_v, st_re, st_im, c, c_ctx, norm_w, w_ada, b_ada, w_in, q_norm_w, k_norm_w,
           rpb, a_re, a_im, log_dt, b_re, b_im, c_re, c_im, d, w_glu, b_glu, w_so, w_ao, w_o):
    G, P = SSM_GROUPS, SSM_STATE

    cond8 = jnp.zeros((8, D_MODEL), F32).at[0].set(c_ctx).at[1:1 + DEC_BATCH].set(c)
    mod = _modulation(cond8, w_ada, b_ada)
    shift3 = mod[:1 + DEC_BATCH, None, :D_MODEL]
    scale3 = mod[:1 + DEC_BATCH, None, D_MODEL:2 * D_MODEL]
    gate3 = mod[:1 + DEC_BATCH, None, 2 * D_MODEL:]

    h = _norm_modulate(xp, xs, shift3, scale3, norm_w)
    u = _in_proj(h, w_in, OFF_U, SSM_WIDTH, "u")
    zs = _in_proj(h, w_in, OFF_ZS, SSM_WIDTH, "plain")
    q = _in_proj(h, w_in, OFF_Q, ATT_WIDTH, "q", q_norm_w)
    k, k_ctx32 = _in_proj(h, w_in, OFF_K, ATT_WIDTH, "k", k_norm_w)
    v, v_ctx32 = _in_proj(h, w_in, OFF_V, ATT_WIDTH, "v")
    za = _in_proj(h, w_in, OFF_ZA, ATT_WIDTH, "plain")
    gs = _in_proj(h, w_in, OFF_GS, D_MODEL, "plain")
    ga = _in_proj(h, w_in, OFF_GA, D_MODEL, "plain")

    arow = jnp.transpose(a_re, (1, 0, 2)).reshape(G, 2 * P)
    airow = jnp.transpose(a_im, (1, 0, 2)).reshape(G, 2 * P)
    dtrow = jnp.broadcast_to(jnp.transpose(log_dt, (1, 0))[:, :, None], (G, 2, P)).reshape(G, 2 * P)
    rowp = jnp.stack([arow, airow, dtrow], axis=1)
    btr = jnp.transpose(b_re, (1, 3, 0, 2)).reshape(G, SSM_GROUP, 2 * P)
    bti = jnp.transpose(b_im, (1, 3, 0, 2)).reshape(G, SSM_GROUP, 2 * P)
    ctr = jnp.transpose(c_re, (1, 2, 0, 3)).reshape(G, SSM_GROUP, 2 * P)
    cti = jnp.transpose(c_im, (1, 2, 0, 3)).reshape(G, SSM_GROUP, 2 * P)
    t_op, s_op, r_op, lam, powc = _ssm_operators(rowp, btr, bti, ctr, cti)

    dtile = d.reshape(1, SSM_WIDTH)
    h0c = jnp.zeros((G, BATCH, 2 * P), F32)
    h0d_re = jnp.transpose(st_re, (2, 0, 1, 3)).reshape(G, DEC_BATCH, 2 * P)
    h0d_im = jnp.transpose(st_im, (2, 0, 1, 3)).reshape(G, DEC_BATCH, 2 * P)
    y2d, fin = _ssm_scan(u.reshape(N_TOK, SSM_WIDTH), t_op, s_op, r_op, lam, powc, dtile,
                         h0c, h0c, h0d_re, h0d_im)
    y = y2d.reshape(2, CH_PER_SEG, N_SEG, CHUNK, SSM_WIDTH)

    a_ctx = _ctx_attention(q, k, v, za)
    nseq4 = N_TOK // DEC_SEQ
    a_dec = _nbr_attention(q.reshape(nseq4, DEC_SEQ, ATT_WIDTH), k.reshape(nseq4, DEC_SEQ, ATT_WIDTH),
                           v.reshape(nseq4, DEC_SEQ, ATT_WIDTH),
                           cache_k.reshape(DEC_BATCH, PAST_LEN, ATT_WIDTH),
                           cache_v.reshape(DEC_BATCH, PAST_LEN, ATT_WIDTH),
                           _nbr_bias_band(rpb.astype(F32)),
                           za.reshape(nseq4, DEC_SEQ, ATT_WIDTH))

    yp, ys_out = _back(xp, xs, y, zs, a_ctx, a_dec.reshape(N_DEC_TOK, ATT_WIDTH), gs, ga, gate3, b_glu,
                       w_glu.astype(BF16), w_so.astype(BF16), w_ao.astype(BF16), w_o.astype(BF16))

    fin_ctx = fin[0]
    new_re = jnp.transpose(fin_ctx[:, :, :2 * P].reshape(G, BATCH, 2, P), (1, 2, 0, 3))
    new_im = jnp.transpose(fin_ctx[:, :, 2 * P:].reshape(G, BATCH, 2, P), (1, 2, 0, 3))
    return yp, ys_out, k_ctx32, v_ctx32, new_re, new_im


def kernel(x_prompt, x_sample, cache_k, cache_v, state_ssm_re, state_ssm_im, c, c_ctx, norm_w, w_ada, b_ada,
           w_in, q_norm_w, k_norm_w, rel_pos_bias, ssm_a_re, ssm_a_im, ssm_log_dt, ssm_b_re, ssm_b_im,
           ssm_c_re, ssm_c_im, ssm_d, w_glu, b_glu, w_ssm_out, w_att_out, w_o):
    depth = norm_w.shape[0]
    xp = x_prompt.reshape(N_CTX_TOK, D_MODEL)
    xs = x_sample.reshape(N_DEC_TOK, D_MODEL)
    new_k, new_v, new_re, new_im = [], [], [], []
    for l in range(depth):
        xp, xs, kl, vl, rl, il = _layer(
            xp, xs, cache_k[:, l], cache_v[:, l], state_ssm_re[:, l], state_ssm_im[:, l], c, c_ctx,
            norm_w[l], w_ada[l], b_ada[l], w_in[l], q_norm_w[l], k_norm_w[l], rel_pos_bias[l],
            ssm_a_re[l], ssm_a_im[l], ssm_log_dt[l], ssm_b_re[l], ssm_b_im[l], ssm_c_re[l], ssm_c_im[l],
            ssm_d[l], w_glu[l], b_glu[l], w_ssm_out[l], w_att_out[l], w_o[l])
        new_k.append(kl.reshape(BATCH, SEQ, N_HEADS, HEAD_DIM))
        new_v.append(vl.reshape(BATCH, SEQ, N_HEADS, HEAD_DIM))
        new_re.append(rl)
        new_im.append(il)
    return (xp.reshape(BATCH, SEQ, D_MODEL), xs.reshape(DEC_BATCH, DEC_SEQ, D_MODEL),
            jnp.stack(new_k, axis=1), jnp.stack(new_v, axis=1),
            jnp.stack(new_re, axis=1), jnp.stack(new_im, axis=1))
```

```python
import functools
import math

import jax
import jax.numpy as jnp
from jax import lax
from jax.experimental import pallas as pl
from jax.experimental.pallas import tpu as pltpu

D_MODEL = 2048
BATCH = 16
SEQ = 256
DEC_BATCH = 2
DEC_SEQ = 2048
PAST_LEN = 512
GRID_W = 64
GRID_H = DEC_SEQ // GRID_W
SSM_WIDTH = D_MODEL // 2
SSM_GROUP = 16
SSM_GROUPS = SSM_WIDTH // SSM_GROUP
SSM_STATE = 64
N_HEADS = 16
HEAD_DIM = 64
ATT_WIDTH = N_HEADS * HEAD_DIM
WIN_ROWS = 8
WIN_COLS = 16
EPS = 1e-6
NEG_INF = -1e30

N_CTX_TOK = BATCH * SEQ
N_DEC_TOK = DEC_BATCH * DEC_SEQ
N_TOK = N_CTX_TOK + N_DEC_TOK

OFF_U = 0
OFF_ZS = SSM_WIDTH
OFF_Q = 2 * SSM_WIDTH
OFF_K = OFF_Q + ATT_WIDTH
OFF_V = OFF_K + ATT_WIDTH
OFF_ZA = OFF_V + ATT_WIDTH
OFF_GS = OFF_ZA + ATT_WIDTH
OFF_GA = OFF_GS + D_MODEL

CHUNK = 16
FLAT = CHUNK * SSM_GROUP
N_SEG = 16
CH_PER_SEG = 16
DSTATE = 2 * SSM_STATE

Q_ROWS_PER_BLOCK = 4
NBR_Q = Q_ROWS_PER_BLOCK * GRID_W
NBR_KROWS = 12
NBR_K = NBR_KROWS * GRID_W

VMEM_LIMIT = 56 * 1024 * 1024

F32 = jnp.float32
BF16 = jnp.bfloat16
HIGHEST = lax.Precision.HIGHEST


def _sigmoid(x):
    return 1.0 / (1.0 + jnp.exp(-x))


def _silu(x):
    return x * _sigmoid(x)


def _gelu_tanh(x):
    return 0.5 * x * (1.0 + jnp.tanh(math.sqrt(2.0 / math.pi) * (x + 0.044715 * (x * x * x))))


def _cmul(ar, ai, br, bi):
    return ar * br - ai * bi, ar * bi + ai * br


def _dot_nt(a, b):
    return lax.dot_general(a, b, (((1,), (1,)), ((), ())), preferred_element_type=F32)


def _params(*sem):
    return pltpu.CompilerParams(dimension_semantics=sem, vmem_limit_bytes=VMEM_LIMIT)


def _mod_kernel(cond_ref, w_ref, b_ref, o_ref):
    c = cond_ref[...]
    s = _silu(c).astype(BF16)
    o_ref[...] = jnp.dot(s, w_ref[...].astype(BF16), preferred_element_type=F32) + b_ref[...]


def _modulation(cond8, w_ada, b_ada):
    tn = 768
    n = w_ada.shape[1]
    return pl.pallas_call(
        _mod_kernel,
        grid=(n // tn,),
        in_specs=[pl.BlockSpec((8, D_MODEL), lambda j: (0, 0)),
                  pl.BlockSpec((D_MODEL, tn), lambda j: (0, j)),
                  pl.BlockSpec((1, tn), lambda j: (0, j))],
        out_specs=pl.BlockSpec((8, tn), lambda j: (0, j)),
        out_shape=jax.ShapeDtypeStruct((8, n), F32),
        compiler_params=_params("arbitrary"),
        name="modulation",
    )(cond8, w_ada, b_ada.reshape(1, n))


NORM_TM = 512
NORM_ROWS = 16
NORM_UNROLL = 8
NORM_CTX_STEPS = N_CTX_TOK // NORM_TM
NORM_STEPS_PER_DEC_SEQ = DEC_SEQ // NORM_TM


def _mod_row(i, ctx_steps, steps_per_seq):
    return jnp.where(i < ctx_steps, 0, 1 + (i - ctx_steps) // steps_per_seq)


def _norm_kernel(xp_ref, xs_ref, shift_ref, scale_ref, nw_ref, o_ref):
    i = pl.program_id(0)

    gain = nw_ref[...] * (1.0 + scale_ref[0])
    shift = shift_ref[0]

    def body(x_ref):
        def rows(r, carry):
            sl = pl.ds(pl.multiple_of(r * NORM_ROWS, NORM_ROWS), NORM_ROWS)
            x = x_ref[sl, :]
            ms = jnp.mean(x * x, axis=-1, keepdims=True)
            o_ref[sl, :] = (x * lax.rsqrt(ms + EPS) * gain + shift).astype(BF16)
            return carry

        lax.fori_loop(0, NORM_TM // NORM_ROWS, rows, 0, unroll=NORM_UNROLL)

    @pl.when(i < NORM_CTX_STEPS)
    def _():
        body(xp_ref)

    @pl.when(i >= NORM_CTX_STEPS)
    def _():
        body(xs_ref)


def _norm_modulate(xp, xs, shift3, scale3, norm_w):
    steps = N_TOK // NORM_TM
    row = functools.partial(_mod_row, ctx_steps=NORM_CTX_STEPS, steps_per_seq=NORM_STEPS_PER_DEC_SEQ)
    return pl.pallas_call(
        _norm_kernel,
        grid=(steps,),
        in_specs=[pl.BlockSpec((NORM_TM, D_MODEL), lambda i: (jnp.minimum(i, NORM_CTX_STEPS - 1), 0)),
                  pl.BlockSpec((NORM_TM, D_MODEL), lambda i: (jnp.maximum(i - NORM_CTX_STEPS, 0), 0)),
                  pl.BlockSpec((1, 1, D_MODEL), lambda i: (row(i), 0, 0)),
                  pl.BlockSpec((1, 1, D_MODEL), lambda i: (row(i), 0, 0)),
                  pl.BlockSpec((1, D_MODEL), lambda i: (0, 0))],
        out_specs=pl.BlockSpec((NORM_TM, D_MODEL), lambda i: (i, 0)),
        out_shape=jax.ShapeDtypeStruct((N_TOK, D_MODEL), BF16),
        compiler_params=_params("arbitrary"),
        name="norm_modulate",
    )(xp, xs, shift3, scale3, norm_w.reshape(1, D_MODEL))


PROJ_TM = 1024
PROJ_TN = 1024
PROJ_CTX_STEPS = N_CTX_TOK // PROJ_TM
PROJ_SEGS = PROJ_TM // (CHUNK * CH_PER_SEG)


def _head_group_ones():
    r = lax.broadcasted_iota(jnp.int32, (256, 256), 0) // HEAD_DIM
    c = lax.broadcasted_iota(jnp.int32, (256, 256), 1) // HEAD_DIM
    return jnp.where(r == c, 1.0, 0.0).astype(BF16)


def _head_rms(acc, nw):
    ones = _head_group_ones()
    outs = []
    for c in range(acc.shape[1] // 256):
        a = acc[:, c * 256:(c + 1) * 256]
        ssum = jnp.dot((a * a).astype(BF16), ones, preferred_element_type=F32)
        outs.append(a * lax.rsqrt(ssum * (1.0 / HEAD_DIM) + EPS))
    return jnp.concatenate(outs, axis=1) * nw


def _proj_tile(h_ref, w_ref, wbf):
    @pl.when(pl.program_id(1) == 0)
    def _():
        wbf[...] = w_ref[...].astype(BF16)

    return jnp.dot(h_ref[...], wbf[...], preferred_element_type=F32)


def _proj_s5_kernel(h_ref, w_ref, u_ref, zs_ref, wbf):
    j = pl.program_id(0)
    acc = _proj_tile(h_ref, w_ref, wbf)

    @pl.when(j == 0)
    def _():
        for seg in range(PROJ_SEGS):
            for ch in range(CH_PER_SEG):
                r0 = (seg * CH_PER_SEG + ch) * CHUNK
                u_ref[0, ch, seg] = acc[r0:r0 + CHUNK, :]

    @pl.when(j == 1)
    def _():
        zs_ref[...] = acc.astype(BF16)


def _proj_qkv_kernel(h_ref, w_ref, nw_ref, o_ref, kv_ref, wbf):
    j = pl.program_id(0)
    i = pl.program_id(1)
    acc = _proj_tile(h_ref, w_ref, wbf)

    @pl.when(j == 0)
    def _():
        o_ref[...] = (_head_rms(acc, nw_ref[0]) * (HEAD_DIM ** -0.5)).astype(BF16)

    @pl.when(j == 1)
    def _():
        kn = _head_rms(acc, nw_ref[0])
        o_ref[...] = kn.astype(BF16)

        @pl.when(i < PROJ_CTX_STEPS)
        def _():
            kv_ref[0] = kn

    @pl.when(j == 2)
    def _():
        o_ref[...] = acc.astype(BF16)

        @pl.when(i < PROJ_CTX_STEPS)
        def _():
            kv_ref[0] = acc


def _proj_gate_kernel(h_ref, w_ref, o_ref, wbf):
    o_ref[...] = _proj_tile(h_ref, w_ref, wbf).astype(BF16)


def _in_proj(h, w_in, q_norm_w, k_norm_w):
    ni = N_TOK // PROJ_TM
    last = ni - 1
    ctx_last = PROJ_CTX_STEPS - 1
    tiles_per_path = N_SEG // PROJ_SEGS
    h_spec = pl.BlockSpec((PROJ_TM, D_MODEL), lambda j, i: (i, 0))
    scratch = [pltpu.VMEM((D_MODEL, PROJ_TN), BF16)]
    params = _params("arbitrary", "arbitrary")

    def u_map(j, i):
        ii = jnp.where(j == 0, i, last)
        return (ii // tiles_per_path, 0, ii % tiles_per_path, 0, 0)

    u, zs = pl.pallas_call(
        _proj_s5_kernel,
        grid=(2, ni),
        in_specs=[h_spec, pl.BlockSpec((D_MODEL, PROJ_TN), lambda j, i: (0, OFF_U // PROJ_TN + j))],
        out_specs=[pl.BlockSpec((1, CH_PER_SEG, PROJ_SEGS, CHUNK, PROJ_TN), u_map),
                   pl.BlockSpec((PROJ_TM, PROJ_TN), lambda j, i: (jnp.where(j == 0, 0, i), 0))],
        out_shape=[jax.ShapeDtypeStruct((2, CH_PER_SEG, N_SEG, CHUNK, SSM_WIDTH), F32),
                   jax.ShapeDtypeStruct((N_TOK, SSM_WIDTH), BF16)],
        scratch_shapes=scratch, compiler_params=params, name="in_proj_s5",
    )(h, w_in)

    nw = jnp.stack([jnp.tile(q_norm_w.reshape(1, HEAD_DIM), (1, PROJ_TN // HEAD_DIM)),
                    jnp.tile(k_norm_w.reshape(1, HEAD_DIM), (1, PROJ_TN // HEAD_DIM))])
    qkv, kv32 = pl.pallas_call(
        _proj_qkv_kernel,
        grid=(3, ni),
        in_specs=[h_spec, pl.BlockSpec((D_MODEL, PROJ_TN), lambda j, i: (0, OFF_Q // PROJ_TN + j)),
                  pl.BlockSpec((1, 1, PROJ_TN), lambda j, i: (jnp.where(j == 1, 1, 0), 0, 0))],
        out_specs=[pl.BlockSpec((PROJ_TM, PROJ_TN), lambda j, i: (i, j)),
                   pl.BlockSpec((1, PROJ_TM, PROJ_TN),
                                lambda j, i: (jnp.where(j == 2, 1, 0),
                                              jnp.where(j == 0, 0, jnp.minimum(i, ctx_last)), 0))],
        out_shape=[jax.ShapeDtypeStruct((N_TOK, 3 * ATT_WIDTH), BF16),
                   jax.ShapeDtypeStruct((2, N_CTX_TOK, ATT_WIDTH), F32)],
        scratch_shapes=scratch, compiler_params=params, name="in_proj_qkv",
    )(h, w_in, nw)

    gate_tiles = 2 * D_MODEL // PROJ_TN
    gz = pl.pallas_call(
        _proj_gate_kernel,
        grid=(gate_tiles + 1, ni),
        in_specs=[h_spec,
                  pl.BlockSpec((D_MODEL, PROJ_TN),
                               lambda j, i: (0, jnp.where(j < gate_tiles, OFF_GS // PROJ_TN + j,
                                                          OFF_ZA // PROJ_TN)))],
        out_specs=pl.BlockSpec((PROJ_TM, PROJ_TN), lambda j, i: (i, j)),
        out_shape=jax.ShapeDtypeStruct((N_TOK, 2 * D_MODEL + ATT_WIDTH), BF16),
        scratch_shapes=scratch, compiler_params=params, name="in_proj_gates",
    )(h, w_in)
    return u, zs, qkv, kv32, gz


SSM_GB = 4


def _pow_select(e, pows):
    rr = jnp.where((e & 1) != 0, pows[0][0], 1.0)
    ri = jnp.where((e & 1) != 0, pows[0][1], 0.0)
    for b in range(1, len(pows)):
        bit = (e & (1 << b)) != 0
        fr = jnp.where(bit, pows[b][0], 1.0)
        fi = jnp.where(bit, pows[b][1], 0.0)
        rr, ri = _cmul(rr, ri, fr, fi)
    return rr, ri


def _discretize(a_re, a_im, log_dt):
    lr = jnp.minimum(a_re, -1e-4)
    li = a_im
    dt = jnp.exp(log_dt)
    mag = jnp.exp(lr * dt)
    br = mag * jnp.cos(li * dt)
    bi = mag * jnp.sin(li * dt)
    den = lr * lr + li * li
    nr = br - 1.0
    cr = (nr * lr + bi * li) / den
    ci = (bi * lr - nr * li) / den
    return (br, bi), (cr, ci)


def _squarings(pr, pi, n):
    out = [(pr, pi)]
    for _ in range(n):
        pr, pi = _cmul(pr, pi, pr, pi)
        out.append((pr, pi))
    return out


def _dot_nt_exact(a, b):
    return lax.dot_general(a, b, (((1,), (1,)), ((), ())), preferred_element_type=F32, precision=HIGHEST)


def _ssm_ops_kernel(rowp_ref, btr_ref, bti_ref, ctr_ref, cti_ref, t_ref, s_ref, rt_ref, lam_ref, powc_ref):
    lane_b = lax.broadcasted_iota(jnp.int32, (SSM_GROUP, DSTATE), 1)
    lane_t = lax.broadcasted_iota(jnp.int32, (SSM_GROUP, FLAT), 1)
    row_c = lax.broadcasted_iota(jnp.int32, (CH_PER_SEG, DSTATE), 0)
    lane_c = lax.broadcasted_iota(jnp.int32, (CH_PER_SEG, DSTATE), 1)
    exp_c = jnp.where(lane_c < SSM_STATE, row_c, (CH_PER_SEG - 1) - row_c)

    for g in range(SSM_GB):
        rp = rowp_ref[g]
        (lbr, lbi), (cfr, cfi) = _discretize(rp[0:1], rp[1:2], rp[2:3])
        sq = _squarings(lbr, lbi, 8)
        bbr, bbi = _cmul(btr_ref[g], bti_ref[g], cfr, cfi)

        ptr, pti = _pow_select(exp_c, sq[:4])

        s_re, s_im = [], []
        for sp in range(CHUNK):
            e = CHUNK - 1 - sp
            br, bi = _cmul(bbr, bbi, ptr[e:e + 1], pti[e:e + 1])
            s_re.append(br)
            s_im.append(bi)
        s_ref[g] = jnp.concatenate([jnp.concatenate(s_re, axis=0), jnp.concatenate(s_im, axis=0)],
                                   axis=1).astype(BF16)

        xcr = jnp.concatenate([ctr_ref[g]] * CHUNK, axis=0)
        xci = jnp.concatenate([cti_ref[g]] * CHUNK, axis=0)
        xpr = jnp.broadcast_to(ptr[:, None, :], (CHUNK, SSM_GROUP, DSTATE)).reshape(FLAT, DSTATE)
        xpi = jnp.broadcast_to(pti[:, None, :], (CHUNK, SSM_GROUP, DSTATE)).reshape(FLAT, DSTATE)
        ykr, yki = _cmul(xcr, xci, xpr, xpi)
        yrr, yri = _cmul(ykr, yki, lbr, lbi)
        rt_ref[g] = jnp.concatenate([yrr, -yri], axis=1).astype(BF16)

        fwd = lane_b < SSM_STATE
        lhs = jnp.concatenate([jnp.where(fwd, bbr, 0.0), jnp.where(fwd, bbi, 0.0),
                               jnp.where(fwd, 0.0, bbr), jnp.where(fwd, 0.0, bbi)], axis=0)
        p1 = _dot_nt_exact(lhs, ykr)
        p2 = _dot_nt_exact(lhs, yki)
        kf = p1[0:16] - p2[16:32]
        kb = p1[32:48] - p2[48:64]
        for sp in range(CHUNK):
            tf = kf if sp == 0 else pltpu.roll(kf, SSM_GROUP * sp, axis=1)
            tf = jnp.where(lane_t >= SSM_GROUP * sp, tf, 0.0)
            shift = (FLAT - SSM_GROUP * (CHUNK - 1 - sp)) % FLAT
            tb = kb if shift == 0 else pltpu.roll(kb, shift, axis=1)
            tb = jnp.where(lane_t < SSM_GROUP * (sp + 1), tb, 0.0)
            t_ref[g, sp * SSM_GROUP:(sp + 1) * SSM_GROUP, :] = (tf + tb).astype(BF16)

        l16 = sq[4]
        l256 = sq[8]
        lam_ref[g] = jnp.concatenate([l16[0], l16[1], l256[0], l256[1],
                                      jnp.zeros((4, DSTATE), F32)], axis=0)
        q16 = _squarings(l16[0], l16[1], 3)
        pcr, pci = _pow_select(exp_c, q16)
        powc_ref[g, 0] = pcr
        powc_ref[g, 1] = pci


def _ssm_operators(rowp, btr, bti, ctr, cti):
    G = SSM_GROUPS
    gb = SSM_GB
    return pl.pallas_call(
        _ssm_ops_kernel,
        grid=(G // gb,),
        in_specs=[pl.BlockSpec((gb, 3, DSTATE), lambda g: (g, 0, 0)),
                  pl.BlockSpec((gb, SSM_GROUP, DSTATE), lambda g: (g, 0, 0)),
                  pl.BlockSpec((gb, SSM_GROUP, DSTATE), lambda g: (g, 0, 0)),
                  pl.BlockSpec((gb, SSM_GROUP, DSTATE), lambda g: (g, 0, 0)),
                  pl.BlockSpec((gb, SSM_GROUP, DSTATE), lambda g: (g, 0, 0))],
        out_specs=[pl.BlockSpec((gb, FLAT, FLAT), lambda g: (g, 0, 0)),
                   pl.BlockSpec((gb, FLAT, FLAT), lambda g: (g, 0, 0)),
                   pl.BlockSpec((gb, FLAT, FLAT), lambda g: (g, 0, 0)),
                   pl.BlockSpec((gb, 8, DSTATE), lambda g: (g, 0, 0)),
                   pl.BlockSpec((gb, 2, CH_PER_SEG, DSTATE), lambda g: (g, 0, 0, 0))],
        out_shape=[jax.ShapeDtypeStruct((G, FLAT, FLAT), BF16),
                   jax.ShapeDtypeStruct((G, FLAT, FLAT), BF16),
                   jax.ShapeDtypeStruct((G, FLAT, FLAT), BF16),
                   jax.ShapeDtypeStruct((G, 8, DSTATE), F32),
                   jax.ShapeDtypeStruct((G, 2, CH_PER_SEG, DSTATE), F32)],
        compiler_params=_params("arbitrary"),
        name="ssm_operators",
    )(rowp, btr, bti, ctr, cti)


SSM_PATHS = ((BATCH, SEQ // (CHUNK * CH_PER_SEG)), (DEC_BATCH, DEC_SEQ // (CHUNK * CH_PER_SEG)))


def _ssm_path(u, s_op, t_op, rt_op, lam, powc_ref, h0r, h0i, nseg):
    lane = lax.broadcasted_iota(jnp.int32, (N_SEG, DSTATE), 1)
    isf = lane < SSM_STATE
    l16r, l16i, l256r, l256i = lam[0:1], lam[1:2], lam[2:3], lam[3:4]

    z = jnp.dot(u, s_op, preferred_element_type=F32)
    zre, zim = z[:, :DSTATE], z[:, DSTATE:]

    hr = jnp.zeros((N_SEG, DSTATE), F32)
    hi = jnp.zeros((N_SEG, DSTATE), F32)
    hist = []
    for k in range(CH_PER_SEG):
        hist.append((hr, hi))
        kb = CH_PER_SEG - 1 - k
        zr = jnp.where(isf, zre[k * N_SEG:(k + 1) * N_SEG], zre[kb * N_SEG:(kb + 1) * N_SEG])
        zi = jnp.where(isf, zim[k * N_SEG:(k + 1) * N_SEG], zim[kb * N_SEG:(kb + 1) * N_SEG])
        nr, ni = _cmul(l16r, l16i, hr, hi)
        hr, hi = nr + zr, ni + zi

    if nseg > 1:
        h0r = jnp.broadcast_to(h0r[:, None, :], (N_SEG // nseg, nseg, DSTATE)).reshape(N_SEG, DSTATE)
        h0i = jnp.broadcast_to(h0i[:, None, :], (N_SEG // nseg, nseg, DSTATE)).reshape(N_SEG, DSTATE)
    seg = lax.broadcasted_iota(jnp.int32, (N_SEG, DSTATE), 0) & (nseg - 1)
    segpow = _squarings(l256r, l256i, max(int(math.log2(nseg)), 0))

    def shifted(x, d):
        dn = jnp.where(seg >= d, pltpu.roll(x, d, axis=0), 0.0)
        up = jnp.where(seg <= nseg - 1 - d, pltpu.roll(x, N_SEG - d, axis=0), 0.0)
        return jnp.where(isf, dn, up)

    pr, pi = hr, hi
    d = 1
    lvl = 0
    while d < nseg:
        ar, ai = _cmul(segpow[lvl][0], segpow[lvl][1], shifted(pr, d), shifted(pi, d))
        pr, pi = pr + ar, pi + ai
        d *= 2
        lvl += 1
    if nseg > 1:
        e_in = jnp.where(isf, seg, nseg - 1 - seg)
        wr, wi = _pow_select(e_in, segpow[:lvl])
        ar, ai = _cmul(wr, wi, h0r, h0i)
        hsr, hsi = shifted(pr, 1) + ar, shifted(pi, 1) + ai
    else:
        hsr, hsi = h0r, h0i
    er, ei = _cmul(segpow[lvl][0], segpow[lvl][1], h0r, h0i)
    fin = jnp.concatenate([pr + er, pi + ei], axis=1)

    rows = []
    for c in range(CH_PER_SEG):
        cb = CH_PER_SEG - 1 - c
        lr = jnp.where(isf, hist[c][0], hist[cb][0])
        li = jnp.where(isf, hist[c][1], hist[cb][1])
        ar, ai = _cmul(powc_ref[0, c:c + 1, :], powc_ref[1, c:c + 1, :], hsr, hsi)
        rows.append(jnp.concatenate([lr + ar, li + ai], axis=1))
    hent = jnp.concatenate(rows, axis=0).astype(BF16)

    y = jnp.dot(u, t_op, preferred_element_type=F32) + _dot_nt(hent, rt_op)
    return y, fin


GROUPS_PER_STEP = 128 // SSM_GROUP
GROUPS_PER_ITER = 8


def _block_transpose(arrs, blk):
    n = len(arrs)
    width = arrs[0].shape[1]
    j = lax.broadcasted_iota(jnp.int32, arrs[0].shape, 1) // blk
    k = n // 2
    while k >= 1:
        bit = (j & k) != 0
        new = list(arrs)
        for x in range(n):
            if x & k == 0:
                a, b = arrs[x], arrs[x | k]
                new[x] = jnp.where(bit, pltpu.roll(b, k * blk, axis=1), a)
                new[x | k] = jnp.where(bit, b, pltpu.roll(a, width - k * blk, axis=1))
        arrs = new
        k //= 2
    return arrs


def _ssm_kernel(u_ref, t_ref, s_ref, r_ref, lam_ref, powc_ref, d_ref, h0cr_ref, h0ci_ref, h0dr_ref, h0di_ref,
                y_ref, fin_ref, ubuf, ybuf):
    h0 = ((h0cr_ref, h0ci_ref), (h0dr_ref, h0di_ref))
    nhalf = FLAT // 128
    rows = N_SEG * CH_PER_SEG
    def token_rows(path, tok, r0, nrows):
        return pl.ds(path * rows * CHUNK + r0 * CHUNK + tok, nrows, stride=CHUNK)

    in_rows = rows
    out_rows = rows
    for path, (_, nseg) in enumerate(SSM_PATHS):
        for half in range(nhalf):
            for r0 in range(0, rows, in_rows):
                toks = [pltpu.bitcast(
                    u_ref[token_rows(path, half * GROUPS_PER_STEP + sb, r0, in_rows), :].astype(BF16), jnp.uint32)
                    for sb in range(GROUPS_PER_STEP)]
                grouped = _block_transpose(toks, SSM_GROUP)
                for g in range(GROUPS_PER_STEP):
                    ubuf[g, r0:r0 + in_rows, half * 128:(half + 1) * 128] = pltpu.bitcast(grouped[g], BF16)

        def body(gi, carry, path=path, nseg=nseg):
            for dg in range(GROUPS_PER_ITER):
                g = gi * GROUPS_PER_ITER + dg
                y, fin = _ssm_path(ubuf[g], s_ref[g], t_ref[g], r_ref[g], lam_ref[g], powc_ref.at[g],
                                   h0[path][0][g], h0[path][1][g], nseg)
                ybuf[g] = y
                fin_ref[path, g] = fin
            return carry

        lax.fori_loop(0, GROUPS_PER_STEP // GROUPS_PER_ITER, body, 0)

        for half in range(nhalf):
            for r0 in range(0, rows, out_rows):
                grouped = [ybuf[g, r0:r0 + out_rows, half * 128:(half + 1) * 128] for g in range(GROUPS_PER_STEP)]
                toks = _block_transpose(grouped, SSM_GROUP)
                for sb in range(GROUPS_PER_STEP):
                    sel = token_rows(path, half * GROUPS_PER_STEP + sb, r0, out_rows)
                    y_ref[sel, :] = toks[sb] + u_ref[sel, :] * d_ref[...]


def _ssm_scan(u2d, t_op, s_op, r_op, lam, powc, dtile, h0c_re, h0c_im, h0d_re, h0d_im):
    G = SSM_GROUPS
    gs = GROUPS_PER_STEP
    rows = N_SEG * CH_PER_SEG
    return pl.pallas_call(
        _ssm_kernel,
        grid=(G // gs,),
        in_specs=[
            pl.BlockSpec((N_TOK, 128), lambda o: (0, o)),
            pl.BlockSpec((gs, FLAT, FLAT), lambda o: (o, 0, 0)),
            pl.BlockSpec((gs, FLAT, FLAT), lambda o: (o, 0, 0)),
            pl.BlockSpec((gs, FLAT, FLAT), lambda o: (o, 0, 0)),
            pl.BlockSpec((gs, 8, DSTATE), lambda o: (o, 0, 0)),
            pl.BlockSpec((gs, 2, CH_PER_SEG, DSTATE), lambda o: (o, 0, 0, 0)),
            pl.BlockSpec((1, 128), lambda o: (0, o)),
            pl.BlockSpec((gs, BATCH, DSTATE), lambda o: (o, 0, 0)),
            pl.BlockSpec((gs, BATCH, DSTATE), lambda o: (o, 0, 0)),
            pl.BlockSpec((gs, DEC_BATCH, DSTATE), lambda o: (o, 0, 0)),
            pl.BlockSpec((gs, DEC_BATCH, DSTATE), lambda o: (o, 0, 0))],
        out_specs=[pl.BlockSpec((N_TOK, 128), lambda o: (0, o)),
                   pl.BlockSpec((2, gs, N_SEG, 2 * DSTATE), lambda o: (0, o, 0, 0))],
        out_shape=[jax.ShapeDtypeStruct((N_TOK, SSM_WIDTH), F32),
                   jax.ShapeDtypeStruct((2, G, N_SEG, 2 * DSTATE), F32)],
        scratch_shapes=[pltpu.VMEM((gs, rows, FLAT), BF16), pltpu.VMEM((gs, rows, FLAT), F32)],
        compiler_params=_params("arbitrary"),
        name="ssm_scan",
    )(u2d, t_op, s_op, r_op, lam, powc, dtile, h0c_re, h0c_im, h0d_re, h0d_im)


CTX_BB = 4
HEADS_PER_STEP = 256 // HEAD_DIM
HEAD_LANES = HEADS_PER_STEP * HEAD_DIM


def _softmax_pv(scores, values):
    m = scores[0].max(axis=-1, keepdims=True)
    for s in scores[1:]:
        m = jnp.maximum(m, s.max(axis=-1, keepdims=True))
    l = None
    o = None
    for s, v in zip(scores, values):
        p = jnp.exp(s - m)
        ls = p.sum(axis=-1, keepdims=True)
        os_ = jnp.dot(p.astype(BF16), v, preferred_element_type=F32)
        l = ls if l is None else l + ls
        o = os_ if o is None else o + os_
    return o / l


def _head_of_lane():
    return lax.broadcasted_iota(jnp.int32, (1, HEAD_LANES), 1) // HEAD_DIM


def _merge_heads(outs, za):
    head = _head_of_lane()
    o = outs[-1]
    for hh in range(HEADS_PER_STEP - 2, -1, -1):
        o = jnp.where(head == hh, outs[hh], o)
    return (o * _silu(za.astype(F32))).astype(BF16)


def _ctx_attn_kernel(q_ref, k_ref, v_ref, za_ref, o_ref):
    head = _head_of_lane()
    for b in range(CTX_BB):
        sl = slice(b * SEQ, (b + 1) * SEQ)
        q, k, v = q_ref[sl, :], k_ref[sl, :], v_ref[sl, :]
        outs = []
        for hh in range(HEADS_PER_STEP):
            qh = jnp.where(head == hh, q, jnp.zeros_like(q))
            outs.append(_softmax_pv([_dot_nt(qh, k)], [v]))
        o_ref[sl, :] = _merge_heads(outs, za_ref[sl, :])


HEAD_BLOCKS = ATT_WIDTH // HEAD_LANES
QKV_Q, QKV_K, QKV_V = 0, HEAD_BLOCKS, 2 * HEAD_BLOCKS
GZ_ZA = 2 * D_MODEL // HEAD_LANES


def _ctx_attention(qkv, gz):
    rows = CTX_BB * SEQ

    def spec(first):
        return pl.BlockSpec((rows, HEAD_LANES), lambda b, hq: (b, first + hq))

    return pl.pallas_call(
        _ctx_attn_kernel,
        grid=(BATCH // CTX_BB, N_HEADS // HEADS_PER_STEP),
        in_specs=[spec(QKV_Q), spec(QKV_K), spec(QKV_V), spec(GZ_ZA)],
        out_specs=spec(0),
        out_shape=jax.ShapeDtypeStruct((N_CTX_TOK, ATT_WIDTH), BF16),
        compiler_params=_params("arbitrary", "arbitrary"),
        name="ctx_attention",
    )(qkv, qkv, qkv, gz)


N_DROW = 2 * WIN_ROWS - 1


def _nbr_build_table(band_ref, tbl):
    lane = lax.broadcasted_iota(jnp.int32, (GRID_W, 2 * GRID_W), 1)
    qc = lax.broadcasted_iota(jnp.int32, (GRID_W, 2 * GRID_W), 0)
    kc = lane & (GRID_W - 1)
    cs = jnp.clip(qc - WIN_COLS // 2, 0, GRID_W - WIN_COLS)
    valid = jnp.logical_and(kc >= cs, kc < cs + WIN_COLS)
    for hh in range(HEADS_PER_STEP):
        for dr in range(N_DROW):
            x = jnp.broadcast_to(band_ref[hh, dr:dr + 1, :], (GRID_W, 2 * GRID_W))
            lo = pltpu.roll(x, 0, axis=1, stride=1, stride_axis=0)
            hi = pltpu.roll(x, GRID_W, axis=1, stride=1, stride_axis=0)
            tbl[hh, dr] = jnp.where(valid, jnp.where(lane < GRID_W, lo, hi), NEG_INF)
        tbl[hh, N_DROW] = jnp.full((GRID_W, 2 * GRID_W), NEG_INF, F32)


NBR_KEYS = NBR_K + PAST_LEN
KEY_TILE = 256
SOFTMAX_ROWS = 64


def _nbr_block_index(a):
    ks = jnp.clip(Q_ROWS_PER_BLOCK * a - Q_ROWS_PER_BLOCK, 0, GRID_H - NBR_KROWS)
    idx = []
    for ri in range(Q_ROWS_PER_BLOCK):
        r = Q_ROWS_PER_BLOCK * a + ri
        rs = jnp.clip(r - WIN_ROWS // 2, 0, GRID_H - WIN_ROWS)
        row = []
        for kri in range(NBR_KROWS):
            kr = ks + kri
            valid = jnp.logical_and(kr >= rs, kr < rs + WIN_ROWS)
            row.append(jnp.where(valid, kr - r + WIN_ROWS - 1, N_DROW))
        idx.append(row)
    return idx


def _nbr_bias_tile(tbl, hh, idx, t):
    half0 = lax.broadcasted_iota(jnp.int32, (1, 2 * GRID_W), 1) < GRID_W
    krows = KEY_TILE // GRID_W
    rows = []
    for ri in range(Q_ROWS_PER_BLOCK):
        tiles = [jnp.where(half0, tbl[hh, idx[ri][kri]], tbl[hh, idx[ri][kri + 1]])
                 for kri in range(krows * t, krows * (t + 1), 2)]
        rows.append(jnp.concatenate(tiles, axis=1))
    return jnp.concatenate(rows, axis=0)


def _nbr_attn_kernel(q_ref, k_ref, v_ref, kc_ref, vc_ref, band_ref, za_ref, o_ref,
                     tbl, kcat, vcat, s_scr, p_scr, o_scr):
    a = pl.program_id(1)

    @pl.when(a == 0)
    def _():
        _nbr_build_table(band_ref, tbl)

    start = pl.multiple_of(jnp.clip(a * NBR_Q - NBR_Q, 0, DEC_SEQ - NBR_K), NBR_Q)
    head = _head_of_lane()
    idx = _nbr_block_index(a)
    for b in range(DEC_BATCH):
        kcat[b, 0:NBR_K, :] = k_ref[b, pl.ds(start, NBR_K), :]
        kcat[b, NBR_K:NBR_KEYS, :] = kc_ref[b].astype(BF16)
        vcat[b, 0:NBR_K, :] = v_ref[b, pl.ds(start, NBR_K), :]
        vcat[b, NBR_K:NBR_KEYS, :] = vc_ref[b].astype(BF16)

    def scores(b, hh):
        q = q_ref[b]
        qh = jnp.where(head == hh, q, jnp.zeros_like(q))
        mrun = None
        for t in range(NBR_KEYS // KEY_TILE):
            st = _dot_nt(qh, kcat[b, t * KEY_TILE:(t + 1) * KEY_TILE, :])
            if t < NBR_K // KEY_TILE:
                st = st + _nbr_bias_tile(tbl, hh, idx, t)
            s_scr[hh, :, t * KEY_TILE:(t + 1) * KEY_TILE] = st
            mt = jnp.maximum(st[:, :128], st[:, 128:])
            mrun = mt if mrun is None else jnp.maximum(mrun, mt)
        return mrun.max(axis=-1, keepdims=True)

    def attend(b, hh, m):
        ls = []
        for r0 in range(0, NBR_Q, SOFTMAX_ROWS):
            mr = m[r0:r0 + SOFTMAX_ROWS]
            lrun = jnp.zeros((SOFTMAX_ROWS, 128), F32)
            for c0 in range(0, NBR_KEYS, 128):
                p = jnp.exp(s_scr[hh, r0:r0 + SOFTMAX_ROWS, c0:c0 + 128] - mr)
                lrun = lrun + p
                p_scr[hh, r0:r0 + SOFTMAX_ROWS, c0:c0 + 128] = p.astype(BF16)
            ls.append(lrun.sum(axis=-1, keepdims=True))
        l = jnp.concatenate(ls, axis=0)
        o = jnp.dot(p_scr[hh], vcat[b], preferred_element_type=F32)
        lanes = slice(hh * HEAD_DIM, (hh + 1) * HEAD_DIM)
        o_scr[b, :, lanes] = o[:, lanes] / l

    order = [(b, hh) for b in range(DEC_BATCH) for hh in range(HEADS_PER_STEP)]
    m_cur = scores(*order[0])
    for n, (b, hh) in enumerate(order):
        m_next = scores(*order[n + 1]) if n + 1 < len(order) else None
        attend(b, hh, m_cur)
        m_cur = m_next
    for b in range(DEC_BATCH):
        o_ref[b] = (o_scr[b] * _silu(za_ref[b].astype(F32))).astype(BF16)


def _nbr_attention(qkv4, kc, vc, band, gz4):
    first = N_CTX_TOK // DEC_SEQ // DEC_BATCH
    nblk = GRID_H // Q_ROWS_PER_BLOCK
    hw = HEAD_LANES

    def qspec(col0):
        return pl.BlockSpec((DEC_BATCH, NBR_Q, hw), lambda hp, a: (first, a, col0 + hp))

    def kspec(col0):
        return pl.BlockSpec((DEC_BATCH, DEC_SEQ, hw), lambda hp, a: (first, 0, col0 + hp))

    cspec = pl.BlockSpec((DEC_BATCH, PAST_LEN, hw), lambda hp, a: (0, 0, hp))
    bspec = pl.BlockSpec((HEADS_PER_STEP, N_DROW + 1, 2 * GRID_W), lambda hp, a: (hp, 0, 0))
    return pl.pallas_call(
        _nbr_attn_kernel,
        grid=(N_HEADS // HEADS_PER_STEP, nblk),
        in_specs=[qspec(QKV_Q), kspec(QKV_K), kspec(QKV_V), cspec, cspec, bspec, qspec(GZ_ZA)],
        out_specs=pl.BlockSpec((DEC_BATCH, NBR_Q, hw), lambda hp, a: (0, a, hp)),
        out_shape=jax.ShapeDtypeStruct((DEC_BATCH, DEC_SEQ, ATT_WIDTH), BF16),
        scratch_shapes=[pltpu.VMEM((HEADS_PER_STEP, N_DROW + 1, GRID_W, 2 * GRID_W), F32),
                        pltpu.VMEM((DEC_BATCH, NBR_KEYS, HEAD_LANES), BF16),
                        pltpu.VMEM((DEC_BATCH, NBR_KEYS, HEAD_LANES), BF16),
                        pltpu.VMEM((HEADS_PER_STEP, NBR_Q, NBR_KEYS), F32),
                        pltpu.VMEM((HEADS_PER_STEP, NBR_Q, NBR_KEYS), BF16),
                        pltpu.VMEM((DEC_BATCH, NBR_Q, HEAD_LANES), F32)],
        compiler_params=_params("arbitrary", "arbitrary"),
        name="nbr_attention",
    )(qkv4, qkv4, qkv4, kc, vc, band, gz4)


def _nbr_bias_band(rpb):
    ncol = 2 * WIN_COLS - 1
    fill = jnp.full(rpb.shape[:2] + (2 * GRID_W - ncol,), NEG_INF, F32)
    band = jnp.concatenate([rpb[..., WIN_COLS - 1:], fill, rpb[..., :WIN_COLS - 1]], axis=-1)
    return jnp.pad(band, ((0, 0), (0, 1), (0, 0)))


BACK_TM = CHUNK * CH_PER_SEG
BACK_CTX_STEPS = N_CTX_TOK // BACK_TM
BACK_STEPS_PER_DEC_SEQ = DEC_SEQ // BACK_TM


def _back_kernel(xp_ref, xs_ref, y_ref, zs_ref, ac_ref, ad_ref, gs_ref, ga_ref, gate_ref, bglu_ref,
                 wglu_ref, wso_ref, wao_ref, wo_ref, op_ref, os_ref):
    i = pl.program_id(0)

    def compute(x, a2):
        ys = _gelu_tanh(y_ref[0].reshape(BACK_TM, SSM_WIDTH))
        t = jnp.dot(ys.astype(BF16), wglu_ref[...], preferred_element_type=F32) + bglu_ref[...]
        ys = ys * _sigmoid(t) * _silu(zs_ref[...].astype(F32))
        p_s = jnp.dot(ys.astype(BF16), wso_ref[...], preferred_element_type=F32)
        p_a = jnp.dot(a2, wao_ref[...], preferred_element_type=F32)
        merged = (_sigmoid(gs_ref[...].astype(F32)) * p_s + _sigmoid(ga_ref[...].astype(F32)) * p_a)
        return x + gate_ref[0] * jnp.dot(merged.astype(BF16), wo_ref[...], preferred_element_type=F32)

    @pl.when(i < BACK_CTX_STEPS)
    def _():
        op_ref[...] = compute(xp_ref[...], ac_ref[...])

    @pl.when(i >= BACK_CTX_STEPS)
    def _():
        os_ref[...] = compute(xs_ref[...], ad_ref[...])


def _back(xp, xs, y, zs, a_ctx, a_dec, gz, gate3, b_glu, w_glu, w_so, w_ao, w_o):
    steps = N_TOK // BACK_TM
    n0 = BACK_CTX_STEPS
    row = functools.partial(_mod_row, ctx_steps=n0, steps_per_seq=BACK_STEPS_PER_DEC_SEQ)
    lo = lambda i: (jnp.minimum(i, n0 - 1), 0)
    hi = lambda i: (jnp.maximum(i - n0, 0), 0)
    cur = lambda i: (i, 0)
    const = lambda i: (0, 0)
    once = pl.Buffered(1)
    return pl.pallas_call(
        _back_kernel,
        grid=(steps,),
        in_specs=[pl.BlockSpec((BACK_TM, D_MODEL), lo),
                  pl.BlockSpec((BACK_TM, D_MODEL), hi),
                  pl.BlockSpec((1, CH_PER_SEG, None, CHUNK, SSM_WIDTH),
                               lambda i: (i // N_SEG, 0, i % N_SEG, 0, 0)),
                  pl.BlockSpec((BACK_TM, SSM_WIDTH), cur),
                  pl.BlockSpec((BACK_TM, ATT_WIDTH), lo),
                  pl.BlockSpec((BACK_TM, ATT_WIDTH), hi),
                  pl.BlockSpec((BACK_TM, D_MODEL), lambda i: (i, 0)),
                  pl.BlockSpec((BACK_TM, D_MODEL), lambda i: (i, 1)),
                  pl.BlockSpec((1, 1, D_MODEL), lambda i: (row(i), 0, 0)),
                  pl.BlockSpec((1, SSM_WIDTH), const),
                  pl.BlockSpec((SSM_WIDTH, SSM_WIDTH), const, pipeline_mode=once),
                  pl.BlockSpec((SSM_WIDTH, D_MODEL), const, pipeline_mode=once),
                  pl.BlockSpec((ATT_WIDTH, D_MODEL), const, pipeline_mode=once),
                  pl.BlockSpec((D_MODEL, D_MODEL), const, pipeline_mode=once)],
        out_specs=[pl.BlockSpec((BACK_TM, D_MODEL), lo),
                   pl.BlockSpec((BACK_TM, D_MODEL), hi)],
        out_shape=[jax.ShapeDtypeStruct((N_CTX_TOK, D_MODEL), F32),
                   jax.ShapeDtypeStruct((N_DEC_TOK, D_MODEL), F32)],
        compiler_params=_params("arbitrary"),
        name="gated_output",
    )(xp, xs, y, zs, a_ctx, a_dec, gz, gz, gate3, b_glu.reshape(1, SSM_WIDTH), w_glu, w_so, w_ao, w_o)


def _layer(xp, xs, cache_k, cache_v, st_re, st_im, c, c_ctx, norm_w, w_ada, b_ada, w_in, q_norm_w, k_norm_w,
           rpb, a_re, a_im, log_dt, b_re, b_im, c_re, c_im, d, w_glu, b_glu, w_so, w_ao, w_o):
    G, P = SSM_GROUPS, SSM_STATE

    cond8 = jnp.zeros((8, D_MODEL), F32).at[0].set(c_ctx).at[1:1 + DEC_BATCH].set(c)
    mod = _modulation(cond8, w_ada, b_ada)
    shift3 = mod[:1 + DEC_BATCH, None, :D_MODEL]
    scale3 = mod[:1 + DEC_BATCH, None, D_MODEL:2 * D_MODEL]
    gate3 = mod[:1 + DEC_BATCH, None, 2 * D_MODEL:]

    h = _norm_modulate(xp, xs, shift3, scale3, norm_w)
    u, zs, qkv, kv32, gz = _in_proj(h, w_in, q_norm_w, k_norm_w)

    arow = jnp.transpose(a_re, (1, 0, 2)).reshape(G, 2 * P)
    airow = jnp.transpose(a_im, (1, 0, 2)).reshape(G, 2 * P)
    dtrow = jnp.broadcast_to(jnp.transpose(log_dt, (1, 0))[:, :, None], (G, 2, P)).reshape(G, 2 * P)
    rowp = jnp.stack([arow, airow, dtrow], axis=1)
    btr = jnp.transpose(b_re, (1, 3, 0, 2)).reshape(G, SSM_GROUP, 2 * P)
    bti = jnp.transpose(b_im, (1, 3, 0, 2)).reshape(G, SSM_GROUP, 2 * P)
    ctr = jnp.transpose(c_re, (1, 2, 0, 3)).reshape(G, SSM_GROUP, 2 * P)
    cti = jnp.transpose(c_im, (1, 2, 0, 3)).reshape(G, SSM_GROUP, 2 * P)
    t_op, s_op, r_op, lam, powc = _ssm_operators(rowp, btr, bti, ctr, cti)

    dtile = d.reshape(1, SSM_WIDTH)
    h0c = jnp.zeros((G, BATCH, 2 * P), F32)
    h0d_re = jnp.transpose(st_re, (2, 0, 1, 3)).reshape(G, DEC_BATCH, 2 * P)
    h0d_im = jnp.transpose(st_im, (2, 0, 1, 3)).reshape(G, DEC_BATCH, 2 * P)
    y2d, fin = _ssm_scan(u.reshape(N_TOK, SSM_WIDTH), t_op, s_op, r_op, lam, powc, dtile,
                         h0c, h0c, h0d_re, h0d_im)
    y = y2d.reshape(2, CH_PER_SEG, N_SEG, CHUNK, SSM_WIDTH)

    a_ctx = _ctx_attention(qkv, gz)
    nseq4 = N_TOK // DEC_SEQ
    a_dec = _nbr_attention(qkv.reshape(nseq4, DEC_SEQ, qkv.shape[1]),
                           cache_k.reshape(DEC_BATCH, PAST_LEN, ATT_WIDTH),
                           cache_v.reshape(DEC_BATCH, PAST_LEN, ATT_WIDTH),
                           _nbr_bias_band(rpb.astype(F32)),
                           gz.reshape(nseq4, DEC_SEQ, gz.shape[1]))

    yp, ys_out = _back(xp, xs, y, zs, a_ctx, a_dec.reshape(N_DEC_TOK, ATT_WIDTH), gz, gate3, b_glu,
                       w_glu.astype(BF16), w_so.astype(BF16), w_ao.astype(BF16), w_o.astype(BF16))

    fin_ctx = fin[0]
    new_re = jnp.transpose(fin_ctx[:, :, :2 * P].reshape(G, BATCH, 2, P), (1, 2, 0, 3))
    new_im = jnp.transpose(fin_ctx[:, :, 2 * P:].reshape(G, BATCH, 2, P), (1, 2, 0, 3))
    return yp, ys_out, kv32[0], kv32[1], new_re, new_im


def kernel(x_prompt, x_sample, cache_k, cache_v, state_ssm_re, state_ssm_im, c, c_ctx, norm_w, w_ada, b_ada,
           w_in, q_norm_w, k_norm_w, rel_pos_bias, ssm_a_re, ssm_a_im, ssm_log_dt, ssm_b_re, ssm_b_im,
           ssm_c_re, ssm_c_im, ssm_d, w_glu, b_glu, w_ssm_out, w_att_out, w_o):
    depth = norm_w.shape[0]
    xp = x_prompt.reshape(N_CTX_TOK, D_MODEL)
    xs = x_sample.reshape(N_DEC_TOK, D_MODEL)
    new_k, new_v, new_re, new_im = [], [], [], []
    for l in range(depth):
        xp, xs, kl, vl, rl, il = _layer(
            xp, xs, cache_k[:, l], cache_v[:, l], state_ssm_re[:, l], state_ssm_im[:, l], c, c_ctx,
            norm_w[l], w_ada[l], b_ada[l], w_in[l], q_norm_w[l], k_norm_w[l], rel_pos_bias[l],
            ssm_a_re[l], ssm_a_im[l], ssm_log_dt[l], ssm_b_re[l], ssm_b_im[l], ssm_c_re[l], ssm_c_im[l],
            ssm_d[l], w_glu[l], b_glu[l], w_ssm_out[l], w_att_out[l], w_o[l])
        new_k.append(kl.reshape(BATCH, SEQ, N_HEADS, HEAD_DIM))
        new_v.append(vl.reshape(BATCH, SEQ, N_HEADS, HEAD_DIM))
        new_re.append(rl)
        new_im.append(il)
    return (xp.reshape(BATCH, SEQ, D_MODEL), xs.reshape(DEC_BATCH, DEC_SEQ, D_MODEL),
            jnp.stack(new_k, axis=1), jnp.stack(new_v, axis=1),
            jnp.stack(new_re, axis=1), jnp.stack(new_im, axis=1))
```

```python
import functools
import math

import jax
import jax.numpy as jnp
from jax import lax
from jax.experimental import pallas as pl
from jax.experimental.pallas import tpu as pltpu

D_MODEL = 2048
BATCH = 16
SEQ = 256
DEC_BATCH = 2
DEC_SEQ = 2048
PAST_LEN = 512
GRID_W = 64
GRID_H = DEC_SEQ // GRID_W
SSM_WIDTH = D_MODEL // 2
SSM_GROUP = 16
SSM_GROUPS = SSM_WIDTH // SSM_GROUP
SSM_STATE = 64
N_HEADS = 16
HEAD_DIM = 64
ATT_WIDTH = N_HEADS * HEAD_DIM
WIN_ROWS = 8
WIN_COLS = 16
EPS = 1e-6
NEG_INF = -1e30

N_CTX_TOK = BATCH * SEQ
N_DEC_TOK = DEC_BATCH * DEC_SEQ
N_TOK = N_CTX_TOK + N_DEC_TOK

OFF_U = 0
OFF_ZS = SSM_WIDTH
OFF_Q = 2 * SSM_WIDTH
OFF_K = OFF_Q + ATT_WIDTH
OFF_V = OFF_K + ATT_WIDTH
OFF_ZA = OFF_V + ATT_WIDTH
OFF_GS = OFF_ZA + ATT_WIDTH
OFF_GA = OFF_GS + D_MODEL

CHUNK = 16
FLAT = CHUNK * SSM_GROUP
N_SEG = 16
CH_PER_SEG = 16
DSTATE = 2 * SSM_STATE

Q_ROWS_PER_BLOCK = 4
NBR_Q = Q_ROWS_PER_BLOCK * GRID_W
NBR_KROWS = 12
NBR_K = NBR_KROWS * GRID_W

VMEM_LIMIT = 56 * 1024 * 1024
QKV_VMEM_LIMIT = 60 * 1024 * 1024

F32 = jnp.float32
BF16 = jnp.bfloat16
HIGHEST = lax.Precision.HIGHEST


def _sigmoid(x):
    return 1.0 / (1.0 + jnp.exp(-x))


def _silu(x):
    return x * _sigmoid(x)


def _gelu_tanh(x):
    return 0.5 * x * (1.0 + jnp.tanh(math.sqrt(2.0 / math.pi) * (x + 0.044715 * (x * x * x))))


def _cmul(ar, ai, br, bi):
    return ar * br - ai * bi, ar * bi + ai * br


def _dot_nt(a, b):
    return lax.dot_general(a, b, (((1,), (1,)), ((), ())), preferred_element_type=F32)


def _params(*sem, vmem_limit=VMEM_LIMIT):
    return pltpu.CompilerParams(dimension_semantics=sem, vmem_limit_bytes=vmem_limit)


def _mod_kernel(cond_ref, w_ref, b_ref, o_ref):
    c = cond_ref[...]
    s = _silu(c).astype(BF16)
    o_ref[...] = jnp.dot(s, w_ref[...].astype(BF16), preferred_element_type=F32) + b_ref[...]


def _modulation(cond8, w_ada, b_ada):
    tn = 768
    n = w_ada.shape[1]
    return pl.pallas_call(
        _mod_kernel,
        grid=(n // tn,),
        in_specs=[pl.BlockSpec((8, D_MODEL), lambda j: (0, 0)),
                  pl.BlockSpec((D_MODEL, tn), lambda j: (0, j)),
                  pl.BlockSpec((1, tn), lambda j: (0, j))],
        out_specs=pl.BlockSpec((8, tn), lambda j: (0, j)),
        out_shape=jax.ShapeDtypeStruct((8, n), F32),
        compiler_params=_params("arbitrary"),
        name="modulation",
    )(cond8, w_ada, b_ada.reshape(1, n))


NORM_TM = 512
NORM_ROWS = 16
NORM_UNROLL = 8
NORM_CTX_STEPS = N_CTX_TOK // NORM_TM
NORM_STEPS_PER_DEC_SEQ = DEC_SEQ // NORM_TM


def _mod_row(i, ctx_steps, steps_per_seq):
    return jnp.where(i < ctx_steps, 0, 1 + (i - ctx_steps) // steps_per_seq)


def _norm_kernel(xp_ref, xs_ref, shift_ref, scale_ref, nw_ref, o_ref):
    i = pl.program_id(0)

    gain = nw_ref[...] * (1.0 + scale_ref[0])
    shift = shift_ref[0]

    def body(x_ref):
        def rows(r, carry):
            sl = pl.ds(pl.multiple_of(r * NORM_ROWS, NORM_ROWS), NORM_ROWS)
            x = x_ref[sl, :]
            ms = jnp.mean(x * x, axis=-1, keepdims=True)
            o_ref[sl, :] = (x * lax.rsqrt(ms + EPS) * gain + shift).astype(BF16)
            return carry

        lax.fori_loop(0, NORM_TM // NORM_ROWS, rows, 0, unroll=NORM_UNROLL)

    @pl.when(i < NORM_CTX_STEPS)
    def _():
        body(xp_ref)

    @pl.when(i >= NORM_CTX_STEPS)
    def _():
        body(xs_ref)


def _norm_modulate(xp, xs, shift3, scale3, norm_w):
    steps = N_TOK // NORM_TM
    row = functools.partial(_mod_row, ctx_steps=NORM_CTX_STEPS, steps_per_seq=NORM_STEPS_PER_DEC_SEQ)
    return pl.pallas_call(
        _norm_kernel,
        grid=(steps,),
        in_specs=[pl.BlockSpec((NORM_TM, D_MODEL), lambda i: (jnp.minimum(i, NORM_CTX_STEPS - 1), 0)),
                  pl.BlockSpec((NORM_TM, D_MODEL), lambda i: (jnp.maximum(i - NORM_CTX_STEPS, 0), 0)),
                  pl.BlockSpec((1, 1, D_MODEL), lambda i: (row(i), 0, 0)),
                  pl.BlockSpec((1, 1, D_MODEL), lambda i: (row(i), 0, 0)),
                  pl.BlockSpec((1, D_MODEL), lambda i: (0, 0))],
        out_specs=pl.BlockSpec((NORM_TM, D_MODEL), lambda i: (i, 0)),
        out_shape=jax.ShapeDtypeStruct((N_TOK, D_MODEL), BF16),
        compiler_params=_params("arbitrary"),
        name="norm_modulate",
    )(xp, xs, shift3, scale3, norm_w.reshape(1, D_MODEL))


PROJ_TM = 1024
PROJ_TN = 1024
PROJ_CTX_STEPS = N_CTX_TOK // PROJ_TM
PROJ_SEGS = PROJ_TM // (CHUNK * CH_PER_SEG)


def _head_group_ones():
    r = lax.broadcasted_iota(jnp.int32, (256, 256), 0) // HEAD_DIM
    c = lax.broadcasted_iota(jnp.int32, (256, 256), 1) // HEAD_DIM
    return jnp.where(r == c, 1.0, 0.0).astype(BF16)


def _head_rms(acc, nw):
    ones = _head_group_ones()
    outs = []
    for c in range(acc.shape[1] // 256):
        a = acc[:, c * 256:(c + 1) * 256]
        ssum = jnp.dot((a * a).astype(BF16), ones, preferred_element_type=F32)
        outs.append(a * lax.rsqrt(ssum * (1.0 / HEAD_DIM) + EPS))
    return jnp.concatenate(outs, axis=1) * nw


def _proj_tile(h_ref, w_ref, wbf):
    @pl.when(pl.program_id(1) == 0)
    def _():
        wbf[...] = w_ref[...].astype(BF16)

    return jnp.dot(h_ref[...], wbf[...], preferred_element_type=F32)


def _proj_s5_kernel(h_ref, w_ref, u_ref, zs_ref, wbf):
    j = pl.program_id(0)
    acc = _proj_tile(h_ref, w_ref, wbf)

    @pl.when(j == 0)
    def _():
        for seg in range(PROJ_SEGS):
            for ch in range(CH_PER_SEG):
                r0 = (seg * CH_PER_SEG + ch) * CHUNK
                u_ref[0, ch, seg] = acc[r0:r0 + CHUNK, :]

    @pl.when(j == 1)
    def _():
        zs_ref[...] = acc.astype(BF16)


def _proj_qkv_kernel(h_ref, w_ref, nw_ref, o_ref, kf_ref, vf_ref, wbf):
    j = pl.program_id(0)
    i = pl.program_id(1)
    acc = _proj_tile(h_ref, w_ref, wbf)

    @pl.when(j == 0)
    def _():
        o_ref[...] = (_head_rms(acc, nw_ref[0]) * (HEAD_DIM ** -0.5)).astype(BF16)

    @pl.when(j == 1)
    def _():
        kn = _head_rms(acc, nw_ref[0])
        o_ref[...] = kn.astype(BF16)

        @pl.when(i < PROJ_CTX_STEPS)
        def _():
            kf_ref[...] = kn

    @pl.when(j == 2)
    def _():
        o_ref[...] = acc.astype(BF16)

        @pl.when(i < PROJ_CTX_STEPS)
        def _():
            vf_ref[...] = acc


def _proj_gate_kernel(h_ref, w_ref, o_ref, wbf):
    o_ref[...] = _proj_tile(h_ref, w_ref, wbf).astype(BF16)


def _in_proj(h, w_in, q_norm_w, k_norm_w):
    ni = N_TOK // PROJ_TM
    last = ni - 1
    ctx_last = PROJ_CTX_STEPS - 1
    tiles_per_path = N_SEG // PROJ_SEGS
    h_spec = pl.BlockSpec((PROJ_TM, D_MODEL), lambda j, i: (i, 0))
    scratch = [pltpu.VMEM((D_MODEL, PROJ_TN), BF16)]
    params = _params("arbitrary", "arbitrary")

    def u_map(j, i):
        ii = jnp.where(j == 0, i, last)
        return (ii // tiles_per_path, 0, ii % tiles_per_path, 0, 0)

    u, zs = pl.pallas_call(
        _proj_s5_kernel,
        grid=(2, ni),
        in_specs=[h_spec, pl.BlockSpec((D_MODEL, PROJ_TN), lambda j, i: (0, OFF_U // PROJ_TN + j))],
        out_specs=[pl.BlockSpec((1, CH_PER_SEG, PROJ_SEGS, CHUNK, PROJ_TN), u_map),
                   pl.BlockSpec((PROJ_TM, PROJ_TN), lambda j, i: (jnp.where(j == 0, 0, i), 0))],
        out_shape=[jax.ShapeDtypeStruct((2, CH_PER_SEG, N_SEG, CHUNK, SSM_WIDTH), F32),
                   jax.ShapeDtypeStruct((N_TOK, SSM_WIDTH), BF16)],
        scratch_shapes=scratch, compiler_params=params, name="in_proj_s5",
    )(h, w_in)

    nw = jnp.stack([jnp.tile(q_norm_w.reshape(1, HEAD_DIM), (1, PROJ_TN // HEAD_DIM)),
                    jnp.tile(k_norm_w.reshape(1, HEAD_DIM), (1, PROJ_TN // HEAD_DIM))])
    qkv, kf32, vf32 = pl.pallas_call(
        _proj_qkv_kernel,
        grid=(3, ni),
        in_specs=[h_spec, pl.BlockSpec((D_MODEL, PROJ_TN), lambda j, i: (0, OFF_Q // PROJ_TN + j)),
                  pl.BlockSpec((1, 1, PROJ_TN), lambda j, i: (jnp.where(j == 1, 1, 0), 0, 0))],
        out_specs=[pl.BlockSpec((PROJ_TM, PROJ_TN), lambda j, i: (i, j)),
                   pl.BlockSpec((PROJ_TM, PROJ_TN),
                                lambda j, i: (jnp.where(j == 0, 0, jnp.where(j == 1, jnp.minimum(i, ctx_last),
                                                                             ctx_last)), 0)),
                   pl.BlockSpec((PROJ_TM, PROJ_TN),
                                lambda j, i: (jnp.where(j == 2, jnp.minimum(i, ctx_last), 0), 0))],
        out_shape=[jax.ShapeDtypeStruct((N_TOK, 3 * ATT_WIDTH), BF16),
                   jax.ShapeDtypeStruct((N_CTX_TOK, ATT_WIDTH), F32),
                   jax.ShapeDtypeStruct((N_CTX_TOK, ATT_WIDTH), F32)],
        scratch_shapes=scratch, name="in_proj_qkv",
        compiler_params=_params("arbitrary", "arbitrary", vmem_limit=QKV_VMEM_LIMIT),
    )(h, w_in, nw)

    gate_tiles = 2 * D_MODEL // PROJ_TN
    gz = pl.pallas_call(
        _proj_gate_kernel,
        grid=(gate_tiles + 1, ni),
        in_specs=[h_spec,
                  pl.BlockSpec((D_MODEL, PROJ_TN),
                               lambda j, i: (0, jnp.where(j < gate_tiles, OFF_GS // PROJ_TN + j,
                                                          OFF_ZA // PROJ_TN)))],
        out_specs=pl.BlockSpec((PROJ_TM, PROJ_TN), lambda j, i: (i, j)),
        out_shape=jax.ShapeDtypeStruct((N_TOK, 2 * D_MODEL + ATT_WIDTH), BF16),
        scratch_shapes=scratch, compiler_params=params, name="in_proj_gates",
    )(h, w_in)
    return u, zs, qkv, kf32, vf32, gz


SSM_GB = 4


def _pow_select(e, pows):
    rr = jnp.where((e & 1) != 0, pows[0][0], 1.0)
    ri = jnp.where((e & 1) != 0, pows[0][1], 0.0)
    for b in range(1, len(pows)):
        bit = (e & (1 << b)) != 0
        fr = jnp.where(bit, pows[b][0], 1.0)
        fi = jnp.where(bit, pows[b][1], 0.0)
        rr, ri = _cmul(rr, ri, fr, fi)
    return rr, ri


def _discretize(a_re, a_im, log_dt):
    lr = jnp.minimum(a_re, -1e-4)
    li = a_im
    dt = jnp.exp(log_dt)
    mag = jnp.exp(lr * dt)
    br = mag * jnp.cos(li * dt)
    bi = mag * jnp.sin(li * dt)
    den = lr * lr + li * li
    nr = br - 1.0
    cr = (nr * lr + bi * li) / den
    ci = (bi * lr - nr * li) / den
    return (br, bi), (cr, ci)


def _squarings(pr, pi, n):
    out = [(pr, pi)]
    for _ in range(n):
        pr, pi = _cmul(pr, pi, pr, pi)
        out.append((pr, pi))
    return out


def _dot_nt_exact(a, b):
    return lax.dot_general(a, b, (((1,), (1,)), ((), ())), preferred_element_type=F32, precision=HIGHEST)


def _ssm_ops_kernel(rowp_ref, btr_ref, bti_ref, ctr_ref, cti_ref, t_ref, s_ref, rt_ref, lam_ref, powc_ref):
    lane_b = lax.broadcasted_iota(jnp.int32, (SSM_GROUP, DSTATE), 1)
    lane_t = lax.broadcasted_iota(jnp.int32, (SSM_GROUP, FLAT), 1)
    row_c = lax.broadcasted_iota(jnp.int32, (CH_PER_SEG, DSTATE), 0)
    lane_c = lax.broadcasted_iota(jnp.int32, (CH_PER_SEG, DSTATE), 1)
    exp_c = jnp.where(lane_c < SSM_STATE, row_c, (CH_PER_SEG - 1) - row_c)

    for g in range(SSM_GB):
        rp = rowp_ref[g]
        (lbr, lbi), (cfr, cfi) = _discretize(rp[0:1], rp[1:2], rp[2:3])
        sq = _squarings(lbr, lbi, 8)
        bbr, bbi = _cmul(btr_ref[g], bti_ref[g], cfr, cfi)

        ptr, pti = _pow_select(exp_c, sq[:4])

        s_re, s_im = [], []
        for sp in range(CHUNK):
            e = CHUNK - 1 - sp
            br, bi = _cmul(bbr, bbi, ptr[e:e + 1], pti[e:e + 1])
            s_re.append(br)
            s_im.append(bi)
        s_ref[g] = jnp.concatenate([jnp.concatenate(s_re, axis=0), jnp.concatenate(s_im, axis=0)],
                                   axis=1).astype(BF16)

        xcr = jnp.concatenate([ctr_ref[g]] * CHUNK, axis=0)
        xci = jnp.concatenate([cti_ref[g]] * CHUNK, axis=0)
        xpr = jnp.broadcast_to(ptr[:, None, :], (CHUNK, SSM_GROUP, DSTATE)).reshape(FLAT, DSTATE)
        xpi = jnp.broadcast_to(pti[:, None, :], (CHUNK, SSM_GROUP, DSTATE)).reshape(FLAT, DSTATE)
        ykr, yki = _cmul(xcr, xci, xpr, xpi)
        yrr, yri = _cmul(ykr, yki, lbr, lbi)
        rt_ref[g] = jnp.concatenate([yrr, -yri], axis=1).astype(BF16)

        fwd = lane_b < SSM_STATE
        lhs = jnp.concatenate([jnp.where(fwd, bbr, 0.0), jnp.where(fwd, bbi, 0.0),
                               jnp.where(fwd, 0.0, bbr), jnp.where(fwd, 0.0, bbi)], axis=0)
        p1 = _dot_nt_exact(lhs, ykr)
        p2 = _dot_nt_exact(lhs, yki)
        kf = p1[0:16] - p2[16:32]
        kb = p1[32:48] - p2[48:64]
        for sp in range(CHUNK):
            tf = kf if sp == 0 else pltpu.roll(kf, SSM_GROUP * sp, axis=1)
            tf = jnp.where(lane_t >= SSM_GROUP * sp, tf, 0.0)
            shift = (FLAT - SSM_GROUP * (CHUNK - 1 - sp)) % FLAT
            tb = kb if shift == 0 else pltpu.roll(kb, shift, axis=1)
            tb = jnp.where(lane_t < SSM_GROUP * (sp + 1), tb, 0.0)
            t_ref[g, sp * SSM_GROUP:(sp + 1) * SSM_GROUP, :] = (tf + tb).astype(BF16)

        l16 = sq[4]
        l256 = sq[8]
        lam_ref[g] = jnp.concatenate([l16[0], l16[1], l256[0], l256[1],
                                      jnp.zeros((4, DSTATE), F32)], axis=0)
        q16 = _squarings(l16[0], l16[1], 3)
        pcr, pci = _pow_select(exp_c, q16)
        powc_ref[g, 0] = pcr
        powc_ref[g, 1] = pci


def _ssm_operators(rowp, btr, bti, ctr, cti):
    G = SSM_GROUPS
    gb = SSM_GB
    return pl.pallas_call(
        _ssm_ops_kernel,
        grid=(G // gb,),
        in_specs=[pl.BlockSpec((gb, 3, DSTATE), lambda g: (g, 0, 0)),
                  pl.BlockSpec((gb, SSM_GROUP, DSTATE), lambda g: (g, 0, 0)),
                  pl.BlockSpec((gb, SSM_GROUP, DSTATE), lambda g: (g, 0, 0)),
                  pl.BlockSpec((gb, SSM_GROUP, DSTATE), lambda g: (g, 0, 0)),
                  pl.BlockSpec((gb, SSM_GROUP, DSTATE), lambda g: (g, 0, 0))],
        out_specs=[pl.BlockSpec((gb, FLAT, FLAT), lambda g: (g, 0, 0)),
                   pl.BlockSpec((gb, FLAT, FLAT), lambda g: (g, 0, 0)),
                   pl.BlockSpec((gb, FLAT, FLAT), lambda g: (g, 0, 0)),
                   pl.BlockSpec((gb, 8, DSTATE), lambda g: (g, 0, 0)),
                   pl.BlockSpec((gb, 2, CH_PER_SEG, DSTATE), lambda g: (g, 0, 0, 0))],
        out_shape=[jax.ShapeDtypeStruct((G, FLAT, FLAT), BF16),
                   jax.ShapeDtypeStruct((G, FLAT, FLAT), BF16),
                   jax.ShapeDtypeStruct((G, FLAT, FLAT), BF16),
                   jax.ShapeDtypeStruct((G, 8, DSTATE), F32),
                   jax.ShapeDtypeStruct((G, 2, CH_PER_SEG, DSTATE), F32)],
        compiler_params=_params("arbitrary"),
        name="ssm_operators",
    )(rowp, btr, bti, ctr, cti)


SSM_PATHS = ((BATCH, SEQ // (CHUNK * CH_PER_SEG)), (DEC_BATCH, DEC_SEQ // (CHUNK * CH_PER_SEG)))


def _ssm_path(u, s_op, t_op, rt_op, lam, powc_ref, h0r, h0i, nseg):
    lane = lax.broadcasted_iota(jnp.int32, (N_SEG, DSTATE), 1)
    isf = lane < SSM_STATE
    l16r, l16i, l256r, l256i = lam[0:1], lam[1:2], lam[2:3], lam[3:4]

    z = jnp.dot(u, s_op, preferred_element_type=F32)
    zre, zim = z[:, :DSTATE], z[:, DSTATE:]

    hr = jnp.zeros((N_SEG, DSTATE), F32)
    hi = jnp.zeros((N_SEG, DSTATE), F32)
    hist = []
    for k in range(CH_PER_SEG):
        hist.append((hr, hi))
        kb = CH_PER_SEG - 1 - k
        zr = jnp.where(isf, zre[k * N_SEG:(k + 1) * N_SEG], zre[kb * N_SEG:(kb + 1) * N_SEG])
        zi = jnp.where(isf, zim[k * N_SEG:(k + 1) * N_SEG], zim[kb * N_SEG:(kb + 1) * N_SEG])
        nr, ni = _cmul(l16r, l16i, hr, hi)
        hr, hi = nr + zr, ni + zi

    if nseg > 1:
        h0r = jnp.broadcast_to(h0r[:, None, :], (N_SEG // nseg, nseg, DSTATE)).reshape(N_SEG, DSTATE)
        h0i = jnp.broadcast_to(h0i[:, None, :], (N_SEG // nseg, nseg, DSTATE)).reshape(N_SEG, DSTATE)
    seg = lax.broadcasted_iota(jnp.int32, (N_SEG, DSTATE), 0) & (nseg - 1)
    segpow = _squarings(l256r, l256i, max(int(math.log2(nseg)), 0))

    def shifted(x, d):
        dn = jnp.where(seg >= d, pltpu.roll(x, d, axis=0), 0.0)
        up = jnp.where(seg <= nseg - 1 - d, pltpu.roll(x, N_SEG - d, axis=0), 0.0)
        return jnp.where(isf, dn, up)

    pr, pi = hr, hi
    d = 1
    lvl = 0
    while d < nseg:
        ar, ai = _cmul(segpow[lvl][0], segpow[lvl][1], shifted(pr, d), shifted(pi, d))
        pr, pi = pr + ar, pi + ai
        d *= 2
        lvl += 1
    if nseg > 1:
        e_in = jnp.where(isf, seg, nseg - 1 - seg)
        wr, wi = _pow_select(e_in, segpow[:lvl])
        ar, ai = _cmul(wr, wi, h0r, h0i)
        hsr, hsi = shifted(pr, 1) + ar, shifted(pi, 1) + ai
    else:
        hsr, hsi = h0r, h0i
    er, ei = _cmul(segpow[lvl][0], segpow[lvl][1], h0r, h0i)
    fin = jnp.concatenate([pr + er, pi + ei], axis=1)

    rows = []
    for c in range(CH_PER_SEG):
        cb = CH_PER_SEG - 1 - c
        lr = jnp.where(isf, hist[c][0], hist[cb][0])
        li = jnp.where(isf, hist[c][1], hist[cb][1])
        ar, ai = _cmul(powc_ref[0, c:c + 1, :], powc_ref[1, c:c + 1, :], hsr, hsi)
        rows.append(jnp.concatenate([lr + ar, li + ai], axis=1))
    hent = jnp.concatenate(rows, axis=0).astype(BF16)

    y = jnp.dot(u, t_op, preferred_element_type=F32) + _dot_nt(hent, rt_op)
    return y, fin


GROUPS_PER_STEP = 128 // SSM_GROUP
GROUPS_PER_ITER = 8


def _block_transpose(arrs, blk):
    n = len(arrs)
    width = arrs[0].shape[1]
    j = lax.broadcasted_iota(jnp.int32, arrs[0].shape, 1) // blk
    k = n // 2
    while k >= 1:
        bit = (j & k) != 0
        new = list(arrs)
        for x in range(n):
            if x & k == 0:
                a, b = arrs[x], arrs[x | k]
                new[x] = jnp.where(bit, pltpu.roll(b, k * blk, axis=1), a)
                new[x | k] = jnp.where(bit, b, pltpu.roll(a, width - k * blk, axis=1))
        arrs = new
        k //= 2
    return arrs


def _ssm_kernel(u_ref, t_ref, s_ref, r_ref, lam_ref, powc_ref, d_ref, h0cr_ref, h0ci_ref, h0dr_ref, h0di_ref,
                y_ref, fin_ref, ubuf, ybuf):
    h0 = ((h0cr_ref, h0ci_ref), (h0dr_ref, h0di_ref))
    nhalf = FLAT // 128
    rows = N_SEG * CH_PER_SEG
    def token_rows(path, tok, r0, nrows):
        return pl.ds(path * rows * CHUNK + r0 * CHUNK + tok, nrows, stride=CHUNK)

    in_rows = rows
    out_rows = rows
    for path, (_, nseg) in enumerate(SSM_PATHS):
        for half in range(nhalf):
            for r0 in range(0, rows, in_rows):
                toks = [pltpu.bitcast(
                    u_ref[token_rows(path, half * GROUPS_PER_STEP + sb, r0, in_rows), :].astype(BF16), jnp.uint32)
                    for sb in range(GROUPS_PER_STEP)]
                grouped = _block_transpose(toks, SSM_GROUP)
                for g in range(GROUPS_PER_STEP):
                    ubuf[g, r0:r0 + in_rows, half * 128:(half + 1) * 128] = pltpu.bitcast(grouped[g], BF16)

        def body(gi, carry, path=path, nseg=nseg):
            for dg in range(GROUPS_PER_ITER):
                g = gi * GROUPS_PER_ITER + dg
                y, fin = _ssm_path(ubuf[g], s_ref[g], t_ref[g], r_ref[g], lam_ref[g], powc_ref.at[g],
                                   h0[path][0][g], h0[path][1][g], nseg)
                ybuf[g] = y
                fin_ref[path, g] = fin
            return carry

        lax.fori_loop(0, GROUPS_PER_STEP // GROUPS_PER_ITER, body, 0)

        for half in range(nhalf):
            for r0 in range(0, rows, out_rows):
                grouped = [ybuf[g, r0:r0 + out_rows, half * 128:(half + 1) * 128] for g in range(GROUPS_PER_STEP)]
                toks = _block_transpose(grouped, SSM_GROUP)
                for sb in range(GROUPS_PER_STEP):
                    sel = token_rows(path, half * GROUPS_PER_STEP + sb, r0, out_rows)
                    y_ref[sel, :] = toks[sb] + u_ref[sel, :] * d_ref[...]


def _ssm_scan(u2d, t_op, s_op, r_op, lam, powc, dtile, h0c_re, h0c_im, h0d_re, h0d_im):
    G = SSM_GROUPS
    gs = GROUPS_PER_STEP
    rows = N_SEG * CH_PER_SEG
    return pl.pallas_call(
        _ssm_kernel,
        grid=(G // gs,),
        in_specs=[
            pl.BlockSpec((N_TOK, 128), lambda o: (0, o)),
            pl.BlockSpec((gs, FLAT, FLAT), lambda o: (o, 0, 0)),
            pl.BlockSpec((gs, FLAT, FLAT), lambda o: (o, 0, 0)),
            pl.BlockSpec((gs, FLAT, FLAT), lambda o: (o, 0, 0)),
            pl.BlockSpec((gs, 8, DSTATE), lambda o: (o, 0, 0)),
            pl.BlockSpec((gs, 2, CH_PER_SEG, DSTATE), lambda o: (o, 0, 0, 0)),
            pl.BlockSpec((1, 128), lambda o: (0, o)),
            pl.BlockSpec((gs, BATCH, DSTATE), lambda o: (o, 0, 0)),
            pl.BlockSpec((gs, BATCH, DSTATE), lambda o: (o, 0, 0)),
            pl.BlockSpec((gs, DEC_BATCH, DSTATE), lambda o: (o, 0, 0)),
            pl.BlockSpec((gs, DEC_BATCH, DSTATE), lambda o: (o, 0, 0))],
        out_specs=[pl.BlockSpec((N_TOK, 128), lambda o: (0, o)),
                   pl.BlockSpec((2, gs, N_SEG, 2 * DSTATE), lambda o: (0, o, 0, 0))],
        out_shape=[jax.ShapeDtypeStruct((N_TOK, SSM_WIDTH), F32),
                   jax.ShapeDtypeStruct((2, G, N_SEG, 2 * DSTATE), F32)],
        scratch_shapes=[pltpu.VMEM((gs, rows, FLAT), BF16), pltpu.VMEM((gs, rows, FLAT), F32)],
        compiler_params=_params("arbitrary"),
        name="ssm_scan",
    )(u2d, t_op, s_op, r_op, lam, powc, dtile, h0c_re, h0c_im, h0d_re, h0d_im)


CTX_BB = 4
HEADS_PER_STEP = 256 // HEAD_DIM
HEAD_LANES = HEADS_PER_STEP * HEAD_DIM


def _softmax_pv(scores, values):
    m = scores[0].max(axis=-1, keepdims=True)
    for s in scores[1:]:
        m = jnp.maximum(m, s.max(axis=-1, keepdims=True))
    l = None
    o = None
    for s, v in zip(scores, values):
        p = jnp.exp(s - m)
        ls = p.sum(axis=-1, keepdims=True)
        os_ = jnp.dot(p.astype(BF16), v, preferred_element_type=F32)
        l = ls if l is None else l + ls
        o = os_ if o is None else o + os_
    return o / l


def _head_of_lane():
    return lax.broadcasted_iota(jnp.int32, (1, HEAD_LANES), 1) // HEAD_DIM


def _merge_heads(outs, za):
    head = _head_of_lane()
    o = outs[-1]
    for hh in range(HEADS_PER_STEP - 2, -1, -1):
        o = jnp.where(head == hh, outs[hh], o)
    return (o * _silu(za.astype(F32))).astype(BF16)


def _ctx_attn_kernel(q_ref, k_ref, v_ref, za_ref, o_ref):
    head = _head_of_lane()
    for b in range(CTX_BB):
        sl = slice(b * SEQ, (b + 1) * SEQ)
        q, k, v = q_ref[sl, :], k_ref[sl, :], v_ref[sl, :]
        outs = []
        for hh in range(HEADS_PER_STEP):
            qh = jnp.where(head == hh, q, jnp.zeros_like(q))
            outs.append(_softmax_pv([_dot_nt(qh, k)], [v]))
        o_ref[sl, :] = _merge_heads(outs, za_ref[sl, :])


HEAD_BLOCKS = ATT_WIDTH // HEAD_LANES
QKV_Q, QKV_K, QKV_V = 0, HEAD_BLOCKS, 2 * HEAD_BLOCKS
GZ_ZA = 2 * D_MODEL // HEAD_LANES


def _ctx_attention(qkv, gz):
    rows = CTX_BB * SEQ

    def spec(first):
        return pl.BlockSpec((rows, HEAD_LANES), lambda b, hq: (b, first + hq))

    return pl.pallas_call(
        _ctx_attn_kernel,
        grid=(BATCH // CTX_BB, N_HEADS // HEADS_PER_STEP),
        in_specs=[spec(QKV_Q), spec(QKV_K), spec(QKV_V), spec(GZ_ZA)],
        out_specs=spec(0),
        out_shape=jax.ShapeDtypeStruct((N_CTX_TOK, ATT_WIDTH), BF16),
        compiler_params=_params("arbitrary", "arbitrary"),
        name="ctx_attention",
    )(qkv, qkv, qkv, gz)


N_DROW = 2 * WIN_ROWS - 1


def _nbr_build_table(band_ref, tbl):
    lane = lax.broadcasted_iota(jnp.int32, (GRID_W, 2 * GRID_W), 1)
    qc = lax.broadcasted_iota(jnp.int32, (GRID_W, 2 * GRID_W), 0)
    kc = lane & (GRID_W - 1)
    cs = jnp.clip(qc - WIN_COLS // 2, 0, GRID_W - WIN_COLS)
    valid = jnp.logical_and(kc >= cs, kc < cs + WIN_COLS)
    for hh in range(HEADS_PER_STEP):
        for dr in range(N_DROW):
            x = jnp.broadcast_to(band_ref[hh, dr:dr + 1, :], (GRID_W, 2 * GRID_W))
            lo = pltpu.roll(x, 0, axis=1, stride=1, stride_axis=0)
            hi = pltpu.roll(x, GRID_W, axis=1, stride=1, stride_axis=0)
            tbl[hh, dr] = jnp.where(valid, jnp.where(lane < GRID_W, lo, hi), NEG_INF)
        tbl[hh, N_DROW] = jnp.full((GRID_W, 2 * GRID_W), NEG_INF, F32)


NBR_KEYS = NBR_K + PAST_LEN
KEY_TILE = 256
SOFTMAX_ROWS = 64


def _nbr_block_index(a):
    ks = jnp.clip(Q_ROWS_PER_BLOCK * a - Q_ROWS_PER_BLOCK, 0, GRID_H - NBR_KROWS)
    idx = []
    for ri in range(Q_ROWS_PER_BLOCK):
        r = Q_ROWS_PER_BLOCK * a + ri
        rs = jnp.clip(r - WIN_ROWS // 2, 0, GRID_H - WIN_ROWS)
        row = []
        for kri in range(NBR_KROWS):
            kr = ks + kri
            valid = jnp.logical_and(kr >= rs, kr < rs + WIN_ROWS)
            row.append(jnp.where(valid, kr - r + WIN_ROWS - 1, N_DROW))
        idx.append(row)
    return idx


def _nbr_bias_tile(tbl, hh, idx, t):
    half0 = lax.broadcasted_iota(jnp.int32, (1, 2 * GRID_W), 1) < GRID_W
    krows = KEY_TILE // GRID_W
    rows = []
    for ri in range(Q_ROWS_PER_BLOCK):
        tiles = [jnp.where(half0, tbl[hh, idx[ri][kri]], tbl[hh, idx[ri][kri + 1]])
                 for kri in range(krows * t, krows * (t + 1), 2)]
        rows.append(jnp.concatenate(tiles, axis=1))
    return jnp.concatenate(rows, axis=0)


def _nbr_attn_kernel(q_ref, k_ref, v_ref, kc_ref, vc_ref, band_ref, za_ref, o_ref,
                     tbl, kcat, vcat, s_scr, p_scr, o_scr):
    a = pl.program_id(1)

    @pl.when(a == 0)
    def _():
        _nbr_build_table(band_ref, tbl)

    start = pl.multiple_of(jnp.clip(a * NBR_Q - NBR_Q, 0, DEC_SEQ - NBR_K), NBR_Q)
    head = _head_of_lane()
    idx = _nbr_block_index(a)
    for b in range(DEC_BATCH):
        kcat[b, 0:NBR_K, :] = k_ref[b, pl.ds(start, NBR_K), :]
        kcat[b, NBR_K:NBR_KEYS, :] = kc_ref[b].astype(BF16)
        vcat[b, 0:NBR_K, :] = v_ref[b, pl.ds(start, NBR_K), :]
        vcat[b, NBR_K:NBR_KEYS, :] = vc_ref[b].astype(BF16)

    def scores(b, hh):
        q = q_ref[b]
        qh = jnp.where(head == hh, q, jnp.zeros_like(q))
        mrun = None
        for t in range(NBR_KEYS // KEY_TILE):
            st = _dot_nt(qh, kcat[b, t * KEY_TILE:(t + 1) * KEY_TILE, :])
            if t < NBR_K // KEY_TILE:
                st = st + _nbr_bias_tile(tbl, hh, idx, t)
            s_scr[hh, :, t * KEY_TILE:(t + 1) * KEY_TILE] = st
            mt = jnp.maximum(st[:, :128], st[:, 128:])
            mrun = mt if mrun is None else jnp.maximum(mrun, mt)
        return mrun.max(axis=-1, keepdims=True)

    def attend(b, hh, m):
        ls = []
        for r0 in range(0, NBR_Q, SOFTMAX_ROWS):
            mr = m[r0:r0 + SOFTMAX_ROWS]
            lrun = jnp.zeros((SOFTMAX_ROWS, 128), F32)
            for c0 in range(0, NBR_KEYS, 128):
                p = jnp.exp(s_scr[hh, r0:r0 + SOFTMAX_ROWS, c0:c0 + 128] - mr)
                lrun = lrun + p
                p_scr[hh, r0:r0 + SOFTMAX_ROWS, c0:c0 + 128] = p.astype(BF16)
            ls.append(lrun.sum(axis=-1, keepdims=True))
        l = jnp.concatenate(ls, axis=0)
        o = jnp.dot(p_scr[hh], vcat[b], preferred_element_type=F32)
        lanes = slice(hh * HEAD_DIM, (hh + 1) * HEAD_DIM)
        o_scr[b, :, lanes] = o[:, lanes] / l

    order = [(b, hh) for b in range(DEC_BATCH) for hh in range(HEADS_PER_STEP)]
    m_cur = scores(*order[0])
    for n, (b, hh) in enumerate(order):
        m_next = scores(*order[n + 1]) if n + 1 < len(order) else None
        attend(b, hh, m_cur)
        m_cur = m_next
    for b in range(DEC_BATCH):
        o_ref[b] = (o_scr[b] * _silu(za_ref[b].astype(F32))).astype(BF16)


def _nbr_attention(qkv4, kc, vc, band, gz4):
    first = N_CTX_TOK // DEC_SEQ // DEC_BATCH
    nblk = GRID_H // Q_ROWS_PER_BLOCK
    hw = HEAD_LANES

    def qspec(col0):
        return pl.BlockSpec((DEC_BATCH, NBR_Q, hw), lambda hp, a: (first, a, col0 + hp))

    def kspec(col0):
        return pl.BlockSpec((DEC_BATCH, DEC_SEQ, hw), lambda hp, a: (first, 0, col0 + hp))

    cspec = pl.BlockSpec((DEC_BATCH, PAST_LEN, hw), lambda hp, a: (0, 0, hp))
    bspec = pl.BlockSpec((HEADS_PER_STEP, N_DROW + 1, 2 * GRID_W), lambda hp, a: (hp, 0, 0))
    return pl.pallas_call(
        _nbr_attn_kernel,
        grid=(N_HEADS // HEADS_PER_STEP, nblk),
        in_specs=[qspec(QKV_Q), kspec(QKV_K), kspec(QKV_V), cspec, cspec, bspec, qspec(GZ_ZA)],
        out_specs=pl.BlockSpec((DEC_BATCH, NBR_Q, hw), lambda hp, a: (0, a, hp)),
        out_shape=jax.ShapeDtypeStruct((DEC_BATCH, DEC_SEQ, ATT_WIDTH), BF16),
        scratch_shapes=[pltpu.VMEM((HEADS_PER_STEP, N_DROW + 1, GRID_W, 2 * GRID_W), F32),
                        pltpu.VMEM((DEC_BATCH, NBR_KEYS, HEAD_LANES), BF16),
                        pltpu.VMEM((DEC_BATCH, NBR_KEYS, HEAD_LANES), BF16),
                        pltpu.VMEM((HEADS_PER_STEP, NBR_Q, NBR_KEYS), F32),
                        pltpu.VMEM((HEADS_PER_STEP, NBR_Q, NBR_KEYS), BF16),
                        pltpu.VMEM((DEC_BATCH, NBR_Q, HEAD_LANES), F32)],
        compiler_params=_params("arbitrary", "arbitrary"),
        name="nbr_attention",
    )(qkv4, qkv4, qkv4, kc, vc, band, gz4)


def _nbr_bias_band(rpb):
    ncol = 2 * WIN_COLS - 1
    fill = jnp.full(rpb.shape[:2] + (2 * GRID_W - ncol,), NEG_INF, F32)
    band = jnp.concatenate([rpb[..., WIN_COLS - 1:], fill, rpb[..., :WIN_COLS - 1]], axis=-1)
    return jnp.pad(band, ((0, 0), (0, 1), (0, 0)))


BACK_TM = CHUNK * CH_PER_SEG
BACK_CTX_STEPS = N_CTX_TOK // BACK_TM
BACK_STEPS_PER_DEC_SEQ = DEC_SEQ // BACK_TM


def _back_kernel(xp_ref, xs_ref, y_ref, zs_ref, ac_ref, ad_ref, gs_ref, ga_ref, gate_ref, bglu_ref,
                 wglu_ref, wso_ref, wao_ref, wo_ref, op_ref, os_ref):
    i = pl.program_id(0)

    def compute(x, a2):
        ys = _gelu_tanh(y_ref[0].reshape(BACK_TM, SSM_WIDTH))
        t = jnp.dot(ys.astype(BF16), wglu_ref[...], preferred_element_type=F32) + bglu_ref[...]
        ys = ys * _sigmoid(t) * _silu(zs_ref[...].astype(F32))
        p_s = jnp.dot(ys.astype(BF16), wso_ref[...], preferred_element_type=F32)
        p_a = jnp.dot(a2, wao_ref[...], preferred_element_type=F32)
        merged = (_sigmoid(gs_ref[...].astype(F32)) * p_s + _sigmoid(ga_ref[...].astype(F32)) * p_a)
        return x + gate_ref[0] * jnp.dot(merged.astype(BF16), wo_ref[...], preferred_element_type=F32)

    @pl.when(i < BACK_CTX_STEPS)
    def _():
        op_ref[...] = compute(xp_ref[...], ac_ref[...])

    @pl.when(i >= BACK_CTX_STEPS)
    def _():
        os_ref[...] = compute(xs_ref[...], ad_ref[...])


def _back(xp, xs, y, zs, a_ctx, a_dec, gz, gate3, b_glu, w_glu, w_so, w_ao, w_o):
    steps = N_TOK // BACK_TM
    n0 = BACK_CTX_STEPS
    row = functools.partial(_mod_row, ctx_steps=n0, steps_per_seq=BACK_STEPS_PER_DEC_SEQ)
    lo = lambda i: (jnp.minimum(i, n0 - 1), 0)
    hi = lambda i: (jnp.maximum(i - n0, 0), 0)
    cur = lambda i: (i, 0)
    const = lambda i: (0, 0)
    once = pl.Buffered(1)
    return pl.pallas_call(
        _back_kernel,
        grid=(steps,),
        in_specs=[pl.BlockSpec((BACK_TM, D_MODEL), lo),
                  pl.BlockSpec((BACK_TM, D_MODEL), hi),
                  pl.BlockSpec((1, CH_PER_SEG, None, CHUNK, SSM_WIDTH),
                               lambda i: (i // N_SEG, 0, i % N_SEG, 0, 0)),
                  pl.BlockSpec((BACK_TM, SSM_WIDTH), cur),
                  pl.BlockSpec((BACK_TM, ATT_WIDTH), lo),
                  pl.BlockSpec((BACK_TM, ATT_WIDTH), hi),
                  pl.BlockSpec((BACK_TM, D_MODEL), lambda i: (i, 0)),
                  pl.BlockSpec((BACK_TM, D_MODEL), lambda i: (i, 1)),
                  pl.BlockSpec((1, 1, D_MODEL), lambda i: (row(i), 0, 0)),
                  pl.BlockSpec((1, SSM_WIDTH), const),
                  pl.BlockSpec((SSM_WIDTH, SSM_WIDTH), const, pipeline_mode=once),
                  pl.BlockSpec((SSM_WIDTH, D_MODEL), const, pipeline_mode=once),
                  pl.BlockSpec((ATT_WIDTH, D_MODEL), const, pipeline_mode=once),
                  pl.BlockSpec((D_MODEL, D_MODEL), const, pipeline_mode=once)],
        out_specs=[pl.BlockSpec((BACK_TM, D_MODEL), lo),
                   pl.BlockSpec((BACK_TM, D_MODEL), hi)],
        out_shape=[jax.ShapeDtypeStruct((N_CTX_TOK, D_MODEL), F32),
                   jax.ShapeDtypeStruct((N_DEC_TOK, D_MODEL), F32)],
        compiler_params=_params("arbitrary"),
        name="gated_output",
    )(xp, xs, y, zs, a_ctx, a_dec, gz, gz, gate3, b_glu.reshape(1, SSM_WIDTH), w_glu, w_so, w_ao, w_o)


def _layer(xp, xs, cache_k, cache_v, st_re, st_im, c, c_ctx, norm_w, w_ada, b_ada, w_in, q_norm_w, k_norm_w,
           rpb, a_re, a_im, log_dt, b_re, b_im, c_re, c_im, d, w_glu, b_glu, w_so, w_ao, w_o):
    G, P = SSM_GROUPS, SSM_STATE

    cond8 = jnp.zeros((8, D_MODEL), F32).at[0].set(c_ctx).at[1:1 + DEC_BATCH].set(c)
    mod = _modulation(cond8, w_ada, b_ada)
    shift3 = mod[:1 + DEC_BATCH, None, :D_MODEL]
    scale3 = mod[:1 + DEC_BATCH, None, D_MODEL:2 * D_MODEL]
    gate3 = mod[:1 + DEC_BATCH, None, 2 * D_MODEL:]

    h = _norm_modulate(xp, xs, shift3, scale3, norm_w)
    u, zs, qkv, kf32, vf32, gz = _in_proj(h, w_in, q_norm_w, k_norm_w)

    arow = jnp.transpose(a_re, (1, 0, 2)).reshape(G, 2 * P)
    airow = jnp.transpose(a_im, (1, 0, 2)).reshape(G, 2 * P)
    dtrow = jnp.broadcast_to(jnp.transpose(log_dt, (1, 0))[:, :, None], (G, 2, P)).reshape(G, 2 * P)
    rowp = jnp.stack([arow, airow, dtrow], axis=1)
    btr = jnp.transpose(b_re, (1, 3, 0, 2)).reshape(G, SSM_GROUP, 2 * P)
    bti = jnp.transpose(b_im, (1, 3, 0, 2)).reshape(G, SSM_GROUP, 2 * P)
    ctr = jnp.transpose(c_re, (1, 2, 0, 3)).reshape(G, SSM_GROUP, 2 * P)
    cti = jnp.transpose(c_im, (1, 2, 0, 3)).reshape(G, SSM_GROUP, 2 * P)
    t_op, s_op, r_op, lam, powc = _ssm_operators(rowp, btr, bti, ctr, cti)

    dtile = d.reshape(1, SSM_WIDTH)
    h0c = jnp.zeros((G, BATCH, 2 * P), F32)
    h0d_re = jnp.transpose(st_re, (2, 0, 1, 3)).reshape(G, DEC_BATCH, 2 * P)
    h0d_im = jnp.transpose(st_im, (2, 0, 1, 3)).reshape(G, DEC_BATCH, 2 * P)
    y2d, fin = _ssm_scan(u.reshape(N_TOK, SSM_WIDTH), t_op, s_op, r_op, lam, powc, dtile,
                         h0c, h0c, h0d_re, h0d_im)
    y = y2d.reshape(2, CH_PER_SEG, N_SEG, CHUNK, SSM_WIDTH)

    a_ctx = _ctx_attention(qkv, gz)
    nseq4 = N_TOK // DEC_SEQ
    a_dec = _nbr_attention(qkv.reshape(nseq4, DEC_SEQ, qkv.shape[1]),
                           cache_k.reshape(DEC_BATCH, PAST_LEN, ATT_WIDTH),
                           cache_v.reshape(DEC_BATCH, PAST_LEN, ATT_WIDTH),
                           _nbr_bias_band(rpb.astype(F32)),
                           gz.reshape(nseq4, DEC_SEQ, gz.shape[1]))

    yp, ys_out = _back(xp, xs, y, zs, a_ctx, a_dec.reshape(N_DEC_TOK, ATT_WIDTH), gz, gate3, b_glu,
                       w_glu.astype(BF16), w_so.astype(BF16), w_ao.astype(BF16), w_o.astype(BF16))

    fin_ctx = fin[0]
    new_re = jnp.transpose(fin_ctx[:, :, :2 * P].reshape(G, BATCH, 2, P), (1, 2, 0, 3))
    new_im = jnp.transpose(fin_ctx[:, :, 2 * P:].reshape(G, BATCH, 2, P), (1, 2, 0, 3))
    return yp, ys_out, kf32, vf32, new_re, new_im


def kernel(x_prompt, x_sample, cache_k, cache_v, state_ssm_re, state_ssm_im, c, c_ctx, norm_w, w_ada, b_ada,
           w_in, q_norm_w, k_norm_w, rel_pos_bias, ssm_a_re, ssm_a_im, ssm_log_dt, ssm_b_re, ssm_b_im,
           ssm_c_re, ssm_c_im, ssm_d, w_glu, b_glu, w_ssm_out, w_att_out, w_o):
    depth = norm_w.shape[0]
    xp = x_prompt.reshape(N_CTX_TOK, D_MODEL)
    xs = x_sample.reshape(N_DEC_TOK, D_MODEL)
    new_k, new_v, new_re, new_im = [], [], [], []
    for l in range(depth):
        xp, xs, kl, vl, rl, il = _layer(
            xp, xs, cache_k[:, l], cache_v[:, l], state_ssm_re[:, l], state_ssm_im[:, l], c, c_ctx,
            norm_w[l], w_ada[l], b_ada[l], w_in[l], q_norm_w[l], k_norm_w[l], rel_pos_bias[l],
            ssm_a_re[l], ssm_a_im[l], ssm_log_dt[l], ssm_b_re[l], ssm_b_im[l], ssm_c_re[l], ssm_c_im[l],
            ssm_d[l], w_glu[l], b_glu[l], w_ssm_out[l], w_att_out[l], w_o[l])
        new_k.append(kl.reshape(BATCH, SEQ, N_HEADS, HEAD_DIM))
        new_v.append(vl.reshape(BATCH, SEQ, N_HEADS, HEAD_DIM))
        new_re.append(rl)
        new_im.append(il)
    return (xp.reshape(BATCH, SEQ, D_MODEL), xs.reshape(DEC_BATCH, DEC_SEQ, D_MODEL),
            jnp.stack(new_k, axis=1), jnp.stack(new_v, axis=1),
            jnp.stack(new_re, axis=1), jnp.stack(new_im, axis=1))
```

```python
import functools
import math

import jax
import jax.numpy as jnp
from jax import lax
from jax.experimental import pallas as pl
from jax.experimental.pallas import tpu as pltpu

D_MODEL = 2048
BATCH = 16
SEQ = 256
DEC_BATCH = 2
DEC_SEQ = 2048
PAST_LEN = 512
GRID_W = 64
GRID_H = DEC_SEQ // GRID_W
SSM_WIDTH = D_MODEL // 2
SSM_GROUP = 16
SSM_GROUPS = SSM_WIDTH // SSM_GROUP
SSM_STATE = 64
N_HEADS = 16
HEAD_DIM = 64
ATT_WIDTH = N_HEADS * HEAD_DIM
WIN_ROWS = 8
WIN_COLS = 16
EPS = 1e-6
NEG_INF = -1e30

N_CTX_TOK = BATCH * SEQ
N_DEC_TOK = DEC_BATCH * DEC_SEQ
N_TOK = N_CTX_TOK + N_DEC_TOK

OFF_U = 0
OFF_ZS = SSM_WIDTH
OFF_Q = 2 * SSM_WIDTH
OFF_K = OFF_Q + ATT_WIDTH
OFF_V = OFF_K + ATT_WIDTH
OFF_ZA = OFF_V + ATT_WIDTH
OFF_GS = OFF_ZA + ATT_WIDTH
OFF_GA = OFF_GS + D_MODEL

CHUNK = 16
FLAT = CHUNK * SSM_GROUP
N_SEG = 16
CH_PER_SEG = 16
DSTATE = 2 * SSM_STATE

Q_ROWS_PER_BLOCK = 4
NBR_Q = Q_ROWS_PER_BLOCK * GRID_W
NBR_KROWS = 12
NBR_K = NBR_KROWS * GRID_W

VMEM_LIMIT = 56 * 1024 * 1024
QKV_VMEM_LIMIT = 60 * 1024 * 1024

F32 = jnp.float32
BF16 = jnp.bfloat16
HIGHEST = lax.Precision.HIGHEST


def _sigmoid(x):
    return 1.0 / (1.0 + jnp.exp(-x))


def _silu(x):
    return x * _sigmoid(x)


def _gelu_tanh(x):
    return 0.5 * x * (1.0 + jnp.tanh(math.sqrt(2.0 / math.pi) * (x + 0.044715 * (x * x * x))))


def _cmul(ar, ai, br, bi):
    return ar * br - ai * bi, ar * bi + ai * br


def _dot_nt(a, b):
    return lax.dot_general(a, b, (((1,), (1,)), ((), ())), preferred_element_type=F32)


def _params(*sem, vmem_limit=VMEM_LIMIT):
    return pltpu.CompilerParams(dimension_semantics=sem, vmem_limit_bytes=vmem_limit)


def _mod_kernel(cond_ref, w_ref, b_ref, o_ref):
    c = cond_ref[...]
    s = _silu(c).astype(BF16)
    o_ref[...] = jnp.dot(s, w_ref[...].astype(BF16), preferred_element_type=F32) + b_ref[...]


def _modulation(cond8, w_ada, b_ada):
    tn = 768
    n = w_ada.shape[1]
    return pl.pallas_call(
        _mod_kernel,
        grid=(n // tn,),
        in_specs=[pl.BlockSpec((8, D_MODEL), lambda j: (0, 0)),
                  pl.BlockSpec((D_MODEL, tn), lambda j: (0, j)),
                  pl.BlockSpec((1, tn), lambda j: (0, j))],
        out_specs=pl.BlockSpec((8, tn), lambda j: (0, j)),
        out_shape=jax.ShapeDtypeStruct((8, n), F32),
        compiler_params=_params("arbitrary"),
        name="modulation",
    )(cond8, w_ada, b_ada.reshape(1, n))


NORM_TM = 512
NORM_ROWS = 16
NORM_UNROLL = 8
NORM_CTX_STEPS = N_CTX_TOK // NORM_TM
NORM_STEPS_PER_DEC_SEQ = DEC_SEQ // NORM_TM


def _mod_row(i, ctx_steps, steps_per_seq):
    return jnp.where(i < ctx_steps, 0, 1 + (i - ctx_steps) // steps_per_seq)


def _norm_kernel(xp_ref, xs_ref, shift_ref, scale_ref, nw_ref, o_ref):
    i = pl.program_id(0)

    gain = nw_ref[...] * (1.0 + scale_ref[0])
    shift = shift_ref[0]

    def body(x_ref):
        def rows(r, carry):
            sl = pl.ds(pl.multiple_of(r * NORM_ROWS, NORM_ROWS), NORM_ROWS)
            x = x_ref[sl, :]
            ms = jnp.mean(x * x, axis=-1, keepdims=True)
            o_ref[sl, :] = (x * lax.rsqrt(ms + EPS) * gain + shift).astype(BF16)
            return carry

        lax.fori_loop(0, NORM_TM // NORM_ROWS, rows, 0, unroll=NORM_UNROLL)

    @pl.when(i < NORM_CTX_STEPS)
    def _():
        body(xp_ref)

    @pl.when(i >= NORM_CTX_STEPS)
    def _():
        body(xs_ref)


def _norm_modulate(xp, xs, shift3, scale3, norm_w):
    steps = N_TOK // NORM_TM
    row = functools.partial(_mod_row, ctx_steps=NORM_CTX_STEPS, steps_per_seq=NORM_STEPS_PER_DEC_SEQ)
    return pl.pallas_call(
        _norm_kernel,
        grid=(steps,),
        in_specs=[pl.BlockSpec((NORM_TM, D_MODEL), lambda i: (jnp.minimum(i, NORM_CTX_STEPS - 1), 0)),
                  pl.BlockSpec((NORM_TM, D_MODEL), lambda i: (jnp.maximum(i - NORM_CTX_STEPS, 0), 0)),
                  pl.BlockSpec((1, 1, D_MODEL), lambda i: (row(i), 0, 0)),
                  pl.BlockSpec((1, 1, D_MODEL), lambda i: (row(i), 0, 0)),
                  pl.BlockSpec((1, D_MODEL), lambda i: (0, 0))],
        out_specs=pl.BlockSpec((NORM_TM, D_MODEL), lambda i: (i, 0)),
        out_shape=jax.ShapeDtypeStruct((N_TOK, D_MODEL), BF16),
        compiler_params=_params("arbitrary"),
        name="norm_modulate",
    )(xp, xs, shift3, scale3, norm_w.reshape(1, D_MODEL))


PROJ_TM = 1024
PROJ_TN = 1024
PROJ_CTX_STEPS = N_CTX_TOK // PROJ_TM
PROJ_SEGS = PROJ_TM // (CHUNK * CH_PER_SEG)


def _head_group_ones():
    r = lax.broadcasted_iota(jnp.int32, (256, 256), 0) // HEAD_DIM
    c = lax.broadcasted_iota(jnp.int32, (256, 256), 1) // HEAD_DIM
    return jnp.where(r == c, 1.0, 0.0).astype(BF16)


def _head_rms(acc, nw):
    ones = _head_group_ones()
    outs = []
    for c in range(acc.shape[1] // 256):
        a = acc[:, c * 256:(c + 1) * 256]
        ssum = jnp.dot((a * a).astype(BF16), ones, preferred_element_type=F32)
        outs.append(a * lax.rsqrt(ssum * (1.0 / HEAD_DIM) + EPS))
    return jnp.concatenate(outs, axis=1) * nw


def _proj_tile(h_ref, w_ref, wbf):
    @pl.when(pl.program_id(1) == 0)
    def _():
        wbf[...] = w_ref[...].astype(BF16)

    return jnp.dot(h_ref[...], wbf[...], preferred_element_type=F32)


def _proj_s5_kernel(h_ref, w_ref, u_ref, zs_ref, wbf):
    j = pl.program_id(0)
    acc = _proj_tile(h_ref, w_ref, wbf)

    @pl.when(j == 0)
    def _():
        for seg in range(PROJ_SEGS):
            for ch in range(CH_PER_SEG):
                r0 = (seg * CH_PER_SEG + ch) * CHUNK
                u_ref[0, ch, seg] = acc[r0:r0 + CHUNK, :]

    @pl.when(j == 1)
    def _():
        zs_ref[...] = acc.astype(BF16)


def _proj_qkv_kernel(h_ref, w_ref, nw_ref, o_ref, kf_ref, vf_ref, wbf):
    j = pl.program_id(0)
    i = pl.program_id(1)
    acc = _proj_tile(h_ref, w_ref, wbf)

    @pl.when(j == 0)
    def _():
        o_ref[...] = (_head_rms(acc, nw_ref[0]) * (HEAD_DIM ** -0.5)).astype(BF16)

    @pl.when(j == 1)
    def _():
        kn = _head_rms(acc, nw_ref[0])
        o_ref[...] = kn.astype(BF16)

        @pl.when(i < PROJ_CTX_STEPS)
        def _():
            kf_ref[...] = kn

    @pl.when(j == 2)
    def _():
        o_ref[...] = acc.astype(BF16)

        @pl.when(i < PROJ_CTX_STEPS)
        def _():
            vf_ref[...] = acc


def _proj_gate_kernel(h_ref, w_ref, o_ref, wbf):
    o_ref[...] = _proj_tile(h_ref, w_ref, wbf).astype(BF16)


def _in_proj(h, w_in, q_norm_w, k_norm_w):
    ni = N_TOK // PROJ_TM
    last = ni - 1
    ctx_last = PROJ_CTX_STEPS - 1
    tiles_per_path = N_SEG // PROJ_SEGS
    h_spec = pl.BlockSpec((PROJ_TM, D_MODEL), lambda j, i: (i, 0))
    scratch = [pltpu.VMEM((D_MODEL, PROJ_TN), BF16)]
    params = _params("arbitrary", "arbitrary")

    def u_map(j, i):
        ii = jnp.where(j == 0, i, last)
        return (ii // tiles_per_path, 0, ii % tiles_per_path, 0, 0)

    u, zs = pl.pallas_call(
        _proj_s5_kernel,
        grid=(2, ni),
        in_specs=[h_spec, pl.BlockSpec((D_MODEL, PROJ_TN), lambda j, i: (0, OFF_U // PROJ_TN + j))],
        out_specs=[pl.BlockSpec((1, CH_PER_SEG, PROJ_SEGS, CHUNK, PROJ_TN), u_map),
                   pl.BlockSpec((PROJ_TM, PROJ_TN), lambda j, i: (jnp.where(j == 0, 0, i), 0))],
        out_shape=[jax.ShapeDtypeStruct((2, CH_PER_SEG, N_SEG, CHUNK, SSM_WIDTH), F32),
                   jax.ShapeDtypeStruct((N_TOK, SSM_WIDTH), BF16)],
        scratch_shapes=scratch, compiler_params=params, name="in_proj_s5",
    )(h, w_in)

    nw = jnp.stack([jnp.tile(q_norm_w.reshape(1, HEAD_DIM), (1, PROJ_TN // HEAD_DIM)),
                    jnp.tile(k_norm_w.reshape(1, HEAD_DIM), (1, PROJ_TN // HEAD_DIM))])
    qkv, kf32, vf32 = pl.pallas_call(
        _proj_qkv_kernel,
        grid=(3, ni),
        in_specs=[h_spec, pl.BlockSpec((D_MODEL, PROJ_TN), lambda j, i: (0, OFF_Q // PROJ_TN + j)),
                  pl.BlockSpec((1, 1, PROJ_TN), lambda j, i: (jnp.where(j == 1, 1, 0), 0, 0))],
        out_specs=[pl.BlockSpec((PROJ_TM, PROJ_TN), lambda j, i: (i, j)),
                   pl.BlockSpec((PROJ_TM, PROJ_TN),
                                lambda j, i: (jnp.where(j == 0, 0, jnp.where(j == 1, jnp.minimum(i, ctx_last),
                                                                             ctx_last)), 0)),
                   pl.BlockSpec((PROJ_TM, PROJ_TN),
                                lambda j, i: (jnp.where(j == 2, jnp.minimum(i, ctx_last), 0), 0))],
        out_shape=[jax.ShapeDtypeStruct((N_TOK, 3 * ATT_WIDTH), BF16),
                   jax.ShapeDtypeStruct((N_CTX_TOK, ATT_WIDTH), F32),
                   jax.ShapeDtypeStruct((N_CTX_TOK, ATT_WIDTH), F32)],
        scratch_shapes=scratch, name="in_proj_qkv",
        compiler_params=_params("arbitrary", "arbitrary", vmem_limit=QKV_VMEM_LIMIT),
    )(h, w_in, nw)

    gate_tiles = 2 * D_MODEL // PROJ_TN
    gz = pl.pallas_call(
        _proj_gate_kernel,
        grid=(gate_tiles + 1, ni),
        in_specs=[h_spec,
                  pl.BlockSpec((D_MODEL, PROJ_TN),
                               lambda j, i: (0, jnp.where(j < gate_tiles, OFF_GS // PROJ_TN + j,
                                                          OFF_ZA // PROJ_TN)))],
        out_specs=pl.BlockSpec((PROJ_TM, PROJ_TN), lambda j, i: (i, j)),
        out_shape=jax.ShapeDtypeStruct((N_TOK, 2 * D_MODEL + ATT_WIDTH), BF16),
        scratch_shapes=scratch, compiler_params=params, name="in_proj_gates",
    )(h, w_in)
    return u, zs, qkv, kf32, vf32, gz


def _pow_select(e, pows):
    rr = jnp.where((e & 1) != 0, pows[0][0], 1.0)
    ri = jnp.where((e & 1) != 0, pows[0][1], 0.0)
    for b in range(1, len(pows)):
        bit = (e & (1 << b)) != 0
        fr = jnp.where(bit, pows[b][0], 1.0)
        fi = jnp.where(bit, pows[b][1], 0.0)
        rr, ri = _cmul(rr, ri, fr, fi)
    return rr, ri


def _discretize(a_re, a_im, log_dt):
    lr = jnp.minimum(a_re, -1e-4)
    li = a_im
    dt = jnp.exp(log_dt)
    mag = jnp.exp(lr * dt)
    br = mag * jnp.cos(li * dt)
    bi = mag * jnp.sin(li * dt)
    den = lr * lr + li * li
    nr = br - 1.0
    cr = (nr * lr + bi * li) / den
    ci = (bi * lr - nr * li) / den
    return (br, bi), (cr, ci)


def _squarings(pr, pi, n):
    out = [(pr, pi)]
    for _ in range(n):
        pr, pi = _cmul(pr, pi, pr, pi)
        out.append((pr, pi))
    return out


def _dot_nt_exact(a, b):
    return lax.dot_general(a, b, (((1,), (1,)), ((), ())), preferred_element_type=F32, precision=HIGHEST)


def _ssm_build_ops(n_groups, rowp_ref, btr_ref, bti_ref, ctr_ref, cti_ref, t_ref, s_ref, rt_ref, lam_ref, powc_ref):
    lane_b = lax.broadcasted_iota(jnp.int32, (SSM_GROUP, DSTATE), 1)
    lane_t = lax.broadcasted_iota(jnp.int32, (SSM_GROUP, FLAT), 1)
    row_c = lax.broadcasted_iota(jnp.int32, (CH_PER_SEG, DSTATE), 0)
    lane_c = lax.broadcasted_iota(jnp.int32, (CH_PER_SEG, DSTATE), 1)
    exp_c = jnp.where(lane_c < SSM_STATE, row_c, (CH_PER_SEG - 1) - row_c)

    for g in range(n_groups):
        rp = rowp_ref[g]
        (lbr, lbi), (cfr, cfi) = _discretize(rp[0:1], rp[1:2], rp[2:3])
        sq = _squarings(lbr, lbi, 8)
        bbr, bbi = _cmul(btr_ref[g], bti_ref[g], cfr, cfi)

        ptr, pti = _pow_select(exp_c, sq[:4])

        s_re, s_im = [], []
        for sp in range(CHUNK):
            e = CHUNK - 1 - sp
            br, bi = _cmul(bbr, bbi, ptr[e:e + 1], pti[e:e + 1])
            s_re.append(br)
            s_im.append(bi)
        s_ref[g] = jnp.concatenate([jnp.concatenate(s_re, axis=0), jnp.concatenate(s_im, axis=0)],
                                   axis=1).astype(BF16)

        xcr = jnp.concatenate([ctr_ref[g]] * CHUNK, axis=0)
        xci = jnp.concatenate([cti_ref[g]] * CHUNK, axis=0)
        xpr = jnp.broadcast_to(ptr[:, None, :], (CHUNK, SSM_GROUP, DSTATE)).reshape(FLAT, DSTATE)
        xpi = jnp.broadcast_to(pti[:, None, :], (CHUNK, SSM_GROUP, DSTATE)).reshape(FLAT, DSTATE)
        ykr, yki = _cmul(xcr, xci, xpr, xpi)
        yrr, yri = _cmul(ykr, yki, lbr, lbi)
        rt_ref[g] = jnp.concatenate([yrr, -yri], axis=1).astype(BF16)

        fwd = lane_b < SSM_STATE
        lhs = jnp.concatenate([jnp.where(fwd, bbr, 0.0), jnp.where(fwd, bbi, 0.0),
                               jnp.where(fwd, 0.0, bbr), jnp.where(fwd, 0.0, bbi)], axis=0)
        p1 = _dot_nt_exact(lhs, ykr)
        p2 = _dot_nt_exact(lhs, yki)
        kf = p1[0:16] - p2[16:32]
        kb = p1[32:48] - p2[48:64]
        for sp in range(CHUNK):
            tf = kf if sp == 0 else pltpu.roll(kf, SSM_GROUP * sp, axis=1)
            tf = jnp.where(lane_t >= SSM_GROUP * sp, tf, 0.0)
            shift = (FLAT - SSM_GROUP * (CHUNK - 1 - sp)) % FLAT
            tb = kb if shift == 0 else pltpu.roll(kb, shift, axis=1)
            tb = jnp.where(lane_t < SSM_GROUP * (sp + 1), tb, 0.0)
            t_ref[g, sp * SSM_GROUP:(sp + 1) * SSM_GROUP, :] = (tf + tb).astype(BF16)

        l16 = sq[4]
        l256 = sq[8]
        lam_ref[g] = jnp.concatenate([l16[0], l16[1], l256[0], l256[1],
                                      jnp.zeros((4, DSTATE), F32)], axis=0)
        q16 = _squarings(l16[0], l16[1], 3)
        pcr, pci = _pow_select(exp_c, q16)
        powc_ref[g, 0] = pcr
        powc_ref[g, 1] = pci


SSM_PATHS = ((BATCH, SEQ // (CHUNK * CH_PER_SEG)), (DEC_BATCH, DEC_SEQ // (CHUNK * CH_PER_SEG)))


def _ssm_path(u, s_op, t_op, rt_op, lam, powc_ref, h0r, h0i, nseg):
    lane = lax.broadcasted_iota(jnp.int32, (N_SEG, DSTATE), 1)
    isf = lane < SSM_STATE
    l16r, l16i, l256r, l256i = lam[0:1], lam[1:2], lam[2:3], lam[3:4]

    z = jnp.dot(u, s_op, preferred_element_type=F32)
    zre, zim = z[:, :DSTATE], z[:, DSTATE:]

    hr = jnp.zeros((N_SEG, DSTATE), F32)
    hi = jnp.zeros((N_SEG, DSTATE), F32)
    hist = []
    for k in range(CH_PER_SEG):
        hist.append((hr, hi))
        kb = CH_PER_SEG - 1 - k
        zr = jnp.where(isf, zre[k * N_SEG:(k + 1) * N_SEG], zre[kb * N_SEG:(kb + 1) * N_SEG])
        zi = jnp.where(isf, zim[k * N_SEG:(k + 1) * N_SEG], zim[kb * N_SEG:(kb + 1) * N_SEG])
        nr, ni = _cmul(l16r, l16i, hr, hi)
        hr, hi = nr + zr, ni + zi

    if nseg > 1:
        h0r = jnp.broadcast_to(h0r[:, None, :], (N_SEG // nseg, nseg, DSTATE)).reshape(N_SEG, DSTATE)
        h0i = jnp.broadcast_to(h0i[:, None, :], (N_SEG // nseg, nseg, DSTATE)).reshape(N_SEG, DSTATE)
    seg = lax.broadcasted_iota(jnp.int32, (N_SEG, DSTATE), 0) & (nseg - 1)
    segpow = _squarings(l256r, l256i, max(int(math.log2(nseg)), 0))

    def shifted(x, d):
        dn = jnp.where(seg >= d, pltpu.roll(x, d, axis=0), 0.0)
        up = jnp.where(seg <= nseg - 1 - d, pltpu.roll(x, N_SEG - d, axis=0), 0.0)
        return jnp.where(isf, dn, up)

    pr, pi = hr, hi
    d = 1
    lvl = 0
    while d < nseg:
        ar, ai = _cmul(segpow[lvl][0], segpow[lvl][1], shifted(pr, d), shifted(pi, d))
        pr, pi = pr + ar, pi + ai
        d *= 2
        lvl += 1
    if nseg > 1:
        e_in = jnp.where(isf, seg, nseg - 1 - seg)
        wr, wi = _pow_select(e_in, segpow[:lvl])
        ar, ai = _cmul(wr, wi, h0r, h0i)
        hsr, hsi = shifted(pr, 1) + ar, shifted(pi, 1) + ai
    else:
        hsr, hsi = h0r, h0i
    er, ei = _cmul(segpow[lvl][0], segpow[lvl][1], h0r, h0i)
    fin = jnp.concatenate([pr + er, pi + ei], axis=1)

    rows = []
    for c in range(CH_PER_SEG):
        cb = CH_PER_SEG - 1 - c
        lr = jnp.where(isf, hist[c][0], hist[cb][0])
        li = jnp.where(isf, hist[c][1], hist[cb][1])
        ar, ai = _cmul(powc_ref[0, c:c + 1, :], powc_ref[1, c:c + 1, :], hsr, hsi)
        rows.append(jnp.concatenate([lr + ar, li + ai], axis=1))
    hent = jnp.concatenate(rows, axis=0).astype(BF16)

    y = jnp.dot(u, t_op, preferred_element_type=F32) + _dot_nt(hent, rt_op)
    return y, fin


GROUPS_PER_STEP = 128 // SSM_GROUP
GROUPS_PER_ITER = 8


def _block_transpose(arrs, blk):
    n = len(arrs)
    width = arrs[0].shape[1]
    j = lax.broadcasted_iota(jnp.int32, arrs[0].shape, 1) // blk
    k = n // 2
    while k >= 1:
        bit = (j & k) != 0
        new = list(arrs)
        for x in range(n):
            if x & k == 0:
                a, b = arrs[x], arrs[x | k]
                new[x] = jnp.where(bit, pltpu.roll(b, k * blk, axis=1), a)
                new[x | k] = jnp.where(bit, b, pltpu.roll(a, width - k * blk, axis=1))
        arrs = new
        k //= 2
    return arrs


def _ssm_kernel(u_ref, rowp_ref, btr_ref, bti_ref, ctr_ref, cti_ref, d_ref, h0cr_ref, h0ci_ref, h0dr_ref, h0di_ref,
                y_ref, fin_ref, ubuf, ybuf, t_ref, s_ref, r_ref, lam_ref, powc_ref):
    _ssm_build_ops(GROUPS_PER_STEP, rowp_ref, btr_ref, bti_ref, ctr_ref, cti_ref,
                   t_ref, s_ref, r_ref, lam_ref, powc_ref)
    h0 = ((h0cr_ref, h0ci_ref), (h0dr_ref, h0di_ref))
    nhalf = FLAT // 128
    rows = N_SEG * CH_PER_SEG
    def token_rows(path, tok, r0, nrows):
        return pl.ds(path * rows * CHUNK + r0 * CHUNK + tok, nrows, stride=CHUNK)

    in_rows = rows
    out_rows = rows
    for path, (_, nseg) in enumerate(SSM_PATHS):
        for half in range(nhalf):
            for r0 in range(0, rows, in_rows):
                toks = [pltpu.bitcast(
                    u_ref[token_rows(path, half * GROUPS_PER_STEP + sb, r0, in_rows), :].astype(BF16), jnp.uint32)
                    for sb in range(GROUPS_PER_STEP)]
                grouped = _block_transpose(toks, SSM_GROUP)
                for g in range(GROUPS_PER_STEP):
                    ubuf[g, r0:r0 + in_rows, half * 128:(half + 1) * 128] = pltpu.bitcast(grouped[g], BF16)

        def body(gi, carry, path=path, nseg=nseg):
            for dg in range(GROUPS_PER_ITER):
                g = gi * GROUPS_PER_ITER + dg
                y, fin = _ssm_path(ubuf[g], s_ref[g], t_ref[g], r_ref[g], lam_ref[g], powc_ref.at[g],
                                   h0[path][0][g], h0[path][1][g], nseg)
                ybuf[g] = y
                fin_ref[path, g] = fin
            return carry

        lax.fori_loop(0, GROUPS_PER_STEP // GROUPS_PER_ITER, body, 0)

        for half in range(nhalf):
            for r0 in range(0, rows, out_rows):
                grouped = [ybuf[g, r0:r0 + out_rows, half * 128:(half + 1) * 128] for g in range(GROUPS_PER_STEP)]
                toks = _block_transpose(grouped, SSM_GROUP)
                for sb in range(GROUPS_PER_STEP):
                    sel = token_rows(path, half * GROUPS_PER_STEP + sb, r0, out_rows)
                    y_ref[sel, :] = toks[sb] + u_ref[sel, :] * d_ref[...]


def _ssm_scan(u2d, rowp, btr, bti, ctr, cti, dtile, h0c_re, h0c_im, h0d_re, h0d_im):
    G = SSM_GROUPS
    gs = GROUPS_PER_STEP
    rows = N_SEG * CH_PER_SEG
    return pl.pallas_call(
        _ssm_kernel,
        grid=(G // gs,),
        in_specs=[
            pl.BlockSpec((N_TOK, 128), lambda o: (0, o)),
            pl.BlockSpec((gs, 3, DSTATE), lambda o: (o, 0, 0)),
            pl.BlockSpec((gs, SSM_GROUP, DSTATE), lambda o: (o, 0, 0)),
            pl.BlockSpec((gs, SSM_GROUP, DSTATE), lambda o: (o, 0, 0)),
            pl.BlockSpec((gs, SSM_GROUP, DSTATE), lambda o: (o, 0, 0)),
            pl.BlockSpec((gs, SSM_GROUP, DSTATE), lambda o: (o, 0, 0)),
            pl.BlockSpec((1, 128), lambda o: (0, o)),
            pl.BlockSpec((gs, BATCH, DSTATE), lambda o: (o, 0, 0)),
            pl.BlockSpec((gs, BATCH, DSTATE), lambda o: (o, 0, 0)),
            pl.BlockSpec((gs, DEC_BATCH, DSTATE), lambda o: (o, 0, 0)),
            pl.BlockSpec((gs, DEC_BATCH, DSTATE), lambda o: (o, 0, 0))],
        out_specs=[pl.BlockSpec((N_TOK, 128), lambda o: (0, o)),
                   pl.BlockSpec((2, gs, N_SEG, 2 * DSTATE), lambda o: (0, o, 0, 0))],
        out_shape=[jax.ShapeDtypeStruct((N_TOK, SSM_WIDTH), F32),
                   jax.ShapeDtypeStruct((2, G, N_SEG, 2 * DSTATE), F32)],
        scratch_shapes=[pltpu.VMEM((gs, rows, FLAT), BF16), pltpu.VMEM((gs, rows, FLAT), F32),
                        pltpu.VMEM((gs, FLAT, FLAT), BF16), pltpu.VMEM((gs, FLAT, FLAT), BF16),
                        pltpu.VMEM((gs, FLAT, FLAT), BF16), pltpu.VMEM((gs, 8, DSTATE), F32),
                        pltpu.VMEM((gs, 2, CH_PER_SEG, DSTATE), F32)],
        compiler_params=_params("arbitrary"),
        name="ssm_scan",
    )(u2d, rowp, btr, bti, ctr, cti, dtile, h0c_re, h0c_im, h0d_re, h0d_im)


CTX_BB = 4
HEADS_PER_STEP = 256 // HEAD_DIM
HEAD_LANES = HEADS_PER_STEP * HEAD_DIM


def _softmax_pv(scores, values):
    m = scores[0].max(axis=-1, keepdims=True)
    for s in scores[1:]:
        m = jnp.maximum(m, s.max(axis=-1, keepdims=True))
    l = None
    o = None
    for s, v in zip(scores, values):
        p = jnp.exp(s - m)
        ls = p.sum(axis=-1, keepdims=True)
        os_ = jnp.dot(p.astype(BF16), v, preferred_element_type=F32)
        l = ls if l is None else l + ls
        o = os_ if o is None else o + os_
    return o / l


def _head_of_lane():
    return lax.broadcasted_iota(jnp.int32, (1, HEAD_LANES), 1) // HEAD_DIM


def _merge_heads(outs, za):
    head = _head_of_lane()
    o = outs[-1]
    for hh in range(HEADS_PER_STEP - 2, -1, -1):
        o = jnp.where(head == hh, outs[hh], o)
    return (o * _silu(za.astype(F32))).astype(BF16)


def _ctx_attn_kernel(q_ref, k_ref, v_ref, za_ref, o_ref):
    head = _head_of_lane()
    for b in range(CTX_BB):
        sl = slice(b * SEQ, (b + 1) * SEQ)
        q, k, v = q_ref[sl, :], k_ref[sl, :], v_ref[sl, :]
        outs = []
        for hh in range(HEADS_PER_STEP):
            qh = jnp.where(head == hh, q, jnp.zeros_like(q))
            outs.append(_softmax_pv([_dot_nt(qh, k)], [v]))
        o_ref[sl, :] = _merge_heads(outs, za_ref[sl, :])


HEAD_BLOCKS = ATT_WIDTH // HEAD_LANES
QKV_Q, QKV_K, QKV_V = 0, HEAD_BLOCKS, 2 * HEAD_BLOCKS
GZ_ZA = 2 * D_MODEL // HEAD_LANES


def _ctx_attention(qkv, gz):
    rows = CTX_BB * SEQ

    def spec(first):
        return pl.BlockSpec((rows, HEAD_LANES), lambda b, hq: (b, first + hq))

    return pl.pallas_call(
        _ctx_attn_kernel,
        grid=(BATCH // CTX_BB, N_HEADS // HEADS_PER_STEP),
        in_specs=[spec(QKV_Q), spec(QKV_K), spec(QKV_V), spec(GZ_ZA)],
        out_specs=spec(0),
        out_shape=jax.ShapeDtypeStruct((N_CTX_TOK, ATT_WIDTH), BF16),
        compiler_params=_params("arbitrary", "arbitrary"),
        name="ctx_attention",
    )(qkv, qkv, qkv, gz)


N_DROW = 2 * WIN_ROWS - 1


def _nbr_build_table(band_ref, tbl):
    lane = lax.broadcasted_iota(jnp.int32, (GRID_W, 2 * GRID_W), 1)
    qc = lax.broadcasted_iota(jnp.int32, (GRID_W, 2 * GRID_W), 0)
    kc = lane & (GRID_W - 1)
    cs = jnp.clip(qc - WIN_COLS // 2, 0, GRID_W - WIN_COLS)
    valid = jnp.logical_and(kc >= cs, kc < cs + WIN_COLS)
    for hh in range(HEADS_PER_STEP):
        for dr in range(N_DROW):
            x = jnp.broadcast_to(band_ref[hh, dr:dr + 1, :], (GRID_W, 2 * GRID_W))
            lo = pltpu.roll(x, 0, axis=1, stride=1, stride_axis=0)
            hi = pltpu.roll(x, GRID_W, axis=1, stride=1, stride_axis=0)
            tbl[hh, dr] = jnp.where(valid, jnp.where(lane < GRID_W, lo, hi), NEG_INF)
        tbl[hh, N_DROW] = jnp.full((GRID_W, 2 * GRID_W), NEG_INF, F32)


NBR_KEYS = NBR_K + PAST_LEN
KEY_TILE = 256
SOFTMAX_ROWS = 64


def _nbr_block_index(a):
    ks = jnp.clip(Q_ROWS_PER_BLOCK * a - Q_ROWS_PER_BLOCK, 0, GRID_H - NBR_KROWS)
    idx = []
    for ri in range(Q_ROWS_PER_BLOCK):
        r = Q_ROWS_PER_BLOCK * a + ri
        rs = jnp.clip(r - WIN_ROWS // 2, 0, GRID_H - WIN_ROWS)
        row = []
        for kri in range(NBR_KROWS):
            kr = ks + kri
            valid = jnp.logical_and(kr >= rs, kr < rs + WIN_ROWS)
            row.append(jnp.where(valid, kr - r + WIN_ROWS - 1, N_DROW))
        idx.append(row)
    return idx


def _nbr_bias_tile(tbl, hh, idx, t):
    half0 = lax.broadcasted_iota(jnp.int32, (1, 2 * GRID_W), 1) < GRID_W
    krows = KEY_TILE // GRID_W
    rows = []
    for ri in range(Q_ROWS_PER_BLOCK):
        tiles = [jnp.where(half0, tbl[hh, idx[ri][kri]], tbl[hh, idx[ri][kri + 1]])
                 for kri in range(krows * t, krows * (t + 1), 2)]
        rows.append(jnp.concatenate(tiles, axis=1))
    return jnp.concatenate(rows, axis=0)


def _nbr_attn_kernel(q_ref, k_ref, v_ref, kc_ref, vc_ref, band_ref, za_ref, o_ref,
                     tbl, kcat, vcat, s_scr, p_scr, o_scr):
    a = pl.program_id(1)

    @pl.when(a == 0)
    def _():
        _nbr_build_table(band_ref, tbl)

    start = pl.multiple_of(jnp.clip(a * NBR_Q - NBR_Q, 0, DEC_SEQ - NBR_K), NBR_Q)
    head = _head_of_lane()
    idx = _nbr_block_index(a)
    for b in range(DEC_BATCH):
        kcat[b, 0:NBR_K, :] = k_ref[b, pl.ds(start, NBR_K), :]
        kcat[b, NBR_K:NBR_KEYS, :] = kc_ref[b].astype(BF16)
        vcat[b, 0:NBR_K, :] = v_ref[b, pl.ds(start, NBR_K), :]
        vcat[b, NBR_K:NBR_KEYS, :] = vc_ref[b].astype(BF16)

    def scores(b, hh):
        q = q_ref[b]
        qh = jnp.where(head == hh, q, jnp.zeros_like(q))
        mrun = None
        for t in range(NBR_KEYS // KEY_TILE):
            st = _dot_nt(qh, kcat[b, t * KEY_TILE:(t + 1) * KEY_TILE, :])
            if t < NBR_K // KEY_TILE:
                st = st + _nbr_bias_tile(tbl, hh, idx, t)
            s_scr[hh, :, t * KEY_TILE:(t + 1) * KEY_TILE] = st
            mt = jnp.maximum(st[:, :128], st[:, 128:])
            mrun = mt if mrun is None else jnp.maximum(mrun, mt)
        return mrun.max(axis=-1, keepdims=True)

    def attend(b, hh, m):
        ls = []
        for r0 in range(0, NBR_Q, SOFTMAX_ROWS):
            mr = m[r0:r0 + SOFTMAX_ROWS]
            lrun = jnp.zeros((SOFTMAX_ROWS, 128), F32)
            for c0 in range(0, NBR_KEYS, 128):
                p = jnp.exp(s_scr[hh, r0:r0 + SOFTMAX_ROWS, c0:c0 + 128] - mr)
                lrun = lrun + p
                p_scr[hh, r0:r0 + SOFTMAX_ROWS, c0:c0 + 128] = p.astype(BF16)
            ls.append(lrun.sum(axis=-1, keepdims=True))
        l = jnp.concatenate(ls, axis=0)
        o = jnp.dot(p_scr[hh], vcat[b], preferred_element_type=F32)
        lanes = slice(hh * HEAD_DIM, (hh + 1) * HEAD_DIM)
        o_scr[b, :, lanes] = o[:, lanes] / l

    order = [(b, hh) for b in range(DEC_BATCH) for hh in range(HEADS_PER_STEP)]
    m_cur = scores(*order[0])
    for n, (b, hh) in enumerate(order):
        m_next = scores(*order[n + 1]) if n + 1 < len(order) else None
        attend(b, hh, m_cur)
        m_cur = m_next
    for b in range(DEC_BATCH):
        o_ref[b] = (o_scr[b] * _silu(za_ref[b].astype(F32))).astype(BF16)


def _nbr_attention(qkv4, kc, vc, band, gz4):
    first = N_CTX_TOK // DEC_SEQ // DEC_BATCH
    nblk = GRID_H // Q_ROWS_PER_BLOCK
    hw = HEAD_LANES

    def qspec(col0):
        return pl.BlockSpec((DEC_BATCH, NBR_Q, hw), lambda hp, a: (first, a, col0 + hp))

    def kspec(col0):
        return pl.BlockSpec((DEC_BATCH, DEC_SEQ, hw), lambda hp, a: (first, 0, col0 + hp))

    cspec = pl.BlockSpec((DEC_BATCH, PAST_LEN, hw), lambda hp, a: (0, 0, hp))
    bspec = pl.BlockSpec((HEADS_PER_STEP, N_DROW + 1, 2 * GRID_W), lambda hp, a: (hp, 0, 0))
    return pl.pallas_call(
        _nbr_attn_kernel,
        grid=(N_HEADS // HEADS_PER_STEP, nblk),
        in_specs=[qspec(QKV_Q), kspec(QKV_K), kspec(QKV_V), cspec, cspec, bspec, qspec(GZ_ZA)],
        out_specs=pl.BlockSpec((DEC_BATCH, NBR_Q, hw), lambda hp, a: (0, a, hp)),
        out_shape=jax.ShapeDtypeStruct((DEC_BATCH, DEC_SEQ, ATT_WIDTH), BF16),
        scratch_shapes=[pltpu.VMEM((HEADS_PER_STEP, N_DROW + 1, GRID_W, 2 * GRID_W), F32),
                        pltpu.VMEM((DEC_BATCH, NBR_KEYS, HEAD_LANES), BF16),
                        pltpu.VMEM((DEC_BATCH, NBR_KEYS, HEAD_LANES), BF16),
                        pltpu.VMEM((HEADS_PER_STEP, NBR_Q, NBR_KEYS), F32),
                        pltpu.VMEM((HEADS_PER_STEP, NBR_Q, NBR_KEYS), BF16),
                        pltpu.VMEM((DEC_BATCH, NBR_Q, HEAD_LANES), F32)],
        compiler_params=_params("arbitrary", "arbitrary"),
        name="nbr_attention",
    )(qkv4, qkv4, qkv4, kc, vc, band, gz4)


def _nbr_bias_band(rpb):
    ncol = 2 * WIN_COLS - 1
    fill = jnp.full(rpb.shape[:2] + (2 * GRID_W - ncol,), NEG_INF, F32)
    band = jnp.concatenate([rpb[..., WIN_COLS - 1:], fill, rpb[..., :WIN_COLS - 1]], axis=-1)
    return jnp.pad(band, ((0, 0), (0, 1), (0, 0)))


BACK_TM = CHUNK * CH_PER_SEG
BACK_CTX_STEPS = N_CTX_TOK // BACK_TM
BACK_STEPS_PER_DEC_SEQ = DEC_SEQ // BACK_TM


def _back_kernel(xp_ref, xs_ref, y_ref, zs_ref, ac_ref, ad_ref, gs_ref, ga_ref, gate_ref, bglu_ref,
                 wglu_ref, wso_ref, wao_ref, wo_ref, op_ref, os_ref):
    i = pl.program_id(0)

    def compute(x, a2):
        ys = _gelu_tanh(y_ref[0].reshape(BACK_TM, SSM_WIDTH))
        t = jnp.dot(ys.astype(BF16), wglu_ref[...], preferred_element_type=F32) + bglu_ref[...]
        ys = ys * _sigmoid(t) * _silu(zs_ref[...].astype(F32))
        p_s = jnp.dot(ys.astype(BF16), wso_ref[...], preferred_element_type=F32)
        p_a = jnp.dot(a2, wao_ref[...], preferred_element_type=F32)
        merged = (_sigmoid(gs_ref[...].astype(F32)) * p_s + _sigmoid(ga_ref[...].astype(F32)) * p_a)
        return x + gate_ref[0] * jnp.dot(merged.astype(BF16), wo_ref[...], preferred_element_type=F32)

    @pl.when(i < BACK_CTX_STEPS)
    def _():
        op_ref[...] = compute(xp_ref[...], ac_ref[...])

    @pl.when(i >= BACK_CTX_STEPS)
    def _():
        os_ref[...] = compute(xs_ref[...], ad_ref[...])


def _back(xp, xs, y, zs, a_ctx, a_dec, gz, gate3, b_glu, w_glu, w_so, w_ao, w_o):
    steps = N_TOK // BACK_TM
    n0 = BACK_CTX_STEPS
    row = functools.partial(_mod_row, ctx_steps=n0, steps_per_seq=BACK_STEPS_PER_DEC_SEQ)
    lo = lambda i: (jnp.minimum(i, n0 - 1), 0)
    hi = lambda i: (jnp.maximum(i - n0, 0), 0)
    cur = lambda i: (i, 0)
    const = lambda i: (0, 0)
    once = pl.Buffered(1)
    return pl.pallas_call(
        _back_kernel,
        grid=(steps,),
        in_specs=[pl.BlockSpec((BACK_TM, D_MODEL), lo),
                  pl.BlockSpec((BACK_TM, D_MODEL), hi),
                  pl.BlockSpec((1, CH_PER_SEG, None, CHUNK, SSM_WIDTH),
                               lambda i: (i // N_SEG, 0, i % N_SEG, 0, 0)),
                  pl.BlockSpec((BACK_TM, SSM_WIDTH), cur),
                  pl.BlockSpec((BACK_TM, ATT_WIDTH), lo),
                  pl.BlockSpec((BACK_TM, ATT_WIDTH), hi),
                  pl.BlockSpec((BACK_TM, D_MODEL), lambda i: (i, 0)),
                  pl.BlockSpec((BACK_TM, D_MODEL), lambda i: (i, 1)),
                  pl.BlockSpec((1, 1, D_MODEL), lambda i: (row(i), 0, 0)),
                  pl.BlockSpec((1, SSM_WIDTH), const),
                  pl.BlockSpec((SSM_WIDTH, SSM_WIDTH), const, pipeline_mode=once),
                  pl.BlockSpec((SSM_WIDTH, D_MODEL), const, pipeline_mode=once),
                  pl.BlockSpec((ATT_WIDTH, D_MODEL), const, pipeline_mode=once),
                  pl.BlockSpec((D_MODEL, D_MODEL), const, pipeline_mode=once)],
        out_specs=[pl.BlockSpec((BACK_TM, D_MODEL), lo),
                   pl.BlockSpec((BACK_TM, D_MODEL), hi)],
        out_shape=[jax.ShapeDtypeStruct((N_CTX_TOK, D_MODEL), F32),
                   jax.ShapeDtypeStruct((N_DEC_TOK, D_MODEL), F32)],
        compiler_params=_params("arbitrary"),
        name="gated_output",
    )(xp, xs, y, zs, a_ctx, a_dec, gz, gz, gate3, b_glu.reshape(1, SSM_WIDTH), w_glu, w_so, w_ao, w_o)


def _layer(xp, xs, cache_k, cache_v, st_re, st_im, c, c_ctx, norm_w, w_ada, b_ada, w_in, q_norm_w, k_norm_w,
           rpb, a_re, a_im, log_dt, b_re, b_im, c_re, c_im, d, w_glu, b_glu, w_so, w_ao, w_o):
    G, P = SSM_GROUPS, SSM_STATE

    cond8 = jnp.zeros((8, D_MODEL), F32).at[0].set(c_ctx).at[1:1 + DEC_BATCH].set(c)
    mod = _modulation(cond8, w_ada, b_ada)
    shift3 = mod[:1 + DEC_BATCH, None, :D_MODEL]
    scale3 = mod[:1 + DEC_BATCH, None, D_MODEL:2 * D_MODEL]
    gate3 = mod[:1 + DEC_BATCH, None, 2 * D_MODEL:]

    h = _norm_modulate(xp, xs, shift3, scale3, norm_w)
    u, zs, qkv, kf32, vf32, gz = _in_proj(h, w_in, q_norm_w, k_norm_w)

    arow = jnp.transpose(a_re, (1, 0, 2)).reshape(G, 2 * P)
    airow = jnp.transpose(a_im, (1, 0, 2)).reshape(G, 2 * P)
    dtrow = jnp.broadcast_to(jnp.transpose(log_dt, (1, 0))[:, :, None], (G, 2, P)).reshape(G, 2 * P)
    rowp = jnp.stack([arow, airow, dtrow], axis=1)
    btr = jnp.transpose(b_re, (1, 3, 0, 2)).reshape(G, SSM_GROUP, 2 * P)
    bti = jnp.transpose(b_im, (1, 3, 0, 2)).reshape(G, SSM_GROUP, 2 * P)
    ctr = jnp.transpose(c_re, (1, 2, 0, 3)).reshape(G, SSM_GROUP, 2 * P)
    cti = jnp.transpose(c_im, (1, 2, 0, 3)).reshape(G, SSM_GROUP, 2 * P)

    dtile = d.reshape(1, SSM_WIDTH)
    h0c = jnp.zeros((G, BATCH, 2 * P), F32)
    h0d_re = jnp.transpose(st_re, (2, 0, 1, 3)).reshape(G, DEC_BATCH, 2 * P)
    h0d_im = jnp.transpose(st_im, (2, 0, 1, 3)).reshape(G, DEC_BATCH, 2 * P)
    y2d, fin = _ssm_scan(u.reshape(N_TOK, SSM_WIDTH), rowp, btr, bti, ctr, cti, dtile,
                         h0c, h0c, h0d_re, h0d_im)
    y = y2d.reshape(2, CH_PER_SEG, N_SEG, CHUNK, SSM_WIDTH)

    a_ctx = _ctx_attention(qkv, gz)
    nseq4 = N_TOK // DEC_SEQ
    a_dec = _nbr_attention(qkv.reshape(nseq4, DEC_SEQ, qkv.shape[1]),
                           cache_k.reshape(DEC_BATCH, PAST_LEN, ATT_WIDTH),
                           cache_v.reshape(DEC_BATCH, PAST_LEN, ATT_WIDTH),
                           _nbr_bias_band(rpb.astype(F32)),
                           gz.reshape(nseq4, DEC_SEQ, gz.shape[1]))

    yp, ys_out = _back(xp, xs, y, zs, a_ctx, a_dec.reshape(N_DEC_TOK, ATT_WIDTH), gz, gate3, b_glu,
                       w_glu.astype(BF16), w_so.astype(BF16), w_ao.astype(BF16), w_o.astype(BF16))

    fin_ctx = fin[0]
    new_re = jnp.transpose(fin_ctx[:, :, :2 * P].reshape(G, BATCH, 2, P), (1, 2, 0, 3))
    new_im = jnp.transpose(fin_ctx[:, :, 2 * P:].reshape(G, BATCH, 2, P), (1, 2, 0, 3))
    return yp, ys_out, kf32, vf32, new_re, new_im


def kernel(x_prompt, x_sample, cache_k, cache_v, state_ssm_re, state_ssm_im, c, c_ctx, norm_w, w_ada, b_ada,
           w_in, q_norm_w, k_norm_w, rel_pos_bias, ssm_a_re, ssm_a_im, ssm_log_dt, ssm_b_re, ssm_b_im,
           ssm_c_re, ssm_c_im, ssm_d, w_glu, b_glu, w_ssm_out, w_att_out, w_o):
    depth = norm_w.shape[0]
    xp = x_prompt.reshape(N_CTX_TOK, D_MODEL)
    xs = x_sample.reshape(N_DEC_TOK, D_MODEL)
    new_k, new_v, new_re, new_im = [], [], [], []
    for l in range(depth):
        xp, xs, kl, vl, rl, il = _layer(
            xp, xs, cache_k[:, l], cache_v[:, l], state_ssm_re[:, l], state_ssm_im[:, l], c, c_ctx,
            norm_w[l], w_ada[l], b_ada[l], w_in[l], q_norm_w[l], k_norm_w[l], rel_pos_bias[l],
            ssm_a_re[l], ssm_a_im[l], ssm_log_dt[l], ssm_b_re[l], ssm_b_im[l], ssm_c_re[l], ssm_c_im[l],
            ssm_d[l], w_glu[l], b_glu[l], w_ssm_out[l], w_att_out[l], w_o[l])
        new_k.append(kl.reshape(BATCH, SEQ, N_HEADS, HEAD_DIM))
        new_v.append(vl.reshape(BATCH, SEQ, N_HEADS, HEAD_DIM))
        new_re.append(rl)
        new_im.append(il)
    return (xp.reshape(BATCH, SEQ, D_MODEL), xs.reshape(DEC_BATCH, DEC_SEQ, D_MODEL),
            jnp.stack(new_k, axis=1), jnp.stack(new_v, axis=1),
            jnp.stack(new_re, axis=1), jnp.stack(new_im, axis=1))
```

```python
import functools
import math

import jax
import jax.numpy as jnp
from jax import lax
from jax.experimental import pallas as pl
from jax.experimental.pallas import tpu as pltpu

D_MODEL = 2048
BATCH = 16
SEQ = 256
DEC_BATCH = 2
DEC_SEQ = 2048
PAST_LEN = 512
GRID_W = 64
GRID_H = DEC_SEQ // GRID_W
SSM_WIDTH = D_MODEL // 2
SSM_GROUP = 16
SSM_GROUPS = SSM_WIDTH // SSM_GROUP
SSM_STATE = 64
N_HEADS = 16
HEAD_DIM = 64
ATT_WIDTH = N_HEADS * HEAD_DIM
WIN_ROWS = 8
WIN_COLS = 16
EPS = 1e-6
NEG_INF = -1e30

N_CTX_TOK = BATCH * SEQ
N_DEC_TOK = DEC_BATCH * DEC_SEQ
N_TOK = N_CTX_TOK + N_DEC_TOK

OFF_U = 0
OFF_ZS = SSM_WIDTH
OFF_Q = 2 * SSM_WIDTH
OFF_K = OFF_Q + ATT_WIDTH
OFF_V = OFF_K + ATT_WIDTH
OFF_ZA = OFF_V + ATT_WIDTH
OFF_GS = OFF_ZA + ATT_WIDTH
OFF_GA = OFF_GS + D_MODEL

CHUNK = 16
FLAT = CHUNK * SSM_GROUP
N_SEG = 16
CH_PER_SEG = 16
DSTATE = 2 * SSM_STATE

Q_ROWS_PER_BLOCK = 4
NBR_Q = Q_ROWS_PER_BLOCK * GRID_W
NBR_KROWS = 12
NBR_K = NBR_KROWS * GRID_W

VMEM_LIMIT = 56 * 1024 * 1024
QKV_VMEM_LIMIT = 60 * 1024 * 1024

F32 = jnp.float32
BF16 = jnp.bfloat16
HIGHEST = lax.Precision.HIGHEST


def _sigmoid(x):
    return 1.0 / (1.0 + jnp.exp(-x))


def _silu(x):
    return x * _sigmoid(x)


def _gelu_tanh(x):
    return 0.5 * x * (1.0 + jnp.tanh(math.sqrt(2.0 / math.pi) * (x + 0.044715 * (x * x * x))))


def _cmul(ar, ai, br, bi):
    return ar * br - ai * bi, ar * bi + ai * br


def _dot_nt(a, b):
    return lax.dot_general(a, b, (((1,), (1,)), ((), ())), preferred_element_type=F32)


def _params(*sem, vmem_limit=VMEM_LIMIT):
    return pltpu.CompilerParams(dimension_semantics=sem, vmem_limit_bytes=vmem_limit)


def _mod_kernel(cond_ref, w_ref, b_ref, o_ref):
    c = cond_ref[...]
    s = _silu(c).astype(BF16)
    o_ref[...] = jnp.dot(s, w_ref[...].astype(BF16), preferred_element_type=F32) + b_ref[...]


def _modulation(cond8, w_ada, b_ada):
    tn = 768
    n = w_ada.shape[1]
    return pl.pallas_call(
        _mod_kernel,
        grid=(n // tn,),
        in_specs=[pl.BlockSpec((8, D_MODEL), lambda j: (0, 0)),
                  pl.BlockSpec((D_MODEL, tn), lambda j: (0, j)),
                  pl.BlockSpec((1, tn), lambda j: (0, j))],
        out_specs=pl.BlockSpec((8, tn), lambda j: (0, j)),
        out_shape=jax.ShapeDtypeStruct((8, n), F32),
        compiler_params=_params("arbitrary"),
        name="modulation",
    )(cond8, w_ada, b_ada.reshape(1, n))


NORM_TM = 512
NORM_ROWS = 16
NORM_UNROLL = 8
NORM_CTX_STEPS = N_CTX_TOK // NORM_TM
NORM_STEPS_PER_DEC_SEQ = DEC_SEQ // NORM_TM


def _mod_row(i, ctx_steps, steps_per_seq):
    return jnp.where(i < ctx_steps, 0, 1 + (i - ctx_steps) // steps_per_seq)


def _norm_kernel(xp_ref, xs_ref, shift_ref, scale_ref, nw_ref, o_ref):
    i = pl.program_id(0)

    gain = nw_ref[...] * (1.0 + scale_ref[0])
    shift = shift_ref[0]

    def body(x_ref):
        def rows(r, carry):
            sl = pl.ds(pl.multiple_of(r * NORM_ROWS, NORM_ROWS), NORM_ROWS)
            x = x_ref[sl, :]
            ms = jnp.mean(x * x, axis=-1, keepdims=True)
            o_ref[sl, :] = (x * lax.rsqrt(ms + EPS) * gain + shift).astype(BF16)
            return carry

        lax.fori_loop(0, NORM_TM // NORM_ROWS, rows, 0, unroll=NORM_UNROLL)

    @pl.when(i < NORM_CTX_STEPS)
    def _():
        body(xp_ref)

    @pl.when(i >= NORM_CTX_STEPS)
    def _():
        body(xs_ref)


def _norm_modulate(xp, xs, shift3, scale3, norm_w):
    steps = N_TOK // NORM_TM
    row = functools.partial(_mod_row, ctx_steps=NORM_CTX_STEPS, steps_per_seq=NORM_STEPS_PER_DEC_SEQ)
    return pl.pallas_call(
        _norm_kernel,
        grid=(steps,),
        in_specs=[pl.BlockSpec((NORM_TM, D_MODEL), lambda i: (jnp.minimum(i, NORM_CTX_STEPS - 1), 0)),
                  pl.BlockSpec((NORM_TM, D_MODEL), lambda i: (jnp.maximum(i - NORM_CTX_STEPS, 0), 0)),
                  pl.BlockSpec((1, 1, D_MODEL), lambda i: (row(i), 0, 0)),
                  pl.BlockSpec((1, 1, D_MODEL), lambda i: (row(i), 0, 0)),
                  pl.BlockSpec((1, D_MODEL), lambda i: (0, 0))],
        out_specs=pl.BlockSpec((NORM_TM, D_MODEL), lambda i: (i, 0)),
        out_shape=jax.ShapeDtypeStruct((N_TOK, D_MODEL), BF16),
        compiler_params=_params("arbitrary"),
        name="norm_modulate",
    )(xp, xs, shift3, scale3, norm_w.reshape(1, D_MODEL))


PROJ_TM = 1024
PROJ_TN = 1024
PROJ_CTX_STEPS = N_CTX_TOK // PROJ_TM
PROJ_SEGS = PROJ_TM // (CHUNK * CH_PER_SEG)


def _head_group_ones():
    r = lax.broadcasted_iota(jnp.int32, (256, 256), 0) // HEAD_DIM
    c = lax.broadcasted_iota(jnp.int32, (256, 256), 1) // HEAD_DIM
    return jnp.where(r == c, 1.0, 0.0).astype(BF16)


def _head_rms(acc, nw):
    ones = _head_group_ones()
    outs = []
    for c in range(acc.shape[1] // 256):
        a = acc[:, c * 256:(c + 1) * 256]
        ssum = jnp.dot((a * a).astype(BF16), ones, preferred_element_type=F32)
        outs.append(a * lax.rsqrt(ssum * (1.0 / HEAD_DIM) + EPS))
    return jnp.concatenate(outs, axis=1) * nw


def _proj_tile(h_ref, w_ref, wbf):
    @pl.when(pl.program_id(1) == 0)
    def _():
        wbf[...] = w_ref[...].astype(BF16)

    return jnp.dot(h_ref[...], wbf[...], preferred_element_type=F32)


def _proj_s5_kernel(h_ref, w_ref, u_ref, zs_ref, wbf):
    j = pl.program_id(0)
    acc = _proj_tile(h_ref, w_ref, wbf)

    @pl.when(j == 0)
    def _():
        for seg in range(PROJ_SEGS):
            for ch in range(CH_PER_SEG):
                r0 = (seg * CH_PER_SEG + ch) * CHUNK
                u_ref[0, ch, seg] = acc[r0:r0 + CHUNK, :]

    @pl.when(j == 1)
    def _():
        zs_ref[...] = acc.astype(BF16)


def _proj_qkv_kernel(h_ref, w_ref, nw_ref, o_ref, kf_ref, vf_ref, wbf):
    j = pl.program_id(0)
    i = pl.program_id(1)
    acc = _proj_tile(h_ref, w_ref, wbf)

    @pl.when(j == 0)
    def _():
        o_ref[...] = (_head_rms(acc, nw_ref[0]) * (HEAD_DIM ** -0.5)).astype(BF16)

    @pl.when(j == 1)
    def _():
        kn = _head_rms(acc, nw_ref[0])
        o_ref[...] = kn.astype(BF16)

        @pl.when(i < PROJ_CTX_STEPS)
        def _():
            kf_ref[...] = kn

    @pl.when(j == 2)
    def _():
        o_ref[...] = acc.astype(BF16)

        @pl.when(i < PROJ_CTX_STEPS)
        def _():
            vf_ref[...] = acc


def _in_proj(h, w_in, q_norm_w, k_norm_w):
    ni = N_TOK // PROJ_TM
    last = ni - 1
    ctx_last = PROJ_CTX_STEPS - 1
    tiles_per_path = N_SEG // PROJ_SEGS
    h_spec = pl.BlockSpec((PROJ_TM, D_MODEL), lambda j, i: (i, 0))
    scratch = [pltpu.VMEM((D_MODEL, PROJ_TN), BF16)]
    params = _params("arbitrary", "arbitrary")

    def u_map(j, i):
        ii = jnp.where(j == 0, i, last)
        return (ii // tiles_per_path, 0, ii % tiles_per_path, 0, 0)

    u, zs = pl.pallas_call(
        _proj_s5_kernel,
        grid=(2, ni),
        in_specs=[h_spec, pl.BlockSpec((D_MODEL, PROJ_TN), lambda j, i: (0, OFF_U // PROJ_TN + j))],
        out_specs=[pl.BlockSpec((1, CH_PER_SEG, PROJ_SEGS, CHUNK, PROJ_TN), u_map),
                   pl.BlockSpec((PROJ_TM, PROJ_TN), lambda j, i: (jnp.where(j == 0, 0, i), 0))],
        out_shape=[jax.ShapeDtypeStruct((2, CH_PER_SEG, N_SEG, CHUNK, SSM_WIDTH), F32),
                   jax.ShapeDtypeStruct((N_TOK, SSM_WIDTH), BF16)],
        scratch_shapes=scratch, compiler_params=params, name="in_proj_s5",
    )(h, w_in)

    nw = jnp.stack([jnp.tile(q_norm_w.reshape(1, HEAD_DIM), (1, PROJ_TN // HEAD_DIM)),
                    jnp.tile(k_norm_w.reshape(1, HEAD_DIM), (1, PROJ_TN // HEAD_DIM))])
    qkv, kf32, vf32 = pl.pallas_call(
        _proj_qkv_kernel,
        grid=(3, ni),
        in_specs=[h_spec, pl.BlockSpec((D_MODEL, PROJ_TN), lambda j, i: (0, OFF_Q // PROJ_TN + j)),
                  pl.BlockSpec((1, 1, PROJ_TN), lambda j, i: (jnp.where(j == 1, 1, 0), 0, 0))],
        out_specs=[pl.BlockSpec((PROJ_TM, PROJ_TN), lambda j, i: (i, j)),
                   pl.BlockSpec((PROJ_TM, PROJ_TN),
                                lambda j, i: (jnp.where(j == 0, 0, jnp.where(j == 1, jnp.minimum(i, ctx_last),
                                                                             ctx_last)), 0)),
                   pl.BlockSpec((PROJ_TM, PROJ_TN),
                                lambda j, i: (jnp.where(j == 2, jnp.minimum(i, ctx_last), 0), 0))],
        out_shape=[jax.ShapeDtypeStruct((N_TOK, 3 * ATT_WIDTH), BF16),
                   jax.ShapeDtypeStruct((N_CTX_TOK, ATT_WIDTH), F32),
                   jax.ShapeDtypeStruct((N_CTX_TOK, ATT_WIDTH), F32)],
        scratch_shapes=scratch, name="in_proj_qkv",
        compiler_params=_params("arbitrary", "arbitrary", vmem_limit=QKV_VMEM_LIMIT),
    )(h, w_in, nw)

    return u, zs, qkv, kf32, vf32


def _pow_select(e, pows):
    rr = jnp.where((e & 1) != 0, pows[0][0], 1.0)
    ri = jnp.where((e & 1) != 0, pows[0][1], 0.0)
    for b in range(1, len(pows)):
        bit = (e & (1 << b)) != 0
        fr = jnp.where(bit, pows[b][0], 1.0)
        fi = jnp.where(bit, pows[b][1], 0.0)
        rr, ri = _cmul(rr, ri, fr, fi)
    return rr, ri


def _discretize(a_re, a_im, log_dt):
    lr = jnp.minimum(a_re, -1e-4)
    li = a_im
    dt = jnp.exp(log_dt)
    mag = jnp.exp(lr * dt)
    br = mag * jnp.cos(li * dt)
    bi = mag * jnp.sin(li * dt)
    den = lr * lr + li * li
    nr = br - 1.0
    cr = (nr * lr + bi * li) / den
    ci = (bi * lr - nr * li) / den
    return (br, bi), (cr, ci)


def _squarings(pr, pi, n):
    out = [(pr, pi)]
    for _ in range(n):
        pr, pi = _cmul(pr, pi, pr, pi)
        out.append((pr, pi))
    return out


def _dot_nt_exact(a, b):
    return lax.dot_general(a, b, (((1,), (1,)), ((), ())), preferred_element_type=F32, precision=HIGHEST)


def _ssm_build_ops(n_groups, rowp_ref, btr_ref, bti_ref, ctr_ref, cti_ref, t_ref, s_ref, rt_ref, lam_ref, powc_ref):
    lane_b = lax.broadcasted_iota(jnp.int32, (SSM_GROUP, DSTATE), 1)
    lane_t = lax.broadcasted_iota(jnp.int32, (SSM_GROUP, FLAT), 1)
    row_c = lax.broadcasted_iota(jnp.int32, (CH_PER_SEG, DSTATE), 0)
    lane_c = lax.broadcasted_iota(jnp.int32, (CH_PER_SEG, DSTATE), 1)
    exp_c = jnp.where(lane_c < SSM_STATE, row_c, (CH_PER_SEG - 1) - row_c)

    for g in range(n_groups):
        rp = rowp_ref[g]
        (lbr, lbi), (cfr, cfi) = _discretize(rp[0:1], rp[1:2], rp[2:3])
        sq = _squarings(lbr, lbi, 8)
        bbr, bbi = _cmul(btr_ref[g], bti_ref[g], cfr, cfi)

        ptr, pti = _pow_select(exp_c, sq[:4])

        s_re, s_im = [], []
        for sp in range(CHUNK):
            e = CHUNK - 1 - sp
            br, bi = _cmul(bbr, bbi, ptr[e:e + 1], pti[e:e + 1])
            s_re.append(br)
            s_im.append(bi)
        s_ref[g] = jnp.concatenate([jnp.concatenate(s_re, axis=0), jnp.concatenate(s_im, axis=0)],
                                   axis=1).astype(BF16)

        xcr = jnp.concatenate([ctr_ref[g]] * CHUNK, axis=0)
        xci = jnp.concatenate([cti_ref[g]] * CHUNK, axis=0)
        xpr = jnp.broadcast_to(ptr[:, None, :], (CHUNK, SSM_GROUP, DSTATE)).reshape(FLAT, DSTATE)
        xpi = jnp.broadcast_to(pti[:, None, :], (CHUNK, SSM_GROUP, DSTATE)).reshape(FLAT, DSTATE)
        ykr, yki = _cmul(xcr, xci, xpr, xpi)
        yrr, yri = _cmul(ykr, yki, lbr, lbi)
        rt_ref[g] = jnp.concatenate([yrr, -yri], axis=1).astype(BF16)

        fwd = lane_b < SSM_STATE
        lhs = jnp.concatenate([jnp.where(fwd, bbr, 0.0), jnp.where(fwd, bbi, 0.0),
                               jnp.where(fwd, 0.0, bbr), jnp.where(fwd, 0.0, bbi)], axis=0)
        p1 = _dot_nt_exact(lhs, ykr)
        p2 = _dot_nt_exact(lhs, yki)
        kf = p1[0:16] - p2[16:32]
        kb = p1[32:48] - p2[48:64]
        for sp in range(CHUNK):
            tf = kf if sp == 0 else pltpu.roll(kf, SSM_GROUP * sp, axis=1)
            tf = jnp.where(lane_t >= SSM_GROUP * sp, tf, 0.0)
            shift = (FLAT - SSM_GROUP * (CHUNK - 1 - sp)) % FLAT
            tb = kb if shift == 0 else pltpu.roll(kb, shift, axis=1)
            tb = jnp.where(lane_t < SSM_GROUP * (sp + 1), tb, 0.0)
            t_ref[g, sp * SSM_GROUP:(sp + 1) * SSM_GROUP, :] = (tf + tb).astype(BF16)

        l16 = sq[4]
        l256 = sq[8]
        lam_ref[g] = jnp.concatenate([l16[0], l16[1], l256[0], l256[1],
                                      jnp.zeros((4, DSTATE), F32)], axis=0)
        q16 = _squarings(l16[0], l16[1], 3)
        pcr, pci = _pow_select(exp_c, q16)
        powc_ref[g, 0] = pcr
        powc_ref[g, 1] = pci


SSM_PATHS = ((BATCH, SEQ // (CHUNK * CH_PER_SEG)), (DEC_BATCH, DEC_SEQ // (CHUNK * CH_PER_SEG)))


def _ssm_path(u, s_op, t_op, rt_op, lam, powc_ref, h0r, h0i, nseg):
    lane = lax.broadcasted_iota(jnp.int32, (N_SEG, DSTATE), 1)
    isf = lane < SSM_STATE
    l16r, l16i, l256r, l256i = lam[0:1], lam[1:2], lam[2:3], lam[3:4]

    z = jnp.dot(u, s_op, preferred_element_type=F32)
    zre, zim = z[:, :DSTATE], z[:, DSTATE:]

    hr = jnp.zeros((N_SEG, DSTATE), F32)
    hi = jnp.zeros((N_SEG, DSTATE), F32)
    hist = []
    for k in range(CH_PER_SEG):
        hist.append((hr, hi))
        kb = CH_PER_SEG - 1 - k
        zr = jnp.where(isf, zre[k * N_SEG:(k + 1) * N_SEG], zre[kb * N_SEG:(kb + 1) * N_SEG])
        zi = jnp.where(isf, zim[k * N_SEG:(k + 1) * N_SEG], zim[kb * N_SEG:(kb + 1) * N_SEG])
        nr, ni = _cmul(l16r, l16i, hr, hi)
        hr, hi = nr + zr, ni + zi

    if nseg > 1:
        h0r = jnp.broadcast_to(h0r[:, None, :], (N_SEG // nseg, nseg, DSTATE)).reshape(N_SEG, DSTATE)
        h0i = jnp.broadcast_to(h0i[:, None, :], (N_SEG // nseg, nseg, DSTATE)).reshape(N_SEG, DSTATE)
    seg = lax.broadcasted_iota(jnp.int32, (N_SEG, DSTATE), 0) & (nseg - 1)
    segpow = _squarings(l256r, l256i, max(int(math.log2(nseg)), 0))

    def shifted(x, d):
        dn = jnp.where(seg >= d, pltpu.roll(x, d, axis=0), 0.0)
        up = jnp.where(seg <= nseg - 1 - d, pltpu.roll(x, N_SEG - d, axis=0), 0.0)
        return jnp.where(isf, dn, up)

    pr, pi = hr, hi
    d = 1
    lvl = 0
    while d < nseg:
        ar, ai = _cmul(segpow[lvl][0], segpow[lvl][1], shifted(pr, d), shifted(pi, d))
        pr, pi = pr + ar, pi + ai
        d *= 2
        lvl += 1
    if nseg > 1:
        e_in = jnp.where(isf, seg, nseg - 1 - seg)
        wr, wi = _pow_select(e_in, segpow[:lvl])
        ar, ai = _cmul(wr, wi, h0r, h0i)
        hsr, hsi = shifted(pr, 1) + ar, shifted(pi, 1) + ai
    else:
        hsr, hsi = h0r, h0i
    er, ei = _cmul(segpow[lvl][0], segpow[lvl][1], h0r, h0i)
    fin = jnp.concatenate([pr + er, pi + ei], axis=1)

    rows = []
    for c in range(CH_PER_SEG):
        cb = CH_PER_SEG - 1 - c
        lr = jnp.where(isf, hist[c][0], hist[cb][0])
        li = jnp.where(isf, hist[c][1], hist[cb][1])
        ar, ai = _cmul(powc_ref[0, c:c + 1, :], powc_ref[1, c:c + 1, :], hsr, hsi)
        rows.append(jnp.concatenate([lr + ar, li + ai], axis=1))
    hent = jnp.concatenate(rows, axis=0).astype(BF16)

    y = jnp.dot(u, t_op, preferred_element_type=F32) + _dot_nt(hent, rt_op)
    return y, fin


GROUPS_PER_STEP = 128 // SSM_GROUP


def _block_transpose(arrs, blk):
    n = len(arrs)
    width = arrs[0].shape[1]
    j = lax.broadcasted_iota(jnp.int32, arrs[0].shape, 1) // blk
    k = n // 2
    while k >= 1:
        bit = (j & k) != 0
        new = list(arrs)
        for x in range(n):
            if x & k == 0:
                a, b = arrs[x], arrs[x | k]
                new[x] = jnp.where(bit, pltpu.roll(b, k * blk, axis=1), a)
                new[x | k] = jnp.where(bit, b, pltpu.roll(a, width - k * blk, axis=1))
        arrs = new
        k //= 2
    return arrs


S5_PHASES = 5
GATE_TILES = (2 * D_MODEL + ATT_WIDTH) // PROJ_TN


def _s5_phase(phase, u_ref, rowp_ref, btr_ref, bti_ref, ctr_ref, cti_ref, d_ref, h0, y_ref, fin_ref,
              ubuf, ybuf, t_ref, s_ref, r_ref, lam_ref, powc_ref):
    nhalf = FLAT // 128
    rows = N_SEG * CH_PER_SEG

    def token_rows(path, tok):
        return pl.ds(path * rows * CHUNK + tok, rows, stride=CHUNK)

    if phase == 0:
        _ssm_build_ops(GROUPS_PER_STEP, rowp_ref, btr_ref, bti_ref, ctr_ref, cti_ref,
                       t_ref, s_ref, r_ref, lam_ref, powc_ref)
    elif phase == 1:
        for path in range(len(SSM_PATHS)):
            for half in range(nhalf):
                toks = [pltpu.bitcast(u_ref[token_rows(path, half * GROUPS_PER_STEP + sb), :].astype(BF16),
                                      jnp.uint32) for sb in range(GROUPS_PER_STEP)]
                grouped = _block_transpose(toks, SSM_GROUP)
                for g in range(GROUPS_PER_STEP):
                    ubuf[path, g, :, half * 128:(half + 1) * 128] = pltpu.bitcast(grouped[g], BF16)
    elif phase in (2, 3):
        path = phase - 2
        nseg = SSM_PATHS[path][1]
        for g in range(GROUPS_PER_STEP):
            y, fin = _ssm_path(ubuf[path, g], s_ref[g], t_ref[g], r_ref[g], lam_ref[g], powc_ref.at[g],
                               h0[path][0][g], h0[path][1][g], nseg)
            ybuf[path, g] = y
            fin_ref[path, g] = fin
    else:
        for path in range(len(SSM_PATHS)):
            for half in range(nhalf):
                grouped = [ybuf[path, g, :, half * 128:(half + 1) * 128] for g in range(GROUPS_PER_STEP)]
                toks = _block_transpose(grouped, SSM_GROUP)
                for sb in range(GROUPS_PER_STEP):
                    sel = token_rows(path, half * GROUPS_PER_STEP + sb)
                    y_ref[sel, :] = toks[sb] + u_ref[sel, :] * d_ref[...]


def _gates_s5_kernel(h_ref, w_ref, u_ref, rowp_ref, btr_ref, bti_ref, ctr_ref, cti_ref, d_ref,
                     h0cr_ref, h0ci_ref, h0dr_ref, h0di_ref, o_ref, y_ref, fin_ref,
                     wbf, ubuf, ybuf, t_ref, s_ref, r_ref, lam_ref, powc_ref):
    step = pl.program_id(0) * pl.num_programs(1) + pl.program_id(1)
    phase = step % S5_PHASES

    @pl.when(pl.program_id(1) == 0)
    def _():
        wbf[...] = w_ref[...].astype(BF16)

    h0 = ((h0cr_ref, h0ci_ref), (h0dr_ref, h0di_ref))
    for k in range(S5_PHASES):
        @pl.when(phase == k)
        def _(k=k):
            o_ref[...] = jnp.dot(h_ref[...], wbf[...], preferred_element_type=F32).astype(BF16)
            _s5_phase(k, u_ref, rowp_ref, btr_ref, bti_ref, ctr_ref, cti_ref, d_ref, h0, y_ref, fin_ref,
                      ubuf, ybuf, t_ref, s_ref, r_ref, lam_ref, powc_ref)


def _gates_and_s5(h, w_in, u2d, rowp, btr, bti, ctr, cti, dtile, h0c_re, h0c_im, h0d_re, h0d_im):
    G = SSM_GROUPS
    gs = GROUPS_PER_STEP
    rows = N_SEG * CH_PER_SEG
    ni = N_TOK // PROJ_TM
    gate_tiles = 2 * D_MODEL // PROJ_TN
    assert GATE_TILES * ni == S5_PHASES * (G // gs)

    def blk(j, i):
        return (j * ni + i) // S5_PHASES

    def grp(*tail):
        return lambda j, i: (blk(j, i),) + tail

    return pl.pallas_call(
        _gates_s5_kernel,
        grid=(GATE_TILES, ni),
        in_specs=[
            pl.BlockSpec((PROJ_TM, D_MODEL), lambda j, i: (i, 0)),
            pl.BlockSpec((D_MODEL, PROJ_TN),
                         lambda j, i: (0, jnp.where(j < gate_tiles, OFF_GS // PROJ_TN + j, OFF_ZA // PROJ_TN))),
            pl.BlockSpec((N_TOK, 128), lambda j, i: (0, blk(j, i)), pipeline_mode=pl.Buffered(1)),
            pl.BlockSpec((gs, 3, DSTATE), grp(0, 0)),
            pl.BlockSpec((gs, SSM_GROUP, DSTATE), grp(0, 0)),
            pl.BlockSpec((gs, SSM_GROUP, DSTATE), grp(0, 0)),
            pl.BlockSpec((gs, SSM_GROUP, DSTATE), grp(0, 0)),
            pl.BlockSpec((gs, SSM_GROUP, DSTATE), grp(0, 0)),
            pl.BlockSpec((1, 128), lambda j, i: (0, blk(j, i))),
            pl.BlockSpec((gs, BATCH, DSTATE), grp(0, 0)),
            pl.BlockSpec((gs, BATCH, DSTATE), grp(0, 0)),
            pl.BlockSpec((gs, DEC_BATCH, DSTATE), grp(0, 0)),
            pl.BlockSpec((gs, DEC_BATCH, DSTATE), grp(0, 0))],
        out_specs=[pl.BlockSpec((PROJ_TM, PROJ_TN), lambda j, i: (i, j)),
                   pl.BlockSpec((N_TOK, 128), lambda j, i: (0, blk(j, i))),
                   pl.BlockSpec((2, gs, N_SEG, 2 * DSTATE), lambda j, i: (0, blk(j, i), 0, 0))],
        out_shape=[jax.ShapeDtypeStruct((N_TOK, 2 * D_MODEL + ATT_WIDTH), BF16),
                   jax.ShapeDtypeStruct((N_TOK, SSM_WIDTH), F32),
                   jax.ShapeDtypeStruct((2, G, N_SEG, 2 * DSTATE), F32)],
        scratch_shapes=[pltpu.VMEM((D_MODEL, PROJ_TN), BF16),
                        pltpu.VMEM((2, gs, rows, FLAT), BF16), pltpu.VMEM((2, gs, rows, FLAT), F32),
                        pltpu.VMEM((gs, FLAT, FLAT), BF16), pltpu.VMEM((gs, FLAT, FLAT), BF16),
                        pltpu.VMEM((gs, FLAT, FLAT), BF16), pltpu.VMEM((gs, 8, DSTATE), F32),
                        pltpu.VMEM((gs, 2, CH_PER_SEG, DSTATE), F32)],
        compiler_params=_params("arbitrary", "arbitrary", vmem_limit=QKV_VMEM_LIMIT),
        name="gates_and_s5",
    )(h, w_in, u2d, rowp, btr, bti, ctr, cti, dtile, h0c_re, h0c_im, h0d_re, h0d_im)


CTX_BB = 4
HEADS_PER_STEP = 256 // HEAD_DIM
HEAD_LANES = HEADS_PER_STEP * HEAD_DIM


def _softmax_pv(scores, values):
    m = scores[0].max(axis=-1, keepdims=True)
    for s in scores[1:]:
        m = jnp.maximum(m, s.max(axis=-1, keepdims=True))
    l = None
    o = None
    for s, v in zip(scores, values):
        p = jnp.exp(s - m)
        ls = p.sum(axis=-1, keepdims=True)
        os_ = jnp.dot(p.astype(BF16), v, preferred_element_type=F32)
        l = ls if l is None else l + ls
        o = os_ if o is None else o + os_
    return o / l


def _head_of_lane():
    return lax.broadcasted_iota(jnp.int32, (1, HEAD_LANES), 1) // HEAD_DIM


def _merge_heads(outs, za):
    head = _head_of_lane()
    o = outs[-1]
    for hh in range(HEADS_PER_STEP - 2, -1, -1):
        o = jnp.where(head == hh, outs[hh], o)
    return (o * _silu(za.astype(F32))).astype(BF16)


def _ctx_attn_kernel(q_ref, k_ref, v_ref, za_ref, o_ref):
    head = _head_of_lane()
    for b in range(CTX_BB):
        sl = slice(b * SEQ, (b + 1) * SEQ)
        q, k, v = q_ref[sl, :], k_ref[sl, :], v_ref[sl, :]
        outs = []
        for hh in range(HEADS_PER_STEP):
            qh = jnp.where(head == hh, q, jnp.zeros_like(q))
            outs.append(_softmax_pv([_dot_nt(qh, k)], [v]))
        o_ref[sl, :] = _merge_heads(outs, za_ref[sl, :])


HEAD_BLOCKS = ATT_WIDTH // HEAD_LANES
QKV_Q, QKV_K, QKV_V = 0, HEAD_BLOCKS, 2 * HEAD_BLOCKS
GZ_ZA = 2 * D_MODEL // HEAD_LANES


def _ctx_attention(qkv, gz):
    rows = CTX_BB * SEQ

    def spec(first):
        return pl.BlockSpec((rows, HEAD_LANES), lambda b, hq: (b, first + hq))

    return pl.pallas_call(
        _ctx_attn_kernel,
        grid=(BATCH // CTX_BB, N_HEADS // HEADS_PER_STEP),
        in_specs=[spec(QKV_Q), spec(QKV_K), spec(QKV_V), spec(GZ_ZA)],
        out_specs=spec(0),
        out_shape=jax.ShapeDtypeStruct((N_CTX_TOK, ATT_WIDTH), BF16),
        compiler_params=_params("arbitrary", "arbitrary"),
        name="ctx_attention",
    )(qkv, qkv, qkv, gz)


N_DROW = 2 * WIN_ROWS - 1


def _nbr_build_table(band_ref, tbl):
    lane = lax.broadcasted_iota(jnp.int32, (GRID_W, 2 * GRID_W), 1)
    qc = lax.broadcasted_iota(jnp.int32, (GRID_W, 2 * GRID_W), 0)
    kc = lane & (GRID_W - 1)
    cs = jnp.clip(qc - WIN_COLS // 2, 0, GRID_W - WIN_COLS)
    valid = jnp.logical_and(kc >= cs, kc < cs + WIN_COLS)
    for hh in range(HEADS_PER_STEP):
        for dr in range(N_DROW):
            x = jnp.broadcast_to(band_ref[hh, dr:dr + 1, :], (GRID_W, 2 * GRID_W))
            lo = pltpu.roll(x, 0, axis=1, stride=1, stride_axis=0)
            hi = pltpu.roll(x, GRID_W, axis=1, stride=1, stride_axis=0)
            tbl[hh, dr] = jnp.where(valid, jnp.where(lane < GRID_W, lo, hi), NEG_INF)
        tbl[hh, N_DROW] = jnp.full((GRID_W, 2 * GRID_W), NEG_INF, F32)


NBR_KEYS = NBR_K + PAST_LEN
KEY_TILE = 256
SOFTMAX_ROWS = 64


def _nbr_block_index(a):
    ks = jnp.clip(Q_ROWS_PER_BLOCK * a - Q_ROWS_PER_BLOCK, 0, GRID_H - NBR_KROWS)
    idx = []
    for ri in range(Q_ROWS_PER_BLOCK):
        r = Q_ROWS_PER_BLOCK * a + ri
        rs = jnp.clip(r - WIN_ROWS // 2, 0, GRID_H - WIN_ROWS)
        row = []
        for kri in range(NBR_KROWS):
            kr = ks + kri
            valid = jnp.logical_and(kr >= rs, kr < rs + WIN_ROWS)
            row.append(jnp.where(valid, kr - r + WIN_ROWS - 1, N_DROW))
        idx.append(row)
    return idx


def _nbr_bias_tile(tbl, hh, idx, t):
    half0 = lax.broadcasted_iota(jnp.int32, (1, 2 * GRID_W), 1) < GRID_W
    krows = KEY_TILE // GRID_W
    rows = []
    for ri in range(Q_ROWS_PER_BLOCK):
        tiles = [jnp.where(half0, tbl[hh, idx[ri][kri]], tbl[hh, idx[ri][kri + 1]])
                 for kri in range(krows * t, krows * (t + 1), 2)]
        rows.append(jnp.concatenate(tiles, axis=1))
    return jnp.concatenate(rows, axis=0)


def _nbr_attn_kernel(q_ref, k_ref, v_ref, kc_ref, vc_ref, band_ref, za_ref, o_ref,
                     tbl, kcat, vcat, s_scr, p_scr, o_scr):
    a = pl.program_id(1)

    @pl.when(a == 0)
    def _():
        _nbr_build_table(band_ref, tbl)

    start = pl.multiple_of(jnp.clip(a * NBR_Q - NBR_Q, 0, DEC_SEQ - NBR_K), NBR_Q)
    head = _head_of_lane()
    idx = _nbr_block_index(a)
    for b in range(DEC_BATCH):
        kcat[b, 0:NBR_K, :] = k_ref[b, pl.ds(start, NBR_K), :]
        kcat[b, NBR_K:NBR_KEYS, :] = kc_ref[b].astype(BF16)
        vcat[b, 0:NBR_K, :] = v_ref[b, pl.ds(start, NBR_K), :]
        vcat[b, NBR_K:NBR_KEYS, :] = vc_ref[b].astype(BF16)

    def scores(b, hh):
        q = q_ref[b]
        qh = jnp.where(head == hh, q, jnp.zeros_like(q))
        mrun = None
        for t in range(NBR_KEYS // KEY_TILE):
            st = _dot_nt(qh, kcat[b, t * KEY_TILE:(t + 1) * KEY_TILE, :])
            if t < NBR_K // KEY_TILE:
                st = st + _nbr_bias_tile(tbl, hh, idx, t)
            s_scr[hh, :, t * KEY_TILE:(t + 1) * KEY_TILE] = st
            mt = jnp.maximum(st[:, :128], st[:, 128:])
            mrun = mt if mrun is None else jnp.maximum(mrun, mt)
        return mrun.max(axis=-1, keepdims=True)

    def attend(b, hh, m):
        ls = []
        for r0 in range(0, NBR_Q, SOFTMAX_ROWS):
            mr = m[r0:r0 + SOFTMAX_ROWS]
            lrun = jnp.zeros((SOFTMAX_ROWS, 128), F32)
            for c0 in range(0, NBR_KEYS, 128):
                p = jnp.exp(s_scr[hh, r0:r0 + SOFTMAX_ROWS, c0:c0 + 128] - mr)
                lrun = lrun + p
                p_scr[hh, r0:r0 + SOFTMAX_ROWS, c0:c0 + 128] = p.astype(BF16)
            ls.append(lrun.sum(axis=-1, keepdims=True))
        l = jnp.concatenate(ls, axis=0)
        o = jnp.dot(p_scr[hh], vcat[b], preferred_element_type=F32)
        lanes = slice(hh * HEAD_DIM, (hh + 1) * HEAD_DIM)
        o_scr[b, :, lanes] = o[:, lanes] / l

    order = [(b, hh) for b in range(DEC_BATCH) for hh in range(HEADS_PER_STEP)]
    m_cur = scores(*order[0])
    for n, (b, hh) in enumerate(order):
        m_next = scores(*order[n + 1]) if n + 1 < len(order) else None
        attend(b, hh, m_cur)
        m_cur = m_next
    for b in range(DEC_BATCH):
        o_ref[b] = (o_scr[b] * _silu(za_ref[b].astype(F32))).astype(BF16)


def _nbr_attention(qkv4, kc, vc, band, gz4):
    first = N_CTX_TOK // DEC_SEQ // DEC_BATCH
    nblk = GRID_H // Q_ROWS_PER_BLOCK
    hw = HEAD_LANES

    def qspec(col0):
        return pl.BlockSpec((DEC_BATCH, NBR_Q, hw), lambda hp, a: (first, a, col0 + hp))

    def kspec(col0):
        return pl.BlockSpec((DEC_BATCH, DEC_SEQ, hw), lambda hp, a: (first, 0, col0 + hp))

    cspec = pl.BlockSpec((DEC_BATCH, PAST_LEN, hw), lambda hp, a: (0, 0, hp))
    bspec = pl.BlockSpec((HEADS_PER_STEP, N_DROW + 1, 2 * GRID_W), lambda hp, a: (hp, 0, 0))
    return pl.pallas_call(
        _nbr_attn_kernel,
        grid=(N_HEADS // HEADS_PER_STEP, nblk),
        in_specs=[qspec(QKV_Q), kspec(QKV_K), kspec(QKV_V), cspec, cspec, bspec, qspec(GZ_ZA)],
        out_specs=pl.BlockSpec((DEC_BATCH, NBR_Q, hw), lambda hp, a: (0, a, hp)),
        out_shape=jax.ShapeDtypeStruct((DEC_BATCH, DEC_SEQ, ATT_WIDTH), BF16),
        scratch_shapes=[pltpu.VMEM((HEADS_PER_STEP, N_DROW + 1, GRID_W, 2 * GRID_W), F32),
                        pltpu.VMEM((DEC_BATCH, NBR_KEYS, HEAD_LANES), BF16),
                        pltpu.VMEM((DEC_BATCH, NBR_KEYS, HEAD_LANES), BF16),
                        pltpu.VMEM((HEADS_PER_STEP, NBR_Q, NBR_KEYS), F32),
                        pltpu.VMEM((HEADS_PER_STEP, NBR_Q, NBR_KEYS), BF16),
                        pltpu.VMEM((DEC_BATCH, NBR_Q, HEAD_LANES), F32)],
        compiler_params=_params("arbitrary", "arbitrary"),
        name="nbr_attention",
    )(qkv4, qkv4, qkv4, kc, vc, band, gz4)


def _nbr_bias_band(rpb):
    ncol = 2 * WIN_COLS - 1
    fill = jnp.full(rpb.shape[:2] + (2 * GRID_W - ncol,), NEG_INF, F32)
    band = jnp.concatenate([rpb[..., WIN_COLS - 1:], fill, rpb[..., :WIN_COLS - 1]], axis=-1)
    return jnp.pad(band, ((0, 0), (0, 1), (0, 0)))


BACK_TM = CHUNK * CH_PER_SEG
BACK_CTX_STEPS = N_CTX_TOK // BACK_TM
BACK_STEPS_PER_DEC_SEQ = DEC_SEQ // BACK_TM


def _back_kernel(xp_ref, xs_ref, y_ref, zs_ref, ac_ref, ad_ref, gs_ref, ga_ref, gate_ref, bglu_ref,
                 wglu_ref, wso_ref, wao_ref, wo_ref, op_ref, os_ref):
    i = pl.program_id(0)

    def compute(x, a2):
        ys = _gelu_tanh(y_ref[0].reshape(BACK_TM, SSM_WIDTH))
        t = jnp.dot(ys.astype(BF16), wglu_ref[...], preferred_element_type=F32) + bglu_ref[...]
        ys = ys * _sigmoid(t) * _silu(zs_ref[...].astype(F32))
        p_s = jnp.dot(ys.astype(BF16), wso_ref[...], preferred_element_type=F32)
        p_a = jnp.dot(a2, wao_ref[...], preferred_element_type=F32)
        merged = (_sigmoid(gs_ref[...].astype(F32)) * p_s + _sigmoid(ga_ref[...].astype(F32)) * p_a)
        return x + gate_ref[0] * jnp.dot(merged.astype(BF16), wo_ref[...], preferred_element_type=F32)

    @pl.when(i < BACK_CTX_STEPS)
    def _():
        op_ref[...] = compute(xp_ref[...], ac_ref[...])

    @pl.when(i >= BACK_CTX_STEPS)
    def _():
        os_ref[...] = compute(xs_ref[...], ad_ref[...])


def _back(xp, xs, y, zs, a_ctx, a_dec, gz, gate3, b_glu, w_glu, w_so, w_ao, w_o):
    steps = N_TOK // BACK_TM
    n0 = BACK_CTX_STEPS
    row = functools.partial(_mod_row, ctx_steps=n0, steps_per_seq=BACK_STEPS_PER_DEC_SEQ)
    lo = lambda i: (jnp.minimum(i, n0 - 1), 0)
    hi = lambda i: (jnp.maximum(i - n0, 0), 0)
    cur = lambda i: (i, 0)
    const = lambda i: (0, 0)
    once = pl.Buffered(1)
    return pl.pallas_call(
        _back_kernel,
        grid=(steps,),
        in_specs=[pl.BlockSpec((BACK_TM, D_MODEL), lo),
                  pl.BlockSpec((BACK_TM, D_MODEL), hi),
                  pl.BlockSpec((1, CH_PER_SEG, None, CHUNK, SSM_WIDTH),
                               lambda i: (i // N_SEG, 0, i % N_SEG, 0, 0)),
                  pl.BlockSpec((BACK_TM, SSM_WIDTH), cur),
                  pl.BlockSpec((BACK_TM, ATT_WIDTH), lo),
                  pl.BlockSpec((BACK_TM, ATT_WIDTH), hi),
                  pl.BlockSpec((BACK_TM, D_MODEL), lambda i: (i, 0)),
                  pl.BlockSpec((BACK_TM, D_MODEL), lambda i: (i, 1)),
                  pl.BlockSpec((1, 1, D_MODEL), lambda i: (row(i), 0, 0)),
                  pl.BlockSpec((1, SSM_WIDTH), const),
                  pl.BlockSpec((SSM_WIDTH, SSM_WIDTH), const, pipeline_mode=once),
                  pl.BlockSpec((SSM_WIDTH, D_MODEL), const, pipeline_mode=once),
                  pl.BlockSpec((ATT_WIDTH, D_MODEL), const, pipeline_mode=once),
                  pl.BlockSpec((D_MODEL, D_MODEL), const, pipeline_mode=once)],
        out_specs=[pl.BlockSpec((BACK_TM, D_MODEL), lo),
                   pl.BlockSpec((BACK_TM, D_MODEL), hi)],
        out_shape=[jax.ShapeDtypeStruct((N_CTX_TOK, D_MODEL), F32),
                   jax.ShapeDtypeStruct((N_DEC_TOK, D_MODEL), F32)],
        compiler_params=_params("arbitrary"),
        name="gated_output",
    )(xp, xs, y, zs, a_ctx, a_dec, gz, gz, gate3, b_glu.reshape(1, SSM_WIDTH), w_glu, w_so, w_ao, w_o)


def _layer(xp, xs, cache_k, cache_v, st_re, st_im, c, c_ctx, norm_w, w_ada, b_ada, w_in, q_norm_w, k_norm_w,
           rpb, a_re, a_im, log_dt, b_re, b_im, c_re, c_im, d, w_glu, b_glu, w_so, w_ao, w_o):
    G, P = SSM_GROUPS, SSM_STATE

    cond8 = jnp.zeros((8, D_MODEL), F32).at[0].set(c_ctx).at[1:1 + DEC_BATCH].set(c)
    mod = _modulation(cond8, w_ada, b_ada)
    shift3 = mod[:1 + DEC_BATCH, None, :D_MODEL]
    scale3 = mod[:1 + DEC_BATCH, None, D_MODEL:2 * D_MODEL]
    gate3 = mod[:1 + DEC_BATCH, None, 2 * D_MODEL:]

    h = _norm_modulate(xp, xs, shift3, scale3, norm_w)
    u, zs, qkv, kf32, vf32 = _in_proj(h, w_in, q_norm_w, k_norm_w)

    arow = jnp.transpose(a_re, (1, 0, 2)).reshape(G, 2 * P)
    airow = jnp.transpose(a_im, (1, 0, 2)).reshape(G, 2 * P)
    dtrow = jnp.broadcast_to(jnp.transpose(log_dt, (1, 0))[:, :, None], (G, 2, P)).reshape(G, 2 * P)
    rowp = jnp.stack([arow, airow, dtrow], axis=1)
    btr = jnp.transpose(b_re, (1, 3, 0, 2)).reshape(G, SSM_GROUP, 2 * P)
    bti = jnp.transpose(b_im, (1, 3, 0, 2)).reshape(G, SSM_GROUP, 2 * P)
    ctr = jnp.transpose(c_re, (1, 2, 0, 3)).reshape(G, SSM_GROUP, 2 * P)
    cti = jnp.transpose(c_im, (1, 2, 0, 3)).reshape(G, SSM_GROUP, 2 * P)

    dtile = d.reshape(1, SSM_WIDTH)
    h0c = jnp.zeros((G, BATCH, 2 * P), F32)
    h0d_re = jnp.transpose(st_re, (2, 0, 1, 3)).reshape(G, DEC_BATCH, 2 * P)
    h0d_im = jnp.transpose(st_im, (2, 0, 1, 3)).reshape(G, DEC_BATCH, 2 * P)
    gz, y2d, fin = _gates_and_s5(h, w_in, u.reshape(N_TOK, SSM_WIDTH), rowp, btr, bti, ctr, cti, dtile,
                                 h0c, h0c, h0d_re, h0d_im)
    y = y2d.reshape(2, CH_PER_SEG, N_SEG, CHUNK, SSM_WIDTH)

    a_ctx = _ctx_attention(qkv, gz)
    nseq4 = N_TOK // DEC_SEQ
    a_dec = _nbr_attention(qkv.reshape(nseq4, DEC_SEQ, qkv.shape[1]),
                           cache_k.reshape(DEC_BATCH, PAST_LEN, ATT_WIDTH),
                           cache_v.reshape(DEC_BATCH, PAST_LEN, ATT_WIDTH),
                           _nbr_bias_band(rpb.astype(F32)),
                           gz.reshape(nseq4, DEC_SEQ, gz.shape[1]))

    yp, ys_out = _back(xp, xs, y, zs, a_ctx, a_dec.reshape(N_DEC_TOK, ATT_WIDTH), gz, gate3, b_glu,
                       w_glu.astype(BF16), w_so.astype(BF16), w_ao.astype(BF16), w_o.astype(BF16))

    fin_ctx = fin[0]
    new_re = jnp.transpose(fin_ctx[:, :, :2 * P].reshape(G, BATCH, 2, P), (1, 2, 0, 3))
    new_im = jnp.transpose(fin_ctx[:, :, 2 * P:].reshape(G, BATCH, 2, P), (1, 2, 0, 3))
    return yp, ys_out, kf32, vf32, new_re, new_im


def kernel(x_prompt, x_sample, cache_k, cache_v, state_ssm_re, state_ssm_im, c, c_ctx, norm_w, w_ada, b_ada,
           w_in, q_norm_w, k_norm_w, rel_pos_bias, ssm_a_re, ssm_a_im, ssm_log_dt, ssm_b_re, ssm_b_im,
           ssm_c_re, ssm_c_im, ssm_d, w_glu, b_glu, w_ssm_out, w_att_out, w_o):
    depth = norm_w.shape[0]
    xp = x_prompt.reshape(N_CTX_TOK, D_MODEL)
    xs = x_sample.reshape(N_DEC_TOK, D_MODEL)
    new_k, new_v, new_re, new_im = [], [], [], []
    for l in range(depth):
        xp, xs, kl, vl, rl, il = _layer(
            xp, xs, cache_k[:, l], cache_v[:, l], state_ssm_re[:, l], state_ssm_im[:, l], c, c_ctx,
            norm_w[l], w_ada[l], b_ada[l], w_in[l], q_norm_w[l], k_norm_w[l], rel_pos_bias[l],
            ssm_a_re[l], ssm_a_im[l], ssm_log_dt[l], ssm_b_re[l], ssm_b_im[l], ssm_c_re[l], ssm_c_im[l],
            ssm_d[l], w_glu[l], b_glu[l], w_ssm_out[l], w_att_out[l], w_o[l])
        new_k.append(kl.reshape(BATCH, SEQ, N_HEADS, HEAD_DIM))
        new_v.append(vl.reshape(BATCH, SEQ, N_HEADS, HEAD_DIM))
        new_re.append(rl)
        new_im.append(il)
    return (xp.reshape(BATCH, SEQ, D_MODEL), xs.reshape(DEC_BATCH, DEC_SEQ, D_MODEL),
            jnp.stack(new_k, axis=1), jnp.stack(new_v, axis=1),
            jnp.stack(new_re, axis=1), jnp.stack(new_im, axis=1))
```

```python
import functools
import math

import jax
import jax.numpy as jnp
from jax import lax
from jax.experimental import pallas as pl
from jax.experimental.pallas import tpu as pltpu

D_MODEL = 2048
BATCH = 16
SEQ = 256
DEC_BATCH = 2
DEC_SEQ = 2048
PAST_LEN = 512
GRID_W = 64
GRID_H = DEC_SEQ // GRID_W
SSM_WIDTH = D_MODEL // 2
SSM_GROUP = 16
SSM_GROUPS = SSM_WIDTH // SSM_GROUP
SSM_STATE = 64
N_HEADS = 16
HEAD_DIM = 64
ATT_WIDTH = N_HEADS * HEAD_DIM
WIN_ROWS = 8
WIN_COLS = 16
EPS = 1e-6
NEG_INF = -1e30

N_CTX_TOK = BATCH * SEQ
N_DEC_TOK = DEC_BATCH * DEC_SEQ
N_TOK = N_CTX_TOK + N_DEC_TOK

OFF_U = 0
OFF_ZS = SSM_WIDTH
OFF_Q = 2 * SSM_WIDTH
OFF_K = OFF_Q + ATT_WIDTH
OFF_V = OFF_K + ATT_WIDTH
OFF_ZA = OFF_V + ATT_WIDTH
OFF_GS = OFF_ZA + ATT_WIDTH
OFF_GA = OFF_GS + D_MODEL

CHUNK = 16
FLAT = CHUNK * SSM_GROUP
N_SEG = 16
CH_PER_SEG = 16
DSTATE = 2 * SSM_STATE

Q_ROWS_PER_BLOCK = 4
NBR_Q = Q_ROWS_PER_BLOCK * GRID_W
NBR_KROWS = 12
NBR_K = NBR_KROWS * GRID_W

VMEM_LIMIT = 56 * 1024 * 1024
QKV_VMEM_LIMIT = 60 * 1024 * 1024

F32 = jnp.float32
BF16 = jnp.bfloat16
HIGHEST = lax.Precision.HIGHEST


def _sigmoid(x):
    return 1.0 / (1.0 + jnp.exp(-x))


def _silu(x):
    return x * _sigmoid(x)


def _gelu_tanh(x):
    return 0.5 * x * (1.0 + jnp.tanh(math.sqrt(2.0 / math.pi) * (x + 0.044715 * (x * x * x))))


def _cmul(ar, ai, br, bi):
    return ar * br - ai * bi, ar * bi + ai * br


def _dot_nt(a, b):
    return lax.dot_general(a, b, (((1,), (1,)), ((), ())), preferred_element_type=F32)


def _params(*sem, vmem_limit=VMEM_LIMIT):
    return pltpu.CompilerParams(dimension_semantics=sem, vmem_limit_bytes=vmem_limit)


MOD_TN = 1536


def _mod_kernel(cond_ref, w_ref, b_ref, o_ref):
    c = cond_ref[...]
    s = _silu(c).astype(BF16)
    o_ref[...] = jnp.dot(s, w_ref[...].astype(BF16), preferred_element_type=F32) + b_ref[...]


def _modulation(cond8, w_ada, b_ada):
    tn = MOD_TN
    n = w_ada.shape[1]
    return pl.pallas_call(
        _mod_kernel,
        grid=(n // tn,),
        in_specs=[pl.BlockSpec((8, D_MODEL), lambda j: (0, 0)),
                  pl.BlockSpec((D_MODEL, tn), lambda j: (0, j)),
                  pl.BlockSpec((1, tn), lambda j: (0, j))],
        out_specs=pl.BlockSpec((8, tn), lambda j: (0, j)),
        out_shape=jax.ShapeDtypeStruct((8, n), F32),
        compiler_params=_params("arbitrary"),
        name="modulation",
    )(cond8, w_ada, b_ada.reshape(1, n))


NORM_TM = 1024
NORM_ROWS = 16
NORM_UNROLL = 8
NORM_CTX_STEPS = N_CTX_TOK // NORM_TM
NORM_STEPS_PER_DEC_SEQ = DEC_SEQ // NORM_TM


def _mod_row(i, ctx_steps, steps_per_seq):
    return jnp.where(i < ctx_steps, 0, 1 + (i - ctx_steps) // steps_per_seq)


def _norm_kernel(xp_ref, xs_ref, shift_ref, scale_ref, nw_ref, o_ref):
    i = pl.program_id(0)

    gain = nw_ref[...] * (1.0 + scale_ref[0])
    shift = shift_ref[0]

    def body(x_ref):
        def rows(r, carry):
            sl = pl.ds(pl.multiple_of(r * NORM_ROWS, NORM_ROWS), NORM_ROWS)
            x = x_ref[sl, :]
            ms = jnp.mean(x * x, axis=-1, keepdims=True)
            o_ref[sl, :] = (x * lax.rsqrt(ms + EPS) * gain + shift).astype(BF16)
            return carry

        lax.fori_loop(0, NORM_TM // NORM_ROWS, rows, 0, unroll=NORM_UNROLL)

    @pl.when(i < NORM_CTX_STEPS)
    def _():
        body(xp_ref)

    @pl.when(i >= NORM_CTX_STEPS)
    def _():
        body(xs_ref)


def _norm_modulate(xp, xs, shift3, scale3, norm_w):
    steps = N_TOK // NORM_TM
    row = functools.partial(_mod_row, ctx_steps=NORM_CTX_STEPS, steps_per_seq=NORM_STEPS_PER_DEC_SEQ)
    return pl.pallas_call(
        _norm_kernel,
        grid=(steps,),
        in_specs=[pl.BlockSpec((NORM_TM, D_MODEL), lambda i: (jnp.minimum(i, NORM_CTX_STEPS - 1), 0)),
                  pl.BlockSpec((NORM_TM, D_MODEL), lambda i: (jnp.maximum(i - NORM_CTX_STEPS, 0), 0)),
                  pl.BlockSpec((1, 1, D_MODEL), lambda i: (row(i), 0, 0)),
                  pl.BlockSpec((1, 1, D_MODEL), lambda i: (row(i), 0, 0)),
                  pl.BlockSpec((1, D_MODEL), lambda i: (0, 0))],
        out_specs=pl.BlockSpec((NORM_TM, D_MODEL), lambda i: (i, 0)),
        out_shape=jax.ShapeDtypeStruct((N_TOK, D_MODEL), BF16),
        compiler_params=_params("arbitrary"),
        name="norm_modulate",
    )(xp, xs, shift3, scale3, norm_w.reshape(1, D_MODEL))


PROJ_TM = 1024
PROJ_TN = 1024
PROJ_CTX_STEPS = N_CTX_TOK // PROJ_TM
PROJ_SEGS = PROJ_TM // (CHUNK * CH_PER_SEG)


def _head_group_ones():
    r = lax.broadcasted_iota(jnp.int32, (256, 256), 0) // HEAD_DIM
    c = lax.broadcasted_iota(jnp.int32, (256, 256), 1) // HEAD_DIM
    return jnp.where(r == c, 1.0, 0.0).astype(BF16)


def _head_rms(acc, nw):
    ones = _head_group_ones()
    outs = []
    for c in range(acc.shape[1] // 256):
        a = acc[:, c * 256:(c + 1) * 256]
        ssum = jnp.dot((a * a).astype(BF16), ones, preferred_element_type=F32)
        outs.append(a * lax.rsqrt(ssum * (1.0 / HEAD_DIM) + EPS))
    return jnp.concatenate(outs, axis=1) * nw


def _proj_tile(h_ref, w_ref, wbf):
    @pl.when(pl.program_id(1) == 0)
    def _():
        wbf[...] = w_ref[...].astype(BF16)

    return jnp.dot(h_ref[...], wbf[...], preferred_element_type=F32)


def _proj_s5_kernel(h_ref, w_ref, u_ref, zs_ref, wbf):
    j = pl.program_id(0)
    acc = _proj_tile(h_ref, w_ref, wbf)

    @pl.when(j == 0)
    def _():
        for seg in range(PROJ_SEGS):
            for ch in range(CH_PER_SEG):
                r0 = (seg * CH_PER_SEG + ch) * CHUNK
                u_ref[0, ch, seg] = acc[r0:r0 + CHUNK, :]

    @pl.when(j == 1)
    def _():
        zs_ref[...] = acc.astype(BF16)


def _proj_qkv_kernel(h_ref, w_ref, nw_ref, o_ref, kf_ref, vf_ref, wbf):
    j = pl.program_id(0)
    i = pl.program_id(1)
    acc = _proj_tile(h_ref, w_ref, wbf)

    @pl.when(j == 0)
    def _():
        o_ref[...] = (_head_rms(acc, nw_ref[0]) * (HEAD_DIM ** -0.5)).astype(BF16)

    @pl.when(j == 1)
    def _():
        kn = _head_rms(acc, nw_ref[0])
        o_ref[...] = kn.astype(BF16)

        @pl.when(i < PROJ_CTX_STEPS)
        def _():
            kf_ref[...] = kn

    @pl.when(j == 2)
    def _():
        o_ref[...] = acc.astype(BF16)

        @pl.when(i < PROJ_CTX_STEPS)
        def _():
            vf_ref[...] = acc


def _proj_gate_kernel(h_ref, w_ref, o_ref, wbf):
    o_ref[...] = _proj_tile(h_ref, w_ref, wbf).astype(BF16)


def _in_proj(h, w_in, q_norm_w, k_norm_w):
    ni = N_TOK // PROJ_TM
    last = ni - 1
    ctx_last = PROJ_CTX_STEPS - 1
    tiles_per_path = N_SEG // PROJ_SEGS
    h_spec = pl.BlockSpec((PROJ_TM, D_MODEL), lambda j, i: (i, 0))
    scratch = [pltpu.VMEM((D_MODEL, PROJ_TN), BF16)]
    params = _params("arbitrary", "arbitrary")

    def u_map(j, i):
        ii = jnp.where(j == 0, i, last)
        return (ii // tiles_per_path, 0, ii % tiles_per_path, 0, 0)

    u, zs = pl.pallas_call(
        _proj_s5_kernel,
        grid=(2, ni),
        in_specs=[h_spec, pl.BlockSpec((D_MODEL, PROJ_TN), lambda j, i: (0, OFF_U // PROJ_TN + j))],
        out_specs=[pl.BlockSpec((1, CH_PER_SEG, PROJ_SEGS, CHUNK, PROJ_TN), u_map),
                   pl.BlockSpec((PROJ_TM, PROJ_TN), lambda j, i: (jnp.where(j == 0, 0, i), 0))],
        out_shape=[jax.ShapeDtypeStruct((2, CH_PER_SEG, N_SEG, CHUNK, SSM_WIDTH), F32),
                   jax.ShapeDtypeStruct((N_TOK, SSM_WIDTH), BF16)],
        scratch_shapes=scratch, compiler_params=params, name="in_proj_s5",
    )(h, w_in)

    nw = jnp.stack([jnp.tile(q_norm_w.reshape(1, HEAD_DIM), (1, PROJ_TN // HEAD_DIM)),
                    jnp.tile(k_norm_w.reshape(1, HEAD_DIM), (1, PROJ_TN // HEAD_DIM))])
    qkv, kf32, vf32 = pl.pallas_call(
        _proj_qkv_kernel,
        grid=(3, ni),
        in_specs=[h_spec, pl.BlockSpec((D_MODEL, PROJ_TN), lambda j, i: (0, OFF_Q // PROJ_TN + j)),
                  pl.BlockSpec((1, 1, PROJ_TN), lambda j, i: (jnp.where(j == 1, 1, 0), 0, 0))],
        out_specs=[pl.BlockSpec((PROJ_TM, PROJ_TN), lambda j, i: (i, j)),
                   pl.BlockSpec((PROJ_TM, PROJ_TN),
                                lambda j, i: (jnp.where(j == 0, 0, jnp.where(j == 1, jnp.minimum(i, ctx_last),
                                                                             ctx_last)), 0)),
                   pl.BlockSpec((PROJ_TM, PROJ_TN),
                                lambda j, i: (jnp.where(j == 2, jnp.minimum(i, ctx_last), 0), 0))],
        out_shape=[jax.ShapeDtypeStruct((N_TOK, 3 * ATT_WIDTH), BF16),
                   jax.ShapeDtypeStruct((N_CTX_TOK, ATT_WIDTH), F32),
                   jax.ShapeDtypeStruct((N_CTX_TOK, ATT_WIDTH), F32)],
        scratch_shapes=scratch, name="in_proj_qkv",
        compiler_params=_params("arbitrary", "arbitrary", vmem_limit=QKV_VMEM_LIMIT),
    )(h, w_in, nw)

    gate_tiles = 2 * D_MODEL // PROJ_TN
    gz = pl.pallas_call(
        _proj_gate_kernel,
        grid=(gate_tiles + 1, ni),
        in_specs=[h_spec,
                  pl.BlockSpec((D_MODEL, PROJ_TN),
                               lambda j, i: (0, jnp.where(j < gate_tiles, OFF_GS // PROJ_TN + j,
                                                          OFF_ZA // PROJ_TN)))],
        out_specs=pl.BlockSpec((PROJ_TM, PROJ_TN), lambda j, i: (i, j)),
        out_shape=jax.ShapeDtypeStruct((N_TOK, 2 * D_MODEL + ATT_WIDTH), BF16),
        scratch_shapes=scratch, compiler_params=params, name="in_proj_gates",
    )(h, w_in)
    return u, zs, qkv, kf32, vf32, gz


def _pow_select(e, pows):
    rr = jnp.where((e & 1) != 0, pows[0][0], 1.0)
    ri = jnp.where((e & 1) != 0, pows[0][1], 0.0)
    for b in range(1, len(pows)):
        bit = (e & (1 << b)) != 0
        fr = jnp.where(bit, pows[b][0], 1.0)
        fi = jnp.where(bit, pows[b][1], 0.0)
        rr, ri = _cmul(rr, ri, fr, fi)
    return rr, ri


def _discretize(a_re, a_im, log_dt):
    lr = jnp.minimum(a_re, -1e-4)
    li = a_im
    dt = jnp.exp(log_dt)
    mag = jnp.exp(lr * dt)
    br = mag * jnp.cos(li * dt)
    bi = mag * jnp.sin(li * dt)
    den = lr * lr + li * li
    nr = br - 1.0
    cr = (nr * lr + bi * li) / den
    ci = (bi * lr - nr * li) / den
    return (br, bi), (cr, ci)


def _squarings(pr, pi, n):
    out = [(pr, pi)]
    for _ in range(n):
        pr, pi = _cmul(pr, pi, pr, pi)
        out.append((pr, pi))
    return out


def _dot_nt_exact(a, b):
    return lax.dot_general(a, b, (((1,), (1,)), ((), ())), preferred_element_type=F32, precision=HIGHEST)


def _ssm_build_ops(n_groups, rowp_ref, btr_ref, bti_ref, ctr_ref, cti_ref, t_ref, s_ref, rt_ref, lam_ref, powc_ref):
    lane_b = lax.broadcasted_iota(jnp.int32, (SSM_GROUP, DSTATE), 1)
    lane_t = lax.broadcasted_iota(jnp.int32, (SSM_GROUP, FLAT), 1)
    row_c = lax.broadcasted_iota(jnp.int32, (CH_PER_SEG, DSTATE), 0)
    lane_c = lax.broadcasted_iota(jnp.int32, (CH_PER_SEG, DSTATE), 1)
    exp_c = jnp.where(lane_c < SSM_STATE, row_c, (CH_PER_SEG - 1) - row_c)

    for g in range(n_groups):
        rp = rowp_ref[g]
        (lbr, lbi), (cfr, cfi) = _discretize(rp[0:1], rp[1:2], rp[2:3])
        sq = _squarings(lbr, lbi, 8)
        bbr, bbi = _cmul(btr_ref[g], bti_ref[g], cfr, cfi)

        ptr, pti = _pow_select(exp_c, sq[:4])

        s_re, s_im = [], []
        for sp in range(CHUNK):
            e = CHUNK - 1 - sp
            br, bi = _cmul(bbr, bbi, ptr[e:e + 1], pti[e:e + 1])
            s_re.append(br)
            s_im.append(bi)
        s_ref[g] = jnp.concatenate([jnp.concatenate(s_re, axis=0), jnp.concatenate(s_im, axis=0)],
                                   axis=1).astype(BF16)

        xcr = jnp.concatenate([ctr_ref[g]] * CHUNK, axis=0)
        xci = jnp.concatenate([cti_ref[g]] * CHUNK, axis=0)
        xpr = jnp.broadcast_to(ptr[:, None, :], (CHUNK, SSM_GROUP, DSTATE)).reshape(FLAT, DSTATE)
        xpi = jnp.broadcast_to(pti[:, None, :], (CHUNK, SSM_GROUP, DSTATE)).reshape(FLAT, DSTATE)
        ykr, yki = _cmul(xcr, xci, xpr, xpi)
        yrr, yri = _cmul(ykr, yki, lbr, lbi)
        rt_ref[g] = jnp.concatenate([yrr, -yri], axis=1).astype(BF16)

        fwd = lane_b < SSM_STATE
        lhs = jnp.concatenate([jnp.where(fwd, bbr, 0.0), jnp.where(fwd, bbi, 0.0),
                               jnp.where(fwd, 0.0, bbr), jnp.where(fwd, 0.0, bbi)], axis=0)
        p1 = _dot_nt_exact(lhs, ykr)
        p2 = _dot_nt_exact(lhs, yki)
        kf = p1[0:16] - p2[16:32]
        kb = p1[32:48] - p2[48:64]
        for sp in range(CHUNK):
            tf = kf if sp == 0 else pltpu.roll(kf, SSM_GROUP * sp, axis=1)
            tf = jnp.where(lane_t >= SSM_GROUP * sp, tf, 0.0)
            shift = (FLAT - SSM_GROUP * (CHUNK - 1 - sp)) % FLAT
            tb = kb if shift == 0 else pltpu.roll(kb, shift, axis=1)
            tb = jnp.where(lane_t < SSM_GROUP * (sp + 1), tb, 0.0)
            t_ref[g, sp * SSM_GROUP:(sp + 1) * SSM_GROUP, :] = (tf + tb).astype(BF16)

        l16 = sq[4]
        l256 = sq[8]
        lam_ref[g] = jnp.concatenate([l16[0], l16[1], l256[0], l256[1],
                                      jnp.zeros((4, DSTATE), F32)], axis=0)
        q16 = _squarings(l16[0], l16[1], 3)
        pcr, pci = _pow_select(exp_c, q16)
        powc_ref[g, 0] = pcr
        powc_ref[g, 1] = pci


SSM_PATHS = ((BATCH, SEQ // (CHUNK * CH_PER_SEG)), (DEC_BATCH, DEC_SEQ // (CHUNK * CH_PER_SEG)))


def _ssm_path(u, s_op, t_op, rt_op, lam, powc_ref, h0r, h0i, nseg):
    lane = lax.broadcasted_iota(jnp.int32, (N_SEG, DSTATE), 1)
    isf = lane < SSM_STATE
    l16r, l16i, l256r, l256i = lam[0:1], lam[1:2], lam[2:3], lam[3:4]

    z = jnp.dot(u, s_op, preferred_element_type=F32)
    zre, zim = z[:, :DSTATE], z[:, DSTATE:]

    hr = jnp.zeros((N_SEG, DSTATE), F32)
    hi = jnp.zeros((N_SEG, DSTATE), F32)
    hist = []
    for k in range(CH_PER_SEG):
        hist.append((hr, hi))
        kb = CH_PER_SEG - 1 - k
        zr = jnp.where(isf, zre[k * N_SEG:(k + 1) * N_SEG], zre[kb * N_SEG:(kb + 1) * N_SEG])
        zi = jnp.where(isf, zim[k * N_SEG:(k + 1) * N_SEG], zim[kb * N_SEG:(kb + 1) * N_SEG])
        nr, ni = _cmul(l16r, l16i, hr, hi)
        hr, hi = nr + zr, ni + zi

    if nseg > 1:
        h0r = jnp.broadcast_to(h0r[:, None, :], (N_SEG // nseg, nseg, DSTATE)).reshape(N_SEG, DSTATE)
        h0i = jnp.broadcast_to(h0i[:, None, :], (N_SEG // nseg, nseg, DSTATE)).reshape(N_SEG, DSTATE)
    seg = lax.broadcasted_iota(jnp.int32, (N_SEG, DSTATE), 0) & (nseg - 1)
    segpow = _squarings(l256r, l256i, max(int(math.log2(nseg)), 0))

    def shifted(x, d):
        dn = jnp.where(seg >= d, pltpu.roll(x, d, axis=0), 0.0)
        up = jnp.where(seg <= nseg - 1 - d, pltpu.roll(x, N_SEG - d, axis=0), 0.0)
        return jnp.where(isf, dn, up)

    pr, pi = hr, hi
    d = 1
    lvl = 0
    while d < nseg:
        ar, ai = _cmul(segpow[lvl][0], segpow[lvl][1], shifted(pr, d), shifted(pi, d))
        pr, pi = pr + ar, pi + ai
        d *= 2
        lvl += 1
    if nseg > 1:
        e_in = jnp.where(isf, seg, nseg - 1 - seg)
        wr, wi = _pow_select(e_in, segpow[:lvl])
        ar, ai = _cmul(wr, wi, h0r, h0i)
        hsr, hsi = shifted(pr, 1) + ar, shifted(pi, 1) + ai
    else:
        hsr, hsi = h0r, h0i
    er, ei = _cmul(segpow[lvl][0], segpow[lvl][1], h0r, h0i)
    fin = jnp.concatenate([pr + er, pi + ei], axis=1)

    rows = []
    for c in range(CH_PER_SEG):
        cb = CH_PER_SEG - 1 - c
        lr = jnp.where(isf, hist[c][0], hist[cb][0])
        li = jnp.where(isf, hist[c][1], hist[cb][1])
        ar, ai = _cmul(powc_ref[0, c:c + 1, :], powc_ref[1, c:c + 1, :], hsr, hsi)
        rows.append(jnp.concatenate([lr + ar, li + ai], axis=1))
    hent = jnp.concatenate(rows, axis=0).astype(BF16)

    y = jnp.dot(u, t_op, preferred_element_type=F32) + _dot_nt(hent, rt_op)
    return y, fin


GROUPS_PER_STEP = 128 // SSM_GROUP


def _block_transpose(arrs, blk):
    n = len(arrs)
    width = arrs[0].shape[1]
    j = lax.broadcasted_iota(jnp.int32, arrs[0].shape, 1) // blk
    k = n // 2
    while k >= 1:
        bit = (j & k) != 0
        new = list(arrs)
        for x in range(n):
            if x & k == 0:
                a, b = arrs[x], arrs[x | k]
                new[x] = jnp.where(bit, pltpu.roll(b, k * blk, axis=1), a)
                new[x | k] = jnp.where(bit, b, pltpu.roll(a, width - k * blk, axis=1))
        arrs = new
        k //= 2
    return arrs


def _ssm_kernel(u_ref, rowp_ref, btr_ref, bti_ref, ctr_ref, cti_ref, d_ref, h0cr_ref, h0ci_ref, h0dr_ref, h0di_ref,
                y_ref, fin_ref, ubuf, ybuf, t_ref, s_ref, r_ref, lam_ref, powc_ref):
    _ssm_build_ops(GROUPS_PER_STEP, rowp_ref, btr_ref, bti_ref, ctr_ref, cti_ref,
                   t_ref, s_ref, r_ref, lam_ref, powc_ref)
    h0 = ((h0cr_ref, h0ci_ref), (h0dr_ref, h0di_ref))
    nhalf = FLAT // 128
    rows = N_SEG * CH_PER_SEG
    def token_rows(path, tok):
        return pl.ds(path * rows * CHUNK + tok, rows, stride=CHUNK)

    for path, (_, nseg) in enumerate(SSM_PATHS):
        for half in range(nhalf):
            toks = [pltpu.bitcast(u_ref[token_rows(path, half * GROUPS_PER_STEP + sb), :].astype(BF16), jnp.uint32)
                    for sb in range(GROUPS_PER_STEP)]
            grouped = _block_transpose(toks, SSM_GROUP)
            for g in range(GROUPS_PER_STEP):
                ubuf[g, :, half * 128:(half + 1) * 128] = pltpu.bitcast(grouped[g], BF16)

        for g in range(GROUPS_PER_STEP):
            y, fin = _ssm_path(ubuf[g], s_ref[g], t_ref[g], r_ref[g], lam_ref[g], powc_ref.at[g],
                               h0[path][0][g], h0[path][1][g], nseg)
            ybuf[g] = y
            fin_ref[path, g] = fin

        for half in range(nhalf):
            grouped = [ybuf[g, :, half * 128:(half + 1) * 128] for g in range(GROUPS_PER_STEP)]
            toks = _block_transpose(grouped, SSM_GROUP)
            for sb in range(GROUPS_PER_STEP):
                sel = token_rows(path, half * GROUPS_PER_STEP + sb)
                y_ref[sel, :] = toks[sb] + u_ref[sel, :] * d_ref[...]


def _ssm_scan(u2d, rowp, btr, bti, ctr, cti, dtile, h0c_re, h0c_im, h0d_re, h0d_im):
    G = SSM_GROUPS
    gs = GROUPS_PER_STEP
    rows = N_SEG * CH_PER_SEG
    return pl.pallas_call(
        _ssm_kernel,
        grid=(G // gs,),
        in_specs=[
            pl.BlockSpec((N_TOK, 128), lambda o: (0, o)),
            pl.BlockSpec((gs, 3, DSTATE), lambda o: (o, 0, 0)),
            pl.BlockSpec((gs, SSM_GROUP, DSTATE), lambda o: (o, 0, 0)),
            pl.BlockSpec((gs, SSM_GROUP, DSTATE), lambda o: (o, 0, 0)),
            pl.BlockSpec((gs, SSM_GROUP, DSTATE), lambda o: (o, 0, 0)),
            pl.BlockSpec((gs, SSM_GROUP, DSTATE), lambda o: (o, 0, 0)),
            pl.BlockSpec((1, 128), lambda o: (0, o)),
            pl.BlockSpec((gs, BATCH, DSTATE), lambda o: (o, 0, 0)),
            pl.BlockSpec((gs, BATCH, DSTATE), lambda o: (o, 0, 0)),
            pl.BlockSpec((gs, DEC_BATCH, DSTATE), lambda o: (o, 0, 0)),
            pl.BlockSpec((gs, DEC_BATCH, DSTATE), lambda o: (o, 0, 0))],
        out_specs=[pl.BlockSpec((N_TOK, 128), lambda o: (0, o)),
                   pl.BlockSpec((2, gs, N_SEG, 2 * DSTATE), lambda o: (0, o, 0, 0))],
        out_shape=[jax.ShapeDtypeStruct((N_TOK, SSM_WIDTH), F32),
                   jax.ShapeDtypeStruct((2, G, N_SEG, 2 * DSTATE), F32)],
        scratch_shapes=[pltpu.VMEM((gs, rows, FLAT), BF16), pltpu.VMEM((gs, rows, FLAT), F32),
                        pltpu.VMEM((gs, FLAT, FLAT), BF16), pltpu.VMEM((gs, FLAT, FLAT), BF16),
                        pltpu.VMEM((gs, FLAT, FLAT), BF16), pltpu.VMEM((gs, 8, DSTATE), F32),
                        pltpu.VMEM((gs, 2, CH_PER_SEG, DSTATE), F32)],
        compiler_params=_params("arbitrary"),
        name="ssm_scan",
    )(u2d, rowp, btr, bti, ctr, cti, dtile, h0c_re, h0c_im, h0d_re, h0d_im)


CTX_BB = 4
HEADS_PER_STEP = 256 // HEAD_DIM
HEAD_LANES = HEADS_PER_STEP * HEAD_DIM


def _softmax_pv(scores, values):
    m = scores[0].max(axis=-1, keepdims=True)
    for s in scores[1:]:
        m = jnp.maximum(m, s.max(axis=-1, keepdims=True))
    l = None
    o = None
    for s, v in zip(scores, values):
        p = jnp.exp(s - m)
        ls = p.sum(axis=-1, keepdims=True)
        os_ = jnp.dot(p.astype(BF16), v, preferred_element_type=F32)
        l = ls if l is None else l + ls
        o = os_ if o is None else o + os_
    return o / l


def _head_of_lane():
    return lax.broadcasted_iota(jnp.int32, (1, HEAD_LANES), 1) // HEAD_DIM


def _merge_heads(outs, za):
    head = _head_of_lane()
    o = outs[-1]
    for hh in range(HEADS_PER_STEP - 2, -1, -1):
        o = jnp.where(head == hh, outs[hh], o)
    return (o * _silu(za.astype(F32))).astype(BF16)


def _ctx_attn_kernel(q_ref, k_ref, v_ref, za_ref, o_ref):
    head = _head_of_lane()
    for b in range(CTX_BB):
        sl = slice(b * SEQ, (b + 1) * SEQ)
        q, k, v = q_ref[sl, :], k_ref[sl, :], v_ref[sl, :]
        outs = []
        for hh in range(HEADS_PER_STEP):
            qh = jnp.where(head == hh, q, jnp.zeros_like(q))
            outs.append(_softmax_pv([_dot_nt(qh, k)], [v]))
        o_ref[sl, :] = _merge_heads(outs, za_ref[sl, :])


HEAD_BLOCKS = ATT_WIDTH // HEAD_LANES
QKV_Q, QKV_K, QKV_V = 0, HEAD_BLOCKS, 2 * HEAD_BLOCKS
GZ_ZA = 2 * D_MODEL // HEAD_LANES


def _ctx_attention(qkv, gz):
    rows = CTX_BB * SEQ

    def spec(first):
        return pl.BlockSpec((rows, HEAD_LANES), lambda b, hq: (b, first + hq))

    return pl.pallas_call(
        _ctx_attn_kernel,
        grid=(BATCH // CTX_BB, N_HEADS // HEADS_PER_STEP),
        in_specs=[spec(QKV_Q), spec(QKV_K), spec(QKV_V), spec(GZ_ZA)],
        out_specs=spec(0),
        out_shape=jax.ShapeDtypeStruct((N_CTX_TOK, ATT_WIDTH), BF16),
        compiler_params=_params("arbitrary", "arbitrary"),
        name="ctx_attention",
    )(qkv, qkv, qkv, gz)


N_DROW = 2 * WIN_ROWS - 1


def _nbr_build_table(band_ref, tbl):
    lane = lax.broadcasted_iota(jnp.int32, (GRID_W, 2 * GRID_W), 1)
    qc = lax.broadcasted_iota(jnp.int32, (GRID_W, 2 * GRID_W), 0)
    kc = lane & (GRID_W - 1)
    cs = jnp.clip(qc - WIN_COLS // 2, 0, GRID_W - WIN_COLS)
    valid = jnp.logical_and(kc >= cs, kc < cs + WIN_COLS)
    for hh in range(HEADS_PER_STEP):
        for dr in range(N_DROW):
            x = jnp.broadcast_to(band_ref[hh, dr:dr + 1, :], (GRID_W, 2 * GRID_W))
            lo = pltpu.roll(x, 0, axis=1, stride=1, stride_axis=0)
            hi = pltpu.roll(x, GRID_W, axis=1, stride=1, stride_axis=0)
            tbl[hh, dr] = jnp.where(valid, jnp.where(lane < GRID_W, lo, hi), NEG_INF)
        tbl[hh, N_DROW] = jnp.full((GRID_W, 2 * GRID_W), NEG_INF, F32)


NBR_KEYS = NBR_K + PAST_LEN
KEY_TILE = 256
SOFTMAX_ROWS = 64


def _nbr_block_index(a):
    ks = jnp.clip(Q_ROWS_PER_BLOCK * a - Q_ROWS_PER_BLOCK, 0, GRID_H - NBR_KROWS)
    idx = []
    for ri in range(Q_ROWS_PER_BLOCK):
        r = Q_ROWS_PER_BLOCK * a + ri
        rs = jnp.clip(r - WIN_ROWS // 2, 0, GRID_H - WIN_ROWS)
        row = []
        for kri in range(NBR_KROWS):
            kr = ks + kri
            valid = jnp.logical_and(kr >= rs, kr < rs + WIN_ROWS)
            row.append(jnp.where(valid, kr - r + WIN_ROWS - 1, N_DROW))
        idx.append(row)
    return idx


def _nbr_bias_tile(tbl, hh, idx, t):
    half0 = lax.broadcasted_iota(jnp.int32, (1, 2 * GRID_W), 1) < GRID_W
    krows = KEY_TILE // GRID_W
    rows = []
    for ri in range(Q_ROWS_PER_BLOCK):
        tiles = [jnp.where(half0, tbl[hh, idx[ri][kri]], tbl[hh, idx[ri][kri + 1]])
                 for kri in range(krows * t, krows * (t + 1), 2)]
        rows.append(jnp.concatenate(tiles, axis=1))
    return jnp.concatenate(rows, axis=0)


def _nbr_attn_kernel(q_ref, k_ref, v_ref, kc_ref, vc_ref, band_ref, za_ref, o_ref,
                     tbl, kcat, vcat, s_scr, p_scr, o_scr):
    a = pl.program_id(1)

    @pl.when(a == 0)
    def _():
        _nbr_build_table(band_ref, tbl)

    start = pl.multiple_of(jnp.clip(a * NBR_Q - NBR_Q, 0, DEC_SEQ - NBR_K), NBR_Q)
    head = _head_of_lane()
    idx = _nbr_block_index(a)
    for b in range(DEC_BATCH):
        kcat[b, 0:NBR_K, :] = k_ref[b, pl.ds(start, NBR_K), :]
        kcat[b, NBR_K:NBR_KEYS, :] = kc_ref[b].astype(BF16)
        vcat[b, 0:NBR_K, :] = v_ref[b, pl.ds(start, NBR_K), :]
        vcat[b, NBR_K:NBR_KEYS, :] = vc_ref[b].astype(BF16)

    def scores(b, hh):
        q = q_ref[b]
        qh = jnp.where(head == hh, q, jnp.zeros_like(q))
        mrun = None
        for t in range(NBR_KEYS // KEY_TILE):
            st = _dot_nt(qh, kcat[b, t * KEY_TILE:(t + 1) * KEY_TILE, :])
            if t < NBR_K // KEY_TILE:
                st = st + _nbr_bias_tile(tbl, hh, idx, t)
            s_scr[hh, :, t * KEY_TILE:(t + 1) * KEY_TILE] = st
            mt = jnp.maximum(st[:, :128], st[:, 128:])
            mrun = mt if mrun is None else jnp.maximum(mrun, mt)
        return mrun.max(axis=-1, keepdims=True)

    def attend(b, hh, m):
        ls = []
        for r0 in range(0, NBR_Q, SOFTMAX_ROWS):
            mr = m[r0:r0 + SOFTMAX_ROWS]
            lrun = jnp.zeros((SOFTMAX_ROWS, 128), F32)
            for c0 in range(0, NBR_KEYS, 128):
                p = jnp.exp(s_scr[hh, r0:r0 + SOFTMAX_ROWS, c0:c0 + 128] - mr)
                lrun = lrun + p
                p_scr[hh, r0:r0 + SOFTMAX_ROWS, c0:c0 + 128] = p.astype(BF16)
            ls.append(lrun.sum(axis=-1, keepdims=True))
        l = jnp.concatenate(ls, axis=0)
        o = jnp.dot(p_scr[hh], vcat[b], preferred_element_type=F32)
        lanes = slice(hh * HEAD_DIM, (hh + 1) * HEAD_DIM)
        o_scr[b, :, lanes] = o[:, lanes] / l

    order = [(b, hh) for b in range(DEC_BATCH) for hh in range(HEADS_PER_STEP)]
    m_cur = scores(*order[0])
    for n, (b, hh) in enumerate(order):
        m_next = scores(*order[n + 1]) if n + 1 < len(order) else None
        attend(b, hh, m_cur)
        m_cur = m_next
    for b in range(DEC_BATCH):
        o_ref[b] = (o_scr[b] * _silu(za_ref[b].astype(F32))).astype(BF16)


def _nbr_attention(qkv4, kc, vc, band, gz4):
    first = N_CTX_TOK // DEC_SEQ // DEC_BATCH
    nblk = GRID_H // Q_ROWS_PER_BLOCK
    hw = HEAD_LANES

    def qspec(col0):
        return pl.BlockSpec((DEC_BATCH, NBR_Q, hw), lambda hp, a: (first, a, col0 + hp))

    def kspec(col0):
        return pl.BlockSpec((DEC_BATCH, DEC_SEQ, hw), lambda hp, a: (first, 0, col0 + hp))

    cspec = pl.BlockSpec((DEC_BATCH, PAST_LEN, hw), lambda hp, a: (0, 0, hp))
    bspec = pl.BlockSpec((HEADS_PER_STEP, N_DROW + 1, 2 * GRID_W), lambda hp, a: (hp, 0, 0))
    return pl.pallas_call(
        _nbr_attn_kernel,
        grid=(N_HEADS // HEADS_PER_STEP, nblk),
        in_specs=[qspec(QKV_Q), kspec(QKV_K), kspec(QKV_V), cspec, cspec, bspec, qspec(GZ_ZA)],
        out_specs=pl.BlockSpec((DEC_BATCH, NBR_Q, hw), lambda hp, a: (0, a, hp)),
        out_shape=jax.ShapeDtypeStruct((DEC_BATCH, DEC_SEQ, ATT_WIDTH), BF16),
        scratch_shapes=[pltpu.VMEM((HEADS_PER_STEP, N_DROW + 1, GRID_W, 2 * GRID_W), F32),
                        pltpu.VMEM((DEC_BATCH, NBR_KEYS, HEAD_LANES), BF16),
                        pltpu.VMEM((DEC_BATCH, NBR_KEYS, HEAD_LANES), BF16),
                        pltpu.VMEM((HEADS_PER_STEP, NBR_Q, NBR_KEYS), F32),
                        pltpu.VMEM((HEADS_PER_STEP, NBR_Q, NBR_KEYS), BF16),
                        pltpu.VMEM((DEC_BATCH, NBR_Q, HEAD_LANES), F32)],
        compiler_params=_params("arbitrary", "arbitrary"),
        name="nbr_attention",
    )(qkv4, qkv4, qkv4, kc, vc, band, gz4)


def _nbr_bias_band(rpb):
    ncol = 2 * WIN_COLS - 1
    fill = jnp.full(rpb.shape[:2] + (2 * GRID_W - ncol,), NEG_INF, F32)
    band = jnp.concatenate([rpb[..., WIN_COLS - 1:], fill, rpb[..., :WIN_COLS - 1]], axis=-1)
    return jnp.pad(band, ((0, 0), (0, 1), (0, 0)))


BACK_TM = CHUNK * CH_PER_SEG
BACK_CTX_STEPS = N_CTX_TOK // BACK_TM
BACK_STEPS_PER_DEC_SEQ = DEC_SEQ // BACK_TM


def _back_kernel(xp_ref, xs_ref, y_ref, zs_ref, ac_ref, ad_ref, gs_ref, ga_ref, gate_ref, bglu_ref,
                 wglu_ref, wso_ref, wao_ref, wo_ref, op_ref, os_ref):
    i = pl.program_id(0)

    def compute(x, a2):
        ys = _gelu_tanh(y_ref[0].reshape(BACK_TM, SSM_WIDTH))
        t = jnp.dot(ys.astype(BF16), wglu_ref[...], preferred_element_type=F32) + bglu_ref[...]
        ys = ys * _sigmoid(t) * _silu(zs_ref[...].astype(F32))
        p_s = jnp.dot(ys.astype(BF16), wso_ref[...], preferred_element_type=F32)
        p_a = jnp.dot(a2, wao_ref[...], preferred_element_type=F32)
        merged = (_sigmoid(gs_ref[...].astype(F32)) * p_s + _sigmoid(ga_ref[...].astype(F32)) * p_a)
        return x + gate_ref[0] * jnp.dot(merged.astype(BF16), wo_ref[...], preferred_element_type=F32)

    @pl.when(i < BACK_CTX_STEPS)
    def _():
        op_ref[...] = compute(xp_ref[...], ac_ref[...])

    @pl.when(i >= BACK_CTX_STEPS)
    def _():
        os_ref[...] = compute(xs_ref[...], ad_ref[...])


def _back(xp, xs, y, zs, a_ctx, a_dec, gz, gate3, b_glu, w_glu, w_so, w_ao, w_o):
    steps = N_TOK // BACK_TM
    n0 = BACK_CTX_STEPS
    row = functools.partial(_mod_row, ctx_steps=n0, steps_per_seq=BACK_STEPS_PER_DEC_SEQ)
    lo = lambda i: (jnp.minimum(i, n0 - 1), 0)
    hi = lambda i: (jnp.maximum(i - n0, 0), 0)
    cur = lambda i: (i, 0)
    const = lambda i: (0, 0)
    once = pl.Buffered(1)
    return pl.pallas_call(
        _back_kernel,
        grid=(steps,),
        in_specs=[pl.BlockSpec((BACK_TM, D_MODEL), lo),
                  pl.BlockSpec((BACK_TM, D_MODEL), hi),
                  pl.BlockSpec((1, CH_PER_SEG, None, CHUNK, SSM_WIDTH),
                               lambda i: (i // N_SEG, 0, i % N_SEG, 0, 0)),
                  pl.BlockSpec((BACK_TM, SSM_WIDTH), cur),
                  pl.BlockSpec((BACK_TM, ATT_WIDTH), lo),
                  pl.BlockSpec((BACK_TM, ATT_WIDTH), hi),
                  pl.BlockSpec((BACK_TM, D_MODEL), lambda i: (i, 0)),
                  pl.BlockSpec((BACK_TM, D_MODEL), lambda i: (i, 1)),
                  pl.BlockSpec((1, 1, D_MODEL), lambda i: (row(i), 0, 0)),
                  pl.BlockSpec((1, SSM_WIDTH), const),
                  pl.BlockSpec((SSM_WIDTH, SSM_WIDTH), const, pipeline_mode=once),
                  pl.BlockSpec((SSM_WIDTH, D_MODEL), const, pipeline_mode=once),
                  pl.BlockSpec((ATT_WIDTH, D_MODEL), const, pipeline_mode=once),
                  pl.BlockSpec((D_MODEL, D_MODEL), const, pipeline_mode=once)],
        out_specs=[pl.BlockSpec((BACK_TM, D_MODEL), lo),
                   pl.BlockSpec((BACK_TM, D_MODEL), hi)],
        out_shape=[jax.ShapeDtypeStruct((N_CTX_TOK, D_MODEL), F32),
                   jax.ShapeDtypeStruct((N_DEC_TOK, D_MODEL), F32)],
        compiler_params=_params("arbitrary"),
        name="gated_output",
    )(xp, xs, y, zs, a_ctx, a_dec, gz, gz, gate3, b_glu.reshape(1, SSM_WIDTH), w_glu, w_so, w_ao, w_o)


def _layer(xp, xs, cache_k, cache_v, st_re, st_im, c, c_ctx, norm_w, w_ada, b_ada, w_in, q_norm_w, k_norm_w,
           rpb, a_re, a_im, log_dt, b_re, b_im, c_re, c_im, d, w_glu, b_glu, w_so, w_ao, w_o):
    G, P = SSM_GROUPS, SSM_STATE

    cond8 = jnp.zeros((8, D_MODEL), F32).at[0].set(c_ctx).at[1:1 + DEC_BATCH].set(c)
    mod = _modulation(cond8, w_ada, b_ada)
    shift3 = mod[:1 + DEC_BATCH, None, :D_MODEL]
    scale3 = mod[:1 + DEC_BATCH, None, D_MODEL:2 * D_MODEL]
    gate3 = mod[:1 + DEC_BATCH, None, 2 * D_MODEL:]

    h = _norm_modulate(xp, xs, shift3, scale3, norm_w)
    u, zs, qkv, kf32, vf32, gz = _in_proj(h, w_in, q_norm_w, k_norm_w)

    arow = jnp.transpose(a_re, (1, 0, 2)).reshape(G, 2 * P)
    airow = jnp.transpose(a_im, (1, 0, 2)).reshape(G, 2 * P)
    dtrow = jnp.broadcast_to(jnp.transpose(log_dt, (1, 0))[:, :, None], (G, 2, P)).reshape(G, 2 * P)
    rowp = jnp.stack([arow, airow, dtrow], axis=1)
    btr = jnp.transpose(b_re, (1, 3, 0, 2)).reshape(G, SSM_GROUP, 2 * P)
    bti = jnp.transpose(b_im, (1, 3, 0, 2)).reshape(G, SSM_GROUP, 2 * P)
    ctr = jnp.transpose(c_re, (1, 2, 0, 3)).reshape(G, SSM_GROUP, 2 * P)
    cti = jnp.transpose(c_im, (1, 2, 0, 3)).reshape(G, SSM_GROUP, 2 * P)

    dtile = d.reshape(1, SSM_WIDTH)
    h0c = jnp.zeros((G, BATCH, 2 * P), F32)
    h0d_re = jnp.transpose(st_re, (2, 0, 1, 3)).reshape(G, DEC_BATCH, 2 * P)
    h0d_im = jnp.transpose(st_im, (2, 0, 1, 3)).reshape(G, DEC_BATCH, 2 * P)
    y2d, fin = _ssm_scan(u.reshape(N_TOK, SSM_WIDTH), rowp, btr, bti, ctr, cti, dtile,
                         h0c, h0c, h0d_re, h0d_im)
    y = y2d.reshape(2, CH_PER_SEG, N_SEG, CHUNK, SSM_WIDTH)

    a_ctx = _ctx_attention(qkv, gz)
    nseq4 = N_TOK // DEC_SEQ
    a_dec = _nbr_attention(qkv.reshape(nseq4, DEC_SEQ, qkv.shape[1]),
                           cache_k.reshape(DEC_BATCH, PAST_LEN, ATT_WIDTH),
                           cache_v.reshape(DEC_BATCH, PAST_LEN, ATT_WIDTH),
                           _nbr_bias_band(rpb.astype(F32)),
                           gz.reshape(nseq4, DEC_SEQ, gz.shape[1]))

    yp, ys_out = _back(xp, xs, y, zs, a_ctx, a_dec.reshape(N_DEC_TOK, ATT_WIDTH), gz, gate3, b_glu,
                       w_glu.astype(BF16), w_so.astype(BF16), w_ao.astype(BF16), w_o.astype(BF16))

    fin_ctx = fin[0]
    new_re = jnp.transpose(fin_ctx[:, :, :2 * P].reshape(G, BATCH, 2, P), (1, 2, 0, 3))
    new_im = jnp.transpose(fin_ctx[:, :, 2 * P:].reshape(G, BATCH, 2, P), (1, 2, 0, 3))
    return yp, ys_out, kf32, vf32, new_re, new_im


def kernel(x_prompt, x_sample, cache_k, cache_v, state_ssm_re, state_ssm_im, c, c_ctx, norm_w, w_ada, b_ada,
           w_in, q_norm_w, k_norm_w, rel_pos_bias, ssm_a_re, ssm_a_im, ssm_log_dt, ssm_b_re, ssm_b_im,
           ssm_c_re, ssm_c_im, ssm_d, w_glu, b_glu, w_ssm_out, w_att_out, w_o):
    depth = norm_w.shape[0]
    xp = x_prompt.reshape(N_CTX_TOK, D_MODEL)
    xs = x_sample.reshape(N_DEC_TOK, D_MODEL)
    new_k, new_v, new_re, new_im = [], [], [], []
    for l in range(depth):
        xp, xs, kl, vl, rl, il = _layer(
            xp, xs, cache_k[:, l], cache_v[:, l], state_ssm_re[:, l], state_ssm_im[:, l], c, c_ctx,
            norm_w[l], w_ada[l], b_ada[l], w_in[l], q_norm_w[l], k_norm_w[l], rel_pos_bias[l],
            ssm_a_re[l], ssm_a_im[l], ssm_log_dt[l], ssm_b_re[l], ssm_b_im[l], ssm_c_re[l], ssm_c_im[l],
            ssm_d[l], w_glu[l], b_glu[l], w_ssm_out[l], w_att_out[l], w_o[l])
        new_k.append(kl.reshape(BATCH, SEQ, N_HEADS, HEAD_DIM))
        new_v.append(vl.reshape(BATCH, SEQ, N_HEADS, HEAD_DIM))
        new_re.append(rl)
        new_im.append(il)
    return (xp.reshape(BATCH, SEQ, D_MODEL), xs.reshape(DEC_BATCH, DEC_SEQ, D_MODEL),
            jnp.stack(new_k, axis=1), jnp.stack(new_v, axis=1),
            jnp.stack(new_re, axis=1), jnp.stack(new_im, axis=1))
```

```python
import functools
import math

import jax
import jax.numpy as jnp
from jax import lax
from jax.experimental import pallas as pl
from jax.experimental.pallas import tpu as pltpu

D_MODEL = 2048
BATCH = 16
SEQ = 256
DEC_BATCH = 2
DEC_SEQ = 2048
PAST_LEN = 512
GRID_W = 64
GRID_H = DEC_SEQ // GRID_W
SSM_WIDTH = D_MODEL // 2
SSM_GROUP = 16
SSM_GROUPS = SSM_WIDTH // SSM_GROUP
SSM_STATE = 64
N_HEADS = 16
HEAD_DIM = 64
ATT_WIDTH = N_HEADS * HEAD_DIM
WIN_ROWS = 8
WIN_COLS = 16
EPS = 1e-6
NEG_INF = -1e30

N_CTX_TOK = BATCH * SEQ
N_DEC_TOK = DEC_BATCH * DEC_SEQ
N_TOK = N_CTX_TOK + N_DEC_TOK

OFF_U = 0
OFF_ZS = SSM_WIDTH
OFF_Q = 2 * SSM_WIDTH
OFF_K = OFF_Q + ATT_WIDTH
OFF_V = OFF_K + ATT_WIDTH
OFF_ZA = OFF_V + ATT_WIDTH
OFF_GS = OFF_ZA + ATT_WIDTH
OFF_GA = OFF_GS + D_MODEL

CHUNK = 16
FLAT = CHUNK * SSM_GROUP
N_SEG = 16
CH_PER_SEG = 16
DSTATE = 2 * SSM_STATE

Q_ROWS_PER_BLOCK = 4
NBR_Q = Q_ROWS_PER_BLOCK * GRID_W
NBR_KROWS = 12
NBR_K = NBR_KROWS * GRID_W

V7X_VMEM_BYTES = 64 * 1024 * 1024
VMEM_LIMIT = V7X_VMEM_BYTES - 8 * 1024 * 1024
QKV_VMEM_LIMIT = V7X_VMEM_BYTES - 4 * 1024 * 1024

LANES = 128
MXU_TILE = 256

F32 = jnp.float32
BF16 = jnp.bfloat16
HIGHEST = lax.Precision.HIGHEST


def _sigmoid(x):
    return 1.0 / (1.0 + jnp.exp(-x))


def _silu(x):
    return x * _sigmoid(x)


def _gelu_tanh(x):
    return 0.5 * x * (1.0 + jnp.tanh(math.sqrt(2.0 / math.pi) * (x + 0.044715 * (x * x * x))))


def _cmul(ar, ai, br, bi):
    return ar * br - ai * bi, ar * bi + ai * br


def _dot_nt(a, b):
    return lax.dot_general(a, b, (((1,), (1,)), ((), ())), preferred_element_type=F32)


def _params(*sem, vmem_limit=VMEM_LIMIT):
    return pltpu.CompilerParams(dimension_semantics=sem, vmem_limit_bytes=vmem_limit)


MOD_TN = 1536


def _mod_kernel(cond_ref, w_ref, b_ref, o_ref):
    c = cond_ref[...]
    s = _silu(c).astype(BF16)
    o_ref[...] = jnp.dot(s, w_ref[...].astype(BF16), preferred_element_type=F32) + b_ref[...]


def _modulation(cond8, w_ada, b_ada):
    tn = MOD_TN
    n = w_ada.shape[1]
    return pl.pallas_call(
        _mod_kernel,
        grid=(n // tn,),
        in_specs=[pl.BlockSpec((8, D_MODEL), lambda j: (0, 0)),
                  pl.BlockSpec((D_MODEL, tn), lambda j: (0, j)),
                  pl.BlockSpec((1, tn), lambda j: (0, j))],
        out_specs=pl.BlockSpec((8, tn), lambda j: (0, j)),
        out_shape=jax.ShapeDtypeStruct((8, n), F32),
        compiler_params=_params("arbitrary"),
        name="modulation",
    )(cond8, w_ada, b_ada.reshape(1, n))


NORM_TM = 1024
NORM_ROWS = 16
NORM_UNROLL = 8
NORM_CTX_STEPS = N_CTX_TOK // NORM_TM
NORM_STEPS_PER_DEC_SEQ = DEC_SEQ // NORM_TM


def _mod_row(i, ctx_steps, steps_per_seq):
    return jnp.where(i < ctx_steps, 0, 1 + (i - ctx_steps) // steps_per_seq)


def _norm_kernel(xp_ref, xs_ref, shift_ref, scale_ref, nw_ref, o_ref):
    i = pl.program_id(0)

    gain = nw_ref[...] * (1.0 + scale_ref[0])
    shift = shift_ref[0]

    def body(x_ref):
        def rows(r, carry):
            sl = pl.ds(pl.multiple_of(r * NORM_ROWS, NORM_ROWS), NORM_ROWS)
            x = x_ref[sl, :]
            ms = jnp.mean(x * x, axis=-1, keepdims=True)
            o_ref[sl, :] = (x * lax.rsqrt(ms + EPS) * gain + shift).astype(BF16)
            return carry

        lax.fori_loop(0, NORM_TM // NORM_ROWS, rows, 0, unroll=NORM_UNROLL)

    @pl.when(i < NORM_CTX_STEPS)
    def _():
        body(xp_ref)

    @pl.when(i >= NORM_CTX_STEPS)
    def _():
        body(xs_ref)


def _norm_modulate(xp, xs, shift3, scale3, norm_w):
    steps = N_TOK // NORM_TM
    row = functools.partial(_mod_row, ctx_steps=NORM_CTX_STEPS, steps_per_seq=NORM_STEPS_PER_DEC_SEQ)
    return pl.pallas_call(
        _norm_kernel,
        grid=(steps,),
        in_specs=[pl.BlockSpec((NORM_TM, D_MODEL), lambda i: (jnp.minimum(i, NORM_CTX_STEPS - 1), 0)),
                  pl.BlockSpec((NORM_TM, D_MODEL), lambda i: (jnp.maximum(i - NORM_CTX_STEPS, 0), 0)),
                  pl.BlockSpec((1, 1, D_MODEL), lambda i: (row(i), 0, 0)),
                  pl.BlockSpec((1, 1, D_MODEL), lambda i: (row(i), 0, 0)),
                  pl.BlockSpec((1, D_MODEL), lambda i: (0, 0))],
        out_specs=pl.BlockSpec((NORM_TM, D_MODEL), lambda i: (i, 0)),
        out_shape=jax.ShapeDtypeStruct((N_TOK, D_MODEL), BF16),
        compiler_params=_params("arbitrary"),
        name="norm_modulate",
    )(xp, xs, shift3, scale3, norm_w.reshape(1, D_MODEL))


PROJ_TM = 1024
PROJ_TN = 1024
PROJ_CTX_STEPS = N_CTX_TOK // PROJ_TM
PROJ_SEGS = PROJ_TM // (CHUNK * CH_PER_SEG)


def _head_group_ones():
    r = lax.broadcasted_iota(jnp.int32, (MXU_TILE, MXU_TILE), 0) // HEAD_DIM
    c = lax.broadcasted_iota(jnp.int32, (MXU_TILE, MXU_TILE), 1) // HEAD_DIM
    return jnp.where(r == c, 1.0, 0.0).astype(BF16)


def _head_rms(acc, nw):
    ones = _head_group_ones()
    outs = []
    for c in range(acc.shape[1] // MXU_TILE):
        a = acc[:, c * MXU_TILE:(c + 1) * MXU_TILE]
        ssum = jnp.dot((a * a).astype(BF16), ones, preferred_element_type=F32)
        outs.append(a * lax.rsqrt(ssum * (1.0 / HEAD_DIM) + EPS))
    return jnp.concatenate(outs, axis=1) * nw


def _proj_tile(h_ref, w_ref, wbf):
    @pl.when(pl.program_id(1) == 0)
    def _():
        wbf[...] = w_ref[...].astype(BF16)

    return jnp.dot(h_ref[...], wbf[...], preferred_element_type=F32)


def _proj_s5_kernel(h_ref, w_ref, u_ref, zs_ref, wbf):
    j = pl.program_id(0)
    acc = _proj_tile(h_ref, w_ref, wbf)

    @pl.when(j == 0)
    def _():
        for seg in range(PROJ_SEGS):
            for ch in range(CH_PER_SEG):
                r0 = (seg * CH_PER_SEG + ch) * CHUNK
                u_ref[0, ch, seg] = acc[r0:r0 + CHUNK, :]

    @pl.when(j == 1)
    def _():
        zs_ref[...] = acc.astype(BF16)


def _proj_qkv_kernel(h_ref, w_ref, nw_ref, o_ref, kf_ref, vf_ref, wbf):
    j = pl.program_id(0)
    i = pl.program_id(1)
    acc = _proj_tile(h_ref, w_ref, wbf)

    @pl.when(j == 0)
    def _():
        o_ref[...] = (_head_rms(acc, nw_ref[0]) * (HEAD_DIM ** -0.5)).astype(BF16)

    @pl.when(j == 1)
    def _():
        kn = _head_rms(acc, nw_ref[0])
        o_ref[...] = kn.astype(BF16)

        @pl.when(i < PROJ_CTX_STEPS)
        def _():
            kf_ref[...] = kn

    @pl.when(j == 2)
    def _():
        o_ref[...] = acc.astype(BF16)

        @pl.when(i < PROJ_CTX_STEPS)
        def _():
            vf_ref[...] = acc


def _proj_gate_kernel(h_ref, w_ref, o_ref, wbf):
    o_ref[...] = _proj_tile(h_ref, w_ref, wbf).astype(BF16)


def _in_proj(h, w_in, q_norm_w, k_norm_w):
    ni = N_TOK // PROJ_TM
    last = ni - 1
    ctx_last = PROJ_CTX_STEPS - 1
    tiles_per_path = N_SEG // PROJ_SEGS
    h_spec = pl.BlockSpec((PROJ_TM, D_MODEL), lambda j, i: (i, 0))
    scratch = [pltpu.VMEM((D_MODEL, PROJ_TN), BF16)]
    params = _params("arbitrary", "arbitrary")

    def u_map(j, i):
        ii = jnp.where(j == 0, i, last)
        return (ii // tiles_per_path, 0, ii % tiles_per_path, 0, 0)

    u, zs = pl.pallas_call(
        _proj_s5_kernel,
        grid=(2, ni),
        in_specs=[h_spec, pl.BlockSpec((D_MODEL, PROJ_TN), lambda j, i: (0, OFF_U // PROJ_TN + j))],
        out_specs=[pl.BlockSpec((1, CH_PER_SEG, PROJ_SEGS, CHUNK, PROJ_TN), u_map),
                   pl.BlockSpec((PROJ_TM, PROJ_TN), lambda j, i: (jnp.where(j == 0, 0, i), 0))],
        out_shape=[jax.ShapeDtypeStruct((2, CH_PER_SEG, N_SEG, CHUNK, SSM_WIDTH), F32),
                   jax.ShapeDtypeStruct((N_TOK, SSM_WIDTH), BF16)],
        scratch_shapes=scratch, compiler_params=params, name="in_proj_s5",
    )(h, w_in)

    nw = jnp.stack([jnp.tile(q_norm_w.reshape(1, HEAD_DIM), (1, PROJ_TN // HEAD_DIM)),
                    jnp.tile(k_norm_w.reshape(1, HEAD_DIM), (1, PROJ_TN // HEAD_DIM))])
    qkv, kf32, vf32 = pl.pallas_call(
        _proj_qkv_kernel,
        grid=(3, ni),
        in_specs=[h_spec, pl.BlockSpec((D_MODEL, PROJ_TN), lambda j, i: (0, OFF_Q // PROJ_TN + j)),
                  pl.BlockSpec((1, 1, PROJ_TN), lambda j, i: (jnp.where(j == 1, 1, 0), 0, 0))],
        out_specs=[pl.BlockSpec((PROJ_TM, PROJ_TN), lambda j, i: (i, j)),
                   pl.BlockSpec((PROJ_TM, PROJ_TN),
                                lambda j, i: (jnp.where(j == 0, 0, jnp.where(j == 1, jnp.minimum(i, ctx_last),
                                                                             ctx_last)), 0)),
                   pl.BlockSpec((PROJ_TM, PROJ_TN),
                                lambda j, i: (jnp.where(j == 2, jnp.minimum(i, ctx_last), 0), 0))],
        out_shape=[jax.ShapeDtypeStruct((N_TOK, 3 * ATT_WIDTH), BF16),
                   jax.ShapeDtypeStruct((N_CTX_TOK, ATT_WIDTH), F32),
                   jax.ShapeDtypeStruct((N_CTX_TOK, ATT_WIDTH), F32)],
        scratch_shapes=scratch, name="in_proj_qkv",
        compiler_params=_params("arbitrary", "arbitrary", vmem_limit=QKV_VMEM_LIMIT),
    )(h, w_in, nw)

    gate_tiles = 2 * D_MODEL // PROJ_TN
    gz = pl.pallas_call(
        _proj_gate_kernel,
        grid=(gate_tiles + 1, ni),
        in_specs=[h_spec,
                  pl.BlockSpec((D_MODEL, PROJ_TN),
                               lambda j, i: (0, jnp.where(j < gate_tiles, OFF_GS // PROJ_TN + j,
                                                          OFF_ZA // PROJ_TN)))],
        out_specs=pl.BlockSpec((PROJ_TM, PROJ_TN), lambda j, i: (i, j)),
        out_shape=jax.ShapeDtypeStruct((N_TOK, 2 * D_MODEL + ATT_WIDTH), BF16),
        scratch_shapes=scratch, compiler_params=params, name="in_proj_gates",
    )(h, w_in)
    return u, zs, qkv, kf32, vf32, gz


def _pow_select(e, pows):
    rr = jnp.where((e & 1) != 0, pows[0][0], 1.0)
    ri = jnp.where((e & 1) != 0, pows[0][1], 0.0)
    for b in range(1, len(pows)):
        bit = (e & (1 << b)) != 0
        fr = jnp.where(bit, pows[b][0], 1.0)
        fi = jnp.where(bit, pows[b][1], 0.0)
        rr, ri = _cmul(rr, ri, fr, fi)
    return rr, ri


def _discretize(a_re, a_im, log_dt):
    lr = jnp.minimum(a_re, -1e-4)
    li = a_im
    dt = jnp.exp(log_dt)
    mag = jnp.exp(lr * dt)
    br = mag * jnp.cos(li * dt)
    bi = mag * jnp.sin(li * dt)
    den = lr * lr + li * li
    nr = br - 1.0
    cr = (nr * lr + bi * li) / den
    ci = (bi * lr - nr * li) / den
    return (br, bi), (cr, ci)


def _squarings(pr, pi, n):
    out = [(pr, pi)]
    for _ in range(n):
        pr, pi = _cmul(pr, pi, pr, pi)
        out.append((pr, pi))
    return out


def _dot_nt_exact(a, b):
    return lax.dot_general(a, b, (((1,), (1,)), ((), ())), preferred_element_type=F32, precision=HIGHEST)


def _ssm_build_ops(n_groups, rowp_ref, btr_ref, bti_ref, ctr_ref, cti_ref, t_ref, s_ref, rt_ref, lam_ref, powc_ref):
    lane_b = lax.broadcasted_iota(jnp.int32, (SSM_GROUP, DSTATE), 1)
    lane_t = lax.broadcasted_iota(jnp.int32, (SSM_GROUP, FLAT), 1)
    row_c = lax.broadcasted_iota(jnp.int32, (CH_PER_SEG, DSTATE), 0)
    lane_c = lax.broadcasted_iota(jnp.int32, (CH_PER_SEG, DSTATE), 1)
    exp_c = jnp.where(lane_c < SSM_STATE, row_c, (CH_PER_SEG - 1) - row_c)

    for g in range(n_groups):
        rp = rowp_ref[g]
        (lbr, lbi), (cfr, cfi) = _discretize(rp[0:1], rp[1:2], rp[2:3])
        sq = _squarings(lbr, lbi, 8)
        bbr, bbi = _cmul(btr_ref[g], bti_ref[g], cfr, cfi)

        ptr, pti = _pow_select(exp_c, sq[:4])

        s_re, s_im = [], []
        for sp in range(CHUNK):
            e = CHUNK - 1 - sp
            br, bi = _cmul(bbr, bbi, ptr[e:e + 1], pti[e:e + 1])
            s_re.append(br)
            s_im.append(bi)
        s_ref[g] = jnp.concatenate([jnp.concatenate(s_re, axis=0), jnp.concatenate(s_im, axis=0)],
                                   axis=1).astype(BF16)

        xcr = jnp.concatenate([ctr_ref[g]] * CHUNK, axis=0)
        xci = jnp.concatenate([cti_ref[g]] * CHUNK, axis=0)
        xpr = jnp.broadcast_to(ptr[:, None, :], (CHUNK, SSM_GROUP, DSTATE)).reshape(FLAT, DSTATE)
        xpi = jnp.broadcast_to(pti[:, None, :], (CHUNK, SSM_GROUP, DSTATE)).reshape(FLAT, DSTATE)
        ykr, yki = _cmul(xcr, xci, xpr, xpi)
        yrr, yri = _cmul(ykr, yki, lbr, lbi)
        rt_ref[g] = jnp.concatenate([yrr, -yri], axis=1).astype(BF16)

        fwd = lane_b < SSM_STATE
        lhs = jnp.concatenate([jnp.where(fwd, bbr, 0.0), jnp.where(fwd, bbi, 0.0),
                               jnp.where(fwd, 0.0, bbr), jnp.where(fwd, 0.0, bbi)], axis=0)
        p1 = _dot_nt_exact(lhs, ykr)
        p2 = _dot_nt_exact(lhs, yki)
        kf = p1[0:16] - p2[16:32]
        kb = p1[32:48] - p2[48:64]
        for sp in range(CHUNK):
            tf = kf if sp == 0 else pltpu.roll(kf, SSM_GROUP * sp, axis=1)
            tf = jnp.where(lane_t >= SSM_GROUP * sp, tf, 0.0)
            shift = (FLAT - SSM_GROUP * (CHUNK - 1 - sp)) % FLAT
            tb = kb if shift == 0 else pltpu.roll(kb, shift, axis=1)
            tb = jnp.where(lane_t < SSM_GROUP * (sp + 1), tb, 0.0)
            t_ref[g, sp * SSM_GROUP:(sp + 1) * SSM_GROUP, :] = (tf + tb).astype(BF16)

        l16 = sq[4]
        l256 = sq[8]
        lam_ref[g] = jnp.concatenate([l16[0], l16[1], l256[0], l256[1],
                                      jnp.zeros((4, DSTATE), F32)], axis=0)
        q16 = _squarings(l16[0], l16[1], 3)
        pcr, pci = _pow_select(exp_c, q16)
        powc_ref[g, 0] = pcr
        powc_ref[g, 1] = pci


SSM_PATHS = ((BATCH, SEQ // (CHUNK * CH_PER_SEG)), (DEC_BATCH, DEC_SEQ // (CHUNK * CH_PER_SEG)))


def _ssm_path(u, s_op, t_op, rt_op, lam, powc_ref, h0r, h0i, nseg):
    lane = lax.broadcasted_iota(jnp.int32, (N_SEG, DSTATE), 1)
    isf = lane < SSM_STATE
    l16r, l16i, l256r, l256i = lam[0:1], lam[1:2], lam[2:3], lam[3:4]

    z = jnp.dot(u, s_op, preferred_element_type=F32)
    zre, zim = z[:, :DSTATE], z[:, DSTATE:]

    hr = jnp.zeros((N_SEG, DSTATE), F32)
    hi = jnp.zeros((N_SEG, DSTATE), F32)
    hist = []
    for k in range(CH_PER_SEG):
        hist.append((hr, hi))
        kb = CH_PER_SEG - 1 - k
        zr = jnp.where(isf, zre[k * N_SEG:(k + 1) * N_SEG], zre[kb * N_SEG:(kb + 1) * N_SEG])
        zi = jnp.where(isf, zim[k * N_SEG:(k + 1) * N_SEG], zim[kb * N_SEG:(kb + 1) * N_SEG])
        nr, ni = _cmul(l16r, l16i, hr, hi)
        hr, hi = nr + zr, ni + zi

    if nseg > 1:
        h0r = jnp.broadcast_to(h0r[:, None, :], (N_SEG // nseg, nseg, DSTATE)).reshape(N_SEG, DSTATE)
        h0i = jnp.broadcast_to(h0i[:, None, :], (N_SEG // nseg, nseg, DSTATE)).reshape(N_SEG, DSTATE)
    seg = lax.broadcasted_iota(jnp.int32, (N_SEG, DSTATE), 0) & (nseg - 1)
    segpow = _squarings(l256r, l256i, max(int(math.log2(nseg)), 0))

    def shifted(x, d):
        dn = jnp.where(seg >= d, pltpu.roll(x, d, axis=0), 0.0)
        up = jnp.where(seg <= nseg - 1 - d, pltpu.roll(x, N_SEG - d, axis=0), 0.0)
        return jnp.where(isf, dn, up)

    pr, pi = hr, hi
    d = 1
    lvl = 0
    while d < nseg:
        ar, ai = _cmul(segpow[lvl][0], segpow[lvl][1], shifted(pr, d), shifted(pi, d))
        pr, pi = pr + ar, pi + ai
        d *= 2
        lvl += 1
    if nseg > 1:
        e_in = jnp.where(isf, seg, nseg - 1 - seg)
        wr, wi = _pow_select(e_in, segpow[:lvl])
        ar, ai = _cmul(wr, wi, h0r, h0i)
        hsr, hsi = shifted(pr, 1) + ar, shifted(pi, 1) + ai
    else:
        hsr, hsi = h0r, h0i
    er, ei = _cmul(segpow[lvl][0], segpow[lvl][1], h0r, h0i)
    fin = jnp.concatenate([pr + er, pi + ei], axis=1)

    rows = []
    for c in range(CH_PER_SEG):
        cb = CH_PER_SEG - 1 - c
        lr = jnp.where(isf, hist[c][0], hist[cb][0])
        li = jnp.where(isf, hist[c][1], hist[cb][1])
        ar, ai = _cmul(powc_ref[0, c:c + 1, :], powc_ref[1, c:c + 1, :], hsr, hsi)
        rows.append(jnp.concatenate([lr + ar, li + ai], axis=1))
    hent = jnp.concatenate(rows, axis=0).astype(BF16)

    y = jnp.dot(u, t_op, preferred_element_type=F32) + _dot_nt(hent, rt_op)
    return y, fin


GROUPS_PER_STEP = LANES // SSM_GROUP


def _block_transpose(arrs, blk):
    n = len(arrs)
    width = arrs[0].shape[1]
    j = lax.broadcasted_iota(jnp.int32, arrs[0].shape, 1) // blk
    k = n // 2
    while k >= 1:
        bit = (j & k) != 0
        new = list(arrs)
        for x in range(n):
            if x & k == 0:
                a, b = arrs[x], arrs[x | k]
                new[x] = jnp.where(bit, pltpu.roll(b, k * blk, axis=1), a)
                new[x | k] = jnp.where(bit, b, pltpu.roll(a, width - k * blk, axis=1))
        arrs = new
        k //= 2
    return arrs


def _ssm_kernel(u_ref, rowp_ref, btr_ref, bti_ref, ctr_ref, cti_ref, d_ref, h0cr_ref, h0ci_ref, h0dr_ref, h0di_ref,
                y_ref, fin_ref, ubuf, ybuf, t_ref, s_ref, r_ref, lam_ref, powc_ref):
    _ssm_build_ops(GROUPS_PER_STEP, rowp_ref, btr_ref, bti_ref, ctr_ref, cti_ref,
                   t_ref, s_ref, r_ref, lam_ref, powc_ref)
    h0 = ((h0cr_ref, h0ci_ref), (h0dr_ref, h0di_ref))
    nhalf = FLAT // LANES
    rows = N_SEG * CH_PER_SEG
    def token_rows(path, tok):
        return pl.ds(path * rows * CHUNK + tok, rows, stride=CHUNK)

    for path, (_, nseg) in enumerate(SSM_PATHS):
        for half in range(nhalf):
            toks = [pltpu.bitcast(u_ref[token_rows(path, half * GROUPS_PER_STEP + sb), :].astype(BF16), jnp.uint32)
                    for sb in range(GROUPS_PER_STEP)]
            grouped = _block_transpose(toks, SSM_GROUP)
            for g in range(GROUPS_PER_STEP):
                ubuf[g, :, half * LANES:(half + 1) * LANES] = pltpu.bitcast(grouped[g], BF16)

        for g in range(GROUPS_PER_STEP):
            y, fin = _ssm_path(ubuf[g], s_ref[g], t_ref[g], r_ref[g], lam_ref[g], powc_ref.at[g],
                               h0[path][0][g], h0[path][1][g], nseg)
            ybuf[g] = y
            fin_ref[path, g] = fin

        for half in range(nhalf):
            grouped = [ybuf[g, :, half * LANES:(half + 1) * LANES] for g in range(GROUPS_PER_STEP)]
            toks = _block_transpose(grouped, SSM_GROUP)
            for sb in range(GROUPS_PER_STEP):
                sel = token_rows(path, half * GROUPS_PER_STEP + sb)
                y_ref[sel, :] = toks[sb] + u_ref[sel, :] * d_ref[...]


def _ssm_scan(u2d, rowp, btr, bti, ctr, cti, dtile, h0c_re, h0c_im, h0d_re, h0d_im):
    G = SSM_GROUPS
    gs = GROUPS_PER_STEP
    rows = N_SEG * CH_PER_SEG
    return pl.pallas_call(
        _ssm_kernel,
        grid=(G // gs,),
        in_specs=[
            pl.BlockSpec((N_TOK, LANES), lambda o: (0, o)),
            pl.BlockSpec((gs, 3, DSTATE), lambda o: (o, 0, 0)),
            pl.BlockSpec((gs, SSM_GROUP, DSTATE), lambda o: (o, 0, 0)),
            pl.BlockSpec((gs, SSM_GROUP, DSTATE), lambda o: (o, 0, 0)),
            pl.BlockSpec((gs, SSM_GROUP, DSTATE), lambda o: (o, 0, 0)),
            pl.BlockSpec((gs, SSM_GROUP, DSTATE), lambda o: (o, 0, 0)),
            pl.BlockSpec((1, LANES), lambda o: (0, o)),
            pl.BlockSpec((gs, BATCH, DSTATE), lambda o: (o, 0, 0)),
            pl.BlockSpec((gs, BATCH, DSTATE), lambda o: (o, 0, 0)),
            pl.BlockSpec((gs, DEC_BATCH, DSTATE), lambda o: (o, 0, 0)),
            pl.BlockSpec((gs, DEC_BATCH, DSTATE), lambda o: (o, 0, 0))],
        out_specs=[pl.BlockSpec((N_TOK, LANES), lambda o: (0, o)),
                   pl.BlockSpec((2, gs, N_SEG, 2 * DSTATE), lambda o: (0, o, 0, 0))],
        out_shape=[jax.ShapeDtypeStruct((N_TOK, SSM_WIDTH), F32),
                   jax.ShapeDtypeStruct((2, G, N_SEG, 2 * DSTATE), F32)],
        scratch_shapes=[pltpu.VMEM((gs, rows, FLAT), BF16), pltpu.VMEM((gs, rows, FLAT), F32),
                        pltpu.VMEM((gs, FLAT, FLAT), BF16), pltpu.VMEM((gs, FLAT, FLAT), BF16),
                        pltpu.VMEM((gs, FLAT, FLAT), BF16), pltpu.VMEM((gs, 8, DSTATE), F32),
                        pltpu.VMEM((gs, 2, CH_PER_SEG, DSTATE), F32)],
        compiler_params=_params("arbitrary"),
        name="ssm_scan",
    )(u2d, rowp, btr, bti, ctr, cti, dtile, h0c_re, h0c_im, h0d_re, h0d_im)


CTX_BB = 4
HEADS_PER_STEP = MXU_TILE // HEAD_DIM
HEAD_LANES = HEADS_PER_STEP * HEAD_DIM


def _softmax_pv(scores, values):
    m = scores[0].max(axis=-1, keepdims=True)
    for s in scores[1:]:
        m = jnp.maximum(m, s.max(axis=-1, keepdims=True))
    l = None
    o = None
    for s, v in zip(scores, values):
        p = jnp.exp(s - m)
        ls = p.sum(axis=-1, keepdims=True)
        os_ = jnp.dot(p.astype(BF16), v, preferred_element_type=F32)
        l = ls if l is None else l + ls
        o = os_ if o is None else o + os_
    return o / l


def _head_of_lane():
    return lax.broadcasted_iota(jnp.int32, (1, HEAD_LANES), 1) // HEAD_DIM


def _merge_heads(outs, za):
    head = _head_of_lane()
    o = outs[-1]
    for hh in range(HEADS_PER_STEP - 2, -1, -1):
        o = jnp.where(head == hh, outs[hh], o)
    return (o * _silu(za.astype(F32))).astype(BF16)


def _ctx_attn_kernel(q_ref, k_ref, v_ref, za_ref, o_ref):
    head = _head_of_lane()
    for b in range(CTX_BB):
        sl = slice(b * SEQ, (b + 1) * SEQ)
        q, k, v = q_ref[sl, :], k_ref[sl, :], v_ref[sl, :]
        outs = []
        for hh in range(HEADS_PER_STEP):
            qh = jnp.where(head == hh, q, jnp.zeros_like(q))
            outs.append(_softmax_pv([_dot_nt(qh, k)], [v]))
        o_ref[sl, :] = _merge_heads(outs, za_ref[sl, :])


HEAD_BLOCKS = ATT_WIDTH // HEAD_LANES
QKV_Q, QKV_K, QKV_V = 0, HEAD_BLOCKS, 2 * HEAD_BLOCKS
GZ_ZA = 2 * D_MODEL // HEAD_LANES


def _ctx_attention(qkv, gz):
    rows = CTX_BB * SEQ

    def spec(first):
        return pl.BlockSpec((rows, HEAD_LANES), lambda b, hq: (b, first + hq))

    return pl.pallas_call(
        _ctx_attn_kernel,
        grid=(BATCH // CTX_BB, N_HEADS // HEADS_PER_STEP),
        in_specs=[spec(QKV_Q), spec(QKV_K), spec(QKV_V), spec(GZ_ZA)],
        out_specs=spec(0),
        out_shape=jax.ShapeDtypeStruct((N_CTX_TOK, ATT_WIDTH), BF16),
        compiler_params=_params("arbitrary", "arbitrary"),
        name="ctx_attention",
    )(qkv, qkv, qkv, gz)


N_DROW = 2 * WIN_ROWS - 1


def _nbr_build_table(band_ref, tbl):
    lane = lax.broadcasted_iota(jnp.int32, (GRID_W, 2 * GRID_W), 1)
    qc = lax.broadcasted_iota(jnp.int32, (GRID_W, 2 * GRID_W), 0)
    kc = lane & (GRID_W - 1)
    cs = jnp.clip(qc - WIN_COLS // 2, 0, GRID_W - WIN_COLS)
    valid = jnp.logical_and(kc >= cs, kc < cs + WIN_COLS)
    for hh in range(HEADS_PER_STEP):
        for dr in range(N_DROW):
            x = jnp.broadcast_to(band_ref[hh, dr:dr + 1, :], (GRID_W, 2 * GRID_W))
            lo = pltpu.roll(x, 0, axis=1, stride=1, stride_axis=0)
            hi = pltpu.roll(x, GRID_W, axis=1, stride=1, stride_axis=0)
            tbl[hh, dr] = jnp.where(valid, jnp.where(lane < GRID_W, lo, hi), NEG_INF)
        tbl[hh, N_DROW] = jnp.full((GRID_W, 2 * GRID_W), NEG_INF, F32)


NBR_KEYS = NBR_K + PAST_LEN
KEY_TILE = MXU_TILE
SOFTMAX_ROWS = 64


def _nbr_block_index(a):
    ks = jnp.clip(Q_ROWS_PER_BLOCK * a - Q_ROWS_PER_BLOCK, 0, GRID_H - NBR_KROWS)
    idx = []
    for ri in range(Q_ROWS_PER_BLOCK):
        r = Q_ROWS_PER_BLOCK * a + ri
        rs = jnp.clip(r - WIN_ROWS // 2, 0, GRID_H - WIN_ROWS)
        row = []
        for kri in range(NBR_KROWS):
            kr = ks + kri
            valid = jnp.logical_and(kr >= rs, kr < rs + WIN_ROWS)
            row.append(jnp.where(valid, kr - r + WIN_ROWS - 1, N_DROW))
        idx.append(row)
    return idx


def _nbr_bias_tile(tbl, hh, idx, t):
    half0 = lax.broadcasted_iota(jnp.int32, (1, 2 * GRID_W), 1) < GRID_W
    krows = KEY_TILE // GRID_W
    rows = []
    for ri in range(Q_ROWS_PER_BLOCK):
        tiles = [jnp.where(half0, tbl[hh, idx[ri][kri]], tbl[hh, idx[ri][kri + 1]])
                 for kri in range(krows * t, krows * (t + 1), 2)]
        rows.append(jnp.concatenate(tiles, axis=1))
    return jnp.concatenate(rows, axis=0)


def _nbr_attn_kernel(q_ref, k_ref, v_ref, kc_ref, vc_ref, band_ref, za_ref, o_ref,
                     tbl, kcat, vcat, s_scr, p_scr, o_scr):
    a = pl.program_id(1)

    @pl.when(a == 0)
    def _():
        _nbr_build_table(band_ref, tbl)

    start = pl.multiple_of(jnp.clip(a * NBR_Q - NBR_Q, 0, DEC_SEQ - NBR_K), NBR_Q)
    head = _head_of_lane()
    idx = _nbr_block_index(a)
    for b in range(DEC_BATCH):
        kcat[b, 0:NBR_K, :] = k_ref[b, pl.ds(start, NBR_K), :]
        kcat[b, NBR_K:NBR_KEYS, :] = kc_ref[b].astype(BF16)
        vcat[b, 0:NBR_K, :] = v_ref[b, pl.ds(start, NBR_K), :]
        vcat[b, NBR_K:NBR_KEYS, :] = vc_ref[b].astype(BF16)

    def scores(b, hh):
        q = q_ref[b]
        qh = jnp.where(head == hh, q, jnp.zeros_like(q))
        mrun = None
        for t in range(NBR_KEYS // KEY_TILE):
            st = _dot_nt(qh, kcat[b, t * KEY_TILE:(t + 1) * KEY_TILE, :])
            if t < NBR_K // KEY_TILE:
                st = st + _nbr_bias_tile(tbl, hh, idx, t)
            s_scr[hh, :, t * KEY_TILE:(t + 1) * KEY_TILE] = st
            mt = jnp.maximum(st[:, :LANES], st[:, LANES:])
            mrun = mt if mrun is None else jnp.maximum(mrun, mt)
        return mrun.max(axis=-1, keepdims=True)

    def attend(b, hh, m):
        ls = []
        for r0 in range(0, NBR_Q, SOFTMAX_ROWS):
            mr = m[r0:r0 + SOFTMAX_ROWS]
            lrun = jnp.zeros((SOFTMAX_ROWS, LANES), F32)
            for c0 in range(0, NBR_KEYS, LANES):
                p = jnp.exp(s_scr[hh, r0:r0 + SOFTMAX_ROWS, c0:c0 + LANES] - mr)
                lrun = lrun + p
                p_scr[hh, r0:r0 + SOFTMAX_ROWS, c0:c0 + LANES] = p.astype(BF16)
            ls.append(lrun.sum(axis=-1, keepdims=True))
        l = jnp.concatenate(ls, axis=0)
        o = jnp.dot(p_scr[hh], vcat[b], preferred_element_type=F32)
        lanes = slice(hh * HEAD_DIM, (hh + 1) * HEAD_DIM)
        o_scr[b, :, lanes] = o[:, lanes] / l

    order = [(b, hh) for b in range(DEC_BATCH) for hh in range(HEADS_PER_STEP)]
    m_cur = scores(*order[0])
    for n, (b, hh) in enumerate(order):
        m_next = scores(*order[n + 1]) if n + 1 < len(order) else None
        attend(b, hh, m_cur)
        m_cur = m_next
    for b in range(DEC_BATCH):
        o_ref[b] = (o_scr[b] * _silu(za_ref[b].astype(F32))).astype(BF16)


def _nbr_attention(qkv4, kc, vc, band, gz4):
    first = N_CTX_TOK // DEC_SEQ // DEC_BATCH
    nblk = GRID_H // Q_ROWS_PER_BLOCK
    hw = HEAD_LANES

    def qspec(col0):
        return pl.BlockSpec((DEC_BATCH, NBR_Q, hw), lambda hp, a: (first, a, col0 + hp))

    def kspec(col0):
        return pl.BlockSpec((DEC_BATCH, DEC_SEQ, hw), lambda hp, a: (first, 0, col0 + hp))

    cspec = pl.BlockSpec((DEC_BATCH, PAST_LEN, hw), lambda hp, a: (0, 0, hp))
    bspec = pl.BlockSpec((HEADS_PER_STEP, N_DROW + 1, 2 * GRID_W), lambda hp, a: (hp, 0, 0))
    return pl.pallas_call(
        _nbr_attn_kernel,
        grid=(N_HEADS // HEADS_PER_STEP, nblk),
        in_specs=[qspec(QKV_Q), kspec(QKV_K), kspec(QKV_V), cspec, cspec, bspec, qspec(GZ_ZA)],
        out_specs=pl.BlockSpec((DEC_BATCH, NBR_Q, hw), lambda hp, a: (0, a, hp)),
        out_shape=jax.ShapeDtypeStruct((DEC_BATCH, DEC_SEQ, ATT_WIDTH), BF16),
        scratch_shapes=[pltpu.VMEM((HEADS_PER_STEP, N_DROW + 1, GRID_W, 2 * GRID_W), F32),
                        pltpu.VMEM((DEC_BATCH, NBR_KEYS, HEAD_LANES), BF16),
                        pltpu.VMEM((DEC_BATCH, NBR_KEYS, HEAD_LANES), BF16),
                        pltpu.VMEM((HEADS_PER_STEP, NBR_Q, NBR_KEYS), F32),
                        pltpu.VMEM((HEADS_PER_STEP, NBR_Q, NBR_KEYS), BF16),
                        pltpu.VMEM((DEC_BATCH, NBR_Q, HEAD_LANES), F32)],
        compiler_params=_params("arbitrary", "arbitrary"),
        name="nbr_attention",
    )(qkv4, qkv4, qkv4, kc, vc, band, gz4)


def _nbr_bias_band(rpb):
    ncol = 2 * WIN_COLS - 1
    fill = jnp.full(rpb.shape[:2] + (2 * GRID_W - ncol,), NEG_INF, F32)
    band = jnp.concatenate([rpb[..., WIN_COLS - 1:], fill, rpb[..., :WIN_COLS - 1]], axis=-1)
    return jnp.pad(band, ((0, 0), (0, 1), (0, 0)))


BACK_TM = CHUNK * CH_PER_SEG
BACK_CTX_STEPS = N_CTX_TOK // BACK_TM
BACK_STEPS_PER_DEC_SEQ = DEC_SEQ // BACK_TM


def _back_kernel(xp_ref, xs_ref, y_ref, zs_ref, ac_ref, ad_ref, gs_ref, ga_ref, gate_ref, bglu_ref,
                 wglu_ref, wso_ref, wao_ref, wo_ref, op_ref, os_ref):
    i = pl.program_id(0)

    def compute(x, a2):
        ys = _gelu_tanh(y_ref[0].reshape(BACK_TM, SSM_WIDTH))
        t = jnp.dot(ys.astype(BF16), wglu_ref[...], preferred_element_type=F32) + bglu_ref[...]
        ys = ys * _sigmoid(t) * _silu(zs_ref[...].astype(F32))
        p_s = jnp.dot(ys.astype(BF16), wso_ref[...], preferred_element_type=F32)
        p_a = jnp.dot(a2, wao_ref[...], preferred_element_type=F32)
        merged = (_sigmoid(gs_ref[...].astype(F32)) * p_s + _sigmoid(ga_ref[...].astype(F32)) * p_a)
        return x + gate_ref[0] * jnp.dot(merged.astype(BF16), wo_ref[...], preferred_element_type=F32)

    @pl.when(i < BACK_CTX_STEPS)
    def _():
        op_ref[...] = compute(xp_ref[...], ac_ref[...])

    @pl.when(i >= BACK_CTX_STEPS)
    def _():
        os_ref[...] = compute(xs_ref[...], ad_ref[...])


def _back(xp, xs, y, zs, a_ctx, a_dec, gz, gate3, b_glu, w_glu, w_so, w_ao, w_o):
    steps = N_TOK // BACK_TM
    n0 = BACK_CTX_STEPS
    row = functools.partial(_mod_row, ctx_steps=n0, steps_per_seq=BACK_STEPS_PER_DEC_SEQ)
    lo = lambda i: (jnp.minimum(i, n0 - 1), 0)
    hi = lambda i: (jnp.maximum(i - n0, 0), 0)
    cur = lambda i: (i, 0)
    const = lambda i: (0, 0)
    once = pl.Buffered(1)
    return pl.pallas_call(
        _back_kernel,
        grid=(steps,),
        in_specs=[pl.BlockSpec((BACK_TM, D_MODEL), lo),
                  pl.BlockSpec((BACK_TM, D_MODEL), hi),
                  pl.BlockSpec((1, CH_PER_SEG, None, CHUNK, SSM_WIDTH),
                               lambda i: (i // N_SEG, 0, i % N_SEG, 0, 0)),
                  pl.BlockSpec((BACK_TM, SSM_WIDTH), cur),
                  pl.BlockSpec((BACK_TM, ATT_WIDTH), lo),
                  pl.BlockSpec((BACK_TM, ATT_WIDTH), hi),
                  pl.BlockSpec((BACK_TM, D_MODEL), lambda i: (i, 0)),
                  pl.BlockSpec((BACK_TM, D_MODEL), lambda i: (i, 1)),
                  pl.BlockSpec((1, 1, D_MODEL), lambda i: (row(i), 0, 0)),
                  pl.BlockSpec((1, SSM_WIDTH), const),
                  pl.BlockSpec((SSM_WIDTH, SSM_WIDTH), const, pipeline_mode=once),
                  pl.BlockSpec((SSM_WIDTH, D_MODEL), const, pipeline_mode=once),
                  pl.BlockSpec((ATT_WIDTH, D_MODEL), const, pipeline_mode=once),
                  pl.BlockSpec((D_MODEL, D_MODEL), const, pipeline_mode=once)],
        out_specs=[pl.BlockSpec((BACK_TM, D_MODEL), lo),
                   pl.BlockSpec((BACK_TM, D_MODEL), hi)],
        out_shape=[jax.ShapeDtypeStruct((N_CTX_TOK, D_MODEL), F32),
                   jax.ShapeDtypeStruct((N_DEC_TOK, D_MODEL), F32)],
        compiler_params=_params("arbitrary"),
        name="gated_output",
    )(xp, xs, y, zs, a_ctx, a_dec, gz, gz, gate3, b_glu.reshape(1, SSM_WIDTH), w_glu, w_so, w_ao, w_o)


def _layer(xp, xs, cache_k, cache_v, st_re, st_im, c, c_ctx, norm_w, w_ada, b_ada, w_in, q_norm_w, k_norm_w,
           rpb, a_re, a_im, log_dt, b_re, b_im, c_re, c_im, d, w_glu, b_glu, w_so, w_ao, w_o):
    G, P = SSM_GROUPS, SSM_STATE

    cond8 = jnp.zeros((8, D_MODEL), F32).at[0].set(c_ctx).at[1:1 + DEC_BATCH].set(c)
    mod = _modulation(cond8, w_ada, b_ada)
    shift3 = mod[:1 + DEC_BATCH, None, :D_MODEL]
    scale3 = mod[:1 + DEC_BATCH, None, D_MODEL:2 * D_MODEL]
    gate3 = mod[:1 + DEC_BATCH, None, 2 * D_MODEL:]

    h = _norm_modulate(xp, xs, shift3, scale3, norm_w)
    u, zs, qkv, kf32, vf32, gz = _in_proj(h, w_in, q_norm_w, k_norm_w)

    arow = jnp.transpose(a_re, (1, 0, 2)).reshape(G, 2 * P)
    airow = jnp.transpose(a_im, (1, 0, 2)).reshape(G, 2 * P)
    dtrow = jnp.broadcast_to(jnp.transpose(log_dt, (1, 0))[:, :, None], (G, 2, P)).reshape(G, 2 * P)
    rowp = jnp.stack([arow, airow, dtrow], axis=1)
    btr = jnp.transpose(b_re, (1, 3, 0, 2)).reshape(G, SSM_GROUP, 2 * P)
    bti = jnp.transpose(b_im, (1, 3, 0, 2)).reshape(G, SSM_GROUP, 2 * P)
    ctr = jnp.transpose(c_re, (1, 2, 0, 3)).reshape(G, SSM_GROUP, 2 * P)
    cti = jnp.transpose(c_im, (1, 2, 0, 3)).reshape(G, SSM_GROUP, 2 * P)

    dtile = d.reshape(1, SSM_WIDTH)
    h0c = jnp.zeros((G, BATCH, 2 * P), F32)
    h0d_re = jnp.transpose(st_re, (2, 0, 1, 3)).reshape(G, DEC_BATCH, 2 * P)
    h0d_im = jnp.transpose(st_im, (2, 0, 1, 3)).reshape(G, DEC_BATCH, 2 * P)
    y2d, fin = _ssm_scan(u.reshape(N_TOK, SSM_WIDTH), rowp, btr, bti, ctr, cti, dtile,
                         h0c, h0c, h0d_re, h0d_im)
    y = y2d.reshape(2, CH_PER_SEG, N_SEG, CHUNK, SSM_WIDTH)

    a_ctx = _ctx_attention(qkv, gz)
    nseq4 = N_TOK // DEC_SEQ
    a_dec = _nbr_attention(qkv.reshape(nseq4, DEC_SEQ, qkv.shape[1]),
                           cache_k.reshape(DEC_BATCH, PAST_LEN, ATT_WIDTH),
                           cache_v.reshape(DEC_BATCH, PAST_LEN, ATT_WIDTH),
                           _nbr_bias_band(rpb.astype(F32)),
                           gz.reshape(nseq4, DEC_SEQ, gz.shape[1]))

    yp, ys_out = _back(xp, xs, y, zs, a_ctx, a_dec.reshape(N_DEC_TOK, ATT_WIDTH), gz, gate3, b_glu,
                       w_glu.astype(BF16), w_so.astype(BF16), w_ao.astype(BF16), w_o.astype(BF16))

    fin_ctx = fin[0]
    new_re = jnp.transpose(fin_ctx[:, :, :2 * P].reshape(G, BATCH, 2, P), (1, 2, 0, 3))
    new_im = jnp.transpose(fin_ctx[:, :, 2 * P:].reshape(G, BATCH, 2, P), (1, 2, 0, 3))
    return yp, ys_out, kf32, vf32, new_re, new_im


def kernel(x_prompt, x_sample, cache_k, cache_v, state_ssm_re, state_ssm_im, c, c_ctx, norm_w, w_ada, b_ada,
           w_in, q_norm_w, k_norm_w, rel_pos_bias, ssm_a_re, ssm_a_im, ssm_log_dt, ssm_b_re, ssm_b_im,
           ssm_c_re, ssm_c_im, ssm_d, w_glu, b_glu, w_ssm_out, w_att_out, w_o):
    depth = norm_w.shape[0]
    xp = x_prompt.reshape(N_CTX_TOK, D_MODEL)
    xs = x_sample.reshape(N_DEC_TOK, D_MODEL)
    new_k, new_v, new_re, new_im = [], [], [], []
    for l in range(depth):
        xp, xs, kl, vl, rl, il = _layer(
            xp, xs, cache_k[:, l], cache_v[:, l], state_ssm_re[:, l], state_ssm_im[:, l], c, c_ctx,
            norm_w[l], w_ada[l], b_ada[l], w_in[l], q_norm_w[l], k_norm_w[l], rel_pos_bias[l],
            ssm_a_re[l], ssm_a_im[l], ssm_log_dt[l], ssm_b_re[l], ssm_b_im[l], ssm_c_re[l], ssm_c_im[l],
            ssm_d[l], w_glu[l], b_glu[l], w_ssm_out[l], w_att_out[l], w_o[l])
        new_k.append(kl.reshape(BATCH, SEQ, N_HEADS, HEAD_DIM))
        new_v.append(vl.reshape(BATCH, SEQ, N_HEADS, HEAD_DIM))
        new_re.append(rl)
        new_im.append(il)
    return (xp.reshape(BATCH, SEQ, D_MODEL), xs.reshape(DEC_BATCH, DEC_SEQ, D_MODEL),
            jnp.stack(new_k, axis=1), jnp.stack(new_v, axis=1),
            jnp.stack(new_re, axis=1), jnp.stack(new_im, axis=1))
```

```python
import functools
import math

import jax
import jax.numpy as jnp
from jax import lax
from jax.experimental import pallas as pl
from jax.experimental.pallas import tpu as pltpu

D_MODEL = 2048
BATCH = 16
SEQ = 256
DEC_BATCH = 2
DEC_SEQ = 2048
PAST_LEN = 512
GRID_W = 64
GRID_H = DEC_SEQ // GRID_W
SSM_WIDTH = D_MODEL // 2
SSM_GROUP = 16
SSM_GROUPS = SSM_WIDTH // SSM_GROUP
SSM_STATE = 64
N_HEADS = 16
HEAD_DIM = 64
ATT_WIDTH = N_HEADS * HEAD_DIM
WIN_ROWS = 8
WIN_COLS = 16
EPS = 1e-6
NEG_INF = -1e30

N_CTX_TOK = BATCH * SEQ
N_DEC_TOK = DEC_BATCH * DEC_SEQ
N_TOK = N_CTX_TOK + N_DEC_TOK

OFF_U = 0
OFF_ZS = SSM_WIDTH
OFF_Q = 2 * SSM_WIDTH
OFF_K = OFF_Q + ATT_WIDTH
OFF_V = OFF_K + ATT_WIDTH
OFF_ZA = OFF_V + ATT_WIDTH
OFF_GS = OFF_ZA + ATT_WIDTH
OFF_GA = OFF_GS + D_MODEL

CHUNK = 16
FLAT = CHUNK * SSM_GROUP
N_SEG = 16
CH_PER_SEG = 16
DSTATE = 2 * SSM_STATE

Q_ROWS_PER_BLOCK = 4
NBR_Q = Q_ROWS_PER_BLOCK * GRID_W
NBR_KROWS = 12
NBR_K = NBR_KROWS * GRID_W

V7X_VMEM_BYTES = 64 * 1024 * 1024
VMEM_LIMIT = V7X_VMEM_BYTES - 8 * 1024 * 1024
QKV_VMEM_LIMIT = V7X_VMEM_BYTES - 4 * 1024 * 1024

LANES = 128
MXU_TILE = 256

F32 = jnp.float32
BF16 = jnp.bfloat16
HIGHEST = lax.Precision.HIGHEST


def _sigmoid(x):
    return 1.0 / (1.0 + jnp.exp(-x))


def _silu(x):
    return x * _sigmoid(x)


def _gelu_tanh(x):
    return 0.5 * x * (1.0 + jnp.tanh(math.sqrt(2.0 / math.pi) * (x + 0.044715 * (x * x * x))))


def _cmul(ar, ai, br, bi):
    return ar * br - ai * bi, ar * bi + ai * br


def _dot_nt(a, b):
    return lax.dot_general(a, b, (((1,), (1,)), ((), ())), preferred_element_type=F32)


def _params(*sem, vmem_limit=VMEM_LIMIT):
    return pltpu.CompilerParams(dimension_semantics=sem, vmem_limit_bytes=vmem_limit)


MOD_TN = 1536


def _mod_kernel(cond_ref, w_ref, b_ref, o_ref):
    c = cond_ref[...]
    s = _silu(c).astype(BF16)
    o_ref[...] = jnp.dot(s, w_ref[...].astype(BF16), preferred_element_type=F32) + b_ref[...]


def _modulation(cond8, w_ada, b_ada):
    tn = MOD_TN
    n = w_ada.shape[1]
    return pl.pallas_call(
        _mod_kernel,
        grid=(n // tn,),
        in_specs=[pl.BlockSpec((8, D_MODEL), lambda j: (0, 0)),
                  pl.BlockSpec((D_MODEL, tn), lambda j: (0, j)),
                  pl.BlockSpec((1, tn), lambda j: (0, j))],
        out_specs=pl.BlockSpec((8, tn), lambda j: (0, j)),
        out_shape=jax.ShapeDtypeStruct((8, n), F32),
        compiler_params=_params("arbitrary"),
        name="modulation",
    )(cond8, w_ada, b_ada.reshape(1, n))


NORM_TM = 1024
NORM_ROWS = 16
NORM_UNROLL = 8
NORM_CTX_STEPS = N_CTX_TOK // NORM_TM
NORM_STEPS_PER_DEC_SEQ = DEC_SEQ // NORM_TM


def _mod_row(i, ctx_steps, steps_per_seq):
    return jnp.where(i < ctx_steps, 0, 1 + (i - ctx_steps) // steps_per_seq)


def _norm_kernel(xp_ref, xs_ref, shift_ref, scale_ref, nw_ref, o_ref):
    i = pl.program_id(0)

    gain = nw_ref[...] * (1.0 + scale_ref[0])
    shift = shift_ref[0]

    def body(x_ref):
        def rows(r, carry):
            sl = pl.ds(pl.multiple_of(r * NORM_ROWS, NORM_ROWS), NORM_ROWS)
            x = x_ref[sl, :]
            ms = jnp.mean(x * x, axis=-1, keepdims=True)
            o_ref[sl, :] = (x * lax.rsqrt(ms + EPS) * gain + shift).astype(BF16)
            return carry

        lax.fori_loop(0, NORM_TM // NORM_ROWS, rows, 0, unroll=NORM_UNROLL)

    @pl.when(i < NORM_CTX_STEPS)
    def _():
        body(xp_ref)

    @pl.when(i >= NORM_CTX_STEPS)
    def _():
        body(xs_ref)


def _norm_modulate(xp, xs, shift3, scale3, norm_w):
    steps = N_TOK // NORM_TM
    row = functools.partial(_mod_row, ctx_steps=NORM_CTX_STEPS, steps_per_seq=NORM_STEPS_PER_DEC_SEQ)
    return pl.pallas_call(
        _norm_kernel,
        grid=(steps,),
        in_specs=[pl.BlockSpec((NORM_TM, D_MODEL), lambda i: (jnp.minimum(i, NORM_CTX_STEPS - 1), 0)),
                  pl.BlockSpec((NORM_TM, D_MODEL), lambda i: (jnp.maximum(i - NORM_CTX_STEPS, 0), 0)),
                  pl.BlockSpec((1, 1, D_MODEL), lambda i: (row(i), 0, 0)),
                  pl.BlockSpec((1, 1, D_MODEL), lambda i: (row(i), 0, 0)),
                  pl.BlockSpec((1, D_MODEL), lambda i: (0, 0))],
        out_specs=pl.BlockSpec((NORM_TM, D_MODEL), lambda i: (i, 0)),
        out_shape=jax.ShapeDtypeStruct((N_TOK, D_MODEL), BF16),
        compiler_params=_params("arbitrary"),
        name="norm_modulate",
    )(xp, xs, shift3, scale3, norm_w.reshape(1, D_MODEL))


PROJ_TM = 1024
PROJ_TN = 1024
PROJ_CTX_STEPS = N_CTX_TOK // PROJ_TM
PROJ_SEGS = PROJ_TM // (CHUNK * CH_PER_SEG)


def _head_group_ones():
    r = lax.broadcasted_iota(jnp.int32, (MXU_TILE, MXU_TILE), 0) // HEAD_DIM
    c = lax.broadcasted_iota(jnp.int32, (MXU_TILE, MXU_TILE), 1) // HEAD_DIM
    return jnp.where(r == c, 1.0, 0.0).astype(BF16)


def _head_rms(acc, nw):
    ones = _head_group_ones()
    outs = []
    for c in range(acc.shape[1] // MXU_TILE):
        a = acc[:, c * MXU_TILE:(c + 1) * MXU_TILE]
        ssum = jnp.dot((a * a).astype(BF16), ones, preferred_element_type=F32)
        outs.append(a * lax.rsqrt(ssum * (1.0 / HEAD_DIM) + EPS))
    return jnp.concatenate(outs, axis=1) * nw


def _proj_tile(h_ref, w_ref, wbf):
    @pl.when(pl.program_id(1) == 0)
    def _():
        wbf[...] = w_ref[...].astype(BF16)

    return jnp.dot(h_ref[...], wbf[...], preferred_element_type=F32)


def _proj_s5_kernel(h_ref, w_ref, u_ref, zs_ref, wbf):
    j = pl.program_id(0)
    acc = _proj_tile(h_ref, w_ref, wbf)

    @pl.when(j == 0)
    def _():
        for seg in range(PROJ_SEGS):
            for ch in range(CH_PER_SEG):
                r0 = (seg * CH_PER_SEG + ch) * CHUNK
                u_ref[0, ch, seg] = acc[r0:r0 + CHUNK, :]

    @pl.when(j == 1)
    def _():
        zs_ref[...] = acc.astype(BF16)


def _proj_qkv_kernel(h_ref, w_ref, nw_ref, o_ref, kf_ref, vf_ref, wbf):
    j = pl.program_id(0)
    i = pl.program_id(1)
    acc = _proj_tile(h_ref, w_ref, wbf)

    @pl.when(j == 0)
    def _():
        o_ref[...] = (_head_rms(acc, nw_ref[0]) * (HEAD_DIM ** -0.5)).astype(BF16)

    @pl.when(j == 1)
    def _():
        kn = _head_rms(acc, nw_ref[0])
        o_ref[...] = kn.astype(BF16)

        @pl.when(i < PROJ_CTX_STEPS)
        def _():
            kf_ref[...] = kn

    @pl.when(j == 2)
    def _():
        o_ref[...] = acc.astype(BF16)

        @pl.when(i < PROJ_CTX_STEPS)
        def _():
            vf_ref[...] = acc


def _proj_gate_kernel(h_ref, w_ref, o_ref, wbf):
    o_ref[...] = _proj_tile(h_ref, w_ref, wbf).astype(BF16)


def _in_proj(h, w_in, q_norm_w, k_norm_w):
    ni = N_TOK // PROJ_TM
    last = ni - 1
    ctx_last = PROJ_CTX_STEPS - 1
    tiles_per_path = N_SEG // PROJ_SEGS
    h_spec = pl.BlockSpec((PROJ_TM, D_MODEL), lambda j, i: (i, 0))
    scratch = [pltpu.VMEM((D_MODEL, PROJ_TN), BF16)]
    params = _params("arbitrary", "arbitrary")

    def u_map(j, i):
        ii = jnp.where(j == 0, i, last)
        return (ii // tiles_per_path, 0, ii % tiles_per_path, 0, 0)

    u, zs = pl.pallas_call(
        _proj_s5_kernel,
        grid=(2, ni),
        in_specs=[h_spec, pl.BlockSpec((D_MODEL, PROJ_TN), lambda j, i: (0, OFF_U // PROJ_TN + j))],
        out_specs=[pl.BlockSpec((1, CH_PER_SEG, PROJ_SEGS, CHUNK, PROJ_TN), u_map),
                   pl.BlockSpec((PROJ_TM, PROJ_TN), lambda j, i: (jnp.where(j == 0, 0, i), 0))],
        out_shape=[jax.ShapeDtypeStruct((2, CH_PER_SEG, N_SEG, CHUNK, SSM_WIDTH), F32),
                   jax.ShapeDtypeStruct((N_TOK, SSM_WIDTH), BF16)],
        scratch_shapes=scratch, compiler_params=params, name="in_proj_s5",
    )(h, w_in)

    nw = jnp.stack([jnp.tile(q_norm_w.reshape(1, HEAD_DIM), (1, PROJ_TN // HEAD_DIM)),
                    jnp.tile(k_norm_w.reshape(1, HEAD_DIM), (1, PROJ_TN // HEAD_DIM))])
    qkv, kf32, vf32 = pl.pallas_call(
        _proj_qkv_kernel,
        grid=(3, ni),
        in_specs=[h_spec, pl.BlockSpec((D_MODEL, PROJ_TN), lambda j, i: (0, OFF_Q // PROJ_TN + j)),
                  pl.BlockSpec((1, 1, PROJ_TN), lambda j, i: (jnp.where(j == 1, 1, 0), 0, 0))],
        out_specs=[pl.BlockSpec((PROJ_TM, PROJ_TN), lambda j, i: (i, j)),
                   pl.BlockSpec((PROJ_TM, PROJ_TN),
                                lambda j, i: (jnp.where(j == 0, 0, jnp.where(j == 1, jnp.minimum(i, ctx_last),
                                                                             ctx_last)), 0)),
                   pl.BlockSpec((PROJ_TM, PROJ_TN),
                                lambda j, i: (jnp.where(j == 2, jnp.minimum(i, ctx_last), 0), 0))],
        out_shape=[jax.ShapeDtypeStruct((N_TOK, 3 * ATT_WIDTH), BF16),
                   jax.ShapeDtypeStruct((N_CTX_TOK, ATT_WIDTH), F32),
                   jax.ShapeDtypeStruct((N_CTX_TOK, ATT_WIDTH), F32)],
        scratch_shapes=scratch, name="in_proj_qkv",
        compiler_params=_params("arbitrary", "arbitrary", vmem_limit=QKV_VMEM_LIMIT),
    )(h, w_in, nw)

    gate_tiles = 2 * D_MODEL // PROJ_TN
    gz = pl.pallas_call(
        _proj_gate_kernel,
        grid=(gate_tiles + 1, ni),
        in_specs=[h_spec,
                  pl.BlockSpec((D_MODEL, PROJ_TN),
                               lambda j, i: (0, jnp.where(j < gate_tiles, OFF_GS // PROJ_TN + j,
                                                          OFF_ZA // PROJ_TN)))],
        out_specs=pl.BlockSpec((PROJ_TM, PROJ_TN), lambda j, i: (i, j)),
        out_shape=jax.ShapeDtypeStruct((N_TOK, 2 * D_MODEL + ATT_WIDTH), BF16),
        scratch_shapes=scratch, compiler_params=params, name="in_proj_gates",
    )(h, w_in)
    return u, zs, qkv, kf32, vf32, gz


def _pow_select(e, pows):
    rr = jnp.where((e & 1) != 0, pows[0][0], 1.0)
    ri = jnp.where((e & 1) != 0, pows[0][1], 0.0)
    for b in range(1, len(pows)):
        bit = (e & (1 << b)) != 0
        fr = jnp.where(bit, pows[b][0], 1.0)
        fi = jnp.where(bit, pows[b][1], 0.0)
        rr, ri = _cmul(rr, ri, fr, fi)
    return rr, ri


def _discretize(a_re, a_im, log_dt):
    lr = jnp.minimum(a_re, -1e-4)
    li = a_im
    dt = jnp.exp(log_dt)
    mag = jnp.exp(lr * dt)
    br = mag * jnp.cos(li * dt)
    bi = mag * jnp.sin(li * dt)
    den = lr * lr + li * li
    nr = br - 1.0
    cr = (nr * lr + bi * li) / den
    ci = (bi * lr - nr * li) / den
    return (br, bi), (cr, ci)


def _squarings(pr, pi, n):
    out = [(pr, pi)]
    for _ in range(n):
        pr, pi = _cmul(pr, pi, pr, pi)
        out.append((pr, pi))
    return out


def _dot_nt_exact(a, b):
    return lax.dot_general(a, b, (((1,), (1,)), ((), ())), preferred_element_type=F32, precision=HIGHEST)


def _ssm_build_ops(n_groups, rowp_ref, btr_ref, bti_ref, ctr_ref, cti_ref, t_ref, s_ref, rt_ref, lam_ref, powc_ref):
    lane_b = lax.broadcasted_iota(jnp.int32, (SSM_GROUP, DSTATE), 1)
    lane_t = lax.broadcasted_iota(jnp.int32, (SSM_GROUP, FLAT), 1)
    row_c = lax.broadcasted_iota(jnp.int32, (CH_PER_SEG, DSTATE), 0)
    lane_c = lax.broadcasted_iota(jnp.int32, (CH_PER_SEG, DSTATE), 1)
    exp_c = jnp.where(lane_c < SSM_STATE, row_c, (CH_PER_SEG - 1) - row_c)

    for g in range(n_groups):
        rp = rowp_ref[g]
        (lbr, lbi), (cfr, cfi) = _discretize(rp[0:1], rp[1:2], rp[2:3])
        sq = _squarings(lbr, lbi, 8)
        bbr, bbi = _cmul(btr_ref[g], bti_ref[g], cfr, cfi)

        ptr, pti = _pow_select(exp_c, sq[:4])

        s_re, s_im = [], []
        for sp in range(CHUNK):
            e = CHUNK - 1 - sp
            br, bi = _cmul(bbr, bbi, ptr[e:e + 1], pti[e:e + 1])
            s_re.append(br)
            s_im.append(bi)
        s_ref[g] = jnp.concatenate([jnp.concatenate(s_re, axis=0), jnp.concatenate(s_im, axis=0)],
                                   axis=1).astype(BF16)

        xcr = jnp.concatenate([ctr_ref[g]] * CHUNK, axis=0)
        xci = jnp.concatenate([cti_ref[g]] * CHUNK, axis=0)
        xpr = jnp.broadcast_to(ptr[:, None, :], (CHUNK, SSM_GROUP, DSTATE)).reshape(FLAT, DSTATE)
        xpi = jnp.broadcast_to(pti[:, None, :], (CHUNK, SSM_GROUP, DSTATE)).reshape(FLAT, DSTATE)
        ykr, yki = _cmul(xcr, xci, xpr, xpi)
        yrr, yri = _cmul(ykr, yki, lbr, lbi)
        rt_ref[g] = jnp.concatenate([yrr, -yri], axis=1).astype(BF16)

        fwd = lane_b < SSM_STATE
        lhs = jnp.concatenate([jnp.where(fwd, bbr, 0.0), jnp.where(fwd, bbi, 0.0),
                               jnp.where(fwd, 0.0, bbr), jnp.where(fwd, 0.0, bbi)], axis=0)
        p1 = _dot_nt_exact(lhs, ykr)
        p2 = _dot_nt_exact(lhs, yki)
        kf = p1[0:16] - p2[16:32]
        kb = p1[32:48] - p2[48:64]
        for sp in range(CHUNK):
            tf = kf if sp == 0 else pltpu.roll(kf, SSM_GROUP * sp, axis=1)
            tf = jnp.where(lane_t >= SSM_GROUP * sp, tf, 0.0)
            shift = (FLAT - SSM_GROUP * (CHUNK - 1 - sp)) % FLAT
            tb = kb if shift == 0 else pltpu.roll(kb, shift, axis=1)
            tb = jnp.where(lane_t < SSM_GROUP * (sp + 1), tb, 0.0)
            t_ref[g, sp * SSM_GROUP:(sp + 1) * SSM_GROUP, :] = (tf + tb).astype(BF16)

        l16 = sq[4]
        l256 = sq[8]
        lam_ref[g] = jnp.concatenate([l16[0], l16[1], l256[0], l256[1],
                                      jnp.zeros((4, DSTATE), F32)], axis=0)
        q16 = _squarings(l16[0], l16[1], 3)
        pcr, pci = _pow_select(exp_c, q16)
        powc_ref[g, 0] = pcr
        powc_ref[g, 1] = pci


SSM_PATHS = ((BATCH, SEQ // (CHUNK * CH_PER_SEG)), (DEC_BATCH, DEC_SEQ // (CHUNK * CH_PER_SEG)))


def _ssm_path(u, s_op, t_op, rt_op, lam, powc_ref, h0r, h0i, nseg):
    lane = lax.broadcasted_iota(jnp.int32, (N_SEG, DSTATE), 1)
    isf = lane < SSM_STATE
    l16r, l16i, l256r, l256i = lam[0:1], lam[1:2], lam[2:3], lam[3:4]

    z = jnp.dot(u, s_op, preferred_element_type=F32)
    zre, zim = z[:, :DSTATE], z[:, DSTATE:]

    hr = jnp.zeros((N_SEG, DSTATE), F32)
    hi = jnp.zeros((N_SEG, DSTATE), F32)
    hist = []
    for k in range(CH_PER_SEG):
        hist.append((hr, hi))
        kb = CH_PER_SEG - 1 - k
        zr = jnp.where(isf, zre[k * N_SEG:(k + 1) * N_SEG], zre[kb * N_SEG:(kb + 1) * N_SEG])
        zi = jnp.where(isf, zim[k * N_SEG:(k + 1) * N_SEG], zim[kb * N_SEG:(kb + 1) * N_SEG])
        nr, ni = _cmul(l16r, l16i, hr, hi)
        hr, hi = nr + zr, ni + zi

    if nseg > 1:
        h0r = jnp.broadcast_to(h0r[:, None, :], (N_SEG // nseg, nseg, DSTATE)).reshape(N_SEG, DSTATE)
        h0i = jnp.broadcast_to(h0i[:, None, :], (N_SEG // nseg, nseg, DSTATE)).reshape(N_SEG, DSTATE)
    seg = lax.broadcasted_iota(jnp.int32, (N_SEG, DSTATE), 0) & (nseg - 1)
    segpow = _squarings(l256r, l256i, max(int(math.log2(nseg)), 0))

    def shifted(x, d):
        dn = jnp.where(seg >= d, pltpu.roll(x, d, axis=0), 0.0)
        up = jnp.where(seg <= nseg - 1 - d, pltpu.roll(x, N_SEG - d, axis=0), 0.0)
        return jnp.where(isf, dn, up)

    pr, pi = hr, hi
    d = 1
    lvl = 0
    while d < nseg:
        ar, ai = _cmul(segpow[lvl][0], segpow[lvl][1], shifted(pr, d), shifted(pi, d))
        pr, pi = pr + ar, pi + ai
        d *= 2
        lvl += 1
    if nseg > 1:
        e_in = jnp.where(isf, seg, nseg - 1 - seg)
        wr, wi = _pow_select(e_in, segpow[:lvl])
        ar, ai = _cmul(wr, wi, h0r, h0i)
        hsr, hsi = shifted(pr, 1) + ar, shifted(pi, 1) + ai
    else:
        hsr, hsi = h0r, h0i
    er, ei = _cmul(segpow[lvl][0], segpow[lvl][1], h0r, h0i)
    fin = jnp.concatenate([pr + er, pi + ei], axis=1)

    rows = []
    for c in range(CH_PER_SEG):
        cb = CH_PER_SEG - 1 - c
        lr = jnp.where(isf, hist[c][0], hist[cb][0])
        li = jnp.where(isf, hist[c][1], hist[cb][1])
        ar, ai = _cmul(powc_ref[0, c:c + 1, :], powc_ref[1, c:c + 1, :], hsr, hsi)
        rows.append(jnp.concatenate([lr + ar, li + ai], axis=1))
    hent = jnp.concatenate(rows, axis=0).astype(BF16)

    y = jnp.dot(u, t_op, preferred_element_type=F32) + _dot_nt(hent, rt_op)
    return y, fin


GROUPS_PER_STEP = LANES // SSM_GROUP


def _block_transpose(arrs, blk):
    n = len(arrs)
    width = arrs[0].shape[1]
    j = lax.broadcasted_iota(jnp.int32, arrs[0].shape, 1) // blk
    k = n // 2
    while k >= 1:
        bit = (j & k) != 0
        new = list(arrs)
        for x in range(n):
            if x & k == 0:
                a, b = arrs[x], arrs[x | k]
                new[x] = jnp.where(bit, pltpu.roll(b, k * blk, axis=1), a)
                new[x | k] = jnp.where(bit, b, pltpu.roll(a, width - k * blk, axis=1))
        arrs = new
        k //= 2
    return arrs


def _ssm_kernel(u_ref, rowp_ref, btr_ref, bti_ref, ctr_ref, cti_ref, d_ref, h0cr_ref, h0ci_ref, h0dr_ref, h0di_ref,
                y_ref, fin_ref, ubuf, ybuf, t_ref, s_ref, r_ref, lam_ref, powc_ref):
    _ssm_build_ops(GROUPS_PER_STEP, rowp_ref, btr_ref, bti_ref, ctr_ref, cti_ref,
                   t_ref, s_ref, r_ref, lam_ref, powc_ref)
    h0 = ((h0cr_ref, h0ci_ref), (h0dr_ref, h0di_ref))
    nhalf = FLAT // LANES
    rows = N_SEG * CH_PER_SEG
    def token_rows(path, tok):
        return pl.ds(path * rows * CHUNK + tok, rows, stride=CHUNK)

    for path, (_, nseg) in enumerate(SSM_PATHS):
        for half in range(nhalf):
            toks = [pltpu.bitcast(u_ref[token_rows(path, half * GROUPS_PER_STEP + sb), :].astype(BF16), jnp.uint32)
                    for sb in range(GROUPS_PER_STEP)]
            grouped = _block_transpose(toks, SSM_GROUP)
            for g in range(GROUPS_PER_STEP):
                ubuf[g, :, half * LANES:(half + 1) * LANES] = pltpu.bitcast(grouped[g], BF16)

        for g in range(GROUPS_PER_STEP):
            y, fin = _ssm_path(ubuf[g], s_ref[g], t_ref[g], r_ref[g], lam_ref[g], powc_ref.at[g],
                               h0[path][0][g], h0[path][1][g], nseg)
            ybuf[g] = y
            fin_ref[path, g] = fin

        for half in range(nhalf):
            grouped = [ybuf[g, :, half * LANES:(half + 1) * LANES] for g in range(GROUPS_PER_STEP)]
            toks = _block_transpose(grouped, SSM_GROUP)
            for sb in range(GROUPS_PER_STEP):
                sel = token_rows(path, half * GROUPS_PER_STEP + sb)
                y_ref[sel, :] = toks[sb] + u_ref[sel, :] * d_ref[...]


def _ssm_scan(u2d, rowp, btr, bti, ctr, cti, dtile, h0c_re, h0c_im, h0d_re, h0d_im):
    G = SSM_GROUPS
    gs = GROUPS_PER_STEP
    rows = N_SEG * CH_PER_SEG
    return pl.pallas_call(
        _ssm_kernel,
        grid=(G // gs,),
        in_specs=[
            pl.BlockSpec((N_TOK, LANES), lambda o: (0, o)),
            pl.BlockSpec((gs, 3, DSTATE), lambda o: (o, 0, 0)),
            pl.BlockSpec((gs, SSM_GROUP, DSTATE), lambda o: (o, 0, 0)),
            pl.BlockSpec((gs, SSM_GROUP, DSTATE), lambda o: (o, 0, 0)),
            pl.BlockSpec((gs, SSM_GROUP, DSTATE), lambda o: (o, 0, 0)),
            pl.BlockSpec((gs, SSM_GROUP, DSTATE), lambda o: (o, 0, 0)),
            pl.BlockSpec((1, LANES), lambda o: (0, o)),
            pl.BlockSpec((gs, BATCH, DSTATE), lambda o: (o, 0, 0)),
            pl.BlockSpec((gs, BATCH, DSTATE), lambda o: (o, 0, 0)),
            pl.BlockSpec((gs, DEC_BATCH, DSTATE), lambda o: (o, 0, 0)),
            pl.BlockSpec((gs, DEC_BATCH, DSTATE), lambda o: (o, 0, 0))],
        out_specs=[pl.BlockSpec((N_TOK, LANES), lambda o: (0, o)),
                   pl.BlockSpec((2, gs, N_SEG, 2 * DSTATE), lambda o: (0, o, 0, 0))],
        out_shape=[jax.ShapeDtypeStruct((N_TOK, SSM_WIDTH), F32),
                   jax.ShapeDtypeStruct((2, G, N_SEG, 2 * DSTATE), F32)],
        scratch_shapes=[pltpu.VMEM((gs, rows, FLAT), BF16), pltpu.VMEM((gs, rows, FLAT), F32),
                        pltpu.VMEM((gs, FLAT, FLAT), BF16), pltpu.VMEM((gs, FLAT, FLAT), BF16),
                        pltpu.VMEM((gs, FLAT, FLAT), BF16), pltpu.VMEM((gs, 8, DSTATE), F32),
                        pltpu.VMEM((gs, 2, CH_PER_SEG, DSTATE), F32)],
        compiler_params=_params("arbitrary"),
        name="ssm_scan",
    )(u2d, rowp, btr, bti, ctr, cti, dtile, h0c_re, h0c_im, h0d_re, h0d_im)


CTX_BB = 4
HEADS_PER_STEP = MXU_TILE // HEAD_DIM
HEAD_LANES = HEADS_PER_STEP * HEAD_DIM


def _softmax_pv(scores, values):
    m = scores[0].max(axis=-1, keepdims=True)
    for s in scores[1:]:
        m = jnp.maximum(m, s.max(axis=-1, keepdims=True))
    l = None
    o = None
    for s, v in zip(scores, values):
        p = jnp.exp(s - m)
        ls = p.sum(axis=-1, keepdims=True)
        os_ = jnp.dot(p.astype(BF16), v, preferred_element_type=F32)
        l = ls if l is None else l + ls
        o = os_ if o is None else o + os_
    return o / l


def _head_of_lane():
    return lax.broadcasted_iota(jnp.int32, (1, HEAD_LANES), 1) // HEAD_DIM


def _merge_heads(outs, za):
    head = _head_of_lane()
    o = outs[-1]
    for hh in range(HEADS_PER_STEP - 2, -1, -1):
        o = jnp.where(head == hh, outs[hh], o)
    return (o * _silu(za.astype(F32))).astype(BF16)


def _ctx_attn_kernel(q_ref, k_ref, v_ref, za_ref, o_ref):
    head = _head_of_lane()
    for b in range(CTX_BB):
        sl = slice(b * SEQ, (b + 1) * SEQ)
        q, k, v = q_ref[sl, :], k_ref[sl, :], v_ref[sl, :]
        outs = []
        for hh in range(HEADS_PER_STEP):
            qh = jnp.where(head == hh, q, jnp.zeros_like(q))
            outs.append(_softmax_pv([_dot_nt(qh, k)], [v]))
        o_ref[sl, :] = _merge_heads(outs, za_ref[sl, :])


HEAD_BLOCKS = ATT_WIDTH // HEAD_LANES
QKV_Q, QKV_K, QKV_V = 0, HEAD_BLOCKS, 2 * HEAD_BLOCKS
GZ_ZA = 2 * D_MODEL // HEAD_LANES


def _ctx_attention(qkv, gz):
    rows = CTX_BB * SEQ

    def spec(first):
        return pl.BlockSpec((rows, HEAD_LANES), lambda b, hq: (b, first + hq))

    return pl.pallas_call(
        _ctx_attn_kernel,
        grid=(BATCH // CTX_BB, N_HEADS // HEADS_PER_STEP),
        in_specs=[spec(QKV_Q), spec(QKV_K), spec(QKV_V), spec(GZ_ZA)],
        out_specs=spec(0),
        out_shape=jax.ShapeDtypeStruct((N_CTX_TOK, ATT_WIDTH), BF16),
        compiler_params=_params("arbitrary", "arbitrary"),
        name="ctx_attention",
    )(qkv, qkv, qkv, gz)


N_DROW = 2 * WIN_ROWS - 1


def _nbr_build_table(band_ref, tbl):
    lane = lax.broadcasted_iota(jnp.int32, (GRID_W, 2 * GRID_W), 1)
    qc = lax.broadcasted_iota(jnp.int32, (GRID_W, 2 * GRID_W), 0)
    kc = lane & (GRID_W - 1)
    cs = jnp.clip(qc - WIN_COLS // 2, 0, GRID_W - WIN_COLS)
    valid = jnp.logical_and(kc >= cs, kc < cs + WIN_COLS)
    for hh in range(HEADS_PER_STEP):
        for dr in range(N_DROW):
            x = jnp.broadcast_to(band_ref[hh, dr:dr + 1, :], (GRID_W, 2 * GRID_W))
            lo = pltpu.roll(x, 0, axis=1, stride=1, stride_axis=0)
            hi = pltpu.roll(x, GRID_W, axis=1, stride=1, stride_axis=0)
            tbl[hh, dr] = jnp.where(valid, jnp.where(lane < GRID_W, lo, hi), NEG_INF)
        tbl[hh, N_DROW] = jnp.full((GRID_W, 2 * GRID_W), NEG_INF, F32)


NBR_KEYS = NBR_K + PAST_LEN
KEY_TILE = MXU_TILE
SOFTMAX_ROWS = 64


def _nbr_block_index(a):
    ks = jnp.clip(Q_ROWS_PER_BLOCK * a - Q_ROWS_PER_BLOCK, 0, GRID_H - NBR_KROWS)
    idx = []
    for ri in range(Q_ROWS_PER_BLOCK):
        r = Q_ROWS_PER_BLOCK * a + ri
        rs = jnp.clip(r - WIN_ROWS // 2, 0, GRID_H - WIN_ROWS)
        row = []
        for kri in range(NBR_KROWS):
            kr = ks + kri
            valid = jnp.logical_and(kr >= rs, kr < rs + WIN_ROWS)
            row.append(jnp.where(valid, kr - r + WIN_ROWS - 1, N_DROW))
        idx.append(row)
    return idx


def _nbr_bias_tile(tbl, hh, idx, t):
    half0 = lax.broadcasted_iota(jnp.int32, (1, 2 * GRID_W), 1) < GRID_W
    krows = KEY_TILE // GRID_W
    rows = []
    for ri in range(Q_ROWS_PER_BLOCK):
        tiles = [jnp.where(half0, tbl[hh, idx[ri][kri]], tbl[hh, idx[ri][kri + 1]])
                 for kri in range(krows * t, krows * (t + 1), 2)]
        rows.append(jnp.concatenate(tiles, axis=1))
    return jnp.concatenate(rows, axis=0)


def _nbr_attn_kernel(q_ref, k_ref, v_ref, kc_ref, vc_ref, band_ref, za_ref, o_ref,
                     tbl, kcat, vcat, s_scr, p_scr, o_scr):
    a = pl.program_id(1)

    @pl.when(a == 0)
    def _():
        _nbr_build_table(band_ref, tbl)

    start = pl.multiple_of(jnp.clip(a * NBR_Q - NBR_Q, 0, DEC_SEQ - NBR_K), NBR_Q)
    head = _head_of_lane()
    idx = _nbr_block_index(a)
    for b in range(DEC_BATCH):
        kcat[b, 0:NBR_K, :] = k_ref[b, pl.ds(start, NBR_K), :]
        kcat[b, NBR_K:NBR_KEYS, :] = kc_ref[b].astype(BF16)
        vcat[b, 0:NBR_K, :] = v_ref[b, pl.ds(start, NBR_K), :]
        vcat[b, NBR_K:NBR_KEYS, :] = vc_ref[b].astype(BF16)

    def scores(b, hh):
        q = q_ref[b]
        qh = jnp.where(head == hh, q, jnp.zeros_like(q))
        mrun = None
        for t in range(NBR_KEYS // KEY_TILE):
            st = _dot_nt(qh, kcat[b, t * KEY_TILE:(t + 1) * KEY_TILE, :])
            if t < NBR_K // KEY_TILE:
                st = st + _nbr_bias_tile(tbl, hh, idx, t)
            s_scr[hh, :, t * KEY_TILE:(t + 1) * KEY_TILE] = st
            mt = jnp.maximum(st[:, :LANES], st[:, LANES:])
            mrun = mt if mrun is None else jnp.maximum(mrun, mt)
        return mrun.max(axis=-1, keepdims=True)

    def attend(b, hh, m):
        ls = []
        for r0 in range(0, NBR_Q, SOFTMAX_ROWS):
            mr = m[r0:r0 + SOFTMAX_ROWS]
            lrun = jnp.zeros((SOFTMAX_ROWS, LANES), F32)
            for c0 in range(0, NBR_KEYS, LANES):
                p = jnp.exp(s_scr[hh, r0:r0 + SOFTMAX_ROWS, c0:c0 + LANES] - mr)
                lrun = lrun + p
                p_scr[hh, r0:r0 + SOFTMAX_ROWS, c0:c0 + LANES] = p.astype(BF16)
            ls.append(lrun.sum(axis=-1, keepdims=True))
        l = jnp.concatenate(ls, axis=0)
        o = jnp.dot(p_scr[hh], vcat[b], preferred_element_type=F32)
        lanes = slice(hh * HEAD_DIM, (hh + 1) * HEAD_DIM)
        o_scr[b, :, lanes] = o[:, lanes] / l

    order = [(b, hh) for b in range(DEC_BATCH) for hh in range(HEADS_PER_STEP)]
    m_cur = scores(*order[0])
    for n, (b, hh) in enumerate(order):
        m_next = scores(*order[n + 1]) if n + 1 < len(order) else None
        attend(b, hh, m_cur)
        m_cur = m_next
    for b in range(DEC_BATCH):
        o_ref[b] = (o_scr[b] * _silu(za_ref[b].astype(F32))).astype(BF16)


def _nbr_attention(qkv4, kc, vc, band, gz4):
    first = N_CTX_TOK // DEC_SEQ // DEC_BATCH
    nblk = GRID_H // Q_ROWS_PER_BLOCK
    hw = HEAD_LANES

    def qspec(col0):
        return pl.BlockSpec((DEC_BATCH, NBR_Q, hw), lambda hp, a: (first, a, col0 + hp))

    def kspec(col0):
        return pl.BlockSpec((DEC_BATCH, DEC_SEQ, hw), lambda hp, a: (first, 0, col0 + hp))

    cspec = pl.BlockSpec((DEC_BATCH, PAST_LEN, hw), lambda hp, a: (0, 0, hp))
    bspec = pl.BlockSpec((HEADS_PER_STEP, N_DROW + 1, 2 * GRID_W), lambda hp, a: (hp, 0, 0))
    return pl.pallas_call(
        _nbr_attn_kernel,
        grid=(N_HEADS // HEADS_PER_STEP, nblk),
        in_specs=[qspec(QKV_Q), kspec(QKV_K), kspec(QKV_V), cspec, cspec, bspec, qspec(GZ_ZA)],
        out_specs=pl.BlockSpec((DEC_BATCH, NBR_Q, hw), lambda hp, a: (0, a, hp)),
        out_shape=jax.ShapeDtypeStruct((DEC_BATCH, DEC_SEQ, ATT_WIDTH), BF16),
        scratch_shapes=[pltpu.VMEM((HEADS_PER_STEP, N_DROW + 1, GRID_W, 2 * GRID_W), F32),
                        pltpu.VMEM((DEC_BATCH, NBR_KEYS, HEAD_LANES), BF16),
                        pltpu.VMEM((DEC_BATCH, NBR_KEYS, HEAD_LANES), BF16),
                        pltpu.VMEM((HEADS_PER_STEP, NBR_Q, NBR_KEYS), F32),
                        pltpu.VMEM((HEADS_PER_STEP, NBR_Q, NBR_KEYS), BF16),
                        pltpu.VMEM((DEC_BATCH, NBR_Q, HEAD_LANES), F32)],
        compiler_params=_params("arbitrary", "arbitrary"),
        name="nbr_attention",
    )(qkv4, qkv4, qkv4, kc, vc, band, gz4)


def _nbr_bias_band(rpb):
    ncol = 2 * WIN_COLS - 1
    fill = jnp.full(rpb.shape[:2] + (2 * GRID_W - ncol,), NEG_INF, F32)
    band = jnp.concatenate([rpb[..., WIN_COLS - 1:], fill, rpb[..., :WIN_COLS - 1]], axis=-1)
    return jnp.pad(band, ((0, 0), (0, 1), (0, 0)))


BACK_TM = CHUNK * CH_PER_SEG
BACK_CTX_STEPS = N_CTX_TOK // BACK_TM
BACK_STEPS_PER_DEC_SEQ = DEC_SEQ // BACK_TM


BACK_WARM = 8


def _back_kernel(xp_ref, xs_ref, y_ref, zs_ref, ac_ref, ad_ref, gs_ref, ga_ref, gate_ref, bglu_ref,
                 wglu_ref, wso_ref, wao_ref, wo_ref, op_ref, os_ref, wglu_bf, wso_bf, wao_bf, wo_bf):
    i = pl.program_id(0)
    step = i - BACK_WARM

    @pl.when(i < BACK_WARM)
    def _():
        for src, dst in ((wglu_ref, wglu_bf), (wso_ref, wso_bf), (wao_ref, wao_bf), (wo_ref, wo_bf)):
            slab = src.shape[0]
            dst[pl.ds(pl.multiple_of(i * slab, slab), slab), :] = src[...].astype(BF16)

    def compute(x, a2):
        ys = _gelu_tanh(y_ref[0].reshape(BACK_TM, SSM_WIDTH))
        t = jnp.dot(ys.astype(BF16), wglu_bf[...], preferred_element_type=F32) + bglu_ref[...]
        ys = ys * _sigmoid(t) * _silu(zs_ref[...].astype(F32))
        p_s = jnp.dot(ys.astype(BF16), wso_bf[...], preferred_element_type=F32)
        p_a = jnp.dot(a2, wao_bf[...], preferred_element_type=F32)
        merged = (_sigmoid(gs_ref[...].astype(F32)) * p_s + _sigmoid(ga_ref[...].astype(F32)) * p_a)
        return x + gate_ref[0] * jnp.dot(merged.astype(BF16), wo_bf[...], preferred_element_type=F32)

    @pl.when(jnp.logical_and(step >= 0, step < BACK_CTX_STEPS))
    def _():
        op_ref[...] = compute(xp_ref[...], ac_ref[...])

    @pl.when(step >= BACK_CTX_STEPS)
    def _():
        os_ref[...] = compute(xs_ref[...], ad_ref[...])


def _back(xp, xs, y, zs, a_ctx, a_dec, gz, gate3, b_glu, w_glu, w_so, w_ao, w_o):
    steps = N_TOK // BACK_TM
    n0 = BACK_CTX_STEPS
    warm = BACK_WARM
    row = functools.partial(_mod_row, ctx_steps=n0, steps_per_seq=BACK_STEPS_PER_DEC_SEQ)

    def tile(i):
        return jnp.maximum(i - warm, 0)

    lo = lambda i: (jnp.minimum(tile(i), n0 - 1), 0)
    hi = lambda i: (jnp.maximum(tile(i) - n0, 0), 0)
    cur = lambda i: (tile(i), 0)
    const = lambda i: (0, 0)
    slab = lambda i: (jnp.minimum(i, warm - 1), 0)

    def weight(w):
        return pl.BlockSpec((w.shape[0] // warm, w.shape[1]), slab)

    return pl.pallas_call(
        _back_kernel,
        grid=(warm + steps,),
        in_specs=[pl.BlockSpec((BACK_TM, D_MODEL), lo),
                  pl.BlockSpec((BACK_TM, D_MODEL), hi),
                  pl.BlockSpec((1, CH_PER_SEG, None, CHUNK, SSM_WIDTH),
                               lambda i: (tile(i) // N_SEG, 0, tile(i) % N_SEG, 0, 0)),
                  pl.BlockSpec((BACK_TM, SSM_WIDTH), cur),
                  pl.BlockSpec((BACK_TM, ATT_WIDTH), lo),
                  pl.BlockSpec((BACK_TM, ATT_WIDTH), hi),
                  pl.BlockSpec((BACK_TM, D_MODEL), lambda i: (tile(i), 0)),
                  pl.BlockSpec((BACK_TM, D_MODEL), lambda i: (tile(i), 1)),
                  pl.BlockSpec((1, 1, D_MODEL), lambda i: (row(tile(i)), 0, 0)),
                  pl.BlockSpec((1, SSM_WIDTH), const),
                  weight(w_glu), weight(w_so), weight(w_ao), weight(w_o)],
        out_specs=[pl.BlockSpec((BACK_TM, D_MODEL), lo),
                   pl.BlockSpec((BACK_TM, D_MODEL), hi)],
        out_shape=[jax.ShapeDtypeStruct((N_CTX_TOK, D_MODEL), F32),
                   jax.ShapeDtypeStruct((N_DEC_TOK, D_MODEL), F32)],
        scratch_shapes=[pltpu.VMEM(w.shape, BF16) for w in (w_glu, w_so, w_ao, w_o)],
        compiler_params=_params("arbitrary"),
        name="gated_output",
    )(xp, xs, y, zs, a_ctx, a_dec, gz, gz, gate3, b_glu.reshape(1, SSM_WIDTH), w_glu, w_so, w_ao, w_o)


def _layer(xp, xs, cache_k, cache_v, st_re, st_im, c, c_ctx, norm_w, w_ada, b_ada, w_in, q_norm_w, k_norm_w,
           rpb, a_re, a_im, log_dt, b_re, b_im, c_re, c_im, d, w_glu, b_glu, w_so, w_ao, w_o):
    G, P = SSM_GROUPS, SSM_STATE

    cond8 = jnp.zeros((8, D_MODEL), F32).at[0].set(c_ctx).at[1:1 + DEC_BATCH].set(c)
    mod = _modulation(cond8, w_ada, b_ada)
    shift3 = mod[:1 + DEC_BATCH, None, :D_MODEL]
    scale3 = mod[:1 + DEC_BATCH, None, D_MODEL:2 * D_MODEL]
    gate3 = mod[:1 + DEC_BATCH, None, 2 * D_MODEL:]

    h = _norm_modulate(xp, xs, shift3, scale3, norm_w)
    u, zs, qkv, kf32, vf32, gz = _in_proj(h, w_in, q_norm_w, k_norm_w)

    arow = jnp.transpose(a_re, (1, 0, 2)).reshape(G, 2 * P)
    airow = jnp.transpose(a_im, (1, 0, 2)).reshape(G, 2 * P)
    dtrow = jnp.broadcast_to(jnp.transpose(log_dt, (1, 0))[:, :, None], (G, 2, P)).reshape(G, 2 * P)
    rowp = jnp.stack([arow, airow, dtrow], axis=1)
    btr = jnp.transpose(b_re, (1, 3, 0, 2)).reshape(G, SSM_GROUP, 2 * P)
    bti = jnp.transpose(b_im, (1, 3, 0, 2)).reshape(G, SSM_GROUP, 2 * P)
    ctr = jnp.transpose(c_re, (1, 2, 0, 3)).reshape(G, SSM_GROUP, 2 * P)
    cti = jnp.transpose(c_im, (1, 2, 0, 3)).reshape(G, SSM_GROUP, 2 * P)

    dtile = d.reshape(1, SSM_WIDTH)
    h0c = jnp.zeros((G, BATCH, 2 * P), F32)
    h0d_re = jnp.transpose(st_re, (2, 0, 1, 3)).reshape(G, DEC_BATCH, 2 * P)
    h0d_im = jnp.transpose(st_im, (2, 0, 1, 3)).reshape(G, DEC_BATCH, 2 * P)
    y2d, fin = _ssm_scan(u.reshape(N_TOK, SSM_WIDTH), rowp, btr, bti, ctr, cti, dtile,
                         h0c, h0c, h0d_re, h0d_im)
    y = y2d.reshape(2, CH_PER_SEG, N_SEG, CHUNK, SSM_WIDTH)

    a_ctx = _ctx_attention(qkv, gz)
    nseq4 = N_TOK // DEC_SEQ
    a_dec = _nbr_attention(qkv.reshape(nseq4, DEC_SEQ, qkv.shape[1]),
                           cache_k.reshape(DEC_BATCH, PAST_LEN, ATT_WIDTH),
                           cache_v.reshape(DEC_BATCH, PAST_LEN, ATT_WIDTH),
                           _nbr_bias_band(rpb.astype(F32)),
                           gz.reshape(nseq4, DEC_SEQ, gz.shape[1]))

    yp, ys_out = _back(xp, xs, y, zs, a_ctx, a_dec.reshape(N_DEC_TOK, ATT_WIDTH), gz, gate3, b_glu,
                       w_glu, w_so, w_ao, w_o)

    fin_ctx = fin[0]
    new_re = jnp.transpose(fin_ctx[:, :, :2 * P].reshape(G, BATCH, 2, P), (1, 2, 0, 3))
    new_im = jnp.transpose(fin_ctx[:, :, 2 * P:].reshape(G, BATCH, 2, P), (1, 2, 0, 3))
    return yp, ys_out, kf32, vf32, new_re, new_im


def kernel(x_prompt, x_sample, cache_k, cache_v, state_ssm_re, state_ssm_im, c, c_ctx, norm_w, w_ada, b_ada,
           w_in, q_norm_w, k_norm_w, rel_pos_bias, ssm_a_re, ssm_a_im, ssm_log_dt, ssm_b_re, ssm_b_im,
           ssm_c_re, ssm_c_im, ssm_d, w_glu, b_glu, w_ssm_out, w_att_out, w_o):
    depth = norm_w.shape[0]
    xp = x_prompt.reshape(N_CTX_TOK, D_MODEL)
    xs = x_sample.reshape(N_DEC_TOK, D_MODEL)
    new_k, new_v, new_re, new_im = [], [], [], []
    for l in range(depth):
        xp, xs, kl, vl, rl, il = _layer(
            xp, xs, cache_k[:, l], cache_v[:, l], state_ssm_re[:, l], state_ssm_im[:, l], c, c_ctx,
            norm_w[l], w_ada[l], b_ada[l], w_in[l], q_norm_w[l], k_norm_w[l], rel_pos_bias[l],
            ssm_a_re[l], ssm_a_im[l], ssm_log_dt[l], ssm_b_re[l], ssm_b_im[l], ssm_c_re[l], ssm_c_im[l],
            ssm_d[l], w_glu[l], b_glu[l], w_ssm_out[l], w_att_out[l], w_o[l])
        new_k.append(kl.reshape(BATCH, SEQ, N_HEADS, HEAD_DIM))
        new_v.append(vl.reshape(BATCH, SEQ, N_HEADS, HEAD_DIM))
        new_re.append(rl)
        new_im.append(il)
    return (xp.reshape(BATCH, SEQ, D_MODEL), xs.reshape(DEC_BATCH, DEC_SEQ, D_MODEL),
            jnp.stack(new_k, axis=1), jnp.stack(new_v, axis=1),
            jnp.stack(new_re, axis=1), jnp.stack(new_im, axis=1))
```

```python
import functools
import math

import jax
import jax.numpy as jnp
from jax import lax
from jax.experimental import pallas as pl
from jax.experimental.pallas import tpu as pltpu

D_MODEL = 2048
BATCH = 16
SEQ = 256
DEC_BATCH = 2
DEC_SEQ = 2048
PAST_LEN = 512
GRID_W = 64
GRID_H = DEC_SEQ // GRID_W
SSM_WIDTH = D_MODEL // 2
SSM_GROUP = 16
SSM_GROUPS = SSM_WIDTH // SSM_GROUP
SSM_STATE = 64
N_HEADS = 16
HEAD_DIM = 64
ATT_WIDTH = N_HEADS * HEAD_DIM
WIN_ROWS = 8
WIN_COLS = 16
EPS = 1e-6
NEG_INF = -1e30

N_CTX_TOK = BATCH * SEQ
N_DEC_TOK = DEC_BATCH * DEC_SEQ
N_TOK = N_CTX_TOK + N_DEC_TOK

OFF_U = 0
OFF_ZS = SSM_WIDTH
OFF_Q = 2 * SSM_WIDTH
OFF_K = OFF_Q + ATT_WIDTH
OFF_V = OFF_K + ATT_WIDTH
OFF_ZA = OFF_V + ATT_WIDTH
OFF_GS = OFF_ZA + ATT_WIDTH
OFF_GA = OFF_GS + D_MODEL

CHUNK = 16
FLAT = CHUNK * SSM_GROUP
N_SEG = 16
CH_PER_SEG = 16
U_PITCH = 24
DSTATE = 2 * SSM_STATE

Q_ROWS_PER_BLOCK = 4
NBR_Q = Q_ROWS_PER_BLOCK * GRID_W
NBR_KROWS = 12
NBR_K = NBR_KROWS * GRID_W

V7X_VMEM_BYTES = 64 * 1024 * 1024
VMEM_LIMIT = V7X_VMEM_BYTES - 8 * 1024 * 1024
QKV_VMEM_LIMIT = V7X_VMEM_BYTES - 4 * 1024 * 1024

LANES = 128
MXU_TILE = 256

F32 = jnp.float32
BF16 = jnp.bfloat16
HIGHEST = lax.Precision.HIGHEST


def _sigmoid(x):
    return 1.0 / (1.0 + jnp.exp(-x))


def _silu(x):
    return x * _sigmoid(x)


def _gelu_tanh(x):
    return 0.5 * x * (1.0 + jnp.tanh(math.sqrt(2.0 / math.pi) * (x + 0.044715 * (x * x * x))))


def _cmul(ar, ai, br, bi):
    return ar * br - ai * bi, ar * bi + ai * br


def _dot_nt(a, b):
    return lax.dot_general(a, b, (((1,), (1,)), ((), ())), preferred_element_type=F32)


def _params(*sem, vmem_limit=VMEM_LIMIT):
    return pltpu.CompilerParams(dimension_semantics=sem, vmem_limit_bytes=vmem_limit)


MOD_TN = 1536


def _mod_kernel(cond_ref, w_ref, b_ref, o_ref):
    c = cond_ref[...]
    s = _silu(c).astype(BF16)
    o_ref[...] = jnp.dot(s, w_ref[...].astype(BF16), preferred_element_type=F32) + b_ref[...]


def _modulation(cond8, w_ada, b_ada):
    tn = MOD_TN
    n = w_ada.shape[1]
    return pl.pallas_call(
        _mod_kernel,
        grid=(n // tn,),
        in_specs=[pl.BlockSpec((8, D_MODEL), lambda j: (0, 0)),
                  pl.BlockSpec((D_MODEL, tn), lambda j: (0, j)),
                  pl.BlockSpec((1, tn), lambda j: (0, j))],
        out_specs=pl.BlockSpec((8, tn), lambda j: (0, j)),
        out_shape=jax.ShapeDtypeStruct((8, n), F32),
        compiler_params=_params("arbitrary"),
        name="modulation",
    )(cond8, w_ada, b_ada.reshape(1, n))


NORM_TM = 1024
NORM_ROWS = 16
NORM_UNROLL = 8
NORM_CTX_STEPS = N_CTX_TOK // NORM_TM
NORM_STEPS_PER_DEC_SEQ = DEC_SEQ // NORM_TM


def _mod_row(i, ctx_steps, steps_per_seq):
    return jnp.where(i < ctx_steps, 0, 1 + (i - ctx_steps) // steps_per_seq)


def _norm_kernel(xp_ref, xs_ref, shift_ref, scale_ref, nw_ref, o_ref):
    i = pl.program_id(0)

    gain = nw_ref[...] * (1.0 + scale_ref[0])
    shift = shift_ref[0]

    def body(x_ref):
        def rows(r, carry):
            sl = pl.ds(pl.multiple_of(r * NORM_ROWS, NORM_ROWS), NORM_ROWS)
            x = x_ref[sl, :]
            ms = jnp.mean(x * x, axis=-1, keepdims=True)
            o_ref[sl, :] = (x * lax.rsqrt(ms + EPS) * gain + shift).astype(BF16)
            return carry

        lax.fori_loop(0, NORM_TM // NORM_ROWS, rows, 0, unroll=NORM_UNROLL)

    @pl.when(i < NORM_CTX_STEPS)
    def _():
        body(xp_ref)

    @pl.when(i >= NORM_CTX_STEPS)
    def _():
        body(xs_ref)


def _norm_modulate(xp, xs, shift3, scale3, norm_w):
    steps = N_TOK // NORM_TM
    row = functools.partial(_mod_row, ctx_steps=NORM_CTX_STEPS, steps_per_seq=NORM_STEPS_PER_DEC_SEQ)
    return pl.pallas_call(
        _norm_kernel,
        grid=(steps,),
        in_specs=[pl.BlockSpec((NORM_TM, D_MODEL), lambda i: (jnp.minimum(i, NORM_CTX_STEPS - 1), 0)),
                  pl.BlockSpec((NORM_TM, D_MODEL), lambda i: (jnp.maximum(i - NORM_CTX_STEPS, 0), 0)),
                  pl.BlockSpec((1, 1, D_MODEL), lambda i: (row(i), 0, 0)),
                  pl.BlockSpec((1, 1, D_MODEL), lambda i: (row(i), 0, 0)),
                  pl.BlockSpec((1, D_MODEL), lambda i: (0, 0))],
        out_specs=pl.BlockSpec((NORM_TM, D_MODEL), lambda i: (i, 0)),
        out_shape=jax.ShapeDtypeStruct((N_TOK, D_MODEL), BF16),
        compiler_params=_params("arbitrary"),
        name="norm_modulate",
    )(xp, xs, shift3, scale3, norm_w.reshape(1, D_MODEL))


PROJ_TM = 1024
PROJ_TN = 1024
PROJ_CTX_STEPS = N_CTX_TOK // PROJ_TM
PROJ_SEGS = PROJ_TM // (CHUNK * CH_PER_SEG)


def _head_group_ones():
    r = lax.broadcasted_iota(jnp.int32, (MXU_TILE, MXU_TILE), 0) // HEAD_DIM
    c = lax.broadcasted_iota(jnp.int32, (MXU_TILE, MXU_TILE), 1) // HEAD_DIM
    return jnp.where(r == c, 1.0, 0.0).astype(BF16)


def _head_rms(acc, nw):
    ones = _head_group_ones()
    outs = []
    for c in range(acc.shape[1] // MXU_TILE):
        a = acc[:, c * MXU_TILE:(c + 1) * MXU_TILE]
        ssum = jnp.dot((a * a).astype(BF16), ones, preferred_element_type=F32)
        outs.append(a * lax.rsqrt(ssum * (1.0 / HEAD_DIM) + EPS))
    return jnp.concatenate(outs, axis=1) * nw


def _proj_tile(h_ref, w_ref, wbf):
    @pl.when(pl.program_id(1) == 0)
    def _():
        wbf[...] = w_ref[...].astype(BF16)

    return jnp.dot(h_ref[...], wbf[...], preferred_element_type=F32)


def _proj_s5_kernel(h_ref, w_ref, u_ref, zs_ref, wbf):
    j = pl.program_id(0)
    acc = _proj_tile(h_ref, w_ref, wbf)

    @pl.when(j == 0)
    def _():
        pad = jnp.zeros((U_PITCH - CHUNK, acc.shape[1]), F32)
        for seg in range(PROJ_SEGS):
            for ch in range(CH_PER_SEG):
                r0 = (seg * CH_PER_SEG + ch) * CHUNK
                u_ref[0, ch, seg, 0:CHUNK] = acc[r0:r0 + CHUNK, :]
                u_ref[0, ch, seg, CHUNK:U_PITCH] = pad

    @pl.when(j == 1)
    def _():
        zs_ref[...] = acc.astype(BF16)


def _proj_qkv_kernel(h_ref, w_ref, nw_ref, o_ref, kf_ref, vf_ref, wbf):
    j = pl.program_id(0)
    i = pl.program_id(1)
    acc = _proj_tile(h_ref, w_ref, wbf)

    @pl.when(j == 0)
    def _():
        o_ref[...] = (_head_rms(acc, nw_ref[0]) * (HEAD_DIM ** -0.5)).astype(BF16)

    @pl.when(j == 1)
    def _():
        kn = _head_rms(acc, nw_ref[0])
        o_ref[...] = kn.astype(BF16)

        @pl.when(i < PROJ_CTX_STEPS)
        def _():
            kf_ref[...] = kn

    @pl.when(j == 2)
    def _():
        o_ref[...] = acc.astype(BF16)

        @pl.when(i < PROJ_CTX_STEPS)
        def _():
            vf_ref[...] = acc


def _proj_gate_kernel(h_ref, w_ref, o_ref, wbf):
    o_ref[...] = _proj_tile(h_ref, w_ref, wbf).astype(BF16)


def _in_proj(h, w_in, q_norm_w, k_norm_w):
    ni = N_TOK // PROJ_TM
    last = ni - 1
    ctx_last = PROJ_CTX_STEPS - 1
    tiles_per_path = N_SEG // PROJ_SEGS
    h_spec = pl.BlockSpec((PROJ_TM, D_MODEL), lambda j, i: (i, 0))
    scratch = [pltpu.VMEM((D_MODEL, PROJ_TN), BF16)]
    params = _params("arbitrary", "arbitrary")

    def u_map(j, i):
        ii = jnp.where(j == 0, i, last)
        return (ii // tiles_per_path, 0, ii % tiles_per_path, 0, 0)

    u, zs = pl.pallas_call(
        _proj_s5_kernel,
        grid=(2, ni),
        in_specs=[h_spec, pl.BlockSpec((D_MODEL, PROJ_TN), lambda j, i: (0, OFF_U // PROJ_TN + j))],
        out_specs=[pl.BlockSpec((1, CH_PER_SEG, PROJ_SEGS, U_PITCH, PROJ_TN), u_map),
                   pl.BlockSpec((PROJ_TM, PROJ_TN), lambda j, i: (jnp.where(j == 0, 0, i), 0))],
        out_shape=[jax.ShapeDtypeStruct((2, CH_PER_SEG, N_SEG, U_PITCH, SSM_WIDTH), F32),
                   jax.ShapeDtypeStruct((N_TOK, SSM_WIDTH), BF16)],
        scratch_shapes=scratch, compiler_params=params, name="in_proj_s5",
    )(h, w_in)

    nw = jnp.stack([jnp.tile(q_norm_w.reshape(1, HEAD_DIM), (1, PROJ_TN // HEAD_DIM)),
                    jnp.tile(k_norm_w.reshape(1, HEAD_DIM), (1, PROJ_TN // HEAD_DIM))])
    qkv, kf32, vf32 = pl.pallas_call(
        _proj_qkv_kernel,
        grid=(3, ni),
        in_specs=[h_spec, pl.BlockSpec((D_MODEL, PROJ_TN), lambda j, i: (0, OFF_Q // PROJ_TN + j)),
                  pl.BlockSpec((1, 1, PROJ_TN), lambda j, i: (jnp.where(j == 1, 1, 0), 0, 0))],
        out_specs=[pl.BlockSpec((PROJ_TM, PROJ_TN), lambda j, i: (i, j)),
                   pl.BlockSpec((PROJ_TM, PROJ_TN),
                                lambda j, i: (jnp.where(j == 0, 0, jnp.where(j == 1, jnp.minimum(i, ctx_last),
                                                                             ctx_last)), 0)),
                   pl.BlockSpec((PROJ_TM, PROJ_TN),
                                lambda j, i: (jnp.where(j == 2, jnp.minimum(i, ctx_last), 0), 0))],
        out_shape=[jax.ShapeDtypeStruct((N_TOK, 3 * ATT_WIDTH), BF16),
                   jax.ShapeDtypeStruct((N_CTX_TOK, ATT_WIDTH), F32),
                   jax.ShapeDtypeStruct((N_CTX_TOK, ATT_WIDTH), F32)],
        scratch_shapes=scratch, name="in_proj_qkv",
        compiler_params=_params("arbitrary", "arbitrary", vmem_limit=QKV_VMEM_LIMIT),
    )(h, w_in, nw)

    gate_tiles = 2 * D_MODEL // PROJ_TN
    gz = pl.pallas_call(
        _proj_gate_kernel,
        grid=(gate_tiles + 1, ni),
        in_specs=[h_spec,
                  pl.BlockSpec((D_MODEL, PROJ_TN),
                               lambda j, i: (0, jnp.where(j < gate_tiles, OFF_GS // PROJ_TN + j,
                                                          OFF_ZA // PROJ_TN)))],
        out_specs=pl.BlockSpec((PROJ_TM, PROJ_TN), lambda j, i: (i, j)),
        out_shape=jax.ShapeDtypeStruct((N_TOK, 2 * D_MODEL + ATT_WIDTH), BF16),
        scratch_shapes=scratch, compiler_params=params, name="in_proj_gates",
    )(h, w_in)
    return u, zs, qkv, kf32, vf32, gz


def _pow_select(e, pows):
    rr = jnp.where((e & 1) != 0, pows[0][0], 1.0)
    ri = jnp.where((e & 1) != 0, pows[0][1], 0.0)
    for b in range(1, len(pows)):
        bit = (e & (1 << b)) != 0
        fr = jnp.where(bit, pows[b][0], 1.0)
        fi = jnp.where(bit, pows[b][1], 0.0)
        rr, ri = _cmul(rr, ri, fr, fi)
    return rr, ri


def _discretize(a_re, a_im, log_dt):
    lr = jnp.minimum(a_re, -1e-4)
    li = a_im
    dt = jnp.exp(log_dt)
    mag = jnp.exp(lr * dt)
    br = mag * jnp.cos(li * dt)
    bi = mag * jnp.sin(li * dt)
    den = lr * lr + li * li
    nr = br - 1.0
    cr = (nr * lr + bi * li) / den
    ci = (bi * lr - nr * li) / den
    return (br, bi), (cr, ci)


def _squarings(pr, pi, n):
    out = [(pr, pi)]
    for _ in range(n):
        pr, pi = _cmul(pr, pi, pr, pi)
        out.append((pr, pi))
    return out


def _dot_nt_exact(a, b):
    return lax.dot_general(a, b, (((1,), (1,)), ((), ())), preferred_element_type=F32, precision=HIGHEST)


def _ssm_build_ops(n_groups, rowp_ref, btr_ref, bti_ref, ctr_ref, cti_ref, t_ref, s_ref, rt_ref, lam_ref, powc_ref):
    lane_b = lax.broadcasted_iota(jnp.int32, (SSM_GROUP, DSTATE), 1)
    lane_t = lax.broadcasted_iota(jnp.int32, (SSM_GROUP, FLAT), 1)
    row_c = lax.broadcasted_iota(jnp.int32, (CH_PER_SEG, DSTATE), 0)
    lane_c = lax.broadcasted_iota(jnp.int32, (CH_PER_SEG, DSTATE), 1)
    exp_c = jnp.where(lane_c < SSM_STATE, row_c, (CH_PER_SEG - 1) - row_c)

    for g in range(n_groups):
        rp = rowp_ref[g]
        (lbr, lbi), (cfr, cfi) = _discretize(rp[0:1], rp[1:2], rp[2:3])
        sq = _squarings(lbr, lbi, 8)
        bbr, bbi = _cmul(btr_ref[g], bti_ref[g], cfr, cfi)

        ptr, pti = _pow_select(exp_c, sq[:4])

        s_re, s_im = [], []
        for sp in range(CHUNK):
            e = CHUNK - 1 - sp
            br, bi = _cmul(bbr, bbi, ptr[e:e + 1], pti[e:e + 1])
            s_re.append(br)
            s_im.append(bi)
        s_ref[g] = jnp.concatenate([jnp.concatenate(s_re, axis=0), jnp.concatenate(s_im, axis=0)],
                                   axis=1).astype(BF16)

        xcr = jnp.concatenate([ctr_ref[g]] * CHUNK, axis=0)
        xci = jnp.concatenate([cti_ref[g]] * CHUNK, axis=0)
        xpr = jnp.broadcast_to(ptr[:, None, :], (CHUNK, SSM_GROUP, DSTATE)).reshape(FLAT, DSTATE)
        xpi = jnp.broadcast_to(pti[:, None, :], (CHUNK, SSM_GROUP, DSTATE)).reshape(FLAT, DSTATE)
        ykr, yki = _cmul(xcr, xci, xpr, xpi)
        yrr, yri = _cmul(ykr, yki, lbr, lbi)
        rt_ref[g] = jnp.concatenate([yrr, -yri], axis=1).astype(BF16)

        fwd = lane_b < SSM_STATE
        lhs = jnp.concatenate([jnp.where(fwd, bbr, 0.0), jnp.where(fwd, bbi, 0.0),
                               jnp.where(fwd, 0.0, bbr), jnp.where(fwd, 0.0, bbi)], axis=0)
        p1 = _dot_nt_exact(lhs, ykr)
        p2 = _dot_nt_exact(lhs, yki)
        kf = p1[0:16] - p2[16:32]
        kb = p1[32:48] - p2[48:64]
        for sp in range(CHUNK):
            tf = kf if sp == 0 else pltpu.roll(kf, SSM_GROUP * sp, axis=1)
            tf = jnp.where(lane_t >= SSM_GROUP * sp, tf, 0.0)
            shift = (FLAT - SSM_GROUP * (CHUNK - 1 - sp)) % FLAT
            tb = kb if shift == 0 else pltpu.roll(kb, shift, axis=1)
            tb = jnp.where(lane_t < SSM_GROUP * (sp + 1), tb, 0.0)
            t_ref[g, sp * SSM_GROUP:(sp + 1) * SSM_GROUP, :] = (tf + tb).astype(BF16)

        l16 = sq[4]
        l256 = sq[8]
        lam_ref[g] = jnp.concatenate([l16[0], l16[1], l256[0], l256[1],
                                      jnp.zeros((4, DSTATE), F32)], axis=0)
        q16 = _squarings(l16[0], l16[1], 3)
        pcr, pci = _pow_select(exp_c, q16)
        powc_ref[g, 0] = pcr
        powc_ref[g, 1] = pci


SSM_PATHS = ((BATCH, SEQ // (CHUNK * CH_PER_SEG)), (DEC_BATCH, DEC_SEQ // (CHUNK * CH_PER_SEG)))


def _ssm_path(u, s_op, t_op, rt_op, lam, powc_ref, h0r, h0i, nseg):
    lane = lax.broadcasted_iota(jnp.int32, (N_SEG, DSTATE), 1)
    isf = lane < SSM_STATE
    l16r, l16i, l256r, l256i = lam[0:1], lam[1:2], lam[2:3], lam[3:4]

    z = jnp.dot(u, s_op, preferred_element_type=F32)
    zre, zim = z[:, :DSTATE], z[:, DSTATE:]

    hr = jnp.zeros((N_SEG, DSTATE), F32)
    hi = jnp.zeros((N_SEG, DSTATE), F32)
    hist = []
    for k in range(CH_PER_SEG):
        hist.append((hr, hi))
        kb = CH_PER_SEG - 1 - k
        zr = jnp.where(isf, zre[k * N_SEG:(k + 1) * N_SEG], zre[kb * N_SEG:(kb + 1) * N_SEG])
        zi = jnp.where(isf, zim[k * N_SEG:(k + 1) * N_SEG], zim[kb * N_SEG:(kb + 1) * N_SEG])
        nr, ni = _cmul(l16r, l16i, hr, hi)
        hr, hi = nr + zr, ni + zi

    if nseg > 1:
        h0r = jnp.broadcast_to(h0r[:, None, :], (N_SEG // nseg, nseg, DSTATE)).reshape(N_SEG, DSTATE)
        h0i = jnp.broadcast_to(h0i[:, None, :], (N_SEG // nseg, nseg, DSTATE)).reshape(N_SEG, DSTATE)
    seg = lax.broadcasted_iota(jnp.int32, (N_SEG, DSTATE), 0) & (nseg - 1)
    segpow = _squarings(l256r, l256i, max(int(math.log2(nseg)), 0))

    def shifted(x, d):
        dn = jnp.where(seg >= d, pltpu.roll(x, d, axis=0), 0.0)
        up = jnp.where(seg <= nseg - 1 - d, pltpu.roll(x, N_SEG - d, axis=0), 0.0)
        return jnp.where(isf, dn, up)

    pr, pi = hr, hi
    d = 1
    lvl = 0
    while d < nseg:
        ar, ai = _cmul(segpow[lvl][0], segpow[lvl][1], shifted(pr, d), shifted(pi, d))
        pr, pi = pr + ar, pi + ai
        d *= 2
        lvl += 1
    if nseg > 1:
        e_in = jnp.where(isf, seg, nseg - 1 - seg)
        wr, wi = _pow_select(e_in, segpow[:lvl])
        ar, ai = _cmul(wr, wi, h0r, h0i)
        hsr, hsi = shifted(pr, 1) + ar, shifted(pi, 1) + ai
    else:
        hsr, hsi = h0r, h0i
    er, ei = _cmul(segpow[lvl][0], segpow[lvl][1], h0r, h0i)
    fin = jnp.concatenate([pr + er, pi + ei], axis=1)

    rows = []
    for c in range(CH_PER_SEG):
        cb = CH_PER_SEG - 1 - c
        lr = jnp.where(isf, hist[c][0], hist[cb][0])
        li = jnp.where(isf, hist[c][1], hist[cb][1])
        ar, ai = _cmul(powc_ref[0, c:c + 1, :], powc_ref[1, c:c + 1, :], hsr, hsi)
        rows.append(jnp.concatenate([lr + ar, li + ai], axis=1))
    hent = jnp.concatenate(rows, axis=0).astype(BF16)

    y = jnp.dot(u, t_op, preferred_element_type=F32) + _dot_nt(hent, rt_op)
    return y, fin


GROUPS_PER_STEP = LANES // SSM_GROUP


def _block_transpose(arrs, blk):
    n = len(arrs)
    width = arrs[0].shape[1]
    j = lax.broadcasted_iota(jnp.int32, arrs[0].shape, 1) // blk
    k = n // 2
    while k >= 1:
        bit = (j & k) != 0
        new = list(arrs)
        for x in range(n):
            if x & k == 0:
                a, b = arrs[x], arrs[x | k]
                new[x] = jnp.where(bit, pltpu.roll(b, k * blk, axis=1), a)
                new[x | k] = jnp.where(bit, b, pltpu.roll(a, width - k * blk, axis=1))
        arrs = new
        k //= 2
    return arrs


def _ssm_kernel(u_ref, rowp_ref, btr_ref, bti_ref, ctr_ref, cti_ref, d_ref, h0cr_ref, h0ci_ref, h0dr_ref, h0di_ref,
                y_ref, fin_ref, ubuf, ybuf, t_ref, s_ref, r_ref, lam_ref, powc_ref):
    _ssm_build_ops(GROUPS_PER_STEP, rowp_ref, btr_ref, bti_ref, ctr_ref, cti_ref,
                   t_ref, s_ref, r_ref, lam_ref, powc_ref)
    h0 = ((h0cr_ref, h0ci_ref), (h0dr_ref, h0di_ref))
    nhalf = FLAT // LANES
    rows = N_SEG * CH_PER_SEG
    def token_rows(path, tok, pitch):
        return pl.ds(path * rows * pitch + tok, rows, stride=pitch)

    for path, (_, nseg) in enumerate(SSM_PATHS):
        for half in range(nhalf):
            toks = [pltpu.bitcast(u_ref[token_rows(path, half * GROUPS_PER_STEP + sb, U_PITCH), :].astype(BF16),
                                  jnp.uint32) for sb in range(GROUPS_PER_STEP)]
            grouped = _block_transpose(toks, SSM_GROUP)
            for g in range(GROUPS_PER_STEP):
                ubuf[g, :, half * LANES:(half + 1) * LANES] = pltpu.bitcast(grouped[g], BF16)

        for g in range(GROUPS_PER_STEP):
            y, fin = _ssm_path(ubuf[g], s_ref[g], t_ref[g], r_ref[g], lam_ref[g], powc_ref.at[g],
                               h0[path][0][g], h0[path][1][g], nseg)
            ybuf[g] = y
            fin_ref[path, g] = fin

        for half in range(nhalf):
            grouped = [ybuf[g, :, half * LANES:(half + 1) * LANES] for g in range(GROUPS_PER_STEP)]
            toks = _block_transpose(grouped, SSM_GROUP)
            for sb in range(GROUPS_PER_STEP):
                tok = half * GROUPS_PER_STEP + sb
                y_ref[token_rows(path, tok, CHUNK), :] = (toks[sb]
                                                         + u_ref[token_rows(path, tok, U_PITCH), :] * d_ref[...])


def _ssm_scan(u2d, rowp, btr, bti, ctr, cti, dtile, h0c_re, h0c_im, h0d_re, h0d_im):
    G = SSM_GROUPS
    gs = GROUPS_PER_STEP
    rows = N_SEG * CH_PER_SEG
    return pl.pallas_call(
        _ssm_kernel,
        grid=(G // gs,),
        in_specs=[
            pl.BlockSpec((N_TOK // CHUNK * U_PITCH, LANES), lambda o: (0, o)),
            pl.BlockSpec((gs, 3, DSTATE), lambda o: (o, 0, 0)),
            pl.BlockSpec((gs, SSM_GROUP, DSTATE), lambda o: (o, 0, 0)),
            pl.BlockSpec((gs, SSM_GROUP, DSTATE), lambda o: (o, 0, 0)),
            pl.BlockSpec((gs, SSM_GROUP, DSTATE), lambda o: (o, 0, 0)),
            pl.BlockSpec((gs, SSM_GROUP, DSTATE), lambda o: (o, 0, 0)),
            pl.BlockSpec((1, LANES), lambda o: (0, o)),
            pl.BlockSpec((gs, BATCH, DSTATE), lambda o: (o, 0, 0)),
            pl.BlockSpec((gs, BATCH, DSTATE), lambda o: (o, 0, 0)),
            pl.BlockSpec((gs, DEC_BATCH, DSTATE), lambda o: (o, 0, 0)),
            pl.BlockSpec((gs, DEC_BATCH, DSTATE), lambda o: (o, 0, 0))],
        out_specs=[pl.BlockSpec((N_TOK, LANES), lambda o: (0, o)),
                   pl.BlockSpec((2, gs, N_SEG, 2 * DSTATE), lambda o: (0, o, 0, 0))],
        out_shape=[jax.ShapeDtypeStruct((N_TOK, SSM_WIDTH), F32),
                   jax.ShapeDtypeStruct((2, G, N_SEG, 2 * DSTATE), F32)],
        scratch_shapes=[pltpu.VMEM((gs, rows, FLAT), BF16), pltpu.VMEM((gs, rows, FLAT), F32),
                        pltpu.VMEM((gs, FLAT, FLAT), BF16), pltpu.VMEM((gs, FLAT, FLAT), BF16),
                        pltpu.VMEM((gs, FLAT, FLAT), BF16), pltpu.VMEM((gs, 8, DSTATE), F32),
                        pltpu.VMEM((gs, 2, CH_PER_SEG, DSTATE), F32)],
        compiler_params=_params("arbitrary"),
        name="ssm_scan",
    )(u2d, rowp, btr, bti, ctr, cti, dtile, h0c_re, h0c_im, h0d_re, h0d_im)


CTX_BB = 4
HEADS_PER_STEP = MXU_TILE // HEAD_DIM
HEAD_LANES = HEADS_PER_STEP * HEAD_DIM


def _softmax_pv(scores, values):
    m = scores[0].max(axis=-1, keepdims=True)
    for s in scores[1:]:
        m = jnp.maximum(m, s.max(axis=-1, keepdims=True))
    l = None
    o = None
    for s, v in zip(scores, values):
        p = jnp.exp(s - m)
        ls = p.sum(axis=-1, keepdims=True)
        os_ = jnp.dot(p.astype(BF16), v, preferred_element_type=F32)
        l = ls if l is None else l + ls
        o = os_ if o is None else o + os_
    return o / l


def _head_of_lane():
    return lax.broadcasted_iota(jnp.int32, (1, HEAD_LANES), 1) // HEAD_DIM


def _merge_heads(outs, za):
    head = _head_of_lane()
    o = outs[-1]
    for hh in range(HEADS_PER_STEP - 2, -1, -1):
        o = jnp.where(head == hh, outs[hh], o)
    return (o * _silu(za.astype(F32))).astype(BF16)


def _ctx_attn_kernel(q_ref, k_ref, v_ref, za_ref, o_ref):
    head = _head_of_lane()
    for b in range(CTX_BB):
        sl = slice(b * SEQ, (b + 1) * SEQ)
        q, k, v = q_ref[sl, :], k_ref[sl, :], v_ref[sl, :]
        outs = []
        for hh in range(HEADS_PER_STEP):
            qh = jnp.where(head == hh, q, jnp.zeros_like(q))
            outs.append(_softmax_pv([_dot_nt(qh, k)], [v]))
        o_ref[sl, :] = _merge_heads(outs, za_ref[sl, :])


HEAD_BLOCKS = ATT_WIDTH // HEAD_LANES
QKV_Q, QKV_K, QKV_V = 0, HEAD_BLOCKS, 2 * HEAD_BLOCKS
GZ_ZA = 2 * D_MODEL // HEAD_LANES


def _ctx_attention(qkv, gz):
    rows = CTX_BB * SEQ

    def spec(first):
        return pl.BlockSpec((rows, HEAD_LANES), lambda b, hq: (b, first + hq))

    return pl.pallas_call(
        _ctx_attn_kernel,
        grid=(BATCH // CTX_BB, N_HEADS // HEADS_PER_STEP),
        in_specs=[spec(QKV_Q), spec(QKV_K), spec(QKV_V), spec(GZ_ZA)],
        out_specs=spec(0),
        out_shape=jax.ShapeDtypeStruct((N_CTX_TOK, ATT_WIDTH), BF16),
        compiler_params=_params("arbitrary", "arbitrary"),
        name="ctx_attention",
    )(qkv, qkv, qkv, gz)


N_DROW = 2 * WIN_ROWS - 1


def _nbr_build_table(band_ref, tbl):
    lane = lax.broadcasted_iota(jnp.int32, (GRID_W, 2 * GRID_W), 1)
    qc = lax.broadcasted_iota(jnp.int32, (GRID_W, 2 * GRID_W), 0)
    kc = lane & (GRID_W - 1)
    cs = jnp.clip(qc - WIN_COLS // 2, 0, GRID_W - WIN_COLS)
    valid = jnp.logical_and(kc >= cs, kc < cs + WIN_COLS)
    for hh in range(HEADS_PER_STEP):
        for dr in range(N_DROW):
            x = jnp.broadcast_to(band_ref[hh, dr:dr + 1, :], (GRID_W, 2 * GRID_W))
            lo = pltpu.roll(x, 0, axis=1, stride=1, stride_axis=0)
            hi = pltpu.roll(x, GRID_W, axis=1, stride=1, stride_axis=0)
            tbl[hh, dr] = jnp.where(valid, jnp.where(lane < GRID_W, lo, hi), NEG_INF)
        tbl[hh, N_DROW] = jnp.full((GRID_W, 2 * GRID_W), NEG_INF, F32)


NBR_KEYS = NBR_K + PAST_LEN
KEY_TILE = MXU_TILE
SOFTMAX_ROWS = 64


def _nbr_block_index(a):
    ks = jnp.clip(Q_ROWS_PER_BLOCK * a - Q_ROWS_PER_BLOCK, 0, GRID_H - NBR_KROWS)
    idx = []
    for ri in range(Q_ROWS_PER_BLOCK):
        r = Q_ROWS_PER_BLOCK * a + ri
        rs = jnp.clip(r - WIN_ROWS // 2, 0, GRID_H - WIN_ROWS)
        row = []
        for kri in range(NBR_KROWS):
            kr = ks + kri
            valid = jnp.logical_and(kr >= rs, kr < rs + WIN_ROWS)
            row.append(jnp.where(valid, kr - r + WIN_ROWS - 1, N_DROW))
        idx.append(row)
    return idx


def _nbr_bias_tile(tbl, hh, idx, t):
    half0 = lax.broadcasted_iota(jnp.int32, (1, 2 * GRID_W), 1) < GRID_W
    krows = KEY_TILE // GRID_W
    rows = []
    for ri in range(Q_ROWS_PER_BLOCK):
        tiles = [jnp.where(half0, tbl[hh, idx[ri][kri]], tbl[hh, idx[ri][kri + 1]])
                 for kri in range(krows * t, krows * (t + 1), 2)]
        rows.append(jnp.concatenate(tiles, axis=1))
    return jnp.concatenate(rows, axis=0)


def _nbr_attn_kernel(q_ref, k_ref, v_ref, kc_ref, vc_ref, band_ref, za_ref, o_ref,
                     tbl, kcat, vcat, s_scr, p_scr, o_scr):
    a = pl.program_id(1)

    @pl.when(a == 0)
    def _():
        _nbr_build_table(band_ref, tbl)

    start = pl.multiple_of(jnp.clip(a * NBR_Q - NBR_Q, 0, DEC_SEQ - NBR_K), NBR_Q)
    head = _head_of_lane()
    idx = _nbr_block_index(a)
    for b in range(DEC_BATCH):
        kcat[b, 0:NBR_K, :] = k_ref[b, pl.ds(start, NBR_K), :]
        kcat[b, NBR_K:NBR_KEYS, :] = kc_ref[b].astype(BF16)
        vcat[b, 0:NBR_K, :] = v_ref[b, pl.ds(start, NBR_K), :]
        vcat[b, NBR_K:NBR_KEYS, :] = vc_ref[b].astype(BF16)

    def scores(b, hh):
        q = q_ref[b]
        qh = jnp.where(head == hh, q, jnp.zeros_like(q))
        mrun = None
        for t in range(NBR_KEYS // KEY_TILE):
            st = _dot_nt(qh, kcat[b, t * KEY_TILE:(t + 1) * KEY_TILE, :])
            if t < NBR_K // KEY_TILE:
                st = st + _nbr_bias_tile(tbl, hh, idx, t)
            s_scr[hh, :, t * KEY_TILE:(t + 1) * KEY_TILE] = st
            mt = jnp.maximum(st[:, :LANES], st[:, LANES:])
            mrun = mt if mrun is None else jnp.maximum(mrun, mt)
        return mrun.max(axis=-1, keepdims=True)

    def attend(b, hh, m):
        ls = []
        for r0 in range(0, NBR_Q, SOFTMAX_ROWS):
            mr = m[r0:r0 + SOFTMAX_ROWS]
            lrun = jnp.zeros((SOFTMAX_ROWS, LANES), F32)
            for c0 in range(0, NBR_KEYS, LANES):
                p = jnp.exp(s_scr[hh, r0:r0 + SOFTMAX_ROWS, c0:c0 + LANES] - mr)
                lrun = lrun + p
                p_scr[hh, r0:r0 + SOFTMAX_ROWS, c0:c0 + LANES] = p.astype(BF16)
            ls.append(lrun.sum(axis=-1, keepdims=True))
        l = jnp.concatenate(ls, axis=0)
        o = jnp.dot(p_scr[hh], vcat[b], preferred_element_type=F32)
        lanes = slice(hh * HEAD_DIM, (hh + 1) * HEAD_DIM)
        o_scr[b, :, lanes] = o[:, lanes] / l

    order = [(b, hh) for b in range(DEC_BATCH) for hh in range(HEADS_PER_STEP)]
    m_cur = scores(*order[0])
    for n, (b, hh) in enumerate(order):
        m_next = scores(*order[n + 1]) if n + 1 < len(order) else None
        attend(b, hh, m_cur)
        m_cur = m_next
    for b in range(DEC_BATCH):
        o_ref[b] = (o_scr[b] * _silu(za_ref[b].astype(F32))).astype(BF16)


def _nbr_attention(qkv4, kc, vc, band, gz4):
    first = N_CTX_TOK // DEC_SEQ // DEC_BATCH
    nblk = GRID_H // Q_ROWS_PER_BLOCK
    hw = HEAD_LANES

    def qspec(col0):
        return pl.BlockSpec((DEC_BATCH, NBR_Q, hw), lambda hp, a: (first, a, col0 + hp))

    def kspec(col0):
        return pl.BlockSpec((DEC_BATCH, DEC_SEQ, hw), lambda hp, a: (first, 0, col0 + hp))

    cspec = pl.BlockSpec((DEC_BATCH, PAST_LEN, hw), lambda hp, a: (0, 0, hp))
    bspec = pl.BlockSpec((HEADS_PER_STEP, N_DROW + 1, 2 * GRID_W), lambda hp, a: (hp, 0, 0))
    return pl.pallas_call(
        _nbr_attn_kernel,
        grid=(N_HEADS // HEADS_PER_STEP, nblk),
        in_specs=[qspec(QKV_Q), kspec(QKV_K), kspec(QKV_V), cspec, cspec, bspec, qspec(GZ_ZA)],
        out_specs=pl.BlockSpec((DEC_BATCH, NBR_Q, hw), lambda hp, a: (0, a, hp)),
        out_shape=jax.ShapeDtypeStruct((DEC_BATCH, DEC_SEQ, ATT_WIDTH), BF16),
        scratch_shapes=[pltpu.VMEM((HEADS_PER_STEP, N_DROW + 1, GRID_W, 2 * GRID_W), F32),
                        pltpu.VMEM((DEC_BATCH, NBR_KEYS, HEAD_LANES), BF16),
                        pltpu.VMEM((DEC_BATCH, NBR_KEYS, HEAD_LANES), BF16),
                        pltpu.VMEM((HEADS_PER_STEP, NBR_Q, NBR_KEYS), F32),
                        pltpu.VMEM((HEADS_PER_STEP, NBR_Q, NBR_KEYS), BF16),
                        pltpu.VMEM((DEC_BATCH, NBR_Q, HEAD_LANES), F32)],
        compiler_params=_params("arbitrary", "arbitrary"),
        name="nbr_attention",
    )(qkv4, qkv4, qkv4, kc, vc, band, gz4)


def _nbr_bias_band(rpb):
    ncol = 2 * WIN_COLS - 1
    fill = jnp.full(rpb.shape[:2] + (2 * GRID_W - ncol,), NEG_INF, F32)
    band = jnp.concatenate([rpb[..., WIN_COLS - 1:], fill, rpb[..., :WIN_COLS - 1]], axis=-1)
    return jnp.pad(band, ((0, 0), (0, 1), (0, 0)))


BACK_TM = CHUNK * CH_PER_SEG
BACK_CTX_STEPS = N_CTX_TOK // BACK_TM
BACK_STEPS_PER_DEC_SEQ = DEC_SEQ // BACK_TM


BACK_WARM = 8


def _back_kernel(xp_ref, xs_ref, y_ref, zs_ref, ac_ref, ad_ref, gs_ref, ga_ref, gate_ref, bglu_ref,
                 wglu_ref, wso_ref, wao_ref, wo_ref, op_ref, os_ref, wglu_bf, wso_bf, wao_bf, wo_bf):
    i = pl.program_id(0)
    step = i - BACK_WARM

    @pl.when(i < BACK_WARM)
    def _():
        for src, dst in ((wglu_ref, wglu_bf), (wso_ref, wso_bf), (wao_ref, wao_bf), (wo_ref, wo_bf)):
            slab = src.shape[0]
            dst[pl.ds(pl.multiple_of(i * slab, slab), slab), :] = src[...].astype(BF16)

    def compute(x, a2):
        ys = _gelu_tanh(y_ref[0].reshape(BACK_TM, SSM_WIDTH))
        t = jnp.dot(ys.astype(BF16), wglu_bf[...], preferred_element_type=F32) + bglu_ref[...]
        ys = ys * _sigmoid(t) * _silu(zs_ref[...].astype(F32))
        p_s = jnp.dot(ys.astype(BF16), wso_bf[...], preferred_element_type=F32)
        p_a = jnp.dot(a2, wao_bf[...], preferred_element_type=F32)
        merged = (_sigmoid(gs_ref[...].astype(F32)) * p_s + _sigmoid(ga_ref[...].astype(F32)) * p_a)
        return x + gate_ref[0] * jnp.dot(merged.astype(BF16), wo_bf[...], preferred_element_type=F32)

    @pl.when(jnp.logical_and(step >= 0, step < BACK_CTX_STEPS))
    def _():
        op_ref[...] = compute(xp_ref[...], ac_ref[...])

    @pl.when(step >= BACK_CTX_STEPS)
    def _():
        os_ref[...] = compute(xs_ref[...], ad_ref[...])


def _back(xp, xs, y, zs, a_ctx, a_dec, gz, gate3, b_glu, w_glu, w_so, w_ao, w_o):
    steps = N_TOK // BACK_TM
    n0 = BACK_CTX_STEPS
    warm = BACK_WARM
    row = functools.partial(_mod_row, ctx_steps=n0, steps_per_seq=BACK_STEPS_PER_DEC_SEQ)

    def tile(i):
        return jnp.maximum(i - warm, 0)

    lo = lambda i: (jnp.minimum(tile(i), n0 - 1), 0)
    hi = lambda i: (jnp.maximum(tile(i) - n0, 0), 0)
    cur = lambda i: (tile(i), 0)
    const = lambda i: (0, 0)
    slab = lambda i: (jnp.minimum(i, warm - 1), 0)

    def weight(w):
        return pl.BlockSpec((w.shape[0] // warm, w.shape[1]), slab)

    return pl.pallas_call(
        _back_kernel,
        grid=(warm + steps,),
        in_specs=[pl.BlockSpec((BACK_TM, D_MODEL), lo),
                  pl.BlockSpec((BACK_TM, D_MODEL), hi),
                  pl.BlockSpec((1, CH_PER_SEG, None, CHUNK, SSM_WIDTH),
                               lambda i: (tile(i) // N_SEG, 0, tile(i) % N_SEG, 0, 0)),
                  pl.BlockSpec((BACK_TM, SSM_WIDTH), cur),
                  pl.BlockSpec((BACK_TM, ATT_WIDTH), lo),
                  pl.BlockSpec((BACK_TM, ATT_WIDTH), hi),
                  pl.BlockSpec((BACK_TM, D_MODEL), lambda i: (tile(i), 0)),
                  pl.BlockSpec((BACK_TM, D_MODEL), lambda i: (tile(i), 1)),
                  pl.BlockSpec((1, 1, D_MODEL), lambda i: (row(tile(i)), 0, 0)),
                  pl.BlockSpec((1, SSM_WIDTH), const),
                  weight(w_glu), weight(w_so), weight(w_ao), weight(w_o)],
        out_specs=[pl.BlockSpec((BACK_TM, D_MODEL), lo),
                   pl.BlockSpec((BACK_TM, D_MODEL), hi)],
        out_shape=[jax.ShapeDtypeStruct((N_CTX_TOK, D_MODEL), F32),
                   jax.ShapeDtypeStruct((N_DEC_TOK, D_MODEL), F32)],
        scratch_shapes=[pltpu.VMEM(w.shape, BF16) for w in (w_glu, w_so, w_ao, w_o)],
        compiler_params=_params("arbitrary"),
        name="gated_output",
    )(xp, xs, y, zs, a_ctx, a_dec, gz, gz, gate3, b_glu.reshape(1, SSM_WIDTH), w_glu, w_so, w_ao, w_o)


def _layer(xp, xs, cache_k, cache_v, st_re, st_im, c, c_ctx, norm_w, w_ada, b_ada, w_in, q_norm_w, k_norm_w,
           rpb, a_re, a_im, log_dt, b_re, b_im, c_re, c_im, d, w_glu, b_glu, w_so, w_ao, w_o):
    G, P = SSM_GROUPS, SSM_STATE

    cond8 = jnp.zeros((8, D_MODEL), F32).at[0].set(c_ctx).at[1:1 + DEC_BATCH].set(c)
    mod = _modulation(cond8, w_ada, b_ada)
    shift3 = mod[:1 + DEC_BATCH, None, :D_MODEL]
    scale3 = mod[:1 + DEC_BATCH, None, D_MODEL:2 * D_MODEL]
    gate3 = mod[:1 + DEC_BATCH, None, 2 * D_MODEL:]

    h = _norm_modulate(xp, xs, shift3, scale3, norm_w)
    u, zs, qkv, kf32, vf32, gz = _in_proj(h, w_in, q_norm_w, k_norm_w)

    arow = jnp.transpose(a_re, (1, 0, 2)).reshape(G, 2 * P)
    airow = jnp.transpose(a_im, (1, 0, 2)).reshape(G, 2 * P)
    dtrow = jnp.broadcast_to(jnp.transpose(log_dt, (1, 0))[:, :, None], (G, 2, P)).reshape(G, 2 * P)
    rowp = jnp.stack([arow, airow, dtrow], axis=1)
    btr = jnp.transpose(b_re, (1, 3, 0, 2)).reshape(G, SSM_GROUP, 2 * P)
    bti = jnp.transpose(b_im, (1, 3, 0, 2)).reshape(G, SSM_GROUP, 2 * P)
    ctr = jnp.transpose(c_re, (1, 2, 0, 3)).reshape(G, SSM_GROUP, 2 * P)
    cti = jnp.transpose(c_im, (1, 2, 0, 3)).reshape(G, SSM_GROUP, 2 * P)

    dtile = d.reshape(1, SSM_WIDTH)
    h0c = jnp.zeros((G, BATCH, 2 * P), F32)
    h0d_re = jnp.transpose(st_re, (2, 0, 1, 3)).reshape(G, DEC_BATCH, 2 * P)
    h0d_im = jnp.transpose(st_im, (2, 0, 1, 3)).reshape(G, DEC_BATCH, 2 * P)
    y2d, fin = _ssm_scan(u.reshape(N_TOK // CHUNK * U_PITCH, SSM_WIDTH), rowp, btr, bti, ctr, cti, dtile,
                         h0c, h0c, h0d_re, h0d_im)
    y = y2d.reshape(2, CH_PER_SEG, N_SEG, CHUNK, SSM_WIDTH)

    a_ctx = _ctx_attention(qkv, gz)
    nseq4 = N_TOK // DEC_SEQ
    a_dec = _nbr_attention(qkv.reshape(nseq4, DEC_SEQ, qkv.shape[1]),
                           cache_k.reshape(DEC_BATCH, PAST_LEN, ATT_WIDTH),
                           cache_v.reshape(DEC_BATCH, PAST_LEN, ATT_WIDTH),
                           _nbr_bias_band(rpb.astype(F32)),
                           gz.reshape(nseq4, DEC_SEQ, gz.shape[1]))

    yp, ys_out = _back(xp, xs, y, zs, a_ctx, a_dec.reshape(N_DEC_TOK, ATT_WIDTH), gz, gate3, b_glu,
                       w_glu, w_so, w_ao, w_o)

    fin_ctx = fin[0]
    new_re = jnp.transpose(fin_ctx[:, :, :2 * P].reshape(G, BATCH, 2, P), (1, 2, 0, 3))
    new_im = jnp.transpose(fin_ctx[:, :, 2 * P:].reshape(G, BATCH, 2, P), (1, 2, 0, 3))
    return yp, ys_out, kf32, vf32, new_re, new_im


def kernel(x_prompt, x_sample, cache_k, cache_v, state_ssm_re, state_ssm_im, c, c_ctx, norm_w, w_ada, b_ada,
           w_in, q_norm_w, k_norm_w, rel_pos_bias, ssm_a_re, ssm_a_im, ssm_log_dt, ssm_b_re, ssm_b_im,
           ssm_c_re, ssm_c_im, ssm_d, w_glu, b_glu, w_ssm_out, w_att_out, w_o):
    depth = norm_w.shape[0]
    xp = x_prompt.reshape(N_CTX_TOK, D_MODEL)
    xs = x_sample.reshape(N_DEC_TOK, D_MODEL)
    new_k, new_v, new_re, new_im = [], [], [], []
    for l in range(depth):
        xp, xs, kl, vl, rl, il = _layer(
            xp, xs, cache_k[:, l], cache_v[:, l], state_ssm_re[:, l], state_ssm_im[:, l], c, c_ctx,
            norm_w[l], w_ada[l], b_ada[l], w_in[l], q_norm_w[l], k_norm_w[l], rel_pos_bias[l],
            ssm_a_re[l], ssm_a_im[l], ssm_log_dt[l], ssm_b_re[l], ssm_b_im[l], ssm_c_re[l], ssm_c_im[l],
            ssm_d[l], w_glu[l], b_glu[l], w_ssm_out[l], w_att_out[l], w_o[l])
        new_k.append(kl.reshape(BATCH, SEQ, N_HEADS, HEAD_DIM))
        new_v.append(vl.reshape(BATCH, SEQ, N_HEADS, HEAD_DIM))
        new_re.append(rl)
        new_im.append(il)
    return (xp.reshape(BATCH, SEQ, D_MODEL), xs.reshape(DEC_BATCH, DEC_SEQ, D_MODEL),
            jnp.stack(new_k, axis=1), jnp.stack(new_v, axis=1),
            jnp.stack(new_re, axis=1), jnp.stack(new_im, axis=1))
```

```python
import functools
import math

import jax
import jax.numpy as jnp
from jax import lax
from jax.experimental import pallas as pl
from jax.experimental.pallas import tpu as pltpu

D_MODEL = 2048
BATCH = 16
SEQ = 256
DEC_BATCH = 2
DEC_SEQ = 2048
PAST_LEN = 512
GRID_W = 64
GRID_H = DEC_SEQ // GRID_W
SSM_WIDTH = D_MODEL // 2
SSM_GROUP = 16
SSM_GROUPS = SSM_WIDTH // SSM_GROUP
SSM_STATE = 64
N_HEADS = 16
HEAD_DIM = 64
ATT_WIDTH = N_HEADS * HEAD_DIM
WIN_ROWS = 8
WIN_COLS = 16
EPS = 1e-6
NEG_INF = -1e30

N_CTX_TOK = BATCH * SEQ
N_DEC_TOK = DEC_BATCH * DEC_SEQ
N_TOK = N_CTX_TOK + N_DEC_TOK

OFF_U = 0
OFF_ZS = SSM_WIDTH
OFF_Q = 2 * SSM_WIDTH
OFF_K = OFF_Q + ATT_WIDTH
OFF_V = OFF_K + ATT_WIDTH
OFF_ZA = OFF_V + ATT_WIDTH
OFF_GS = OFF_ZA + ATT_WIDTH
OFF_GA = OFF_GS + D_MODEL

CHUNK = 16
FLAT = CHUNK * SSM_GROUP
N_SEG = 16
CH_PER_SEG = 16
U_PITCH = 24
DSTATE = 2 * SSM_STATE

Q_ROWS_PER_BLOCK = 4
NBR_Q = Q_ROWS_PER_BLOCK * GRID_W
NBR_KROWS = 12
NBR_K = NBR_KROWS * GRID_W

V7X_VMEM_BYTES = 64 * 1024 * 1024
VMEM_LIMIT = V7X_VMEM_BYTES - 8 * 1024 * 1024
QKV_VMEM_LIMIT = V7X_VMEM_BYTES - 4 * 1024 * 1024

LANES = 128
MXU_TILE = 256

F32 = jnp.float32
BF16 = jnp.bfloat16
HIGHEST = lax.Precision.HIGHEST


def _sigmoid(x):
    return 1.0 / (1.0 + jnp.exp(-x))


def _silu(x):
    return x * _sigmoid(x)


def _gelu_tanh(x):
    return 0.5 * x * (1.0 + jnp.tanh(math.sqrt(2.0 / math.pi) * (x + 0.044715 * (x * x * x))))


def _cmul(ar, ai, br, bi):
    return ar * br - ai * bi, ar * bi + ai * br


def _dot_nt(a, b):
    return lax.dot_general(a, b, (((1,), (1,)), ((), ())), preferred_element_type=F32)


def _params(*sem, vmem_limit=VMEM_LIMIT):
    return pltpu.CompilerParams(dimension_semantics=sem, vmem_limit_bytes=vmem_limit)


MOD_TN = 1536


def _mod_kernel(cond_ref, w_ref, b_ref, o_ref):
    c = cond_ref[...]
    s = _silu(c).astype(BF16)
    o_ref[...] = jnp.dot(s, w_ref[...].astype(BF16), preferred_element_type=F32) + b_ref[...]


def _modulation(cond8, w_ada, b_ada):
    tn = MOD_TN
    n = w_ada.shape[1]
    return pl.pallas_call(
        _mod_kernel,
        grid=(n // tn,),
        in_specs=[pl.BlockSpec((8, D_MODEL), lambda j: (0, 0)),
                  pl.BlockSpec((D_MODEL, tn), lambda j: (0, j)),
                  pl.BlockSpec((1, tn), lambda j: (0, j))],
        out_specs=pl.BlockSpec((8, tn), lambda j: (0, j)),
        out_shape=jax.ShapeDtypeStruct((8, n), F32),
        compiler_params=_params("arbitrary"),
        name="modulation",
    )(cond8, w_ada, b_ada.reshape(1, n))


NORM_TM = 1024
NORM_ROWS = 16
NORM_UNROLL = 8
NORM_CTX_STEPS = N_CTX_TOK // NORM_TM
NORM_STEPS_PER_DEC_SEQ = DEC_SEQ // NORM_TM


def _mod_row(i, ctx_steps, steps_per_seq):
    return jnp.where(i < ctx_steps, 0, 1 + (i - ctx_steps) // steps_per_seq)


def _norm_kernel(xp_ref, xs_ref, shift_ref, scale_ref, nw_ref, o_ref):
    i = pl.program_id(0)

    gain = nw_ref[...] * (1.0 + scale_ref[0])
    shift = shift_ref[0]

    def body(x_ref):
        def rows(r, carry):
            sl = pl.ds(pl.multiple_of(r * NORM_ROWS, NORM_ROWS), NORM_ROWS)
            x = x_ref[sl, :]
            ms = jnp.mean(x * x, axis=-1, keepdims=True)
            o_ref[sl, :] = (x * lax.rsqrt(ms + EPS) * gain + shift).astype(BF16)
            return carry

        lax.fori_loop(0, NORM_TM // NORM_ROWS, rows, 0, unroll=NORM_UNROLL)

    @pl.when(i < NORM_CTX_STEPS)
    def _():
        body(xp_ref)

    @pl.when(i >= NORM_CTX_STEPS)
    def _():
        body(xs_ref)


def _norm_modulate(xp, xs, shift3, scale3, norm_w):
    steps = N_TOK // NORM_TM
    row = functools.partial(_mod_row, ctx_steps=NORM_CTX_STEPS, steps_per_seq=NORM_STEPS_PER_DEC_SEQ)
    return pl.pallas_call(
        _norm_kernel,
        grid=(steps,),
        in_specs=[pl.BlockSpec((NORM_TM, D_MODEL), lambda i: (jnp.minimum(i, NORM_CTX_STEPS - 1), 0)),
                  pl.BlockSpec((NORM_TM, D_MODEL), lambda i: (jnp.maximum(i - NORM_CTX_STEPS, 0), 0)),
                  pl.BlockSpec((1, 1, D_MODEL), lambda i: (row(i), 0, 0)),
                  pl.BlockSpec((1, 1, D_MODEL), lambda i: (row(i), 0, 0)),
                  pl.BlockSpec((1, D_MODEL), lambda i: (0, 0))],
        out_specs=pl.BlockSpec((NORM_TM, D_MODEL), lambda i: (i, 0)),
        out_shape=jax.ShapeDtypeStruct((N_TOK, D_MODEL), BF16),
        compiler_params=_params("arbitrary"),
        name="norm_modulate",
    )(xp, xs, shift3, scale3, norm_w.reshape(1, D_MODEL))


PROJ_TM = 1024
PROJ_TN = 1024
PROJ_CTX_STEPS = N_CTX_TOK // PROJ_TM
PROJ_SEGS = PROJ_TM // (CHUNK * CH_PER_SEG)


def _head_group_ones():
    r = lax.broadcasted_iota(jnp.int32, (MXU_TILE, MXU_TILE), 0) // HEAD_DIM
    c = lax.broadcasted_iota(jnp.int32, (MXU_TILE, MXU_TILE), 1) // HEAD_DIM
    return jnp.where(r == c, 1.0, 0.0).astype(BF16)


def _head_rms(acc, nw):
    ones = _head_group_ones()
    outs = []
    for c in range(acc.shape[1] // MXU_TILE):
        a = acc[:, c * MXU_TILE:(c + 1) * MXU_TILE]
        ssum = jnp.dot((a * a).astype(BF16), ones, preferred_element_type=F32)
        outs.append(a * lax.rsqrt(ssum * (1.0 / HEAD_DIM) + EPS))
    return jnp.concatenate(outs, axis=1) * nw


def _cast_weight_tile(w_ref, wbf):
    @pl.when(pl.program_id(1) == 0)
    def _():
        wbf[...] = w_ref[...].astype(BF16)


def _proj_s5_kernel(h_ref, w_ref, u_ref, zs_ref, wbf):
    j = pl.program_id(0)
    _cast_weight_tile(w_ref, wbf)

    def column_blocks():
        for c in range(PROJ_TN // MXU_TILE):
            cols = slice(c * MXU_TILE, (c + 1) * MXU_TILE)
            yield cols, jnp.dot(h_ref[...], wbf[:, cols], preferred_element_type=F32)

    @pl.when(j == 0)
    def _():
        pad = jnp.zeros((U_PITCH - CHUNK, MXU_TILE), F32)
        for cols, acc in column_blocks():
            for seg in range(PROJ_SEGS):
                for ch in range(CH_PER_SEG):
                    r0 = (seg * CH_PER_SEG + ch) * CHUNK
                    u_ref[0, ch, seg, 0:CHUNK, cols] = acc[r0:r0 + CHUNK, :]
                    u_ref[0, ch, seg, CHUNK:U_PITCH, cols] = pad

    @pl.when(j == 1)
    def _():
        for cols, acc in column_blocks():
            zs_ref[:, cols] = acc.astype(BF16)


def _proj_qkv_kernel(h_ref, w_ref, nw_ref, o_ref, kf_ref, vf_ref, wbf):
    j = pl.program_id(0)
    is_ctx = pl.program_id(1) < PROJ_CTX_STEPS
    _cast_weight_tile(w_ref, wbf)

    def tile(normed, scale, f32_ref):
        width = 2 * MXU_TILE if normed else MXU_TILE
        for c in range(PROJ_TN // width):
            cols = slice(c * width, (c + 1) * width)
            a = jnp.dot(h_ref[...], wbf[:, cols], preferred_element_type=F32)
            if normed:
                a = _head_rms(a, nw_ref[0][:, cols])
            if scale is not None:
                a = a * scale
            o_ref[:, cols] = a.astype(BF16)
            if f32_ref is not None:
                f32_ref[:, cols] = a

    pl.when(j == 0)(lambda: tile(True, HEAD_DIM ** -0.5, None))
    pl.when((j == 1) & is_ctx)(lambda: tile(True, None, kf_ref))
    pl.when((j == 1) & jnp.logical_not(is_ctx))(lambda: tile(True, None, None))
    pl.when((j == 2) & is_ctx)(lambda: tile(False, None, vf_ref))
    pl.when((j == 2) & jnp.logical_not(is_ctx))(lambda: tile(False, None, None))


def _proj_gate_kernel(h_ref, w_ref, o_ref, wbf):
    _cast_weight_tile(w_ref, wbf)
    o_ref[...] = jnp.dot(h_ref[...], wbf[...], preferred_element_type=F32).astype(BF16)


def _in_proj(h, w_in, q_norm_w, k_norm_w):
    ni = N_TOK // PROJ_TM
    last = ni - 1
    ctx_last = PROJ_CTX_STEPS - 1
    tiles_per_path = N_SEG // PROJ_SEGS
    h_spec = pl.BlockSpec((PROJ_TM, D_MODEL), lambda j, i: (i, 0))
    scratch = [pltpu.VMEM((D_MODEL, PROJ_TN), BF16)]
    params = _params("arbitrary", "arbitrary")

    def u_map(j, i):
        ii = jnp.where(j == 0, i, last)
        return (ii // tiles_per_path, 0, ii % tiles_per_path, 0, 0)

    u, zs = pl.pallas_call(
        _proj_s5_kernel,
        grid=(2, ni),
        in_specs=[h_spec, pl.BlockSpec((D_MODEL, PROJ_TN), lambda j, i: (0, OFF_U // PROJ_TN + j))],
        out_specs=[pl.BlockSpec((1, CH_PER_SEG, PROJ_SEGS, U_PITCH, PROJ_TN), u_map),
                   pl.BlockSpec((PROJ_TM, PROJ_TN), lambda j, i: (jnp.where(j == 0, 0, i), 0))],
        out_shape=[jax.ShapeDtypeStruct((2, CH_PER_SEG, N_SEG, U_PITCH, SSM_WIDTH), F32),
                   jax.ShapeDtypeStruct((N_TOK, SSM_WIDTH), BF16)],
        scratch_shapes=scratch, compiler_params=params, name="in_proj_s5",
    )(h, w_in)

    nw = jnp.stack([jnp.tile(q_norm_w.reshape(1, HEAD_DIM), (1, PROJ_TN // HEAD_DIM)),
                    jnp.tile(k_norm_w.reshape(1, HEAD_DIM), (1, PROJ_TN // HEAD_DIM))])
    qkv, kf32, vf32 = pl.pallas_call(
        _proj_qkv_kernel,
        grid=(3, ni),
        in_specs=[h_spec, pl.BlockSpec((D_MODEL, PROJ_TN), lambda j, i: (0, OFF_Q // PROJ_TN + j)),
                  pl.BlockSpec((1, 1, PROJ_TN), lambda j, i: (jnp.where(j == 1, 1, 0), 0, 0))],
        out_specs=[pl.BlockSpec((PROJ_TM, PROJ_TN), lambda j, i: (i, j)),
                   pl.BlockSpec((PROJ_TM, PROJ_TN),
                                lambda j, i: (jnp.where(j == 0, 0, jnp.where(j == 1, jnp.minimum(i, ctx_last),
                                                                             ctx_last)), 0)),
                   pl.BlockSpec((PROJ_TM, PROJ_TN),
                                lambda j, i: (jnp.where(j == 2, jnp.minimum(i, ctx_last), 0), 0))],
        out_shape=[jax.ShapeDtypeStruct((N_TOK, 3 * ATT_WIDTH), BF16),
                   jax.ShapeDtypeStruct((N_CTX_TOK, ATT_WIDTH), F32),
                   jax.ShapeDtypeStruct((N_CTX_TOK, ATT_WIDTH), F32)],
        scratch_shapes=scratch, name="in_proj_qkv",
        compiler_params=_params("arbitrary", "arbitrary", vmem_limit=QKV_VMEM_LIMIT),
    )(h, w_in, nw)

    gate_tiles = 2 * D_MODEL // PROJ_TN
    gz = pl.pallas_call(
        _proj_gate_kernel,
        grid=(gate_tiles + 1, ni),
        in_specs=[h_spec,
                  pl.BlockSpec((D_MODEL, PROJ_TN),
                               lambda j, i: (0, jnp.where(j < gate_tiles, OFF_GS // PROJ_TN + j,
                                                          OFF_ZA // PROJ_TN)))],
        out_specs=pl.BlockSpec((PROJ_TM, PROJ_TN), lambda j, i: (i, j)),
        out_shape=jax.ShapeDtypeStruct((N_TOK, 2 * D_MODEL + ATT_WIDTH), BF16),
        scratch_shapes=scratch, compiler_params=params, name="in_proj_gates",
    )(h, w_in)
    return u, zs, qkv, kf32, vf32, gz


def _pow_select(e, pows):
    rr = jnp.where((e & 1) != 0, pows[0][0], 1.0)
    ri = jnp.where((e & 1) != 0, pows[0][1], 0.0)
    for b in range(1, len(pows)):
        bit = (e & (1 << b)) != 0
        fr = jnp.where(bit, pows[b][0], 1.0)
        fi = jnp.where(bit, pows[b][1], 0.0)
        rr, ri = _cmul(rr, ri, fr, fi)
    return rr, ri


def _discretize(a_re, a_im, log_dt):
    lr = jnp.minimum(a_re, -1e-4)
    li = a_im
    dt = jnp.exp(log_dt)
    mag = jnp.exp(lr * dt)
    br = mag * jnp.cos(li * dt)
    bi = mag * jnp.sin(li * dt)
    den = lr * lr + li * li
    nr = br - 1.0
    cr = (nr * lr + bi * li) / den
    ci = (bi * lr - nr * li) / den
    return (br, bi), (cr, ci)


def _squarings(pr, pi, n):
    out = [(pr, pi)]
    for _ in range(n):
        pr, pi = _cmul(pr, pi, pr, pi)
        out.append((pr, pi))
    return out


def _dot_nt_exact(a, b):
    return lax.dot_general(a, b, (((1,), (1,)), ((), ())), preferred_element_type=F32, precision=HIGHEST)


def _ssm_build_ops(n_groups, rowp_ref, btr_ref, bti_ref, ctr_ref, cti_ref, t_ref, s_ref, rt_ref, lam_ref, powc_ref):
    lane_b = lax.broadcasted_iota(jnp.int32, (SSM_GROUP, DSTATE), 1)
    lane_t = lax.broadcasted_iota(jnp.int32, (SSM_GROUP, FLAT), 1)
    row_c = lax.broadcasted_iota(jnp.int32, (CH_PER_SEG, DSTATE), 0)
    lane_c = lax.broadcasted_iota(jnp.int32, (CH_PER_SEG, DSTATE), 1)
    exp_c = jnp.where(lane_c < SSM_STATE, row_c, (CH_PER_SEG - 1) - row_c)

    for g in range(n_groups):
        rp = rowp_ref[g]
        (lbr, lbi), (cfr, cfi) = _discretize(rp[0:1], rp[1:2], rp[2:3])
        sq = _squarings(lbr, lbi, 8)
        bbr, bbi = _cmul(btr_ref[g], bti_ref[g], cfr, cfi)

        ptr, pti = _pow_select(exp_c, sq[:4])

        s_re, s_im = [], []
        for sp in range(CHUNK):
            e = CHUNK - 1 - sp
            br, bi = _cmul(bbr, bbi, ptr[e:e + 1], pti[e:e + 1])
            s_re.append(br)
            s_im.append(bi)
        s_ref[g] = jnp.concatenate([jnp.concatenate(s_re, axis=0), jnp.concatenate(s_im, axis=0)],
                                   axis=1).astype(BF16)

        xcr = jnp.concatenate([ctr_ref[g]] * CHUNK, axis=0)
        xci = jnp.concatenate([cti_ref[g]] * CHUNK, axis=0)
        xpr = jnp.broadcast_to(ptr[:, None, :], (CHUNK, SSM_GROUP, DSTATE)).reshape(FLAT, DSTATE)
        xpi = jnp.broadcast_to(pti[:, None, :], (CHUNK, SSM_GROUP, DSTATE)).reshape(FLAT, DSTATE)
        ykr, yki = _cmul(xcr, xci, xpr, xpi)
        yrr, yri = _cmul(ykr, yki, lbr, lbi)
        rt_ref[g] = jnp.concatenate([yrr, -yri], axis=1).astype(BF16)

        fwd = lane_b < SSM_STATE
        lhs = jnp.concatenate([jnp.where(fwd, bbr, 0.0), jnp.where(fwd, bbi, 0.0),
                               jnp.where(fwd, 0.0, bbr), jnp.where(fwd, 0.0, bbi)], axis=0)
        p1 = _dot_nt_exact(lhs, ykr)
        p2 = _dot_nt_exact(lhs, yki)
        kf = p1[0:16] - p2[16:32]
        kb = p1[32:48] - p2[48:64]
        for sp in range(CHUNK):
            tf = kf if sp == 0 else pltpu.roll(kf, SSM_GROUP * sp, axis=1)
            tf = jnp.where(lane_t >= SSM_GROUP * sp, tf, 0.0)
            shift = (FLAT - SSM_GROUP * (CHUNK - 1 - sp)) % FLAT
            tb = kb if shift == 0 else pltpu.roll(kb, shift, axis=1)
            tb = jnp.where(lane_t < SSM_GROUP * (sp + 1), tb, 0.0)
            t_ref[g, sp * SSM_GROUP:(sp + 1) * SSM_GROUP, :] = (tf + tb).astype(BF16)

        l16 = sq[4]
        l256 = sq[8]
        lam_ref[g] = jnp.concatenate([l16[0], l16[1], l256[0], l256[1],
                                      jnp.zeros((4, DSTATE), F32)], axis=0)
        q16 = _squarings(l16[0], l16[1], 3)
        pcr, pci = _pow_select(exp_c, q16)
        powc_ref[g, 0] = pcr
        powc_ref[g, 1] = pci


SSM_PATHS = ((BATCH, SEQ // (CHUNK * CH_PER_SEG)), (DEC_BATCH, DEC_SEQ // (CHUNK * CH_PER_SEG)))


def _ssm_path(u, s_op, t_op, rt_op, lam, powc_ref, h0r, h0i, nseg):
    lane = lax.broadcasted_iota(jnp.int32, (N_SEG, DSTATE), 1)
    isf = lane < SSM_STATE
    l16r, l16i, l256r, l256i = lam[0:1], lam[1:2], lam[2:3], lam[3:4]

    z = jnp.dot(u, s_op, preferred_element_type=F32)
    zre, zim = z[:, :DSTATE], z[:, DSTATE:]

    hr = jnp.zeros((N_SEG, DSTATE), F32)
    hi = jnp.zeros((N_SEG, DSTATE), F32)
    hist = []
    for k in range(CH_PER_SEG):
        hist.append((hr, hi))
        kb = CH_PER_SEG - 1 - k
        zr = jnp.where(isf, zre[k * N_SEG:(k + 1) * N_SEG], zre[kb * N_SEG:(kb + 1) * N_SEG])
        zi = jnp.where(isf, zim[k * N_SEG:(k + 1) * N_SEG], zim[kb * N_SEG:(kb + 1) * N_SEG])
        nr, ni = _cmul(l16r, l16i, hr, hi)
        hr, hi = nr + zr, ni + zi

    zero_start = h0r is None
    if nseg > 1 and not zero_start:
        h0r = jnp.broadcast_to(h0r[:, None, :], (N_SEG // nseg, nseg, DSTATE)).reshape(N_SEG, DSTATE)
        h0i = jnp.broadcast_to(h0i[:, None, :], (N_SEG // nseg, nseg, DSTATE)).reshape(N_SEG, DSTATE)
    seg = lax.broadcasted_iota(jnp.int32, (N_SEG, DSTATE), 0) & (nseg - 1)
    segpow = _squarings(l256r, l256i, max(int(math.log2(nseg)), 0))

    def shifted(x, d):
        dn = jnp.where(seg >= d, pltpu.roll(x, d, axis=0), 0.0)
        up = jnp.where(seg <= nseg - 1 - d, pltpu.roll(x, N_SEG - d, axis=0), 0.0)
        return jnp.where(isf, dn, up)

    pr, pi = hr, hi
    d = 1
    lvl = 0
    while d < nseg:
        ar, ai = _cmul(segpow[lvl][0], segpow[lvl][1], shifted(pr, d), shifted(pi, d))
        pr, pi = pr + ar, pi + ai
        d *= 2
        lvl += 1
    hsr = hsi = None
    if nseg > 1:
        hsr, hsi = shifted(pr, 1), shifted(pi, 1)
        if not zero_start:
            e_in = jnp.where(isf, seg, nseg - 1 - seg)
            wr, wi = _pow_select(e_in, segpow[:lvl])
            ar, ai = _cmul(wr, wi, h0r, h0i)
            hsr, hsi = hsr + ar, hsi + ai
    elif not zero_start:
        hsr, hsi = h0r, h0i
    if zero_start:
        fin = jnp.concatenate([pr, pi], axis=1)
    else:
        er, ei = _cmul(segpow[lvl][0], segpow[lvl][1], h0r, h0i)
        fin = jnp.concatenate([pr + er, pi + ei], axis=1)

    rows = []
    for c in range(CH_PER_SEG):
        cb = CH_PER_SEG - 1 - c
        lr = jnp.where(isf, hist[c][0], hist[cb][0])
        li = jnp.where(isf, hist[c][1], hist[cb][1])
        if hsr is not None:
            ar, ai = _cmul(powc_ref[0, c:c + 1, :], powc_ref[1, c:c + 1, :], hsr, hsi)
            lr, li = lr + ar, li + ai
        rows.append(jnp.concatenate([lr, li], axis=1))
    hent = jnp.concatenate(rows, axis=0).astype(BF16)

    y = jnp.dot(u, t_op, preferred_element_type=F32) + _dot_nt(hent, rt_op)
    return y, fin


GROUPS_PER_STEP = LANES // SSM_GROUP


def _block_transpose(arrs, blk):
    n = len(arrs)
    width = arrs[0].shape[1]
    j = lax.broadcasted_iota(jnp.int32, arrs[0].shape, 1) // blk
    k = n // 2
    while k >= 1:
        bit = (j & k) != 0
        new = list(arrs)
        for x in range(n):
            if x & k == 0:
                a, b = arrs[x], arrs[x | k]
                new[x] = jnp.where(bit, pltpu.roll(b, k * blk, axis=1), a)
                new[x | k] = jnp.where(bit, b, pltpu.roll(a, width - k * blk, axis=1))
        arrs = new
        k //= 2
    return arrs


def _ssm_kernel(u_ref, rowp_ref, btr_ref, bti_ref, ctr_ref, cti_ref, d_ref, h0cr_ref, h0ci_ref, h0dr_ref, h0di_ref,
                y_ref, fin_ref, ubuf, ybuf, t_ref, s_ref, r_ref, lam_ref, powc_ref):
    _ssm_build_ops(GROUPS_PER_STEP, rowp_ref, btr_ref, bti_ref, ctr_ref, cti_ref,
                   t_ref, s_ref, r_ref, lam_ref, powc_ref)
    h0 = ((h0cr_ref, h0ci_ref), (h0dr_ref, h0di_ref))
    nhalf = FLAT // LANES
    rows = N_SEG * CH_PER_SEG
    def token_rows(path, tok, pitch):
        return pl.ds(path * rows * pitch + tok, rows, stride=pitch)

    for path, (_, nseg) in enumerate(SSM_PATHS):
        for half in range(nhalf):
            toks = [pltpu.bitcast(u_ref[token_rows(path, half * GROUPS_PER_STEP + sb, U_PITCH), :].astype(BF16),
                                  jnp.uint32) for sb in range(GROUPS_PER_STEP)]
            grouped = _block_transpose(toks, SSM_GROUP)
            for g in range(GROUPS_PER_STEP):
                ubuf[g, :, half * LANES:(half + 1) * LANES] = pltpu.bitcast(grouped[g], BF16)

        for g in range(GROUPS_PER_STEP):
            h0r, h0i = (None, None) if h0[path] is None else (h0[path][0][g], h0[path][1][g])
            y, fin = _ssm_path(ubuf[g], s_ref[g], t_ref[g], r_ref[g], lam_ref[g], powc_ref.at[g], h0r, h0i, nseg)
            ybuf[g] = y
            fin_ref[path, g] = fin

        for half in range(nhalf):
            grouped = [ybuf[g, :, half * LANES:(half + 1) * LANES] for g in range(GROUPS_PER_STEP)]
            toks = _block_transpose(grouped, SSM_GROUP)
            for sb in range(GROUPS_PER_STEP):
                tok = half * GROUPS_PER_STEP + sb
                y_ref[token_rows(path, tok, CHUNK), :] = (toks[sb]
                                                         + u_ref[token_rows(path, tok, U_PITCH), :] * d_ref[...])


def _ssm_scan(u2d, rowp, btr, bti, ctr, cti, dtile, h0c_re, h0c_im, h0d_re, h0d_im):
    G = SSM_GROUPS
    gs = GROUPS_PER_STEP
    rows = N_SEG * CH_PER_SEG
    return pl.pallas_call(
        _ssm_kernel,
        grid=(G // gs,),
        in_specs=[
            pl.BlockSpec((N_TOK // CHUNK * U_PITCH, LANES), lambda o: (0, o)),
            pl.BlockSpec((gs, 3, DSTATE), lambda o: (o, 0, 0)),
            pl.BlockSpec((gs, SSM_GROUP, DSTATE), lambda o: (o, 0, 0)),
            pl.BlockSpec((gs, SSM_GROUP, DSTATE), lambda o: (o, 0, 0)),
            pl.BlockSpec((gs, SSM_GROUP, DSTATE), lambda o: (o, 0, 0)),
            pl.BlockSpec((gs, SSM_GROUP, DSTATE), lambda o: (o, 0, 0)),
            pl.BlockSpec((1, LANES), lambda o: (0, o)),
            pl.BlockSpec((gs, BATCH, DSTATE), lambda o: (o, 0, 0)),
            pl.BlockSpec((gs, BATCH, DSTATE), lambda o: (o, 0, 0)),
            pl.BlockSpec((gs, DEC_BATCH, DSTATE), lambda o: (o, 0, 0)),
            pl.BlockSpec((gs, DEC_BATCH, DSTATE), lambda o: (o, 0, 0))],
        out_specs=[pl.BlockSpec((N_TOK, LANES), lambda o: (0, o)),
                   pl.BlockSpec((2, gs, N_SEG, 2 * DSTATE), lambda o: (0, o, 0, 0))],
        out_shape=[jax.ShapeDtypeStruct((N_TOK, SSM_WIDTH), F32),
                   jax.ShapeDtypeStruct((2, G, N_SEG, 2 * DSTATE), F32)],
        scratch_shapes=[pltpu.VMEM((gs, rows, FLAT), BF16), pltpu.VMEM((gs, rows, FLAT), F32),
                        pltpu.VMEM((gs, FLAT, FLAT), BF16), pltpu.VMEM((gs, FLAT, FLAT), BF16),
                        pltpu.VMEM((gs, FLAT, FLAT), BF16), pltpu.VMEM((gs, 8, DSTATE), F32),
                        pltpu.VMEM((gs, 2, CH_PER_SEG, DSTATE), F32)],
        compiler_params=_params("arbitrary"),
        name="ssm_scan",
    )(u2d, rowp, btr, bti, ctr, cti, dtile, h0c_re, h0c_im, h0d_re, h0d_im)


CTX_BB = 4
HEADS_PER_STEP = MXU_TILE // HEAD_DIM
HEAD_LANES = HEADS_PER_STEP * HEAD_DIM


def _softmax_pv(scores, values):
    m = scores[0].max(axis=-1, keepdims=True)
    for s in scores[1:]:
        m = jnp.maximum(m, s.max(axis=-1, keepdims=True))
    l = None
    o = None
    for s, v in zip(scores, values):
        p = jnp.exp(s - m)
        ls = p.sum(axis=-1, keepdims=True)
        os_ = jnp.dot(p.astype(BF16), v, preferred_element_type=F32)
        l = ls if l is None else l + ls
        o = os_ if o is None else o + os_
    return o / l


def _head_of_lane():
    return lax.broadcasted_iota(jnp.int32, (1, HEAD_LANES), 1) // HEAD_DIM


def _merge_heads(outs, za):
    head = _head_of_lane()
    o = outs[-1]
    for hh in range(HEADS_PER_STEP - 2, -1, -1):
        o = jnp.where(head == hh, outs[hh], o)
    return (o * _silu(za.astype(F32))).astype(BF16)


def _ctx_attn_kernel(q_ref, k_ref, v_ref, za_ref, o_ref):
    head = _head_of_lane()
    for b in range(CTX_BB):
        sl = slice(b * SEQ, (b + 1) * SEQ)
        q, k, v = q_ref[sl, :], k_ref[sl, :], v_ref[sl, :]
        outs = []
        for hh in range(HEADS_PER_STEP):
            qh = jnp.where(head == hh, q, jnp.zeros_like(q))
            outs.append(_softmax_pv([_dot_nt(qh, k)], [v]))
        o_ref[sl, :] = _merge_heads(outs, za_ref[sl, :])


HEAD_BLOCKS = ATT_WIDTH // HEAD_LANES
QKV_Q, QKV_K, QKV_V = 0, HEAD_BLOCKS, 2 * HEAD_BLOCKS
GZ_ZA = 2 * D_MODEL // HEAD_LANES


def _ctx_attention(qkv, gz):
    rows = CTX_BB * SEQ

    def spec(first):
        return pl.BlockSpec((rows, HEAD_LANES), lambda b, hq: (b, first + hq))

    return pl.pallas_call(
        _ctx_attn_kernel,
        grid=(BATCH // CTX_BB, N_HEADS // HEADS_PER_STEP),
        in_specs=[spec(QKV_Q), spec(QKV_K), spec(QKV_V), spec(GZ_ZA)],
        out_specs=spec(0),
        out_shape=jax.ShapeDtypeStruct((N_CTX_TOK, ATT_WIDTH), BF16),
        compiler_params=_params("arbitrary", "arbitrary"),
        name="ctx_attention",
    )(qkv, qkv, qkv, gz)


N_DROW = 2 * WIN_ROWS - 1


def _nbr_build_table(band_ref, tbl):
    lane = lax.broadcasted_iota(jnp.int32, (GRID_W, 2 * GRID_W), 1)
    qc = lax.broadcasted_iota(jnp.int32, (GRID_W, 2 * GRID_W), 0)
    kc = lane & (GRID_W - 1)
    cs = jnp.clip(qc - WIN_COLS // 2, 0, GRID_W - WIN_COLS)
    valid = jnp.logical_and(kc >= cs, kc < cs + WIN_COLS)
    for hh in range(HEADS_PER_STEP):
        for dr in range(N_DROW):
            x = jnp.broadcast_to(band_ref[hh, dr:dr + 1, :], (GRID_W, 2 * GRID_W))
            lo = pltpu.roll(x, 0, axis=1, stride=1, stride_axis=0)
            hi = pltpu.roll(x, GRID_W, axis=1, stride=1, stride_axis=0)
            tbl[hh, dr] = jnp.where(valid, jnp.where(lane < GRID_W, lo, hi), NEG_INF)
        tbl[hh, N_DROW] = jnp.full((GRID_W, 2 * GRID_W), NEG_INF, F32)


NBR_KEYS = NBR_K + PAST_LEN
KEY_TILE = MXU_TILE
SOFTMAX_ROWS = 64


def _nbr_block_index(a):
    ks = jnp.clip(Q_ROWS_PER_BLOCK * a - Q_ROWS_PER_BLOCK, 0, GRID_H - NBR_KROWS)
    idx = []
    for ri in range(Q_ROWS_PER_BLOCK):
        r = Q_ROWS_PER_BLOCK * a + ri
        rs = jnp.clip(r - WIN_ROWS // 2, 0, GRID_H - WIN_ROWS)
        row = []
        for kri in range(NBR_KROWS):
            kr = ks + kri
            valid = jnp.logical_and(kr >= rs, kr < rs + WIN_ROWS)
            row.append(jnp.where(valid, kr - r + WIN_ROWS - 1, N_DROW))
        idx.append(row)
    return idx


def _nbr_bias_tile(tbl, hh, idx, t):
    half0 = lax.broadcasted_iota(jnp.int32, (1, 2 * GRID_W), 1) < GRID_W
    krows = KEY_TILE // GRID_W
    rows = []
    for ri in range(Q_ROWS_PER_BLOCK):
        tiles = [jnp.where(half0, tbl[hh, idx[ri][kri]], tbl[hh, idx[ri][kri + 1]])
                 for kri in range(krows * t, krows * (t + 1), 2)]
        rows.append(jnp.concatenate(tiles, axis=1))
    return jnp.concatenate(rows, axis=0)


def _nbr_attn_kernel(q_ref, k_ref, v_ref, kc_ref, vc_ref, band_ref, za_ref, o_ref,
                     tbl, kcat, vcat, s_scr, p_scr, o_scr):
    a = pl.program_id(1)

    @pl.when(a == 0)
    def _():
        _nbr_build_table(band_ref, tbl)

    start = pl.multiple_of(jnp.clip(a * NBR_Q - NBR_Q, 0, DEC_SEQ - NBR_K), NBR_Q)
    head = _head_of_lane()
    idx = _nbr_block_index(a)
    for b in range(DEC_BATCH):
        kcat[b, 0:NBR_K, :] = k_ref[b, pl.ds(start, NBR_K), :]
        kcat[b, NBR_K:NBR_KEYS, :] = kc_ref[b].astype(BF16)
        vcat[b, 0:NBR_K, :] = v_ref[b, pl.ds(start, NBR_K), :]
        vcat[b, NBR_K:NBR_KEYS, :] = vc_ref[b].astype(BF16)

    def scores(b, hh):
        q = q_ref[b]
        qh = jnp.where(head == hh, q, jnp.zeros_like(q))
        mrun = None
        for t in range(NBR_KEYS // KEY_TILE):
            st = _dot_nt(qh, kcat[b, t * KEY_TILE:(t + 1) * KEY_TILE, :])
            if t < NBR_K // KEY_TILE:
                st = st + _nbr_bias_tile(tbl, hh, idx, t)
            s_scr[hh, :, t * KEY_TILE:(t + 1) * KEY_TILE] = st
            mt = jnp.maximum(st[:, :LANES], st[:, LANES:])
            mrun = mt if mrun is None else jnp.maximum(mrun, mt)
        return mrun.max(axis=-1, keepdims=True)

    def attend(b, hh, m):
        ls = []
        for r0 in range(0, NBR_Q, SOFTMAX_ROWS):
            mr = m[r0:r0 + SOFTMAX_ROWS]
            lrun = jnp.zeros((SOFTMAX_ROWS, LANES), F32)
            for c0 in range(0, NBR_KEYS, LANES):
                p = jnp.exp(s_scr[hh, r0:r0 + SOFTMAX_ROWS, c0:c0 + LANES] - mr)
                lrun = lrun + p
                p_scr[hh, r0:r0 + SOFTMAX_ROWS, c0:c0 + LANES] = p.astype(BF16)
            ls.append(lrun.sum(axis=-1, keepdims=True))
        l = jnp.concatenate(ls, axis=0)
        o = jnp.dot(p_scr[hh], vcat[b], preferred_element_type=F32)
        lanes = slice(hh * HEAD_DIM, (hh + 1) * HEAD_DIM)
        o_scr[b, :, lanes] = o[:, lanes] / l

    order = [(b, hh) for b in range(DEC_BATCH) for hh in range(HEADS_PER_STEP)]
    m_cur = scores(*order[0])
    for n, (b, hh) in enumerate(order):
        m_next = scores(*order[n + 1]) if n + 1 < len(order) else None
        attend(b, hh, m_cur)
        m_cur = m_next
    for b in range(DEC_BATCH):
        o_ref[b] = (o_scr[b] * _silu(za_ref[b].astype(F32))).astype(BF16)


def _nbr_attention(qkv4, kc, vc, band, gz4):
    first = N_CTX_TOK // DEC_SEQ // DEC_BATCH
    nblk = GRID_H // Q_ROWS_PER_BLOCK
    hw = HEAD_LANES

    def qspec(col0):
        return pl.BlockSpec((DEC_BATCH, NBR_Q, hw), lambda hp, a: (first, a, col0 + hp))

    def kspec(col0):
        return pl.BlockSpec((DEC_BATCH, DEC_SEQ, hw), lambda hp, a: (first, 0, col0 + hp))

    cspec = pl.BlockSpec((DEC_BATCH, PAST_LEN, hw), lambda hp, a: (0, 0, hp))
    bspec = pl.BlockSpec((HEADS_PER_STEP, N_DROW + 1, 2 * GRID_W), lambda hp, a: (hp, 0, 0))
    return pl.pallas_call(
        _nbr_attn_kernel,
        grid=(N_HEADS // HEADS_PER_STEP, nblk),
        in_specs=[qspec(QKV_Q), kspec(QKV_K), kspec(QKV_V), cspec, cspec, bspec, qspec(GZ_ZA)],
        out_specs=pl.BlockSpec((DEC_BATCH, NBR_Q, hw), lambda hp, a: (0, a, hp)),
        out_shape=jax.ShapeDtypeStruct((DEC_BATCH, DEC_SEQ, ATT_WIDTH), BF16),
        scratch_shapes=[pltpu.VMEM((HEADS_PER_STEP, N_DROW + 1, GRID_W, 2 * GRID_W), F32),
                        pltpu.VMEM((DEC_BATCH, NBR_KEYS, HEAD_LANES), BF16),
                        pltpu.VMEM((DEC_BATCH, NBR_KEYS, HEAD_LANES), BF16),
                        pltpu.VMEM((HEADS_PER_STEP, NBR_Q, NBR_KEYS), F32),
                        pltpu.VMEM((HEADS_PER_STEP, NBR_Q, NBR_KEYS), BF16),
                        pltpu.VMEM((DEC_BATCH, NBR_Q, HEAD_LANES), F32)],
        compiler_params=_params("arbitrary", "arbitrary"),
        name="nbr_attention",
    )(qkv4, qkv4, qkv4, kc, vc, band, gz4)


def _nbr_bias_band(rpb):
    ncol = 2 * WIN_COLS - 1
    fill = jnp.full(rpb.shape[:2] + (2 * GRID_W - ncol,), NEG_INF, F32)
    band = jnp.concatenate([rpb[..., WIN_COLS - 1:], fill, rpb[..., :WIN_COLS - 1]], axis=-1)
    return jnp.pad(band, ((0, 0), (0, 1), (0, 0)))


BACK_TM = CHUNK * CH_PER_SEG
BACK_CTX_STEPS = N_CTX_TOK // BACK_TM
BACK_STEPS_PER_DEC_SEQ = DEC_SEQ // BACK_TM


BACK_WARM = 8


def _back_kernel(xp_ref, xs_ref, y_ref, zs_ref, ac_ref, ad_ref, gs_ref, ga_ref, gate_ref, bglu_ref,
                 wglu_ref, wso_ref, wao_ref, wo_ref, op_ref, os_ref, wglu_bf, wso_bf, wao_bf, wo_bf):
    i = pl.program_id(0)
    step = i - BACK_WARM

    @pl.when(i < BACK_WARM)
    def _():
        for src, dst in ((wglu_ref, wglu_bf), (wso_ref, wso_bf), (wao_ref, wao_bf), (wo_ref, wo_bf)):
            slab = src.shape[0]
            dst[pl.ds(pl.multiple_of(i * slab, slab), slab), :] = src[...].astype(BF16)

    def compute(x, a2):
        ys = _gelu_tanh(y_ref[0].reshape(BACK_TM, SSM_WIDTH))
        t = jnp.dot(ys.astype(BF16), wglu_bf[...], preferred_element_type=F32) + bglu_ref[...]
        ys = ys * _sigmoid(t) * _silu(zs_ref[...].astype(F32))
        p_s = jnp.dot(ys.astype(BF16), wso_bf[...], preferred_element_type=F32)
        p_a = jnp.dot(a2, wao_bf[...], preferred_element_type=F32)
        merged = (_sigmoid(gs_ref[...].astype(F32)) * p_s + _sigmoid(ga_ref[...].astype(F32)) * p_a)
        return x + gate_ref[0] * jnp.dot(merged.astype(BF16), wo_bf[...], preferred_element_type=F32)

    @pl.when(jnp.logical_and(step >= 0, step < BACK_CTX_STEPS))
    def _():
        op_ref[...] = compute(xp_ref[...], ac_ref[...])

    @pl.when(step >= BACK_CTX_STEPS)
    def _():
        os_ref[...] = compute(xs_ref[...], ad_ref[...])


def _back(xp, xs, y, zs, a_ctx, a_dec, gz, gate3, b_glu, w_glu, w_so, w_ao, w_o):
    steps = N_TOK // BACK_TM
    n0 = BACK_CTX_STEPS
    warm = BACK_WARM
    row = functools.partial(_mod_row, ctx_steps=n0, steps_per_seq=BACK_STEPS_PER_DEC_SEQ)

    def tile(i):
        return jnp.maximum(i - warm, 0)

    lo = lambda i: (jnp.minimum(tile(i), n0 - 1), 0)
    hi = lambda i: (jnp.maximum(tile(i) - n0, 0), 0)
    cur = lambda i: (tile(i), 0)
    const = lambda i: (0, 0)
    slab = lambda i: (jnp.minimum(i, warm - 1), 0)

    def weight(w):
        return pl.BlockSpec((w.shape[0] // warm, w.shape[1]), slab)

    return pl.pallas_call(
        _back_kernel,
        grid=(warm + steps,),
        in_specs=[pl.BlockSpec((BACK_TM, D_MODEL), lo),
                  pl.BlockSpec((BACK_TM, D_MODEL), hi),
                  pl.BlockSpec((1, CH_PER_SEG, None, CHUNK, SSM_WIDTH),
                               lambda i: (tile(i) // N_SEG, 0, tile(i) % N_SEG, 0, 0)),
                  pl.BlockSpec((BACK_TM, SSM_WIDTH), cur),
                  pl.BlockSpec((BACK_TM, ATT_WIDTH), lo),
                  pl.BlockSpec((BACK_TM, ATT_WIDTH), hi),
                  pl.BlockSpec((BACK_TM, D_MODEL), lambda i: (tile(i), 0)),
                  pl.BlockSpec((BACK_TM, D_MODEL), lambda i: (tile(i), 1)),
                  pl.BlockSpec((1, 1, D_MODEL), lambda i: (row(tile(i)), 0, 0)),
                  pl.BlockSpec((1, SSM_WIDTH), const),
                  weight(w_glu), weight(w_so), weight(w_ao), weight(w_o)],
        out_specs=[pl.BlockSpec((BACK_TM, D_MODEL), lo),
                   pl.BlockSpec((BACK_TM, D_MODEL), hi)],
        out_shape=[jax.ShapeDtypeStruct((N_CTX_TOK, D_MODEL), F32),
                   jax.ShapeDtypeStruct((N_DEC_TOK, D_MODEL), F32)],
        scratch_shapes=[pltpu.VMEM(w.shape, BF16) for w in (w_glu, w_so, w_ao, w_o)],
        compiler_params=_params("arbitrary"),
        name="gated_output",
    )(xp, xs, y, zs, a_ctx, a_dec, gz, gz, gate3, b_glu.reshape(1, SSM_WIDTH), w_glu, w_so, w_ao, w_o)


def _layer(xp, xs, cache_k, cache_v, st_re, st_im, c, c_ctx, norm_w, w_ada, b_ada, w_in, q_norm_w, k_norm_w,
           rpb, a_re, a_im, log_dt, b_re, b_im, c_re, c_im, d, w_glu, b_glu, w_so, w_ao, w_o):
    G, P = SSM_GROUPS, SSM_STATE

    cond8 = jnp.zeros((8, D_MODEL), F32).at[0].set(c_ctx).at[1:1 + DEC_BATCH].set(c)
    mod = _modulation(cond8, w_ada, b_ada)
    shift3 = mod[:1 + DEC_BATCH, None, :D_MODEL]
    scale3 = mod[:1 + DEC_BATCH, None, D_MODEL:2 * D_MODEL]
    gate3 = mod[:1 + DEC_BATCH, None, 2 * D_MODEL:]

    h = _norm_modulate(xp, xs, shift3, scale3, norm_w)
    u, zs, qkv, kf32, vf32, gz = _in_proj(h, w_in, q_norm_w, k_norm_w)

    arow = jnp.transpose(a_re, (1, 0, 2)).reshape(G, 2 * P)
    airow = jnp.transpose(a_im, (1, 0, 2)).reshape(G, 2 * P)
    dtrow = jnp.broadcast_to(jnp.transpose(log_dt, (1, 0))[:, :, None], (G, 2, P)).reshape(G, 2 * P)
    rowp = jnp.stack([arow, airow, dtrow], axis=1)
    btr = jnp.transpose(b_re, (1, 3, 0, 2)).reshape(G, SSM_GROUP, 2 * P)
    bti = jnp.transpose(b_im, (1, 3, 0, 2)).reshape(G, SSM_GROUP, 2 * P)
    ctr = jnp.transpose(c_re, (1, 2, 0, 3)).reshape(G, SSM_GROUP, 2 * P)
    cti = jnp.transpose(c_im, (1, 2, 0, 3)).reshape(G, SSM_GROUP, 2 * P)

    dtile = d.reshape(1, SSM_WIDTH)
    h0c = jnp.zeros((G, BATCH, 2 * P), F32)
    h0d_re = jnp.transpose(st_re, (2, 0, 1, 3)).reshape(G, DEC_BATCH, 2 * P)
    h0d_im = jnp.transpose(st_im, (2, 0, 1, 3)).reshape(G, DEC_BATCH, 2 * P)
    y2d, fin = _ssm_scan(u.reshape(N_TOK // CHUNK * U_PITCH, SSM_WIDTH), rowp, btr, bti, ctr, cti, dtile,
                         h0c, h0c, h0d_re, h0d_im)
    y = y2d.reshape(2, CH_PER_SEG, N_SEG, CHUNK, SSM_WIDTH)

    a_ctx = _ctx_attention(qkv, gz)
    nseq4 = N_TOK // DEC_SEQ
    a_dec = _nbr_attention(qkv.reshape(nseq4, DEC_SEQ, qkv.shape[1]),
                           cache_k.reshape(DEC_BATCH, PAST_LEN, ATT_WIDTH),
                           cache_v.reshape(DEC_BATCH, PAST_LEN, ATT_WIDTH),
                           _nbr_bias_band(rpb.astype(F32)),
                           gz.reshape(nseq4, DEC_SEQ, gz.shape[1]))

    yp, ys_out = _back(xp, xs, y, zs, a_ctx, a_dec.reshape(N_DEC_TOK, ATT_WIDTH), gz, gate3, b_glu,
                       w_glu, w_so, w_ao, w_o)

    fin_ctx = fin[0]
    new_re = jnp.transpose(fin_ctx[:, :, :2 * P].reshape(G, BATCH, 2, P), (1, 2, 0, 3))
    new_im = jnp.transpose(fin_ctx[:, :, 2 * P:].reshape(G, BATCH, 2, P), (1, 2, 0, 3))
    return yp, ys_out, kf32, vf32, new_re, new_im


def kernel(x_prompt, x_sample, cache_k, cache_v, state_ssm_re, state_ssm_im, c, c_ctx, norm_w, w_ada, b_ada,
           w_in, q_norm_w, k_norm_w, rel_pos_bias, ssm_a_re, ssm_a_im, ssm_log_dt, ssm_b_re, ssm_b_im,
           ssm_c_re, ssm_c_im, ssm_d, w_glu, b_glu, w_ssm_out, w_att_out, w_o):
    depth = norm_w.shape[0]
    xp = x_prompt.reshape(N_CTX_TOK, D_MODEL)
    xs = x_sample.reshape(N_DEC_TOK, D_MODEL)
    new_k, new_v, new_re, new_im = [], [], [], []
    for l in range(depth):
        xp, xs, kl, vl, rl, il = _layer(
            xp, xs, cache_k[:, l], cache_v[:, l], state_ssm_re[:, l], state_ssm_im[:, l], c, c_ctx,
            norm_w[l], w_ada[l], b_ada[l], w_in[l], q_norm_w[l], k_norm_w[l], rel_pos_bias[l],
            ssm_a_re[l], ssm_a_im[l], ssm_log_dt[l], ssm_b_re[l], ssm_b_im[l], ssm_c_re[l], ssm_c_im[l],
            ssm_d[l], w_glu[l], b_glu[l], w_ssm_out[l], w_att_out[l], w_o[l])
        new_k.append(kl.reshape(BATCH, SEQ, N_HEADS, HEAD_DIM))
        new_v.append(vl.reshape(BATCH, SEQ, N_HEADS, HEAD_DIM))
        new_re.append(rl)
        new_im.append(il)
    return (xp.reshape(BATCH, SEQ, D_MODEL), xs.reshape(DEC_BATCH, DEC_SEQ, D_MODEL),
            jnp.stack(new_k, axis=1), jnp.stack(new_v, axis=1),
            jnp.stack(new_re, axis=1), jnp.stack(new_im, axis=1))
```

```python
import functools
import math

import jax
import jax.numpy as jnp
from jax import lax
from jax.experimental import pallas as pl
from jax.experimental.pallas import tpu as pltpu

D_MODEL = 2048
BATCH = 16
SEQ = 256
DEC_BATCH = 2
DEC_SEQ = 2048
PAST_LEN = 512
GRID_W = 64
GRID_H = DEC_SEQ // GRID_W
SSM_WIDTH = D_MODEL // 2
SSM_GROUP = 16
SSM_GROUPS = SSM_WIDTH // SSM_GROUP
SSM_STATE = 64
N_HEADS = 16
HEAD_DIM = 64
ATT_WIDTH = N_HEADS * HEAD_DIM
WIN_ROWS = 8
WIN_COLS = 16
EPS = 1e-6
NEG_INF = -1e30

N_CTX_TOK = BATCH * SEQ
N_DEC_TOK = DEC_BATCH * DEC_SEQ
N_TOK = N_CTX_TOK + N_DEC_TOK

OFF_U = 0
OFF_ZS = SSM_WIDTH
OFF_Q = 2 * SSM_WIDTH
OFF_K = OFF_Q + ATT_WIDTH
OFF_V = OFF_K + ATT_WIDTH
OFF_ZA = OFF_V + ATT_WIDTH
OFF_GS = OFF_ZA + ATT_WIDTH
OFF_GA = OFF_GS + D_MODEL

CHUNK = 16
FLAT = CHUNK * SSM_GROUP
N_SEG = 16
CH_PER_SEG = 16
U_PITCH = 24
DSTATE = 2 * SSM_STATE

Q_ROWS_PER_BLOCK = 4
NBR_Q = Q_ROWS_PER_BLOCK * GRID_W
NBR_KROWS = 12
NBR_K = NBR_KROWS * GRID_W

V7X_VMEM_BYTES = 64 * 1024 * 1024
VMEM_LIMIT = V7X_VMEM_BYTES - 8 * 1024 * 1024
QKV_VMEM_LIMIT = V7X_VMEM_BYTES - 4 * 1024 * 1024

LANES = 128
MXU_TILE = 256

F32 = jnp.float32
BF16 = jnp.bfloat16
HIGHEST = lax.Precision.HIGHEST


def _sigmoid(x):
    return 1.0 / (1.0 + jnp.exp(-x))


def _silu(x):
    return x * _sigmoid(x)


def _gelu_tanh(x):
    return 0.5 * x * (1.0 + jnp.tanh(math.sqrt(2.0 / math.pi) * (x + 0.044715 * (x * x * x))))


def _cmul(ar, ai, br, bi):
    return ar * br - ai * bi, ar * bi + ai * br


def _dot_nt(a, b):
    return lax.dot_general(a, b, (((1,), (1,)), ((), ())), preferred_element_type=F32)


def _params(*sem, vmem_limit=VMEM_LIMIT):
    return pltpu.CompilerParams(dimension_semantics=sem, vmem_limit_bytes=vmem_limit)


MOD_TN = 1536


def _mod_kernel(cond_ref, w_ref, b_ref, o_ref):
    c = cond_ref[...]
    s = _silu(c).astype(BF16)
    o_ref[...] = jnp.dot(s, w_ref[...].astype(BF16), preferred_element_type=F32) + b_ref[...]


def _modulation(cond8, w_ada, b_ada):
    tn = MOD_TN
    n = w_ada.shape[1]
    return pl.pallas_call(
        _mod_kernel,
        grid=(n // tn,),
        in_specs=[pl.BlockSpec((8, D_MODEL), lambda j: (0, 0)),
                  pl.BlockSpec((D_MODEL, tn), lambda j: (0, j)),
                  pl.BlockSpec((1, tn), lambda j: (0, j))],
        out_specs=pl.BlockSpec((8, tn), lambda j: (0, j)),
        out_shape=jax.ShapeDtypeStruct((8, n), F32),
        compiler_params=_params("arbitrary"),
        name="modulation",
    )(cond8, w_ada, b_ada.reshape(1, n))


NORM_TM = 1024
NORM_ROWS = 16
NORM_UNROLL = 8
NORM_CTX_STEPS = N_CTX_TOK // NORM_TM
NORM_STEPS_PER_DEC_SEQ = DEC_SEQ // NORM_TM


def _mod_row(i, ctx_steps, steps_per_seq):
    return jnp.where(i < ctx_steps, 0, 1 + (i - ctx_steps) // steps_per_seq)


def _norm_kernel(xp_ref, xs_ref, shift_ref, scale_ref, nw_ref, o_ref):
    i = pl.program_id(0)

    gain = nw_ref[...] * (1.0 + scale_ref[0])
    shift = shift_ref[0]

    def body(x_ref):
        def rows(r, carry):
            sl = pl.ds(pl.multiple_of(r * NORM_ROWS, NORM_ROWS), NORM_ROWS)
            x = x_ref[sl, :]
            ms = jnp.mean(x * x, axis=-1, keepdims=True)
            o_ref[sl, :] = (x * lax.rsqrt(ms + EPS) * gain + shift).astype(BF16)
            return carry

        lax.fori_loop(0, NORM_TM // NORM_ROWS, rows, 0, unroll=NORM_UNROLL)

    @pl.when(i < NORM_CTX_STEPS)
    def _():
        body(xp_ref)

    @pl.when(i >= NORM_CTX_STEPS)
    def _():
        body(xs_ref)


def _norm_modulate(xp, xs, shift3, scale3, norm_w):
    steps = N_TOK // NORM_TM
    row = functools.partial(_mod_row, ctx_steps=NORM_CTX_STEPS, steps_per_seq=NORM_STEPS_PER_DEC_SEQ)
    return pl.pallas_call(
        _norm_kernel,
        grid=(steps,),
        in_specs=[pl.BlockSpec((NORM_TM, D_MODEL), lambda i: (jnp.minimum(i, NORM_CTX_STEPS - 1), 0)),
                  pl.BlockSpec((NORM_TM, D_MODEL), lambda i: (jnp.maximum(i - NORM_CTX_STEPS, 0), 0)),
                  pl.BlockSpec((1, 1, D_MODEL), lambda i: (row(i), 0, 0)),
                  pl.BlockSpec((1, 1, D_MODEL), lambda i: (row(i), 0, 0)),
                  pl.BlockSpec((1, D_MODEL), lambda i: (0, 0))],
        out_specs=pl.BlockSpec((NORM_TM, D_MODEL), lambda i: (i, 0)),
        out_shape=jax.ShapeDtypeStruct((N_TOK, D_MODEL), BF16),
        compiler_params=_params("arbitrary"),
        name="norm_modulate",
    )(xp, xs, shift3, scale3, norm_w.reshape(1, D_MODEL))


PROJ_TM = 1024
PROJ_TN = 1024
PROJ_CTX_STEPS = N_CTX_TOK // PROJ_TM
PROJ_SEGS = PROJ_TM // (CHUNK * CH_PER_SEG)


def _head_group_ones():
    r = lax.broadcasted_iota(jnp.int32, (MXU_TILE, MXU_TILE), 0) // HEAD_DIM
    c = lax.broadcasted_iota(jnp.int32, (MXU_TILE, MXU_TILE), 1) // HEAD_DIM
    return jnp.where(r == c, 1.0, 0.0).astype(BF16)


def _head_rms(acc, nw):
    ones = _head_group_ones()
    outs = []
    for c in range(acc.shape[1] // MXU_TILE):
        a = acc[:, c * MXU_TILE:(c + 1) * MXU_TILE]
        ssum = jnp.dot((a * a).astype(BF16), ones, preferred_element_type=F32)
        outs.append(a * lax.rsqrt(ssum * (1.0 / HEAD_DIM) + EPS))
    return jnp.concatenate(outs, axis=1) * nw


def _cast_weight_tile(w_ref, wbf):
    @pl.when(pl.program_id(1) == 0)
    def _():
        wbf[...] = w_ref[...].astype(BF16)


def _proj_s5_kernel(h_ref, w_ref, u_ref, zs_ref, wbf):
    j = pl.program_id(0)
    _cast_weight_tile(w_ref, wbf)

    def column_blocks():
        for c in range(PROJ_TN // MXU_TILE):
            cols = slice(c * MXU_TILE, (c + 1) * MXU_TILE)
            yield cols, jnp.dot(h_ref[...], wbf[:, cols], preferred_element_type=F32)

    @pl.when(j == 0)
    def _():
        pad = jnp.zeros((U_PITCH - CHUNK, MXU_TILE), F32)
        for cols, acc in column_blocks():
            for seg in range(PROJ_SEGS):
                for ch in range(CH_PER_SEG):
                    r0 = (seg * CH_PER_SEG + ch) * CHUNK
                    u_ref[0, ch, seg, 0:CHUNK, cols] = acc[r0:r0 + CHUNK, :]
                    u_ref[0, ch, seg, CHUNK:U_PITCH, cols] = pad

    @pl.when(j == 1)
    def _():
        for cols, acc in column_blocks():
            zs_ref[:, cols] = acc.astype(BF16)


def _proj_qkv_kernel(h_ref, w_ref, nw_ref, o_ref, kf_ref, vf_ref, wbf):
    j = pl.program_id(0)
    is_ctx = pl.program_id(1) < PROJ_CTX_STEPS
    _cast_weight_tile(w_ref, wbf)

    def tile(normed, scale, f32_ref):
        width = 2 * MXU_TILE if normed else MXU_TILE
        for c in range(PROJ_TN // width):
            cols = slice(c * width, (c + 1) * width)
            a = jnp.dot(h_ref[...], wbf[:, cols], preferred_element_type=F32)
            if normed:
                a = _head_rms(a, nw_ref[0][:, cols])
            if scale is not None:
                a = a * scale
            o_ref[:, cols] = a.astype(BF16)
            if f32_ref is not None:
                f32_ref[:, cols] = a

    pl.when(j == 0)(lambda: tile(True, HEAD_DIM ** -0.5, None))
    pl.when((j == 1) & is_ctx)(lambda: tile(True, None, kf_ref))
    pl.when((j == 1) & jnp.logical_not(is_ctx))(lambda: tile(True, None, None))
    pl.when((j == 2) & is_ctx)(lambda: tile(False, None, vf_ref))
    pl.when((j == 2) & jnp.logical_not(is_ctx))(lambda: tile(False, None, None))


def _proj_gate_kernel(h_ref, w_ref, o_ref, wbf):
    _cast_weight_tile(w_ref, wbf)
    o_ref[...] = jnp.dot(h_ref[...], wbf[...], preferred_element_type=F32).astype(BF16)


def _in_proj(h, w_in, q_norm_w, k_norm_w):
    ni = N_TOK // PROJ_TM
    last = ni - 1
    ctx_last = PROJ_CTX_STEPS - 1
    tiles_per_path = N_SEG // PROJ_SEGS
    h_spec = pl.BlockSpec((PROJ_TM, D_MODEL), lambda j, i: (i, 0))
    scratch = [pltpu.VMEM((D_MODEL, PROJ_TN), BF16)]
    params = _params("arbitrary", "arbitrary")

    def u_map(j, i):
        ii = jnp.where(j == 0, i, last)
        return (ii // tiles_per_path, 0, ii % tiles_per_path, 0, 0)

    u, zs = pl.pallas_call(
        _proj_s5_kernel,
        grid=(2, ni),
        in_specs=[h_spec, pl.BlockSpec((D_MODEL, PROJ_TN), lambda j, i: (0, OFF_U // PROJ_TN + j))],
        out_specs=[pl.BlockSpec((1, CH_PER_SEG, PROJ_SEGS, U_PITCH, PROJ_TN), u_map),
                   pl.BlockSpec((PROJ_TM, PROJ_TN), lambda j, i: (jnp.where(j == 0, 0, i), 0))],
        out_shape=[jax.ShapeDtypeStruct((2, CH_PER_SEG, N_SEG, U_PITCH, SSM_WIDTH), F32),
                   jax.ShapeDtypeStruct((N_TOK, SSM_WIDTH), BF16)],
        scratch_shapes=scratch, compiler_params=params, name="in_proj_s5",
    )(h, w_in)

    nw = jnp.stack([jnp.tile(q_norm_w.reshape(1, HEAD_DIM), (1, PROJ_TN // HEAD_DIM)),
                    jnp.tile(k_norm_w.reshape(1, HEAD_DIM), (1, PROJ_TN // HEAD_DIM))])
    qkv, kf32, vf32 = pl.pallas_call(
        _proj_qkv_kernel,
        grid=(3, ni),
        in_specs=[h_spec, pl.BlockSpec((D_MODEL, PROJ_TN), lambda j, i: (0, OFF_Q // PROJ_TN + j)),
                  pl.BlockSpec((1, 1, PROJ_TN), lambda j, i: (jnp.where(j == 1, 1, 0), 0, 0))],
        out_specs=[pl.BlockSpec((PROJ_TM, PROJ_TN), lambda j, i: (i, j)),
                   pl.BlockSpec((PROJ_TM, PROJ_TN),
                                lambda j, i: (jnp.where(j == 0, 0, jnp.where(j == 1, jnp.minimum(i, ctx_last),
                                                                             ctx_last)), 0)),
                   pl.BlockSpec((PROJ_TM, PROJ_TN),
                                lambda j, i: (jnp.where(j == 2, jnp.minimum(i, ctx_last), 0), 0))],
        out_shape=[jax.ShapeDtypeStruct((N_TOK, 3 * ATT_WIDTH), BF16),
                   jax.ShapeDtypeStruct((N_CTX_TOK, ATT_WIDTH), F32),
                   jax.ShapeDtypeStruct((N_CTX_TOK, ATT_WIDTH), F32)],
        scratch_shapes=scratch, name="in_proj_qkv",
        compiler_params=_params("arbitrary", "arbitrary", vmem_limit=QKV_VMEM_LIMIT),
    )(h, w_in, nw)

    gate_tiles = 2 * D_MODEL // PROJ_TN
    gz = pl.pallas_call(
        _proj_gate_kernel,
        grid=(gate_tiles + 1, ni),
        in_specs=[h_spec,
                  pl.BlockSpec((D_MODEL, PROJ_TN),
                               lambda j, i: (0, jnp.where(j < gate_tiles, OFF_GS // PROJ_TN + j,
                                                          OFF_ZA // PROJ_TN)))],
        out_specs=pl.BlockSpec((PROJ_TM, PROJ_TN), lambda j, i: (i, j)),
        out_shape=jax.ShapeDtypeStruct((N_TOK, 2 * D_MODEL + ATT_WIDTH), BF16),
        scratch_shapes=scratch, compiler_params=params, name="in_proj_gates",
    )(h, w_in)
    return u, zs, qkv, kf32, vf32, gz


CACHE_TM = 256


def _cache_layout_kernel(k_ref, v_ref, ko_ref, vo_ref):
    for src, dst in ((k_ref, ko_ref), (v_ref, vo_ref)):
        for pair in range(ATT_WIDTH // LANES):
            x = src[:, pair * LANES:(pair + 1) * LANES]
            swapped = pltpu.roll(x, HEAD_DIM, axis=1)
            dst[pl.ds(2 * pair, CACHE_TM, stride=N_HEADS), :] = x[:, :HEAD_DIM]
            dst[pl.ds(2 * pair + 1, CACHE_TM, stride=N_HEADS), :] = swapped[:, :HEAD_DIM]


def _cache_layout(kf32, vf32):
    assert LANES == 2 * HEAD_DIM
    spec_in = pl.BlockSpec((CACHE_TM, ATT_WIDTH), lambda i: (i, 0))
    spec_out = pl.BlockSpec((CACHE_TM * N_HEADS, HEAD_DIM), lambda i: (i, 0))
    shape = jax.ShapeDtypeStruct((N_CTX_TOK * N_HEADS, HEAD_DIM), F32)
    return pl.pallas_call(
        _cache_layout_kernel,
        grid=(N_CTX_TOK // CACHE_TM,),
        in_specs=[spec_in, spec_in], out_specs=[spec_out, spec_out], out_shape=[shape, shape],
        compiler_params=_params("arbitrary"), name="cache_layout",
    )(kf32, vf32)


def _pow_select(e, pows):
    rr = jnp.where((e & 1) != 0, pows[0][0], 1.0)
    ri = jnp.where((e & 1) != 0, pows[0][1], 0.0)
    for b in range(1, len(pows)):
        bit = (e & (1 << b)) != 0
        fr = jnp.where(bit, pows[b][0], 1.0)
        fi = jnp.where(bit, pows[b][1], 0.0)
        rr, ri = _cmul(rr, ri, fr, fi)
    return rr, ri


def _discretize(a_re, a_im, log_dt):
    lr = jnp.minimum(a_re, -1e-4)
    li = a_im
    dt = jnp.exp(log_dt)
    mag = jnp.exp(lr * dt)
    br = mag * jnp.cos(li * dt)
    bi = mag * jnp.sin(li * dt)
    den = lr * lr + li * li
    nr = br - 1.0
    cr = (nr * lr + bi * li) / den
    ci = (bi * lr - nr * li) / den
    return (br, bi), (cr, ci)


def _squarings(pr, pi, n):
    out = [(pr, pi)]
    for _ in range(n):
        pr, pi = _cmul(pr, pi, pr, pi)
        out.append((pr, pi))
    return out


def _dot_nt_exact(a, b):
    return lax.dot_general(a, b, (((1,), (1,)), ((), ())), preferred_element_type=F32, precision=HIGHEST)


def _ssm_build_ops(n_groups, rowp_ref, btr_ref, bti_ref, ctr_ref, cti_ref, t_ref, s_ref, rt_ref, lam_ref, powc_ref):
    lane_b = lax.broadcasted_iota(jnp.int32, (SSM_GROUP, DSTATE), 1)
    lane_t = lax.broadcasted_iota(jnp.int32, (SSM_GROUP, FLAT), 1)
    row_c = lax.broadcasted_iota(jnp.int32, (CH_PER_SEG, DSTATE), 0)
    lane_c = lax.broadcasted_iota(jnp.int32, (CH_PER_SEG, DSTATE), 1)
    exp_c = jnp.where(lane_c < SSM_STATE, row_c, (CH_PER_SEG - 1) - row_c)

    for g in range(n_groups):
        rp = rowp_ref[g]
        (lbr, lbi), (cfr, cfi) = _discretize(rp[0:1], rp[1:2], rp[2:3])
        sq = _squarings(lbr, lbi, 8)
        bbr, bbi = _cmul(btr_ref[g], bti_ref[g], cfr, cfi)

        ptr, pti = _pow_select(exp_c, sq[:4])

        s_re, s_im = [], []
        for sp in range(CHUNK):
            e = CHUNK - 1 - sp
            br, bi = _cmul(bbr, bbi, ptr[e:e + 1], pti[e:e + 1])
            s_re.append(br)
            s_im.append(bi)
        s_ref[g] = jnp.concatenate([jnp.concatenate(s_re, axis=0), jnp.concatenate(s_im, axis=0)],
                                   axis=1).astype(BF16)

        xcr = jnp.concatenate([ctr_ref[g]] * CHUNK, axis=0)
        xci = jnp.concatenate([cti_ref[g]] * CHUNK, axis=0)
        xpr = jnp.broadcast_to(ptr[:, None, :], (CHUNK, SSM_GROUP, DSTATE)).reshape(FLAT, DSTATE)
        xpi = jnp.broadcast_to(pti[:, None, :], (CHUNK, SSM_GROUP, DSTATE)).reshape(FLAT, DSTATE)
        ykr, yki = _cmul(xcr, xci, xpr, xpi)
        yrr, yri = _cmul(ykr, yki, lbr, lbi)
        rt_ref[g] = jnp.concatenate([yrr, -yri], axis=1).astype(BF16)

        fwd = lane_b < SSM_STATE
        lhs = jnp.concatenate([jnp.where(fwd, bbr, 0.0), jnp.where(fwd, bbi, 0.0),
                               jnp.where(fwd, 0.0, bbr), jnp.where(fwd, 0.0, bbi)], axis=0)
        p1 = _dot_nt_exact(lhs, ykr)
        p2 = _dot_nt_exact(lhs, yki)
        kf = p1[0:16] - p2[16:32]
        kb = p1[32:48] - p2[48:64]
        for sp in range(CHUNK):
            tf = kf if sp == 0 else pltpu.roll(kf, SSM_GROUP * sp, axis=1)
            tf = jnp.where(lane_t >= SSM_GROUP * sp, tf, 0.0)
            shift = (FLAT - SSM_GROUP * (CHUNK - 1 - sp)) % FLAT
            tb = kb if shift == 0 else pltpu.roll(kb, shift, axis=1)
            tb = jnp.where(lane_t < SSM_GROUP * (sp + 1), tb, 0.0)
            t_ref[g, sp * SSM_GROUP:(sp + 1) * SSM_GROUP, :] = (tf + tb).astype(BF16)

        l16 = sq[4]
        l256 = sq[8]
        lam_ref[g] = jnp.concatenate([l16[0], l16[1], l256[0], l256[1],
                                      jnp.zeros((4, DSTATE), F32)], axis=0)
        q16 = _squarings(l16[0], l16[1], 3)
        pcr, pci = _pow_select(exp_c, q16)
        powc_ref[g, 0] = pcr
        powc_ref[g, 1] = pci


SSM_PATHS = ((BATCH, SEQ // (CHUNK * CH_PER_SEG)), (DEC_BATCH, DEC_SEQ // (CHUNK * CH_PER_SEG)))


def _ssm_path(u, s_op, t_op, rt_op, lam, powc_ref, h0r, h0i, nseg):
    lane = lax.broadcasted_iota(jnp.int32, (N_SEG, DSTATE), 1)
    isf = lane < SSM_STATE
    l16r, l16i, l256r, l256i = lam[0:1], lam[1:2], lam[2:3], lam[3:4]

    z = jnp.dot(u, s_op, preferred_element_type=F32)
    zre, zim = z[:, :DSTATE], z[:, DSTATE:]

    hr = jnp.zeros((N_SEG, DSTATE), F32)
    hi = jnp.zeros((N_SEG, DSTATE), F32)
    hist = []
    for k in range(CH_PER_SEG):
        hist.append((hr, hi))
        kb = CH_PER_SEG - 1 - k
        zr = jnp.where(isf, zre[k * N_SEG:(k + 1) * N_SEG], zre[kb * N_SEG:(kb + 1) * N_SEG])
        zi = jnp.where(isf, zim[k * N_SEG:(k + 1) * N_SEG], zim[kb * N_SEG:(kb + 1) * N_SEG])
        nr, ni = _cmul(l16r, l16i, hr, hi)
        hr, hi = nr + zr, ni + zi

    zero_start = h0r is None
    if nseg > 1 and not zero_start:
        h0r = jnp.broadcast_to(h0r[:, None, :], (N_SEG // nseg, nseg, DSTATE)).reshape(N_SEG, DSTATE)
        h0i = jnp.broadcast_to(h0i[:, None, :], (N_SEG // nseg, nseg, DSTATE)).reshape(N_SEG, DSTATE)
    seg = lax.broadcasted_iota(jnp.int32, (N_SEG, DSTATE), 0) & (nseg - 1)
    segpow = _squarings(l256r, l256i, max(int(math.log2(nseg)), 0))

    def shifted(x, d):
        dn = jnp.where(seg >= d, pltpu.roll(x, d, axis=0), 0.0)
        up = jnp.where(seg <= nseg - 1 - d, pltpu.roll(x, N_SEG - d, axis=0), 0.0)
        return jnp.where(isf, dn, up)

    pr, pi = hr, hi
    d = 1
    lvl = 0
    while d < nseg:
        ar, ai = _cmul(segpow[lvl][0], segpow[lvl][1], shifted(pr, d), shifted(pi, d))
        pr, pi = pr + ar, pi + ai
        d *= 2
        lvl += 1
    hsr = hsi = None
    if nseg > 1:
        hsr, hsi = shifted(pr, 1), shifted(pi, 1)
        if not zero_start:
            e_in = jnp.where(isf, seg, nseg - 1 - seg)
            wr, wi = _pow_select(e_in, segpow[:lvl])
            ar, ai = _cmul(wr, wi, h0r, h0i)
            hsr, hsi = hsr + ar, hsi + ai
    elif not zero_start:
        hsr, hsi = h0r, h0i
    if zero_start:
        fin = jnp.concatenate([pr, pi], axis=1)
    else:
        er, ei = _cmul(segpow[lvl][0], segpow[lvl][1], h0r, h0i)
        fin = jnp.concatenate([pr + er, pi + ei], axis=1)

    rows = []
    for c in range(CH_PER_SEG):
        cb = CH_PER_SEG - 1 - c
        lr = jnp.where(isf, hist[c][0], hist[cb][0])
        li = jnp.where(isf, hist[c][1], hist[cb][1])
        if hsr is not None:
            ar, ai = _cmul(powc_ref[0, c:c + 1, :], powc_ref[1, c:c + 1, :], hsr, hsi)
            lr, li = lr + ar, li + ai
        rows.append(jnp.concatenate([lr, li], axis=1))
    hent = jnp.concatenate(rows, axis=0).astype(BF16)

    y = jnp.dot(u, t_op, preferred_element_type=F32) + _dot_nt(hent, rt_op)
    return y, fin


GROUPS_PER_STEP = LANES // SSM_GROUP


def _block_transpose(arrs, blk):
    n = len(arrs)
    width = arrs[0].shape[1]
    j = lax.broadcasted_iota(jnp.int32, arrs[0].shape, 1) // blk
    k = n // 2
    while k >= 1:
        bit = (j & k) != 0
        new = list(arrs)
        for x in range(n):
            if x & k == 0:
                a, b = arrs[x], arrs[x | k]
                new[x] = jnp.where(bit, pltpu.roll(b, k * blk, axis=1), a)
                new[x | k] = jnp.where(bit, b, pltpu.roll(a, width - k * blk, axis=1))
        arrs = new
        k //= 2
    return arrs


def _ssm_kernel(u_ref, rowp_ref, btr_ref, bti_ref, ctr_ref, cti_ref, d_ref, h0cr_ref, h0ci_ref, h0dr_ref, h0di_ref,
                y_ref, fin_ref, ubuf, ybuf, t_ref, s_ref, r_ref, lam_ref, powc_ref):
    _ssm_build_ops(GROUPS_PER_STEP, rowp_ref, btr_ref, bti_ref, ctr_ref, cti_ref,
                   t_ref, s_ref, r_ref, lam_ref, powc_ref)
    h0 = ((h0cr_ref, h0ci_ref), (h0dr_ref, h0di_ref))
    nhalf = FLAT // LANES
    rows = N_SEG * CH_PER_SEG
    def token_rows(path, tok, pitch):
        return pl.ds(path * rows * pitch + tok, rows, stride=pitch)

    for path, (_, nseg) in enumerate(SSM_PATHS):
        for half in range(nhalf):
            toks = [pltpu.bitcast(u_ref[token_rows(path, half * GROUPS_PER_STEP + sb, U_PITCH), :].astype(BF16),
                                  jnp.uint32) for sb in range(GROUPS_PER_STEP)]
            grouped = _block_transpose(toks, SSM_GROUP)
            for g in range(GROUPS_PER_STEP):
                ubuf[g, :, half * LANES:(half + 1) * LANES] = pltpu.bitcast(grouped[g], BF16)

        for g in range(GROUPS_PER_STEP):
            h0r, h0i = (None, None) if h0[path] is None else (h0[path][0][g], h0[path][1][g])
            y, fin = _ssm_path(ubuf[g], s_ref[g], t_ref[g], r_ref[g], lam_ref[g], powc_ref.at[g], h0r, h0i, nseg)
            ybuf[g] = y
            fin_ref[path, g] = fin

        for half in range(nhalf):
            grouped = [ybuf[g, :, half * LANES:(half + 1) * LANES] for g in range(GROUPS_PER_STEP)]
            toks = _block_transpose(grouped, SSM_GROUP)
            for sb in range(GROUPS_PER_STEP):
                tok = half * GROUPS_PER_STEP + sb
                y_ref[token_rows(path, tok, CHUNK), :] = (toks[sb]
                                                         + u_ref[token_rows(path, tok, U_PITCH), :] * d_ref[...])


def _ssm_scan(u2d, rowp, btr, bti, ctr, cti, dtile, h0c_re, h0c_im, h0d_re, h0d_im):
    G = SSM_GROUPS
    gs = GROUPS_PER_STEP
    rows = N_SEG * CH_PER_SEG
    return pl.pallas_call(
        _ssm_kernel,
        grid=(G // gs,),
        in_specs=[
            pl.BlockSpec((N_TOK // CHUNK * U_PITCH, LANES), lambda o: (0, o)),
            pl.BlockSpec((gs, 3, DSTATE), lambda o: (o, 0, 0)),
            pl.BlockSpec((gs, SSM_GROUP, DSTATE), lambda o: (o, 0, 0)),
            pl.BlockSpec((gs, SSM_GROUP, DSTATE), lambda o: (o, 0, 0)),
            pl.BlockSpec((gs, SSM_GROUP, DSTATE), lambda o: (o, 0, 0)),
            pl.BlockSpec((gs, SSM_GROUP, DSTATE), lambda o: (o, 0, 0)),
            pl.BlockSpec((1, LANES), lambda o: (0, o)),
            pl.BlockSpec((gs, BATCH, DSTATE), lambda o: (o, 0, 0)),
            pl.BlockSpec((gs, BATCH, DSTATE), lambda o: (o, 0, 0)),
            pl.BlockSpec((gs, DEC_BATCH, DSTATE), lambda o: (o, 0, 0)),
            pl.BlockSpec((gs, DEC_BATCH, DSTATE), lambda o: (o, 0, 0))],
        out_specs=[pl.BlockSpec((N_TOK, LANES), lambda o: (0, o)),
                   pl.BlockSpec((2, gs, N_SEG, 2 * DSTATE), lambda o: (0, o, 0, 0))],
        out_shape=[jax.ShapeDtypeStruct((N_TOK, SSM_WIDTH), F32),
                   jax.ShapeDtypeStruct((2, G, N_SEG, 2 * DSTATE), F32)],
        scratch_shapes=[pltpu.VMEM((gs, rows, FLAT), BF16), pltpu.VMEM((gs, rows, FLAT), F32),
                        pltpu.VMEM((gs, FLAT, FLAT), BF16), pltpu.VMEM((gs, FLAT, FLAT), BF16),
                        pltpu.VMEM((gs, FLAT, FLAT), BF16), pltpu.VMEM((gs, 8, DSTATE), F32),
                        pltpu.VMEM((gs, 2, CH_PER_SEG, DSTATE), F32)],
        compiler_params=_params("arbitrary"),
        name="ssm_scan",
    )(u2d, rowp, btr, bti, ctr, cti, dtile, h0c_re, h0c_im, h0d_re, h0d_im)


CTX_BB = 4
HEADS_PER_STEP = MXU_TILE // HEAD_DIM
HEAD_LANES = HEADS_PER_STEP * HEAD_DIM


def _softmax_pv(scores, values):
    m = scores[0].max(axis=-1, keepdims=True)
    for s in scores[1:]:
        m = jnp.maximum(m, s.max(axis=-1, keepdims=True))
    l = None
    o = None
    for s, v in zip(scores, values):
        p = jnp.exp(s - m)
        ls = p.sum(axis=-1, keepdims=True)
        os_ = jnp.dot(p.astype(BF16), v, preferred_element_type=F32)
        l = ls if l is None else l + ls
        o = os_ if o is None else o + os_
    return o / l


def _head_of_lane():
    return lax.broadcasted_iota(jnp.int32, (1, HEAD_LANES), 1) // HEAD_DIM


def _merge_heads(outs, za):
    head = _head_of_lane()
    o = outs[-1]
    for hh in range(HEADS_PER_STEP - 2, -1, -1):
        o = jnp.where(head == hh, outs[hh], o)
    return (o * _silu(za.astype(F32))).astype(BF16)


def _ctx_attn_kernel(q_ref, k_ref, v_ref, za_ref, o_ref):
    head = _head_of_lane()
    for b in range(CTX_BB):
        sl = slice(b * SEQ, (b + 1) * SEQ)
        q, k, v = q_ref[sl, :], k_ref[sl, :], v_ref[sl, :]
        outs = []
        for hh in range(HEADS_PER_STEP):
            qh = jnp.where(head == hh, q, jnp.zeros_like(q))
            outs.append(_softmax_pv([_dot_nt(qh, k)], [v]))
        o_ref[sl, :] = _merge_heads(outs, za_ref[sl, :])


HEAD_BLOCKS = ATT_WIDTH // HEAD_LANES
QKV_Q, QKV_K, QKV_V = 0, HEAD_BLOCKS, 2 * HEAD_BLOCKS
GZ_ZA = 2 * D_MODEL // HEAD_LANES


def _ctx_attention(qkv, gz):
    rows = CTX_BB * SEQ

    def spec(first):
        return pl.BlockSpec((rows, HEAD_LANES), lambda b, hq: (b, first + hq))

    return pl.pallas_call(
        _ctx_attn_kernel,
        grid=(BATCH // CTX_BB, N_HEADS // HEADS_PER_STEP),
        in_specs=[spec(QKV_Q), spec(QKV_K), spec(QKV_V), spec(GZ_ZA)],
        out_specs=spec(0),
        out_shape=jax.ShapeDtypeStruct((N_CTX_TOK, ATT_WIDTH), BF16),
        compiler_params=_params("arbitrary", "arbitrary"),
        name="ctx_attention",
    )(qkv, qkv, qkv, gz)


N_DROW = 2 * WIN_ROWS - 1


def _nbr_build_table(band_ref, tbl):
    lane = lax.broadcasted_iota(jnp.int32, (GRID_W, 2 * GRID_W), 1)
    qc = lax.broadcasted_iota(jnp.int32, (GRID_W, 2 * GRID_W), 0)
    kc = lane & (GRID_W - 1)
    cs = jnp.clip(qc - WIN_COLS // 2, 0, GRID_W - WIN_COLS)
    valid = jnp.logical_and(kc >= cs, kc < cs + WIN_COLS)
    for hh in range(HEADS_PER_STEP):
        for dr in range(N_DROW):
            x = jnp.broadcast_to(band_ref[hh, dr:dr + 1, :], (GRID_W, 2 * GRID_W))
            lo = pltpu.roll(x, 0, axis=1, stride=1, stride_axis=0)
            hi = pltpu.roll(x, GRID_W, axis=1, stride=1, stride_axis=0)
            tbl[hh, dr] = jnp.where(valid, jnp.where(lane < GRID_W, lo, hi), NEG_INF)
        tbl[hh, N_DROW] = jnp.full((GRID_W, 2 * GRID_W), NEG_INF, F32)


NBR_KEYS = NBR_K + PAST_LEN
KEY_TILE = MXU_TILE
SOFTMAX_ROWS = 64


def _nbr_block_index(a):
    ks = jnp.clip(Q_ROWS_PER_BLOCK * a - Q_ROWS_PER_BLOCK, 0, GRID_H - NBR_KROWS)
    idx = []
    for ri in range(Q_ROWS_PER_BLOCK):
        r = Q_ROWS_PER_BLOCK * a + ri
        rs = jnp.clip(r - WIN_ROWS // 2, 0, GRID_H - WIN_ROWS)
        row = []
        for kri in range(NBR_KROWS):
            kr = ks + kri
            valid = jnp.logical_and(kr >= rs, kr < rs + WIN_ROWS)
            row.append(jnp.where(valid, kr - r + WIN_ROWS - 1, N_DROW))
        idx.append(row)
    return idx


def _nbr_bias_tile(tbl, hh, idx, t):
    half0 = lax.broadcasted_iota(jnp.int32, (1, 2 * GRID_W), 1) < GRID_W
    krows = KEY_TILE // GRID_W
    rows = []
    for ri in range(Q_ROWS_PER_BLOCK):
        tiles = [jnp.where(half0, tbl[hh, idx[ri][kri]], tbl[hh, idx[ri][kri + 1]])
                 for kri in range(krows * t, krows * (t + 1), 2)]
        rows.append(jnp.concatenate(tiles, axis=1))
    return jnp.concatenate(rows, axis=0)


def _nbr_attn_kernel(q_ref, k_ref, v_ref, kc_ref, vc_ref, band_ref, za_ref, o_ref,
                     tbl, kcat, vcat, s_scr, p_scr, o_scr):
    a = pl.program_id(1)

    @pl.when(a == 0)
    def _():
        _nbr_build_table(band_ref, tbl)

    start = pl.multiple_of(jnp.clip(a * NBR_Q - NBR_Q, 0, DEC_SEQ - NBR_K), NBR_Q)
    head = _head_of_lane()
    idx = _nbr_block_index(a)
    for b in range(DEC_BATCH):
        kcat[b, 0:NBR_K, :] = k_ref[b, pl.ds(start, NBR_K), :]
        kcat[b, NBR_K:NBR_KEYS, :] = kc_ref[b].astype(BF16)
        vcat[b, 0:NBR_K, :] = v_ref[b, pl.ds(start, NBR_K), :]
        vcat[b, NBR_K:NBR_KEYS, :] = vc_ref[b].astype(BF16)

    def scores(b, hh):
        q = q_ref[b]
        qh = jnp.where(head == hh, q, jnp.zeros_like(q))
        mrun = None
        for t in range(NBR_KEYS // KEY_TILE):
            st = _dot_nt(qh, kcat[b, t * KEY_TILE:(t + 1) * KEY_TILE, :])
            if t < NBR_K // KEY_TILE:
                st = st + _nbr_bias_tile(tbl, hh, idx, t)
            s_scr[hh, :, t * KEY_TILE:(t + 1) * KEY_TILE] = st
            mt = jnp.maximum(st[:, :LANES], st[:, LANES:])
            mrun = mt if mrun is None else jnp.maximum(mrun, mt)
        return mrun.max(axis=-1, keepdims=True)

    def attend(b, hh, m):
        ls = []
        for r0 in range(0, NBR_Q, SOFTMAX_ROWS):
            mr = m[r0:r0 + SOFTMAX_ROWS]
            lrun = jnp.zeros((SOFTMAX_ROWS, LANES), F32)
            for c0 in range(0, NBR_KEYS, LANES):
                p = jnp.exp(s_scr[hh, r0:r0 + SOFTMAX_ROWS, c0:c0 + LANES] - mr)
                lrun = lrun + p
                p_scr[hh, r0:r0 + SOFTMAX_ROWS, c0:c0 + LANES] = p.astype(BF16)
            ls.append(lrun.sum(axis=-1, keepdims=True))
        l = jnp.concatenate(ls, axis=0)
        o = jnp.dot(p_scr[hh], vcat[b], preferred_element_type=F32)
        lanes = slice(hh * HEAD_DIM, (hh + 1) * HEAD_DIM)
        o_scr[b, :, lanes] = o[:, lanes] / l

    order = [(b, hh) for b in range(DEC_BATCH) for hh in range(HEADS_PER_STEP)]
    m_cur = scores(*order[0])
    for n, (b, hh) in enumerate(order):
        m_next = scores(*order[n + 1]) if n + 1 < len(order) else None
        attend(b, hh, m_cur)
        m_cur = m_next
    for b in range(DEC_BATCH):
        o_ref[b] = (o_scr[b] * _silu(za_ref[b].astype(F32))).astype(BF16)


def _nbr_attention(qkv4, kc, vc, band, gz4):
    first = N_CTX_TOK // DEC_SEQ // DEC_BATCH
    nblk = GRID_H // Q_ROWS_PER_BLOCK
    hw = HEAD_LANES

    def qspec(col0):
        return pl.BlockSpec((DEC_BATCH, NBR_Q, hw), lambda hp, a: (first, a, col0 + hp))

    def kspec(col0):
        return pl.BlockSpec((DEC_BATCH, DEC_SEQ, hw), lambda hp, a: (first, 0, col0 + hp))

    cspec = pl.BlockSpec((DEC_BATCH, PAST_LEN, hw), lambda hp, a: (0, 0, hp))
    bspec = pl.BlockSpec((HEADS_PER_STEP, N_DROW + 1, 2 * GRID_W), lambda hp, a: (hp, 0, 0))
    return pl.pallas_call(
        _nbr_attn_kernel,
        grid=(N_HEADS // HEADS_PER_STEP, nblk),
        in_specs=[qspec(QKV_Q), kspec(QKV_K), kspec(QKV_V), cspec, cspec, bspec, qspec(GZ_ZA)],
        out_specs=pl.BlockSpec((DEC_BATCH, NBR_Q, hw), lambda hp, a: (0, a, hp)),
        out_shape=jax.ShapeDtypeStruct((DEC_BATCH, DEC_SEQ, ATT_WIDTH), BF16),
        scratch_shapes=[pltpu.VMEM((HEADS_PER_STEP, N_DROW + 1, GRID_W, 2 * GRID_W), F32),
                        pltpu.VMEM((DEC_BATCH, NBR_KEYS, HEAD_LANES), BF16),
                        pltpu.VMEM((DEC_BATCH, NBR_KEYS, HEAD_LANES), BF16),
                        pltpu.VMEM((HEADS_PER_STEP, NBR_Q, NBR_KEYS), F32),
                        pltpu.VMEM((HEADS_PER_STEP, NBR_Q, NBR_KEYS), BF16),
                        pltpu.VMEM((DEC_BATCH, NBR_Q, HEAD_LANES), F32)],
        compiler_params=_params("arbitrary", "arbitrary"),
        name="nbr_attention",
    )(qkv4, qkv4, qkv4, kc, vc, band, gz4)


def _nbr_bias_band(rpb):
    ncol = 2 * WIN_COLS - 1
    fill = jnp.full(rpb.shape[:2] + (2 * GRID_W - ncol,), NEG_INF, F32)
    band = jnp.concatenate([rpb[..., WIN_COLS - 1:], fill, rpb[..., :WIN_COLS - 1]], axis=-1)
    return jnp.pad(band, ((0, 0), (0, 1), (0, 0)))


BACK_TM = CHUNK * CH_PER_SEG
BACK_CTX_STEPS = N_CTX_TOK // BACK_TM
BACK_STEPS_PER_DEC_SEQ = DEC_SEQ // BACK_TM


BACK_WARM = 8


def _back_kernel(xp_ref, xs_ref, y_ref, zs_ref, ac_ref, ad_ref, gs_ref, ga_ref, gate_ref, bglu_ref,
                 wglu_ref, wso_ref, wao_ref, wo_ref, op_ref, os_ref, wglu_bf, wso_bf, wao_bf, wo_bf):
    i = pl.program_id(0)
    step = i - BACK_WARM

    @pl.when(i < BACK_WARM)
    def _():
        for src, dst in ((wglu_ref, wglu_bf), (wso_ref, wso_bf), (wao_ref, wao_bf), (wo_ref, wo_bf)):
            slab = src.shape[0]
            dst[pl.ds(pl.multiple_of(i * slab, slab), slab), :] = src[...].astype(BF16)

    def compute(x, a2):
        ys = _gelu_tanh(y_ref[0].reshape(BACK_TM, SSM_WIDTH))
        t = jnp.dot(ys.astype(BF16), wglu_bf[...], preferred_element_type=F32) + bglu_ref[...]
        ys = ys * _sigmoid(t) * _silu(zs_ref[...].astype(F32))
        p_s = jnp.dot(ys.astype(BF16), wso_bf[...], preferred_element_type=F32)
        p_a = jnp.dot(a2, wao_bf[...], preferred_element_type=F32)
        merged = (_sigmoid(gs_ref[...].astype(F32)) * p_s + _sigmoid(ga_ref[...].astype(F32)) * p_a)
        return x + gate_ref[0] * jnp.dot(merged.astype(BF16), wo_bf[...], preferred_element_type=F32)

    @pl.when(jnp.logical_and(step >= 0, step < BACK_CTX_STEPS))
    def _():
        op_ref[...] = compute(xp_ref[...], ac_ref[...])

    @pl.when(step >= BACK_CTX_STEPS)
    def _():
        os_ref[...] = compute(xs_ref[...], ad_ref[...])


def _back(xp, xs, y, zs, a_ctx, a_dec, gz, gate3, b_glu, w_glu, w_so, w_ao, w_o):
    steps = N_TOK // BACK_TM
    n0 = BACK_CTX_STEPS
    warm = BACK_WARM
    row = functools.partial(_mod_row, ctx_steps=n0, steps_per_seq=BACK_STEPS_PER_DEC_SEQ)

    def tile(i):
        return jnp.maximum(i - warm, 0)

    lo = lambda i: (jnp.minimum(tile(i), n0 - 1), 0)
    hi = lambda i: (jnp.maximum(tile(i) - n0, 0), 0)
    cur = lambda i: (tile(i), 0)
    const = lambda i: (0, 0)
    slab = lambda i: (jnp.minimum(i, warm - 1), 0)

    def weight(w):
        return pl.BlockSpec((w.shape[0] // warm, w.shape[1]), slab)

    return pl.pallas_call(
        _back_kernel,
        grid=(warm + steps,),
        in_specs=[pl.BlockSpec((BACK_TM, D_MODEL), lo),
                  pl.BlockSpec((BACK_TM, D_MODEL), hi),
                  pl.BlockSpec((1, CH_PER_SEG, None, CHUNK, SSM_WIDTH),
                               lambda i: (tile(i) // N_SEG, 0, tile(i) % N_SEG, 0, 0)),
                  pl.BlockSpec((BACK_TM, SSM_WIDTH), cur),
                  pl.BlockSpec((BACK_TM, ATT_WIDTH), lo),
                  pl.BlockSpec((BACK_TM, ATT_WIDTH), hi),
                  pl.BlockSpec((BACK_TM, D_MODEL), lambda i: (tile(i), 0)),
                  pl.BlockSpec((BACK_TM, D_MODEL), lambda i: (tile(i), 1)),
                  pl.BlockSpec((1, 1, D_MODEL), lambda i: (row(tile(i)), 0, 0)),
                  pl.BlockSpec((1, SSM_WIDTH), const),
                  weight(w_glu), weight(w_so), weight(w_ao), weight(w_o)],
        out_specs=[pl.BlockSpec((BACK_TM, D_MODEL), lo),
                   pl.BlockSpec((BACK_TM, D_MODEL), hi)],
        out_shape=[jax.ShapeDtypeStruct((N_CTX_TOK, D_MODEL), F32),
                   jax.ShapeDtypeStruct((N_DEC_TOK, D_MODEL), F32)],
        scratch_shapes=[pltpu.VMEM(w.shape, BF16) for w in (w_glu, w_so, w_ao, w_o)],
        compiler_params=_params("arbitrary"),
        name="gated_output",
    )(xp, xs, y, zs, a_ctx, a_dec, gz, gz, gate3, b_glu.reshape(1, SSM_WIDTH), w_glu, w_so, w_ao, w_o)


def _layer(xp, xs, cache_k, cache_v, st_re, st_im, c, c_ctx, norm_w, w_ada, b_ada, w_in, q_norm_w, k_norm_w,
           rpb, a_re, a_im, log_dt, b_re, b_im, c_re, c_im, d, w_glu, b_glu, w_so, w_ao, w_o):
    G, P = SSM_GROUPS, SSM_STATE

    cond8 = jnp.zeros((8, D_MODEL), F32).at[0].set(c_ctx).at[1:1 + DEC_BATCH].set(c)
    mod = _modulation(cond8, w_ada, b_ada)
    shift3 = mod[:1 + DEC_BATCH, None, :D_MODEL]
    scale3 = mod[:1 + DEC_BATCH, None, D_MODEL:2 * D_MODEL]
    gate3 = mod[:1 + DEC_BATCH, None, 2 * D_MODEL:]

    h = _norm_modulate(xp, xs, shift3, scale3, norm_w)
    u, zs, qkv, kf32, vf32, gz = _in_proj(h, w_in, q_norm_w, k_norm_w)

    arow = jnp.transpose(a_re, (1, 0, 2)).reshape(G, 2 * P)
    airow = jnp.transpose(a_im, (1, 0, 2)).reshape(G, 2 * P)
    dtrow = jnp.broadcast_to(jnp.transpose(log_dt, (1, 0))[:, :, None], (G, 2, P)).reshape(G, 2 * P)
    rowp = jnp.stack([arow, airow, dtrow], axis=1)
    btr = jnp.transpose(b_re, (1, 3, 0, 2)).reshape(G, SSM_GROUP, 2 * P)
    bti = jnp.transpose(b_im, (1, 3, 0, 2)).reshape(G, SSM_GROUP, 2 * P)
    ctr = jnp.transpose(c_re, (1, 2, 0, 3)).reshape(G, SSM_GROUP, 2 * P)
    cti = jnp.transpose(c_im, (1, 2, 0, 3)).reshape(G, SSM_GROUP, 2 * P)

    dtile = d.reshape(1, SSM_WIDTH)
    h0c = jnp.zeros((G, BATCH, 2 * P), F32)
    h0d_re = jnp.transpose(st_re, (2, 0, 1, 3)).reshape(G, DEC_BATCH, 2 * P)
    h0d_im = jnp.transpose(st_im, (2, 0, 1, 3)).reshape(G, DEC_BATCH, 2 * P)
    y2d, fin = _ssm_scan(u.reshape(N_TOK // CHUNK * U_PITCH, SSM_WIDTH), rowp, btr, bti, ctr, cti, dtile,
                         h0c, h0c, h0d_re, h0d_im)
    y = y2d.reshape(2, CH_PER_SEG, N_SEG, CHUNK, SSM_WIDTH)

    a_ctx = _ctx_attention(qkv, gz)
    nseq4 = N_TOK // DEC_SEQ
    a_dec = _nbr_attention(qkv.reshape(nseq4, DEC_SEQ, qkv.shape[1]),
                           cache_k.reshape(DEC_BATCH, PAST_LEN, ATT_WIDTH),
                           cache_v.reshape(DEC_BATCH, PAST_LEN, ATT_WIDTH),
                           _nbr_bias_band(rpb.astype(F32)),
                           gz.reshape(nseq4, DEC_SEQ, gz.shape[1]))

    yp, ys_out = _back(xp, xs, y, zs, a_ctx, a_dec.reshape(N_DEC_TOK, ATT_WIDTH), gz, gate3, b_glu,
                       w_glu, w_so, w_ao, w_o)

    fin_ctx = fin[0]
    new_re = jnp.transpose(fin_ctx[:, :, :2 * P].reshape(G, BATCH, 2, P), (1, 2, 0, 3))
    new_im = jnp.transpose(fin_ctx[:, :, 2 * P:].reshape(G, BATCH, 2, P), (1, 2, 0, 3))
    new_k, new_v = _cache_layout(kf32, vf32)
    return yp, ys_out, new_k, new_v, new_re, new_im


def kernel(x_prompt, x_sample, cache_k, cache_v, state_ssm_re, state_ssm_im, c, c_ctx, norm_w, w_ada, b_ada,
           w_in, q_norm_w, k_norm_w, rel_pos_bias, ssm_a_re, ssm_a_im, ssm_log_dt, ssm_b_re, ssm_b_im,
           ssm_c_re, ssm_c_im, ssm_d, w_glu, b_glu, w_ssm_out, w_att_out, w_o):
    depth = norm_w.shape[0]
    xp = x_prompt.reshape(N_CTX_TOK, D_MODEL)
    xs = x_sample.reshape(N_DEC_TOK, D_MODEL)
    new_k, new_v, new_re, new_im = [], [], [], []
    for l in range(depth):
        xp, xs, kl, vl, rl, il = _layer(
            xp, xs, cache_k[:, l], cache_v[:, l], state_ssm_re[:, l], state_ssm_im[:, l], c, c_ctx,
            norm_w[l], w_ada[l], b_ada[l], w_in[l], q_norm_w[l], k_norm_w[l], rel_pos_bias[l],
            ssm_a_re[l], ssm_a_im[l], ssm_log_dt[l], ssm_b_re[l], ssm_b_im[l], ssm_c_re[l], ssm_c_im[l],
            ssm_d[l], w_glu[l], b_glu[l], w_ssm_out[l], w_att_out[l], w_o[l])
        new_k.append(kl.reshape(BATCH, SEQ, N_HEADS, HEAD_DIM))
        new_v.append(vl.reshape(BATCH, SEQ, N_HEADS, HEAD_DIM))
        new_re.append(rl)
        new_im.append(il)
    return (xp.reshape(BATCH, SEQ, D_MODEL), xs.reshape(DEC_BATCH, DEC_SEQ, D_MODEL),
            jnp.stack(new_k, axis=1), jnp.stack(new_v, axis=1),
            jnp.stack(new_re, axis=1), jnp.stack(new_im, axis=1))
```

```python
import functools
import math

import jax
import jax.numpy as jnp
from jax import lax
from jax.experimental import pallas as pl
from jax.experimental.pallas import tpu as pltpu

D_MODEL = 2048
BATCH = 16
SEQ = 256
DEC_BATCH = 2
DEC_SEQ = 2048
PAST_LEN = 512
GRID_W = 64
GRID_H = DEC_SEQ // GRID_W
SSM_WIDTH = D_MODEL // 2
SSM_GROUP = 16
SSM_GROUPS = SSM_WIDTH // SSM_GROUP
SSM_STATE = 64
N_HEADS = 16
HEAD_DIM = 64
ATT_WIDTH = N_HEADS * HEAD_DIM
WIN_ROWS = 8
WIN_COLS = 16
EPS = 1e-6
NEG_INF = -1e30

N_CTX_TOK = BATCH * SEQ
N_DEC_TOK = DEC_BATCH * DEC_SEQ
N_TOK = N_CTX_TOK + N_DEC_TOK

OFF_U = 0
OFF_ZS = SSM_WIDTH
OFF_Q = 2 * SSM_WIDTH
OFF_K = OFF_Q + ATT_WIDTH
OFF_V = OFF_K + ATT_WIDTH
OFF_ZA = OFF_V + ATT_WIDTH
OFF_GS = OFF_ZA + ATT_WIDTH
OFF_GA = OFF_GS + D_MODEL

CHUNK = 16
FLAT = CHUNK * SSM_GROUP
N_SEG = 16
CH_PER_SEG = 16
U_PITCH = 24
DSTATE = 2 * SSM_STATE

Q_ROWS_PER_BLOCK = 4
NBR_Q = Q_ROWS_PER_BLOCK * GRID_W
NBR_KROWS = 12
NBR_K = NBR_KROWS * GRID_W

V7X_VMEM_BYTES = 64 * 1024 * 1024
VMEM_LIMIT = V7X_VMEM_BYTES - 8 * 1024 * 1024
QKV_VMEM_LIMIT = V7X_VMEM_BYTES - 4 * 1024 * 1024

LANES = 128
MXU_TILE = 256

F32 = jnp.float32
BF16 = jnp.bfloat16
HIGHEST = lax.Precision.HIGHEST


def _sigmoid(x):
    return 1.0 / (1.0 + jnp.exp(-x))


def _silu(x):
    return x * _sigmoid(x)


def _gelu_tanh(x):
    return 0.5 * x * (1.0 + jnp.tanh(math.sqrt(2.0 / math.pi) * (x + 0.044715 * (x * x * x))))


def _cmul(ar, ai, br, bi):
    return ar * br - ai * bi, ar * bi + ai * br


def _dot_nt(a, b):
    return lax.dot_general(a, b, (((1,), (1,)), ((), ())), preferred_element_type=F32)


def _params(*sem, vmem_limit=VMEM_LIMIT):
    return pltpu.CompilerParams(dimension_semantics=sem, vmem_limit_bytes=vmem_limit)


MOD_TN = 1536


def _mod_kernel(cond_ref, w_ref, b_ref, o_ref):
    c = cond_ref[...]
    s = _silu(c).astype(BF16)
    o_ref[...] = jnp.dot(s, w_ref[...].astype(BF16), preferred_element_type=F32) + b_ref[...]


def _modulation(cond8, w_ada, b_ada):
    tn = MOD_TN
    n = w_ada.shape[1]
    return pl.pallas_call(
        _mod_kernel,
        grid=(n // tn,),
        in_specs=[pl.BlockSpec((8, D_MODEL), lambda j: (0, 0)),
                  pl.BlockSpec((D_MODEL, tn), lambda j: (0, j)),
                  pl.BlockSpec((1, tn), lambda j: (0, j))],
        out_specs=pl.BlockSpec((8, tn), lambda j: (0, j)),
        out_shape=jax.ShapeDtypeStruct((8, n), F32),
        compiler_params=_params("arbitrary"),
        name="modulation",
    )(cond8, w_ada, b_ada.reshape(1, n))


NORM_TM = 1024
NORM_ROWS = 16
NORM_UNROLL = 8
NORM_CTX_STEPS = N_CTX_TOK // NORM_TM
NORM_STEPS_PER_DEC_SEQ = DEC_SEQ // NORM_TM


def _mod_row(i, ctx_steps, steps_per_seq):
    return jnp.where(i < ctx_steps, 0, 1 + (i - ctx_steps) // steps_per_seq)


def _norm_kernel(xp_ref, xs_ref, shift_ref, scale_ref, nw_ref, o_ref):
    i = pl.program_id(0)

    gain = nw_ref[...] * (1.0 + scale_ref[0])
    shift = shift_ref[0]

    def body(x_ref):
        def rows(r, carry):
            sl = pl.ds(pl.multiple_of(r * NORM_ROWS, NORM_ROWS), NORM_ROWS)
            x = x_ref[sl, :]
            ms = jnp.mean(x * x, axis=-1, keepdims=True)
            o_ref[sl, :] = (x * lax.rsqrt(ms + EPS) * gain + shift).astype(BF16)
            return carry

        lax.fori_loop(0, NORM_TM // NORM_ROWS, rows, 0, unroll=NORM_UNROLL)

    @pl.when(i < NORM_CTX_STEPS)
    def _():
        body(xp_ref)

    @pl.when(i >= NORM_CTX_STEPS)
    def _():
        body(xs_ref)


def _norm_modulate(xp, xs, shift3, scale3, norm_w):
    steps = N_TOK // NORM_TM
    row = functools.partial(_mod_row, ctx_steps=NORM_CTX_STEPS, steps_per_seq=NORM_STEPS_PER_DEC_SEQ)
    return pl.pallas_call(
        _norm_kernel,
        grid=(steps,),
        in_specs=[pl.BlockSpec((NORM_TM, D_MODEL), lambda i: (jnp.minimum(i, NORM_CTX_STEPS - 1), 0)),
                  pl.BlockSpec((NORM_TM, D_MODEL), lambda i: (jnp.maximum(i - NORM_CTX_STEPS, 0), 0)),
                  pl.BlockSpec((1, 1, D_MODEL), lambda i: (row(i), 0, 0)),
                  pl.BlockSpec((1, 1, D_MODEL), lambda i: (row(i), 0, 0)),
                  pl.BlockSpec((1, D_MODEL), lambda i: (0, 0))],
        out_specs=pl.BlockSpec((NORM_TM, D_MODEL), lambda i: (i, 0)),
        out_shape=jax.ShapeDtypeStruct((N_TOK, D_MODEL), BF16),
        compiler_params=_params("arbitrary"),
        name="norm_modulate",
    )(xp, xs, shift3, scale3, norm_w.reshape(1, D_MODEL))


PROJ_TM = 1024
PROJ_TN = 1024
PROJ_CTX_STEPS = N_CTX_TOK // PROJ_TM
PROJ_SEGS = PROJ_TM // (CHUNK * CH_PER_SEG)


def _head_group_ones():
    r = lax.broadcasted_iota(jnp.int32, (MXU_TILE, MXU_TILE), 0) // HEAD_DIM
    c = lax.broadcasted_iota(jnp.int32, (MXU_TILE, MXU_TILE), 1) // HEAD_DIM
    return jnp.where(r == c, 1.0, 0.0).astype(BF16)


def _head_rms(acc, nw):
    ones = _head_group_ones()
    outs = []
    for c in range(acc.shape[1] // MXU_TILE):
        a = acc[:, c * MXU_TILE:(c + 1) * MXU_TILE]
        ssum = jnp.dot((a * a).astype(BF16), ones, preferred_element_type=F32)
        outs.append(a * lax.rsqrt(ssum * (1.0 / HEAD_DIM) + EPS))
    return jnp.concatenate(outs, axis=1) * nw


def _cast_weight_tile(w_ref, wbf):
    @pl.when(pl.program_id(1) == 0)
    def _():
        wbf[...] = w_ref[...].astype(BF16)


def _proj_s5_kernel(h_ref, w_ref, u_ref, zs_ref, wbf):
    j = pl.program_id(0)
    _cast_weight_tile(w_ref, wbf)

    def column_blocks():
        for c in range(PROJ_TN // MXU_TILE):
            cols = slice(c * MXU_TILE, (c + 1) * MXU_TILE)
            yield cols, jnp.dot(h_ref[...], wbf[:, cols], preferred_element_type=F32)

    @pl.when(j == 0)
    def _():
        pad = jnp.zeros((U_PITCH - CHUNK, MXU_TILE), F32)
        for cols, acc in column_blocks():
            for seg in range(PROJ_SEGS):
                for ch in range(CH_PER_SEG):
                    r0 = (seg * CH_PER_SEG + ch) * CHUNK
                    u_ref[0, ch, seg, 0:CHUNK, cols] = acc[r0:r0 + CHUNK, :]
                    u_ref[0, ch, seg, CHUNK:U_PITCH, cols] = pad

    @pl.when(j == 1)
    def _():
        for cols, acc in column_blocks():
            zs_ref[:, cols] = acc.astype(BF16)


def _proj_qkv_kernel(h_ref, w_ref, nw_ref, o_ref, kf_ref, vf_ref, wbf):
    j = pl.program_id(0)
    is_ctx = pl.program_id(1) < PROJ_CTX_STEPS
    _cast_weight_tile(w_ref, wbf)

    def tile(normed, scale, f32_ref):
        width = 2 * MXU_TILE if normed else MXU_TILE
        for c in range(PROJ_TN // width):
            cols = slice(c * width, (c + 1) * width)
            a = jnp.dot(h_ref[...], wbf[:, cols], preferred_element_type=F32)
            if normed:
                a = _head_rms(a, nw_ref[0][:, cols])
            if scale is not None:
                a = a * scale
            o_ref[:, cols] = a.astype(BF16)
            if f32_ref is not None:
                f32_ref[:, cols] = a

    pl.when(j == 0)(lambda: tile(True, HEAD_DIM ** -0.5, None))
    pl.when((j == 1) & is_ctx)(lambda: tile(True, None, kf_ref))
    pl.when((j == 1) & jnp.logical_not(is_ctx))(lambda: tile(True, None, None))
    pl.when((j == 2) & is_ctx)(lambda: tile(False, None, vf_ref))
    pl.when((j == 2) & jnp.logical_not(is_ctx))(lambda: tile(False, None, None))


def _proj_gate_kernel(h_ref, w_ref, o_ref, wbf):
    _cast_weight_tile(w_ref, wbf)
    o_ref[...] = jnp.dot(h_ref[...], wbf[...], preferred_element_type=F32).astype(BF16)


def _in_proj(h, w_in, q_norm_w, k_norm_w):
    ni = N_TOK // PROJ_TM
    last = ni - 1
    ctx_last = PROJ_CTX_STEPS - 1
    tiles_per_path = N_SEG // PROJ_SEGS
    h_spec = pl.BlockSpec((PROJ_TM, D_MODEL), lambda j, i: (i, 0))
    scratch = [pltpu.VMEM((D_MODEL, PROJ_TN), BF16)]
    params = _params("arbitrary", "arbitrary")

    def u_map(j, i):
        ii = jnp.where(j == 0, i, last)
        return (ii // tiles_per_path, 0, ii % tiles_per_path, 0, 0)

    u, zs = pl.pallas_call(
        _proj_s5_kernel,
        grid=(2, ni),
        in_specs=[h_spec, pl.BlockSpec((D_MODEL, PROJ_TN), lambda j, i: (0, OFF_U // PROJ_TN + j))],
        out_specs=[pl.BlockSpec((1, CH_PER_SEG, PROJ_SEGS, U_PITCH, PROJ_TN), u_map),
                   pl.BlockSpec((PROJ_TM, PROJ_TN), lambda j, i: (jnp.where(j == 0, 0, i), 0))],
        out_shape=[jax.ShapeDtypeStruct((2, CH_PER_SEG, N_SEG, U_PITCH, SSM_WIDTH), F32),
                   jax.ShapeDtypeStruct((N_TOK, SSM_WIDTH), BF16)],
        scratch_shapes=scratch, compiler_params=params, name="in_proj_s5",
    )(h, w_in)

    nw = jnp.stack([jnp.tile(q_norm_w.reshape(1, HEAD_DIM), (1, PROJ_TN // HEAD_DIM)),
                    jnp.tile(k_norm_w.reshape(1, HEAD_DIM), (1, PROJ_TN // HEAD_DIM))])
    qkv, kf32, vf32 = pl.pallas_call(
        _proj_qkv_kernel,
        grid=(3, ni),
        in_specs=[h_spec, pl.BlockSpec((D_MODEL, PROJ_TN), lambda j, i: (0, OFF_Q // PROJ_TN + j)),
                  pl.BlockSpec((1, 1, PROJ_TN), lambda j, i: (jnp.where(j == 1, 1, 0), 0, 0))],
        out_specs=[pl.BlockSpec((PROJ_TM, PROJ_TN), lambda j, i: (i, j)),
                   pl.BlockSpec((PROJ_TM, PROJ_TN),
                                lambda j, i: (jnp.where(j == 0, 0, jnp.where(j == 1, jnp.minimum(i, ctx_last),
                                                                             ctx_last)), 0)),
                   pl.BlockSpec((PROJ_TM, PROJ_TN),
                                lambda j, i: (jnp.where(j == 2, jnp.minimum(i, ctx_last), 0), 0))],
        out_shape=[jax.ShapeDtypeStruct((N_TOK, 3 * ATT_WIDTH), BF16),
                   jax.ShapeDtypeStruct((N_CTX_TOK, ATT_WIDTH), F32),
                   jax.ShapeDtypeStruct((N_CTX_TOK, ATT_WIDTH), F32)],
        scratch_shapes=scratch, name="in_proj_qkv",
        compiler_params=_params("arbitrary", "arbitrary", vmem_limit=QKV_VMEM_LIMIT),
    )(h, w_in, nw)

    gate_tiles = 2 * D_MODEL // PROJ_TN
    gz = pl.pallas_call(
        _proj_gate_kernel,
        grid=(gate_tiles + 1, ni),
        in_specs=[h_spec,
                  pl.BlockSpec((D_MODEL, PROJ_TN),
                               lambda j, i: (0, jnp.where(j < gate_tiles, OFF_GS // PROJ_TN + j,
                                                          OFF_ZA // PROJ_TN)))],
        out_specs=pl.BlockSpec((PROJ_TM, PROJ_TN), lambda j, i: (i, j)),
        out_shape=jax.ShapeDtypeStruct((N_TOK, 2 * D_MODEL + ATT_WIDTH), BF16),
        scratch_shapes=scratch, compiler_params=params, name="in_proj_gates",
    )(h, w_in)
    return u, zs, qkv, kf32, vf32, gz


def _pow_select(e, pows):
    rr = jnp.where((e & 1) != 0, pows[0][0], 1.0)
    ri = jnp.where((e & 1) != 0, pows[0][1], 0.0)
    for b in range(1, len(pows)):
        bit = (e & (1 << b)) != 0
        fr = jnp.where(bit, pows[b][0], 1.0)
        fi = jnp.where(bit, pows[b][1], 0.0)
        rr, ri = _cmul(rr, ri, fr, fi)
    return rr, ri


def _discretize(a_re, a_im, log_dt):
    lr = jnp.minimum(a_re, -1e-4)
    li = a_im
    dt = jnp.exp(log_dt)
    mag = jnp.exp(lr * dt)
    br = mag * jnp.cos(li * dt)
    bi = mag * jnp.sin(li * dt)
    den = lr * lr + li * li
    nr = br - 1.0
    cr = (nr * lr + bi * li) / den
    ci = (bi * lr - nr * li) / den
    return (br, bi), (cr, ci)


def _squarings(pr, pi, n):
    out = [(pr, pi)]
    for _ in range(n):
        pr, pi = _cmul(pr, pi, pr, pi)
        out.append((pr, pi))
    return out


def _dot_nt_exact(a, b):
    return lax.dot_general(a, b, (((1,), (1,)), ((), ())), preferred_element_type=F32, precision=HIGHEST)


def _ssm_build_ops(n_groups, rowp_ref, btr_ref, bti_ref, ctr_ref, cti_ref, t_ref, s_ref, rt_ref, lam_ref, powc_ref):
    lane_b = lax.broadcasted_iota(jnp.int32, (SSM_GROUP, DSTATE), 1)
    lane_t = lax.broadcasted_iota(jnp.int32, (SSM_GROUP, FLAT), 1)
    row_c = lax.broadcasted_iota(jnp.int32, (CH_PER_SEG, DSTATE), 0)
    lane_c = lax.broadcasted_iota(jnp.int32, (CH_PER_SEG, DSTATE), 1)
    exp_c = jnp.where(lane_c < SSM_STATE, row_c, (CH_PER_SEG - 1) - row_c)

    for g in range(n_groups):
        rp = rowp_ref[g]
        (lbr, lbi), (cfr, cfi) = _discretize(rp[0:1], rp[1:2], rp[2:3])
        sq = _squarings(lbr, lbi, 8)
        bbr, bbi = _cmul(btr_ref[g], bti_ref[g], cfr, cfi)

        ptr, pti = _pow_select(exp_c, sq[:4])

        s_re, s_im = [], []
        for sp in range(CHUNK):
            e = CHUNK - 1 - sp
            br, bi = _cmul(bbr, bbi, ptr[e:e + 1], pti[e:e + 1])
            s_re.append(br)
            s_im.append(bi)
        s_ref[g] = jnp.concatenate([jnp.concatenate(s_re, axis=0), jnp.concatenate(s_im, axis=0)],
                                   axis=1).astype(BF16)

        xcr = jnp.concatenate([ctr_ref[g]] * CHUNK, axis=0)
        xci = jnp.concatenate([cti_ref[g]] * CHUNK, axis=0)
        xpr = jnp.broadcast_to(ptr[:, None, :], (CHUNK, SSM_GROUP, DSTATE)).reshape(FLAT, DSTATE)
        xpi = jnp.broadcast_to(pti[:, None, :], (CHUNK, SSM_GROUP, DSTATE)).reshape(FLAT, DSTATE)
        ykr, yki = _cmul(xcr, xci, xpr, xpi)
        yrr, yri = _cmul(ykr, yki, lbr, lbi)
        rt_ref[g] = jnp.concatenate([yrr, -yri], axis=1).astype(BF16)

        fwd = lane_b < SSM_STATE
        lhs = jnp.concatenate([jnp.where(fwd, bbr, 0.0), jnp.where(fwd, bbi, 0.0),
                               jnp.where(fwd, 0.0, bbr), jnp.where(fwd, 0.0, bbi)], axis=0)
        p1 = _dot_nt_exact(lhs, ykr)
        p2 = _dot_nt_exact(lhs, yki)
        kf = p1[0:16] - p2[16:32]
        kb = p1[32:48] - p2[48:64]
        for sp in range(CHUNK):
            tf = kf if sp == 0 else pltpu.roll(kf, SSM_GROUP * sp, axis=1)
            tf = jnp.where(lane_t >= SSM_GROUP * sp, tf, 0.0)
            shift = (FLAT - SSM_GROUP * (CHUNK - 1 - sp)) % FLAT
            tb = kb if shift == 0 else pltpu.roll(kb, shift, axis=1)
            tb = jnp.where(lane_t < SSM_GROUP * (sp + 1), tb, 0.0)
            t_ref[g, sp * SSM_GROUP:(sp + 1) * SSM_GROUP, :] = (tf + tb).astype(BF16)

        l16 = sq[4]
        l256 = sq[8]
        lam_ref[g] = jnp.concatenate([l16[0], l16[1], l256[0], l256[1],
                                      jnp.zeros((4, DSTATE), F32)], axis=0)
        q16 = _squarings(l16[0], l16[1], 3)
        pcr, pci = _pow_select(exp_c, q16)
        powc_ref[g, 0] = pcr
        powc_ref[g, 1] = pci


SSM_PATHS = ((BATCH, SEQ // (CHUNK * CH_PER_SEG)), (DEC_BATCH, DEC_SEQ // (CHUNK * CH_PER_SEG)))


def _ssm_path(u, s_op, t_op, rt_op, lam, powc_ref, h0r, h0i, nseg):
    lane = lax.broadcasted_iota(jnp.int32, (N_SEG, DSTATE), 1)
    isf = lane < SSM_STATE
    l16r, l16i, l256r, l256i = lam[0:1], lam[1:2], lam[2:3], lam[3:4]

    z = jnp.dot(u, s_op, preferred_element_type=F32)
    zre, zim = z[:, :DSTATE], z[:, DSTATE:]

    hr = jnp.zeros((N_SEG, DSTATE), F32)
    hi = jnp.zeros((N_SEG, DSTATE), F32)
    hist = []
    for k in range(CH_PER_SEG):
        hist.append((hr, hi))
        kb = CH_PER_SEG - 1 - k
        zr = jnp.where(isf, zre[k * N_SEG:(k + 1) * N_SEG], zre[kb * N_SEG:(kb + 1) * N_SEG])
        zi = jnp.where(isf, zim[k * N_SEG:(k + 1) * N_SEG], zim[kb * N_SEG:(kb + 1) * N_SEG])
        nr, ni = _cmul(l16r, l16i, hr, hi)
        hr, hi = nr + zr, ni + zi

    zero_start = h0r is None
    if nseg > 1 and not zero_start:
        h0r = jnp.broadcast_to(h0r[:, None, :], (N_SEG // nseg, nseg, DSTATE)).reshape(N_SEG, DSTATE)
        h0i = jnp.broadcast_to(h0i[:, None, :], (N_SEG // nseg, nseg, DSTATE)).reshape(N_SEG, DSTATE)
    seg = lax.broadcasted_iota(jnp.int32, (N_SEG, DSTATE), 0) & (nseg - 1)
    segpow = _squarings(l256r, l256i, max(int(math.log2(nseg)), 0))

    def shifted(x, d):
        dn = jnp.where(seg >= d, pltpu.roll(x, d, axis=0), 0.0)
        up = jnp.where(seg <= nseg - 1 - d, pltpu.roll(x, N_SEG - d, axis=0), 0.0)
        return jnp.where(isf, dn, up)

    pr, pi = hr, hi
    d = 1
    lvl = 0
    while d < nseg:
        ar, ai = _cmul(segpow[lvl][0], segpow[lvl][1], shifted(pr, d), shifted(pi, d))
        pr, pi = pr + ar, pi + ai
        d *= 2
        lvl += 1
    hsr = hsi = None
    if nseg > 1:
        hsr, hsi = shifted(pr, 1), shifted(pi, 1)
        if not zero_start:
            e_in = jnp.where(isf, seg, nseg - 1 - seg)
            wr, wi = _pow_select(e_in, segpow[:lvl])
            ar, ai = _cmul(wr, wi, h0r, h0i)
            hsr, hsi = hsr + ar, hsi + ai
    elif not zero_start:
        hsr, hsi = h0r, h0i
    if zero_start:
        fin = jnp.concatenate([pr, pi], axis=1)
    else:
        er, ei = _cmul(segpow[lvl][0], segpow[lvl][1], h0r, h0i)
        fin = jnp.concatenate([pr + er, pi + ei], axis=1)

    rows = []
    for c in range(CH_PER_SEG):
        cb = CH_PER_SEG - 1 - c
        lr = jnp.where(isf, hist[c][0], hist[cb][0])
        li = jnp.where(isf, hist[c][1], hist[cb][1])
        if hsr is not None:
            ar, ai = _cmul(powc_ref[0, c:c + 1, :], powc_ref[1, c:c + 1, :], hsr, hsi)
            lr, li = lr + ar, li + ai
        rows.append(jnp.concatenate([lr, li], axis=1))
    hent = jnp.concatenate(rows, axis=0).astype(BF16)

    y = jnp.dot(u, t_op, preferred_element_type=F32) + _dot_nt(hent, rt_op)
    return y, fin


GROUPS_PER_STEP = LANES // SSM_GROUP


def _block_transpose(arrs, blk):
    n = len(arrs)
    width = arrs[0].shape[1]
    j = lax.broadcasted_iota(jnp.int32, arrs[0].shape, 1) // blk
    k = n // 2
    while k >= 1:
        bit = (j & k) != 0
        new = list(arrs)
        for x in range(n):
            if x & k == 0:
                a, b = arrs[x], arrs[x | k]
                new[x] = jnp.where(bit, pltpu.roll(b, k * blk, axis=1), a)
                new[x | k] = jnp.where(bit, b, pltpu.roll(a, width - k * blk, axis=1))
        arrs = new
        k //= 2
    return arrs


def _ssm_kernel(u_ref, rowp_ref, btr_ref, bti_ref, ctr_ref, cti_ref, d_ref, h0cr_ref, h0ci_ref, h0dr_ref, h0di_ref,
                y_ref, fin_ref, ubuf, ybuf, t_ref, s_ref, r_ref, lam_ref, powc_ref):
    _ssm_build_ops(GROUPS_PER_STEP, rowp_ref, btr_ref, bti_ref, ctr_ref, cti_ref,
                   t_ref, s_ref, r_ref, lam_ref, powc_ref)
    h0 = ((h0cr_ref, h0ci_ref), (h0dr_ref, h0di_ref))
    nhalf = FLAT // LANES
    rows = N_SEG * CH_PER_SEG
    def token_rows(path, tok, pitch):
        return pl.ds(path * rows * pitch + tok, rows, stride=pitch)

    for path, (_, nseg) in enumerate(SSM_PATHS):
        for half in range(nhalf):
            toks = [pltpu.bitcast(u_ref[token_rows(path, half * GROUPS_PER_STEP + sb, U_PITCH), :].astype(BF16),
                                  jnp.uint32) for sb in range(GROUPS_PER_STEP)]
            grouped = _block_transpose(toks, SSM_GROUP)
            for g in range(GROUPS_PER_STEP):
                ubuf[g, :, half * LANES:(half + 1) * LANES] = pltpu.bitcast(grouped[g], BF16)

        for g in range(GROUPS_PER_STEP):
            h0r, h0i = (None, None) if h0[path] is None else (h0[path][0][g], h0[path][1][g])
            y, fin = _ssm_path(ubuf[g], s_ref[g], t_ref[g], r_ref[g], lam_ref[g], powc_ref.at[g], h0r, h0i, nseg)
            ybuf[g] = y
            fin_ref[path, g] = fin

        for half in range(nhalf):
            grouped = [ybuf[g, :, half * LANES:(half + 1) * LANES] for g in range(GROUPS_PER_STEP)]
            toks = _block_transpose(grouped, SSM_GROUP)
            for sb in range(GROUPS_PER_STEP):
                tok = half * GROUPS_PER_STEP + sb
                y_ref[token_rows(path, tok, CHUNK), :] = (toks[sb]
                                                         + u_ref[token_rows(path, tok, U_PITCH), :] * d_ref[...])


def _ssm_scan(u2d, rowp, btr, bti, ctr, cti, dtile, h0c_re, h0c_im, h0d_re, h0d_im):
    G = SSM_GROUPS
    gs = GROUPS_PER_STEP
    rows = N_SEG * CH_PER_SEG
    return pl.pallas_call(
        _ssm_kernel,
        grid=(G // gs,),
        in_specs=[
            pl.BlockSpec((N_TOK // CHUNK * U_PITCH, LANES), lambda o: (0, o)),
            pl.BlockSpec((gs, 3, DSTATE), lambda o: (o, 0, 0)),
            pl.BlockSpec((gs, SSM_GROUP, DSTATE), lambda o: (o, 0, 0)),
            pl.BlockSpec((gs, SSM_GROUP, DSTATE), lambda o: (o, 0, 0)),
            pl.BlockSpec((gs, SSM_GROUP, DSTATE), lambda o: (o, 0, 0)),
            pl.BlockSpec((gs, SSM_GROUP, DSTATE), lambda o: (o, 0, 0)),
            pl.BlockSpec((1, LANES), lambda o: (0, o)),
            pl.BlockSpec((gs, BATCH, DSTATE), lambda o: (o, 0, 0)),
            pl.BlockSpec((gs, BATCH, DSTATE), lambda o: (o, 0, 0)),
            pl.BlockSpec((gs, DEC_BATCH, DSTATE), lambda o: (o, 0, 0)),
            pl.BlockSpec((gs, DEC_BATCH, DSTATE), lambda o: (o, 0, 0))],
        out_specs=[pl.BlockSpec((N_TOK, LANES), lambda o: (0, o)),
                   pl.BlockSpec((2, gs, N_SEG, 2 * DSTATE), lambda o: (0, o, 0, 0))],
        out_shape=[jax.ShapeDtypeStruct((N_TOK, SSM_WIDTH), F32),
                   jax.ShapeDtypeStruct((2, G, N_SEG, 2 * DSTATE), F32)],
        scratch_shapes=[pltpu.VMEM((gs, rows, FLAT), BF16), pltpu.VMEM((gs, rows, FLAT), F32),
                        pltpu.VMEM((gs, FLAT, FLAT), BF16), pltpu.VMEM((gs, FLAT, FLAT), BF16),
                        pltpu.VMEM((gs, FLAT, FLAT), BF16), pltpu.VMEM((gs, 8, DSTATE), F32),
                        pltpu.VMEM((gs, 2, CH_PER_SEG, DSTATE), F32)],
        compiler_params=_params("arbitrary"),
        name="ssm_scan",
    )(u2d, rowp, btr, bti, ctr, cti, dtile, h0c_re, h0c_im, h0d_re, h0d_im)


CTX_BB = 4
HEADS_PER_STEP = MXU_TILE // HEAD_DIM
HEAD_LANES = HEADS_PER_STEP * HEAD_DIM


def _softmax_pv(scores, values):
    m = scores[0].max(axis=-1, keepdims=True)
    for s in scores[1:]:
        m = jnp.maximum(m, s.max(axis=-1, keepdims=True))
    l = None
    o = None
    for s, v in zip(scores, values):
        p = jnp.exp(s - m)
        ls = p.sum(axis=-1, keepdims=True)
        os_ = jnp.dot(p.astype(BF16), v, preferred_element_type=F32)
        l = ls if l is None else l + ls
        o = os_ if o is None else o + os_
    return o / l


def _head_of_lane():
    return lax.broadcasted_iota(jnp.int32, (1, HEAD_LANES), 1) // HEAD_DIM


def _merge_heads(outs, za):
    head = _head_of_lane()
    o = outs[-1]
    for hh in range(HEADS_PER_STEP - 2, -1, -1):
        o = jnp.where(head == hh, outs[hh], o)
    return (o * _silu(za.astype(F32))).astype(BF16)


def _heads_to_rows(src, dst, tok0, ntok, head0):
    for pair in range(src.shape[1] // LANES):
        x = src[tok0:tok0 + ntok, pair * LANES:(pair + 1) * LANES]
        swapped = pltpu.roll(x, HEAD_DIM, axis=1)
        row = tok0 * N_HEADS + head0 + 2 * pair
        dst[pl.ds(row, ntok, stride=N_HEADS), :] = x[:, :HEAD_DIM]
        dst[pl.ds(row + 1, ntok, stride=N_HEADS), :] = swapped[:, :HEAD_DIM]


def _ctx_attn_kernel(q_ref, k_ref, v_ref, za_ref, kf_ref, vf_ref, o_ref, ko_ref, vo_ref):
    head = _head_of_lane()
    head0 = pl.program_id(1) * HEADS_PER_STEP
    for b in range(CTX_BB):
        sl = slice(b * SEQ, (b + 1) * SEQ)
        q, k, v = q_ref[sl, :], k_ref[sl, :], v_ref[sl, :]
        outs = []
        for hh in range(HEADS_PER_STEP):
            qh = jnp.where(head == hh, q, jnp.zeros_like(q))
            outs.append(_softmax_pv([_dot_nt(qh, k)], [v]))
        o_ref[sl, :] = _merge_heads(outs, za_ref[sl, :])
        _heads_to_rows(kf_ref, ko_ref, b * SEQ, SEQ, head0)
        _heads_to_rows(vf_ref, vo_ref, b * SEQ, SEQ, head0)


HEAD_BLOCKS = ATT_WIDTH // HEAD_LANES
QKV_Q, QKV_K, QKV_V = 0, HEAD_BLOCKS, 2 * HEAD_BLOCKS
GZ_ZA = 2 * D_MODEL // HEAD_LANES


def _ctx_attention(qkv, gz, kf32, vf32):
    assert LANES == 2 * HEAD_DIM
    rows = CTX_BB * SEQ

    def spec(first):
        return pl.BlockSpec((rows, HEAD_LANES), lambda b, hq: (b, first + hq))

    cache_spec = pl.BlockSpec((rows * N_HEADS, HEAD_DIM), lambda b, hq: (b, 0))
    cache_shape = jax.ShapeDtypeStruct((N_CTX_TOK * N_HEADS, HEAD_DIM), F32)
    return pl.pallas_call(
        _ctx_attn_kernel,
        grid=(BATCH // CTX_BB, N_HEADS // HEADS_PER_STEP),
        in_specs=[spec(QKV_Q), spec(QKV_K), spec(QKV_V), spec(GZ_ZA), spec(0), spec(0)],
        out_specs=[spec(0), cache_spec, cache_spec],
        out_shape=[jax.ShapeDtypeStruct((N_CTX_TOK, ATT_WIDTH), BF16), cache_shape, cache_shape],
        compiler_params=_params("arbitrary", "arbitrary"),
        name="ctx_attention",
    )(qkv, qkv, qkv, gz, kf32, vf32)


N_DROW = 2 * WIN_ROWS - 1


def _nbr_build_table(band_ref, tbl):
    lane = lax.broadcasted_iota(jnp.int32, (GRID_W, 2 * GRID_W), 1)
    qc = lax.broadcasted_iota(jnp.int32, (GRID_W, 2 * GRID_W), 0)
    kc = lane & (GRID_W - 1)
    cs = jnp.clip(qc - WIN_COLS // 2, 0, GRID_W - WIN_COLS)
    valid = jnp.logical_and(kc >= cs, kc < cs + WIN_COLS)
    for hh in range(HEADS_PER_STEP):
        for dr in range(N_DROW):
            x = jnp.broadcast_to(band_ref[hh, dr:dr + 1, :], (GRID_W, 2 * GRID_W))
            lo = pltpu.roll(x, 0, axis=1, stride=1, stride_axis=0)
            hi = pltpu.roll(x, GRID_W, axis=1, stride=1, stride_axis=0)
            tbl[hh, dr] = jnp.where(valid, jnp.where(lane < GRID_W, lo, hi), NEG_INF)
        tbl[hh, N_DROW] = jnp.full((GRID_W, 2 * GRID_W), NEG_INF, F32)


NBR_KEYS = NBR_K + PAST_LEN
KEY_TILE = MXU_TILE
SOFTMAX_ROWS = 64


def _nbr_block_index(a):
    ks = jnp.clip(Q_ROWS_PER_BLOCK * a - Q_ROWS_PER_BLOCK, 0, GRID_H - NBR_KROWS)
    idx = []
    for ri in range(Q_ROWS_PER_BLOCK):
        r = Q_ROWS_PER_BLOCK * a + ri
        rs = jnp.clip(r - WIN_ROWS // 2, 0, GRID_H - WIN_ROWS)
        row = []
        for kri in range(NBR_KROWS):
            kr = ks + kri
            valid = jnp.logical_and(kr >= rs, kr < rs + WIN_ROWS)
            row.append(jnp.where(valid, kr - r + WIN_ROWS - 1, N_DROW))
        idx.append(row)
    return idx


def _nbr_bias_tile(tbl, hh, idx, t):
    half0 = lax.broadcasted_iota(jnp.int32, (1, 2 * GRID_W), 1) < GRID_W
    krows = KEY_TILE // GRID_W
    rows = []
    for ri in range(Q_ROWS_PER_BLOCK):
        tiles = [jnp.where(half0, tbl[hh, idx[ri][kri]], tbl[hh, idx[ri][kri + 1]])
                 for kri in range(krows * t, krows * (t + 1), 2)]
        rows.append(jnp.concatenate(tiles, axis=1))
    return jnp.concatenate(rows, axis=0)


def _nbr_attn_kernel(q_ref, k_ref, v_ref, kc_ref, vc_ref, band_ref, za_ref, o_ref,
                     tbl, kcat, vcat, s_scr, p_scr, o_scr):
    a = pl.program_id(1)

    @pl.when(a == 0)
    def _():
        _nbr_build_table(band_ref, tbl)

    start = pl.multiple_of(jnp.clip(a * NBR_Q - NBR_Q, 0, DEC_SEQ - NBR_K), NBR_Q)
    head = _head_of_lane()
    idx = _nbr_block_index(a)
    for b in range(DEC_BATCH):
        kcat[b, 0:NBR_K, :] = k_ref[b, pl.ds(start, NBR_K), :]
        kcat[b, NBR_K:NBR_KEYS, :] = kc_ref[b].astype(BF16)
        vcat[b, 0:NBR_K, :] = v_ref[b, pl.ds(start, NBR_K), :]
        vcat[b, NBR_K:NBR_KEYS, :] = vc_ref[b].astype(BF16)

    def scores(b, hh):
        q = q_ref[b]
        qh = jnp.where(head == hh, q, jnp.zeros_like(q))
        mrun = None
        for t in range(NBR_KEYS // KEY_TILE):
            st = _dot_nt(qh, kcat[b, t * KEY_TILE:(t + 1) * KEY_TILE, :])
            if t < NBR_K // KEY_TILE:
                st = st + _nbr_bias_tile(tbl, hh, idx, t)
            s_scr[hh, :, t * KEY_TILE:(t + 1) * KEY_TILE] = st
            mt = jnp.maximum(st[:, :LANES], st[:, LANES:])
            mrun = mt if mrun is None else jnp.maximum(mrun, mt)
        return mrun.max(axis=-1, keepdims=True)

    def attend(b, hh, m):
        ls = []
        for r0 in range(0, NBR_Q, SOFTMAX_ROWS):
            mr = m[r0:r0 + SOFTMAX_ROWS]
            lrun = jnp.zeros((SOFTMAX_ROWS, LANES), F32)
            for c0 in range(0, NBR_KEYS, LANES):
                p = jnp.exp(s_scr[hh, r0:r0 + SOFTMAX_ROWS, c0:c0 + LANES] - mr)
                lrun = lrun + p
                p_scr[hh, r0:r0 + SOFTMAX_ROWS, c0:c0 + LANES] = p.astype(BF16)
            ls.append(lrun.sum(axis=-1, keepdims=True))
        l = jnp.concatenate(ls, axis=0)
        o = jnp.dot(p_scr[hh], vcat[b], preferred_element_type=F32)
        lanes = slice(hh * HEAD_DIM, (hh + 1) * HEAD_DIM)
        o_scr[b, :, lanes] = o[:, lanes] / l

    order = [(b, hh) for b in range(DEC_BATCH) for hh in range(HEADS_PER_STEP)]
    m_cur = scores(*order[0])
    for n, (b, hh) in enumerate(order):
        m_next = scores(*order[n + 1]) if n + 1 < len(order) else None
        attend(b, hh, m_cur)
        m_cur = m_next
    for b in range(DEC_BATCH):
        o_ref[b] = (o_scr[b] * _silu(za_ref[b].astype(F32))).astype(BF16)


def _nbr_attention(qkv4, kc, vc, band, gz4):
    first = N_CTX_TOK // DEC_SEQ // DEC_BATCH
    nblk = GRID_H // Q_ROWS_PER_BLOCK
    hw = HEAD_LANES

    def qspec(col0):
        return pl.BlockSpec((DEC_BATCH, NBR_Q, hw), lambda hp, a: (first, a, col0 + hp))

    def kspec(col0):
        return pl.BlockSpec((DEC_BATCH, DEC_SEQ, hw), lambda hp, a: (first, 0, col0 + hp))

    cspec = pl.BlockSpec((DEC_BATCH, PAST_LEN, hw), lambda hp, a: (0, 0, hp))
    bspec = pl.BlockSpec((HEADS_PER_STEP, N_DROW + 1, 2 * GRID_W), lambda hp, a: (hp, 0, 0))
    return pl.pallas_call(
        _nbr_attn_kernel,
        grid=(N_HEADS // HEADS_PER_STEP, nblk),
        in_specs=[qspec(QKV_Q), kspec(QKV_K), kspec(QKV_V), cspec, cspec, bspec, qspec(GZ_ZA)],
        out_specs=pl.BlockSpec((DEC_BATCH, NBR_Q, hw), lambda hp, a: (0, a, hp)),
        out_shape=jax.ShapeDtypeStruct((DEC_BATCH, DEC_SEQ, ATT_WIDTH), BF16),
        scratch_shapes=[pltpu.VMEM((HEADS_PER_STEP, N_DROW + 1, GRID_W, 2 * GRID_W), F32),
                        pltpu.VMEM((DEC_BATCH, NBR_KEYS, HEAD_LANES), BF16),
                        pltpu.VMEM((DEC_BATCH, NBR_KEYS, HEAD_LANES), BF16),
                        pltpu.VMEM((HEADS_PER_STEP, NBR_Q, NBR_KEYS), F32),
                        pltpu.VMEM((HEADS_PER_STEP, NBR_Q, NBR_KEYS), BF16),
                        pltpu.VMEM((DEC_BATCH, NBR_Q, HEAD_LANES), F32)],
        compiler_params=_params("arbitrary", "arbitrary"),
        name="nbr_attention",
    )(qkv4, qkv4, qkv4, kc, vc, band, gz4)


def _nbr_bias_band(rpb):
    ncol = 2 * WIN_COLS - 1
    fill = jnp.full(rpb.shape[:2] + (2 * GRID_W - ncol,), NEG_INF, F32)
    band = jnp.concatenate([rpb[..., WIN_COLS - 1:], fill, rpb[..., :WIN_COLS - 1]], axis=-1)
    return jnp.pad(band, ((0, 0), (0, 1), (0, 0)))


BACK_TM = CHUNK * CH_PER_SEG
BACK_CTX_STEPS = N_CTX_TOK // BACK_TM
BACK_STEPS_PER_DEC_SEQ = DEC_SEQ // BACK_TM


BACK_WARM = 8


def _back_kernel(xp_ref, xs_ref, y_ref, zs_ref, ac_ref, ad_ref, gs_ref, ga_ref, gate_ref, bglu_ref,
                 wglu_ref, wso_ref, wao_ref, wo_ref, op_ref, os_ref, wglu_bf, wso_bf, wao_bf, wo_bf):
    i = pl.program_id(0)
    step = i - BACK_WARM

    @pl.when(i < BACK_WARM)
    def _():
        for src, dst in ((wglu_ref, wglu_bf), (wso_ref, wso_bf), (wao_ref, wao_bf), (wo_ref, wo_bf)):
            slab = src.shape[0]
            dst[pl.ds(pl.multiple_of(i * slab, slab), slab), :] = src[...].astype(BF16)

    def compute(x, a2):
        ys = _gelu_tanh(y_ref[0].reshape(BACK_TM, SSM_WIDTH))
        t = jnp.dot(ys.astype(BF16), wglu_bf[...], preferred_element_type=F32) + bglu_ref[...]
        ys = ys * _sigmoid(t) * _silu(zs_ref[...].astype(F32))
        p_s = jnp.dot(ys.astype(BF16), wso_bf[...], preferred_element_type=F32)
        p_a = jnp.dot(a2, wao_bf[...], preferred_element_type=F32)
        merged = (_sigmoid(gs_ref[...].astype(F32)) * p_s + _sigmoid(ga_ref[...].astype(F32)) * p_a)
        return x + gate_ref[0] * jnp.dot(merged.astype(BF16), wo_bf[...], preferred_element_type=F32)

    @pl.when(jnp.logical_and(step >= 0, step < BACK_CTX_STEPS))
    def _():
        op_ref[...] = compute(xp_ref[...], ac_ref[...])

    @pl.when(step >= BACK_CTX_STEPS)
    def _():
        os_ref[...] = compute(xs_ref[...], ad_ref[...])


def _back(xp, xs, y, zs, a_ctx, a_dec, gz, gate3, b_glu, w_glu, w_so, w_ao, w_o):
    steps = N_TOK // BACK_TM
    n0 = BACK_CTX_STEPS
    warm = BACK_WARM
    row = functools.partial(_mod_row, ctx_steps=n0, steps_per_seq=BACK_STEPS_PER_DEC_SEQ)

    def tile(i):
        return jnp.maximum(i - warm, 0)

    lo = lambda i: (jnp.minimum(tile(i), n0 - 1), 0)
    hi = lambda i: (jnp.maximum(tile(i) - n0, 0), 0)
    cur = lambda i: (tile(i), 0)
    const = lambda i: (0, 0)
    slab = lambda i: (jnp.minimum(i, warm - 1), 0)

    def weight(w):
        return pl.BlockSpec((w.shape[0] // warm, w.shape[1]), slab)

    return pl.pallas_call(
        _back_kernel,
        grid=(warm + steps,),
        in_specs=[pl.BlockSpec((BACK_TM, D_MODEL), lo),
                  pl.BlockSpec((BACK_TM, D_MODEL), hi),
                  pl.BlockSpec((1, CH_PER_SEG, None, CHUNK, SSM_WIDTH),
                               lambda i: (tile(i) // N_SEG, 0, tile(i) % N_SEG, 0, 0)),
                  pl.BlockSpec((BACK_TM, SSM_WIDTH), cur),
                  pl.BlockSpec((BACK_TM, ATT_WIDTH), lo),
                  pl.BlockSpec((BACK_TM, ATT_WIDTH), hi),
                  pl.BlockSpec((BACK_TM, D_MODEL), lambda i: (tile(i), 0)),
                  pl.BlockSpec((BACK_TM, D_MODEL), lambda i: (tile(i), 1)),
                  pl.BlockSpec((1, 1, D_MODEL), lambda i: (row(tile(i)), 0, 0)),
                  pl.BlockSpec((1, SSM_WIDTH), const),
                  weight(w_glu), weight(w_so), weight(w_ao), weight(w_o)],
        out_specs=[pl.BlockSpec((BACK_TM, D_MODEL), lo),
                   pl.BlockSpec((BACK_TM, D_MODEL), hi)],
        out_shape=[jax.ShapeDtypeStruct((N_CTX_TOK, D_MODEL), F32),
                   jax.ShapeDtypeStruct((N_DEC_TOK, D_MODEL), F32)],
        scratch_shapes=[pltpu.VMEM(w.shape, BF16) for w in (w_glu, w_so, w_ao, w_o)],
        compiler_params=_params("arbitrary"),
        name="gated_output",
    )(xp, xs, y, zs, a_ctx, a_dec, gz, gz, gate3, b_glu.reshape(1, SSM_WIDTH), w_glu, w_so, w_ao, w_o)


def _layer(xp, xs, cache_k, cache_v, st_re, st_im, c, c_ctx, norm_w, w_ada, b_ada, w_in, q_norm_w, k_norm_w,
           rpb, a_re, a_im, log_dt, b_re, b_im, c_re, c_im, d, w_glu, b_glu, w_so, w_ao, w_o):
    G, P = SSM_GROUPS, SSM_STATE

    cond8 = jnp.zeros((8, D_MODEL), F32).at[0].set(c_ctx).at[1:1 + DEC_BATCH].set(c)
    mod = _modulation(cond8, w_ada, b_ada)
    shift3 = mod[:1 + DEC_BATCH, None, :D_MODEL]
    scale3 = mod[:1 + DEC_BATCH, None, D_MODEL:2 * D_MODEL]
    gate3 = mod[:1 + DEC_BATCH, None, 2 * D_MODEL:]

    h = _norm_modulate(xp, xs, shift3, scale3, norm_w)
    u, zs, qkv, kf32, vf32, gz = _in_proj(h, w_in, q_norm_w, k_norm_w)

    arow = jnp.transpose(a_re, (1, 0, 2)).reshape(G, 2 * P)
    airow = jnp.transpose(a_im, (1, 0, 2)).reshape(G, 2 * P)
    dtrow = jnp.broadcast_to(jnp.transpose(log_dt, (1, 0))[:, :, None], (G, 2, P)).reshape(G, 2 * P)
    rowp = jnp.stack([arow, airow, dtrow], axis=1)
    btr = jnp.transpose(b_re, (1, 3, 0, 2)).reshape(G, SSM_GROUP, 2 * P)
    bti = jnp.transpose(b_im, (1, 3, 0, 2)).reshape(G, SSM_GROUP, 2 * P)
    ctr = jnp.transpose(c_re, (1, 2, 0, 3)).reshape(G, SSM_GROUP, 2 * P)
    cti = jnp.transpose(c_im, (1, 2, 0, 3)).reshape(G, SSM_GROUP, 2 * P)

    dtile = d.reshape(1, SSM_WIDTH)
    h0c = jnp.zeros((G, BATCH, 2 * P), F32)
    h0d_re = jnp.transpose(st_re, (2, 0, 1, 3)).reshape(G, DEC_BATCH, 2 * P)
    h0d_im = jnp.transpose(st_im, (2, 0, 1, 3)).reshape(G, DEC_BATCH, 2 * P)
    y2d, fin = _ssm_scan(u.reshape(N_TOK // CHUNK * U_PITCH, SSM_WIDTH), rowp, btr, bti, ctr, cti, dtile,
                         h0c, h0c, h0d_re, h0d_im)
    y = y2d.reshape(2, CH_PER_SEG, N_SEG, CHUNK, SSM_WIDTH)

    a_ctx, new_k, new_v = _ctx_attention(qkv, gz, kf32, vf32)
    nseq4 = N_TOK // DEC_SEQ
    a_dec = _nbr_attention(qkv.reshape(nseq4, DEC_SEQ, qkv.shape[1]),
                           cache_k.reshape(DEC_BATCH, PAST_LEN, ATT_WIDTH),
                           cache_v.reshape(DEC_BATCH, PAST_LEN, ATT_WIDTH),
                           _nbr_bias_band(rpb.astype(F32)),
                           gz.reshape(nseq4, DEC_SEQ, gz.shape[1]))

    yp, ys_out = _back(xp, xs, y, zs, a_ctx, a_dec.reshape(N_DEC_TOK, ATT_WIDTH), gz, gate3, b_glu,
                       w_glu, w_so, w_ao, w_o)

    fin_ctx = fin[0]
    new_re = jnp.transpose(fin_ctx[:, :, :2 * P].reshape(G, BATCH, 2, P), (1, 2, 0, 3))
    new_im = jnp.transpose(fin_ctx[:, :, 2 * P:].reshape(G, BATCH, 2, P), (1, 2, 0, 3))
    return yp, ys_out, new_k, new_v, new_re, new_im


def kernel(x_prompt, x_sample, cache_k, cache_v, state_ssm_re, state_ssm_im, c, c_ctx, norm_w, w_ada, b_ada,
           w_in, q_norm_w, k_norm_w, rel_pos_bias, ssm_a_re, ssm_a_im, ssm_log_dt, ssm_b_re, ssm_b_im,
           ssm_c_re, ssm_c_im, ssm_d, w_glu, b_glu, w_ssm_out, w_att_out, w_o):
    depth = norm_w.shape[0]
    xp = x_prompt.reshape(N_CTX_TOK, D_MODEL)
    xs = x_sample.reshape(N_DEC_TOK, D_MODEL)
    new_k, new_v, new_re, new_im = [], [], [], []
    for l in range(depth):
        xp, xs, kl, vl, rl, il = _layer(
            xp, xs, cache_k[:, l], cache_v[:, l], state_ssm_re[:, l], state_ssm_im[:, l], c, c_ctx,
            norm_w[l], w_ada[l], b_ada[l], w_in[l], q_norm_w[l], k_norm_w[l], rel_pos_bias[l],
            ssm_a_re[l], ssm_a_im[l], ssm_log_dt[l], ssm_b_re[l], ssm_b_im[l], ssm_c_re[l], ssm_c_im[l],
            ssm_d[l], w_glu[l], b_glu[l], w_ssm_out[l], w_att_out[l], w_o[l])
        new_k.append(kl.reshape(BATCH, SEQ, N_HEADS, HEAD_DIM))
        new_v.append(vl.reshape(BATCH, SEQ, N_HEADS, HEAD_DIM))
        new_re.append(rl)
        new_im.append(il)
    return (xp.reshape(BATCH, SEQ, D_MODEL), xs.reshape(DEC_BATCH, DEC_SEQ, D_MODEL),
            jnp.stack(new_k, axis=1), jnp.stack(new_v, axis=1),
            jnp.stack(new_re, axis=1), jnp.stack(new_im, axis=1))
```

```python
import functools
import math

import jax
import jax.numpy as jnp
from jax import lax
from jax.experimental import pallas as pl
from jax.experimental.pallas import tpu as pltpu

D_MODEL = 2048
BATCH = 16
SEQ = 256
DEC_BATCH = 2
DEC_SEQ = 2048
PAST_LEN = 512
GRID_W = 64
GRID_H = DEC_SEQ // GRID_W
SSM_WIDTH = D_MODEL // 2
SSM_GROUP = 16
SSM_GROUPS = SSM_WIDTH // SSM_GROUP
SSM_STATE = 64
N_HEADS = 16
HEAD_DIM = 64
ATT_WIDTH = N_HEADS * HEAD_DIM
WIN_ROWS = 8
WIN_COLS = 16
EPS = 1e-6
NEG_INF = -1e30

N_CTX_TOK = BATCH * SEQ
N_DEC_TOK = DEC_BATCH * DEC_SEQ
N_TOK = N_CTX_TOK + N_DEC_TOK

OFF_U = 0
OFF_ZS = SSM_WIDTH
OFF_Q = 2 * SSM_WIDTH
OFF_K = OFF_Q + ATT_WIDTH
OFF_V = OFF_K + ATT_WIDTH
OFF_ZA = OFF_V + ATT_WIDTH
OFF_GS = OFF_ZA + ATT_WIDTH
OFF_GA = OFF_GS + D_MODEL

CHUNK = 16
FLAT = CHUNK * SSM_GROUP
N_SEG = 16
CH_PER_SEG = 16
U_PITCH = 24
DSTATE = 2 * SSM_STATE

Q_ROWS_PER_BLOCK = 4
NBR_Q = Q_ROWS_PER_BLOCK * GRID_W
NBR_KROWS = 12
NBR_K = NBR_KROWS * GRID_W

V7X_VMEM_BYTES = 64 * 1024 * 1024
VMEM_LIMIT = V7X_VMEM_BYTES - 8 * 1024 * 1024
QKV_VMEM_LIMIT = V7X_VMEM_BYTES - 4 * 1024 * 1024

LANES = 128
MXU_TILE = 256

F32 = jnp.float32
BF16 = jnp.bfloat16
HIGHEST = lax.Precision.HIGHEST


def _sigmoid(x):
    return 1.0 / (1.0 + jnp.exp(-x))


def _silu(x):
    return x * _sigmoid(x)


def _gelu_tanh(x):
    return 0.5 * x * (1.0 + jnp.tanh(math.sqrt(2.0 / math.pi) * (x + 0.044715 * (x * x * x))))


def _cmul(ar, ai, br, bi):
    return ar * br - ai * bi, ar * bi + ai * br


def _dot_nt(a, b):
    return lax.dot_general(a, b, (((1,), (1,)), ((), ())), preferred_element_type=F32)


def _params(*sem, vmem_limit=VMEM_LIMIT):
    return pltpu.CompilerParams(dimension_semantics=sem, vmem_limit_bytes=vmem_limit)


MOD_TN = 1536


def _mod_kernel(cond_ref, w_ref, b_ref, o_ref):
    c = cond_ref[...]
    s = _silu(c).astype(BF16)
    o_ref[...] = jnp.dot(s, w_ref[...].astype(BF16), preferred_element_type=F32) + b_ref[...]


def _modulation(cond8, w_ada, b_ada):
    tn = MOD_TN
    n = w_ada.shape[1]
    return pl.pallas_call(
        _mod_kernel,
        grid=(n // tn,),
        in_specs=[pl.BlockSpec((8, D_MODEL), lambda j: (0, 0)),
                  pl.BlockSpec((D_MODEL, tn), lambda j: (0, j)),
                  pl.BlockSpec((1, tn), lambda j: (0, j))],
        out_specs=pl.BlockSpec((8, tn), lambda j: (0, j)),
        out_shape=jax.ShapeDtypeStruct((8, n), F32),
        compiler_params=_params("arbitrary"),
        name="modulation",
    )(cond8, w_ada, b_ada.reshape(1, n))


NORM_TM = 1024
NORM_ROWS = 16
NORM_UNROLL = 8
NORM_CTX_STEPS = N_CTX_TOK // NORM_TM
NORM_STEPS_PER_DEC_SEQ = DEC_SEQ // NORM_TM


def _mod_row(i, ctx_steps, steps_per_seq):
    return jnp.where(i < ctx_steps, 0, 1 + (i - ctx_steps) // steps_per_seq)


def _norm_kernel(xp_ref, xs_ref, shift_ref, scale_ref, nw_ref, o_ref):
    i = pl.program_id(0)

    gain = nw_ref[...] * (1.0 + scale_ref[0])
    shift = shift_ref[0]

    def body(x_ref):
        def rows(r, carry):
            sl = pl.ds(pl.multiple_of(r * NORM_ROWS, NORM_ROWS), NORM_ROWS)
            x = x_ref[sl, :]
            ms = jnp.mean(x * x, axis=-1, keepdims=True)
            o_ref[sl, :] = (x * lax.rsqrt(ms + EPS) * gain + shift).astype(BF16)
            return carry

        lax.fori_loop(0, NORM_TM // NORM_ROWS, rows, 0, unroll=NORM_UNROLL)

    @pl.when(i < NORM_CTX_STEPS)
    def _():
        body(xp_ref)

    @pl.when(i >= NORM_CTX_STEPS)
    def _():
        body(xs_ref)


def _norm_modulate(xp, xs, shift3, scale3, norm_w):
    steps = N_TOK // NORM_TM
    row = functools.partial(_mod_row, ctx_steps=NORM_CTX_STEPS, steps_per_seq=NORM_STEPS_PER_DEC_SEQ)
    return pl.pallas_call(
        _norm_kernel,
        grid=(steps,),
        in_specs=[pl.BlockSpec((NORM_TM, D_MODEL), lambda i: (jnp.minimum(i, NORM_CTX_STEPS - 1), 0)),
                  pl.BlockSpec((NORM_TM, D_MODEL), lambda i: (jnp.maximum(i - NORM_CTX_STEPS, 0), 0)),
                  pl.BlockSpec((1, 1, D_MODEL), lambda i: (row(i), 0, 0)),
                  pl.BlockSpec((1, 1, D_MODEL), lambda i: (row(i), 0, 0)),
                  pl.BlockSpec((1, D_MODEL), lambda i: (0, 0))],
        out_specs=pl.BlockSpec((NORM_TM, D_MODEL), lambda i: (i, 0)),
        out_shape=jax.ShapeDtypeStruct((N_TOK, D_MODEL), BF16),
        compiler_params=_params("arbitrary"),
        name="norm_modulate",
    )(xp, xs, shift3, scale3, norm_w.reshape(1, D_MODEL))


PROJ_TM = 1024
PROJ_TN = 1024
PROJ_CTX_STEPS = N_CTX_TOK // PROJ_TM
PROJ_SEGS = PROJ_TM // (CHUNK * CH_PER_SEG)


def _head_group_ones():
    r = lax.broadcasted_iota(jnp.int32, (MXU_TILE, MXU_TILE), 0) // HEAD_DIM
    c = lax.broadcasted_iota(jnp.int32, (MXU_TILE, MXU_TILE), 1) // HEAD_DIM
    return jnp.where(r == c, 1.0, 0.0).astype(BF16)


def _head_rms(acc, nw):
    ones = _head_group_ones()
    outs = []
    for c in range(acc.shape[1] // MXU_TILE):
        a = acc[:, c * MXU_TILE:(c + 1) * MXU_TILE]
        ssum = jnp.dot((a * a).astype(BF16), ones, preferred_element_type=F32)
        outs.append(a * lax.rsqrt(ssum * (1.0 / HEAD_DIM) + EPS))
    return jnp.concatenate(outs, axis=1) * nw


def _cast_weight_tile(w_ref, wbf):
    @pl.when(pl.program_id(1) == 0)
    def _():
        wbf[...] = w_ref[...].astype(BF16)


def _proj_s5_kernel(h_ref, w_ref, u_ref, zs_ref, wbf):
    j = pl.program_id(0)
    _cast_weight_tile(w_ref, wbf)

    def column_blocks():
        for c in range(PROJ_TN // MXU_TILE):
            cols = slice(c * MXU_TILE, (c + 1) * MXU_TILE)
            yield cols, jnp.dot(h_ref[...], wbf[:, cols], preferred_element_type=F32)

    @pl.when(j == 0)
    def _():
        pad = jnp.zeros((U_PITCH - CHUNK, MXU_TILE), F32)
        for cols, acc in column_blocks():
            for seg in range(PROJ_SEGS):
                for ch in range(CH_PER_SEG):
                    r0 = (seg * CH_PER_SEG + ch) * CHUNK
                    u_ref[0, ch, seg, 0:CHUNK, cols] = acc[r0:r0 + CHUNK, :]
                    u_ref[0, ch, seg, CHUNK:U_PITCH, cols] = pad

    @pl.when(j == 1)
    def _():
        for cols, acc in column_blocks():
            zs_ref[:, cols] = acc.astype(BF16)


def _proj_qkv_kernel(h_ref, w_ref, nw_ref, o_ref, kf_ref, vf_ref, wbf):
    j = pl.program_id(0)
    is_ctx = pl.program_id(1) < PROJ_CTX_STEPS
    _cast_weight_tile(w_ref, wbf)

    def tile(normed, scale, f32_ref):
        width = 2 * MXU_TILE if normed else MXU_TILE
        for c in range(PROJ_TN // width):
            cols = slice(c * width, (c + 1) * width)
            a = jnp.dot(h_ref[...], wbf[:, cols], preferred_element_type=F32)
            if normed:
                a = _head_rms(a, nw_ref[0][:, cols])
            if scale is not None:
                a = a * scale
            o_ref[:, cols] = a.astype(BF16)
            if f32_ref is not None:
                f32_ref[:, cols] = a

    pl.when(j == 0)(lambda: tile(True, HEAD_DIM ** -0.5, None))
    pl.when((j == 1) & is_ctx)(lambda: tile(True, None, kf_ref))
    pl.when((j == 1) & jnp.logical_not(is_ctx))(lambda: tile(True, None, None))
    pl.when((j == 2) & is_ctx)(lambda: tile(False, None, vf_ref))
    pl.when((j == 2) & jnp.logical_not(is_ctx))(lambda: tile(False, None, None))


def _proj_gate_kernel(h_ref, w_ref, o_ref, wbf):
    _cast_weight_tile(w_ref, wbf)
    o_ref[...] = jnp.dot(h_ref[...], wbf[...], preferred_element_type=F32).astype(BF16)


def _in_proj(h, w_in, q_norm_w, k_norm_w):
    ni = N_TOK // PROJ_TM
    last = ni - 1
    ctx_last = PROJ_CTX_STEPS - 1
    tiles_per_path = N_SEG // PROJ_SEGS
    h_spec = pl.BlockSpec((PROJ_TM, D_MODEL), lambda j, i: (i, 0))
    scratch = [pltpu.VMEM((D_MODEL, PROJ_TN), BF16)]
    params = _params("arbitrary", "arbitrary")

    def u_map(j, i):
        ii = jnp.where(j == 0, i, last)
        return (ii // tiles_per_path, 0, ii % tiles_per_path, 0, 0)

    u, zs = pl.pallas_call(
        _proj_s5_kernel,
        grid=(2, ni),
        in_specs=[h_spec, pl.BlockSpec((D_MODEL, PROJ_TN), lambda j, i: (0, OFF_U // PROJ_TN + j))],
        out_specs=[pl.BlockSpec((1, CH_PER_SEG, PROJ_SEGS, U_PITCH, PROJ_TN), u_map),
                   pl.BlockSpec((PROJ_TM, PROJ_TN), lambda j, i: (jnp.where(j == 0, 0, i), 0))],
        out_shape=[jax.ShapeDtypeStruct((2, CH_PER_SEG, N_SEG, U_PITCH, SSM_WIDTH), F32),
                   jax.ShapeDtypeStruct((N_TOK, SSM_WIDTH), BF16)],
        scratch_shapes=scratch, compiler_params=params, name="in_proj_s5",
    )(h, w_in)

    nw = jnp.stack([jnp.tile(q_norm_w.reshape(1, HEAD_DIM), (1, PROJ_TN // HEAD_DIM)),
                    jnp.tile(k_norm_w.reshape(1, HEAD_DIM), (1, PROJ_TN // HEAD_DIM))])
    qkv, kf32, vf32 = pl.pallas_call(
        _proj_qkv_kernel,
        grid=(3, ni),
        in_specs=[h_spec, pl.BlockSpec((D_MODEL, PROJ_TN), lambda j, i: (0, OFF_Q // PROJ_TN + j)),
                  pl.BlockSpec((1, 1, PROJ_TN), lambda j, i: (jnp.where(j == 1, 1, 0), 0, 0))],
        out_specs=[pl.BlockSpec((PROJ_TM, PROJ_TN), lambda j, i: (i, j)),
                   pl.BlockSpec((PROJ_TM, PROJ_TN),
                                lambda j, i: (jnp.where(j == 0, 0, jnp.where(j == 1, jnp.minimum(i, ctx_last),
                                                                             ctx_last)), 0)),
                   pl.BlockSpec((PROJ_TM, PROJ_TN),
                                lambda j, i: (jnp.where(j == 2, jnp.minimum(i, ctx_last), 0), 0))],
        out_shape=[jax.ShapeDtypeStruct((N_TOK, 3 * ATT_WIDTH), BF16),
                   jax.ShapeDtypeStruct((N_CTX_TOK, ATT_WIDTH), F32),
                   jax.ShapeDtypeStruct((N_CTX_TOK, ATT_WIDTH), F32)],
        scratch_shapes=scratch, name="in_proj_qkv",
        compiler_params=_params("arbitrary", "arbitrary", vmem_limit=QKV_VMEM_LIMIT),
    )(h, w_in, nw)

    gate_tiles = 2 * D_MODEL // PROJ_TN
    gz = pl.pallas_call(
        _proj_gate_kernel,
        grid=(gate_tiles + 1, ni),
        in_specs=[h_spec,
                  pl.BlockSpec((D_MODEL, PROJ_TN),
                               lambda j, i: (0, jnp.where(j < gate_tiles, OFF_GS // PROJ_TN + j,
                                                          OFF_ZA // PROJ_TN)))],
        out_specs=pl.BlockSpec((PROJ_TM, PROJ_TN), lambda j, i: (i, j)),
        out_shape=jax.ShapeDtypeStruct((N_TOK, 2 * D_MODEL + ATT_WIDTH), BF16),
        scratch_shapes=scratch, compiler_params=params, name="in_proj_gates",
    )(h, w_in)
    return u, zs, qkv, kf32, vf32, gz


CACHE_TM = 512


CACHE_SLOTS = 3
CACHE_STEPS = N_CTX_TOK // CACHE_TM


def _cache_layout_kernel(k_hbm, v_hbm, ko_ref, vo_ref, kbuf, vbuf, sem):
    s = pl.program_id(0)

    def fetch(step):
        rows = pl.ds(pl.multiple_of(step * CACHE_TM, CACHE_TM), CACHE_TM)
        slot = step % CACHE_SLOTS
        return (pltpu.make_async_copy(k_hbm.at[rows, :], kbuf.at[slot], sem.at[0, slot]),
                pltpu.make_async_copy(v_hbm.at[rows, :], vbuf.at[slot], sem.at[1, slot]))

    @pl.when(s == 0)
    def _():
        for step in range(CACHE_SLOTS - 1):
            for copy in fetch(step):
                copy.start()

    @pl.when(s + CACHE_SLOTS - 1 < CACHE_STEPS)
    def _():
        for copy in fetch(s + CACHE_SLOTS - 1):
            copy.start()

    for copy in fetch(s):
        copy.wait()

    slot = s % CACHE_SLOTS
    for src, dst in ((kbuf, ko_ref), (vbuf, vo_ref)):
        for pair in range(ATT_WIDTH // LANES):
            x = src[slot, :, pair * LANES:(pair + 1) * LANES]
            swapped = pltpu.roll(x, HEAD_DIM, axis=1)
            dst[pl.ds(2 * pair, CACHE_TM, stride=N_HEADS), :] = x[:, :HEAD_DIM]
            dst[pl.ds(2 * pair + 1, CACHE_TM, stride=N_HEADS), :] = swapped[:, :HEAD_DIM]


def _cache_layout(kf32, vf32):
    assert LANES == 2 * HEAD_DIM and CACHE_STEPS >= CACHE_SLOTS - 1
    spec_in = pl.BlockSpec(memory_space=pl.ANY)
    spec_out = pl.BlockSpec((CACHE_TM * N_HEADS, HEAD_DIM), lambda i: (i, 0))
    shape = jax.ShapeDtypeStruct((N_CTX_TOK * N_HEADS, HEAD_DIM), F32)
    ring = pltpu.VMEM((CACHE_SLOTS, CACHE_TM, ATT_WIDTH), F32)
    return pl.pallas_call(
        _cache_layout_kernel,
        grid=(CACHE_STEPS,),
        in_specs=[spec_in, spec_in], out_specs=[spec_out, spec_out], out_shape=[shape, shape],
        scratch_shapes=[ring, ring, pltpu.SemaphoreType.DMA((2, CACHE_SLOTS))],
        compiler_params=_params("arbitrary"), name="cache_layout",
    )(kf32, vf32)


def _pow_select(e, pows):
    rr = jnp.where((e & 1) != 0, pows[0][0], 1.0)
    ri = jnp.where((e & 1) != 0, pows[0][1], 0.0)
    for b in range(1, len(pows)):
        bit = (e & (1 << b)) != 0
        fr = jnp.where(bit, pows[b][0], 1.0)
        fi = jnp.where(bit, pows[b][1], 0.0)
        rr, ri = _cmul(rr, ri, fr, fi)
    return rr, ri


def _discretize(a_re, a_im, log_dt):
    lr = jnp.minimum(a_re, -1e-4)
    li = a_im
    dt = jnp.exp(log_dt)
    mag = jnp.exp(lr * dt)
    br = mag * jnp.cos(li * dt)
    bi = mag * jnp.sin(li * dt)
    den = lr * lr + li * li
    nr = br - 1.0
    cr = (nr * lr + bi * li) / den
    ci = (bi * lr - nr * li) / den
    return (br, bi), (cr, ci)


def _squarings(pr, pi, n):
    out = [(pr, pi)]
    for _ in range(n):
        pr, pi = _cmul(pr, pi, pr, pi)
        out.append((pr, pi))
    return out


def _dot_nt_exact(a, b):
    return lax.dot_general(a, b, (((1,), (1,)), ((), ())), preferred_element_type=F32, precision=HIGHEST)


def _ssm_build_ops(n_groups, rowp_ref, btr_ref, bti_ref, ctr_ref, cti_ref, t_ref, s_ref, rt_ref, lam_ref, powc_ref):
    lane_b = lax.broadcasted_iota(jnp.int32, (SSM_GROUP, DSTATE), 1)
    lane_t = lax.broadcasted_iota(jnp.int32, (SSM_GROUP, FLAT), 1)
    row_c = lax.broadcasted_iota(jnp.int32, (CH_PER_SEG, DSTATE), 0)
    lane_c = lax.broadcasted_iota(jnp.int32, (CH_PER_SEG, DSTATE), 1)
    exp_c = jnp.where(lane_c < SSM_STATE, row_c, (CH_PER_SEG - 1) - row_c)

    for g in range(n_groups):
        rp = rowp_ref[g]
        (lbr, lbi), (cfr, cfi) = _discretize(rp[0:1], rp[1:2], rp[2:3])
        sq = _squarings(lbr, lbi, 8)
        bbr, bbi = _cmul(btr_ref[g], bti_ref[g], cfr, cfi)

        ptr, pti = _pow_select(exp_c, sq[:4])

        s_re, s_im = [], []
        for sp in range(CHUNK):
            e = CHUNK - 1 - sp
            br, bi = _cmul(bbr, bbi, ptr[e:e + 1], pti[e:e + 1])
            s_re.append(br)
            s_im.append(bi)
        s_ref[g] = jnp.concatenate([jnp.concatenate(s_re, axis=0), jnp.concatenate(s_im, axis=0)],
                                   axis=1).astype(BF16)

        xcr = jnp.concatenate([ctr_ref[g]] * CHUNK, axis=0)
        xci = jnp.concatenate([cti_ref[g]] * CHUNK, axis=0)
        xpr = jnp.broadcast_to(ptr[:, None, :], (CHUNK, SSM_GROUP, DSTATE)).reshape(FLAT, DSTATE)
        xpi = jnp.broadcast_to(pti[:, None, :], (CHUNK, SSM_GROUP, DSTATE)).reshape(FLAT, DSTATE)
        ykr, yki = _cmul(xcr, xci, xpr, xpi)
        yrr, yri = _cmul(ykr, yki, lbr, lbi)
        rt_ref[g] = jnp.concatenate([yrr, -yri], axis=1).astype(BF16)

        fwd = lane_b < SSM_STATE
        lhs = jnp.concatenate([jnp.where(fwd, bbr, 0.0), jnp.where(fwd, bbi, 0.0),
                               jnp.where(fwd, 0.0, bbr), jnp.where(fwd, 0.0, bbi)], axis=0)
        p1 = _dot_nt_exact(lhs, ykr)
        p2 = _dot_nt_exact(lhs, yki)
        kf = p1[0:16] - p2[16:32]
        kb = p1[32:48] - p2[48:64]
        for sp in range(CHUNK):
            tf = kf if sp == 0 else pltpu.roll(kf, SSM_GROUP * sp, axis=1)
            tf = jnp.where(lane_t >= SSM_GROUP * sp, tf, 0.0)
            shift = (FLAT - SSM_GROUP * (CHUNK - 1 - sp)) % FLAT
            tb = kb if shift == 0 else pltpu.roll(kb, shift, axis=1)
            tb = jnp.where(lane_t < SSM_GROUP * (sp + 1), tb, 0.0)
            t_ref[g, sp * SSM_GROUP:(sp + 1) * SSM_GROUP, :] = (tf + tb).astype(BF16)

        l16 = sq[4]
        l256 = sq[8]
        lam_ref[g] = jnp.concatenate([l16[0], l16[1], l256[0], l256[1],
                                      jnp.zeros((4, DSTATE), F32)], axis=0)
        q16 = _squarings(l16[0], l16[1], 3)
        pcr, pci = _pow_select(exp_c, q16)
        powc_ref[g, 0] = pcr
        powc_ref[g, 1] = pci


SSM_PATHS = ((BATCH, SEQ // (CHUNK * CH_PER_SEG)), (DEC_BATCH, DEC_SEQ // (CHUNK * CH_PER_SEG)))


def _ssm_path(u, s_op, t_op, rt_op, lam, powc_ref, h0r, h0i, nseg):
    lane = lax.broadcasted_iota(jnp.int32, (N_SEG, DSTATE), 1)
    isf = lane < SSM_STATE
    l16r, l16i, l256r, l256i = lam[0:1], lam[1:2], lam[2:3], lam[3:4]

    z = jnp.dot(u, s_op, preferred_element_type=F32)
    zre, zim = z[:, :DSTATE], z[:, DSTATE:]

    hr = jnp.zeros((N_SEG, DSTATE), F32)
    hi = jnp.zeros((N_SEG, DSTATE), F32)
    hist = []
    for k in range(CH_PER_SEG):
        hist.append((hr, hi))
        kb = CH_PER_SEG - 1 - k
        zr = jnp.where(isf, zre[k * N_SEG:(k + 1) * N_SEG], zre[kb * N_SEG:(kb + 1) * N_SEG])
        zi = jnp.where(isf, zim[k * N_SEG:(k + 1) * N_SEG], zim[kb * N_SEG:(kb + 1) * N_SEG])
        nr, ni = _cmul(l16r, l16i, hr, hi)
        hr, hi = nr + zr, ni + zi

    zero_start = h0r is None
    if nseg > 1 and not zero_start:
        h0r = jnp.broadcast_to(h0r[:, None, :], (N_SEG // nseg, nseg, DSTATE)).reshape(N_SEG, DSTATE)
        h0i = jnp.broadcast_to(h0i[:, None, :], (N_SEG // nseg, nseg, DSTATE)).reshape(N_SEG, DSTATE)
    seg = lax.broadcasted_iota(jnp.int32, (N_SEG, DSTATE), 0) & (nseg - 1)
    segpow = _squarings(l256r, l256i, max(int(math.log2(nseg)), 0))

    def shifted(x, d):
        dn = jnp.where(seg >= d, pltpu.roll(x, d, axis=0), 0.0)
        up = jnp.where(seg <= nseg - 1 - d, pltpu.roll(x, N_SEG - d, axis=0), 0.0)
        return jnp.where(isf, dn, up)

    pr, pi = hr, hi
    d = 1
    lvl = 0
    while d < nseg:
        ar, ai = _cmul(segpow[lvl][0], segpow[lvl][1], shifted(pr, d), shifted(pi, d))
        pr, pi = pr + ar, pi + ai
        d *= 2
        lvl += 1
    hsr = hsi = None
    if nseg > 1:
        hsr, hsi = shifted(pr, 1), shifted(pi, 1)
        if not zero_start:
            e_in = jnp.where(isf, seg, nseg - 1 - seg)
            wr, wi = _pow_select(e_in, segpow[:lvl])
            ar, ai = _cmul(wr, wi, h0r, h0i)
            hsr, hsi = hsr + ar, hsi + ai
    elif not zero_start:
        hsr, hsi = h0r, h0i
    if zero_start:
        fin = jnp.concatenate([pr, pi], axis=1)
    else:
        er, ei = _cmul(segpow[lvl][0], segpow[lvl][1], h0r, h0i)
        fin = jnp.concatenate([pr + er, pi + ei], axis=1)

    rows = []
    for c in range(CH_PER_SEG):
        cb = CH_PER_SEG - 1 - c
        lr = jnp.where(isf, hist[c][0], hist[cb][0])
        li = jnp.where(isf, hist[c][1], hist[cb][1])
        if hsr is not None:
            ar, ai = _cmul(powc_ref[0, c:c + 1, :], powc_ref[1, c:c + 1, :], hsr, hsi)
            lr, li = lr + ar, li + ai
        rows.append(jnp.concatenate([lr, li], axis=1))
    hent = jnp.concatenate(rows, axis=0).astype(BF16)

    y = jnp.dot(u, t_op, preferred_element_type=F32) + _dot_nt(hent, rt_op)
    return y, fin


GROUPS_PER_STEP = LANES // SSM_GROUP


def _block_transpose(arrs, blk):
    n = len(arrs)
    width = arrs[0].shape[1]
    j = lax.broadcasted_iota(jnp.int32, arrs[0].shape, 1) // blk
    k = n // 2
    while k >= 1:
        bit = (j & k) != 0
        new = list(arrs)
        for x in range(n):
            if x & k == 0:
                a, b = arrs[x], arrs[x | k]
                new[x] = jnp.where(bit, pltpu.roll(b, k * blk, axis=1), a)
                new[x | k] = jnp.where(bit, b, pltpu.roll(a, width - k * blk, axis=1))
        arrs = new
        k //= 2
    return arrs


def _ssm_kernel(u_ref, rowp_ref, btr_ref, bti_ref, ctr_ref, cti_ref, d_ref, h0dr_ref, h0di_ref,
                y_ref, fin_ref, ubuf, ybuf, t_ref, s_ref, r_ref, lam_ref, powc_ref):
    _ssm_build_ops(GROUPS_PER_STEP, rowp_ref, btr_ref, bti_ref, ctr_ref, cti_ref,
                   t_ref, s_ref, r_ref, lam_ref, powc_ref)
    h0 = (None, (h0dr_ref, h0di_ref))
    nhalf = FLAT // LANES
    rows = N_SEG * CH_PER_SEG
    def token_rows(path, tok, pitch):
        return pl.ds(path * rows * pitch + tok, rows, stride=pitch)

    for path, (_, nseg) in enumerate(SSM_PATHS):
        for half in range(nhalf):
            toks = [pltpu.bitcast(u_ref[token_rows(path, half * GROUPS_PER_STEP + sb, U_PITCH), :].astype(BF16),
                                  jnp.uint32) for sb in range(GROUPS_PER_STEP)]
            grouped = _block_transpose(toks, SSM_GROUP)
            for g in range(GROUPS_PER_STEP):
                ubuf[g, :, half * LANES:(half + 1) * LANES] = pltpu.bitcast(grouped[g], BF16)

        for g in range(GROUPS_PER_STEP):
            h0r, h0i = (None, None) if h0[path] is None else (h0[path][0][g], h0[path][1][g])
            y, fin = _ssm_path(ubuf[g], s_ref[g], t_ref[g], r_ref[g], lam_ref[g], powc_ref.at[g], h0r, h0i, nseg)
            ybuf[g] = y
            fin_ref[path, g] = fin

        for half in range(nhalf):
            grouped = [ybuf[g, :, half * LANES:(half + 1) * LANES] for g in range(GROUPS_PER_STEP)]
            toks = _block_transpose(grouped, SSM_GROUP)
            for sb in range(GROUPS_PER_STEP):
                tok = half * GROUPS_PER_STEP + sb
                y_ref[token_rows(path, tok, CHUNK), :] = (toks[sb]
                                                         + u_ref[token_rows(path, tok, U_PITCH), :] * d_ref[...])


def _ssm_scan(u2d, rowp, btr, bti, ctr, cti, dtile, h0d_re, h0d_im):
    G = SSM_GROUPS
    gs = GROUPS_PER_STEP
    rows = N_SEG * CH_PER_SEG
    return pl.pallas_call(
        _ssm_kernel,
        grid=(G // gs,),
        in_specs=[
            pl.BlockSpec((N_TOK // CHUNK * U_PITCH, LANES), lambda o: (0, o)),
            pl.BlockSpec((gs, 3, DSTATE), lambda o: (o, 0, 0)),
            pl.BlockSpec((gs, SSM_GROUP, DSTATE), lambda o: (o, 0, 0)),
            pl.BlockSpec((gs, SSM_GROUP, DSTATE), lambda o: (o, 0, 0)),
            pl.BlockSpec((gs, SSM_GROUP, DSTATE), lambda o: (o, 0, 0)),
            pl.BlockSpec((gs, SSM_GROUP, DSTATE), lambda o: (o, 0, 0)),
            pl.BlockSpec((1, LANES), lambda o: (0, o)),
            pl.BlockSpec((gs, DEC_BATCH, DSTATE), lambda o: (o, 0, 0)),
            pl.BlockSpec((gs, DEC_BATCH, DSTATE), lambda o: (o, 0, 0))],
        out_specs=[pl.BlockSpec((N_TOK, LANES), lambda o: (0, o)),
                   pl.BlockSpec((2, gs, N_SEG, 2 * DSTATE), lambda o: (0, o, 0, 0))],
        out_shape=[jax.ShapeDtypeStruct((N_TOK, SSM_WIDTH), F32),
                   jax.ShapeDtypeStruct((2, G, N_SEG, 2 * DSTATE), F32)],
        scratch_shapes=[pltpu.VMEM((gs, rows, FLAT), BF16), pltpu.VMEM((gs, rows, FLAT), F32),
                        pltpu.VMEM((gs, FLAT, FLAT), BF16), pltpu.VMEM((gs, FLAT, FLAT), BF16),
                        pltpu.VMEM((gs, FLAT, FLAT), BF16), pltpu.VMEM((gs, 8, DSTATE), F32),
                        pltpu.VMEM((gs, 2, CH_PER_SEG, DSTATE), F32)],
        compiler_params=_params("arbitrary"),
        name="ssm_scan",
    )(u2d, rowp, btr, bti, ctr, cti, dtile, h0d_re, h0d_im)


CTX_BB = 4
HEADS_PER_STEP = MXU_TILE // HEAD_DIM
HEAD_LANES = HEADS_PER_STEP * HEAD_DIM


def _softmax_pv(scores, values):
    m = scores[0].max(axis=-1, keepdims=True)
    for s in scores[1:]:
        m = jnp.maximum(m, s.max(axis=-1, keepdims=True))
    l = None
    o = None
    for s, v in zip(scores, values):
        p = jnp.exp(s - m)
        ls = p.sum(axis=-1, keepdims=True)
        os_ = jnp.dot(p.astype(BF16), v, preferred_element_type=F32)
        l = ls if l is None else l + ls
        o = os_ if o is None else o + os_
    return o / l


def _head_of_lane():
    return lax.broadcasted_iota(jnp.int32, (1, HEAD_LANES), 1) // HEAD_DIM


def _merge_heads(outs, za):
    head = _head_of_lane()
    o = outs[-1]
    for hh in range(HEADS_PER_STEP - 2, -1, -1):
        o = jnp.where(head == hh, outs[hh], o)
    return (o * _silu(za.astype(F32))).astype(BF16)


def _ctx_attn_kernel(q_ref, k_ref, v_ref, za_ref, o_ref):
    head = _head_of_lane()
    for b in range(CTX_BB):
        sl = slice(b * SEQ, (b + 1) * SEQ)
        q, k, v = q_ref[sl, :], k_ref[sl, :], v_ref[sl, :]
        outs = []
        for hh in range(HEADS_PER_STEP):
            qh = jnp.where(head == hh, q, jnp.zeros_like(q))
            outs.append(_softmax_pv([_dot_nt(qh, k)], [v]))
        o_ref[sl, :] = _merge_heads(outs, za_ref[sl, :])


HEAD_BLOCKS = ATT_WIDTH // HEAD_LANES
QKV_Q, QKV_K, QKV_V = 0, HEAD_BLOCKS, 2 * HEAD_BLOCKS
GZ_ZA = 2 * D_MODEL // HEAD_LANES


def _ctx_attention(qkv, gz):
    rows = CTX_BB * SEQ

    def spec(first):
        return pl.BlockSpec((rows, HEAD_LANES), lambda b, hq: (b, first + hq))

    return pl.pallas_call(
        _ctx_attn_kernel,
        grid=(BATCH // CTX_BB, N_HEADS // HEADS_PER_STEP),
        in_specs=[spec(QKV_Q), spec(QKV_K), spec(QKV_V), spec(GZ_ZA)],
        out_specs=spec(0),
        out_shape=jax.ShapeDtypeStruct((N_CTX_TOK, ATT_WIDTH), BF16),
        compiler_params=_params("arbitrary", "arbitrary"),
        name="ctx_attention",
    )(qkv, qkv, qkv, gz)


N_DROW = 2 * WIN_ROWS - 1


def _nbr_build_table(band_ref, tbl):
    lane = lax.broadcasted_iota(jnp.int32, (GRID_W, 2 * GRID_W), 1)
    qc = lax.broadcasted_iota(jnp.int32, (GRID_W, 2 * GRID_W), 0)
    kc = lane & (GRID_W - 1)
    cs = jnp.clip(qc - WIN_COLS // 2, 0, GRID_W - WIN_COLS)
    valid = jnp.logical_and(kc >= cs, kc < cs + WIN_COLS)
    for hh in range(HEADS_PER_STEP):
        for dr in range(N_DROW):
            x = jnp.broadcast_to(band_ref[hh, dr:dr + 1, :], (GRID_W, 2 * GRID_W))
            lo = pltpu.roll(x, 0, axis=1, stride=1, stride_axis=0)
            hi = pltpu.roll(x, GRID_W, axis=1, stride=1, stride_axis=0)
            tbl[hh, dr] = jnp.where(valid, jnp.where(lane < GRID_W, lo, hi), NEG_INF)
        tbl[hh, N_DROW] = jnp.full((GRID_W, 2 * GRID_W), NEG_INF, F32)


NBR_KEYS = NBR_K + PAST_LEN
KEY_TILE = MXU_TILE
SOFTMAX_ROWS = 64


def _nbr_block_index(a):
    ks = jnp.clip(Q_ROWS_PER_BLOCK * a - Q_ROWS_PER_BLOCK, 0, GRID_H - NBR_KROWS)
    idx = []
    for ri in range(Q_ROWS_PER_BLOCK):
        r = Q_ROWS_PER_BLOCK * a + ri
        rs = jnp.clip(r - WIN_ROWS // 2, 0, GRID_H - WIN_ROWS)
        row = []
        for kri in range(NBR_KROWS):
            kr = ks + kri
            valid = jnp.logical_and(kr >= rs, kr < rs + WIN_ROWS)
            row.append(jnp.where(valid, kr - r + WIN_ROWS - 1, N_DROW))
        idx.append(row)
    return idx


def _nbr_bias_tile(tbl, hh, idx, t):
    half0 = lax.broadcasted_iota(jnp.int32, (1, 2 * GRID_W), 1) < GRID_W
    krows = KEY_TILE // GRID_W
    rows = []
    for ri in range(Q_ROWS_PER_BLOCK):
        tiles = [jnp.where(half0, tbl[hh, idx[ri][kri]], tbl[hh, idx[ri][kri + 1]])
                 for kri in range(krows * t, krows * (t + 1), 2)]
        rows.append(jnp.concatenate(tiles, axis=1))
    return jnp.concatenate(rows, axis=0)


def _nbr_attn_kernel(q_ref, k_ref, v_ref, kc_ref, vc_ref, band_ref, za_ref, o_ref,
                     tbl, kcat, vcat, s_scr, p_scr, o_scr):
    a = pl.program_id(1)

    @pl.when(a == 0)
    def _():
        _nbr_build_table(band_ref, tbl)

    start = pl.multiple_of(jnp.clip(a * NBR_Q - NBR_Q, 0, DEC_SEQ - NBR_K), NBR_Q)
    head = _head_of_lane()
    idx = _nbr_block_index(a)
    for b in range(DEC_BATCH):
        kcat[b, 0:NBR_K, :] = k_ref[b, pl.ds(start, NBR_K), :]
        kcat[b, NBR_K:NBR_KEYS, :] = kc_ref[b].astype(BF16)
        vcat[b, 0:NBR_K, :] = v_ref[b, pl.ds(start, NBR_K), :]
        vcat[b, NBR_K:NBR_KEYS, :] = vc_ref[b].astype(BF16)

    def scores(b, hh):
        q = q_ref[b]
        qh = jnp.where(head == hh, q, jnp.zeros_like(q))
        mrun = None
        for t in range(NBR_KEYS // KEY_TILE):
            st = _dot_nt(qh, kcat[b, t * KEY_TILE:(t + 1) * KEY_TILE, :])
            if t < NBR_K // KEY_TILE:
                st = st + _nbr_bias_tile(tbl, hh, idx, t)
            s_scr[hh, :, t * KEY_TILE:(t + 1) * KEY_TILE] = st
            mt = jnp.maximum(st[:, :LANES], st[:, LANES:])
            mrun = mt if mrun is None else jnp.maximum(mrun, mt)
        return mrun.max(axis=-1, keepdims=True)

    def attend(b, hh, m):
        ls = []
        for r0 in range(0, NBR_Q, SOFTMAX_ROWS):
            mr = m[r0:r0 + SOFTMAX_ROWS]
            lrun = jnp.zeros((SOFTMAX_ROWS, LANES), F32)
            for c0 in range(0, NBR_KEYS, LANES):
                p = jnp.exp(s_scr[hh, r0:r0 + SOFTMAX_ROWS, c0:c0 + LANES] - mr)
                lrun = lrun + p
                p_scr[hh, r0:r0 + SOFTMAX_ROWS, c0:c0 + LANES] = p.astype(BF16)
            ls.append(lrun.sum(axis=-1, keepdims=True))
        l = jnp.concatenate(ls, axis=0)
        o = jnp.dot(p_scr[hh], vcat[b], preferred_element_type=F32)
        lanes = slice(hh * HEAD_DIM, (hh + 1) * HEAD_DIM)
        o_scr[b, :, lanes] = o[:, lanes] / l

    order = [(b, hh) for b in range(DEC_BATCH) for hh in range(HEADS_PER_STEP)]
    m_cur = scores(*order[0])
    for n, (b, hh) in enumerate(order):
        m_next = scores(*order[n + 1]) if n + 1 < len(order) else None
        attend(b, hh, m_cur)
        m_cur = m_next
    for b in range(DEC_BATCH):
        o_ref[b] = (o_scr[b] * _silu(za_ref[b].astype(F32))).astype(BF16)


def _nbr_attention(qkv4, kc, vc, band, gz4):
    first = N_CTX_TOK // DEC_SEQ // DEC_BATCH
    nblk = GRID_H // Q_ROWS_PER_BLOCK
    hw = HEAD_LANES

    def qspec(col0):
        return pl.BlockSpec((DEC_BATCH, NBR_Q, hw), lambda hp, a: (first, a, col0 + hp))

    def kspec(col0):
        return pl.BlockSpec((DEC_BATCH, DEC_SEQ, hw), lambda hp, a: (first, 0, col0 + hp))

    cspec = pl.BlockSpec((DEC_BATCH, PAST_LEN, hw), lambda hp, a: (0, 0, hp))
    bspec = pl.BlockSpec((HEADS_PER_STEP, N_DROW + 1, 2 * GRID_W), lambda hp, a: (hp, 0, 0))
    return pl.pallas_call(
        _nbr_attn_kernel,
        grid=(N_HEADS // HEADS_PER_STEP, nblk),
        in_specs=[qspec(QKV_Q), kspec(QKV_K), kspec(QKV_V), cspec, cspec, bspec, qspec(GZ_ZA)],
        out_specs=pl.BlockSpec((DEC_BATCH, NBR_Q, hw), lambda hp, a: (0, a, hp)),
        out_shape=jax.ShapeDtypeStruct((DEC_BATCH, DEC_SEQ, ATT_WIDTH), BF16),
        scratch_shapes=[pltpu.VMEM((HEADS_PER_STEP, N_DROW + 1, GRID_W, 2 * GRID_W), F32),
                        pltpu.VMEM((DEC_BATCH, NBR_KEYS, HEAD_LANES), BF16),
                        pltpu.VMEM((DEC_BATCH, NBR_KEYS, HEAD_LANES), BF16),
                        pltpu.VMEM((HEADS_PER_STEP, NBR_Q, NBR_KEYS), F32),
                        pltpu.VMEM((HEADS_PER_STEP, NBR_Q, NBR_KEYS), BF16),
                        pltpu.VMEM((DEC_BATCH, NBR_Q, HEAD_LANES), F32)],
        compiler_params=_params("arbitrary", "arbitrary"),
        name="nbr_attention",
    )(qkv4, qkv4, qkv4, kc, vc, band, gz4)


def _nbr_bias_band(rpb):
    ncol = 2 * WIN_COLS - 1
    fill = jnp.full(rpb.shape[:2] + (2 * GRID_W - ncol,), NEG_INF, F32)
    band = jnp.concatenate([rpb[..., WIN_COLS - 1:], fill, rpb[..., :WIN_COLS - 1]], axis=-1)
    return jnp.pad(band, ((0, 0), (0, 1), (0, 0)))


BACK_TM = CHUNK * CH_PER_SEG
BACK_CTX_STEPS = N_CTX_TOK // BACK_TM
BACK_STEPS_PER_DEC_SEQ = DEC_SEQ // BACK_TM


BACK_WARM = 8


def _back_kernel(xp_ref, xs_ref, y_ref, zs_ref, ac_ref, ad_ref, gs_ref, ga_ref, gate_ref, bglu_ref,
                 wglu_ref, wso_ref, wao_ref, wo_ref, op_ref, os_ref, wglu_bf, wso_bf, wao_bf, wo_bf):
    i = pl.program_id(0)
    step = i - BACK_WARM

    @pl.when(i < BACK_WARM)
    def _():
        for src, dst in ((wglu_ref, wglu_bf), (wso_ref, wso_bf), (wao_ref, wao_bf), (wo_ref, wo_bf)):
            slab = src.shape[0]
            dst[pl.ds(pl.multiple_of(i * slab, slab), slab), :] = src[...].astype(BF16)

    def compute(x, a2):
        ys = _gelu_tanh(y_ref[0].reshape(BACK_TM, SSM_WIDTH))
        t = jnp.dot(ys.astype(BF16), wglu_bf[...], preferred_element_type=F32) + bglu_ref[...]
        ys = ys * _sigmoid(t) * _silu(zs_ref[...].astype(F32))
        p_s = jnp.dot(ys.astype(BF16), wso_bf[...], preferred_element_type=F32)
        p_a = jnp.dot(a2, wao_bf[...], preferred_element_type=F32)
        merged = (_sigmoid(gs_ref[...].astype(F32)) * p_s + _sigmoid(ga_ref[...].astype(F32)) * p_a)
        return x + gate_ref[0] * jnp.dot(merged.astype(BF16), wo_bf[...], preferred_element_type=F32)

    @pl.when(jnp.logical_and(step >= 0, step < BACK_CTX_STEPS))
    def _():
        op_ref[...] = compute(xp_ref[...], ac_ref[...])

    @pl.when(step >= BACK_CTX_STEPS)
    def _():
        os_ref[...] = compute(xs_ref[...], ad_ref[...])


def _back(xp, xs, y, zs, a_ctx, a_dec, gz, gate3, b_glu, w_glu, w_so, w_ao, w_o):
    steps = N_TOK // BACK_TM
    n0 = BACK_CTX_STEPS
    warm = BACK_WARM
    row = functools.partial(_mod_row, ctx_steps=n0, steps_per_seq=BACK_STEPS_PER_DEC_SEQ)

    def tile(i):
        return jnp.maximum(i - warm, 0)

    lo = lambda i: (jnp.minimum(tile(i), n0 - 1), 0)
    hi = lambda i: (jnp.maximum(tile(i) - n0, 0), 0)
    cur = lambda i: (tile(i), 0)
    const = lambda i: (0, 0)
    slab = lambda i: (jnp.minimum(i, warm - 1), 0)

    def weight(w):
        return pl.BlockSpec((w.shape[0] // warm, w.shape[1]), slab)

    return pl.pallas_call(
        _back_kernel,
        grid=(warm + steps,),
        in_specs=[pl.BlockSpec((BACK_TM, D_MODEL), lo),
                  pl.BlockSpec((BACK_TM, D_MODEL), hi),
                  pl.BlockSpec((1, CH_PER_SEG, None, CHUNK, SSM_WIDTH),
                               lambda i: (tile(i) // N_SEG, 0, tile(i) % N_SEG, 0, 0)),
                  pl.BlockSpec((BACK_TM, SSM_WIDTH), cur),
                  pl.BlockSpec((BACK_TM, ATT_WIDTH), lo),
                  pl.BlockSpec((BACK_TM, ATT_WIDTH), hi),
                  pl.BlockSpec((BACK_TM, D_MODEL), lambda i: (tile(i), 0)),
                  pl.BlockSpec((BACK_TM, D_MODEL), lambda i: (tile(i), 1)),
                  pl.BlockSpec((1, 1, D_MODEL), lambda i: (row(tile(i)), 0, 0)),
                  pl.BlockSpec((1, SSM_WIDTH), const),
                  weight(w_glu), weight(w_so), weight(w_ao), weight(w_o)],
        out_specs=[pl.BlockSpec((BACK_TM, D_MODEL), lo),
                   pl.BlockSpec((BACK_TM, D_MODEL), hi)],
        out_shape=[jax.ShapeDtypeStruct((N_CTX_TOK, D_MODEL), F32),
                   jax.ShapeDtypeStruct((N_DEC_TOK, D_MODEL), F32)],
        scratch_shapes=[pltpu.VMEM(w.shape, BF16) for w in (w_glu, w_so, w_ao, w_o)],
        compiler_params=_params("arbitrary"),
        name="gated_output",
    )(xp, xs, y, zs, a_ctx, a_dec, gz, gz, gate3, b_glu.reshape(1, SSM_WIDTH), w_glu, w_so, w_ao, w_o)


def _layer(xp, xs, cache_k, cache_v, st_re, st_im, c, c_ctx, norm_w, w_ada, b_ada, w_in, q_norm_w, k_norm_w,
           rpb, a_re, a_im, log_dt, b_re, b_im, c_re, c_im, d, w_glu, b_glu, w_so, w_ao, w_o):
    G, P = SSM_GROUPS, SSM_STATE

    cond8 = jnp.zeros((8, D_MODEL), F32).at[0].set(c_ctx).at[1:1 + DEC_BATCH].set(c)
    mod = _modulation(cond8, w_ada, b_ada)
    shift3 = mod[:1 + DEC_BATCH, None, :D_MODEL]
    scale3 = mod[:1 + DEC_BATCH, None, D_MODEL:2 * D_MODEL]
    gate3 = mod[:1 + DEC_BATCH, None, 2 * D_MODEL:]

    h = _norm_modulate(xp, xs, shift3, scale3, norm_w)
    u, zs, qkv, kf32, vf32, gz = _in_proj(h, w_in, q_norm_w, k_norm_w)

    arow = jnp.transpose(a_re, (1, 0, 2)).reshape(G, 2 * P)
    airow = jnp.transpose(a_im, (1, 0, 2)).reshape(G, 2 * P)
    dtrow = jnp.broadcast_to(jnp.transpose(log_dt, (1, 0))[:, :, None], (G, 2, P)).reshape(G, 2 * P)
    rowp = jnp.stack([arow, airow, dtrow], axis=1)
    btr = jnp.transpose(b_re, (1, 3, 0, 2)).reshape(G, SSM_GROUP, 2 * P)
    bti = jnp.transpose(b_im, (1, 3, 0, 2)).reshape(G, SSM_GROUP, 2 * P)
    ctr = jnp.transpose(c_re, (1, 2, 0, 3)).reshape(G, SSM_GROUP, 2 * P)
    cti = jnp.transpose(c_im, (1, 2, 0, 3)).reshape(G, SSM_GROUP, 2 * P)

    dtile = d.reshape(1, SSM_WIDTH)
    h0d_re = jnp.transpose(st_re, (2, 0, 1, 3)).reshape(G, DEC_BATCH, 2 * P)
    h0d_im = jnp.transpose(st_im, (2, 0, 1, 3)).reshape(G, DEC_BATCH, 2 * P)
    y2d, fin = _ssm_scan(u.reshape(N_TOK // CHUNK * U_PITCH, SSM_WIDTH), rowp, btr, bti, ctr, cti, dtile,
                         h0d_re, h0d_im)
    y = y2d.reshape(2, CH_PER_SEG, N_SEG, CHUNK, SSM_WIDTH)

    a_ctx = _ctx_attention(qkv, gz)
    nseq4 = N_TOK // DEC_SEQ
    a_dec = _nbr_attention(qkv.reshape(nseq4, DEC_SEQ, qkv.shape[1]),
                           cache_k.reshape(DEC_BATCH, PAST_LEN, ATT_WIDTH),
                           cache_v.reshape(DEC_BATCH, PAST_LEN, ATT_WIDTH),
                           _nbr_bias_band(rpb.astype(F32)),
                           gz.reshape(nseq4, DEC_SEQ, gz.shape[1]))

    yp, ys_out = _back(xp, xs, y, zs, a_ctx, a_dec.reshape(N_DEC_TOK, ATT_WIDTH), gz, gate3, b_glu,
                       w_glu, w_so, w_ao, w_o)

    fin_ctx = fin[0]
    new_re = jnp.transpose(fin_ctx[:, :, :2 * P].reshape(G, BATCH, 2, P), (1, 2, 0, 3))
    new_im = jnp.transpose(fin_ctx[:, :, 2 * P:].reshape(G, BATCH, 2, P), (1, 2, 0, 3))
    new_k, new_v = _cache_layout(kf32, vf32)
    return yp, ys_out, new_k, new_v, new_re, new_im


def kernel(x_prompt, x_sample, cache_k, cache_v, state_ssm_re, state_ssm_im, c, c_ctx, norm_w, w_ada, b_ada,
           w_in, q_norm_w, k_norm_w, rel_pos_bias, ssm_a_re, ssm_a_im, ssm_log_dt, ssm_b_re, ssm_b_im,
           ssm_c_re, ssm_c_im, ssm_d, w_glu, b_glu, w_ssm_out, w_att_out, w_o):
    depth = norm_w.shape[0]
    xp = x_prompt.reshape(N_CTX_TOK, D_MODEL)
    xs = x_sample.reshape(N_DEC_TOK, D_MODEL)
    new_k, new_v, new_re, new_im = [], [], [], []
    for l in range(depth):
        xp, xs, kl, vl, rl, il = _layer(
            xp, xs, cache_k[:, l], cache_v[:, l], state_ssm_re[:, l], state_ssm_im[:, l], c, c_ctx,
            norm_w[l], w_ada[l], b_ada[l], w_in[l], q_norm_w[l], k_norm_w[l], rel_pos_bias[l],
            ssm_a_re[l], ssm_a_im[l], ssm_log_dt[l], ssm_b_re[l], ssm_b_im[l], ssm_c_re[l], ssm_c_im[l],
            ssm_d[l], w_glu[l], b_glu[l], w_ssm_out[l], w_att_out[l], w_o[l])
        new_k.append(kl.reshape(BATCH, SEQ, N_HEADS, HEAD_DIM))
        new_v.append(vl.reshape(BATCH, SEQ, N_HEADS, HEAD_DIM))
        new_re.append(rl)
        new_im.append(il)
    return (xp.reshape(BATCH, SEQ, D_MODEL), xs.reshape(DEC_BATCH, DEC_SEQ, D_MODEL),
            jnp.stack(new_k, axis=1), jnp.stack(new_v, axis=1),
            jnp.stack(new_re, axis=1), jnp.stack(new_im, axis=1))
```
